```python
import jax, jax.numpy as jnp
from jax import lax
import numpy as np

D_MODEL = 2048
BATCH = 4
SEQ = 2048
DEPTH = 1
DEC_BATCH = 128
DEC_SEQ = 8
PAST_LEN = 16384
PAGE_SIZE = 128

POOL_WINDOWS = (2, 4, 8, 16)
N_POOL_GROUPS = 4
POOL_GROUP_DIM = D_MODEL // 16
D_POOL = N_POOL_GROUPS * POOL_GROUP_DIM
POOL_STATE_LEN = max(POOL_WINDOWS) - 1
D_CONV = D_MODEL // 2
CONV_WIDTH = 3
N_MEM = 256
N_XHEADS = 4
XHEAD_DIM = D_MODEL // 16
D_XATTN = N_XHEADS * XHEAD_DIM
N_BRANCH = 3
SPLIT_POINTS = (D_POOL, D_POOL + D_CONV, D_POOL + 2 * D_CONV, D_POOL + 3 * D_CONV, D_POOL + 3 * D_CONV + D_XATTN)
D_IN_TOTAL = D_POOL + 3 * D_CONV + D_XATTN + N_BRANCH * D_MODEL
N_EXPERTS = 32
TOP_K = 4
D_FF = D_MODEL
SWIGLU_LIMIT = 7.0
SWIGLU_ALPHA = 1.702
ROW_BLOCK = 128
EPS = 1e-5

kernel_name = 'hybrid_pool_conv_memxattn_moe_step'


def rmsnorm(x, g):
    xf = x.astype(jnp.float32)
    y = xf * lax.rsqrt(jnp.mean(xf * xf, axis=-1, keepdims=True) + EPS)
    return (y * g.astype(jnp.float32)).astype(x.dtype)


def pool_mix(u, prev, start, w_group, scale):
    n, L, _ = u.shape
    ext = jnp.concatenate([prev.astype(u.dtype), u], axis=1)
    c = jnp.cumsum(ext.astype(jnp.float32), axis=1)
    c = jnp.pad(c, ((0, 0), (1, 0), (0, 0)))
    pos = start + jnp.arange(L, dtype=jnp.int32)
    p0 = POOL_STATE_LEN + 1
    means = []
    for g, w in enumerate(POOL_WINDOWS):
        sl = slice(g * POOL_GROUP_DIM, (g + 1) * POOL_GROUP_DIM)
        s = c[:, p0:p0 + L, sl] - c[:, p0 - w:p0 - w + L, sl]
        cnt = jnp.minimum(w, pos + 1).astype(jnp.float32)
        means.append(s / cnt[None, :, None])
    pooled = jnp.concatenate(means, axis=-1) - u.astype(jnp.float32)
    pg = pooled.reshape(n, L, N_POOL_GROUPS, POOL_GROUP_DIM).astype(u.dtype)
    out = jnp.einsum('nlgc,gcd->nlgd', pg, w_group).reshape(n, L, D_POOL) * scale
    return out, ext[:, -POOL_STATE_LEN:]


def conv_mix(v, b, c, prev, w_conv):
    L = v.shape[1]
    z = c * v
    ext = jnp.concatenate([prev.astype(z.dtype), z], axis=1)
    y = ext[:, 0:L] * w_conv[0]
    for k in range(1, CONV_WIDTH):
        y = y + ext[:, k:k + L] * w_conv[k]
    return b * y, ext[:, -(CONV_WIDTH - 1):]


def mem_attention(q, mem_k, mem_v):
    n, L = q.shape[0], q.shape[1]
    s = jnp.einsum('nlhd,nmhd->nhlm', q, mem_k).astype(jnp.float32) * (XHEAD_DIM ** -0.5)
    p = jax.nn.softmax(s, axis=-1).astype(mem_v.dtype)
    o = jnp.einsum('nhlm,nmhd->nlhd', p, mem_v)
    return o.reshape(n, L, D_XATTN)


def mix_layer(x, pool_prev, conv_prev, mem_k, mem_v, start, norm_mix, w_in, b_gate,
              w_pool_group, pool_scale, w_conv, w_pool_out, w_conv_out, w_attn_out, w_o):
    n, L, _ = x.shape
    h = rmsnorm(x, norm_mix)
    proj = h @ w_in
    u, v, gb, gc, q, gl = jnp.split(proj, SPLIT_POINTS, axis=-1)
    gates = jax.nn.sigmoid((gl + b_gate).astype(jnp.float32)).astype(x.dtype)
    gates = gates.reshape(n, L, N_BRANCH, D_MODEL)
    p_out, p_new = pool_mix(u, pool_prev, start, w_pool_group, pool_scale)
    c_out, c_new = conv_mix(v, gb, gc, conv_prev, w_conv)
    a_out = mem_attention(q.reshape(n, L, N_XHEADS, XHEAD_DIM), mem_k, mem_v)
    merged = (gates[:, :, 0] * (p_out @ w_pool_out)
              + gates[:, :, 1] * (c_out @ w_conv_out)
              + gates[:, :, 2] * (a_out @ w_attn_out))
    return x + merged @ w_o, p_new, c_new


def moe_ffn(x, norm_ffn, w_router, b_router, w_exp_in, b_exp_in, w_exp_out, b_exp_out):
    n, L, D = x.shape
    h = rmsnorm(x, norm_ffn).reshape(-1, D)
    T = h.shape[0]
    A = T * TOP_K
    logits = (h @ w_router).astype(jnp.float32) + b_router.astype(jnp.float32)
    top_val, top_idx = lax.top_k(logits, TOP_K)
    wts = jax.nn.softmax(top_val, axis=-1).reshape(-1)
    flat_e = top_idx.reshape(-1)
    order = jnp.argsort(flat_e)
    es = flat_e[order]
    tok = order // TOP_K
    sizes = jnp.bincount(flat_e, length=N_EXPERTS)
    padded = (sizes + ROW_BLOCK - 1) // ROW_BLOCK * ROW_BLOCK
    pad_end = jnp.cumsum(padded)
    pad_start = pad_end - padded
    grp_start = jnp.cumsum(sizes) - sizes
    dest = pad_start[es] + jnp.arange(A, dtype=es.dtype) - grp_start[es]
    n_blocks = A // ROW_BLOCK + 1 + N_EXPERTS
    xs = jnp.zeros((n_blocks * ROW_BLOCK, D), h.dtype).at[dest].set(h[tok])
    blk_e = jnp.minimum(jnp.searchsorted(pad_end, jnp.arange(n_blocks) * ROW_BLOCK, side='right'), N_EXPERTS - 1)

    def expert_block(args):
        xb, e = args
        gu = xb @ w_exp_in[e] + b_exp_in[e]
        g, up = jnp.split(gu, 2, axis=-1)
        g = jnp.minimum(g, SWIGLU_LIMIT)
        up = jnp.clip(up, -SWIGLU_LIMIT, SWIGLU_LIMIT)
        a = g * jax.nn.sigmoid(SWIGLU_ALPHA * g) * (up + 1)
        return a @ w_exp_out[e] + b_exp_out[e]

    ys = lax.map(expert_block, (xs.reshape(n_blocks, ROW_BLOCK, D), blk_e)).reshape(-1, D)
    contrib = ys[dest].astype(jnp.float32) * wts[order][:, None]
    out = jnp.zeros((T, D), jnp.float32).at[tok].add(contrib)
    return x + out.astype(x.dtype).reshape(n, L, D)


def setup_inputs(seed: int = 0) -> dict:
    key = jax.random.key(seed)
    ks = jax.random.split(key, 32)
    f32 = jnp.float32

    def nrm(k, shape, scale):
        return jax.random.normal(k, shape, f32) * scale

    def gain(k, shape):
        return 1.0 + 0.02 * jax.random.normal(k, shape, f32)

    return {
        'x_prompt': nrm(ks[0], (BATCH, SEQ, D_MODEL), 1.0),
        'x_sample': nrm(ks[1], (DEC_BATCH, DEC_SEQ, D_MODEL), 1.0),
        'state_pool': nrm(ks[2], (DEPTH, DEC_BATCH, POOL_STATE_LEN, D_POOL), 1.0),
        'state_conv': nrm(ks[3], (DEPTH, DEC_BATCH, CONV_WIDTH - 1, D_CONV), 1.0),
        'cache_mem_k': nrm(ks[4], (DEPTH, DEC_BATCH, N_MEM, N_XHEADS, XHEAD_DIM), 1.0),
        'cache_mem_v': nrm(ks[5], (DEPTH, DEC_BATCH, N_MEM, N_XHEADS, XHEAD_DIM), 1.0),
        'mem_prompt': nrm(ks[6], (BATCH, N_MEM, D_MODEL), 1.0),
        'norm_mix': gain(ks[7], (DEPTH, D_MODEL)),
        'w_in': nrm(ks[8], (DEPTH, D_MODEL, D_IN_TOTAL), D_MODEL ** -0.5),
        'b_gate': nrm(ks[9], (DEPTH, N_BRANCH * D_MODEL), 0.02),
        'w_pool_group': nrm(ks[10], (DEPTH, N_POOL_GROUPS, POOL_GROUP_DIM, POOL_GROUP_DIM), POOL_GROUP_DIM ** -0.5),
        'pool_scale': gain(ks[11], (DEPTH, D_POOL)),
        'w_conv': nrm(ks[12], (DEPTH, CONV_WIDTH, D_CONV), CONV_WIDTH ** -0.5),
        'mem_norm': gain(ks[13], (DEPTH, D_MODEL)),
        'w_mem_kv': nrm(ks[14], (DEPTH, D_MODEL, 2 * D_XATTN), D_MODEL ** -0.5),
        'w_pool_out': nrm(ks[15], (DEPTH, D_POOL, D_MODEL), D_POOL ** -0.5),
        'w_conv_out': nrm(ks[16], (DEPTH, D_CONV, D_MODEL), D_CONV ** -0.5),
        'w_attn_out': nrm(ks[17], (DEPTH, D_XATTN, D_MODEL), D_XATTN ** -0.5),
        'w_o': nrm(ks[18], (DEPTH, D_MODEL, D_MODEL), D_MODEL ** -0.5),
        'norm_ffn': gain(ks[19], (DEPTH, D_MODEL)),
        'w_router': nrm(ks[20], (DEPTH, D_MODEL, N_EXPERTS), D_MODEL ** -0.5),
        'b_router': nrm(ks[21], (DEPTH, N_EXPERTS), 0.01),
        'w_exp_in': nrm(ks[22], (DEPTH, N_EXPERTS, D_MODEL, 2 * D_FF), D_MODEL ** -0.5),
        'b_exp_in': nrm(ks[23], (DEPTH, N_EXPERTS, 2 * D_FF), 0.02),
        'w_exp_out': nrm(ks[24], (DEPTH, N_EXPERTS, D_FF, D_MODEL), D_FF ** -0.5),
        'b_exp_out': nrm(ks[25], (DEPTH, N_EXPERTS, D_MODEL), 0.02),
        'final_norm': gain(ks[26], (D_MODEL,)),
    }


def reference(x_prompt, x_sample, state_pool, state_conv, cache_mem_k, cache_mem_v, mem_prompt,
              norm_mix, w_in, b_gate, w_pool_group, pool_scale, w_conv, mem_norm, w_mem_kv,
              w_pool_out, w_conv_out, w_attn_out, w_o, norm_ffn, w_router, b_router,
              w_exp_in, b_exp_in, w_exp_out, b_exp_out, final_norm):
    xp, xs = x_prompt, x_sample
    bp = xp.shape[0]
    pool_p, conv_p, mk_p, mv_p, pool_s, conv_s = [], [], [], [], [], []
    for l in range(DEPTH):
        mix_w = (norm_mix[l], w_in[l], b_gate[l], w_pool_group[l], pool_scale[l], w_conv[l],
                 w_pool_out[l], w_conv_out[l], w_attn_out[l], w_o[l])
        moe_w = (norm_ffn[l], w_router[l], b_router[l], w_exp_in[l], b_exp_in[l], w_exp_out[l], b_exp_out[l])
        kv = rmsnorm(mem_prompt, mem_norm[l]) @ w_mem_kv[l]
        mk, mv = jnp.split(kv, 2, axis=-1)
        mk = mk.reshape(bp, N_MEM, N_XHEADS, XHEAD_DIM)
        mv = mv.reshape(bp, N_MEM, N_XHEADS, XHEAD_DIM)
        zp = jnp.zeros((bp, POOL_STATE_LEN, D_POOL), xp.dtype)
        zc = jnp.zeros((bp, CONV_WIDTH - 1, D_CONV), xp.dtype)
        xp, pp, cp = mix_layer(xp, zp, zc, mk, mv, 0, *mix_w)
        xs, ps, cs = mix_layer(xs, state_pool[l], state_conv[l], cache_mem_k[l], cache_mem_v[l], PAST_LEN, *mix_w)
        xp = moe_ffn(xp, *moe_w)
        xs = moe_ffn(xs, *moe_w)
        pool_p.append(pp)
        conv_p.append(cp)
        mk_p.append(mk)
        mv_p.append(mv)
        pool_s.append(ps)
        conv_s.append(cs)
    y_prompt = rmsnorm(xp, final_norm)
    y_sample = rmsnorm(xs, final_norm)
    return (y_prompt, y_sample, jnp.stack(pool_p), jnp.stack(conv_p), jnp.stack(mk_p), jnp.stack(mv_p), jnp.stack(pool_s), jnp.stack(conv_s))
```

```python
import functools

import jax
import jax.numpy as jnp
from jax import lax
from jax.experimental import pallas as pl
from jax.experimental.pallas import tpu as pltpu

F32 = jnp.float32
BF16 = jnp.bfloat16

D_MODEL = 2048
POOL_WINDOWS = (2, 4, 8, 16)
POOL_GROUP_DIM = 128
D_POOL = 512
POOL_STATE_LEN = 15
D_CONV = 1024
CONV_WIDTH = 3
N_MEM = 256
N_XHEADS = 4
XHEAD_DIM = 128
D_XATTN = 512
N_BRANCH = 3
D_MIX = D_POOL + 3 * D_CONV + D_XATTN
D_IN_TOTAL = D_MIX + N_BRANCH * D_MODEL
N_EXPERTS = 32
TOP_K = 4
D_FF = D_MODEL
SWIGLU_LIMIT = 7.0
SWIGLU_ALPHA = 1.702
EPS = 1e-5

C_U = 0
C_V = D_POOL
C_B = D_POOL + D_CONV
C_C = D_POOL + 2 * D_CONV
C_Q = D_POOL + 3 * D_CONV

LANES = 128
HIST = 16

PROJ_TM = 1024
PROJ_TN = 512
MIX_TM = 256
MIX_NS = 8
MERGE_TM = 256
EXP_TM = 256
EXP_IN_TN = 512
EXP_OUT_TN = 1024
COMB_TM = 256
VMEM_LIMIT = 56 * 1024 * 1024


def _sigmoid(x):
    return 1.0 / (1.0 + jnp.exp(-x))


def _rms(x, g):
    ms = jnp.mean(x * x, axis=-1, keepdims=True)
    return x * lax.rsqrt(ms + EPS) * g


def _norm_matmul_kernel(x_ref, g_ref, w_ref, b_ref, o_ref, h_ref, *, act_from):
    j = pl.program_id(1)

    @pl.when(j == 0)
    def _():
        h_ref[...] = _rms(x_ref[...], g_ref[...]).astype(BF16)

    acc = jnp.dot(h_ref[...], w_ref[...].astype(BF16), preferred_element_type=F32) + b_ref[...]

    @pl.when(j < act_from)
    def _():
        o_ref[...] = acc

    @pl.when(j >= act_from)
    def _():
        o_ref[...] = _sigmoid(acc)


def _norm_matmul(x, gain, w, bias, act_from_col, name):
    t, d = x.shape
    n = w.shape[1]
    tm = min(PROJ_TM, t)
    tn = PROJ_TN
    assert t % tm == 0 and n % tn == 0 and act_from_col % tn == 0
    return pl.pallas_call(
        functools.partial(_norm_matmul_kernel, act_from=act_from_col // tn),
        grid=(t // tm, n // tn),
        in_specs=[
            pl.BlockSpec((tm, d), lambda i, j: (i, 0)),
            pl.BlockSpec((1, d), lambda i, j: (0, 0)),
            pl.BlockSpec((d, tn), lambda i, j: (0, j)),
            pl.BlockSpec((1, tn), lambda i, j: (0, j)),
        ],
        out_specs=pl.BlockSpec((tm, tn), lambda i, j: (i, j)),
        out_shape=jax.ShapeDtypeStruct((t, n), F32),
        scratch_shapes=[pltpu.VMEM((tm, d), BF16)],
        compiler_params=pltpu.CompilerParams(
            dimension_semantics=("arbitrary", "arbitrary"), vmem_limit_bytes=VMEM_LIMIT),
        name=name,
    )(x, gain.reshape(1, d), w, bias.reshape(1, n))


def _pool_project(pooled, wpg_ref, scale_ref, g):
    sl = slice(g * POOL_GROUP_DIM, (g + 1) * POOL_GROUP_DIM)
    y = jnp.dot(pooled.astype(BF16), wpg_ref[g].astype(BF16), preferred_element_type=F32)
    return y * scale_ref[:, sl]


def _softmax_rows(s):
    m = jnp.max(s, axis=-1, keepdims=True)
    e = jnp.exp(s - m)
    return e / jnp.sum(e, axis=-1, keepdims=True)


def _mix_prompt_kernel(cur_ref, prev_ref, mk_ref, mv_ref, wpg_ref, scale_ref, wconv_ref,
                       br_ref, zst_ref, extu_ref, extz_ref):
    t = pl.program_id(1)
    tm = cur_ref.shape[0]
    has_prev = t > 0

    u = cur_ref[:, C_U:C_U + D_POOL]
    extu_ref[0:HIST, :] = jnp.where(has_prev, prev_ref[:, C_U:C_U + D_POOL], 0.0)
    extu_ref[HIST:HIST + tm, :] = u
    pos = t * tm + lax.broadcasted_iota(jnp.int32, (tm, 1), 0)
    for g, w in enumerate(POOL_WINDOWS):
        sl = slice(g * POOL_GROUP_DIM, (g + 1) * POOL_GROUP_DIM)
        s = extu_ref[HIST:HIST + tm, sl]
        for k in range(1, w):
            s = s + extu_ref[HIST - k:HIST - k + tm, sl]
        cnt = jnp.minimum(w, pos + 1).astype(F32)
        pooled = s / cnt - extu_ref[HIST:HIST + tm, sl]
        br_ref[:, sl] = _pool_project(pooled, wpg_ref, scale_ref, g).astype(BF16)

    z = cur_ref[:, C_C:C_C + D_CONV] * cur_ref[:, C_V:C_V + D_CONV]
    zprev = prev_ref[:, C_C:C_C + D_CONV] * prev_ref[:, C_V:C_V + D_CONV]
    extz_ref[0:HIST, :] = jnp.where(has_prev, zprev, 0.0)
    extz_ref[HIST:HIST + tm, :] = z
    y = extz_ref[HIST - 2:HIST - 2 + tm, :] * wconv_ref[0:1, :]
    y = y + extz_ref[HIST - 1:HIST - 1 + tm, :] * wconv_ref[1:2, :]
    y = y + extz_ref[HIST:HIST + tm, :] * wconv_ref[2:3, :]
    br_ref[:, D_POOL:D_POOL + D_CONV] = (cur_ref[:, C_B:C_B + D_CONV] * y).astype(BF16)
    zst_ref[0] = extz_ref[HIST + tm - 8:HIST + tm, :]

    for h in range(N_XHEADS):
        sl = slice(h * XHEAD_DIM, (h + 1) * XHEAD_DIM)
        qh = cur_ref[:, C_Q + h * XHEAD_DIM:C_Q + (h + 1) * XHEAD_DIM].astype(BF16)
        kh = mk_ref[0, :, sl].astype(BF16)
        vh = mv_ref[0, :, sl].astype(BF16)
        s = lax.dot_general(qh, kh, (((1,), (1,)), ((), ())), preferred_element_type=F32)
        p = _softmax_rows(s * (XHEAD_DIM ** -0.5))
        o = jnp.dot(p.astype(BF16), vh, preferred_element_type=F32)
        c0 = D_POOL + D_CONV + h * XHEAD_DIM
        br_ref[:, c0:c0 + XHEAD_DIM] = o.astype(BF16)


def _mix_prompt(proj, mk, mv, wpg, scale, wconv, batch, seq):
    tm = MIX_TM
    nt = seq // tm
    assert seq % tm == 0 and tm % HIST == 0
    rpb = tm // HIST
    return pl.pallas_call(
        _mix_prompt_kernel,
        grid=(batch, nt),
        in_specs=[
            pl.BlockSpec((tm, D_MIX), lambda b, t: (b * nt + t, 0)),
            pl.BlockSpec((HIST, D_MIX), lambda b, t: (jnp.maximum((b * nt + t) * rpb - 1, 0), 0)),
            pl.BlockSpec((1, N_MEM, D_XATTN), lambda b, t: (b, 0, 0)),
            pl.BlockSpec((1, N_MEM, D_XATTN), lambda b, t: (b, 0, 0)),
            pl.BlockSpec((len(POOL_WINDOWS), POOL_GROUP_DIM, POOL_GROUP_DIM), lambda b, t: (0, 0, 0)),
            pl.BlockSpec((1, D_POOL), lambda b, t: (0, 0)),
            pl.BlockSpec((CONV_WIDTH, D_CONV), lambda b, t: (0, 0)),
        ],
        out_specs=[
            pl.BlockSpec((tm, D_MODEL), lambda b, t: (b * nt + t, 0)),
            pl.BlockSpec((1, 8, D_CONV), lambda b, t: (b, 0, 0)),
        ],
        out_shape=[
            jax.ShapeDtypeStruct((batch * seq, D_MODEL), BF16),
            jax.ShapeDtypeStruct((batch, 8, D_CONV), F32),
        ],
        scratch_shapes=[pltpu.VMEM((HIST + tm, D_POOL), F32), pltpu.VMEM((HIST + tm, D_CONV), F32)],
        compiler_params=pltpu.CompilerParams(
            dimension_semantics=("arbitrary", "arbitrary"), vmem_limit_bytes=VMEM_LIMIT),
        name="mix_prompt",
    )(proj, proj, mk, mv, wpg, scale.reshape(1, D_POOL), wconv)


def _mix_sample_kernel(cur_ref, sp_ref, sc_ref, k_ref, v_ref, wpg_ref, scale_ref, wconv_ref,
                       br_ref, zst_ref, extu_ref, extz_ref):
    ns, ln = zst_ref.shape[0], zst_ref.shape[1]
    rows = ns * ln

    extu_ref[:, HIST - POOL_STATE_LEN:HIST, :] = sp_ref[...]
    extu_ref[:, HIST:HIST + ln, :] = cur_ref[:, C_U:C_U + D_POOL].reshape(ns, ln, D_POOL)
    for g, w in enumerate(POOL_WINDOWS):
        sl = slice(g * POOL_GROUP_DIM, (g + 1) * POOL_GROUP_DIM)
        s = extu_ref[:, HIST:HIST + ln, sl]
        for k in range(1, w):
            s = s + extu_ref[:, HIST - k:HIST - k + ln, sl]
        pooled = s / float(w) - extu_ref[:, HIST:HIST + ln, sl]
        pooled = pooled.reshape(rows, POOL_GROUP_DIM)
        br_ref[:, sl] = _pool_project(pooled, wpg_ref, scale_ref, g).astype(BF16)

    z = cur_ref[:, C_C:C_C + D_CONV] * cur_ref[:, C_V:C_V + D_CONV]
    extz_ref[:, HIST - 2:HIST, :] = sc_ref[...]
    extz_ref[:, HIST:HIST + ln, :] = z.reshape(ns, ln, D_CONV)
    y = extz_ref[:, HIST - 2:HIST - 2 + ln, :] * wconv_ref[0:1, :]
    y = y + extz_ref[:, HIST - 1:HIST - 1 + ln, :] * wconv_ref[1:2, :]
    y = y + extz_ref[:, HIST:HIST + ln, :] * wconv_ref[2:3, :]
    br_ref[:, D_POOL:D_POOL + D_CONV] = (
        cur_ref[:, C_B:C_B + D_CONV] * y.reshape(rows, D_CONV)).astype(BF16)
    zst_ref[...] = extz_ref[:, HIST:HIST + ln, :]

    q3 = cur_ref[:, C_Q:C_Q + D_XATTN].reshape(ns, ln, D_XATTN)
    row_head = lax.broadcasted_iota(jnp.int32, (N_XHEADS * ln, D_XATTN), 0) // ln
    col_head = lax.broadcasted_iota(jnp.int32, (N_XHEADS * ln, D_XATTN), 1) // XHEAD_DIM
    qbd = jnp.where((row_head == col_head)[None], jnp.concatenate([q3] * N_XHEADS, axis=1), 0.0)
    s = jnp.einsum("nqd,nkd->nqk", qbd.astype(BF16), k_ref[...].astype(BF16),
                   preferred_element_type=F32)
    p = _softmax_rows(s * (XHEAD_DIM ** -0.5))
    r = jnp.einsum("nqk,nkd->nqd", p.astype(BF16), v_ref[...].astype(BF16),
                   preferred_element_type=F32)
    head_of_col = lax.broadcasted_iota(jnp.int32, (ln, D_XATTN), 1) // XHEAD_DIM
    o = jnp.zeros((ns, ln, D_XATTN), F32)
    for h in range(N_XHEADS):
        o = o + jnp.where((head_of_col == h)[None], r[:, h * ln:(h + 1) * ln, :], 0.0)
    br_ref[:, D_POOL + D_CONV:D_MODEL] = o.reshape(rows, D_XATTN).astype(BF16)


def _mix_sample(proj, state_pool, state_conv, mem_k, mem_v, wpg, scale, wconv, nseq, ln):
    ns = MIX_NS
    rows = ns * ln
    assert nseq % ns == 0 and ln == 8
    return pl.pallas_call(
        _mix_sample_kernel,
        grid=(nseq // ns,),
        in_specs=[
            pl.BlockSpec((rows, D_MIX), lambda s: (s, 0)),
            pl.BlockSpec((ns, POOL_STATE_LEN, D_POOL), lambda s: (s, 0, 0)),
            pl.BlockSpec((ns, CONV_WIDTH - 1, D_CONV), lambda s: (s, 0, 0)),
            pl.BlockSpec((ns, N_MEM, D_XATTN), lambda s: (s, 0, 0)),
            pl.BlockSpec((ns, N_MEM, D_XATTN), lambda s: (s, 0, 0)),
            pl.BlockSpec((len(POOL_WINDOWS), POOL_GROUP_DIM, POOL_GROUP_DIM), lambda s: (0, 0, 0)),
            pl.BlockSpec((1, D_POOL), lambda s: (0, 0)),
            pl.BlockSpec((CONV_WIDTH, D_CONV), lambda s: (0, 0)),
        ],
        out_specs=[
            pl.BlockSpec((rows, D_MODEL), lambda s: (s, 0)),
            pl.BlockSpec((ns, ln, D_CONV), lambda s: (s, 0, 0)),
        ],
        out_shape=[
            jax.ShapeDtypeStruct((nseq * ln, D_MODEL), BF16),
            jax.ShapeDtypeStruct((nseq, ln, D_CONV), F32),
        ],
        scratch_shapes=[pltpu.VMEM((ns, HIST + ln, D_POOL), F32),
                        pltpu.VMEM((ns, HIST + ln, D_CONV), F32)],
        compiler_params=pltpu.CompilerParams(
            dimension_semantics=("arbitrary",), vmem_limit_bytes=VMEM_LIMIT),
        name="mix_sample",
    )(proj, state_pool, state_conv, mem_k, mem_v, wpg, scale.reshape(1, D_POOL), wconv)


def _merge_route_kernel(x_ref, br_ref, g0_ref, g1_ref, g2_ref, wpo_ref, wco_ref, wao_ref, wo_ref,
                        nf_ref, wr_ref, brt_ref,
                        x1_ref, h2_ref, ri_ref, rw_ref, cnt_ref, carry_ref):
    i = pl.program_id(0)
    tm = x_ref.shape[0]

    @pl.when(i == 0)
    def _():
        carry_ref[...] = jnp.zeros_like(carry_ref)

    merged = g0_ref[...] * jnp.dot(br_ref[:, 0:D_POOL], wpo_ref[...], preferred_element_type=F32)
    merged = merged + g1_ref[...] * jnp.dot(br_ref[:, D_POOL:D_POOL + D_CONV], wco_ref[...],
                                            preferred_element_type=F32)
    merged = merged + g2_ref[...] * jnp.dot(br_ref[:, D_POOL + D_CONV:D_MODEL], wao_ref[...],
                                            preferred_element_type=F32)
    x1 = x_ref[...] + jnp.dot(merged.astype(BF16), wo_ref[...], preferred_element_type=F32)
    x1_ref[...] = x1
    h2 = _rms(x1, nf_ref[...])
    h2_ref[...] = h2.astype(BF16)

    logits = jnp.dot(h2, wr_ref[...], preferred_element_type=F32,
                     precision=lax.Precision.HIGHEST) + brt_ref[...]
    lane = lax.broadcasted_iota(jnp.int32, (tm, N_EXPERTS), 1).astype(F32)
    vals, idxs, hots = [], [], []
    work = logits
    for _ in range(TOP_K):
        m = jnp.max(work, axis=-1, keepdims=True)
        idx = jnp.min(jnp.where(work == m, lane, float(N_EXPERTS)), axis=-1, keepdims=True)
        hot = lane == idx
        work = jnp.where(hot, -jnp.inf, work)
        vals.append(m)
        idxs.append(idx)
        hots.append(hot)
    es = [jnp.exp(v - vals[0]) for v in vals]
    denom = es[0] + es[1] + es[2] + es[3]

    chosen = jnp.where(hots[0] | hots[1] | hots[2] | hots[3], 1.0, 0.0).astype(BF16)
    r_i = lax.broadcasted_iota(jnp.int32, (tm, tm), 0)
    c_i = lax.broadcasted_iota(jnp.int32, (tm, tm), 1)
    lower = jnp.where(c_i < r_i, 1.0, 0.0).astype(BF16)
    before = jnp.dot(lower, chosen, preferred_element_type=F32) + carry_ref[0:1, 0:N_EXPERTS]
    carry_ref[0:1, 0:N_EXPERTS] = (carry_ref[0:1, 0:N_EXPERTS]
                                   + jnp.sum(chosen.astype(F32), axis=0, keepdims=True))
    cnt_ref[...] = carry_ref[...]

    out_lane = lax.broadcasted_iota(jnp.int32, (tm, LANES), 1)
    ri = jnp.zeros((tm, LANES), jnp.int32)
    rw = jnp.zeros((tm, LANES), F32)
    for k in range(TOP_K):
        rank = jnp.sum(jnp.where(hots[k], before, 0.0), axis=-1, keepdims=True).astype(jnp.int32)
        ri = jnp.where(out_lane == k, idxs[k].astype(jnp.int32), ri)
        ri = jnp.where(out_lane == TOP_K + k, rank, ri)
        rw = jnp.where(out_lane == k, es[k] / denom, rw)
    ri_ref[...] = ri
    rw_ref[...] = rw


def _merge_route(x, branch, proj, wpo, wco, wao, wo, norm_ffn, w_router, b_router, name):
    t = x.shape[0]
    tm = MERGE_TM
    assert t % tm == 0
    gate_blk0 = D_MIX // D_MODEL
    assert D_MIX % D_MODEL == 0
    const = lambda i: (0, 0)
    return pl.pallas_call(
        _merge_route_kernel,
        grid=(t // tm,),
        in_specs=[
            pl.BlockSpec((tm, D_MODEL), lambda i: (i, 0)),
            pl.BlockSpec((tm, D_MODEL), lambda i: (i, 0)),
            pl.BlockSpec((tm, D_MODEL), lambda i: (i, gate_blk0)),
            pl.BlockSpec((tm, D_MODEL), lambda i: (i, gate_blk0 + 1)),
            pl.BlockSpec((tm, D_MODEL), lambda i: (i, gate_blk0 + 2)),
            pl.BlockSpec((D_POOL, D_MODEL), const, pipeline_mode=pl.Buffered(1)),
            pl.BlockSpec((D_CONV, D_MODEL), const, pipeline_mode=pl.Buffered(1)),
            pl.BlockSpec((D_XATTN, D_MODEL), const, pipeline_mode=pl.Buffered(1)),
            pl.BlockSpec((D_MODEL, D_MODEL), const, pipeline_mode=pl.Buffered(1)),
            pl.BlockSpec((1, D_MODEL), const),
            pl.BlockSpec((D_MODEL, N_EXPERTS), const),
            pl.BlockSpec((1, N_EXPERTS), const),
        ],
        out_specs=[
            pl.BlockSpec((tm, D_MODEL), lambda i: (i, 0)),
            pl.BlockSpec((tm, D_MODEL), lambda i: (i, 0)),
            pl.BlockSpec((tm, LANES), lambda i: (i, 0)),
            pl.BlockSpec((tm, LANES), lambda i: (i, 0)),
            pl.BlockSpec((8, LANES), const),
        ],
        out_shape=[
            jax.ShapeDtypeStruct((t, D_MODEL), F32),
            jax.ShapeDtypeStruct((t, D_MODEL), BF16),
            jax.ShapeDtypeStruct((t, LANES), jnp.int32),
            jax.ShapeDtypeStruct((t, LANES), F32),
            jax.ShapeDtypeStruct((8, LANES), F32),
        ],
        scratch_shapes=[pltpu.VMEM((8, LANES), F32)],
        compiler_params=pltpu.CompilerParams(
            dimension_semantics=("arbitrary",), vmem_limit_bytes=VMEM_LIMIT),
        name=name,
    )(x, branch, proj, proj, proj, wpo, wco, wao, wo, norm_ffn.reshape(1, D_MODEL),
      w_router, b_router.reshape(1, N_EXPERTS))


def _expert_changed(i, be_ref):
    return (i == 0) | (be_ref[i] != be_ref[jnp.maximum(i - 1, 0)])


def _expert_in_kernel(be_ref, na_ref, xs_ref, wg_ref, wu_ref, bg_ref, bu_ref, a_ref, wgb_ref, wub_ref):
    i = pl.program_id(1)

    @pl.when(_expert_changed(i, be_ref))
    def _():
        wgb_ref[...] = wg_ref[0].astype(BF16)
        wub_ref[...] = wu_ref[0].astype(BF16)

    @pl.when(i < na_ref[0])
    def _():
        x = xs_ref[...]
        g = jnp.dot(x, wgb_ref[...], preferred_element_type=F32) + bg_ref[0]
        up = jnp.dot(x, wub_ref[...], preferred_element_type=F32) + bu_ref[0]
        g = jnp.minimum(g, SWIGLU_LIMIT)
        up = jnp.clip(up, -SWIGLU_LIMIT, SWIGLU_LIMIT)
        a_ref[...] = (g * _sigmoid(SWIGLU_ALPHA * g) * (up + 1.0)).astype(BF16)


def _expert_in(blk_e, n_active, xs, w_in, b_in):
    rows = xs.shape[0]
    tm, tn = EXP_TM, EXP_IN_TN
    nb = rows // tm
    nj = D_FF // tn

    def row(i, na):
        return jnp.minimum(i, na[0] - 1)

    grid_spec = pltpu.PrefetchScalarGridSpec(
        num_scalar_prefetch=2,
        grid=(nj, nb),
        in_specs=[
            pl.BlockSpec((tm, D_MODEL), lambda j, i, be, na: (row(i, na), 0)),
            pl.BlockSpec((1, D_MODEL, tn), lambda j, i, be, na: (be[i], 0, j)),
            pl.BlockSpec((1, D_MODEL, tn), lambda j, i, be, na: (be[i], 0, nj + j)),
            pl.BlockSpec((1, 1, tn), lambda j, i, be, na: (be[i], 0, j)),
            pl.BlockSpec((1, 1, tn), lambda j, i, be, na: (be[i], 0, nj + j)),
        ],
        out_specs=pl.BlockSpec((tm, tn), lambda j, i, be, na: (row(i, na), j)),
        scratch_shapes=[pltpu.VMEM((D_MODEL, tn), BF16), pltpu.VMEM((D_MODEL, tn), BF16)],
    )
    return pl.pallas_call(
        _expert_in_kernel,
        grid_spec=grid_spec,
        out_shape=jax.ShapeDtypeStruct((rows, D_FF), BF16),
        compiler_params=pltpu.CompilerParams(
            dimension_semantics=("arbitrary", "arbitrary"), vmem_limit_bytes=VMEM_LIMIT),
        name="expert_in",
    )(blk_e, n_active, xs, w_in, w_in, b_in, b_in)


def _expert_out_kernel(be_ref, na_ref, a_ref, w_ref, b_ref, y_ref, wb_ref):
    i = pl.program_id(1)

    @pl.when(_expert_changed(i, be_ref))
    def _():
        wb_ref[...] = w_ref[0].astype(BF16)

    @pl.when(i < na_ref[0])
    def _():
        y_ref[...] = jnp.dot(a_ref[...], wb_ref[...], preferred_element_type=F32) + b_ref[0]


def _expert_out(blk_e, n_active, act, w_out, b_out):
    rows = act.shape[0]
    tm, tn = EXP_TM, EXP_OUT_TN
    nb = rows // tm
    nj = D_MODEL // tn

    def row(i, na):
        return jnp.minimum(i, na[0] - 1)

    grid_spec = pltpu.PrefetchScalarGridSpec(
        num_scalar_prefetch=2,
        grid=(nj, nb),
        in_specs=[
            pl.BlockSpec((tm, D_FF), lambda j, i, be, na: (row(i, na), 0)),
            pl.BlockSpec((1, D_FF, tn), lambda j, i, be, na: (be[i], 0, j)),
            pl.BlockSpec((1, 1, tn), lambda j, i, be, na: (be[i], 0, j)),
        ],
        out_specs=pl.BlockSpec((tm, tn), lambda j, i, be, na: (row(i, na), j)),
        scratch_shapes=[pltpu.VMEM((D_FF, tn), BF16)],
    )
    return pl.pallas_call(
        _expert_out_kernel,
        grid_spec=grid_spec,
        out_shape=jax.ShapeDtypeStruct((rows, D_MODEL), F32),
        compiler_params=pltpu.CompilerParams(
            dimension_semantics=("arbitrary", "arbitrary"), vmem_limit_bytes=VMEM_LIMIT),
        name="expert_out",
    )(blk_e, n_active, act, w_out, b_out)


def _combine_kernel(x1_ref, yg_ref, rw_ref, g_ref, o_ref):
    acc = jnp.zeros(x1_ref.shape, F32)
    for k in range(TOP_K):
        acc = acc + yg_ref[k] * rw_ref[:, k:k + 1]
    o_ref[...] = _rms(x1_ref[...] + acc, g_ref[...])


def _combine(x1, yg, rw, gain, row_off, name):
    t = x1.shape[0]
    tm = COMB_TM
    assert t % tm == 0 and row_off % tm == 0
    off = row_off // tm
    return pl.pallas_call(
        _combine_kernel,
        grid=(t // tm,),
        in_specs=[
            pl.BlockSpec((tm, D_MODEL), lambda i: (i, 0)),
            pl.BlockSpec((TOP_K, tm, D_MODEL), lambda i: (0, i + off, 0)),
            pl.BlockSpec((tm, LANES), lambda i: (i, 0)),
            pl.BlockSpec((1, D_MODEL), lambda i: (0, 0)),
        ],
        out_specs=pl.BlockSpec((tm, D_MODEL), lambda i: (i, 0)),
        out_shape=jax.ShapeDtypeStruct((t, D_MODEL), F32),
        compiler_params=pltpu.CompilerParams(
            dimension_semantics=("arbitrary",), vmem_limit_bytes=VMEM_LIMIT),
        name=name,
    )(x1, yg, rw, gain.reshape(1, D_MODEL))


def kernel(x_prompt, x_sample, state_pool, state_conv, cache_mem_k, cache_mem_v, mem_prompt,
           norm_mix, w_in, b_gate, w_pool_group, pool_scale, w_conv, mem_norm, w_mem_kv,
           w_pool_out, w_conv_out, w_attn_out, w_o, norm_ffn, w_router, b_router,
           w_exp_in, b_exp_in, w_exp_out, b_exp_out, final_norm):
    depth = norm_mix.shape[0]
    assert depth == 1
    l = 0
    bp, seq, _ = x_prompt.shape
    bs, ln, _ = x_sample.shape
    tp, ts = bp * seq, bs * ln

    kv = _norm_matmul(mem_prompt.reshape(bp * N_MEM, D_MODEL), mem_norm[l], w_mem_kv[l],
                      jnp.zeros((2 * D_XATTN,), F32), 2 * D_XATTN, "mem_kv")
    mk = kv[:, :D_XATTN].reshape(bp, N_MEM, D_XATTN)
    mv = kv[:, D_XATTN:].reshape(bp, N_MEM, D_XATTN)

    bias_in = jnp.concatenate([jnp.zeros((D_MIX,), F32), b_gate[l]])
    xp = x_prompt.reshape(tp, D_MODEL)
    xs_ = x_sample.reshape(ts, D_MODEL)
    proj_p = _norm_matmul(xp, norm_mix[l], w_in[l], bias_in, D_MIX, "proj_prompt")
    proj_s = _norm_matmul(xs_, norm_mix[l], w_in[l], bias_in, D_MIX, "proj_sample")

    br_p, zst_p = _mix_prompt(proj_p, mk, mv, w_pool_group[l], pool_scale[l], w_conv[l], bp, seq)
    br_s, zst_s = _mix_sample(proj_s, state_pool[l], state_conv[l],
                              cache_mem_k[l].reshape(bs, N_MEM, D_XATTN),
                              cache_mem_v[l].reshape(bs, N_MEM, D_XATTN),
                              w_pool_group[l], pool_scale[l], w_conv[l], bs, ln)

    wpo, wco, wao, wo = (w.astype(BF16) for w in (w_pool_out[l], w_conv_out[l], w_attn_out[l], w_o[l]))
    x1_p, h2_p, ri_p, rw_p, cnt_p = _merge_route(xp, br_p, proj_p, wpo, wco, wao, wo, norm_ffn[l],
                                                  w_router[l], b_router[l], "merge_route_prompt")
    x1_s, h2_s, ri_s, rw_s, cnt_s = _merge_route(xs_, br_s, proj_s, wpo, wco, wao, wo, norm_ffn[l],
                                                  w_router[l], b_router[l], "merge_route_sample")

    t_all = tp + ts
    n_assign = t_all * TOP_K
    nb_max = n_assign // EXP_TM + N_EXPERTS
    size_p = cnt_p[0, :N_EXPERTS].astype(jnp.int32)
    size_s = cnt_s[0, :N_EXPERTS].astype(jnp.int32)
    nblk = (size_p + size_s + EXP_TM - 1) // EXP_TM
    blk_end = jnp.cumsum(nblk)
    pad_start = (blk_end - nblk) * EXP_TM
    n_active = blk_end[-1:]
    blk_ids = jnp.minimum(jnp.arange(nb_max, dtype=jnp.int32), n_active[0] - 1)
    blk_e = jnp.minimum(jnp.searchsorted(blk_end, blk_ids, side="right"), N_EXPERTS - 1).astype(jnp.int32)

    e_p, r_p = ri_p[:, :TOP_K], ri_p[:, TOP_K:2 * TOP_K]
    e_s, r_s = ri_s[:, :TOP_K], ri_s[:, TOP_K:2 * TOP_K]
    dest_p = pad_start[e_p] + r_p
    dest_s = pad_start[e_s] + size_p[e_s] + r_s
    dest = jnp.concatenate([dest_p, dest_s], axis=0)
    tok = jnp.broadcast_to(jnp.arange(t_all, dtype=jnp.int32)[:, None], dest.shape)
    src = jnp.zeros((nb_max * EXP_TM,), jnp.int32).at[dest.reshape(-1)].set(tok.reshape(-1))
    h2 = jnp.concatenate([h2_p, h2_s], axis=0)
    xs_sorted = h2[src]

    act = _expert_in(blk_e, n_active, xs_sorted, w_exp_in[l], b_exp_in[l].reshape(N_EXPERTS, 1, 2 * D_FF))
    ys = _expert_out(blk_e, n_active, act, w_exp_out[l], b_exp_out[l].reshape(N_EXPERTS, 1, D_MODEL))
    yg = ys[dest.T]

    y_p = _combine(x1_p, yg, rw_p, final_norm, 0, "combine_prompt")
    y_s = _combine(x1_s, yg, rw_s, final_norm, tp, "combine_sample")

    new_pool_p = proj_p[:, :D_POOL].reshape(bp, seq, D_POOL)[:, seq - POOL_STATE_LEN:]
    new_conv_p = zst_p[:, 8 - (CONV_WIDTH - 1):]
    u_s = proj_s[:, :D_POOL].reshape(bs, ln, D_POOL)
    new_pool_s = jnp.concatenate([state_pool[l], u_s], axis=1)[:, -POOL_STATE_LEN:]
    new_conv_s = zst_s[:, ln - (CONV_WIDTH - 1):]

    return (y_p.reshape(bp, seq, D_MODEL), y_s.reshape(bs, ln, D_MODEL),
            new_pool_p[None], new_conv_p[None],
            mk.reshape(1, bp, N_MEM, N_XHEADS, XHEAD_DIM), mv.reshape(1, bp, N_MEM, N_XHEADS, XHEAD_DIM),
            new_pool_s[None], new_conv_s[None])
```

```python
import functools

import jax
import jax.numpy as jnp
from jax import lax
from jax.experimental import pallas as pl
from jax.experimental.pallas import tpu as pltpu
from jax.experimental.pallas import tpu_sc as plsc

F32 = jnp.float32
BF16 = jnp.bfloat16
PACKED = jnp.int32

D_MODEL = 2048
POOL_WINDOWS = (2, 4, 8, 16)
POOL_GROUP_DIM = 128
D_POOL = 512
POOL_STATE_LEN = 15
D_CONV = 1024
CONV_WIDTH = 3
N_MEM = 256
N_XHEADS = 4
XHEAD_DIM = 128
D_XATTN = 512
N_BRANCH = 3
D_MIX = D_POOL + 3 * D_CONV + D_XATTN
D_IN_TOTAL = D_MIX + N_BRANCH * D_MODEL
N_EXPERTS = 32
TOP_K = 4
D_FF = D_MODEL
SWIGLU_LIMIT = 7.0
SWIGLU_ALPHA = 1.702
EPS = 1e-5

C_U = 0
C_V = D_POOL
C_B = D_POOL + D_CONV
C_C = D_POOL + 2 * D_CONV
C_Q = D_POOL + 3 * D_CONV

LANES = 128
HIST = 16

PROJ_TM = 1024
PROJ_TN = 512
MIX_TM = 256
MIX_NS = 8
MERGE_TM = 256
EXP_TM = 256
EXP_IN_TN = 512
EXP_OUT_TN = 1024
COMB_TM = 256
SC_CORES = 2
SC_SUBCORES = 16
SC_WORKERS = SC_CORES * SC_SUBCORES
SC_CHUNK = 32
VMEM_LIMIT = 56 * 1024 * 1024


def _sigmoid(x):
    return 1.0 / (1.0 + jnp.exp(-x))


def _rms(x, g):
    ms = jnp.mean(x * x, axis=-1, keepdims=True)
    return x * lax.rsqrt(ms + EPS) * g


def _pack_bf16_pairs(x):
    n = x.shape[1] // 2
    lo = lax.bitcast_convert_type(x[:, :n].astype(BF16).astype(F32), jnp.uint32)
    hi = lax.bitcast_convert_type(x[:, n:].astype(BF16).astype(F32), jnp.uint32)
    return lax.bitcast_convert_type((hi & jnp.uint32(0xFFFF0000)) | (lo >> 16), PACKED)


def _unpack_bf16_pairs(p):
    p = lax.bitcast_convert_type(p, jnp.uint32)
    lo = lax.bitcast_convert_type(p << 16, F32)
    hi = lax.bitcast_convert_type(p & jnp.uint32(0xFFFF0000), F32)
    return lo, hi


def _norm_matmul_kernel(x_ref, g_ref, w_ref, b_ref, o_ref, h_ref, *, act_from):
    j = pl.program_id(1)

    @pl.when(j == 0)
    def _():
        h_ref[...] = _rms(x_ref[...], g_ref[...]).astype(BF16)

    acc = jnp.dot(h_ref[...], w_ref[...].astype(BF16), preferred_element_type=F32) + b_ref[...]

    @pl.when(j < act_from)
    def _():
        o_ref[...] = acc

    @pl.when(j >= act_from)
    def _():
        o_ref[...] = _sigmoid(acc)


def _norm_matmul(x, gain, w, bias, act_from_col, name):
    t, d = x.shape
    n = w.shape[1]
    tm = min(PROJ_TM, t)
    tn = PROJ_TN
    assert t % tm == 0 and n % tn == 0 and act_from_col % tn == 0
    return pl.pallas_call(
        functools.partial(_norm_matmul_kernel, act_from=act_from_col // tn),
        grid=(t // tm, n // tn),
        in_specs=[
            pl.BlockSpec((tm, d), lambda i, j: (i, 0)),
            pl.BlockSpec((1, d), lambda i, j: (0, 0)),
            pl.BlockSpec((d, tn), lambda i, j: (0, j)),
            pl.BlockSpec((1, tn), lambda i, j: (0, j)),
        ],
        out_specs=pl.BlockSpec((tm, tn), lambda i, j: (i, j)),
        out_shape=jax.ShapeDtypeStruct((t, n), F32),
        scratch_shapes=[pltpu.VMEM((tm, d), BF16)],
        compiler_params=pltpu.CompilerParams(
            dimension_semantics=("arbitrary", "arbitrary"), vmem_limit_bytes=VMEM_LIMIT),
        name=name,
    )(x, gain.reshape(1, d), w, bias.reshape(1, n))


def _pool_project(pooled, wpg_ref, scale_ref, g):
    sl = slice(g * POOL_GROUP_DIM, (g + 1) * POOL_GROUP_DIM)
    y = jnp.dot(pooled.astype(BF16), wpg_ref[g].astype(BF16), preferred_element_type=F32)
    return y * scale_ref[:, sl]


def _softmax_rows(s):
    m = jnp.max(s, axis=-1, keepdims=True)
    e = jnp.exp(s - m)
    return e / jnp.sum(e, axis=-1, keepdims=True)


def _mix_prompt_kernel(cur_ref, prev_ref, mk_ref, mv_ref, wpg_ref, scale_ref, wconv_ref,
                       br_ref, zst_ref, extu_ref, extz_ref):
    t = pl.program_id(1)
    tm = cur_ref.shape[0]
    has_prev = t > 0

    u = cur_ref[:, C_U:C_U + D_POOL]
    extu_ref[0:HIST, :] = jnp.where(has_prev, prev_ref[:, C_U:C_U + D_POOL], 0.0)
    extu_ref[HIST:HIST + tm, :] = u
    pos = t * tm + lax.broadcasted_iota(jnp.int32, (tm, 1), 0)
    for g, w in enumerate(POOL_WINDOWS):
        sl = slice(g * POOL_GROUP_DIM, (g + 1) * POOL_GROUP_DIM)
        s = extu_ref[HIST:HIST + tm, sl]
        for k in range(1, w):
            s = s + extu_ref[HIST - k:HIST - k + tm, sl]
        cnt = jnp.minimum(w, pos + 1).astype(F32)
        pooled = s / cnt - extu_ref[HIST:HIST + tm, sl]
        br_ref[:, sl] = _pool_project(pooled, wpg_ref, scale_ref, g).astype(BF16)

    z = cur_ref[:, C_C:C_C + D_CONV] * cur_ref[:, C_V:C_V + D_CONV]
    zprev = prev_ref[:, C_C:C_C + D_CONV] * prev_ref[:, C_V:C_V + D_CONV]
    extz_ref[0:HIST, :] = jnp.where(has_prev, zprev, 0.0)
    extz_ref[HIST:HIST + tm, :] = z
    y = extz_ref[HIST - 2:HIST - 2 + tm, :] * wconv_ref[0:1, :]
    y = y + extz_ref[HIST - 1:HIST - 1 + tm, :] * wconv_ref[1:2, :]
    y = y + extz_ref[HIST:HIST + tm, :] * wconv_ref[2:3, :]
    br_ref[:, D_POOL:D_POOL + D_CONV] = (cur_ref[:, C_B:C_B + D_CONV] * y).astype(BF16)
    zst_ref[0] = extz_ref[HIST + tm - 8:HIST + tm, :]

    for h in range(N_XHEADS):
        sl = slice(h * XHEAD_DIM, (h + 1) * XHEAD_DIM)
        qh = cur_ref[:, C_Q + h * XHEAD_DIM:C_Q + (h + 1) * XHEAD_DIM].astype(BF16)
        kh = mk_ref[0, :, sl].astype(BF16)
        vh = mv_ref[0, :, sl].astype(BF16)
        s = lax.dot_general(qh, kh, (((1,), (1,)), ((), ())), preferred_element_type=F32)
        p = _softmax_rows(s * (XHEAD_DIM ** -0.5))
        o = jnp.dot(p.astype(BF16), vh, preferred_element_type=F32)
        c0 = D_POOL + D_CONV + h * XHEAD_DIM
        br_ref[:, c0:c0 + XHEAD_DIM] = o.astype(BF16)


def _mix_prompt(proj, mk, mv, wpg, scale, wconv, batch, seq):
    tm = MIX_TM
    nt = seq // tm
    assert seq % tm == 0 and tm % HIST == 0
    rpb = tm // HIST
    return pl.pallas_call(
        _mix_prompt_kernel,
        grid=(batch, nt),
        in_specs=[
            pl.BlockSpec((tm, D_MIX), lambda b, t: (b * nt + t, 0)),
            pl.BlockSpec((HIST, D_MIX), lambda b, t: (jnp.maximum((b * nt + t) * rpb - 1, 0), 0)),
            pl.BlockSpec((1, N_MEM, D_XATTN), lambda b, t: (b, 0, 0)),
            pl.BlockSpec((1, N_MEM, D_XATTN), lambda b, t: (b, 0, 0)),
            pl.BlockSpec((len(POOL_WINDOWS), POOL_GROUP_DIM, POOL_GROUP_DIM), lambda b, t: (0, 0, 0)),
            pl.BlockSpec((1, D_POOL), lambda b, t: (0, 0)),
            pl.BlockSpec((CONV_WIDTH, D_CONV), lambda b, t: (0, 0)),
        ],
        out_specs=[
            pl.BlockSpec((tm, D_MODEL), lambda b, t: (b * nt + t, 0)),
            pl.BlockSpec((1, 8, D_CONV), lambda b, t: (b, 0, 0)),
        ],
        out_shape=[
            jax.ShapeDtypeStruct((batch * seq, D_MODEL), BF16),
            jax.ShapeDtypeStruct((batch, 8, D_CONV), F32),
        ],
        scratch_shapes=[pltpu.VMEM((HIST + tm, D_POOL), F32), pltpu.VMEM((HIST + tm, D_CONV), F32)],
        compiler_params=pltpu.CompilerParams(
            dimension_semantics=("arbitrary", "arbitrary"), vmem_limit_bytes=VMEM_LIMIT),
        name="mix_prompt",
    )(proj, proj, mk, mv, wpg, scale.reshape(1, D_POOL), wconv)


def _mix_sample_kernel(cur_ref, sp_ref, sc_ref, k_ref, v_ref, wpg_ref, scale_ref, wconv_ref,
                       br_ref, zst_ref, extu_ref, extz_ref):
    ns, ln = zst_ref.shape[0], zst_ref.shape[1]
    rows = ns * ln

    extu_ref[:, HIST - POOL_STATE_LEN:HIST, :] = sp_ref[...]
    extu_ref[:, HIST:HIST + ln, :] = cur_ref[:, C_U:C_U + D_POOL].reshape(ns, ln, D_POOL)
    for g, w in enumerate(POOL_WINDOWS):
        sl = slice(g * POOL_GROUP_DIM, (g + 1) * POOL_GROUP_DIM)
        s = extu_ref[:, HIST:HIST + ln, sl]
        for k in range(1, w):
            s = s + extu_ref[:, HIST - k:HIST - k + ln, sl]
        pooled = s / float(w) - extu_ref[:, HIST:HIST + ln, sl]
        pooled = pooled.reshape(rows, POOL_GROUP_DIM)
        br_ref[:, sl] = _pool_project(pooled, wpg_ref, scale_ref, g).astype(BF16)

    z = cur_ref[:, C_C:C_C + D_CONV] * cur_ref[:, C_V:C_V + D_CONV]
    extz_ref[:, HIST - 2:HIST, :] = sc_ref[...]
    extz_ref[:, HIST:HIST + ln, :] = z.reshape(ns, ln, D_CONV)
    y = extz_ref[:, HIST - 2:HIST - 2 + ln, :] * wconv_ref[0:1, :]
    y = y + extz_ref[:, HIST - 1:HIST - 1 + ln, :] * wconv_ref[1:2, :]
    y = y + extz_ref[:, HIST:HIST + ln, :] * wconv_ref[2:3, :]
    br_ref[:, D_POOL:D_POOL + D_CONV] = (
        cur_ref[:, C_B:C_B + D_CONV] * y.reshape(rows, D_CONV)).astype(BF16)
    zst_ref[...] = extz_ref[:, HIST:HIST + ln, :]

    q3 = cur_ref[:, C_Q:C_Q + D_XATTN].reshape(ns, ln, D_XATTN)
    q4 = jnp.concatenate([q3[:, :, h * XHEAD_DIM:(h + 1) * XHEAD_DIM] for h in range(N_XHEADS)], axis=1)
    s = jnp.einsum("nqd,nkd->nqk", q4.astype(BF16), k_ref[...].astype(BF16),
                   preferred_element_type=F32)
    row_head = lax.broadcasted_iota(jnp.int32, (N_XHEADS * ln, N_XHEADS * N_MEM), 0) // ln
    col_head = lax.broadcasted_iota(jnp.int32, (N_XHEADS * ln, N_XHEADS * N_MEM), 1) % N_XHEADS
    s = jnp.where((row_head == col_head)[None], s * (XHEAD_DIM ** -0.5), -jnp.inf)
    p = _softmax_rows(s)
    r = jnp.einsum("nqk,nkd->nqd", p.astype(BF16), v_ref[...].astype(BF16),
                   preferred_element_type=F32)
    for h in range(N_XHEADS):
        c0 = D_POOL + D_CONV + h * XHEAD_DIM
        br_ref[:, c0:c0 + XHEAD_DIM] = r[:, h * ln:(h + 1) * ln, :].reshape(rows, XHEAD_DIM).astype(BF16)


def _mix_sample(proj, state_pool, state_conv, mem_k, mem_v, wpg, scale, wconv, nseq, ln):
    ns = MIX_NS
    rows = ns * ln
    assert nseq % ns == 0 and ln == 8
    return pl.pallas_call(
        _mix_sample_kernel,
        grid=(nseq // ns,),
        in_specs=[
            pl.BlockSpec((rows, D_MIX), lambda s: (s, 0)),
            pl.BlockSpec((ns, POOL_STATE_LEN, D_POOL), lambda s: (s, 0, 0)),
            pl.BlockSpec((ns, CONV_WIDTH - 1, D_CONV), lambda s: (s, 0, 0)),
            pl.BlockSpec((ns, N_MEM * N_XHEADS, XHEAD_DIM), lambda s: (s, 0, 0)),
            pl.BlockSpec((ns, N_MEM * N_XHEADS, XHEAD_DIM), lambda s: (s, 0, 0)),
            pl.BlockSpec((len(POOL_WINDOWS), POOL_GROUP_DIM, POOL_GROUP_DIM), lambda s: (0, 0, 0)),
            pl.BlockSpec((1, D_POOL), lambda s: (0, 0)),
            pl.BlockSpec((CONV_WIDTH, D_CONV), lambda s: (0, 0)),
        ],
        out_specs=[
            pl.BlockSpec((rows, D_MODEL), lambda s: (s, 0)),
            pl.BlockSpec((ns, ln, D_CONV), lambda s: (s, 0, 0)),
        ],
        out_shape=[
            jax.ShapeDtypeStruct((nseq * ln, D_MODEL), BF16),
            jax.ShapeDtypeStruct((nseq, ln, D_CONV), F32),
        ],
        scratch_shapes=[pltpu.VMEM((ns, HIST + ln, D_POOL), F32),
                        pltpu.VMEM((ns, HIST + ln, D_CONV), F32)],
        compiler_params=pltpu.CompilerParams(
            dimension_semantics=("arbitrary",), vmem_limit_bytes=VMEM_LIMIT),
        name="mix_sample",
    )(proj, state_pool, state_conv, mem_k, mem_v, wpg, scale.reshape(1, D_POOL), wconv)


def _merge_route_kernel(x_ref, br_ref, g0_ref, g1_ref, g2_ref, wpo_ref, wco_ref, wao_ref, wo_ref,
                        nf_ref, wr_ref, brt_ref,
                        x1_ref, h2_ref, ri_ref, rw_ref, cnt_ref, carry_ref):
    i = pl.program_id(0)
    tm = x_ref.shape[0]

    @pl.when(i == 0)
    def _():
        carry_ref[...] = jnp.zeros_like(carry_ref)

    merged = g0_ref[...] * jnp.dot(br_ref[:, 0:D_POOL], wpo_ref[...], preferred_element_type=F32)
    merged = merged + g1_ref[...] * jnp.dot(br_ref[:, D_POOL:D_POOL + D_CONV], wco_ref[...],
                                            preferred_element_type=F32)
    merged = merged + g2_ref[...] * jnp.dot(br_ref[:, D_POOL + D_CONV:D_MODEL], wao_ref[...],
                                            preferred_element_type=F32)
    x1 = x_ref[...] + jnp.dot(merged.astype(BF16), wo_ref[...], preferred_element_type=F32)
    x1_ref[...] = x1
    h2 = _rms(x1, nf_ref[...])
    h2_ref[...] = _pack_bf16_pairs(h2)

    logits = jnp.dot(h2, wr_ref[...], preferred_element_type=F32,
                     precision=lax.Precision.HIGHEST) + brt_ref[...]
    lane = lax.broadcasted_iota(jnp.int32, (tm, N_EXPERTS), 1).astype(F32)
    vals, idxs, hots = [], [], []
    work = logits
    for _ in range(TOP_K):
        m = jnp.max(work, axis=-1, keepdims=True)
        idx = jnp.min(jnp.where(work == m, lane, float(N_EXPERTS)), axis=-1, keepdims=True)
        hot = lane == idx
        work = jnp.where(hot, -jnp.inf, work)
        vals.append(m)
        idxs.append(idx)
        hots.append(hot)
    es = [jnp.exp(v - vals[0]) for v in vals]
    denom = es[0] + es[1] + es[2] + es[3]

    chosen = jnp.where(hots[0] | hots[1] | hots[2] | hots[3], 1.0, 0.0).astype(BF16)
    r_i = lax.broadcasted_iota(jnp.int32, (tm, tm), 0)
    c_i = lax.broadcasted_iota(jnp.int32, (tm, tm), 1)
    lower = jnp.where(c_i < r_i, 1.0, 0.0).astype(BF16)
    before = jnp.dot(lower, chosen, preferred_element_type=F32) + carry_ref[0:1, 0:N_EXPERTS]
    carry_ref[0:1, 0:N_EXPERTS] = (carry_ref[0:1, 0:N_EXPERTS]
                                   + jnp.sum(chosen.astype(F32), axis=0, keepdims=True))
    cnt_ref[...] = carry_ref[...]

    out_lane = lax.broadcasted_iota(jnp.int32, (tm, LANES), 1)
    ri = jnp.zeros((tm, LANES), jnp.int32)
    rw = jnp.zeros((tm, LANES), F32)
    for k in range(TOP_K):
        rank = jnp.sum(jnp.where(hots[k], before, 0.0), axis=-1, keepdims=True).astype(jnp.int32)
        ri = jnp.where(out_lane == k, idxs[k].astype(jnp.int32), ri)
        ri = jnp.where(out_lane == TOP_K + k, rank, ri)
        rw = jnp.where(out_lane == k, es[k] / denom, rw)
    ri_ref[...] = ri
    rw_ref[...] = rw


def _merge_route(x, branch, proj, wpo, wco, wao, wo, norm_ffn, w_router, b_router, name):
    t = x.shape[0]
    tm = MERGE_TM
    assert t % tm == 0
    gate_blk0 = D_MIX // D_MODEL
    assert D_MIX % D_MODEL == 0
    const = lambda i: (0, 0)
    return pl.pallas_call(
        _merge_route_kernel,
        grid=(t // tm,),
        in_specs=[
            pl.BlockSpec((tm, D_MODEL), lambda i: (i, 0)),
            pl.BlockSpec((tm, D_MODEL), lambda i: (i, 0)),
            pl.BlockSpec((tm, D_MODEL), lambda i: (i, gate_blk0)),
            pl.BlockSpec((tm, D_MODEL), lambda i: (i, gate_blk0 + 1)),
            pl.BlockSpec((tm, D_MODEL), lambda i: (i, gate_blk0 + 2)),
            pl.BlockSpec((D_POOL, D_MODEL), const, pipeline_mode=pl.Buffered(1)),
            pl.BlockSpec((D_CONV, D_MODEL), const, pipeline_mode=pl.Buffered(1)),
            pl.BlockSpec((D_XATTN, D_MODEL), const, pipeline_mode=pl.Buffered(1)),
            pl.BlockSpec((D_MODEL, D_MODEL), const, pipeline_mode=pl.Buffered(1)),
            pl.BlockSpec((1, D_MODEL), const),
            pl.BlockSpec((D_MODEL, N_EXPERTS), const),
            pl.BlockSpec((1, N_EXPERTS), const),
        ],
        out_specs=[
            pl.BlockSpec((tm, D_MODEL), lambda i: (i, 0)),
            pl.BlockSpec((tm, D_MODEL // 2), lambda i: (i, 0)),
            pl.BlockSpec((tm, LANES), lambda i: (i, 0)),
            pl.BlockSpec((tm, LANES), lambda i: (i, 0)),
            pl.BlockSpec((8, LANES), const),
        ],
        out_shape=[
            jax.ShapeDtypeStruct((t, D_MODEL), F32),
            jax.ShapeDtypeStruct((t, D_MODEL // 2), PACKED),
            jax.ShapeDtypeStruct((t, LANES), jnp.int32),
            jax.ShapeDtypeStruct((t, LANES), F32),
            jax.ShapeDtypeStruct((8, LANES), F32),
        ],
        scratch_shapes=[pltpu.VMEM((8, LANES), F32)],
        compiler_params=pltpu.CompilerParams(
            dimension_semantics=("arbitrary",), vmem_limit_bytes=VMEM_LIMIT),
        name=name,
    )(x, branch, proj, proj, proj, wpo, wco, wao, wo, norm_ffn.reshape(1, D_MODEL),
      w_router, b_router.reshape(1, N_EXPERTS))


def _sc_mesh():
    return plsc.VectorSubcoreMesh(core_axis_name="c", subcore_axis_name="s",
                                  num_cores=SC_CORES, num_subcores=SC_SUBCORES)


def _sc_worker_id():
    return lax.axis_index("s") * SC_CORES + lax.axis_index("c")


def _sc_dispatch(h2, dest, rows_out):
    t, w = h2.shape
    per_w = t // SC_WORKERS
    n_chunks = per_w // SC_CHUNK
    assert t == SC_WORKERS * n_chunks * SC_CHUNK
    idx = dest.reshape(SC_WORKERS, n_chunks, SC_CHUNK, TOP_K).transpose(0, 1, 3, 2)
    idx = idx.reshape(SC_WORKERS, n_chunks * TOP_K, SC_CHUNK)

    def body(h2_hbm, idx_hbm, xs_hbm, idx_v, rows_v, sems):
        base = _sc_worker_id() * per_w
        pltpu.sync_copy(idx_hbm.at[_sc_worker_id()], idx_v)
        pending = [[], []]
        for c in range(n_chunks):
            b = c % 2
            for d in pending[b]:
                d.wait()
            pltpu.sync_copy(h2_hbm.at[pl.ds(base + c * SC_CHUNK, SC_CHUNK)], rows_v.at[b])
            pending[b] = [pltpu.async_copy(rows_v.at[b], xs_hbm.at[idx_v.at[c * TOP_K + k]], sems.at[b])
                          for k in range(TOP_K)]
        for b in range(2):
            for d in pending[b]:
                d.wait()

    return pl.kernel(
        body,
        out_type=jax.ShapeDtypeStruct((rows_out, w), h2.dtype),
        mesh=_sc_mesh(),
        scratch_types=[pltpu.VMEM((n_chunks * TOP_K, SC_CHUNK), jnp.int32),
                       pltpu.VMEM((2, SC_CHUNK, w), h2.dtype),
                       pltpu.SemaphoreType.DMA((2,))],
        name="sc_dispatch",
    )(h2, idx)


def _sc_gather_rows(table, idx):
    n = idx.shape[0]
    w = table.shape[1]
    per_w = n // SC_WORKERS
    n_chunks = per_w // SC_CHUNK
    assert n == SC_WORKERS * n_chunks * SC_CHUNK
    idx3 = idx.reshape(SC_WORKERS, n_chunks, SC_CHUNK)

    def body(table_hbm, idx_hbm, out_hbm, idx_v, rows_v, gsem, wsems):
        base = _sc_worker_id() * per_w
        pltpu.sync_copy(idx_hbm.at[_sc_worker_id()], idx_v)
        pending = [None, None]
        for c in range(n_chunks):
            b = c % 2
            if pending[b] is not None:
                pending[b].wait()
            pltpu.async_copy(table_hbm.at[idx_v.at[c]], rows_v.at[b], gsem).wait()
            pending[b] = pltpu.async_copy(
                rows_v.at[b], out_hbm.at[pl.ds(base + c * SC_CHUNK, SC_CHUNK)], wsems.at[b])
        for b in range(2):
            if pending[b] is not None:
                pending[b].wait()

    return pl.kernel(
        body,
        out_type=jax.ShapeDtypeStruct((n, w), table.dtype),
        mesh=_sc_mesh(),
        scratch_types=[pltpu.VMEM((n_chunks, SC_CHUNK), jnp.int32),
                       pltpu.VMEM((2, SC_CHUNK, w), table.dtype),
                       pltpu.SemaphoreType.DMA,
                       pltpu.SemaphoreType.DMA((2,))],
        name="sc_gather_rows",
    )(table, idx3)


def _expert_changed(i, be_ref):
    return (i == 0) | (be_ref[i] != be_ref[jnp.maximum(i - 1, 0)])


def _expert_in_kernel(be_ref, na_ref, xs_ref, wg_ref, wu_ref, bg_ref, bu_ref, a_ref, wgb_ref, wub_ref):
    i = pl.program_id(1)

    @pl.when(_expert_changed(i, be_ref))
    def _():
        wgb_ref[...] = wg_ref[0].astype(BF16)
        wub_ref[...] = wu_ref[0].astype(BF16)

    @pl.when(i < na_ref[0])
    def _():
        x = jnp.concatenate(_unpack_bf16_pairs(xs_ref[...]), axis=1).astype(BF16)
        g = jnp.dot(x, wgb_ref[...], preferred_element_type=F32) + bg_ref[0]
        up = jnp.dot(x, wub_ref[...], preferred_element_type=F32) + bu_ref[0]
        g = jnp.minimum(g, SWIGLU_LIMIT)
        up = jnp.clip(up, -SWIGLU_LIMIT, SWIGLU_LIMIT)
        a_ref[...] = (g * _sigmoid(SWIGLU_ALPHA * g) * (up + 1.0)).astype(BF16)


def _expert_in(blk_e, n_active, xs, w_in, b_in):
    rows = xs.shape[0]
    tm, tn = EXP_TM, EXP_IN_TN
    nb = rows // tm
    nj = D_FF // tn

    def row(i, na):
        return jnp.minimum(i, na[0] - 1)

    grid_spec = pltpu.PrefetchScalarGridSpec(
        num_scalar_prefetch=2,
        grid=(nj, nb),
        in_specs=[
            pl.BlockSpec((tm, D_MODEL // 2), lambda j, i, be, na: (row(i, na), 0)),
            pl.BlockSpec((1, D_MODEL, tn), lambda j, i, be, na: (be[i], 0, j)),
            pl.BlockSpec((1, D_MODEL, tn), lambda j, i, be, na: (be[i], 0, nj + j)),
            pl.BlockSpec((1, 1, tn), lambda j, i, be, na: (be[i], 0, j)),
            pl.BlockSpec((1, 1, tn), lambda j, i, be, na: (be[i], 0, nj + j)),
        ],
        out_specs=pl.BlockSpec((tm, tn), lambda j, i, be, na: (row(i, na), j)),
        scratch_shapes=[pltpu.VMEM((D_MODEL, tn), BF16), pltpu.VMEM((D_MODEL, tn), BF16)],
    )
    return pl.pallas_call(
        _expert_in_kernel,
        grid_spec=grid_spec,
        out_shape=jax.ShapeDtypeStruct((rows, D_FF), BF16),
        compiler_params=pltpu.CompilerParams(
            dimension_semantics=("arbitrary", "arbitrary"), vmem_limit_bytes=VMEM_LIMIT),
        name="expert_in",
    )(blk_e, n_active, xs, w_in, w_in, b_in, b_in)


def _expert_out_kernel(be_ref, na_ref, a_ref, w_ref, b_ref, y_ref, wb_ref):
    i = pl.program_id(1)

    @pl.when(_expert_changed(i, be_ref))
    def _():
        wb_ref[...] = w_ref[0].astype(BF16)

    @pl.when(i < na_ref[0])
    def _():
        y = jnp.dot(a_ref[...], wb_ref[...], preferred_element_type=F32) + b_ref[0]
        y_ref[...] = _pack_bf16_pairs(y)


def _expert_out(blk_e, n_active, act, w_out, b_out):
    rows = act.shape[0]
    tm, tn = EXP_TM, EXP_OUT_TN
    nb = rows // tm
    nj = D_MODEL // tn

    def row(i, na):
        return jnp.minimum(i, na[0] - 1)

    grid_spec = pltpu.PrefetchScalarGridSpec(
        num_scalar_prefetch=2,
        grid=(nj, nb),
        in_specs=[
            pl.BlockSpec((tm, D_FF), lambda j, i, be, na: (row(i, na), 0)),
            pl.BlockSpec((1, D_FF, tn), lambda j, i, be, na: (be[i], 0, j)),
            pl.BlockSpec((1, 1, tn), lambda j, i, be, na: (be[i], 0, j)),
        ],
        out_specs=pl.BlockSpec((tm, tn // 2), lambda j, i, be, na: (row(i, na), j)),
        scratch_shapes=[pltpu.VMEM((D_FF, tn), BF16)],
    )
    return pl.pallas_call(
        _expert_out_kernel,
        grid_spec=grid_spec,
        out_shape=jax.ShapeDtypeStruct((rows, D_MODEL // 2), PACKED),
        compiler_params=pltpu.CompilerParams(
            dimension_semantics=("arbitrary", "arbitrary"), vmem_limit_bytes=VMEM_LIMIT),
        name="expert_out",
    )(blk_e, n_active, act, w_out, b_out)


def _combine_kernel(x1_ref, yg_ref, rw_ref, g_ref, o_ref):
    half = EXP_OUT_TN // 2
    pieces = []
    ssq = jnp.zeros((x1_ref.shape[0], 1), F32)
    for jt in range(D_MODEL // EXP_OUT_TN):
        acc_lo = x1_ref[:, jt * EXP_OUT_TN:jt * EXP_OUT_TN + half]
        acc_hi = x1_ref[:, jt * EXP_OUT_TN + half:(jt + 1) * EXP_OUT_TN]
        moe_lo = jnp.zeros_like(acc_lo)
        moe_hi = jnp.zeros_like(acc_hi)
        for k in range(TOP_K):
            lo, hi = _unpack_bf16_pairs(yg_ref[k, :, jt * half:(jt + 1) * half])
            moe_lo = moe_lo + lo * rw_ref[:, k:k + 1]
            moe_hi = moe_hi + hi * rw_ref[:, k:k + 1]
        for acc in (acc_lo + moe_lo, acc_hi + moe_hi):
            ssq = ssq + jnp.sum(acc * acc, axis=-1, keepdims=True)
            pieces.append(acc)
    inv = lax.rsqrt(ssq / D_MODEL + EPS)
    for n, acc in enumerate(pieces):
        o_ref[:, n * half:(n + 1) * half] = acc * inv * g_ref[:, n * half:(n + 1) * half]


def _combine(x1, yg, rw, gain, row_off, name):
    t = x1.shape[0]
    tm = COMB_TM
    assert t % tm == 0 and row_off % tm == 0
    off = row_off // tm
    return pl.pallas_call(
        _combine_kernel,
        grid=(t // tm,),
        in_specs=[
            pl.BlockSpec((tm, D_MODEL), lambda i: (i, 0)),
            pl.BlockSpec((TOP_K, tm, D_MODEL // 2), lambda i: (0, i + off, 0)),
            pl.BlockSpec((tm, LANES), lambda i: (i, 0)),
            pl.BlockSpec((1, D_MODEL), lambda i: (0, 0)),
        ],
        out_specs=pl.BlockSpec((tm, D_MODEL), lambda i: (i, 0)),
        out_shape=jax.ShapeDtypeStruct((t, D_MODEL), F32),
        compiler_params=pltpu.CompilerParams(
            dimension_semantics=("arbitrary",), vmem_limit_bytes=VMEM_LIMIT),
        name=name,
    )(x1, yg, rw, gain.reshape(1, D_MODEL))


def kernel(x_prompt, x_sample, state_pool, state_conv, cache_mem_k, cache_mem_v, mem_prompt,
           norm_mix, w_in, b_gate, w_pool_group, pool_scale, w_conv, mem_norm, w_mem_kv,
           w_pool_out, w_conv_out, w_attn_out, w_o, norm_ffn, w_router, b_router,
           w_exp_in, b_exp_in, w_exp_out, b_exp_out, final_norm):
    depth = norm_mix.shape[0]
    assert depth == 1
    l = 0
    bp, seq, _ = x_prompt.shape
    bs, ln, _ = x_sample.shape
    tp, ts = bp * seq, bs * ln

    kv = _norm_matmul(mem_prompt.reshape(bp * N_MEM, D_MODEL), mem_norm[l], w_mem_kv[l],
                      jnp.zeros((2 * D_XATTN,), F32), 2 * D_XATTN, "mem_kv")
    mk = kv[:, :D_XATTN].reshape(bp, N_MEM, D_XATTN)
    mv = kv[:, D_XATTN:].reshape(bp, N_MEM, D_XATTN)

    bias_in = jnp.concatenate([jnp.zeros((D_MIX,), F32), b_gate[l]])
    xp = x_prompt.reshape(tp, D_MODEL)
    xs_ = x_sample.reshape(ts, D_MODEL)
    proj_p = _norm_matmul(xp, norm_mix[l], w_in[l], bias_in, D_MIX, "proj_prompt")
    proj_s = _norm_matmul(xs_, norm_mix[l], w_in[l], bias_in, D_MIX, "proj_sample")

    br_p, zst_p = _mix_prompt(proj_p, mk, mv, w_pool_group[l], pool_scale[l], w_conv[l], bp, seq)
    br_s, zst_s = _mix_sample(proj_s, state_pool[l], state_conv[l],
                              cache_mem_k[l].reshape(bs, N_MEM * N_XHEADS, XHEAD_DIM),
                              cache_mem_v[l].reshape(bs, N_MEM * N_XHEADS, XHEAD_DIM),
                              w_pool_group[l], pool_scale[l], w_conv[l], bs, ln)

    wpo, wco, wao, wo = (w.astype(BF16) for w in (w_pool_out[l], w_conv_out[l], w_attn_out[l], w_o[l]))
    x1_p, h2_p, ri_p, rw_p, cnt_p = _merge_route(xp, br_p, proj_p, wpo, wco, wao, wo, norm_ffn[l],
                                                  w_router[l], b_router[l], "merge_route_prompt")
    x1_s, h2_s, ri_s, rw_s, cnt_s = _merge_route(xs_, br_s, proj_s, wpo, wco, wao, wo, norm_ffn[l],
                                                  w_router[l], b_router[l], "merge_route_sample")

    t_all = tp + ts
    n_assign = t_all * TOP_K
    nb_max = n_assign // EXP_TM + N_EXPERTS
    size_p = cnt_p[0, :N_EXPERTS].astype(jnp.int32)
    size_s = cnt_s[0, :N_EXPERTS].astype(jnp.int32)
    nblk = (size_p + size_s + EXP_TM - 1) // EXP_TM
    blk_end = jnp.cumsum(nblk)
    pad_start = (blk_end - nblk) * EXP_TM
    n_active = blk_end[-1:]
    blk_ids = jnp.minimum(jnp.arange(nb_max, dtype=jnp.int32), n_active[0] - 1)
    blk_e = jnp.sum((blk_ids[:, None] >= blk_end[None, :]).astype(jnp.int32), axis=1)

    e_p, r_p = ri_p[:, :TOP_K], ri_p[:, TOP_K:2 * TOP_K]
    e_s, r_s = ri_s[:, :TOP_K], ri_s[:, TOP_K:2 * TOP_K]
    dest_p = pad_start[e_p] + r_p
    dest_s = pad_start[e_s] + size_p[e_s] + r_s
    dest = jnp.concatenate([dest_p, dest_s], axis=0)
    h2 = jnp.concatenate([h2_p, h2_s], axis=0)
    xs_sorted = _sc_dispatch(h2, dest, nb_max * EXP_TM)

    act = _expert_in(blk_e, n_active, xs_sorted, w_exp_in[l], b_exp_in[l].reshape(N_EXPERTS, 1, 2 * D_FF))
    ys = _expert_out(blk_e, n_active, act, w_exp_out[l], b_exp_out[l].reshape(N_EXPERTS, 1, D_MODEL))
    yg = _sc_gather_rows(ys, dest.T.reshape(-1)).reshape(TOP_K, t_all, D_MODEL // 2)

    y_p = _combine(x1_p, yg, rw_p, final_norm, 0, "combine_prompt")
    y_s = _combine(x1_s, yg, rw_s, final_norm, tp, "combine_sample")

    new_pool_p = proj_p[:, :D_POOL].reshape(bp, seq, D_POOL)[:, seq - POOL_STATE_LEN:]
    new_conv_p = zst_p[:, 8 - (CONV_WIDTH - 1):]
    u_s = proj_s[:, :D_POOL].reshape(bs, ln, D_POOL)
    new_pool_s = jnp.concatenate([state_pool[l], u_s], axis=1)[:, -POOL_STATE_LEN:]
    new_conv_s = zst_s[:, ln - (CONV_WIDTH - 1):]

    return (y_p.reshape(bp, seq, D_MODEL), y_s.reshape(bs, ln, D_MODEL),
            new_pool_p[None], new_conv_p[None],
            mk.reshape(1, bp, N_MEM, N_XHEADS, XHEAD_DIM), mv.reshape(1, bp, N_MEM, N_XHEADS, XHEAD_DIM),
            new_pool_s[None], new_conv_s[None])
```

```python
import functools

import jax
import jax.numpy as jnp
from jax import lax
from jax.experimental import pallas as pl
from jax.experimental.pallas import tpu as pltpu
from jax.experimental.pallas import tpu_sc as plsc

F32 = jnp.float32
BF16 = jnp.bfloat16
PACKED = jnp.int32

D_MODEL = 2048
POOL_WINDOWS = (2, 4, 8, 16)
POOL_GROUP_DIM = 128
D_POOL = 512
POOL_STATE_LEN = 15
D_CONV = 1024
CONV_WIDTH = 3
N_MEM = 256
N_XHEADS = 4
XHEAD_DIM = 128
D_XATTN = 512
N_BRANCH = 3
D_MIX = D_POOL + 3 * D_CONV + D_XATTN
D_IN_TOTAL = D_MIX + N_BRANCH * D_MODEL
N_EXPERTS = 32
TOP_K = 4
D_FF = D_MODEL
SWIGLU_LIMIT = 7.0
SWIGLU_ALPHA = 1.702
EPS = 1e-5

C_U = 0
C_V = D_POOL
C_B = D_POOL + D_CONV
C_C = D_POOL + 2 * D_CONV
C_Q = D_POOL + 3 * D_CONV

LANES = 128
HIST = 16

PROJ_TM = 1024
PROJ_TN = 512
MIX_TM = 256
MIX_NS = 8
MERGE_TM = 256
EXP_TM = 256
EXP_IN_TN = 1024
EXP_OUT_TN = 1024
COMB_TM = 256
SC_CORES = 2
SC_SUBCORES = 16
SC_WORKERS = SC_CORES * SC_SUBCORES
SC_CHUNK = 32
VMEM_LIMIT = 56 * 1024 * 1024


def _sigmoid(x):
    return 1.0 / (1.0 + jnp.exp(-x))


def _rms(x, g):
    ms = jnp.mean(x * x, axis=-1, keepdims=True)
    return x * lax.rsqrt(ms + EPS) * g


def _pack_bf16_pairs(x):
    n = x.shape[1] // 2
    lo = lax.bitcast_convert_type(x[:, :n].astype(BF16).astype(F32), jnp.uint32)
    hi = lax.bitcast_convert_type(x[:, n:].astype(BF16).astype(F32), jnp.uint32)
    return lax.bitcast_convert_type((hi & jnp.uint32(0xFFFF0000)) | (lo >> 16), PACKED)


def _unpack_bf16_pairs(p):
    p = lax.bitcast_convert_type(p, jnp.uint32)
    lo = lax.bitcast_convert_type(p << 16, F32)
    hi = lax.bitcast_convert_type(p & jnp.uint32(0xFFFF0000), F32)
    return lo, hi


def _norm_matmul_kernel(x_ref, g_ref, w_ref, b_ref, o_ref, h_ref, *, act_from):
    j = pl.program_id(1)

    @pl.when(j == 0)
    def _():
        h_ref[...] = _rms(x_ref[...], g_ref[...]).astype(BF16)

    acc = jnp.dot(h_ref[...], w_ref[...].astype(BF16), preferred_element_type=F32) + b_ref[...]

    @pl.when(j < act_from)
    def _():
        o_ref[...] = acc

    @pl.when(j >= act_from)
    def _():
        o_ref[...] = _sigmoid(acc)


def _norm_matmul(x, gain, w, bias, act_from_col, name):
    t, d = x.shape
    n = w.shape[1]
    tm = min(PROJ_TM, t)
    tn = PROJ_TN
    assert t % tm == 0 and n % tn == 0 and act_from_col % tn == 0
    return pl.pallas_call(
        functools.partial(_norm_matmul_kernel, act_from=act_from_col // tn),
        grid=(t // tm, n // tn),
        in_specs=[
            pl.BlockSpec((tm, d), lambda i, j: (i, 0)),
            pl.BlockSpec((1, d), lambda i, j: (0, 0)),
            pl.BlockSpec((d, tn), lambda i, j: (0, j)),
            pl.BlockSpec((1, tn), lambda i, j: (0, j)),
        ],
        out_specs=pl.BlockSpec((tm, tn), lambda i, j: (i, j)),
        out_shape=jax.ShapeDtypeStruct((t, n), F32),
        scratch_shapes=[pltpu.VMEM((tm, d), BF16)],
        compiler_params=pltpu.CompilerParams(
            dimension_semantics=("arbitrary", "arbitrary"), vmem_limit_bytes=VMEM_LIMIT),
        name=name,
    )(x, gain.reshape(1, d), w, bias.reshape(1, n))


def _pool_project(pooled, wpg_ref, scale_ref, g):
    sl = slice(g * POOL_GROUP_DIM, (g + 1) * POOL_GROUP_DIM)
    y = jnp.dot(pooled.astype(BF16), wpg_ref[g].astype(BF16), preferred_element_type=F32)
    return y * scale_ref[:, sl]


def _softmax_rows(s):
    m = jnp.max(s, axis=-1, keepdims=True)
    e = jnp.exp(s - m)
    return e / jnp.sum(e, axis=-1, keepdims=True)


def _mix_prompt_kernel(cur_ref, prev_ref, mk_ref, mv_ref, wpg_ref, scale_ref, wconv_ref,
                       br_ref, zst_ref, extu_ref, extz_ref):
    t = pl.program_id(1)
    tm = cur_ref.shape[0]
    has_prev = t > 0

    u = cur_ref[:, C_U:C_U + D_POOL]
    extu_ref[0:HIST, :] = jnp.where(has_prev, prev_ref[:, C_U:C_U + D_POOL], 0.0)
    extu_ref[HIST:HIST + tm, :] = u
    pos = t * tm + lax.broadcasted_iota(jnp.int32, (tm, 1), 0)
    for g, w in enumerate(POOL_WINDOWS):
        sl = slice(g * POOL_GROUP_DIM, (g + 1) * POOL_GROUP_DIM)
        s = extu_ref[HIST:HIST + tm, sl]
        for k in range(1, w):
            s = s + extu_ref[HIST - k:HIST - k + tm, sl]
        cnt = jnp.minimum(w, pos + 1).astype(F32)
        pooled = s / cnt - extu_ref[HIST:HIST + tm, sl]
        br_ref[:, sl] = _pool_project(pooled, wpg_ref, scale_ref, g).astype(BF16)

    z = cur_ref[:, C_C:C_C + D_CONV] * cur_ref[:, C_V:C_V + D_CONV]
    zprev = prev_ref[:, C_C:C_C + D_CONV] * prev_ref[:, C_V:C_V + D_CONV]
    extz_ref[0:HIST, :] = jnp.where(has_prev, zprev, 0.0)
    extz_ref[HIST:HIST + tm, :] = z
    y = extz_ref[HIST - 2:HIST - 2 + tm, :] * wconv_ref[0:1, :]
    y = y + extz_ref[HIST - 1:HIST - 1 + tm, :] * wconv_ref[1:2, :]
    y = y + extz_ref[HIST:HIST + tm, :] * wconv_ref[2:3, :]
    br_ref[:, D_POOL:D_POOL + D_CONV] = (cur_ref[:, C_B:C_B + D_CONV] * y).astype(BF16)
    zst_ref[0] = extz_ref[HIST + tm - 8:HIST + tm, :]

    for h in range(N_XHEADS):
        sl = slice(h * XHEAD_DIM, (h + 1) * XHEAD_DIM)
        qh = cur_ref[:, C_Q + h * XHEAD_DIM:C_Q + (h + 1) * XHEAD_DIM].astype(BF16)
        kh = mk_ref[0, :, sl].astype(BF16)
        vh = mv_ref[0, :, sl].astype(BF16)
        s = lax.dot_general(qh, kh, (((1,), (1,)), ((), ())), preferred_element_type=F32)
        p = _softmax_rows(s * (XHEAD_DIM ** -0.5))
        o = jnp.dot(p.astype(BF16), vh, preferred_element_type=F32)
        c0 = D_POOL + D_CONV + h * XHEAD_DIM
        br_ref[:, c0:c0 + XHEAD_DIM] = o.astype(BF16)


def _mix_prompt(proj, mk, mv, wpg, scale, wconv, batch, seq):
    tm = MIX_TM
    nt = seq // tm
    assert seq % tm == 0 and tm % HIST == 0
    rpb = tm // HIST
    return pl.pallas_call(
        _mix_prompt_kernel,
        grid=(batch, nt),
        in_specs=[
            pl.BlockSpec((tm, D_MIX), lambda b, t: (b * nt + t, 0)),
            pl.BlockSpec((HIST, D_MIX), lambda b, t: (jnp.maximum((b * nt + t) * rpb - 1, 0), 0)),
            pl.BlockSpec((1, N_MEM, D_XATTN), lambda b, t: (b, 0, 0)),
            pl.BlockSpec((1, N_MEM, D_XATTN), lambda b, t: (b, 0, 0)),
            pl.BlockSpec((len(POOL_WINDOWS), POOL_GROUP_DIM, POOL_GROUP_DIM), lambda b, t: (0, 0, 0)),
            pl.BlockSpec((1, D_POOL), lambda b, t: (0, 0)),
            pl.BlockSpec((CONV_WIDTH, D_CONV), lambda b, t: (0, 0)),
        ],
        out_specs=[
            pl.BlockSpec((tm, D_MODEL), lambda b, t: (b * nt + t, 0)),
            pl.BlockSpec((1, 8, D_CONV), lambda b, t: (b, 0, 0)),
        ],
        out_shape=[
            jax.ShapeDtypeStruct((batch * seq, D_MODEL), BF16),
            jax.ShapeDtypeStruct((batch, 8, D_CONV), F32),
        ],
        scratch_shapes=[pltpu.VMEM((HIST + tm, D_POOL), F32), pltpu.VMEM((HIST + tm, D_CONV), F32)],
        compiler_params=pltpu.CompilerParams(
            dimension_semantics=("arbitrary", "arbitrary"), vmem_limit_bytes=VMEM_LIMIT),
        name="mix_prompt",
    )(proj, proj, mk, mv, wpg, scale.reshape(1, D_POOL), wconv)


def _mix_sample_kernel(cur_ref, sp_ref, sc_ref, k_ref, v_ref, wpg_ref, scale_ref, wconv_ref,
                       br_ref, zst_ref, extu_ref, extz_ref):
    ns, ln = zst_ref.shape[0], zst_ref.shape[1]
    rows = ns * ln

    extu_ref[:, HIST - POOL_STATE_LEN:HIST, :] = sp_ref[...]
    extu_ref[:, HIST:HIST + ln, :] = cur_ref[:, C_U:C_U + D_POOL].reshape(ns, ln, D_POOL)
    for g, w in enumerate(POOL_WINDOWS):
        sl = slice(g * POOL_GROUP_DIM, (g + 1) * POOL_GROUP_DIM)
        s = extu_ref[:, HIST:HIST + ln, sl]
        for k in range(1, w):
            s = s + extu_ref[:, HIST - k:HIST - k + ln, sl]
        pooled = s / float(w) - extu_ref[:, HIST:HIST + ln, sl]
        pooled = pooled.reshape(rows, POOL_GROUP_DIM)
        br_ref[:, sl] = _pool_project(pooled, wpg_ref, scale_ref, g).astype(BF16)

    z = cur_ref[:, C_C:C_C + D_CONV] * cur_ref[:, C_V:C_V + D_CONV]
    extz_ref[:, HIST - 2:HIST, :] = sc_ref[...]
    extz_ref[:, HIST:HIST + ln, :] = z.reshape(ns, ln, D_CONV)
    y = extz_ref[:, HIST - 2:HIST - 2 + ln, :] * wconv_ref[0:1, :]
    y = y + extz_ref[:, HIST - 1:HIST - 1 + ln, :] * wconv_ref[1:2, :]
    y = y + extz_ref[:, HIST:HIST + ln, :] * wconv_ref[2:3, :]
    br_ref[:, D_POOL:D_POOL + D_CONV] = (
        cur_ref[:, C_B:C_B + D_CONV] * y.reshape(rows, D_CONV)).astype(BF16)
    zst_ref[...] = extz_ref[:, HIST:HIST + ln, :]

    q3 = cur_ref[:, C_Q:C_Q + D_XATTN].reshape(ns, ln, D_XATTN)
    q4 = jnp.concatenate([q3[:, :, h * XHEAD_DIM:(h + 1) * XHEAD_DIM] for h in range(N_XHEADS)], axis=1)
    s = jnp.einsum("nqd,nkd->nqk", q4.astype(BF16), k_ref[...].astype(BF16),
                   preferred_element_type=F32)
    row_head = lax.broadcasted_iota(jnp.int32, (N_XHEADS * ln, N_XHEADS * N_MEM), 0) // ln
    col_head = lax.broadcasted_iota(jnp.int32, (N_XHEADS * ln, N_XHEADS * N_MEM), 1) % N_XHEADS
    s = jnp.where((row_head == col_head)[None], s * (XHEAD_DIM ** -0.5), -jnp.inf)
    p = _softmax_rows(s)
    r = jnp.einsum("nqk,nkd->nqd", p.astype(BF16), v_ref[...].astype(BF16),
                   preferred_element_type=F32)
    for h in range(N_XHEADS):
        c0 = D_POOL + D_CONV + h * XHEAD_DIM
        br_ref[:, c0:c0 + XHEAD_DIM] = r[:, h * ln:(h + 1) * ln, :].reshape(rows, XHEAD_DIM).astype(BF16)


def _mix_sample(proj, state_pool, state_conv, mem_k, mem_v, wpg, scale, wconv, nseq, ln):
    ns = MIX_NS
    rows = ns * ln
    assert nseq % ns == 0 and ln == 8
    return pl.pallas_call(
        _mix_sample_kernel,
        grid=(nseq // ns,),
        in_specs=[
            pl.BlockSpec((rows, D_MIX), lambda s: (s, 0)),
            pl.BlockSpec((ns, POOL_STATE_LEN, D_POOL), lambda s: (s, 0, 0)),
            pl.BlockSpec((ns, CONV_WIDTH - 1, D_CONV), lambda s: (s, 0, 0)),
            pl.BlockSpec((ns, N_MEM * N_XHEADS, XHEAD_DIM), lambda s: (s, 0, 0)),
            pl.BlockSpec((ns, N_MEM * N_XHEADS, XHEAD_DIM), lambda s: (s, 0, 0)),
            pl.BlockSpec((len(POOL_WINDOWS), POOL_GROUP_DIM, POOL_GROUP_DIM), lambda s: (0, 0, 0)),
            pl.BlockSpec((1, D_POOL), lambda s: (0, 0)),
            pl.BlockSpec((CONV_WIDTH, D_CONV), lambda s: (0, 0)),
        ],
        out_specs=[
            pl.BlockSpec((rows, D_MODEL), lambda s: (s, 0)),
            pl.BlockSpec((ns, ln, D_CONV), lambda s: (s, 0, 0)),
        ],
        out_shape=[
            jax.ShapeDtypeStruct((nseq * ln, D_MODEL), BF16),
            jax.ShapeDtypeStruct((nseq, ln, D_CONV), F32),
        ],
        scratch_shapes=[pltpu.VMEM((ns, HIST + ln, D_POOL), F32),
                        pltpu.VMEM((ns, HIST + ln, D_CONV), F32)],
        compiler_params=pltpu.CompilerParams(
            dimension_semantics=("arbitrary",), vmem_limit_bytes=VMEM_LIMIT),
        name="mix_sample",
    )(proj, state_pool, state_conv, mem_k, mem_v, wpg, scale.reshape(1, D_POOL), wconv)


def _merge_route_kernel(x_ref, br_ref, g0_ref, g1_ref, g2_ref, wpo_ref, wco_ref, wao_ref, wo_ref,
                        nf_ref, wr_ref, brt_ref,
                        x1_ref, h2_ref, ri_ref, rw_ref, cnt_ref, carry_ref):
    i = pl.program_id(0)
    tm = x_ref.shape[0]

    @pl.when(i == 0)
    def _():
        carry_ref[...] = jnp.zeros_like(carry_ref)

    merged = g0_ref[...] * jnp.dot(br_ref[:, 0:D_POOL], wpo_ref[...], preferred_element_type=F32)
    merged = merged + g1_ref[...] * jnp.dot(br_ref[:, D_POOL:D_POOL + D_CONV], wco_ref[...],
                                            preferred_element_type=F32)
    merged = merged + g2_ref[...] * jnp.dot(br_ref[:, D_POOL + D_CONV:D_MODEL], wao_ref[...],
                                            preferred_element_type=F32)
    x1 = x_ref[...] + jnp.dot(merged.astype(BF16), wo_ref[...], preferred_element_type=F32)
    x1_ref[...] = x1
    h2 = _rms(x1, nf_ref[...])
    h2_ref[...] = _pack_bf16_pairs(h2)

    logits = jnp.dot(h2, wr_ref[...], preferred_element_type=F32,
                     precision=lax.Precision.HIGHEST) + brt_ref[...]
    lane = lax.broadcasted_iota(jnp.int32, (tm, N_EXPERTS), 1).astype(F32)
    vals, idxs, hots = [], [], []
    work = logits
    for _ in range(TOP_K):
        m = jnp.max(work, axis=-1, keepdims=True)
        idx = jnp.min(jnp.where(work == m, lane, float(N_EXPERTS)), axis=-1, keepdims=True)
        hot = lane == idx
        work = jnp.where(hot, -jnp.inf, work)
        vals.append(m)
        idxs.append(idx)
        hots.append(hot)
    es = [jnp.exp(v - vals[0]) for v in vals]
    denom = es[0] + es[1] + es[2] + es[3]

    chosen = jnp.where(hots[0] | hots[1] | hots[2] | hots[3], 1.0, 0.0).astype(BF16)
    r_i = lax.broadcasted_iota(jnp.int32, (tm, tm), 0)
    c_i = lax.broadcasted_iota(jnp.int32, (tm, tm), 1)
    lower = jnp.where(c_i < r_i, 1.0, 0.0).astype(BF16)
    before = jnp.dot(lower, chosen, preferred_element_type=F32) + carry_ref[0:1, 0:N_EXPERTS]
    carry_ref[0:1, 0:N_EXPERTS] = (carry_ref[0:1, 0:N_EXPERTS]
                                   + jnp.sum(chosen.astype(F32), axis=0, keepdims=True))
    cnt_ref[...] = carry_ref[...]

    out_lane = lax.broadcasted_iota(jnp.int32, (tm, LANES), 1)
    ri = jnp.zeros((tm, LANES), jnp.int32)
    rw = jnp.zeros((tm, LANES), F32)
    for k in range(TOP_K):
        rank = jnp.sum(jnp.where(hots[k], before, 0.0), axis=-1, keepdims=True).astype(jnp.int32)
        ri = jnp.where(out_lane == k, idxs[k].astype(jnp.int32), ri)
        ri = jnp.where(out_lane == TOP_K + k, rank, ri)
        rw = jnp.where(out_lane == k, es[k] / denom, rw)
    ri_ref[...] = ri
    rw_ref[...] = rw


def _merge_route(x, branch, proj, wpo, wco, wao, wo, norm_ffn, w_router, b_router, name):
    t = x.shape[0]
    tm = MERGE_TM
    assert t % tm == 0
    gate_blk0 = D_MIX // D_MODEL
    assert D_MIX % D_MODEL == 0
    const = lambda i: (0, 0)
    return pl.pallas_call(
        _merge_route_kernel,
        grid=(t // tm,),
        in_specs=[
            pl.BlockSpec((tm, D_MODEL), lambda i: (i, 0)),
            pl.BlockSpec((tm, D_MODEL), lambda i: (i, 0)),
            pl.BlockSpec((tm, D_MODEL), lambda i: (i, gate_blk0)),
            pl.BlockSpec((tm, D_MODEL), lambda i: (i, gate_blk0 + 1)),
            pl.BlockSpec((tm, D_MODEL), lambda i: (i, gate_blk0 + 2)),
            pl.BlockSpec((D_POOL, D_MODEL), const, pipeline_mode=pl.Buffered(1)),
            pl.BlockSpec((D_CONV, D_MODEL), const, pipeline_mode=pl.Buffered(1)),
            pl.BlockSpec((D_XATTN, D_MODEL), const, pipeline_mode=pl.Buffered(1)),
            pl.BlockSpec((D_MODEL, D_MODEL), const, pipeline_mode=pl.Buffered(1)),
            pl.BlockSpec((1, D_MODEL), const),
            pl.BlockSpec((D_MODEL, N_EXPERTS), const),
            pl.BlockSpec((1, N_EXPERTS), const),
        ],
        out_specs=[
            pl.BlockSpec((tm, D_MODEL), lambda i: (i, 0)),
            pl.BlockSpec((tm, D_MODEL // 2), lambda i: (i, 0)),
            pl.BlockSpec((tm, LANES), lambda i: (i, 0)),
            pl.BlockSpec((tm, LANES), lambda i: (i, 0)),
            pl.BlockSpec((8, LANES), const),
        ],
        out_shape=[
            jax.ShapeDtypeStruct((t, D_MODEL), F32),
            jax.ShapeDtypeStruct((t, D_MODEL // 2), PACKED),
            jax.ShapeDtypeStruct((t, LANES), jnp.int32),
            jax.ShapeDtypeStruct((t, LANES), F32),
            jax.ShapeDtypeStruct((8, LANES), F32),
        ],
        scratch_shapes=[pltpu.VMEM((8, LANES), F32)],
        compiler_params=pltpu.CompilerParams(
            dimension_semantics=("arbitrary",), vmem_limit_bytes=VMEM_LIMIT),
        name=name,
    )(x, branch, proj, proj, proj, wpo, wco, wao, wo, norm_ffn.reshape(1, D_MODEL),
      w_router, b_router.reshape(1, N_EXPERTS))


def _sc_mesh():
    return plsc.VectorSubcoreMesh(core_axis_name="c", subcore_axis_name="s",
                                  num_cores=SC_CORES, num_subcores=SC_SUBCORES)


def _sc_worker_id():
    return lax.axis_index("s") * SC_CORES + lax.axis_index("c")


def _sc_dispatch(h2, dest, rows_out):
    t, w = h2.shape
    per_w = t // SC_WORKERS
    n_chunks = per_w // SC_CHUNK
    assert t == SC_WORKERS * n_chunks * SC_CHUNK
    idx = dest.reshape(SC_WORKERS, n_chunks, SC_CHUNK, TOP_K).transpose(0, 1, 3, 2)
    idx = idx.reshape(SC_WORKERS, n_chunks * TOP_K, SC_CHUNK)

    def body(h2_hbm, idx_hbm, xs_hbm, idx_v, rows_v, sems):
        base = _sc_worker_id() * per_w
        pltpu.sync_copy(idx_hbm.at[_sc_worker_id()], idx_v)
        pending = [[], []]
        for c in range(n_chunks):
            b = c % 2
            for d in pending[b]:
                d.wait()
            pltpu.sync_copy(h2_hbm.at[pl.ds(base + c * SC_CHUNK, SC_CHUNK)], rows_v.at[b])
            pending[b] = [pltpu.async_copy(rows_v.at[b], xs_hbm.at[idx_v.at[c * TOP_K + k]], sems.at[b])
                          for k in range(TOP_K)]
        for b in range(2):
            for d in pending[b]:
                d.wait()

    return pl.kernel(
        body,
        out_type=jax.ShapeDtypeStruct((rows_out, w), h2.dtype),
        mesh=_sc_mesh(),
        scratch_types=[pltpu.VMEM((n_chunks * TOP_K, SC_CHUNK), jnp.int32),
                       pltpu.VMEM((2, SC_CHUNK, w), h2.dtype),
                       pltpu.SemaphoreType.DMA((2,))],
        name="sc_dispatch",
    )(h2, idx)


def _sc_gather_rows(table, idx):
    n = idx.shape[0]
    w = table.shape[1]
    per_w = n // SC_WORKERS
    n_chunks = per_w // SC_CHUNK
    assert n == SC_WORKERS * n_chunks * SC_CHUNK
    idx3 = idx.reshape(SC_WORKERS, n_chunks, SC_CHUNK)

    def body(table_hbm, idx_hbm, out_hbm, idx_v, rows_v, gsem, wsems):
        base = _sc_worker_id() * per_w
        pltpu.sync_copy(idx_hbm.at[_sc_worker_id()], idx_v)
        pending = [None, None]
        for c in range(n_chunks):
            b = c % 2
            if pending[b] is not None:
                pending[b].wait()
            pltpu.async_copy(table_hbm.at[idx_v.at[c]], rows_v.at[b], gsem).wait()
            pending[b] = pltpu.async_copy(
                rows_v.at[b], out_hbm.at[pl.ds(base + c * SC_CHUNK, SC_CHUNK)], wsems.at[b])
        for b in range(2):
            if pending[b] is not None:
                pending[b].wait()

    return pl.kernel(
        body,
        out_type=jax.ShapeDtypeStruct((n, w), table.dtype),
        mesh=_sc_mesh(),
        scratch_types=[pltpu.VMEM((n_chunks, SC_CHUNK), jnp.int32),
                       pltpu.VMEM((2, SC_CHUNK, w), table.dtype),
                       pltpu.SemaphoreType.DMA,
                       pltpu.SemaphoreType.DMA((2,))],
        name="sc_gather_rows",
    )(table, idx3)


def _expert_row_loop(n, in_copy, out_copy, prepare_weights, compute):
    in_copy(0, 0).start()
    prepare_weights()

    def body(b, carry):
        slot = lax.rem(b, 2)
        in_copy(b, slot).wait()

        @pl.when(b + 1 < n)
        def _():
            in_copy(b + 1, 1 - slot).start()

        @pl.when(b >= 2)
        def _():
            out_copy(b - 2, slot).wait()

        compute(slot)
        out_copy(b, slot).start()
        return carry

    lax.fori_loop(0, n, body, 0)

    @pl.when(n >= 2)
    def _():
        out_copy(n - 2, lax.rem(n, 2)).wait()

    out_copy(n - 1, lax.rem(n - 1, 2)).wait()


def _expert_in_kernel(row0_ref, nblk_ref, xs_hbm, wg_ref, wu_ref, bg_ref, bu_ref, act_hbm,
                      wgb_ref, wub_ref, x_buf, a_buf, x_sem, a_sem):
    j = pl.program_id(0)
    e = pl.program_id(1)
    n = nblk_ref[e]
    tm, tn = a_buf.shape[1], a_buf.shape[2]

    def rows(b):
        return pl.ds(pl.multiple_of(row0_ref[e] + b * tm, tm), tm)

    def x_copy(b, slot):
        return pltpu.make_async_copy(xs_hbm.at[rows(b)], x_buf.at[slot], x_sem.at[slot])

    def a_copy(b, slot):
        return pltpu.make_async_copy(
            a_buf.at[slot], act_hbm.at[rows(b), pl.ds(pl.multiple_of(j * tn, tn), tn)], a_sem.at[slot])

    def compute(slot):
        x = jnp.concatenate(_unpack_bf16_pairs(x_buf[slot]), axis=1).astype(BF16)
        g = jnp.dot(x, wgb_ref[...], preferred_element_type=F32) + bg_ref[0]
        up = jnp.dot(x, wub_ref[...], preferred_element_type=F32) + bu_ref[0]
        g = jnp.minimum(g, SWIGLU_LIMIT)
        up = jnp.clip(up, -SWIGLU_LIMIT, SWIGLU_LIMIT)
        a_buf[slot] = (g * _sigmoid(SWIGLU_ALPHA * g) * (up + 1.0)).astype(BF16)

    def prepare_weights():
        wgb_ref[...] = wg_ref[0].astype(BF16)
        wub_ref[...] = wu_ref[0].astype(BF16)

    @pl.when(n > 0)
    def _():
        _expert_row_loop(n, x_copy, a_copy, prepare_weights, compute)


def _expert_in(row0, nblk, xs, w_in, b_in):
    rows = xs.shape[0]
    tm, tn = EXP_TM, EXP_IN_TN
    nj = D_FF // tn
    grid_spec = pltpu.PrefetchScalarGridSpec(
        num_scalar_prefetch=2,
        grid=(nj, N_EXPERTS),
        in_specs=[
            pl.BlockSpec(memory_space=pl.ANY),
            pl.BlockSpec((1, D_MODEL, tn), lambda j, e, r0, nb: (e, 0, j)),
            pl.BlockSpec((1, D_MODEL, tn), lambda j, e, r0, nb: (e, 0, nj + j)),
            pl.BlockSpec((1, 1, tn), lambda j, e, r0, nb: (e, 0, j)),
            pl.BlockSpec((1, 1, tn), lambda j, e, r0, nb: (e, 0, nj + j)),
        ],
        out_specs=pl.BlockSpec(memory_space=pl.ANY),
        scratch_shapes=[pltpu.VMEM((D_MODEL, tn), BF16), pltpu.VMEM((D_MODEL, tn), BF16),
                        pltpu.VMEM((2, tm, D_MODEL // 2), PACKED), pltpu.VMEM((2, tm, tn), BF16),
                        pltpu.SemaphoreType.DMA((2,)), pltpu.SemaphoreType.DMA((2,))],
    )
    return pl.pallas_call(
        _expert_in_kernel,
        grid_spec=grid_spec,
        out_shape=jax.ShapeDtypeStruct((rows, D_FF), BF16),
        compiler_params=pltpu.CompilerParams(
            dimension_semantics=("arbitrary", "arbitrary"), vmem_limit_bytes=VMEM_LIMIT),
        name="expert_in",
    )(row0, nblk, xs, w_in, w_in, b_in, b_in)


def _expert_out_kernel(row0_ref, nblk_ref, act_hbm, w_ref, b_ref, ys_hbm,
                       wb_ref, a_buf, y_buf, a_sem, y_sem):
    j = pl.program_id(0)
    e = pl.program_id(1)
    n = nblk_ref[e]
    tm, tnp = y_buf.shape[1], y_buf.shape[2]

    def rows(b):
        return pl.ds(pl.multiple_of(row0_ref[e] + b * tm, tm), tm)

    def a_copy(b, slot):
        return pltpu.make_async_copy(act_hbm.at[rows(b)], a_buf.at[slot], a_sem.at[slot])

    def y_copy(b, slot):
        return pltpu.make_async_copy(
            y_buf.at[slot], ys_hbm.at[rows(b), pl.ds(pl.multiple_of(j * tnp, tnp), tnp)], y_sem.at[slot])

    def prepare_weights():
        wb_ref[...] = w_ref[0].astype(BF16)

    def compute(slot):
        y = jnp.dot(a_buf[slot], wb_ref[...], preferred_element_type=F32) + b_ref[0]
        y_buf[slot] = _pack_bf16_pairs(y)

    @pl.when(n > 0)
    def _():
        _expert_row_loop(n, a_copy, y_copy, prepare_weights, compute)


def _expert_out(row0, nblk, act, w_out, b_out):
    rows = act.shape[0]
    tm, tn = EXP_TM, EXP_OUT_TN
    nj = D_MODEL // tn
    grid_spec = pltpu.PrefetchScalarGridSpec(
        num_scalar_prefetch=2,
        grid=(nj, N_EXPERTS),
        in_specs=[
            pl.BlockSpec(memory_space=pl.ANY),
            pl.BlockSpec((1, D_FF, tn), lambda j, e, r0, nb: (e, 0, j)),
            pl.BlockSpec((1, 1, tn), lambda j, e, r0, nb: (e, 0, j)),
        ],
        out_specs=pl.BlockSpec(memory_space=pl.ANY),
        scratch_shapes=[pltpu.VMEM((D_FF, tn), BF16),
                        pltpu.VMEM((2, tm, D_FF), BF16), pltpu.VMEM((2, tm, tn // 2), PACKED),
                        pltpu.SemaphoreType.DMA((2,)), pltpu.SemaphoreType.DMA((2,))],
    )
    return pl.pallas_call(
        _expert_out_kernel,
        grid_spec=grid_spec,
        out_shape=jax.ShapeDtypeStruct((rows, D_MODEL // 2), PACKED),
        compiler_params=pltpu.CompilerParams(
            dimension_semantics=("arbitrary", "arbitrary"), vmem_limit_bytes=VMEM_LIMIT),
        name="expert_out",
    )(row0, nblk, act, w_out, b_out)


def _combine_kernel(x1_ref, yg_ref, rw_ref, g_ref, o_ref):
    half = EXP_OUT_TN // 2
    pieces = []
    ssq = jnp.zeros((x1_ref.shape[0], 1), F32)
    for jt in range(D_MODEL // EXP_OUT_TN):
        acc_lo = x1_ref[:, jt * EXP_OUT_TN:jt * EXP_OUT_TN + half]
        acc_hi = x1_ref[:, jt * EXP_OUT_TN + half:(jt + 1) * EXP_OUT_TN]
        moe_lo = jnp.zeros_like(acc_lo)
        moe_hi = jnp.zeros_like(acc_hi)
        for k in range(TOP_K):
            lo, hi = _unpack_bf16_pairs(yg_ref[k, :, jt * half:(jt + 1) * half])
            moe_lo = moe_lo + lo * rw_ref[:, k:k + 1]
            moe_hi = moe_hi + hi * rw_ref[:, k:k + 1]
        for acc in (acc_lo + moe_lo, acc_hi + moe_hi):
            ssq = ssq + jnp.sum(acc * acc, axis=-1, keepdims=True)
            pieces.append(acc)
    inv = lax.rsqrt(ssq / D_MODEL + EPS)
    for n, acc in enumerate(pieces):
        o_ref[:, n * half:(n + 1) * half] = acc * inv * g_ref[:, n * half:(n + 1) * half]


def _combine(x1, yg, rw, gain, row_off, name):
    t = x1.shape[0]
    tm = COMB_TM
    assert t % tm == 0 and row_off % tm == 0
    off = row_off // tm
    return pl.pallas_call(
        _combine_kernel,
        grid=(t // tm,),
        in_specs=[
            pl.BlockSpec((tm, D_MODEL), lambda i: (i, 0)),
            pl.BlockSpec((TOP_K, tm, D_MODEL // 2), lambda i: (0, i + off, 0)),
            pl.BlockSpec((tm, LANES), lambda i: (i, 0)),
            pl.BlockSpec((1, D_MODEL), lambda i: (0, 0)),
        ],
        out_specs=pl.BlockSpec((tm, D_MODEL), lambda i: (i, 0)),
        out_shape=jax.ShapeDtypeStruct((t, D_MODEL), F32),
        compiler_params=pltpu.CompilerParams(
            dimension_semantics=("arbitrary",), vmem_limit_bytes=VMEM_LIMIT),
        name=name,
    )(x1, yg, rw, gain.reshape(1, D_MODEL))


def kernel(x_prompt, x_sample, state_pool, state_conv, cache_mem_k, cache_mem_v, mem_prompt,
           norm_mix, w_in, b_gate, w_pool_group, pool_scale, w_conv, mem_norm, w_mem_kv,
           w_pool_out, w_conv_out, w_attn_out, w_o, norm_ffn, w_router, b_router,
           w_exp_in, b_exp_in, w_exp_out, b_exp_out, final_norm):
    depth = norm_mix.shape[0]
    assert depth == 1
    l = 0
    bp, seq, _ = x_prompt.shape
    bs, ln, _ = x_sample.shape
    tp, ts = bp * seq, bs * ln

    kv = _norm_matmul(mem_prompt.reshape(bp * N_MEM, D_MODEL), mem_norm[l], w_mem_kv[l],
                      jnp.zeros((2 * D_XATTN,), F32), 2 * D_XATTN, "mem_kv")
    mk = kv[:, :D_XATTN].reshape(bp, N_MEM, D_XATTN)
    mv = kv[:, D_XATTN:].reshape(bp, N_MEM, D_XATTN)

    bias_in = jnp.concatenate([jnp.zeros((D_MIX,), F32), b_gate[l]])
    xp = x_prompt.reshape(tp, D_MODEL)
    xs_ = x_sample.reshape(ts, D_MODEL)
    proj_p = _norm_matmul(xp, norm_mix[l], w_in[l], bias_in, D_MIX, "proj_prompt")
    proj_s = _norm_matmul(xs_, norm_mix[l], w_in[l], bias_in, D_MIX, "proj_sample")

    br_p, zst_p = _mix_prompt(proj_p, mk, mv, w_pool_group[l], pool_scale[l], w_conv[l], bp, seq)
    br_s, zst_s = _mix_sample(proj_s, state_pool[l], state_conv[l],
                              cache_mem_k[l].reshape(bs, N_MEM * N_XHEADS, XHEAD_DIM),
                              cache_mem_v[l].reshape(bs, N_MEM * N_XHEADS, XHEAD_DIM),
                              w_pool_group[l], pool_scale[l], w_conv[l], bs, ln)

    wpo, wco, wao, wo = (w.astype(BF16) for w in (w_pool_out[l], w_conv_out[l], w_attn_out[l], w_o[l]))
    x1_p, h2_p, ri_p, rw_p, cnt_p = _merge_route(xp, br_p, proj_p, wpo, wco, wao, wo, norm_ffn[l],
                                                  w_router[l], b_router[l], "merge_route_prompt")
    x1_s, h2_s, ri_s, rw_s, cnt_s = _merge_route(xs_, br_s, proj_s, wpo, wco, wao, wo, norm_ffn[l],
                                                  w_router[l], b_router[l], "merge_route_sample")

    t_all = tp + ts
    n_assign = t_all * TOP_K
    nb_max = n_assign // EXP_TM + N_EXPERTS
    size_p = cnt_p[0, :N_EXPERTS].astype(jnp.int32)
    size_s = cnt_s[0, :N_EXPERTS].astype(jnp.int32)
    nblk = (size_p + size_s + EXP_TM - 1) // EXP_TM
    blk_end = jnp.cumsum(nblk)
    pad_start = (blk_end - nblk) * EXP_TM

    e_p, r_p = ri_p[:, :TOP_K], ri_p[:, TOP_K:2 * TOP_K]
    e_s, r_s = ri_s[:, :TOP_K], ri_s[:, TOP_K:2 * TOP_K]
    dest_p = pad_start[e_p] + r_p
    dest_s = pad_start[e_s] + size_p[e_s] + r_s
    dest = jnp.concatenate([dest_p, dest_s], axis=0)
    h2 = jnp.concatenate([h2_p, h2_s], axis=0)
    xs_sorted = _sc_dispatch(h2, dest, nb_max * EXP_TM)

    act = _expert_in(pad_start, nblk, xs_sorted, w_exp_in[l], b_exp_in[l].reshape(N_EXPERTS, 1, 2 * D_FF))
    ys = _expert_out(pad_start, nblk, act, w_exp_out[l], b_exp_out[l].reshape(N_EXPERTS, 1, D_MODEL))
    yg = _sc_gather_rows(ys, dest.T.reshape(-1)).reshape(TOP_K, t_all, D_MODEL // 2)

    y_p = _combine(x1_p, yg, rw_p, final_norm, 0, "combine_prompt")
    y_s = _combine(x1_s, yg, rw_s, final_norm, tp, "combine_sample")

    new_pool_p = proj_p[:, :D_POOL].reshape(bp, seq, D_POOL)[:, seq - POOL_STATE_LEN:]
    new_conv_p = zst_p[:, 8 - (CONV_WIDTH - 1):]
    u_s = proj_s[:, :D_POOL].reshape(bs, ln, D_POOL)
    new_pool_s = jnp.concatenate([state_pool[l], u_s], axis=1)[:, -POOL_STATE_LEN:]
    new_conv_s = zst_s[:, ln - (CONV_WIDTH - 1):]

    return (y_p.reshape(bp, seq, D_MODEL), y_s.reshape(bs, ln, D_MODEL),
            new_pool_p[None], new_conv_p[None],
            mk.reshape(1, bp, N_MEM, N_XHEADS, XHEAD_DIM), mv.reshape(1, bp, N_MEM, N_XHEADS, XHEAD_DIM),
            new_pool_s[None], new_conv_s[None])
```

```python
import functools

import jax
import jax.numpy as jnp
from jax import lax
from jax.experimental import pallas as pl
from jax.experimental.pallas import tpu as pltpu
from jax.experimental.pallas import tpu_sc as plsc

F32 = jnp.float32
BF16 = jnp.bfloat16
PACKED = jnp.int32

D_MODEL = 2048
POOL_WINDOWS = (2, 4, 8, 16)
POOL_GROUP_DIM = 128
D_POOL = 512
POOL_STATE_LEN = 15
D_CONV = 1024
CONV_WIDTH = 3
N_MEM = 256
N_XHEADS = 4
XHEAD_DIM = 128
D_XATTN = 512
N_BRANCH = 3
D_MIX = D_POOL + 3 * D_CONV + D_XATTN
D_IN_TOTAL = D_MIX + N_BRANCH * D_MODEL
N_EXPERTS = 32
TOP_K = 4
D_FF = D_MODEL
SWIGLU_LIMIT = 7.0
SWIGLU_ALPHA = 1.702
EPS = 1e-5

C_U = 0
C_V = D_POOL
C_B = D_POOL + D_CONV
C_C = D_POOL + 2 * D_CONV
C_Q = D_POOL + 3 * D_CONV

LANES = 128
HIST = 16

PROJ_TM = 1024
PROJ_TN = 1024
MIX_TM = 256
MIX_NS = 8
MERGE_TM = 256
EXP_TM = 256
EXP_IN_TN = 1024
EXP_OUT_TN = 1024
W_SPLIT = 4
COMB_TM = 256
SC_CORES = 2
SC_SUBCORES = 16
SC_WORKERS = SC_CORES * SC_SUBCORES
SC_CHUNK = 32
VMEM_LIMIT = 56 * 1024 * 1024


def _sigmoid(x):
    return 1.0 / (1.0 + jnp.exp(-x))


def _rms(x, g):
    ms = jnp.mean(x * x, axis=-1, keepdims=True)
    return x * lax.rsqrt(ms + EPS) * g


def _pack_bf16_pairs(x):
    n = x.shape[1] // 2
    lo = lax.bitcast_convert_type(x[:, :n].astype(BF16).astype(F32), jnp.uint32)
    hi = lax.bitcast_convert_type(x[:, n:].astype(BF16).astype(F32), jnp.uint32)
    return lax.bitcast_convert_type((hi & jnp.uint32(0xFFFF0000)) | (lo >> 16), PACKED)


def _unpack_bf16_pairs(p):
    p = lax.bitcast_convert_type(p, jnp.uint32)
    lo = lax.bitcast_convert_type(p << 16, F32)
    hi = lax.bitcast_convert_type(p & jnp.uint32(0xFFFF0000), F32)
    return lo, hi


def _norm_matmul_kernel(x_ref, g_ref, *refs, act_from):
    w_refs = refs[:W_SPLIT]
    b_ref, o_ref, h_ref, wb_ref = refs[W_SPLIT:]
    kc = wb_ref.shape[0] // W_SPLIT
    j = pl.program_id(1)

    @pl.when(j == 0)
    def _():
        h_ref[...] = _rms(x_ref[...], g_ref[...]).astype(BF16)

    for c in range(W_SPLIT):
        wb_ref[c * kc:(c + 1) * kc, :] = w_refs[c][...].astype(BF16)
    acc = jnp.dot(h_ref[...], wb_ref[...], preferred_element_type=F32) + b_ref[...]

    @pl.when(j < act_from)
    def _():
        o_ref[...] = acc

    @pl.when(j >= act_from)
    def _():
        o_ref[...] = _sigmoid(acc)


def _norm_matmul(x, gain, w, bias, act_from_col, name):
    t, d = x.shape
    n = w.shape[1]
    tm = min(PROJ_TM, t)
    tn = PROJ_TN
    assert t % tm == 0 and n % tn == 0 and act_from_col % tn == 0
    return pl.pallas_call(
        functools.partial(_norm_matmul_kernel, act_from=act_from_col // tn),
        grid=(t // tm, n // tn),
        in_specs=[
            pl.BlockSpec((tm, d), lambda i, j: (i, 0)),
            pl.BlockSpec((1, d), lambda i, j: (0, 0)),
            *[pl.BlockSpec((d // W_SPLIT, tn), functools.partial(lambda c, i, j: (c, j), c))
              for c in range(W_SPLIT)],
            pl.BlockSpec((1, tn), lambda i, j: (0, j)),
        ],
        out_specs=pl.BlockSpec((tm, tn), lambda i, j: (i, j)),
        out_shape=jax.ShapeDtypeStruct((t, n), F32),
        scratch_shapes=[pltpu.VMEM((tm, d), BF16), pltpu.VMEM((d, tn), BF16)],
        compiler_params=pltpu.CompilerParams(
            dimension_semantics=("arbitrary", "arbitrary"), vmem_limit_bytes=VMEM_LIMIT),
        name=name,
    )(x, gain.reshape(1, d), *([w] * W_SPLIT), bias.reshape(1, n))


def _pool_project(pooled, wpg_ref, scale_ref, g):
    sl = slice(g * POOL_GROUP_DIM, (g + 1) * POOL_GROUP_DIM)
    y = jnp.dot(pooled.astype(BF16), wpg_ref[g].astype(BF16), preferred_element_type=F32)
    return y * scale_ref[:, sl]


def _softmax_rows(s):
    m = jnp.max(s, axis=-1, keepdims=True)
    e = jnp.exp(s - m)
    return e / jnp.sum(e, axis=-1, keepdims=True)


def _mix_prompt_kernel(cur_ref, prev_ref, mk_ref, mv_ref, wpg_ref, scale_ref, wconv_ref,
                       br_ref, zst_ref, extu_ref, extz_ref):
    t = pl.program_id(1)
    tm = cur_ref.shape[0]
    has_prev = t > 0

    u = cur_ref[:, C_U:C_U + D_POOL]
    extu_ref[0:HIST, :] = jnp.where(has_prev, prev_ref[:, C_U:C_U + D_POOL], 0.0)
    extu_ref[HIST:HIST + tm, :] = u
    pos = t * tm + lax.broadcasted_iota(jnp.int32, (tm, 1), 0)
    for g, w in enumerate(POOL_WINDOWS):
        sl = slice(g * POOL_GROUP_DIM, (g + 1) * POOL_GROUP_DIM)
        s = extu_ref[HIST:HIST + tm, sl]
        for k in range(1, w):
            s = s + extu_ref[HIST - k:HIST - k + tm, sl]
        cnt = jnp.minimum(w, pos + 1).astype(F32)
        pooled = s / cnt - extu_ref[HIST:HIST + tm, sl]
        br_ref[:, sl] = _pool_project(pooled, wpg_ref, scale_ref, g).astype(BF16)

    z = cur_ref[:, C_C:C_C + D_CONV] * cur_ref[:, C_V:C_V + D_CONV]
    zprev = prev_ref[:, C_C:C_C + D_CONV] * prev_ref[:, C_V:C_V + D_CONV]
    extz_ref[0:HIST, :] = jnp.where(has_prev, zprev, 0.0)
    extz_ref[HIST:HIST + tm, :] = z
    y = extz_ref[HIST - 2:HIST - 2 + tm, :] * wconv_ref[0:1, :]
    y = y + extz_ref[HIST - 1:HIST - 1 + tm, :] * wconv_ref[1:2, :]
    y = y + extz_ref[HIST:HIST + tm, :] * wconv_ref[2:3, :]
    br_ref[:, D_POOL:D_POOL + D_CONV] = (cur_ref[:, C_B:C_B + D_CONV] * y).astype(BF16)
    zst_ref[0] = extz_ref[HIST + tm - 8:HIST + tm, :]

    for h in range(N_XHEADS):
        sl = slice(h * XHEAD_DIM, (h + 1) * XHEAD_DIM)
        qh = cur_ref[:, C_Q + h * XHEAD_DIM:C_Q + (h + 1) * XHEAD_DIM].astype(BF16)
        kh = mk_ref[0, :, sl].astype(BF16)
        vh = mv_ref[0, :, sl].astype(BF16)
        s = lax.dot_general(qh, kh, (((1,), (1,)), ((), ())), preferred_element_type=F32)
        p = _softmax_rows(s * (XHEAD_DIM ** -0.5))
        o = jnp.dot(p.astype(BF16), vh, preferred_element_type=F32)
        c0 = D_POOL + D_CONV + h * XHEAD_DIM
        br_ref[:, c0:c0 + XHEAD_DIM] = o.astype(BF16)


def _mix_prompt(proj, mk, mv, wpg, scale, wconv, batch, seq):
    tm = MIX_TM
    nt = seq // tm
    assert seq % tm == 0 and tm % HIST == 0
    rpb = tm // HIST
    return pl.pallas_call(
        _mix_prompt_kernel,
        grid=(batch, nt),
        in_specs=[
            pl.BlockSpec((tm, D_MIX), lambda b, t: (b * nt + t, 0)),
            pl.BlockSpec((HIST, D_MIX), lambda b, t: (jnp.maximum((b * nt + t) * rpb - 1, 0), 0)),
            pl.BlockSpec((1, N_MEM, D_XATTN), lambda b, t: (b, 0, 0)),
            pl.BlockSpec((1, N_MEM, D_XATTN), lambda b, t: (b, 0, 0)),
            pl.BlockSpec((len(POOL_WINDOWS), POOL_GROUP_DIM, POOL_GROUP_DIM), lambda b, t: (0, 0, 0)),
            pl.BlockSpec((1, D_POOL), lambda b, t: (0, 0)),
            pl.BlockSpec((CONV_WIDTH, D_CONV), lambda b, t: (0, 0)),
        ],
        out_specs=[
            pl.BlockSpec((tm, D_MODEL), lambda b, t: (b * nt + t, 0)),
            pl.BlockSpec((1, 8, D_CONV), lambda b, t: (b, 0, 0)),
        ],
        out_shape=[
            jax.ShapeDtypeStruct((batch * seq, D_MODEL), BF16),
            jax.ShapeDtypeStruct((batch, 8, D_CONV), F32),
        ],
        scratch_shapes=[pltpu.VMEM((HIST + tm, D_POOL), F32), pltpu.VMEM((HIST + tm, D_CONV), F32)],
        compiler_params=pltpu.CompilerParams(
            dimension_semantics=("arbitrary", "arbitrary"), vmem_limit_bytes=VMEM_LIMIT),
        name="mix_prompt",
    )(proj, proj, mk, mv, wpg, scale.reshape(1, D_POOL), wconv)


def _mix_sample_kernel(cur_ref, sp_ref, sc_ref, k_ref, v_ref, wpg_ref, scale_ref, wconv_ref,
                       br_ref, zst_ref, extu_ref, extz_ref):
    ns, ln = zst_ref.shape[0], zst_ref.shape[1]
    rows = ns * ln

    extu_ref[:, HIST - POOL_STATE_LEN:HIST, :] = sp_ref[...]
    extu_ref[:, HIST:HIST + ln, :] = cur_ref[:, C_U:C_U + D_POOL].reshape(ns, ln, D_POOL)
    for g, w in enumerate(POOL_WINDOWS):
        sl = slice(g * POOL_GROUP_DIM, (g + 1) * POOL_GROUP_DIM)
        s = extu_ref[:, HIST:HIST + ln, sl]
        for k in range(1, w):
            s = s + extu_ref[:, HIST - k:HIST - k + ln, sl]
        pooled = s / float(w) - extu_ref[:, HIST:HIST + ln, sl]
        pooled = pooled.reshape(rows, POOL_GROUP_DIM)
        br_ref[:, sl] = _pool_project(pooled, wpg_ref, scale_ref, g).astype(BF16)

    z = cur_ref[:, C_C:C_C + D_CONV] * cur_ref[:, C_V:C_V + D_CONV]
    extz_ref[:, HIST - 2:HIST, :] = sc_ref[...]
    extz_ref[:, HIST:HIST + ln, :] = z.reshape(ns, ln, D_CONV)
    y = extz_ref[:, HIST - 2:HIST - 2 + ln, :] * wconv_ref[0:1, :]
    y = y + extz_ref[:, HIST - 1:HIST - 1 + ln, :] * wconv_ref[1:2, :]
    y = y + extz_ref[:, HIST:HIST + ln, :] * wconv_ref[2:3, :]
    br_ref[:, D_POOL:D_POOL + D_CONV] = (
        cur_ref[:, C_B:C_B + D_CONV] * y.reshape(rows, D_CONV)).astype(BF16)
    zst_ref[...] = extz_ref[:, HIST:HIST + ln, :]

    q3 = cur_ref[:, C_Q:C_Q + D_XATTN].reshape(ns, ln, D_XATTN)
    q4 = jnp.concatenate([q3[:, :, h * XHEAD_DIM:(h + 1) * XHEAD_DIM] for h in range(N_XHEADS)], axis=1)
    s = jnp.einsum("nqd,nkd->nqk", q4.astype(BF16), k_ref[...].astype(BF16),
                   preferred_element_type=F32)
    row_head = lax.broadcasted_iota(jnp.int32, (N_XHEADS * ln, N_XHEADS * N_MEM), 0) // ln
    col_head = lax.broadcasted_iota(jnp.int32, (N_XHEADS * ln, N_XHEADS * N_MEM), 1) % N_XHEADS
    s = jnp.where((row_head == col_head)[None], s * (XHEAD_DIM ** -0.5), -jnp.inf)
    p = _softmax_rows(s)
    r = jnp.einsum("nqk,nkd->nqd", p.astype(BF16), v_ref[...].astype(BF16),
                   preferred_element_type=F32)
    for h in range(N_XHEADS):
        c0 = D_POOL + D_CONV + h * XHEAD_DIM
        br_ref[:, c0:c0 + XHEAD_DIM] = r[:, h * ln:(h + 1) * ln, :].reshape(rows, XHEAD_DIM).astype(BF16)


def _mix_sample(proj, state_pool, state_conv, mem_k, mem_v, wpg, scale, wconv, nseq, ln):
    ns = MIX_NS
    rows = ns * ln
    assert nseq % ns == 0 and ln == 8
    return pl.pallas_call(
        _mix_sample_kernel,
        grid=(nseq // ns,),
        in_specs=[
            pl.BlockSpec((rows, D_MIX), lambda s: (s, 0)),
            pl.BlockSpec((ns, POOL_STATE_LEN, D_POOL), lambda s: (s, 0, 0)),
            pl.BlockSpec((ns, CONV_WIDTH - 1, D_CONV), lambda s: (s, 0, 0)),
            pl.BlockSpec((ns, N_MEM * N_XHEADS, XHEAD_DIM), lambda s: (s, 0, 0)),
            pl.BlockSpec((ns, N_MEM * N_XHEADS, XHEAD_DIM), lambda s: (s, 0, 0)),
            pl.BlockSpec((len(POOL_WINDOWS), POOL_GROUP_DIM, POOL_GROUP_DIM), lambda s: (0, 0, 0)),
            pl.BlockSpec((1, D_POOL), lambda s: (0, 0)),
            pl.BlockSpec((CONV_WIDTH, D_CONV), lambda s: (0, 0)),
        ],
        out_specs=[
            pl.BlockSpec((rows, D_MODEL), lambda s: (s, 0)),
            pl.BlockSpec((ns, ln, D_CONV), lambda s: (s, 0, 0)),
        ],
        out_shape=[
            jax.ShapeDtypeStruct((nseq * ln, D_MODEL), BF16),
            jax.ShapeDtypeStruct((nseq, ln, D_CONV), F32),
        ],
        scratch_shapes=[pltpu.VMEM((ns, HIST + ln, D_POOL), F32),
                        pltpu.VMEM((ns, HIST + ln, D_CONV), F32)],
        compiler_params=pltpu.CompilerParams(
            dimension_semantics=("arbitrary",), vmem_limit_bytes=VMEM_LIMIT),
        name="mix_sample",
    )(proj, state_pool, state_conv, mem_k, mem_v, wpg, scale.reshape(1, D_POOL), wconv)


def _merge_route_kernel(x_ref, br_ref, g0_ref, g1_ref, g2_ref, wpo_ref, wco_ref, wao_ref, wo_ref,
                        nf_ref, wr_ref, brt_ref,
                        x1_ref, h2_ref, ri_ref, rw_ref, cnt_ref, carry_ref):
    i = pl.program_id(0)
    tm = x_ref.shape[0]

    @pl.when(i == 0)
    def _():
        carry_ref[...] = jnp.zeros_like(carry_ref)

    merged = g0_ref[...] * jnp.dot(br_ref[:, 0:D_POOL], wpo_ref[...], preferred_element_type=F32)
    merged = merged + g1_ref[...] * jnp.dot(br_ref[:, D_POOL:D_POOL + D_CONV], wco_ref[...],
                                            preferred_element_type=F32)
    merged = merged + g2_ref[...] * jnp.dot(br_ref[:, D_POOL + D_CONV:D_MODEL], wao_ref[...],
                                            preferred_element_type=F32)
    x1 = x_ref[...] + jnp.dot(merged.astype(BF16), wo_ref[...], preferred_element_type=F32)
    x1_ref[...] = x1
    h2 = _rms(x1, nf_ref[...])
    h2_ref[...] = _pack_bf16_pairs(h2)

    logits = jnp.dot(h2, wr_ref[...], preferred_element_type=F32,
                     precision=lax.Precision.HIGHEST) + brt_ref[...]
    lane = lax.broadcasted_iota(jnp.int32, (tm, N_EXPERTS), 1).astype(F32)
    vals, idxs, hots = [], [], []
    work = logits
    for _ in range(TOP_K):
        m = jnp.max(work, axis=-1, keepdims=True)
        idx = jnp.min(jnp.where(work == m, lane, float(N_EXPERTS)), axis=-1, keepdims=True)
        hot = lane == idx
        work = jnp.where(hot, -jnp.inf, work)
        vals.append(m)
        idxs.append(idx)
        hots.append(hot)
    es = [jnp.exp(v - vals[0]) for v in vals]
    denom = es[0] + es[1] + es[2] + es[3]

    chosen = jnp.where(hots[0] | hots[1] | hots[2] | hots[3], 1.0, 0.0).astype(BF16)
    r_i = lax.broadcasted_iota(jnp.int32, (tm, tm), 0)
    c_i = lax.broadcasted_iota(jnp.int32, (tm, tm), 1)
    lower = jnp.where(c_i < r_i, 1.0, 0.0).astype(BF16)
    before = jnp.dot(lower, chosen, preferred_element_type=F32) + carry_ref[0:1, 0:N_EXPERTS]
    carry_ref[0:1, 0:N_EXPERTS] = (carry_ref[0:1, 0:N_EXPERTS]
                                   + jnp.sum(chosen.astype(F32), axis=0, keepdims=True))
    cnt_ref[...] = carry_ref[...]

    out_lane = lax.broadcasted_iota(jnp.int32, (tm, LANES), 1)
    ri = jnp.zeros((tm, LANES), jnp.int32)
    rw = jnp.zeros((tm, LANES), F32)
    for k in range(TOP_K):
        rank = jnp.sum(jnp.where(hots[k], before, 0.0), axis=-1, keepdims=True).astype(jnp.int32)
        ri = jnp.where(out_lane == k, idxs[k].astype(jnp.int32), ri)
        ri = jnp.where(out_lane == TOP_K + k, rank, ri)
        rw = jnp.where(out_lane == k, es[k] / denom, rw)
    ri_ref[...] = ri
    rw_ref[...] = rw


def _merge_route(x, branch, proj, wpo, wco, wao, wo, norm_ffn, w_router, b_router, name):
    t = x.shape[0]
    tm = MERGE_TM
    assert t % tm == 0
    gate_blk0 = D_MIX // D_MODEL
    assert D_MIX % D_MODEL == 0
    const = lambda i: (0, 0)
    return pl.pallas_call(
        _merge_route_kernel,
        grid=(t // tm,),
        in_specs=[
            pl.BlockSpec((tm, D_MODEL), lambda i: (i, 0)),
            pl.BlockSpec((tm, D_MODEL), lambda i: (i, 0)),
            pl.BlockSpec((tm, D_MODEL), lambda i: (i, gate_blk0)),
            pl.BlockSpec((tm, D_MODEL), lambda i: (i, gate_blk0 + 1)),
            pl.BlockSpec((tm, D_MODEL), lambda i: (i, gate_blk0 + 2)),
            pl.BlockSpec((D_POOL, D_MODEL), const, pipeline_mode=pl.Buffered(1)),
            pl.BlockSpec((D_CONV, D_MODEL), const, pipeline_mode=pl.Buffered(1)),
            pl.BlockSpec((D_XATTN, D_MODEL), const, pipeline_mode=pl.Buffered(1)),
            pl.BlockSpec((D_MODEL, D_MODEL), const, pipeline_mode=pl.Buffered(1)),
            pl.BlockSpec((1, D_MODEL), const),
            pl.BlockSpec((D_MODEL, N_EXPERTS), const),
            pl.BlockSpec((1, N_EXPERTS), const),
        ],
        out_specs=[
            pl.BlockSpec((tm, D_MODEL), lambda i: (i, 0)),
            pl.BlockSpec((tm, D_MODEL // 2), lambda i: (i, 0)),
            pl.BlockSpec((tm, LANES), lambda i: (i, 0)),
            pl.BlockSpec((tm, LANES), lambda i: (i, 0)),
            pl.BlockSpec((8, LANES), const),
        ],
        out_shape=[
            jax.ShapeDtypeStruct((t, D_MODEL), F32),
            jax.ShapeDtypeStruct((t, D_MODEL // 2), PACKED),
            jax.ShapeDtypeStruct((t, LANES), jnp.int32),
            jax.ShapeDtypeStruct((t, LANES), F32),
            jax.ShapeDtypeStruct((8, LANES), F32),
        ],
        scratch_shapes=[pltpu.VMEM((8, LANES), F32)],
        compiler_params=pltpu.CompilerParams(
            dimension_semantics=("arbitrary",), vmem_limit_bytes=VMEM_LIMIT),
        name=name,
    )(x, branch, proj, proj, proj, wpo, wco, wao, wo, norm_ffn.reshape(1, D_MODEL),
      w_router, b_router.reshape(1, N_EXPERTS))


def _sc_mesh():
    return plsc.VectorSubcoreMesh(core_axis_name="c", subcore_axis_name="s",
                                  num_cores=SC_CORES, num_subcores=SC_SUBCORES)


def _sc_worker_id():
    return lax.axis_index("s") * SC_CORES + lax.axis_index("c")


def _sc_dispatch(h2, dest, rows_out):
    t, w = h2.shape
    per_w = t // SC_WORKERS
    n_chunks = per_w // SC_CHUNK
    assert t == SC_WORKERS * n_chunks * SC_CHUNK
    idx = dest.reshape(SC_WORKERS, n_chunks, SC_CHUNK, TOP_K).transpose(0, 1, 3, 2)
    idx = idx.reshape(SC_WORKERS, n_chunks * TOP_K, SC_CHUNK)

    def body(h2_hbm, idx_hbm, xs_hbm, idx_v, rows_v, sems):
        base = _sc_worker_id() * per_w
        pltpu.sync_copy(idx_hbm.at[_sc_worker_id()], idx_v)
        pending = [[], []]
        for c in range(n_chunks):
            b = c % 2
            for d in pending[b]:
                d.wait()
            pltpu.sync_copy(h2_hbm.at[pl.ds(base + c * SC_CHUNK, SC_CHUNK)], rows_v.at[b])
            pending[b] = [pltpu.async_copy(rows_v.at[b], xs_hbm.at[idx_v.at[c * TOP_K + k]], sems.at[b])
                          for k in range(TOP_K)]
        for b in range(2):
            for d in pending[b]:
                d.wait()

    return pl.kernel(
        body,
        out_type=jax.ShapeDtypeStruct((rows_out, w), h2.dtype),
        mesh=_sc_mesh(),
        scratch_types=[pltpu.VMEM((n_chunks * TOP_K, SC_CHUNK), jnp.int32),
                       pltpu.VMEM((2, SC_CHUNK, w), h2.dtype),
                       pltpu.SemaphoreType.DMA((2,))],
        name="sc_dispatch",
    )(h2, idx)


def _sc_gather_rows(table, idx):
    n = idx.shape[0]
    w = table.shape[1]
    per_w = n // SC_WORKERS
    n_chunks = per_w // SC_CHUNK
    assert n == SC_WORKERS * n_chunks * SC_CHUNK
    idx3 = idx.reshape(SC_WORKERS, n_chunks, SC_CHUNK)

    def body(table_hbm, idx_hbm, out_hbm, idx_v, rows_v, gsem, wsems):
        base = _sc_worker_id() * per_w
        pltpu.sync_copy(idx_hbm.at[_sc_worker_id()], idx_v)
        pending = [None, None]
        for c in range(n_chunks):
            b = c % 2
            if pending[b] is not None:
                pending[b].wait()
            pltpu.async_copy(table_hbm.at[idx_v.at[c]], rows_v.at[b], gsem).wait()
            pending[b] = pltpu.async_copy(
                rows_v.at[b], out_hbm.at[pl.ds(base + c * SC_CHUNK, SC_CHUNK)], wsems.at[b])
        for b in range(2):
            if pending[b] is not None:
                pending[b].wait()

    return pl.kernel(
        body,
        out_type=jax.ShapeDtypeStruct((n, w), table.dtype),
        mesh=_sc_mesh(),
        scratch_types=[pltpu.VMEM((n_chunks, SC_CHUNK), jnp.int32),
                       pltpu.VMEM((2, SC_CHUNK, w), table.dtype),
                       pltpu.SemaphoreType.DMA,
                       pltpu.SemaphoreType.DMA((2,))],
        name="sc_gather_rows",
    )(table, idx3)


def _expert_row_loop(n, in_copy, out_copy, prepare_weights, compute):
    in_copy(0, 0).start()
    prepare_weights()

    def body(b, carry):
        slot = lax.rem(b, 2)
        in_copy(b, slot).wait()

        @pl.when(b + 1 < n)
        def _():
            in_copy(b + 1, 1 - slot).start()

        @pl.when(b >= 2)
        def _():
            out_copy(b - 2, slot).wait()

        compute(slot)
        out_copy(b, slot).start()
        return carry

    lax.fori_loop(0, n, body, 0)

    @pl.when(n >= 2)
    def _():
        out_copy(n - 2, lax.rem(n, 2)).wait()

    out_copy(n - 1, lax.rem(n - 1, 2)).wait()


def _expert_in_kernel(row0_ref, nblk_ref, xs_hbm, *refs):
    wg_refs, wu_refs = refs[:W_SPLIT], refs[W_SPLIT:2 * W_SPLIT]
    bg_ref, bu_ref, act_hbm, wgb_ref, wub_ref, x_buf, a_buf, x_sem, a_sem = refs[2 * W_SPLIT:]
    kc = D_MODEL // W_SPLIT
    j = pl.program_id(0)
    e = pl.program_id(1)
    n = nblk_ref[e]
    tm, tn = a_buf.shape[1], a_buf.shape[2]

    def rows(b):
        return pl.ds(pl.multiple_of(row0_ref[e] + b * tm, tm), tm)

    def x_copy(b, slot):
        return pltpu.make_async_copy(xs_hbm.at[rows(b)], x_buf.at[slot], x_sem.at[slot])

    def a_copy(b, slot):
        return pltpu.make_async_copy(
            a_buf.at[slot], act_hbm.at[rows(b), pl.ds(pl.multiple_of(j * tn, tn), tn)], a_sem.at[slot])

    def compute(slot):
        x = jnp.concatenate(_unpack_bf16_pairs(x_buf[slot]), axis=1).astype(BF16)
        g = jnp.dot(x, wgb_ref[...], preferred_element_type=F32) + bg_ref[0]
        up = jnp.dot(x, wub_ref[...], preferred_element_type=F32) + bu_ref[0]
        g = jnp.minimum(g, SWIGLU_LIMIT)
        up = jnp.clip(up, -SWIGLU_LIMIT, SWIGLU_LIMIT)
        a_buf[slot] = (g * _sigmoid(SWIGLU_ALPHA * g) * (up + 1.0)).astype(BF16)

    def prepare_weights():
        for c in range(W_SPLIT):
            wgb_ref[c * kc:(c + 1) * kc, :] = wg_refs[c][0].astype(BF16)
            wub_ref[c * kc:(c + 1) * kc, :] = wu_refs[c][0].astype(BF16)

    @pl.when(n > 0)
    def _():
        _expert_row_loop(n, x_copy, a_copy, prepare_weights, compute)


def _expert_in(row0, nblk, xs, w_in, b_in):
    rows = xs.shape[0]
    tm, tn = EXP_TM, EXP_IN_TN
    nj = D_FF // tn
    kc = D_MODEL // W_SPLIT

    def w_spec(c, col0):
        return pl.BlockSpec((1, kc, tn), lambda j, e, r0, nb: (e, c, col0 + j))

    grid_spec = pltpu.PrefetchScalarGridSpec(
        num_scalar_prefetch=2,
        grid=(nj, N_EXPERTS),
        in_specs=[
            pl.BlockSpec(memory_space=pl.ANY),
            *[w_spec(c, 0) for c in range(W_SPLIT)],
            *[w_spec(c, nj) for c in range(W_SPLIT)],
            pl.BlockSpec((1, 1, tn), lambda j, e, r0, nb: (e, 0, j)),
            pl.BlockSpec((1, 1, tn), lambda j, e, r0, nb: (e, 0, nj + j)),
        ],
        out_specs=pl.BlockSpec(memory_space=pl.ANY),
        scratch_shapes=[pltpu.VMEM((D_MODEL, tn), BF16), pltpu.VMEM((D_MODEL, tn), BF16),
                        pltpu.VMEM((2, tm, D_MODEL // 2), PACKED), pltpu.VMEM((2, tm, tn), BF16),
                        pltpu.SemaphoreType.DMA((2,)), pltpu.SemaphoreType.DMA((2,))],
    )
    return pl.pallas_call(
        _expert_in_kernel,
        grid_spec=grid_spec,
        out_shape=jax.ShapeDtypeStruct((rows, D_FF), BF16),
        compiler_params=pltpu.CompilerParams(
            dimension_semantics=("arbitrary", "arbitrary"), vmem_limit_bytes=VMEM_LIMIT),
        name="expert_in",
    )(row0, nblk, xs, *([w_in] * (2 * W_SPLIT)), b_in, b_in)


def _expert_out_kernel(row0_ref, nblk_ref, act_hbm, *refs):
    w_refs = refs[:W_SPLIT]
    b_ref, ys_hbm, wb_ref, a_buf, y_buf, a_sem, y_sem = refs[W_SPLIT:]
    kc = D_FF // W_SPLIT
    j = pl.program_id(0)
    e = pl.program_id(1)
    n = nblk_ref[e]
    tm, tnp = y_buf.shape[1], y_buf.shape[2]

    def rows(b):
        return pl.ds(pl.multiple_of(row0_ref[e] + b * tm, tm), tm)

    def a_copy(b, slot):
        return pltpu.make_async_copy(act_hbm.at[rows(b)], a_buf.at[slot], a_sem.at[slot])

    def y_copy(b, slot):
        return pltpu.make_async_copy(
            y_buf.at[slot], ys_hbm.at[rows(b), pl.ds(pl.multiple_of(j * tnp, tnp), tnp)], y_sem.at[slot])

    def prepare_weights():
        for c in range(W_SPLIT):
            wb_ref[c * kc:(c + 1) * kc, :] = w_refs[c][0].astype(BF16)

    def compute(slot):
        y = jnp.dot(a_buf[slot], wb_ref[...], preferred_element_type=F32) + b_ref[0]
        y_buf[slot] = _pack_bf16_pairs(y)

    @pl.when(n > 0)
    def _():
        _expert_row_loop(n, a_copy, y_copy, prepare_weights, compute)


def _expert_out(row0, nblk, act, w_out, b_out):
    rows = act.shape[0]
    tm, tn = EXP_TM, EXP_OUT_TN
    nj = D_MODEL // tn
    grid_spec = pltpu.PrefetchScalarGridSpec(
        num_scalar_prefetch=2,
        grid=(nj, N_EXPERTS),
        in_specs=[
            pl.BlockSpec(memory_space=pl.ANY),
            *[pl.BlockSpec((1, D_FF // W_SPLIT, tn), functools.partial(lambda c, j, e, r0, nb: (e, c, j), c))
              for c in range(W_SPLIT)],
            pl.BlockSpec((1, 1, tn), lambda j, e, r0, nb: (e, 0, j)),
        ],
        out_specs=pl.BlockSpec(memory_space=pl.ANY),
        scratch_shapes=[pltpu.VMEM((D_FF, tn), BF16),
                        pltpu.VMEM((2, tm, D_FF), BF16), pltpu.VMEM((2, tm, tn // 2), PACKED),
                        pltpu.SemaphoreType.DMA((2,)), pltpu.SemaphoreType.DMA((2,))],
    )
    return pl.pallas_call(
        _expert_out_kernel,
        grid_spec=grid_spec,
        out_shape=jax.ShapeDtypeStruct((rows, D_MODEL // 2), PACKED),
        compiler_params=pltpu.CompilerParams(
            dimension_semantics=("arbitrary", "arbitrary"), vmem_limit_bytes=VMEM_LIMIT),
        name="expert_out",
    )(row0, nblk, act, *([w_out] * W_SPLIT), b_out)


def _combine_kernel(x1_ref, yg_ref, rw_ref, g_ref, o_ref):
    half = EXP_OUT_TN // 2
    pieces = []
    ssq = jnp.zeros((x1_ref.shape[0], 1), F32)
    for jt in range(D_MODEL // EXP_OUT_TN):
        acc_lo = x1_ref[:, jt * EXP_OUT_TN:jt * EXP_OUT_TN + half]
        acc_hi = x1_ref[:, jt * EXP_OUT_TN + half:(jt + 1) * EXP_OUT_TN]
        moe_lo = jnp.zeros_like(acc_lo)
        moe_hi = jnp.zeros_like(acc_hi)
        for k in range(TOP_K):
            lo, hi = _unpack_bf16_pairs(yg_ref[k, :, jt * half:(jt + 1) * half])
            moe_lo = moe_lo + lo * rw_ref[:, k:k + 1]
            moe_hi = moe_hi + hi * rw_ref[:, k:k + 1]
        for acc in (acc_lo + moe_lo, acc_hi + moe_hi):
            ssq = ssq + jnp.sum(acc * acc, axis=-1, keepdims=True)
            pieces.append(acc)
    inv = lax.rsqrt(ssq / D_MODEL + EPS)
    for n, acc in enumerate(pieces):
        o_ref[:, n * half:(n + 1) * half] = acc * inv * g_ref[:, n * half:(n + 1) * half]


def _combine(x1, yg, rw, gain, row_off, name):
    t = x1.shape[0]
    tm = COMB_TM
    assert t % tm == 0 and row_off % tm == 0
    off = row_off // tm
    return pl.pallas_call(
        _combine_kernel,
        grid=(t // tm,),
        in_specs=[
            pl.BlockSpec((tm, D_MODEL), lambda i: (i, 0)),
            pl.BlockSpec((TOP_K, tm, D_MODEL // 2), lambda i: (0, i + off, 0)),
            pl.BlockSpec((tm, LANES), lambda i: (i, 0)),
            pl.BlockSpec((1, D_MODEL), lambda i: (0, 0)),
        ],
        out_specs=pl.BlockSpec((tm, D_MODEL), lambda i: (i, 0)),
        out_shape=jax.ShapeDtypeStruct((t, D_MODEL), F32),
        compiler_params=pltpu.CompilerParams(
            dimension_semantics=("arbitrary",), vmem_limit_bytes=VMEM_LIMIT),
        name=name,
    )(x1, yg, rw, gain.reshape(1, D_MODEL))


def kernel(x_prompt, x_sample, state_pool, state_conv, cache_mem_k, cache_mem_v, mem_prompt,
           norm_mix, w_in, b_gate, w_pool_group, pool_scale, w_conv, mem_norm, w_mem_kv,
           w_pool_out, w_conv_out, w_attn_out, w_o, norm_ffn, w_router, b_router,
           w_exp_in, b_exp_in, w_exp_out, b_exp_out, final_norm):
    depth = norm_mix.shape[0]
    assert depth == 1
    l = 0
    bp, seq, _ = x_prompt.shape
    bs, ln, _ = x_sample.shape
    tp, ts = bp * seq, bs * ln

    kv = _norm_matmul(mem_prompt.reshape(bp * N_MEM, D_MODEL), mem_norm[l], w_mem_kv[l],
                      jnp.zeros((2 * D_XATTN,), F32), 2 * D_XATTN, "mem_kv")
    mk = kv[:, :D_XATTN].reshape(bp, N_MEM, D_XATTN)
    mv = kv[:, D_XATTN:].reshape(bp, N_MEM, D_XATTN)

    bias_in = jnp.concatenate([jnp.zeros((D_MIX,), F32), b_gate[l]])
    xp = x_prompt.reshape(tp, D_MODEL)
    xs_ = x_sample.reshape(ts, D_MODEL)
    proj_p = _norm_matmul(xp, norm_mix[l], w_in[l], bias_in, D_MIX, "proj_prompt")
    proj_s = _norm_matmul(xs_, norm_mix[l], w_in[l], bias_in, D_MIX, "proj_sample")

    br_p, zst_p = _mix_prompt(proj_p, mk, mv, w_pool_group[l], pool_scale[l], w_conv[l], bp, seq)
    br_s, zst_s = _mix_sample(proj_s, state_pool[l], state_conv[l],
                              cache_mem_k[l].reshape(bs, N_MEM * N_XHEADS, XHEAD_DIM),
                              cache_mem_v[l].reshape(bs, N_MEM * N_XHEADS, XHEAD_DIM),
                              w_pool_group[l], pool_scale[l], w_conv[l], bs, ln)

    wpo, wco, wao, wo = (w.astype(BF16) for w in (w_pool_out[l], w_conv_out[l], w_attn_out[l], w_o[l]))
    x1_p, h2_p, ri_p, rw_p, cnt_p = _merge_route(xp, br_p, proj_p, wpo, wco, wao, wo, norm_ffn[l],
                                                  w_router[l], b_router[l], "merge_route_prompt")
    x1_s, h2_s, ri_s, rw_s, cnt_s = _merge_route(xs_, br_s, proj_s, wpo, wco, wao, wo, norm_ffn[l],
                                                  w_router[l], b_router[l], "merge_route_sample")

    t_all = tp + ts
    n_assign = t_all * TOP_K
    nb_max = n_assign // EXP_TM + N_EXPERTS
    size_p = cnt_p[0, :N_EXPERTS].astype(jnp.int32)
    size_s = cnt_s[0, :N_EXPERTS].astype(jnp.int32)
    nblk = (size_p + size_s + EXP_TM - 1) // EXP_TM
    blk_end = jnp.cumsum(nblk)
    pad_start = (blk_end - nblk) * EXP_TM

    e_p, r_p = ri_p[:, :TOP_K], ri_p[:, TOP_K:2 * TOP_K]
    e_s, r_s = ri_s[:, :TOP_K], ri_s[:, TOP_K:2 * TOP_K]
    dest_p = pad_start[e_p] + r_p
    dest_s = pad_start[e_s] + size_p[e_s] + r_s
    dest = jnp.concatenate([dest_p, dest_s], axis=0)
    h2 = jnp.concatenate([h2_p, h2_s], axis=0)
    xs_sorted = _sc_dispatch(h2, dest, nb_max * EXP_TM)

    act = _expert_in(pad_start, nblk, xs_sorted, w_exp_in[l], b_exp_in[l].reshape(N_EXPERTS, 1, 2 * D_FF))
    ys = _expert_out(pad_start, nblk, act, w_exp_out[l], b_exp_out[l].reshape(N_EXPERTS, 1, D_MODEL))
    yg = _sc_gather_rows(ys, dest.T.reshape(-1)).reshape(TOP_K, t_all, D_MODEL // 2)

    y_p = _combine(x1_p, yg, rw_p, final_norm, 0, "combine_prompt")
    y_s = _combine(x1_s, yg, rw_s, final_norm, tp, "combine_sample")

    new_pool_p = proj_p[:, :D_POOL].reshape(bp, seq, D_POOL)[:, seq - POOL_STATE_LEN:]
    new_conv_p = zst_p[:, 8 - (CONV_WIDTH - 1):]
    u_s = proj_s[:, :D_POOL].reshape(bs, ln, D_POOL)
    new_pool_s = jnp.concatenate([state_pool[l], u_s], axis=1)[:, -POOL_STATE_LEN:]
    new_conv_s = zst_s[:, ln - (CONV_WIDTH - 1):]

    return (y_p.reshape(bp, seq, D_MODEL), y_s.reshape(bs, ln, D_MODEL),
            new_pool_p[None], new_conv_p[None],
            mk.reshape(1, bp, N_MEM, N_XHEADS, XHEAD_DIM), mv.reshape(1, bp, N_MEM, N_XHEADS, XHEAD_DIM),
            new_pool_s[None], new_conv_s[None])
```

```python
import functools

import jax
import jax.numpy as jnp
from jax import lax
from jax.experimental import pallas as pl
from jax.experimental.pallas import tpu as pltpu
from jax.experimental.pallas import tpu_sc as plsc

F32 = jnp.float32
BF16 = jnp.bfloat16
PACKED = jnp.int32

D_MODEL = 2048
POOL_WINDOWS = (2, 4, 8, 16)
POOL_GROUP_DIM = 128
D_POOL = 512
POOL_STATE_LEN = 15
D_CONV = 1024
CONV_WIDTH = 3
N_MEM = 256
N_XHEADS = 4
XHEAD_DIM = 128
D_XATTN = 512
N_BRANCH = 3
D_MIX = D_POOL + 3 * D_CONV + D_XATTN
D_IN_TOTAL = D_MIX + N_BRANCH * D_MODEL
N_EXPERTS = 32
TOP_K = 4
D_FF = D_MODEL
SWIGLU_LIMIT = 7.0
SWIGLU_ALPHA = 1.702
EPS = 1e-5

C_U = 0
C_V = D_POOL
C_B = D_POOL + D_CONV
C_C = D_POOL + 2 * D_CONV
C_Q = D_POOL + 3 * D_CONV

LANES = 128
HIST = 16

PROJ_TM = 1024
PROJ_TN = 1024
MIX_TM = 256
MIX_NS = 8
MERGE_TM = 256
EXP_TM = 256
EXP_CHUNK = 4
EXP_IN_TN = 512
EXP_OUT_TN = 1024
W_SPLIT = 4
COMB_TM = 256
SC_CORES = 2
SC_SUBCORES = 16
SC_WORKERS = SC_CORES * SC_SUBCORES
SC_CHUNK = 32
VMEM_LIMIT = 56 * 1024 * 1024


def _sigmoid(x):
    return 1.0 / (1.0 + jnp.exp(-x))


def _rms(x, g):
    ms = jnp.mean(x * x, axis=-1, keepdims=True)
    return x * lax.rsqrt(ms + EPS) * g


def _pack_bf16_pairs(x):
    n = x.shape[1] // 2
    lo = lax.bitcast_convert_type(x[:, :n].astype(BF16).astype(F32), jnp.uint32)
    hi = lax.bitcast_convert_type(x[:, n:].astype(BF16).astype(F32), jnp.uint32)
    return lax.bitcast_convert_type((hi & jnp.uint32(0xFFFF0000)) | (lo >> 16), PACKED)


def _unpack_bf16_pairs(p):
    p = lax.bitcast_convert_type(p, jnp.uint32)
    lo = lax.bitcast_convert_type(p << 16, F32)
    hi = lax.bitcast_convert_type(p & jnp.uint32(0xFFFF0000), F32)
    return lo, hi


def _norm_matmul_kernel(x_ref, g_ref, *refs, act_from):
    w_refs = refs[:W_SPLIT]
    b_ref, o_ref, h_ref, wb_ref = refs[W_SPLIT:]
    kc = wb_ref.shape[0] // W_SPLIT
    j = pl.program_id(1)

    @pl.when(j == 0)
    def _():
        h_ref[...] = _rms(x_ref[...], g_ref[...]).astype(BF16)

    for c in range(W_SPLIT):
        wb_ref[c * kc:(c + 1) * kc, :] = w_refs[c][...].astype(BF16)
    acc = jnp.dot(h_ref[...], wb_ref[...], preferred_element_type=F32) + b_ref[...]

    @pl.when(j < act_from)
    def _():
        o_ref[...] = acc

    @pl.when(j >= act_from)
    def _():
        o_ref[...] = _sigmoid(acc)


def _norm_matmul(x, gain, w, bias, act_from_col, name):
    t, d = x.shape
    n = w.shape[1]
    tm = min(PROJ_TM, t)
    tn = PROJ_TN
    assert t % tm == 0 and n % tn == 0 and act_from_col % tn == 0
    return pl.pallas_call(
        functools.partial(_norm_matmul_kernel, act_from=act_from_col // tn),
        grid=(t // tm, n // tn),
        in_specs=[
            pl.BlockSpec((tm, d), lambda i, j: (i, 0)),
            pl.BlockSpec((1, d), lambda i, j: (0, 0)),
            *[pl.BlockSpec((d // W_SPLIT, tn), functools.partial(lambda c, i, j: (c, j), c))
              for c in range(W_SPLIT)],
            pl.BlockSpec((1, tn), lambda i, j: (0, j)),
        ],
        out_specs=pl.BlockSpec((tm, tn), lambda i, j: (i, j)),
        out_shape=jax.ShapeDtypeStruct((t, n), F32),
        scratch_shapes=[pltpu.VMEM((tm, d), BF16), pltpu.VMEM((d, tn), BF16)],
        compiler_params=pltpu.CompilerParams(
            dimension_semantics=("arbitrary", "arbitrary"), vmem_limit_bytes=VMEM_LIMIT),
        name=name,
    )(x, gain.reshape(1, d), *([w] * W_SPLIT), bias.reshape(1, n))


def _pool_project(pooled, wpg_ref, scale_ref, g):
    sl = slice(g * POOL_GROUP_DIM, (g + 1) * POOL_GROUP_DIM)
    y = jnp.dot(pooled.astype(BF16), wpg_ref[g].astype(BF16), preferred_element_type=F32)
    return y * scale_ref[:, sl]


def _softmax_rows(s):
    m = jnp.max(s, axis=-1, keepdims=True)
    e = jnp.exp(s - m)
    return e / jnp.sum(e, axis=-1, keepdims=True)


def _mix_prompt_kernel(cur_ref, prev_ref, mk_ref, mv_ref, wpg_ref, scale_ref, wconv_ref,
                       br_ref, zst_ref, extu_ref, extz_ref):
    t = pl.program_id(1)
    tm = cur_ref.shape[0]
    has_prev = t > 0

    u = cur_ref[:, C_U:C_U + D_POOL]
    extu_ref[0:HIST, :] = jnp.where(has_prev, prev_ref[:, C_U:C_U + D_POOL], 0.0)
    extu_ref[HIST:HIST + tm, :] = u
    pos = t * tm + lax.broadcasted_iota(jnp.int32, (tm, 1), 0)
    for g, w in enumerate(POOL_WINDOWS):
        sl = slice(g * POOL_GROUP_DIM, (g + 1) * POOL_GROUP_DIM)
        s = extu_ref[HIST:HIST + tm, sl]
        for k in range(1, w):
            s = s + extu_ref[HIST - k:HIST - k + tm, sl]
        cnt = jnp.minimum(w, pos + 1).astype(F32)
        pooled = s / cnt - extu_ref[HIST:HIST + tm, sl]
        br_ref[:, sl] = _pool_project(pooled, wpg_ref, scale_ref, g).astype(BF16)

    z = cur_ref[:, C_C:C_C + D_CONV] * cur_ref[:, C_V:C_V + D_CONV]
    zprev = prev_ref[:, C_C:C_C + D_CONV] * prev_ref[:, C_V:C_V + D_CONV]
    extz_ref[0:HIST, :] = jnp.where(has_prev, zprev, 0.0)
    extz_ref[HIST:HIST + tm, :] = z
    y = extz_ref[HIST - 2:HIST - 2 + tm, :] * wconv_ref[0:1, :]
    y = y + extz_ref[HIST - 1:HIST - 1 + tm, :] * wconv_ref[1:2, :]
    y = y + extz_ref[HIST:HIST + tm, :] * wconv_ref[2:3, :]
    br_ref[:, D_POOL:D_POOL + D_CONV] = (cur_ref[:, C_B:C_B + D_CONV] * y).astype(BF16)
    zst_ref[0] = extz_ref[HIST + tm - 8:HIST + tm, :]

    for h in range(N_XHEADS):
        sl = slice(h * XHEAD_DIM, (h + 1) * XHEAD_DIM)
        qh = cur_ref[:, C_Q + h * XHEAD_DIM:C_Q + (h + 1) * XHEAD_DIM].astype(BF16)
        kh = mk_ref[0, :, sl].astype(BF16)
        vh = mv_ref[0, :, sl].astype(BF16)
        s = lax.dot_general(qh, kh, (((1,), (1,)), ((), ())), preferred_element_type=F32)
        p = _softmax_rows(s * (XHEAD_DIM ** -0.5))
        o = jnp.dot(p.astype(BF16), vh, preferred_element_type=F32)
        c0 = D_POOL + D_CONV + h * XHEAD_DIM
        br_ref[:, c0:c0 + XHEAD_DIM] = o.astype(BF16)


def _mix_prompt(proj, mk, mv, wpg, scale, wconv, batch, seq):
    tm = MIX_TM
    nt = seq // tm
    assert seq % tm == 0 and tm % HIST == 0
    rpb = tm // HIST
    return pl.pallas_call(
        _mix_prompt_kernel,
        grid=(batch, nt),
        in_specs=[
            pl.BlockSpec((tm, D_MIX), lambda b, t: (b * nt + t, 0)),
            pl.BlockSpec((HIST, D_MIX), lambda b, t: (jnp.maximum((b * nt + t) * rpb - 1, 0), 0)),
            pl.BlockSpec((1, N_MEM, D_XATTN), lambda b, t: (b, 0, 0)),
            pl.BlockSpec((1, N_MEM, D_XATTN), lambda b, t: (b, 0, 0)),
            pl.BlockSpec((len(POOL_WINDOWS), POOL_GROUP_DIM, POOL_GROUP_DIM), lambda b, t: (0, 0, 0)),
            pl.BlockSpec((1, D_POOL), lambda b, t: (0, 0)),
            pl.BlockSpec((CONV_WIDTH, D_CONV), lambda b, t: (0, 0)),
        ],
        out_specs=[
            pl.BlockSpec((tm, D_MODEL), lambda b, t: (b * nt + t, 0)),
            pl.BlockSpec((1, 8, D_CONV), lambda b, t: (b, 0, 0)),
        ],
        out_shape=[
            jax.ShapeDtypeStruct((batch * seq, D_MODEL), BF16),
            jax.ShapeDtypeStruct((batch, 8, D_CONV), F32),
        ],
        scratch_shapes=[pltpu.VMEM((HIST + tm, D_POOL), F32), pltpu.VMEM((HIST + tm, D_CONV), F32)],
        compiler_params=pltpu.CompilerParams(
            dimension_semantics=("arbitrary", "arbitrary"), vmem_limit_bytes=VMEM_LIMIT),
        name="mix_prompt",
    )(proj, proj, mk, mv, wpg, scale.reshape(1, D_POOL), wconv)


def _mix_sample_kernel(cur_ref, sp_ref, sc_ref, k_ref, v_ref, wpg_ref, scale_ref, wconv_ref,
                       br_ref, zst_ref, extu_ref, extz_ref):
    ns, ln = zst_ref.shape[0], zst_ref.shape[1]
    rows = ns * ln

    extu_ref[:, HIST - POOL_STATE_LEN:HIST, :] = sp_ref[...]
    extu_ref[:, HIST:HIST + ln, :] = cur_ref[:, C_U:C_U + D_POOL].reshape(ns, ln, D_POOL)
    for g, w in enumerate(POOL_WINDOWS):
        sl = slice(g * POOL_GROUP_DIM, (g + 1) * POOL_GROUP_DIM)
        s = extu_ref[:, HIST:HIST + ln, sl]
        for k in range(1, w):
            s = s + extu_ref[:, HIST - k:HIST - k + ln, sl]
        pooled = s / float(w) - extu_ref[:, HIST:HIST + ln, sl]
        pooled = pooled.reshape(rows, POOL_GROUP_DIM)
        br_ref[:, sl] = _pool_project(pooled, wpg_ref, scale_ref, g).astype(BF16)

    z = cur_ref[:, C_C:C_C + D_CONV] * cur_ref[:, C_V:C_V + D_CONV]
    extz_ref[:, HIST - 2:HIST, :] = sc_ref[...]
    extz_ref[:, HIST:HIST + ln, :] = z.reshape(ns, ln, D_CONV)
    y = extz_ref[:, HIST - 2:HIST - 2 + ln, :] * wconv_ref[0:1, :]
    y = y + extz_ref[:, HIST - 1:HIST - 1 + ln, :] * wconv_ref[1:2, :]
    y = y + extz_ref[:, HIST:HIST + ln, :] * wconv_ref[2:3, :]
    br_ref[:, D_POOL:D_POOL + D_CONV] = (
        cur_ref[:, C_B:C_B + D_CONV] * y.reshape(rows, D_CONV)).astype(BF16)
    zst_ref[...] = extz_ref[:, HIST:HIST + ln, :]

    q3 = cur_ref[:, C_Q:C_Q + D_XATTN].reshape(ns, ln, D_XATTN)
    q4 = jnp.concatenate([q3[:, :, h * XHEAD_DIM:(h + 1) * XHEAD_DIM] for h in range(N_XHEADS)], axis=1)
    s = jnp.einsum("nqd,nkd->nqk", q4.astype(BF16), k_ref[...].astype(BF16),
                   preferred_element_type=F32)
    row_head = lax.broadcasted_iota(jnp.int32, (N_XHEADS * ln, N_XHEADS * N_MEM), 0) // ln
    col_head = lax.broadcasted_iota(jnp.int32, (N_XHEADS * ln, N_XHEADS * N_MEM), 1) % N_XHEADS
    s = jnp.where((row_head == col_head)[None], s * (XHEAD_DIM ** -0.5), -jnp.inf)
    p = _softmax_rows(s)
    r = jnp.einsum("nqk,nkd->nqd", p.astype(BF16), v_ref[...].astype(BF16),
                   preferred_element_type=F32)
    for h in range(N_XHEADS):
        c0 = D_POOL + D_CONV + h * XHEAD_DIM
        br_ref[:, c0:c0 + XHEAD_DIM] = r[:, h * ln:(h + 1) * ln, :].reshape(rows, XHEAD_DIM).astype(BF16)


def _mix_sample(proj, state_pool, state_conv, mem_k, mem_v, wpg, scale, wconv, nseq, ln):
    ns = MIX_NS
    rows = ns * ln
    assert nseq % ns == 0 and ln == 8
    return pl.pallas_call(
        _mix_sample_kernel,
        grid=(nseq // ns,),
        in_specs=[
            pl.BlockSpec((rows, D_MIX), lambda s: (s, 0)),
            pl.BlockSpec((ns, POOL_STATE_LEN, D_POOL), lambda s: (s, 0, 0)),
            pl.BlockSpec((ns, CONV_WIDTH - 1, D_CONV), lambda s: (s, 0, 0)),
            pl.BlockSpec((ns, N_MEM * N_XHEADS, XHEAD_DIM), lambda s: (s, 0, 0)),
            pl.BlockSpec((ns, N_MEM * N_XHEADS, XHEAD_DIM), lambda s: (s, 0, 0)),
            pl.BlockSpec((len(POOL_WINDOWS), POOL_GROUP_DIM, POOL_GROUP_DIM), lambda s: (0, 0, 0)),
            pl.BlockSpec((1, D_POOL), lambda s: (0, 0)),
            pl.BlockSpec((CONV_WIDTH, D_CONV), lambda s: (0, 0)),
        ],
        out_specs=[
            pl.BlockSpec((rows, D_MODEL), lambda s: (s, 0)),
            pl.BlockSpec((ns, ln, D_CONV), lambda s: (s, 0, 0)),
        ],
        out_shape=[
            jax.ShapeDtypeStruct((nseq * ln, D_MODEL), BF16),
            jax.ShapeDtypeStruct((nseq, ln, D_CONV), F32),
        ],
        scratch_shapes=[pltpu.VMEM((ns, HIST + ln, D_POOL), F32),
                        pltpu.VMEM((ns, HIST + ln, D_CONV), F32)],
        compiler_params=pltpu.CompilerParams(
            dimension_semantics=("arbitrary",), vmem_limit_bytes=VMEM_LIMIT),
        name="mix_sample",
    )(proj, state_pool, state_conv, mem_k, mem_v, wpg, scale.reshape(1, D_POOL), wconv)


def _merge_route_kernel(x_ref, br_ref, g0_ref, g1_ref, g2_ref, wpo_ref, wco_ref, wao_ref, wo_ref,
                        nf_ref, wr_ref, brt_ref,
                        x1_ref, h2_ref, ri_ref, rw_ref, cnt_ref, carry_ref):
    i = pl.program_id(0)
    tm = x_ref.shape[0]

    @pl.when(i == 0)
    def _():
        carry_ref[...] = jnp.zeros_like(carry_ref)

    merged = g0_ref[...] * jnp.dot(br_ref[:, 0:D_POOL], wpo_ref[...], preferred_element_type=F32)
    merged = merged + g1_ref[...] * jnp.dot(br_ref[:, D_POOL:D_POOL + D_CONV], wco_ref[...],
                                            preferred_element_type=F32)
    merged = merged + g2_ref[...] * jnp.dot(br_ref[:, D_POOL + D_CONV:D_MODEL], wao_ref[...],
                                            preferred_element_type=F32)
    x1 = x_ref[...] + jnp.dot(merged.astype(BF16), wo_ref[...], preferred_element_type=F32)
    x1_ref[...] = x1
    h2 = _rms(x1, nf_ref[...])
    h2_ref[...] = _pack_bf16_pairs(h2)

    logits = jnp.dot(h2, wr_ref[...], preferred_element_type=F32,
                     precision=lax.Precision.HIGHEST) + brt_ref[...]
    lane = lax.broadcasted_iota(jnp.int32, (tm, N_EXPERTS), 1).astype(F32)
    vals, idxs, hots = [], [], []
    work = logits
    for _ in range(TOP_K):
        m = jnp.max(work, axis=-1, keepdims=True)
        idx = jnp.min(jnp.where(work == m, lane, float(N_EXPERTS)), axis=-1, keepdims=True)
        hot = lane == idx
        work = jnp.where(hot, -jnp.inf, work)
        vals.append(m)
        idxs.append(idx)
        hots.append(hot)
    es = [jnp.exp(v - vals[0]) for v in vals]
    denom = es[0] + es[1] + es[2] + es[3]

    chosen = jnp.where(hots[0] | hots[1] | hots[2] | hots[3], 1.0, 0.0).astype(BF16)
    r_i = lax.broadcasted_iota(jnp.int32, (tm, tm), 0)
    c_i = lax.broadcasted_iota(jnp.int32, (tm, tm), 1)
    lower = jnp.where(c_i < r_i, 1.0, 0.0).astype(BF16)
    before = jnp.dot(lower, chosen, preferred_element_type=F32) + carry_ref[0:1, 0:N_EXPERTS]
    carry_ref[0:1, 0:N_EXPERTS] = (carry_ref[0:1, 0:N_EXPERTS]
                                   + jnp.sum(chosen.astype(F32), axis=0, keepdims=True))
    cnt_ref[...] = carry_ref[...]

    out_lane = lax.broadcasted_iota(jnp.int32, (tm, LANES), 1)
    ri = jnp.zeros((tm, LANES), jnp.int32)
    rw = jnp.zeros((tm, LANES), F32)
    for k in range(TOP_K):
        rank = jnp.sum(jnp.where(hots[k], before, 0.0), axis=-1, keepdims=True).astype(jnp.int32)
        ri = jnp.where(out_lane == k, idxs[k].astype(jnp.int32), ri)
        ri = jnp.where(out_lane == TOP_K + k, rank, ri)
        rw = jnp.where(out_lane == k, es[k] / denom, rw)
    ri_ref[...] = ri
    rw_ref[...] = rw


def _merge_route(x, branch, proj, wpo, wco, wao, wo, norm_ffn, w_router, b_router, name):
    t = x.shape[0]
    tm = MERGE_TM
    assert t % tm == 0
    gate_blk0 = D_MIX // D_MODEL
    assert D_MIX % D_MODEL == 0
    const = lambda i: (0, 0)
    return pl.pallas_call(
        _merge_route_kernel,
        grid=(t // tm,),
        in_specs=[
            pl.BlockSpec((tm, D_MODEL), lambda i: (i, 0)),
            pl.BlockSpec((tm, D_MODEL), lambda i: (i, 0)),
            pl.BlockSpec((tm, D_MODEL), lambda i: (i, gate_blk0)),
            pl.BlockSpec((tm, D_MODEL), lambda i: (i, gate_blk0 + 1)),
            pl.BlockSpec((tm, D_MODEL), lambda i: (i, gate_blk0 + 2)),
            pl.BlockSpec((D_POOL, D_MODEL), const, pipeline_mode=pl.Buffered(1)),
            pl.BlockSpec((D_CONV, D_MODEL), const, pipeline_mode=pl.Buffered(1)),
            pl.BlockSpec((D_XATTN, D_MODEL), const, pipeline_mode=pl.Buffered(1)),
            pl.BlockSpec((D_MODEL, D_MODEL), const, pipeline_mode=pl.Buffered(1)),
            pl.BlockSpec((1, D_MODEL), const),
            pl.BlockSpec((D_MODEL, N_EXPERTS), const),
            pl.BlockSpec((1, N_EXPERTS), const),
        ],
        out_specs=[
            pl.BlockSpec((tm, D_MODEL), lambda i: (i, 0)),
            pl.BlockSpec((tm, D_MODEL // 2), lambda i: (i, 0)),
            pl.BlockSpec((tm, LANES), lambda i: (i, 0)),
            pl.BlockSpec((tm, LANES), lambda i: (i, 0)),
            pl.BlockSpec((8, LANES), const),
        ],
        out_shape=[
            jax.ShapeDtypeStruct((t, D_MODEL), F32),
            jax.ShapeDtypeStruct((t, D_MODEL // 2), PACKED),
            jax.ShapeDtypeStruct((t, LANES), jnp.int32),
            jax.ShapeDtypeStruct((t, LANES), F32),
            jax.ShapeDtypeStruct((8, LANES), F32),
        ],
        scratch_shapes=[pltpu.VMEM((8, LANES), F32)],
        compiler_params=pltpu.CompilerParams(
            dimension_semantics=("arbitrary",), vmem_limit_bytes=VMEM_LIMIT),
        name=name,
    )(x, branch, proj, proj, proj, wpo, wco, wao, wo, norm_ffn.reshape(1, D_MODEL),
      w_router, b_router.reshape(1, N_EXPERTS))


def _sc_mesh():
    return plsc.VectorSubcoreMesh(core_axis_name="c", subcore_axis_name="s",
                                  num_cores=SC_CORES, num_subcores=SC_SUBCORES)


def _sc_worker_id():
    return lax.axis_index("s") * SC_CORES + lax.axis_index("c")


def _sc_dispatch(h2, dest, rows_out):
    t, w = h2.shape
    per_w = t // SC_WORKERS
    n_chunks = per_w // SC_CHUNK
    assert t == SC_WORKERS * n_chunks * SC_CHUNK
    idx = dest.reshape(SC_WORKERS, n_chunks, SC_CHUNK, TOP_K).transpose(0, 1, 3, 2)
    idx = idx.reshape(SC_WORKERS, n_chunks * TOP_K, SC_CHUNK)

    def body(h2_hbm, idx_hbm, xs_hbm, idx_v, rows_v, sems):
        base = _sc_worker_id() * per_w
        pltpu.sync_copy(idx_hbm.at[_sc_worker_id()], idx_v)
        pending = [[], []]
        for c in range(n_chunks):
            b = c % 2
            for d in pending[b]:
                d.wait()
            pltpu.sync_copy(h2_hbm.at[pl.ds(base + c * SC_CHUNK, SC_CHUNK)], rows_v.at[b])
            pending[b] = [pltpu.async_copy(rows_v.at[b], xs_hbm.at[idx_v.at[c * TOP_K + k]], sems.at[b])
                          for k in range(TOP_K)]
        for b in range(2):
            for d in pending[b]:
                d.wait()

    return pl.kernel(
        body,
        out_type=jax.ShapeDtypeStruct((rows_out, w), h2.dtype),
        mesh=_sc_mesh(),
        scratch_types=[pltpu.VMEM((n_chunks * TOP_K, SC_CHUNK), jnp.int32),
                       pltpu.VMEM((2, SC_CHUNK, w), h2.dtype),
                       pltpu.SemaphoreType.DMA((2,))],
        name="sc_dispatch",
    )(h2, idx)


def _sc_gather_rows(table, idx):
    n = idx.shape[0]
    w = table.shape[1]
    per_w = n // SC_WORKERS
    n_chunks = per_w // SC_CHUNK
    assert n == SC_WORKERS * n_chunks * SC_CHUNK
    idx3 = idx.reshape(SC_WORKERS, n_chunks, SC_CHUNK)

    def body(table_hbm, idx_hbm, out_hbm, idx_v, rows_v, gsem, wsems):
        base = _sc_worker_id() * per_w
        pltpu.sync_copy(idx_hbm.at[_sc_worker_id()], idx_v)
        pending = [None, None]
        for c in range(n_chunks):
            b = c % 2
            if pending[b] is not None:
                pending[b].wait()
            pltpu.async_copy(table_hbm.at[idx_v.at[c]], rows_v.at[b], gsem).wait()
            pending[b] = pltpu.async_copy(
                rows_v.at[b], out_hbm.at[pl.ds(base + c * SC_CHUNK, SC_CHUNK)], wsems.at[b])
        for b in range(2):
            if pending[b] is not None:
                pending[b].wait()

    return pl.kernel(
        body,
        out_type=jax.ShapeDtypeStruct((n, w), table.dtype),
        mesh=_sc_mesh(),
        scratch_types=[pltpu.VMEM((n_chunks, SC_CHUNK), jnp.int32),
                       pltpu.VMEM((2, SC_CHUNK, w), table.dtype),
                       pltpu.SemaphoreType.DMA,
                       pltpu.SemaphoreType.DMA((2,))],
        name="sc_gather_rows",
    )(table, idx3)


def _expert_row_loop(n, in_copy, out_copy, prepare_weights, compute):
    nchunk = (n + EXP_CHUNK - 1) // EXP_CHUNK

    def blocks_of(c):
        return jnp.minimum(EXP_CHUNK, n - c * EXP_CHUNK)

    def each_block(c, slot, copy, op):
        for k in range(EXP_CHUNK):
            @pl.when(k < blocks_of(c))
            def _():
                op(copy(c * EXP_CHUNK + k, slot, k))

    def start(d):
        d.start()

    def wait(d):
        d.wait()

    each_block(0, 0, in_copy, start)
    prepare_weights()

    def body(c, carry):
        slot = lax.rem(c, 2)
        each_block(c, slot, in_copy, wait)

        @pl.when(c + 1 < nchunk)
        def _():
            each_block(c + 1, 1 - slot, in_copy, start)

        @pl.when(c >= 2)
        def _():
            each_block(c - 2, slot, out_copy, wait)

        for m in range(1, EXP_CHUNK + 1):
            @pl.when(blocks_of(c) == m)
            def _():
                compute(slot, m)

        each_block(c, slot, out_copy, start)
        return carry

    lax.fori_loop(0, nchunk, body, 0)

    @pl.when(nchunk >= 2)
    def _():
        each_block(nchunk - 2, lax.rem(nchunk, 2), out_copy, wait)

    each_block(nchunk - 1, lax.rem(nchunk - 1, 2), out_copy, wait)


def _expert_in_kernel(row0_ref, nblk_ref, xs_hbm, *refs):
    wg_refs, wu_refs = refs[:W_SPLIT], refs[W_SPLIT:2 * W_SPLIT]
    bg_ref, bu_ref, act_hbm, wgb_ref, wub_ref, x_buf, a_buf, x_sem, a_sem = refs[2 * W_SPLIT:]
    kc = D_MODEL // W_SPLIT
    j = pl.program_id(0)
    e = pl.program_id(1)
    n = nblk_ref[e]
    tm, tn = EXP_TM, a_buf.shape[2]

    def rows(b):
        return pl.ds(pl.multiple_of(row0_ref[e] + b * tm, tm), tm)

    def x_copy(b, slot, k):
        return pltpu.make_async_copy(xs_hbm.at[rows(b)], x_buf.at[slot, pl.ds(k * tm, tm)], x_sem.at[slot])

    def a_copy(b, slot, k):
        return pltpu.make_async_copy(
            a_buf.at[slot, pl.ds(k * tm, tm)],
            act_hbm.at[rows(b), pl.ds(pl.multiple_of(j * tn, tn), tn)], a_sem.at[slot])

    def compute(slot, m):
        x = jnp.concatenate(_unpack_bf16_pairs(x_buf[slot, 0:m * tm]), axis=1).astype(BF16)
        g = jnp.dot(x, wgb_ref[...], preferred_element_type=F32) + bg_ref[0]
        up = jnp.dot(x, wub_ref[...], preferred_element_type=F32) + bu_ref[0]
        g = jnp.minimum(g, SWIGLU_LIMIT)
        up = jnp.clip(up, -SWIGLU_LIMIT, SWIGLU_LIMIT)
        a_buf[slot, 0:m * tm] = (g * _sigmoid(SWIGLU_ALPHA * g) * (up + 1.0)).astype(BF16)

    def prepare_weights():
        for c in range(W_SPLIT):
            wgb_ref[c * kc:(c + 1) * kc, :] = wg_refs[c][0].astype(BF16)
            wub_ref[c * kc:(c + 1) * kc, :] = wu_refs[c][0].astype(BF16)

    @pl.when(n > 0)
    def _():
        _expert_row_loop(n, x_copy, a_copy, prepare_weights, compute)


def _expert_in(row0, nblk, xs, w_in, b_in):
    rows = xs.shape[0]
    tm, tn = EXP_TM, EXP_IN_TN
    nj = D_FF // tn
    kc = D_MODEL // W_SPLIT

    def w_spec(c, col0):
        return pl.BlockSpec((1, kc, tn), lambda j, e, r0, nb: (e, c, col0 + j))

    grid_spec = pltpu.PrefetchScalarGridSpec(
        num_scalar_prefetch=2,
        grid=(nj, N_EXPERTS),
        in_specs=[
            pl.BlockSpec(memory_space=pl.ANY),
            *[w_spec(c, 0) for c in range(W_SPLIT)],
            *[w_spec(c, nj) for c in range(W_SPLIT)],
            pl.BlockSpec((1, 1, tn), lambda j, e, r0, nb: (e, 0, j)),
            pl.BlockSpec((1, 1, tn), lambda j, e, r0, nb: (e, 0, nj + j)),
        ],
        out_specs=pl.BlockSpec(memory_space=pl.ANY),
        scratch_shapes=[pltpu.VMEM((D_MODEL, tn), BF16), pltpu.VMEM((D_MODEL, tn), BF16),
                        pltpu.VMEM((2, EXP_CHUNK * tm, D_MODEL // 2), PACKED),
                        pltpu.VMEM((2, EXP_CHUNK * tm, tn), BF16),
                        pltpu.SemaphoreType.DMA((2,)), pltpu.SemaphoreType.DMA((2,))],
    )
    return pl.pallas_call(
        _expert_in_kernel,
        grid_spec=grid_spec,
        out_shape=jax.ShapeDtypeStruct((rows, D_FF), BF16),
        compiler_params=pltpu.CompilerParams(
            dimension_semantics=("arbitrary", "arbitrary"), vmem_limit_bytes=VMEM_LIMIT),
        name="expert_in",
    )(row0, nblk, xs, *([w_in] * (2 * W_SPLIT)), b_in, b_in)


def _expert_out_kernel(row0_ref, nblk_ref, act_hbm, *refs):
    w_refs = refs[:W_SPLIT]
    b_ref, ys_hbm, wb_ref, a_buf, y_buf, a_sem, y_sem = refs[W_SPLIT:]
    kc = D_FF // W_SPLIT
    j = pl.program_id(0)
    e = pl.program_id(1)
    n = nblk_ref[e]
    tm, tnp = EXP_TM, y_buf.shape[2]

    def rows(b):
        return pl.ds(pl.multiple_of(row0_ref[e] + b * tm, tm), tm)

    def a_copy(b, slot, k):
        return pltpu.make_async_copy(act_hbm.at[rows(b)], a_buf.at[slot, pl.ds(k * tm, tm)], a_sem.at[slot])

    def y_copy(b, slot, k):
        return pltpu.make_async_copy(
            y_buf.at[slot, pl.ds(k * tm, tm)],
            ys_hbm.at[rows(b), pl.ds(pl.multiple_of(j * tnp, tnp), tnp)], y_sem.at[slot])

    def prepare_weights():
        for c in range(W_SPLIT):
            wb_ref[c * kc:(c + 1) * kc, :] = w_refs[c][0].astype(BF16)

    def compute(slot, m):
        y = jnp.dot(a_buf[slot, 0:m * tm], wb_ref[...], preferred_element_type=F32) + b_ref[0]
        y_buf[slot, 0:m * tm] = _pack_bf16_pairs(y)

    @pl.when(n > 0)
    def _():
        _expert_row_loop(n, a_copy, y_copy, prepare_weights, compute)


def _expert_out(row0, nblk, act, w_out, b_out):
    rows = act.shape[0]
    tm, tn = EXP_TM, EXP_OUT_TN
    nj = D_MODEL // tn
    grid_spec = pltpu.PrefetchScalarGridSpec(
        num_scalar_prefetch=2,
        grid=(nj, N_EXPERTS),
        in_specs=[
            pl.BlockSpec(memory_space=pl.ANY),
            *[pl.BlockSpec((1, D_FF // W_SPLIT, tn), functools.partial(lambda c, j, e, r0, nb: (e, c, j), c))
              for c in range(W_SPLIT)],
            pl.BlockSpec((1, 1, tn), lambda j, e, r0, nb: (e, 0, j)),
        ],
        out_specs=pl.BlockSpec(memory_space=pl.ANY),
        scratch_shapes=[pltpu.VMEM((D_FF, tn), BF16),
                        pltpu.VMEM((2, EXP_CHUNK * tm, D_FF), BF16),
                        pltpu.VMEM((2, EXP_CHUNK * tm, tn // 2), PACKED),
                        pltpu.SemaphoreType.DMA((2,)), pltpu.SemaphoreType.DMA((2,))],
    )
    return pl.pallas_call(
        _expert_out_kernel,
        grid_spec=grid_spec,
        out_shape=jax.ShapeDtypeStruct((rows, D_MODEL // 2), PACKED),
        compiler_params=pltpu.CompilerParams(
            dimension_semantics=("arbitrary", "arbitrary"), vmem_limit_bytes=VMEM_LIMIT),
        name="expert_out",
    )(row0, nblk, act, *([w_out] * W_SPLIT), b_out)


def _combine_kernel(x1_ref, yg_ref, rw_ref, g_ref, o_ref):
    half = EXP_OUT_TN // 2
    pieces = []
    ssq = jnp.zeros((x1_ref.shape[0], 1), F32)
    for jt in range(D_MODEL // EXP_OUT_TN):
        acc_lo = x1_ref[:, jt * EXP_OUT_TN:jt * EXP_OUT_TN + half]
        acc_hi = x1_ref[:, jt * EXP_OUT_TN + half:(jt + 1) * EXP_OUT_TN]
        moe_lo = jnp.zeros_like(acc_lo)
        moe_hi = jnp.zeros_like(acc_hi)
        for k in range(TOP_K):
            lo, hi = _unpack_bf16_pairs(yg_ref[k, :, jt * half:(jt + 1) * half])
            moe_lo = moe_lo + lo * rw_ref[:, k:k + 1]
            moe_hi = moe_hi + hi * rw_ref[:, k:k + 1]
        for acc in (acc_lo + moe_lo, acc_hi + moe_hi):
            ssq = ssq + jnp.sum(acc * acc, axis=-1, keepdims=True)
            pieces.append(acc)
    inv = lax.rsqrt(ssq / D_MODEL + EPS)
    for n, acc in enumerate(pieces):
        o_ref[:, n * half:(n + 1) * half] = acc * inv * g_ref[:, n * half:(n + 1) * half]


def _combine(x1, yg, rw, gain, row_off, name):
    t = x1.shape[0]
    tm = COMB_TM
    assert t % tm == 0 and row_off % tm == 0
    off = row_off // tm
    return pl.pallas_call(
        _combine_kernel,
        grid=(t // tm,),
        in_specs=[
            pl.BlockSpec((tm, D_MODEL), lambda i: (i, 0)),
            pl.BlockSpec((TOP_K, tm, D_MODEL // 2), lambda i: (0, i + off, 0)),
            pl.BlockSpec((tm, LANES), lambda i: (i, 0)),
            pl.BlockSpec((1, D_MODEL), lambda i: (0, 0)),
        ],
        out_specs=pl.BlockSpec((tm, D_MODEL), lambda i: (i, 0)),
        out_shape=jax.ShapeDtypeStruct((t, D_MODEL), F32),
        compiler_params=pltpu.CompilerParams(
            dimension_semantics=("arbitrary",), vmem_limit_bytes=VMEM_LIMIT),
        name=name,
    )(x1, yg, rw, gain.reshape(1, D_MODEL))


def kernel(x_prompt, x_sample, state_pool, state_conv, cache_mem_k, cache_mem_v, mem_prompt,
           norm_mix, w_in, b_gate, w_pool_group, pool_scale, w_conv, mem_norm, w_mem_kv,
           w_pool_out, w_conv_out, w_attn_out, w_o, norm_ffn, w_router, b_router,
           w_exp_in, b_exp_in, w_exp_out, b_exp_out, final_norm):
    depth = norm_mix.shape[0]
    assert depth == 1
    l = 0
    bp, seq, _ = x_prompt.shape
    bs, ln, _ = x_sample.shape
    tp, ts = bp * seq, bs * ln

    kv = _norm_matmul(mem_prompt.reshape(bp * N_MEM, D_MODEL), mem_norm[l], w_mem_kv[l],
                      jnp.zeros((2 * D_XATTN,), F32), 2 * D_XATTN, "mem_kv")
    mk = kv[:, :D_XATTN].reshape(bp, N_MEM, D_XATTN)
    mv = kv[:, D_XATTN:].reshape(bp, N_MEM, D_XATTN)

    bias_in = jnp.concatenate([jnp.zeros((D_MIX,), F32), b_gate[l]])
    xp = x_prompt.reshape(tp, D_MODEL)
    xs_ = x_sample.reshape(ts, D_MODEL)
    proj_p = _norm_matmul(xp, norm_mix[l], w_in[l], bias_in, D_MIX, "proj_prompt")
    proj_s = _norm_matmul(xs_, norm_mix[l], w_in[l], bias_in, D_MIX, "proj_sample")

    br_p, zst_p = _mix_prompt(proj_p, mk, mv, w_pool_group[l], pool_scale[l], w_conv[l], bp, seq)
    br_s, zst_s = _mix_sample(proj_s, state_pool[l], state_conv[l],
                              cache_mem_k[l].reshape(bs, N_MEM * N_XHEADS, XHEAD_DIM),
                              cache_mem_v[l].reshape(bs, N_MEM * N_XHEADS, XHEAD_DIM),
                              w_pool_group[l], pool_scale[l], w_conv[l], bs, ln)

    wpo, wco, wao, wo = (w.astype(BF16) for w in (w_pool_out[l], w_conv_out[l], w_attn_out[l], w_o[l]))
    x1_p, h2_p, ri_p, rw_p, cnt_p = _merge_route(xp, br_p, proj_p, wpo, wco, wao, wo, norm_ffn[l],
                                                  w_router[l], b_router[l], "merge_route_prompt")
    x1_s, h2_s, ri_s, rw_s, cnt_s = _merge_route(xs_, br_s, proj_s, wpo, wco, wao, wo, norm_ffn[l],
                                                  w_router[l], b_router[l], "merge_route_sample")

    t_all = tp + ts
    n_assign = t_all * TOP_K
    nb_max = n_assign // EXP_TM + N_EXPERTS
    size_p = cnt_p[0, :N_EXPERTS].astype(jnp.int32)
    size_s = cnt_s[0, :N_EXPERTS].astype(jnp.int32)
    nblk = (size_p + size_s + EXP_TM - 1) // EXP_TM
    blk_end = jnp.cumsum(nblk)
    pad_start = (blk_end - nblk) * EXP_TM

    e_p, r_p = ri_p[:, :TOP_K], ri_p[:, TOP_K:2 * TOP_K]
    e_s, r_s = ri_s[:, :TOP_K], ri_s[:, TOP_K:2 * TOP_K]
    dest_p = pad_start[e_p] + r_p
    dest_s = pad_start[e_s] + size_p[e_s] + r_s
    dest = jnp.concatenate([dest_p, dest_s], axis=0)
    h2 = jnp.concatenate([h2_p, h2_s], axis=0)
    xs_sorted = _sc_dispatch(h2, dest, nb_max * EXP_TM)

    act = _expert_in(pad_start, nblk, xs_sorted, w_exp_in[l], b_exp_in[l].reshape(N_EXPERTS, 1, 2 * D_FF))
    ys = _expert_out(pad_start, nblk, act, w_exp_out[l], b_exp_out[l].reshape(N_EXPERTS, 1, D_MODEL))
    yg = _sc_gather_rows(ys, dest.T.reshape(-1)).reshape(TOP_K, t_all, D_MODEL // 2)

    y_p = _combine(x1_p, yg, rw_p, final_norm, 0, "combine_prompt")
    y_s = _combine(x1_s, yg, rw_s, final_norm, tp, "combine_sample")

    new_pool_p = proj_p[:, :D_POOL].reshape(bp, seq, D_POOL)[:, seq - POOL_STATE_LEN:]
    new_conv_p = zst_p[:, 8 - (CONV_WIDTH - 1):]
    u_s = proj_s[:, :D_POOL].reshape(bs, ln, D_POOL)
    new_pool_s = jnp.concatenate([state_pool[l], u_s], axis=1)[:, -POOL_STATE_LEN:]
    new_conv_s = zst_s[:, ln - (CONV_WIDTH - 1):]

    return (y_p.reshape(bp, seq, D_MODEL), y_s.reshape(bs, ln, D_MODEL),
            new_pool_p[None], new_conv_p[None],
            mk.reshape(1, bp, N_MEM, N_XHEADS, XHEAD_DIM), mv.reshape(1, bp, N_MEM, N_XHEADS, XHEAD_DIM),
            new_pool_s[None], new_conv_s[None])
```

```python
import functools

import jax
import jax.numpy as jnp
from jax import lax
from jax.experimental import pallas as pl
from jax.experimental.pallas import tpu as pltpu
from jax.experimental.pallas import tpu_sc as plsc

F32 = jnp.float32
BF16 = jnp.bfloat16
PACKED = jnp.int32

D_MODEL = 2048
POOL_WINDOWS = (2, 4, 8, 16)
POOL_GROUP_DIM = 128
D_POOL = 512
POOL_STATE_LEN = 15
D_CONV = 1024
CONV_WIDTH = 3
N_MEM = 256
N_XHEADS = 4
XHEAD_DIM = 128
D_XATTN = 512
N_BRANCH = 3
D_MIX = D_POOL + 3 * D_CONV + D_XATTN
D_IN_TOTAL = D_MIX + N_BRANCH * D_MODEL
N_EXPERTS = 32
TOP_K = 4
D_FF = D_MODEL
SWIGLU_LIMIT = 7.0
SWIGLU_ALPHA = 1.702
EPS = 1e-5

C_U = 0
C_V = D_POOL
C_B = D_POOL + D_CONV
C_C = D_POOL + 2 * D_CONV
C_Q = D_POOL + 3 * D_CONV

LANES = 128
HIST = 16

PROJ_TM = 1024
PROJ_TN = 1024
MIX_TM = 256
MIX_NS = 8
MERGE_TM = 256
EXP_TM = 256
EXP_CHUNK = 2
ROW_DMA_PRIORITY = 1
EXP_IN_TN = 1024
EXP_OUT_TN = 1024
W_SPLIT = 4
COMB_TM = 256
SC_CORES = 2
SC_SUBCORES = 16
SC_WORKERS = SC_CORES * SC_SUBCORES
SC_CHUNK = 32
VMEM_LIMIT = 56 * 1024 * 1024


def _sigmoid(x):
    return 1.0 / (1.0 + jnp.exp(-x))


def _rms(x, g):
    ms = jnp.mean(x * x, axis=-1, keepdims=True)
    return x * lax.rsqrt(ms + EPS) * g


def _pack_bf16_pairs(x):
    n = x.shape[1] // 2
    lo = lax.bitcast_convert_type(x[:, :n].astype(BF16).astype(F32), jnp.uint32)
    hi = lax.bitcast_convert_type(x[:, n:].astype(BF16).astype(F32), jnp.uint32)
    return lax.bitcast_convert_type((hi & jnp.uint32(0xFFFF0000)) | (lo >> 16), PACKED)


def _unpack_bf16_pairs(p):
    p = lax.bitcast_convert_type(p, jnp.uint32)
    lo = lax.bitcast_convert_type(p << 16, F32)
    hi = lax.bitcast_convert_type(p & jnp.uint32(0xFFFF0000), F32)
    return lo, hi


def _norm_matmul_kernel(x_ref, g_ref, *refs, act_from):
    w_refs = refs[:W_SPLIT]
    b_ref, o_ref, h_ref, wb_ref = refs[W_SPLIT:]
    kc = wb_ref.shape[0] // W_SPLIT
    j = pl.program_id(1)

    @pl.when(j == 0)
    def _():
        h_ref[...] = _rms(x_ref[...], g_ref[...]).astype(BF16)

    for c in range(W_SPLIT):
        wb_ref[c * kc:(c + 1) * kc, :] = w_refs[c][...].astype(BF16)
    acc = jnp.dot(h_ref[...], wb_ref[...], preferred_element_type=F32) + b_ref[...]

    @pl.when(j < act_from)
    def _():
        o_ref[...] = acc

    @pl.when(j >= act_from)
    def _():
        o_ref[...] = _sigmoid(acc)


def _norm_matmul(x, gain, w, bias, act_from_col, name):
    t, d = x.shape
    n = w.shape[1]
    tm = min(PROJ_TM, t)
    tn = PROJ_TN
    assert t % tm == 0 and n % tn == 0 and act_from_col % tn == 0
    return pl.pallas_call(
        functools.partial(_norm_matmul_kernel, act_from=act_from_col // tn),
        grid=(t // tm, n // tn),
        in_specs=[
            pl.BlockSpec((tm, d), lambda i, j: (i, 0)),
            pl.BlockSpec((1, d), lambda i, j: (0, 0)),
            *[pl.BlockSpec((d // W_SPLIT, tn), functools.partial(lambda c, i, j: (c, j), c))
              for c in range(W_SPLIT)],
            pl.BlockSpec((1, tn), lambda i, j: (0, j)),
        ],
        out_specs=pl.BlockSpec((tm, tn), lambda i, j: (i, j)),
        out_shape=jax.ShapeDtypeStruct((t, n), F32),
        scratch_shapes=[pltpu.VMEM((tm, d), BF16), pltpu.VMEM((d, tn), BF16)],
        compiler_params=pltpu.CompilerParams(
            dimension_semantics=("arbitrary", "arbitrary"), vmem_limit_bytes=VMEM_LIMIT),
        name=name,
    )(x, gain.reshape(1, d), *([w] * W_SPLIT), bias.reshape(1, n))


def _pool_project(pooled, wpg_ref, scale_ref, g):
    sl = slice(g * POOL_GROUP_DIM, (g + 1) * POOL_GROUP_DIM)
    y = jnp.dot(pooled.astype(BF16), wpg_ref[g].astype(BF16), preferred_element_type=F32)
    return y * scale_ref[:, sl]


def _softmax_rows(s):
    m = jnp.max(s, axis=-1, keepdims=True)
    e = jnp.exp(s - m)
    return e / jnp.sum(e, axis=-1, keepdims=True)


def _mix_prompt_kernel(cur_ref, prev_ref, mk_ref, mv_ref, wpg_ref, scale_ref, wconv_ref,
                       br_ref, zst_ref, extu_ref, extz_ref):
    t = pl.program_id(1)
    tm = cur_ref.shape[0]
    has_prev = t > 0

    u = cur_ref[:, C_U:C_U + D_POOL]
    extu_ref[0:HIST, :] = jnp.where(has_prev, prev_ref[:, C_U:C_U + D_POOL], 0.0)
    extu_ref[HIST:HIST + tm, :] = u
    pos = t * tm + lax.broadcasted_iota(jnp.int32, (tm, 1), 0)
    for g, w in enumerate(POOL_WINDOWS):
        sl = slice(g * POOL_GROUP_DIM, (g + 1) * POOL_GROUP_DIM)
        s = extu_ref[HIST:HIST + tm, sl]
        for k in range(1, w):
            s = s + extu_ref[HIST - k:HIST - k + tm, sl]
        cnt = jnp.minimum(w, pos + 1).astype(F32)
        pooled = s / cnt - extu_ref[HIST:HIST + tm, sl]
        br_ref[:, sl] = _pool_project(pooled, wpg_ref, scale_ref, g).astype(BF16)

    z = cur_ref[:, C_C:C_C + D_CONV] * cur_ref[:, C_V:C_V + D_CONV]
    zprev = prev_ref[:, C_C:C_C + D_CONV] * prev_ref[:, C_V:C_V + D_CONV]
    extz_ref[0:HIST, :] = jnp.where(has_prev, zprev, 0.0)
    extz_ref[HIST:HIST + tm, :] = z
    y = extz_ref[HIST - 2:HIST - 2 + tm, :] * wconv_ref[0:1, :]
    y = y + extz_ref[HIST - 1:HIST - 1 + tm, :] * wconv_ref[1:2, :]
    y = y + extz_ref[HIST:HIST + tm, :] * wconv_ref[2:3, :]
    br_ref[:, D_POOL:D_POOL + D_CONV] = (cur_ref[:, C_B:C_B + D_CONV] * y).astype(BF16)
    zst_ref[0] = extz_ref[HIST + tm - 8:HIST + tm, :]

    for h in range(N_XHEADS):
        sl = slice(h * XHEAD_DIM, (h + 1) * XHEAD_DIM)
        qh = cur_ref[:, C_Q + h * XHEAD_DIM:C_Q + (h + 1) * XHEAD_DIM].astype(BF16)
        kh = mk_ref[0, :, sl].astype(BF16)
        vh = mv_ref[0, :, sl].astype(BF16)
        s = lax.dot_general(qh, kh, (((1,), (1,)), ((), ())), preferred_element_type=F32)
        p = _softmax_rows(s * (XHEAD_DIM ** -0.5))
        o = jnp.dot(p.astype(BF16), vh, preferred_element_type=F32)
        c0 = D_POOL + D_CONV + h * XHEAD_DIM
        br_ref[:, c0:c0 + XHEAD_DIM] = o.astype(BF16)


def _mix_prompt(proj, mk, mv, wpg, scale, wconv, batch, seq):
    tm = MIX_TM
    nt = seq // tm
    assert seq % tm == 0 and tm % HIST == 0
    rpb = tm // HIST
    return pl.pallas_call(
        _mix_prompt_kernel,
        grid=(batch, nt),
        in_specs=[
            pl.BlockSpec((tm, D_MIX), lambda b, t: (b * nt + t, 0)),
            pl.BlockSpec((HIST, D_MIX), lambda b, t: (jnp.maximum((b * nt + t) * rpb - 1, 0), 0)),
            pl.BlockSpec((1, N_MEM, D_XATTN), lambda b, t: (b, 0, 0)),
            pl.BlockSpec((1, N_MEM, D_XATTN), lambda b, t: (b, 0, 0)),
            pl.BlockSpec((len(POOL_WINDOWS), POOL_GROUP_DIM, POOL_GROUP_DIM), lambda b, t: (0, 0, 0)),
            pl.BlockSpec((1, D_POOL), lambda b, t: (0, 0)),
            pl.BlockSpec((CONV_WIDTH, D_CONV), lambda b, t: (0, 0)),
        ],
        out_specs=[
            pl.BlockSpec((tm, D_MODEL), lambda b, t: (b * nt + t, 0)),
            pl.BlockSpec((1, 8, D_CONV), lambda b, t: (b, 0, 0)),
        ],
        out_shape=[
            jax.ShapeDtypeStruct((batch * seq, D_MODEL), BF16),
            jax.ShapeDtypeStruct((batch, 8, D_CONV), F32),
        ],
        scratch_shapes=[pltpu.VMEM((HIST + tm, D_POOL), F32), pltpu.VMEM((HIST + tm, D_CONV), F32)],
        compiler_params=pltpu.CompilerParams(
            dimension_semantics=("arbitrary", "arbitrary"), vmem_limit_bytes=VMEM_LIMIT),
        name="mix_prompt",
    )(proj, proj, mk, mv, wpg, scale.reshape(1, D_POOL), wconv)


def _mix_sample_kernel(cur_ref, sp_ref, sc_ref, k_ref, v_ref, wpg_ref, scale_ref, wconv_ref,
                       br_ref, zst_ref, extu_ref, extz_ref):
    ns, ln = zst_ref.shape[0], zst_ref.shape[1]
    rows = ns * ln

    extu_ref[:, HIST - POOL_STATE_LEN:HIST, :] = sp_ref[...]
    extu_ref[:, HIST:HIST + ln, :] = cur_ref[:, C_U:C_U + D_POOL].reshape(ns, ln, D_POOL)
    for g, w in enumerate(POOL_WINDOWS):
        sl = slice(g * POOL_GROUP_DIM, (g + 1) * POOL_GROUP_DIM)
        s = extu_ref[:, HIST:HIST + ln, sl]
        for k in range(1, w):
            s = s + extu_ref[:, HIST - k:HIST - k + ln, sl]
        pooled = s / float(w) - extu_ref[:, HIST:HIST + ln, sl]
        pooled = pooled.reshape(rows, POOL_GROUP_DIM)
        br_ref[:, sl] = _pool_project(pooled, wpg_ref, scale_ref, g).astype(BF16)

    z = cur_ref[:, C_C:C_C + D_CONV] * cur_ref[:, C_V:C_V + D_CONV]
    extz_ref[:, HIST - 2:HIST, :] = sc_ref[...]
    extz_ref[:, HIST:HIST + ln, :] = z.reshape(ns, ln, D_CONV)
    y = extz_ref[:, HIST - 2:HIST - 2 + ln, :] * wconv_ref[0:1, :]
    y = y + extz_ref[:, HIST - 1:HIST - 1 + ln, :] * wconv_ref[1:2, :]
    y = y + extz_ref[:, HIST:HIST + ln, :] * wconv_ref[2:3, :]
    br_ref[:, D_POOL:D_POOL + D_CONV] = (
        cur_ref[:, C_B:C_B + D_CONV] * y.reshape(rows, D_CONV)).astype(BF16)
    zst_ref[...] = extz_ref[:, HIST:HIST + ln, :]

    q3 = cur_ref[:, C_Q:C_Q + D_XATTN].reshape(ns, ln, D_XATTN)
    q4 = jnp.concatenate([q3[:, :, h * XHEAD_DIM:(h + 1) * XHEAD_DIM] for h in range(N_XHEADS)], axis=1)
    s = jnp.einsum("nqd,nkd->nqk", q4.astype(BF16), k_ref[...].astype(BF16),
                   preferred_element_type=F32)
    row_head = lax.broadcasted_iota(jnp.int32, (N_XHEADS * ln, N_XHEADS * N_MEM), 0) // ln
    col_head = lax.broadcasted_iota(jnp.int32, (N_XHEADS * ln, N_XHEADS * N_MEM), 1) % N_XHEADS
    s = jnp.where((row_head == col_head)[None], s * (XHEAD_DIM ** -0.5), -jnp.inf)
    p = _softmax_rows(s)
    r = jnp.einsum("nqk,nkd->nqd", p.astype(BF16), v_ref[...].astype(BF16),
                   preferred_element_type=F32)
    for h in range(N_XHEADS):
        c0 = D_POOL + D_CONV + h * XHEAD_DIM
        br_ref[:, c0:c0 + XHEAD_DIM] = r[:, h * ln:(h + 1) * ln, :].reshape(rows, XHEAD_DIM).astype(BF16)


def _mix_sample(proj, state_pool, state_conv, mem_k, mem_v, wpg, scale, wconv, nseq, ln):
    ns = MIX_NS
    rows = ns * ln
    assert nseq % ns == 0 and ln == 8
    return pl.pallas_call(
        _mix_sample_kernel,
        grid=(nseq // ns,),
        in_specs=[
            pl.BlockSpec((rows, D_MIX), lambda s: (s, 0)),
            pl.BlockSpec((ns, POOL_STATE_LEN, D_POOL), lambda s: (s, 0, 0)),
            pl.BlockSpec((ns, CONV_WIDTH - 1, D_CONV), lambda s: (s, 0, 0)),
            pl.BlockSpec((ns, N_MEM * N_XHEADS, XHEAD_DIM), lambda s: (s, 0, 0)),
            pl.BlockSpec((ns, N_MEM * N_XHEADS, XHEAD_DIM), lambda s: (s, 0, 0)),
            pl.BlockSpec((len(POOL_WINDOWS), POOL_GROUP_DIM, POOL_GROUP_DIM), lambda s: (0, 0, 0)),
            pl.BlockSpec((1, D_POOL), lambda s: (0, 0)),
            pl.BlockSpec((CONV_WIDTH, D_CONV), lambda s: (0, 0)),
        ],
        out_specs=[
            pl.BlockSpec((rows, D_MODEL), lambda s: (s, 0)),
            pl.BlockSpec((ns, ln, D_CONV), lambda s: (s, 0, 0)),
        ],
        out_shape=[
            jax.ShapeDtypeStruct((nseq * ln, D_MODEL), BF16),
            jax.ShapeDtypeStruct((nseq, ln, D_CONV), F32),
        ],
        scratch_shapes=[pltpu.VMEM((ns, HIST + ln, D_POOL), F32),
                        pltpu.VMEM((ns, HIST + ln, D_CONV), F32)],
        compiler_params=pltpu.CompilerParams(
            dimension_semantics=("arbitrary",), vmem_limit_bytes=VMEM_LIMIT),
        name="mix_sample",
    )(proj, state_pool, state_conv, mem_k, mem_v, wpg, scale.reshape(1, D_POOL), wconv)


def _merge_route_kernel(x_ref, br_ref, g0_ref, g1_ref, g2_ref, wpo_ref, wco_ref, wao_ref, wo_ref,
                        nf_ref, wr_ref, brt_ref,
                        x1_ref, h2_ref, ri_ref, rw_ref, cnt_ref, carry_ref):
    i = pl.program_id(0)
    tm = x_ref.shape[0]

    @pl.when(i == 0)
    def _():
        carry_ref[...] = jnp.zeros_like(carry_ref)

    merged = g0_ref[...] * jnp.dot(br_ref[:, 0:D_POOL], wpo_ref[...], preferred_element_type=F32)
    merged = merged + g1_ref[...] * jnp.dot(br_ref[:, D_POOL:D_POOL + D_CONV], wco_ref[...],
                                            preferred_element_type=F32)
    merged = merged + g2_ref[...] * jnp.dot(br_ref[:, D_POOL + D_CONV:D_MODEL], wao_ref[...],
                                            preferred_element_type=F32)
    x1 = x_ref[...] + jnp.dot(merged.astype(BF16), wo_ref[...], preferred_element_type=F32)
    x1_ref[...] = x1
    h2 = _rms(x1, nf_ref[...])
    h2_ref[...] = _pack_bf16_pairs(h2)

    logits = jnp.dot(h2, wr_ref[...], preferred_element_type=F32,
                     precision=lax.Precision.HIGHEST) + brt_ref[...]
    lane = lax.broadcasted_iota(jnp.int32, (tm, N_EXPERTS), 1).astype(F32)
    vals, idxs, hots = [], [], []
    work = logits
    for _ in range(TOP_K):
        m = jnp.max(work, axis=-1, keepdims=True)
        idx = jnp.min(jnp.where(work == m, lane, float(N_EXPERTS)), axis=-1, keepdims=True)
        hot = lane == idx
        work = jnp.where(hot, -jnp.inf, work)
        vals.append(m)
        idxs.append(idx)
        hots.append(hot)
    es = [jnp.exp(v - vals[0]) for v in vals]
    denom = es[0] + es[1] + es[2] + es[3]

    chosen = jnp.where(hots[0] | hots[1] | hots[2] | hots[3], 1.0, 0.0).astype(BF16)
    r_i = lax.broadcasted_iota(jnp.int32, (tm, tm), 0)
    c_i = lax.broadcasted_iota(jnp.int32, (tm, tm), 1)
    lower = jnp.where(c_i < r_i, 1.0, 0.0).astype(BF16)
    before = jnp.dot(lower, chosen, preferred_element_type=F32) + carry_ref[0:1, 0:N_EXPERTS]
    carry_ref[0:1, 0:N_EXPERTS] = (carry_ref[0:1, 0:N_EXPERTS]
                                   + jnp.sum(chosen.astype(F32), axis=0, keepdims=True))
    cnt_ref[...] = carry_ref[...]

    out_lane = lax.broadcasted_iota(jnp.int32, (tm, LANES), 1)
    ri = jnp.zeros((tm, LANES), jnp.int32)
    rw = jnp.zeros((tm, LANES), F32)
    for k in range(TOP_K):
        rank = jnp.sum(jnp.where(hots[k], before, 0.0), axis=-1, keepdims=True).astype(jnp.int32)
        ri = jnp.where(out_lane == k, idxs[k].astype(jnp.int32), ri)
        ri = jnp.where(out_lane == TOP_K + k, rank, ri)
        rw = jnp.where(out_lane == k, es[k] / denom, rw)
    ri_ref[...] = ri
    rw_ref[...] = rw


def _merge_route(x, branch, proj, wpo, wco, wao, wo, norm_ffn, w_router, b_router, name):
    t = x.shape[0]
    tm = MERGE_TM
    assert t % tm == 0
    gate_blk0 = D_MIX // D_MODEL
    assert D_MIX % D_MODEL == 0
    const = lambda i: (0, 0)
    return pl.pallas_call(
        _merge_route_kernel,
        grid=(t // tm,),
        in_specs=[
            pl.BlockSpec((tm, D_MODEL), lambda i: (i, 0)),
            pl.BlockSpec((tm, D_MODEL), lambda i: (i, 0)),
            pl.BlockSpec((tm, D_MODEL), lambda i: (i, gate_blk0)),
            pl.BlockSpec((tm, D_MODEL), lambda i: (i, gate_blk0 + 1)),
            pl.BlockSpec((tm, D_MODEL), lambda i: (i, gate_blk0 + 2)),
            pl.BlockSpec((D_POOL, D_MODEL), const, pipeline_mode=pl.Buffered(1)),
            pl.BlockSpec((D_CONV, D_MODEL), const, pipeline_mode=pl.Buffered(1)),
            pl.BlockSpec((D_XATTN, D_MODEL), const, pipeline_mode=pl.Buffered(1)),
            pl.BlockSpec((D_MODEL, D_MODEL), const, pipeline_mode=pl.Buffered(1)),
            pl.BlockSpec((1, D_MODEL), const),
            pl.BlockSpec((D_MODEL, N_EXPERTS), const),
            pl.BlockSpec((1, N_EXPERTS), const),
        ],
        out_specs=[
            pl.BlockSpec((tm, D_MODEL), lambda i: (i, 0)),
            pl.BlockSpec((tm, D_MODEL // 2), lambda i: (i, 0)),
            pl.BlockSpec((tm, LANES), lambda i: (i, 0)),
            pl.BlockSpec((tm, LANES), lambda i: (i, 0)),
            pl.BlockSpec((8, LANES), const),
        ],
        out_shape=[
            jax.ShapeDtypeStruct((t, D_MODEL), F32),
            jax.ShapeDtypeStruct((t, D_MODEL // 2), PACKED),
            jax.ShapeDtypeStruct((t, LANES), jnp.int32),
            jax.ShapeDtypeStruct((t, LANES), F32),
            jax.ShapeDtypeStruct((8, LANES), F32),
        ],
        scratch_shapes=[pltpu.VMEM((8, LANES), F32)],
        compiler_params=pltpu.CompilerParams(
            dimension_semantics=("arbitrary",), vmem_limit_bytes=VMEM_LIMIT),
        name=name,
    )(x, branch, proj, proj, proj, wpo, wco, wao, wo, norm_ffn.reshape(1, D_MODEL),
      w_router, b_router.reshape(1, N_EXPERTS))


def _sc_mesh():
    return plsc.VectorSubcoreMesh(core_axis_name="c", subcore_axis_name="s",
                                  num_cores=SC_CORES, num_subcores=SC_SUBCORES)


def _sc_worker_id():
    return lax.axis_index("s") * SC_CORES + lax.axis_index("c")


def _sc_dispatch(h2, dest, rows_out):
    t, w = h2.shape
    per_w = t // SC_WORKERS
    n_chunks = per_w // SC_CHUNK
    assert t == SC_WORKERS * n_chunks * SC_CHUNK
    idx = dest.reshape(SC_WORKERS, n_chunks, SC_CHUNK, TOP_K).transpose(0, 1, 3, 2)
    idx = idx.reshape(SC_WORKERS, n_chunks * TOP_K, SC_CHUNK)

    def body(h2_hbm, idx_hbm, xs_hbm, idx_v, rows_v, sems):
        base = _sc_worker_id() * per_w
        pltpu.sync_copy(idx_hbm.at[_sc_worker_id()], idx_v)
        pending = [[], []]
        for c in range(n_chunks):
            b = c % 2
            for d in pending[b]:
                d.wait()
            pltpu.sync_copy(h2_hbm.at[pl.ds(base + c * SC_CHUNK, SC_CHUNK)], rows_v.at[b])
            pending[b] = [pltpu.async_copy(rows_v.at[b], xs_hbm.at[idx_v.at[c * TOP_K + k]], sems.at[b])
                          for k in range(TOP_K)]
        for b in range(2):
            for d in pending[b]:
                d.wait()

    return pl.kernel(
        body,
        out_type=jax.ShapeDtypeStruct((rows_out, w), h2.dtype),
        mesh=_sc_mesh(),
        scratch_types=[pltpu.VMEM((n_chunks * TOP_K, SC_CHUNK), jnp.int32),
                       pltpu.VMEM((2, SC_CHUNK, w), h2.dtype),
                       pltpu.SemaphoreType.DMA((2,))],
        name="sc_dispatch",
    )(h2, idx)


def _sc_gather_rows(table, idx):
    n = idx.shape[0]
    w = table.shape[1]
    per_w = n // SC_WORKERS
    n_chunks = per_w // SC_CHUNK
    assert n == SC_WORKERS * n_chunks * SC_CHUNK
    idx3 = idx.reshape(SC_WORKERS, n_chunks, SC_CHUNK)

    def body(table_hbm, idx_hbm, out_hbm, idx_v, rows_v, gsem, wsems):
        base = _sc_worker_id() * per_w
        pltpu.sync_copy(idx_hbm.at[_sc_worker_id()], idx_v)
        pending = [None, None]
        for c in range(n_chunks):
            b = c % 2
            if pending[b] is not None:
                pending[b].wait()
            pltpu.async_copy(table_hbm.at[idx_v.at[c]], rows_v.at[b], gsem).wait()
            pending[b] = pltpu.async_copy(
                rows_v.at[b], out_hbm.at[pl.ds(base + c * SC_CHUNK, SC_CHUNK)], wsems.at[b])
        for b in range(2):
            if pending[b] is not None:
                pending[b].wait()

    return pl.kernel(
        body,
        out_type=jax.ShapeDtypeStruct((n, w), table.dtype),
        mesh=_sc_mesh(),
        scratch_types=[pltpu.VMEM((n_chunks, SC_CHUNK), jnp.int32),
                       pltpu.VMEM((2, SC_CHUNK, w), table.dtype),
                       pltpu.SemaphoreType.DMA,
                       pltpu.SemaphoreType.DMA((2,))],
        name="sc_gather_rows",
    )(table, idx3)


def _expert_row_loop(n, in_copy, out_copy, prepare_weights, compute):
    nchunk = (n + EXP_CHUNK - 1) // EXP_CHUNK

    def blocks_of(c):
        return jnp.minimum(EXP_CHUNK, n - c * EXP_CHUNK)

    def each_block(c, slot, copy, op):
        for k in range(EXP_CHUNK):
            @pl.when(k < blocks_of(c))
            def _():
                op(copy(c * EXP_CHUNK + k, slot, k))

    def start(d):
        d.start(priority=ROW_DMA_PRIORITY)

    def wait(d):
        d.wait()

    each_block(0, 0, in_copy, start)
    prepare_weights()

    def body(c, carry):
        slot = lax.rem(c, 2)
        each_block(c, slot, in_copy, wait)

        @pl.when(c + 1 < nchunk)
        def _():
            each_block(c + 1, 1 - slot, in_copy, start)

        @pl.when(c >= 2)
        def _():
            each_block(c - 2, slot, out_copy, wait)

        for m in range(1, EXP_CHUNK + 1):
            @pl.when(blocks_of(c) == m)
            def _():
                compute(slot, m)

        each_block(c, slot, out_copy, start)
        return carry

    lax.fori_loop(0, nchunk, body, 0)

    @pl.when(nchunk >= 2)
    def _():
        each_block(nchunk - 2, lax.rem(nchunk, 2), out_copy, wait)

    each_block(nchunk - 1, lax.rem(nchunk - 1, 2), out_copy, wait)


def _expert_in_kernel(row0_ref, nblk_ref, xs_hbm, *refs):
    wg_refs, wu_refs = refs[:W_SPLIT], refs[W_SPLIT:2 * W_SPLIT]
    bg_ref, bu_ref, act_hbm, wgb_ref, wub_ref, x_buf, a_buf, x_sem, a_sem = refs[2 * W_SPLIT:]
    kc = D_MODEL // W_SPLIT
    j = pl.program_id(0)
    e = pl.program_id(1)
    n = nblk_ref[e]
    tm, tn = EXP_TM, a_buf.shape[2]

    def rows(b):
        return pl.ds(pl.multiple_of(row0_ref[e] + b * tm, tm), tm)

    def x_copy(b, slot, k):
        return pltpu.make_async_copy(xs_hbm.at[rows(b)], x_buf.at[slot, pl.ds(k * tm, tm)], x_sem.at[slot])

    def a_copy(b, slot, k):
        return pltpu.make_async_copy(
            a_buf.at[slot, pl.ds(k * tm, tm)],
            act_hbm.at[rows(b), pl.ds(pl.multiple_of(j * tn, tn), tn)], a_sem.at[slot])

    def compute(slot, m):
        x = jnp.concatenate(_unpack_bf16_pairs(x_buf[slot, 0:m * tm]), axis=1).astype(BF16)
        g = jnp.dot(x, wgb_ref[...], preferred_element_type=F32) + bg_ref[0]
        up = jnp.dot(x, wub_ref[...], preferred_element_type=F32) + bu_ref[0]
        g = jnp.minimum(g, SWIGLU_LIMIT)
        up = jnp.clip(up, -SWIGLU_LIMIT, SWIGLU_LIMIT)
        a_buf[slot, 0:m * tm] = (g * _sigmoid(SWIGLU_ALPHA * g) * (up + 1.0)).astype(BF16)

    def prepare_weights():
        for c in range(W_SPLIT):
            wgb_ref[c * kc:(c + 1) * kc, :] = wg_refs[c][0].astype(BF16)
            wub_ref[c * kc:(c + 1) * kc, :] = wu_refs[c][0].astype(BF16)

    @pl.when(n > 0)
    def _():
        _expert_row_loop(n, x_copy, a_copy, prepare_weights, compute)


def _expert_in(row0, nblk, xs, w_in, b_in):
    rows = xs.shape[0]
    tm, tn = EXP_TM, EXP_IN_TN
    nj = D_FF // tn
    kc = D_MODEL // W_SPLIT

    def w_spec(c, col0):
        return pl.BlockSpec((1, kc, tn), lambda j, e, r0, nb: (e, c, col0 + j))

    grid_spec = pltpu.PrefetchScalarGridSpec(
        num_scalar_prefetch=2,
        grid=(nj, N_EXPERTS),
        in_specs=[
            pl.BlockSpec(memory_space=pl.ANY),
            *[w_spec(c, 0) for c in range(W_SPLIT)],
            *[w_spec(c, nj) for c in range(W_SPLIT)],
            pl.BlockSpec((1, 1, tn), lambda j, e, r0, nb: (e, 0, j)),
            pl.BlockSpec((1, 1, tn), lambda j, e, r0, nb: (e, 0, nj + j)),
        ],
        out_specs=pl.BlockSpec(memory_space=pl.ANY),
        scratch_shapes=[pltpu.VMEM((D_MODEL, tn), BF16), pltpu.VMEM((D_MODEL, tn), BF16),
                        pltpu.VMEM((2, EXP_CHUNK * tm, D_MODEL // 2), PACKED),
                        pltpu.VMEM((2, EXP_CHUNK * tm, tn), BF16),
                        pltpu.SemaphoreType.DMA((2,)), pltpu.SemaphoreType.DMA((2,))],
    )
    return pl.pallas_call(
        _expert_in_kernel,
        grid_spec=grid_spec,
        out_shape=jax.ShapeDtypeStruct((rows, D_FF), BF16),
        compiler_params=pltpu.CompilerParams(
            dimension_semantics=("arbitrary", "arbitrary"), vmem_limit_bytes=VMEM_LIMIT),
        name="expert_in",
    )(row0, nblk, xs, *([w_in] * (2 * W_SPLIT)), b_in, b_in)


def _expert_out_kernel(row0_ref, nblk_ref, act_hbm, *refs):
    w_refs = refs[:W_SPLIT]
    b_ref, ys_hbm, wb_ref, a_buf, y_buf, a_sem, y_sem = refs[W_SPLIT:]
    kc = D_FF // W_SPLIT
    j = pl.program_id(0)
    e = pl.program_id(1)
    n = nblk_ref[e]
    tm, tnp = EXP_TM, y_buf.shape[2]

    def rows(b):
        return pl.ds(pl.multiple_of(row0_ref[e] + b * tm, tm), tm)

    def a_copy(b, slot, k):
        return pltpu.make_async_copy(act_hbm.at[rows(b)], a_buf.at[slot, pl.ds(k * tm, tm)], a_sem.at[slot])

    def y_copy(b, slot, k):
        return pltpu.make_async_copy(
            y_buf.at[slot, pl.ds(k * tm, tm)],
            ys_hbm.at[rows(b), pl.ds(pl.multiple_of(j * tnp, tnp), tnp)], y_sem.at[slot])

    def prepare_weights():
        for c in range(W_SPLIT):
            wb_ref[c * kc:(c + 1) * kc, :] = w_refs[c][0].astype(BF16)

    def compute(slot, m):
        y = jnp.dot(a_buf[slot, 0:m * tm], wb_ref[...], preferred_element_type=F32) + b_ref[0]
        y_buf[slot, 0:m * tm] = _pack_bf16_pairs(y)

    @pl.when(n > 0)
    def _():
        _expert_row_loop(n, a_copy, y_copy, prepare_weights, compute)


def _expert_out(row0, nblk, act, w_out, b_out):
    rows = act.shape[0]
    tm, tn = EXP_TM, EXP_OUT_TN
    nj = D_MODEL // tn
    grid_spec = pltpu.PrefetchScalarGridSpec(
        num_scalar_prefetch=2,
        grid=(nj, N_EXPERTS),
        in_specs=[
            pl.BlockSpec(memory_space=pl.ANY),
            *[pl.BlockSpec((1, D_FF // W_SPLIT, tn), functools.partial(lambda c, j, e, r0, nb: (e, c, j), c))
              for c in range(W_SPLIT)],
            pl.BlockSpec((1, 1, tn), lambda j, e, r0, nb: (e, 0, j)),
        ],
        out_specs=pl.BlockSpec(memory_space=pl.ANY),
        scratch_shapes=[pltpu.VMEM((D_FF, tn), BF16),
                        pltpu.VMEM((2, EXP_CHUNK * tm, D_FF), BF16),
                        pltpu.VMEM((2, EXP_CHUNK * tm, tn // 2), PACKED),
                        pltpu.SemaphoreType.DMA((2,)), pltpu.SemaphoreType.DMA((2,))],
    )
    return pl.pallas_call(
        _expert_out_kernel,
        grid_spec=grid_spec,
        out_shape=jax.ShapeDtypeStruct((rows, D_MODEL // 2), PACKED),
        compiler_params=pltpu.CompilerParams(
            dimension_semantics=("arbitrary", "arbitrary"), vmem_limit_bytes=VMEM_LIMIT),
        name="expert_out",
    )(row0, nblk, act, *([w_out] * W_SPLIT), b_out)


def _combine_kernel(x1_ref, yg_ref, rw_ref, g_ref, o_ref):
    half = EXP_OUT_TN // 2
    pieces = []
    ssq = jnp.zeros((x1_ref.shape[0], 1), F32)
    for jt in range(D_MODEL // EXP_OUT_TN):
        acc_lo = x1_ref[:, jt * EXP_OUT_TN:jt * EXP_OUT_TN + half]
        acc_hi = x1_ref[:, jt * EXP_OUT_TN + half:(jt + 1) * EXP_OUT_TN]
        moe_lo = jnp.zeros_like(acc_lo)
        moe_hi = jnp.zeros_like(acc_hi)
        for k in range(TOP_K):
            lo, hi = _unpack_bf16_pairs(yg_ref[k, :, jt * half:(jt + 1) * half])
            moe_lo = moe_lo + lo * rw_ref[:, k:k + 1]
            moe_hi = moe_hi + hi * rw_ref[:, k:k + 1]
        for acc in (acc_lo + moe_lo, acc_hi + moe_hi):
            ssq = ssq + jnp.sum(acc * acc, axis=-1, keepdims=True)
            pieces.append(acc)
    inv = lax.rsqrt(ssq / D_MODEL + EPS)
    for n, acc in enumerate(pieces):
        o_ref[:, n * half:(n + 1) * half] = acc * inv * g_ref[:, n * half:(n + 1) * half]


def _combine(x1, yg, rw, gain, row_off, name):
    t = x1.shape[0]
    tm = COMB_TM
    assert t % tm == 0 and row_off % tm == 0
    off = row_off // tm
    return pl.pallas_call(
        _combine_kernel,
        grid=(t // tm,),
        in_specs=[
            pl.BlockSpec((tm, D_MODEL), lambda i: (i, 0)),
            pl.BlockSpec((TOP_K, tm, D_MODEL // 2), lambda i: (0, i + off, 0)),
            pl.BlockSpec((tm, LANES), lambda i: (i, 0)),
            pl.BlockSpec((1, D_MODEL), lambda i: (0, 0)),
        ],
        out_specs=pl.BlockSpec((tm, D_MODEL), lambda i: (i, 0)),
        out_shape=jax.ShapeDtypeStruct((t, D_MODEL), F32),
        compiler_params=pltpu.CompilerParams(
            dimension_semantics=("arbitrary",), vmem_limit_bytes=VMEM_LIMIT),
        name=name,
    )(x1, yg, rw, gain.reshape(1, D_MODEL))


def kernel(x_prompt, x_sample, state_pool, state_conv, cache_mem_k, cache_mem_v, mem_prompt,
           norm_mix, w_in, b_gate, w_pool_group, pool_scale, w_conv, mem_norm, w_mem_kv,
           w_pool_out, w_conv_out, w_attn_out, w_o, norm_ffn, w_router, b_router,
           w_exp_in, b_exp_in, w_exp_out, b_exp_out, final_norm):
    depth = norm_mix.shape[0]
    assert depth == 1
    l = 0
    bp, seq, _ = x_prompt.shape
    bs, ln, _ = x_sample.shape
    tp, ts = bp * seq, bs * ln

    kv = _norm_matmul(mem_prompt.reshape(bp * N_MEM, D_MODEL), mem_norm[l], w_mem_kv[l],
                      jnp.zeros((2 * D_XATTN,), F32), 2 * D_XATTN, "mem_kv")
    mk = kv[:, :D_XATTN].reshape(bp, N_MEM, D_XATTN)
    mv = kv[:, D_XATTN:].reshape(bp, N_MEM, D_XATTN)

    bias_in = jnp.concatenate([jnp.zeros((D_MIX,), F32), b_gate[l]])
    xp = x_prompt.reshape(tp, D_MODEL)
    xs_ = x_sample.reshape(ts, D_MODEL)
    proj_p = _norm_matmul(xp, norm_mix[l], w_in[l], bias_in, D_MIX, "proj_prompt")
    proj_s = _norm_matmul(xs_, norm_mix[l], w_in[l], bias_in, D_MIX, "proj_sample")

    br_p, zst_p = _mix_prompt(proj_p, mk, mv, w_pool_group[l], pool_scale[l], w_conv[l], bp, seq)
    br_s, zst_s = _mix_sample(proj_s, state_pool[l], state_conv[l],
                              cache_mem_k[l].reshape(bs, N_MEM * N_XHEADS, XHEAD_DIM),
                              cache_mem_v[l].reshape(bs, N_MEM * N_XHEADS, XHEAD_DIM),
                              w_pool_group[l], pool_scale[l], w_conv[l], bs, ln)

    wpo, wco, wao, wo = (w.astype(BF16) for w in (w_pool_out[l], w_conv_out[l], w_attn_out[l], w_o[l]))
    x1_p, h2_p, ri_p, rw_p, cnt_p = _merge_route(xp, br_p, proj_p, wpo, wco, wao, wo, norm_ffn[l],
                                                  w_router[l], b_router[l], "merge_route_prompt")
    x1_s, h2_s, ri_s, rw_s, cnt_s = _merge_route(xs_, br_s, proj_s, wpo, wco, wao, wo, norm_ffn[l],
                                                  w_router[l], b_router[l], "merge_route_sample")

    t_all = tp + ts
    n_assign = t_all * TOP_K
    nb_max = n_assign // EXP_TM + N_EXPERTS
    size_p = cnt_p[0, :N_EXPERTS].astype(jnp.int32)
    size_s = cnt_s[0, :N_EXPERTS].astype(jnp.int32)
    nblk = (size_p + size_s + EXP_TM - 1) // EXP_TM
    blk_end = jnp.cumsum(nblk)
    pad_start = (blk_end - nblk) * EXP_TM

    e_p, r_p = ri_p[:, :TOP_K], ri_p[:, TOP_K:2 * TOP_K]
    e_s, r_s = ri_s[:, :TOP_K], ri_s[:, TOP_K:2 * TOP_K]
    dest_p = pad_start[e_p] + r_p
    dest_s = pad_start[e_s] + size_p[e_s] + r_s
    dest = jnp.concatenate([dest_p, dest_s], axis=0)
    h2 = jnp.concatenate([h2_p, h2_s], axis=0)
    xs_sorted = _sc_dispatch(h2, dest, nb_max * EXP_TM)

    act = _expert_in(pad_start, nblk, xs_sorted, w_exp_in[l], b_exp_in[l].reshape(N_EXPERTS, 1, 2 * D_FF))
    ys = _expert_out(pad_start, nblk, act, w_exp_out[l], b_exp_out[l].reshape(N_EXPERTS, 1, D_MODEL))
    yg = _sc_gather_rows(ys, dest.T.reshape(-1)).reshape(TOP_K, t_all, D_MODEL // 2)

    y_p = _combine(x1_p, yg, rw_p, final_norm, 0, "combine_prompt")
    y_s = _combine(x1_s, yg, rw_s, final_norm, tp, "combine_sample")

    new_pool_p = proj_p[:, :D_POOL].reshape(bp, seq, D_POOL)[:, seq - POOL_STATE_LEN:]
    new_conv_p = zst_p[:, 8 - (CONV_WIDTH - 1):]
    u_s = proj_s[:, :D_POOL].reshape(bs, ln, D_POOL)
    new_pool_s = jnp.concatenate([state_pool[l], u_s], axis=1)[:, -POOL_STATE_LEN:]
    new_conv_s = zst_s[:, ln - (CONV_WIDTH - 1):]

    return (y_p.reshape(bp, seq, D_MODEL), y_s.reshape(bs, ln, D_MODEL),
            new_pool_p[None], new_conv_p[None],
            mk.reshape(1, bp, N_MEM, N_XHEADS, XHEAD_DIM), mv.reshape(1, bp, N_MEM, N_XHEADS, XHEAD_DIM),
            new_pool_s[None], new_conv_s[None])
```

```python
import functools

import jax
import jax.numpy as jnp
from jax import lax
from jax.experimental import pallas as pl
from jax.experimental.pallas import tpu as pltpu
from jax.experimental.pallas import tpu_sc as plsc

F32 = jnp.float32
BF16 = jnp.bfloat16
PACKED = jnp.int32

D_MODEL = 2048
POOL_WINDOWS = (2, 4, 8, 16)
POOL_GROUP_DIM = 128
D_POOL = 512
POOL_STATE_LEN = 15
D_CONV = 1024
CONV_WIDTH = 3
N_MEM = 256
N_XHEADS = 4
XHEAD_DIM = 128
D_XATTN = 512
N_BRANCH = 3
D_MIX = D_POOL + 3 * D_CONV + D_XATTN
D_IN_TOTAL = D_MIX + N_BRANCH * D_MODEL
N_EXPERTS = 32
TOP_K = 4
D_FF = D_MODEL
SWIGLU_LIMIT = 7.0
SWIGLU_ALPHA = 1.702
EPS = 1e-5

C_U = 0
C_V = D_POOL
C_B = D_POOL + D_CONV
C_C = D_POOL + 2 * D_CONV
C_Q = D_POOL + 3 * D_CONV

LANES = 128
HIST = 16

PROJ_TM = 1024
PROJ_TN = 1024
MIX_TM = 256
MIX_NS = 8
MERGE_TM = 256
EXP_TM = 256
EXP_CHUNK = 2
EXP_IN_TN = 1024
EXP_OUT_TN = 2048
W_SPLIT = 4
COMB_TM = 256
SC_CORES = 2
SC_SUBCORES = 16
SC_WORKERS = SC_CORES * SC_SUBCORES
SC_CHUNK = 32
VMEM_LIMIT = 56 * 1024 * 1024


def _sigmoid(x):
    return 1.0 / (1.0 + jnp.exp(-x))


def _rms(x, g):
    ms = jnp.mean(x * x, axis=-1, keepdims=True)
    return x * lax.rsqrt(ms + EPS) * g


def _pack_bf16_pairs(x):
    n = x.shape[1] // 2
    lo = lax.bitcast_convert_type(x[:, :n].astype(BF16).astype(F32), jnp.uint32)
    hi = lax.bitcast_convert_type(x[:, n:].astype(BF16).astype(F32), jnp.uint32)
    return lax.bitcast_convert_type((hi & jnp.uint32(0xFFFF0000)) | (lo >> 16), PACKED)


def _unpack_bf16_pairs(p):
    p = lax.bitcast_convert_type(p, jnp.uint32)
    lo = lax.bitcast_convert_type(p << 16, F32)
    hi = lax.bitcast_convert_type(p & jnp.uint32(0xFFFF0000), F32)
    return lo, hi


def _norm_matmul_kernel(x_ref, g_ref, *refs, act_from):
    w_refs = refs[:W_SPLIT]
    b_ref, o_ref, h_ref, wb_ref = refs[W_SPLIT:]
    kc = wb_ref.shape[0] // W_SPLIT
    j = pl.program_id(1)

    @pl.when(j == 0)
    def _():
        h_ref[...] = _rms(x_ref[...], g_ref[...]).astype(BF16)

    for c in range(W_SPLIT):
        wb_ref[c * kc:(c + 1) * kc, :] = w_refs[c][...].astype(BF16)
    acc = jnp.dot(h_ref[...], wb_ref[...], preferred_element_type=F32) + b_ref[...]

    @pl.when(j < act_from)
    def _():
        o_ref[...] = acc

    @pl.when(j >= act_from)
    def _():
        o_ref[...] = _sigmoid(acc)


def _norm_matmul(x, gain, w, bias, act_from_col, name):
    t, d = x.shape
    n = w.shape[1]
    tm = min(PROJ_TM, t)
    tn = PROJ_TN
    assert t % tm == 0 and n % tn == 0 and act_from_col % tn == 0
    return pl.pallas_call(
        functools.partial(_norm_matmul_kernel, act_from=act_from_col // tn),
        grid=(t // tm, n // tn),
        in_specs=[
            pl.BlockSpec((tm, d), lambda i, j: (i, 0)),
            pl.BlockSpec((1, d), lambda i, j: (0, 0)),
            *[pl.BlockSpec((d // W_SPLIT, tn), functools.partial(lambda c, i, j: (c, j), c))
              for c in range(W_SPLIT)],
            pl.BlockSpec((1, tn), lambda i, j: (0, j)),
        ],
        out_specs=pl.BlockSpec((tm, tn), lambda i, j: (i, j)),
        out_shape=jax.ShapeDtypeStruct((t, n), F32),
        scratch_shapes=[pltpu.VMEM((tm, d), BF16), pltpu.VMEM((d, tn), BF16)],
        compiler_params=pltpu.CompilerParams(
            dimension_semantics=("arbitrary", "arbitrary"), vmem_limit_bytes=VMEM_LIMIT),
        name=name,
    )(x, gain.reshape(1, d), *([w] * W_SPLIT), bias.reshape(1, n))


def _pool_project(pooled, wpg_ref, scale_ref, g):
    sl = slice(g * POOL_GROUP_DIM, (g + 1) * POOL_GROUP_DIM)
    y = jnp.dot(pooled.astype(BF16), wpg_ref[g].astype(BF16), preferred_element_type=F32)
    return y * scale_ref[:, sl]


def _softmax_rows(s):
    m = jnp.max(s, axis=-1, keepdims=True)
    e = jnp.exp(s - m)
    return e / jnp.sum(e, axis=-1, keepdims=True)


def _mix_prompt_kernel(cur_ref, prev_ref, mk_ref, mv_ref, wpg_ref, scale_ref, wconv_ref,
                       br_ref, zst_ref, extu_ref, extz_ref):
    t = pl.program_id(1)
    tm = cur_ref.shape[0]
    has_prev = t > 0

    u = cur_ref[:, C_U:C_U + D_POOL]
    extu_ref[0:HIST, :] = jnp.where(has_prev, prev_ref[:, C_U:C_U + D_POOL], 0.0)
    extu_ref[HIST:HIST + tm, :] = u
    pos = t * tm + lax.broadcasted_iota(jnp.int32, (tm, 1), 0)
    for g, w in enumerate(POOL_WINDOWS):
        sl = slice(g * POOL_GROUP_DIM, (g + 1) * POOL_GROUP_DIM)
        s = extu_ref[HIST:HIST + tm, sl]
        for k in range(1, w):
            s = s + extu_ref[HIST - k:HIST - k + tm, sl]
        cnt = jnp.minimum(w, pos + 1).astype(F32)
        pooled = s / cnt - extu_ref[HIST:HIST + tm, sl]
        br_ref[:, sl] = _pool_project(pooled, wpg_ref, scale_ref, g).astype(BF16)

    z = cur_ref[:, C_C:C_C + D_CONV] * cur_ref[:, C_V:C_V + D_CONV]
    zprev = prev_ref[:, C_C:C_C + D_CONV] * prev_ref[:, C_V:C_V + D_CONV]
    extz_ref[0:HIST, :] = jnp.where(has_prev, zprev, 0.0)
    extz_ref[HIST:HIST + tm, :] = z
    y = extz_ref[HIST - 2:HIST - 2 + tm, :] * wconv_ref[0:1, :]
    y = y + extz_ref[HIST - 1:HIST - 1 + tm, :] * wconv_ref[1:2, :]
    y = y + extz_ref[HIST:HIST + tm, :] * wconv_ref[2:3, :]
    br_ref[:, D_POOL:D_POOL + D_CONV] = (cur_ref[:, C_B:C_B + D_CONV] * y).astype(BF16)
    zst_ref[0] = extz_ref[HIST + tm - 8:HIST + tm, :]

    for h in range(N_XHEADS):
        sl = slice(h * XHEAD_DIM, (h + 1) * XHEAD_DIM)
        qh = cur_ref[:, C_Q + h * XHEAD_DIM:C_Q + (h + 1) * XHEAD_DIM].astype(BF16)
        kh = mk_ref[0, :, sl].astype(BF16)
        vh = mv_ref[0, :, sl].astype(BF16)
        s = lax.dot_general(qh, kh, (((1,), (1,)), ((), ())), preferred_element_type=F32)
        p = _softmax_rows(s * (XHEAD_DIM ** -0.5))
        o = jnp.dot(p.astype(BF16), vh, preferred_element_type=F32)
        c0 = D_POOL + D_CONV + h * XHEAD_DIM
        br_ref[:, c0:c0 + XHEAD_DIM] = o.astype(BF16)


def _mix_prompt(proj, mk, mv, wpg, scale, wconv, batch, seq):
    tm = MIX_TM
    nt = seq // tm
    assert seq % tm == 0 and tm % HIST == 0
    rpb = tm // HIST
    return pl.pallas_call(
        _mix_prompt_kernel,
        grid=(batch, nt),
        in_specs=[
            pl.BlockSpec((tm, D_MIX), lambda b, t: (b * nt + t, 0)),
            pl.BlockSpec((HIST, D_MIX), lambda b, t: (jnp.maximum((b * nt + t) * rpb - 1, 0), 0)),
            pl.BlockSpec((1, N_MEM, D_XATTN), lambda b, t: (b, 0, 0)),
            pl.BlockSpec((1, N_MEM, D_XATTN), lambda b, t: (b, 0, 0)),
            pl.BlockSpec((len(POOL_WINDOWS), POOL_GROUP_DIM, POOL_GROUP_DIM), lambda b, t: (0, 0, 0)),
            pl.BlockSpec((1, D_POOL), lambda b, t: (0, 0)),
            pl.BlockSpec((CONV_WIDTH, D_CONV), lambda b, t: (0, 0)),
        ],
        out_specs=[
            pl.BlockSpec((tm, D_MODEL), lambda b, t: (b * nt + t, 0)),
            pl.BlockSpec((1, 8, D_CONV), lambda b, t: (b, 0, 0)),
        ],
        out_shape=[
            jax.ShapeDtypeStruct((batch * seq, D_MODEL), BF16),
            jax.ShapeDtypeStruct((batch, 8, D_CONV), F32),
        ],
        scratch_shapes=[pltpu.VMEM((HIST + tm, D_POOL), F32), pltpu.VMEM((HIST + tm, D_CONV), F32)],
        compiler_params=pltpu.CompilerParams(
            dimension_semantics=("arbitrary", "arbitrary"), vmem_limit_bytes=VMEM_LIMIT),
        name="mix_prompt",
    )(proj, proj, mk, mv, wpg, scale.reshape(1, D_POOL), wconv)


def _mix_sample_kernel(cur_ref, sp_ref, sc_ref, k_ref, v_ref, wpg_ref, scale_ref, wconv_ref,
                       br_ref, zst_ref, extu_ref, extz_ref):
    ns, ln = zst_ref.shape[0], zst_ref.shape[1]
    rows = ns * ln

    extu_ref[:, HIST - POOL_STATE_LEN:HIST, :] = sp_ref[...]
    extu_ref[:, HIST:HIST + ln, :] = cur_ref[:, C_U:C_U + D_POOL].reshape(ns, ln, D_POOL)
    for g, w in enumerate(POOL_WINDOWS):
        sl = slice(g * POOL_GROUP_DIM, (g + 1) * POOL_GROUP_DIM)
        s = extu_ref[:, HIST:HIST + ln, sl]
        for k in range(1, w):
            s = s + extu_ref[:, HIST - k:HIST - k + ln, sl]
        pooled = s / float(w) - extu_ref[:, HIST:HIST + ln, sl]
        pooled = pooled.reshape(rows, POOL_GROUP_DIM)
        br_ref[:, sl] = _pool_project(pooled, wpg_ref, scale_ref, g).astype(BF16)

    z = cur_ref[:, C_C:C_C + D_CONV] * cur_ref[:, C_V:C_V + D_CONV]
    extz_ref[:, HIST - 2:HIST, :] = sc_ref[...]
    extz_ref[:, HIST:HIST + ln, :] = z.reshape(ns, ln, D_CONV)
    y = extz_ref[:, HIST - 2:HIST - 2 + ln, :] * wconv_ref[0:1, :]
    y = y + extz_ref[:, HIST - 1:HIST - 1 + ln, :] * wconv_ref[1:2, :]
    y = y + extz_ref[:, HIST:HIST + ln, :] * wconv_ref[2:3, :]
    br_ref[:, D_POOL:D_POOL + D_CONV] = (
        cur_ref[:, C_B:C_B + D_CONV] * y.reshape(rows, D_CONV)).astype(BF16)
    zst_ref[...] = extz_ref[:, HIST:HIST + ln, :]

    q3 = cur_ref[:, C_Q:C_Q + D_XATTN].reshape(ns, ln, D_XATTN)
    q4 = jnp.concatenate([q3[:, :, h * XHEAD_DIM:(h + 1) * XHEAD_DIM] for h in range(N_XHEADS)], axis=1)
    s = jnp.einsum("nqd,nkd->nqk", q4.astype(BF16), k_ref[...].astype(BF16),
                   preferred_element_type=F32)
    row_head = lax.broadcasted_iota(jnp.int32, (N_XHEADS * ln, N_XHEADS * N_MEM), 0) // ln
    col_head = lax.broadcasted_iota(jnp.int32, (N_XHEADS * ln, N_XHEADS * N_MEM), 1) % N_XHEADS
    s = jnp.where((row_head == col_head)[None], s * (XHEAD_DIM ** -0.5), -jnp.inf)
    p = _softmax_rows(s)
    r = jnp.einsum("nqk,nkd->nqd", p.astype(BF16), v_ref[...].astype(BF16),
                   preferred_element_type=F32)
    for h in range(N_XHEADS):
        c0 = D_POOL + D_CONV + h * XHEAD_DIM
        br_ref[:, c0:c0 + XHEAD_DIM] = r[:, h * ln:(h + 1) * ln, :].reshape(rows, XHEAD_DIM).astype(BF16)


def _mix_sample(proj, state_pool, state_conv, mem_k, mem_v, wpg, scale, wconv, nseq, ln):
    ns = MIX_NS
    rows = ns * ln
    assert nseq % ns == 0 and ln == 8
    return pl.pallas_call(
        _mix_sample_kernel,
        grid=(nseq // ns,),
        in_specs=[
            pl.BlockSpec((rows, D_MIX), lambda s: (s, 0)),
            pl.BlockSpec((ns, POOL_STATE_LEN, D_POOL), lambda s: (s, 0, 0)),
            pl.BlockSpec((ns, CONV_WIDTH - 1, D_CONV), lambda s: (s, 0, 0)),
            pl.BlockSpec((ns, N_MEM * N_XHEADS, XHEAD_DIM), lambda s: (s, 0, 0)),
            pl.BlockSpec((ns, N_MEM * N_XHEADS, XHEAD_DIM), lambda s: (s, 0, 0)),
            pl.BlockSpec((len(POOL_WINDOWS), POOL_GROUP_DIM, POOL_GROUP_DIM), lambda s: (0, 0, 0)),
            pl.BlockSpec((1, D_POOL), lambda s: (0, 0)),
            pl.BlockSpec((CONV_WIDTH, D_CONV), lambda s: (0, 0)),
        ],
        out_specs=[
            pl.BlockSpec((rows, D_MODEL), lambda s: (s, 0)),
            pl.BlockSpec((ns, ln, D_CONV), lambda s: (s, 0, 0)),
        ],
        out_shape=[
            jax.ShapeDtypeStruct((nseq * ln, D_MODEL), BF16),
            jax.ShapeDtypeStruct((nseq, ln, D_CONV), F32),
        ],
        scratch_shapes=[pltpu.VMEM((ns, HIST + ln, D_POOL), F32),
                        pltpu.VMEM((ns, HIST + ln, D_CONV), F32)],
        compiler_params=pltpu.CompilerParams(
            dimension_semantics=("arbitrary",), vmem_limit_bytes=VMEM_LIMIT),
        name="mix_sample",
    )(proj, state_pool, state_conv, mem_k, mem_v, wpg, scale.reshape(1, D_POOL), wconv)


def _merge_route_kernel(x_ref, br_ref, g0_ref, g1_ref, g2_ref, wpo_ref, wco_ref, wao_ref, wo_ref,
                        nf_ref, wr_ref, brt_ref,
                        x1_ref, h2_ref, ri_ref, rw_ref, cnt_ref, carry_ref):
    i = pl.program_id(0)
    tm = x_ref.shape[0]

    @pl.when(i == 0)
    def _():
        carry_ref[...] = jnp.zeros_like(carry_ref)

    merged = g0_ref[...] * jnp.dot(br_ref[:, 0:D_POOL], wpo_ref[...], preferred_element_type=F32)
    merged = merged + g1_ref[...] * jnp.dot(br_ref[:, D_POOL:D_POOL + D_CONV], wco_ref[...],
                                            preferred_element_type=F32)
    merged = merged + g2_ref[...] * jnp.dot(br_ref[:, D_POOL + D_CONV:D_MODEL], wao_ref[...],
                                            preferred_element_type=F32)
    x1 = x_ref[...] + jnp.dot(merged.astype(BF16), wo_ref[...], preferred_element_type=F32)
    x1_ref[...] = x1
    h2 = _rms(x1, nf_ref[...])
    h2_ref[...] = _pack_bf16_pairs(h2)

    logits = jnp.dot(h2, wr_ref[...], preferred_element_type=F32,
                     precision=lax.Precision.HIGHEST) + brt_ref[...]
    lane = lax.broadcasted_iota(jnp.int32, (tm, N_EXPERTS), 1).astype(F32)
    vals, idxs, hots = [], [], []
    work = logits
    for _ in range(TOP_K):
        m = jnp.max(work, axis=-1, keepdims=True)
        idx = jnp.min(jnp.where(work == m, lane, float(N_EXPERTS)), axis=-1, keepdims=True)
        hot = lane == idx
        work = jnp.where(hot, -jnp.inf, work)
        vals.append(m)
        idxs.append(idx)
        hots.append(hot)
    es = [jnp.exp(v - vals[0]) for v in vals]
    denom = es[0] + es[1] + es[2] + es[3]

    chosen = jnp.where(hots[0] | hots[1] | hots[2] | hots[3], 1.0, 0.0).astype(BF16)
    r_i = lax.broadcasted_iota(jnp.int32, (tm, tm), 0)
    c_i = lax.broadcasted_iota(jnp.int32, (tm, tm), 1)
    lower = jnp.where(c_i < r_i, 1.0, 0.0).astype(BF16)
    before = jnp.dot(lower, chosen, preferred_element_type=F32) + carry_ref[0:1, 0:N_EXPERTS]
    carry_ref[0:1, 0:N_EXPERTS] = (carry_ref[0:1, 0:N_EXPERTS]
                                   + jnp.sum(chosen.astype(F32), axis=0, keepdims=True))
    cnt_ref[...] = carry_ref[...]

    out_lane = lax.broadcasted_iota(jnp.int32, (tm, LANES), 1)
    ri = jnp.zeros((tm, LANES), jnp.int32)
    rw = jnp.zeros((tm, LANES), F32)
    for k in range(TOP_K):
        rank = jnp.sum(jnp.where(hots[k], before, 0.0), axis=-1, keepdims=True).astype(jnp.int32)
        ri = jnp.where(out_lane == k, idxs[k].astype(jnp.int32), ri)
        ri = jnp.where(out_lane == TOP_K + k, rank, ri)
        rw = jnp.where(out_lane == k, es[k] / denom, rw)
    ri_ref[...] = ri
    rw_ref[...] = rw


def _merge_route(x, branch, proj, wpo, wco, wao, wo, norm_ffn, w_router, b_router, name):
    t = x.shape[0]
    tm = MERGE_TM
    assert t % tm == 0
    gate_blk0 = D_MIX // D_MODEL
    assert D_MIX % D_MODEL == 0
    const = lambda i: (0, 0)
    return pl.pallas_call(
        _merge_route_kernel,
        grid=(t // tm,),
        in_specs=[
            pl.BlockSpec((tm, D_MODEL), lambda i: (i, 0)),
            pl.BlockSpec((tm, D_MODEL), lambda i: (i, 0)),
            pl.BlockSpec((tm, D_MODEL), lambda i: (i, gate_blk0)),
            pl.BlockSpec((tm, D_MODEL), lambda i: (i, gate_blk0 + 1)),
            pl.BlockSpec((tm, D_MODEL), lambda i: (i, gate_blk0 + 2)),
            pl.BlockSpec((D_POOL, D_MODEL), const, pipeline_mode=pl.Buffered(1)),
            pl.BlockSpec((D_CONV, D_MODEL), const, pipeline_mode=pl.Buffered(1)),
            pl.BlockSpec((D_XATTN, D_MODEL), const, pipeline_mode=pl.Buffered(1)),
            pl.BlockSpec((D_MODEL, D_MODEL), const, pipeline_mode=pl.Buffered(1)),
            pl.BlockSpec((1, D_MODEL), const),
            pl.BlockSpec((D_MODEL, N_EXPERTS), const),
            pl.BlockSpec((1, N_EXPERTS), const),
        ],
        out_specs=[
            pl.BlockSpec((tm, D_MODEL), lambda i: (i, 0)),
            pl.BlockSpec((tm, D_MODEL // 2), lambda i: (i, 0)),
            pl.BlockSpec((tm, LANES), lambda i: (i, 0)),
            pl.BlockSpec((tm, LANES), lambda i: (i, 0)),
            pl.BlockSpec((8, LANES), const),
        ],
        out_shape=[
            jax.ShapeDtypeStruct((t, D_MODEL), F32),
            jax.ShapeDtypeStruct((t, D_MODEL // 2), PACKED),
            jax.ShapeDtypeStruct((t, LANES), jnp.int32),
            jax.ShapeDtypeStruct((t, LANES), F32),
            jax.ShapeDtypeStruct((8, LANES), F32),
        ],
        scratch_shapes=[pltpu.VMEM((8, LANES), F32)],
        compiler_params=pltpu.CompilerParams(
            dimension_semantics=("arbitrary",), vmem_limit_bytes=VMEM_LIMIT),
        name=name,
    )(x, branch, proj, proj, proj, wpo, wco, wao, wo, norm_ffn.reshape(1, D_MODEL),
      w_router, b_router.reshape(1, N_EXPERTS))


def _sc_mesh():
    return plsc.VectorSubcoreMesh(core_axis_name="c", subcore_axis_name="s",
                                  num_cores=SC_CORES, num_subcores=SC_SUBCORES)


def _sc_worker_id():
    return lax.axis_index("s") * SC_CORES + lax.axis_index("c")


def _sc_dispatch(h2, dest, rows_out):
    t, w = h2.shape
    per_w = t // SC_WORKERS
    n_chunks = per_w // SC_CHUNK
    assert t == SC_WORKERS * n_chunks * SC_CHUNK
    idx = dest.reshape(SC_WORKERS, n_chunks, SC_CHUNK, TOP_K).transpose(0, 1, 3, 2)
    idx = idx.reshape(SC_WORKERS, n_chunks * TOP_K, SC_CHUNK)

    def body(h2_hbm, idx_hbm, xs_hbm, idx_v, rows_v, sems):
        base = _sc_worker_id() * per_w
        pltpu.sync_copy(idx_hbm.at[_sc_worker_id()], idx_v)
        pending = [[], []]
        for c in range(n_chunks):
            b = c % 2
            for d in pending[b]:
                d.wait()
            pltpu.sync_copy(h2_hbm.at[pl.ds(base + c * SC_CHUNK, SC_CHUNK)], rows_v.at[b])
            pending[b] = [pltpu.async_copy(rows_v.at[b], xs_hbm.at[idx_v.at[c * TOP_K + k]], sems.at[b])
                          for k in range(TOP_K)]
        for b in range(2):
            for d in pending[b]:
                d.wait()

    return pl.kernel(
        body,
        out_type=jax.ShapeDtypeStruct((rows_out, w), h2.dtype),
        mesh=_sc_mesh(),
        scratch_types=[pltpu.VMEM((n_chunks * TOP_K, SC_CHUNK), jnp.int32),
                       pltpu.VMEM((2, SC_CHUNK, w), h2.dtype),
                       pltpu.SemaphoreType.DMA((2,))],
        name="sc_dispatch",
    )(h2, idx)


def _sc_gather_rows(table, idx):
    n = idx.shape[0]
    w = table.shape[1]
    per_w = n // SC_WORKERS
    n_chunks = per_w // SC_CHUNK
    assert n == SC_WORKERS * n_chunks * SC_CHUNK
    idx3 = idx.reshape(SC_WORKERS, n_chunks, SC_CHUNK)

    def body(table_hbm, idx_hbm, out_hbm, idx_v, rows_v, gsem, wsems):
        base = _sc_worker_id() * per_w
        pltpu.sync_copy(idx_hbm.at[_sc_worker_id()], idx_v)
        pending = [None, None]
        for c in range(n_chunks):
            b = c % 2
            if pending[b] is not None:
                pending[b].wait()
            pltpu.async_copy(table_hbm.at[idx_v.at[c]], rows_v.at[b], gsem).wait()
            pending[b] = pltpu.async_copy(
                rows_v.at[b], out_hbm.at[pl.ds(base + c * SC_CHUNK, SC_CHUNK)], wsems.at[b])
        for b in range(2):
            if pending[b] is not None:
                pending[b].wait()

    return pl.kernel(
        body,
        out_type=jax.ShapeDtypeStruct((n, w), table.dtype),
        mesh=_sc_mesh(),
        scratch_types=[pltpu.VMEM((n_chunks, SC_CHUNK), jnp.int32),
                       pltpu.VMEM((2, SC_CHUNK, w), table.dtype),
                       pltpu.SemaphoreType.DMA,
                       pltpu.SemaphoreType.DMA((2,))],
        name="sc_gather_rows",
    )(table, idx3)


def _work_tables(nblk, pad_start, nj, max_items):
    nch = (nblk + EXP_CHUNK - 1) // EXP_CHUNK
    ch_end = jnp.cumsum(nch)
    ch_start = ch_end - nch
    per_pass = ch_end[-1]
    total = nj * per_pass
    item = jnp.arange(nj * max_items, dtype=jnp.int32)
    j = jnp.minimum(item // per_pass, nj - 1)
    q = item - j * per_pass
    e = jnp.minimum(jnp.sum((q[:, None] >= ch_end[None, :]).astype(jnp.int32), axis=1), N_EXPERTS - 1)
    c = q - ch_start[e]
    row = pad_start[e] + c * (EXP_CHUNK * EXP_TM)
    nb = jnp.clip(nblk[e] - c * EXP_CHUNK, 0, EXP_CHUNK)
    first = (c == 0).astype(jnp.int32)
    nxt = jnp.where(ch_end[e] < per_pass, j * per_pass + ch_end[e], (j + 1) * per_pass)
    has_next = nxt < total
    nxt = jnp.minimum(nxt, nj * max_items - 1)
    next_e = jnp.where(has_next, e[nxt], -1)
    next_j = jnp.where(has_next, j[nxt], -1)
    return e, j, row, nb, first, next_e, next_j, total.reshape(1)


def _grouped_pipeline(tabs, total_ref, weight_copies, cast_weights, in_copy, out_copy, compute):
    te, tj, _, tnb, tfirst, tne, tnj = tabs
    total = total_ref[0]

    def each_block(q, copy, op):
        slot = lax.rem(q, 2)
        for k in range(EXP_CHUNK):
            @pl.when(k < tnb[q])
            def _():
                op(copy(q, slot, k))

    def start(d):
        d.start()

    def wait(d):
        d.wait()

    for d in weight_copies(te[0], tj[0]):
        d.start()
    each_block(0, in_copy, start)

    def body(q, carry):
        each_block(q, in_copy, wait)

        @pl.when(q + 1 < total)
        def _():
            each_block(q + 1, in_copy, start)

        @pl.when(tfirst[q] == 1)
        def _():
            for d in weight_copies(te[q], tj[q]):
                d.wait()
            cast_weights()

            @pl.when(tne[q] >= 0)
            def _():
                for d in weight_copies(tne[q], tnj[q]):
                    d.start()

        @pl.when(q >= 2)
        def _():
            each_block(q - 2, out_copy, wait)

        for m in range(1, EXP_CHUNK + 1):
            @pl.when(tnb[q] == m)
            def _():
                compute(q, lax.rem(q, 2), m)

        each_block(q, out_copy, start)
        return carry

    lax.fori_loop(0, total, body, 0)

    @pl.when(total >= 2)
    def _():
        each_block(total - 2, out_copy, wait)

    each_block(total - 1, out_copy, wait)


def _expert_in_kernel(te, tj, trow, tnb, tfirst, tne, tnj, total_ref, xs_hbm, w_hbm, b_ref, act_hbm,
                      wstage, wb, x_buf, a_buf, w_sem, x_sem, a_sem):
    tm, tn = EXP_TM, a_buf.shape[2]
    nj = D_FF // tn

    def weight_copies(e, j):
        return [pltpu.make_async_copy(
            w_hbm.at[e, :, pl.ds(pl.multiple_of(half * D_FF + j * tn, tn), tn)], wstage.at[half], w_sem)
            for half in range(2)]

    def cast_weights():
        wb[...] = wstage[...].astype(BF16)

    def rows(q, k):
        return pl.ds(pl.multiple_of(trow[q] + k * tm, tm), tm)

    def x_copy(q, slot, k):
        return pltpu.make_async_copy(xs_hbm.at[rows(q, k)], x_buf.at[slot, pl.ds(k * tm, tm)], x_sem.at[slot])

    def a_copy(q, slot, k):
        return pltpu.make_async_copy(
            a_buf.at[slot, pl.ds(k * tm, tm)],
            act_hbm.at[rows(q, k), pl.ds(pl.multiple_of(tj[q] * tn, tn), tn)], a_sem.at[slot])

    def compute(q, slot, m):
        bias0 = te[q] * (2 * nj) + tj[q]
        x = jnp.concatenate(_unpack_bf16_pairs(x_buf[slot, 0:m * tm]), axis=1).astype(BF16)
        g = jnp.dot(x, wb[0], preferred_element_type=F32) + b_ref[bias0]
        up = jnp.dot(x, wb[1], preferred_element_type=F32) + b_ref[bias0 + nj]
        g = jnp.minimum(g, SWIGLU_LIMIT)
        up = jnp.clip(up, -SWIGLU_LIMIT, SWIGLU_LIMIT)
        a_buf[slot, 0:m * tm] = (g * _sigmoid(SWIGLU_ALPHA * g) * (up + 1.0)).astype(BF16)

    _grouped_pipeline((te, tj, trow, tnb, tfirst, tne, tnj), total_ref,
                      weight_copies, cast_weights, x_copy, a_copy, compute)


def _expert_in(tabs, xs, w_in, b_in):
    rows = xs.shape[0]
    tm, tn = EXP_TM, EXP_IN_TN
    nj = D_FF // tn
    bias = b_in.reshape(N_EXPERTS * 2 * nj, 1, tn)
    grid_spec = pltpu.PrefetchScalarGridSpec(
        num_scalar_prefetch=len(tabs),
        grid=(1,),
        in_specs=[
            pl.BlockSpec(memory_space=pl.ANY),
            pl.BlockSpec(memory_space=pl.ANY),
            pl.BlockSpec(bias.shape, lambda i, *_: (0, 0, 0), pipeline_mode=pl.Buffered(1)),
        ],
        out_specs=pl.BlockSpec(memory_space=pl.ANY),
        scratch_shapes=[pltpu.VMEM((2, D_MODEL, tn), F32), pltpu.VMEM((2, D_MODEL, tn), BF16),
                        pltpu.VMEM((2, EXP_CHUNK * tm, D_MODEL // 2), PACKED),
                        pltpu.VMEM((2, EXP_CHUNK * tm, tn), BF16),
                        pltpu.SemaphoreType.DMA, pltpu.SemaphoreType.DMA((2,)),
                        pltpu.SemaphoreType.DMA((2,))],
    )
    return pl.pallas_call(
        _expert_in_kernel,
        grid_spec=grid_spec,
        out_shape=jax.ShapeDtypeStruct((rows, D_FF), BF16),
        compiler_params=pltpu.CompilerParams(
            dimension_semantics=("arbitrary",), vmem_limit_bytes=VMEM_LIMIT),
        name="expert_in",
    )(*tabs, xs, w_in, bias)


def _expert_out_kernel(te, tj, trow, tnb, tfirst, tne, tnj, total_ref, act_hbm, w_hbm, b_ref, ys_hbm,
                       wstage, wb, a_buf, y_buf, w_sem, a_sem, y_sem):
    tm, tnp = EXP_TM, y_buf.shape[2]
    tn = 2 * tnp
    nj = D_MODEL // tn

    def weight_copies(e, j):
        return [pltpu.make_async_copy(
            w_hbm.at[e, :, pl.ds(pl.multiple_of(j * tn, tn), tn)], wstage, w_sem)]

    def cast_weights():
        wb[...] = wstage[...].astype(BF16)

    def rows(q, k):
        return pl.ds(pl.multiple_of(trow[q] + k * tm, tm), tm)

    def a_copy(q, slot, k):
        return pltpu.make_async_copy(act_hbm.at[rows(q, k)], a_buf.at[slot, pl.ds(k * tm, tm)], a_sem.at[slot])

    def y_copy(q, slot, k):
        return pltpu.make_async_copy(
            y_buf.at[slot, pl.ds(k * tm, tm)],
            ys_hbm.at[rows(q, k), pl.ds(pl.multiple_of(tj[q] * tnp, tnp), tnp)], y_sem.at[slot])

    def compute(q, slot, m):
        y = jnp.dot(a_buf[slot, 0:m * tm], wb[...], preferred_element_type=F32) + b_ref[te[q] * nj + tj[q]]
        y_buf[slot, 0:m * tm] = _pack_bf16_pairs(y)

    _grouped_pipeline((te, tj, trow, tnb, tfirst, tne, tnj), total_ref,
                      weight_copies, cast_weights, a_copy, y_copy, compute)


def _expert_out(tabs, act, w_out, b_out):
    rows = act.shape[0]
    tm, tn = EXP_TM, EXP_OUT_TN
    nj = D_MODEL // tn
    bias = b_out.reshape(N_EXPERTS * nj, 1, tn)
    grid_spec = pltpu.PrefetchScalarGridSpec(
        num_scalar_prefetch=len(tabs),
        grid=(1,),
        in_specs=[
            pl.BlockSpec(memory_space=pl.ANY),
            pl.BlockSpec(memory_space=pl.ANY),
            pl.BlockSpec(bias.shape, lambda i, *_: (0, 0, 0), pipeline_mode=pl.Buffered(1)),
        ],
        out_specs=pl.BlockSpec(memory_space=pl.ANY),
        scratch_shapes=[pltpu.VMEM((D_FF, tn), F32), pltpu.VMEM((D_FF, tn), BF16),
                        pltpu.VMEM((2, EXP_CHUNK * tm, D_FF), BF16),
                        pltpu.VMEM((2, EXP_CHUNK * tm, tn // 2), PACKED),
                        pltpu.SemaphoreType.DMA, pltpu.SemaphoreType.DMA((2,)),
                        pltpu.SemaphoreType.DMA((2,))],
    )
    return pl.pallas_call(
        _expert_out_kernel,
        grid_spec=grid_spec,
        out_shape=jax.ShapeDtypeStruct((rows, D_MODEL // 2), PACKED),
        compiler_params=pltpu.CompilerParams(
            dimension_semantics=("arbitrary",), vmem_limit_bytes=VMEM_LIMIT),
        name="expert_out",
    )(*tabs, act, w_out, bias)


def _combine_kernel(x1_ref, yg_ref, rw_ref, g_ref, o_ref):
    half = EXP_OUT_TN // 2
    pieces = []
    ssq = jnp.zeros((x1_ref.shape[0], 1), F32)
    for jt in range(D_MODEL // EXP_OUT_TN):
        acc_lo = x1_ref[:, jt * EXP_OUT_TN:jt * EXP_OUT_TN + half]
        acc_hi = x1_ref[:, jt * EXP_OUT_TN + half:(jt + 1) * EXP_OUT_TN]
        moe_lo = jnp.zeros_like(acc_lo)
        moe_hi = jnp.zeros_like(acc_hi)
        for k in range(TOP_K):
            lo, hi = _unpack_bf16_pairs(yg_ref[k, :, jt * half:(jt + 1) * half])
            moe_lo = moe_lo + lo * rw_ref[:, k:k + 1]
            moe_hi = moe_hi + hi * rw_ref[:, k:k + 1]
        for acc in (acc_lo + moe_lo, acc_hi + moe_hi):
            ssq = ssq + jnp.sum(acc * acc, axis=-1, keepdims=True)
            pieces.append(acc)
    inv = lax.rsqrt(ssq / D_MODEL + EPS)
    for n, acc in enumerate(pieces):
        o_ref[:, n * half:(n + 1) * half] = acc * inv * g_ref[:, n * half:(n + 1) * half]


def _combine(x1, yg, rw, gain, row_off, name):
    t = x1.shape[0]
    tm = COMB_TM
    assert t % tm == 0 and row_off % tm == 0
    off = row_off // tm
    return pl.pallas_call(
        _combine_kernel,
        grid=(t // tm,),
        in_specs=[
            pl.BlockSpec((tm, D_MODEL), lambda i: (i, 0)),
            pl.BlockSpec((TOP_K, tm, D_MODEL // 2), lambda i: (0, i + off, 0)),
            pl.BlockSpec((tm, LANES), lambda i: (i, 0)),
            pl.BlockSpec((1, D_MODEL), lambda i: (0, 0)),
        ],
        out_specs=pl.BlockSpec((tm, D_MODEL), lambda i: (i, 0)),
        out_shape=jax.ShapeDtypeStruct((t, D_MODEL), F32),
        compiler_params=pltpu.CompilerParams(
            dimension_semantics=("arbitrary",), vmem_limit_bytes=VMEM_LIMIT),
        name=name,
    )(x1, yg, rw, gain.reshape(1, D_MODEL))


def kernel(x_prompt, x_sample, state_pool, state_conv, cache_mem_k, cache_mem_v, mem_prompt,
           norm_mix, w_in, b_gate, w_pool_group, pool_scale, w_conv, mem_norm, w_mem_kv,
           w_pool_out, w_conv_out, w_attn_out, w_o, norm_ffn, w_router, b_router,
           w_exp_in, b_exp_in, w_exp_out, b_exp_out, final_norm):
    depth = norm_mix.shape[0]
    assert depth == 1
    l = 0
    bp, seq, _ = x_prompt.shape
    bs, ln, _ = x_sample.shape
    tp, ts = bp * seq, bs * ln

    kv = _norm_matmul(mem_prompt.reshape(bp * N_MEM, D_MODEL), mem_norm[l], w_mem_kv[l],
                      jnp.zeros((2 * D_XATTN,), F32), 2 * D_XATTN, "mem_kv")
    mk = kv[:, :D_XATTN].reshape(bp, N_MEM, D_XATTN)
    mv = kv[:, D_XATTN:].reshape(bp, N_MEM, D_XATTN)

    bias_in = jnp.concatenate([jnp.zeros((D_MIX,), F32), b_gate[l]])
    xp = x_prompt.reshape(tp, D_MODEL)
    xs_ = x_sample.reshape(ts, D_MODEL)
    proj_p = _norm_matmul(xp, norm_mix[l], w_in[l], bias_in, D_MIX, "proj_prompt")
    proj_s = _norm_matmul(xs_, norm_mix[l], w_in[l], bias_in, D_MIX, "proj_sample")

    br_p, zst_p = _mix_prompt(proj_p, mk, mv, w_pool_group[l], pool_scale[l], w_conv[l], bp, seq)
    br_s, zst_s = _mix_sample(proj_s, state_pool[l], state_conv[l],
                              cache_mem_k[l].reshape(bs, N_MEM * N_XHEADS, XHEAD_DIM),
                              cache_mem_v[l].reshape(bs, N_MEM * N_XHEADS, XHEAD_DIM),
                              w_pool_group[l], pool_scale[l], w_conv[l], bs, ln)

    wpo, wco, wao, wo = (w.astype(BF16) for w in (w_pool_out[l], w_conv_out[l], w_attn_out[l], w_o[l]))
    x1_p, h2_p, ri_p, rw_p, cnt_p = _merge_route(xp, br_p, proj_p, wpo, wco, wao, wo, norm_ffn[l],
                                                  w_router[l], b_router[l], "merge_route_prompt")
    x1_s, h2_s, ri_s, rw_s, cnt_s = _merge_route(xs_, br_s, proj_s, wpo, wco, wao, wo, norm_ffn[l],
                                                  w_router[l], b_router[l], "merge_route_sample")

    t_all = tp + ts
    n_assign = t_all * TOP_K
    nb_max = n_assign // EXP_TM + N_EXPERTS
    size_p = cnt_p[0, :N_EXPERTS].astype(jnp.int32)
    size_s = cnt_s[0, :N_EXPERTS].astype(jnp.int32)
    nblk = (size_p + size_s + EXP_TM - 1) // EXP_TM
    blk_end = jnp.cumsum(nblk)
    pad_start = (blk_end - nblk) * EXP_TM

    e_p, r_p = ri_p[:, :TOP_K], ri_p[:, TOP_K:2 * TOP_K]
    e_s, r_s = ri_s[:, :TOP_K], ri_s[:, TOP_K:2 * TOP_K]
    dest_p = pad_start[e_p] + r_p
    dest_s = pad_start[e_s] + size_p[e_s] + r_s
    dest = jnp.concatenate([dest_p, dest_s], axis=0)
    h2 = jnp.concatenate([h2_p, h2_s], axis=0)
    xs_sorted = _sc_dispatch(h2, dest, nb_max * EXP_TM)

    max_items = nb_max // EXP_CHUNK + N_EXPERTS
    act = _expert_in(_work_tables(nblk, pad_start, D_FF // EXP_IN_TN, max_items),
                     xs_sorted, w_exp_in[l], b_exp_in[l])
    ys = _expert_out(_work_tables(nblk, pad_start, D_MODEL // EXP_OUT_TN, max_items),
                     act, w_exp_out[l], b_exp_out[l])
    yg = _sc_gather_rows(ys, dest.T.reshape(-1)).reshape(TOP_K, t_all, D_MODEL // 2)

    y_p = _combine(x1_p, yg, rw_p, final_norm, 0, "combine_prompt")
    y_s = _combine(x1_s, yg, rw_s, final_norm, tp, "combine_sample")

    new_pool_p = proj_p[:, :D_POOL].reshape(bp, seq, D_POOL)[:, seq - POOL_STATE_LEN:]
    new_conv_p = zst_p[:, 8 - (CONV_WIDTH - 1):]
    u_s = proj_s[:, :D_POOL].reshape(bs, ln, D_POOL)
    new_pool_s = jnp.concatenate([state_pool[l], u_s], axis=1)[:, -POOL_STATE_LEN:]
    new_conv_s = zst_s[:, ln - (CONV_WIDTH - 1):]

    return (y_p.reshape(bp, seq, D_MODEL), y_s.reshape(bs, ln, D_MODEL),
            new_pool_p[None], new_conv_p[None],
            mk.reshape(1, bp, N_MEM, N_XHEADS, XHEAD_DIM), mv.reshape(1, bp, N_MEM, N_XHEADS, XHEAD_DIM),
            new_pool_s[None], new_conv_s[None])
```

```python
import functools

import jax
import jax.numpy as jnp
from jax import lax
from jax.experimental import pallas as pl
from jax.experimental.pallas import tpu as pltpu
from jax.experimental.pallas import tpu_sc as plsc

F32 = jnp.float32
BF16 = jnp.bfloat16
PACKED = jnp.int32

D_MODEL = 2048
POOL_WINDOWS = (2, 4, 8, 16)
POOL_GROUP_DIM = 128
D_POOL = 512
POOL_STATE_LEN = 15
D_CONV = 1024
CONV_WIDTH = 3
N_MEM = 256
N_XHEADS = 4
XHEAD_DIM = 128
D_XATTN = 512
N_BRANCH = 3
D_MIX = D_POOL + 3 * D_CONV + D_XATTN
D_IN_TOTAL = D_MIX + N_BRANCH * D_MODEL
N_EXPERTS = 32
TOP_K = 4
D_FF = D_MODEL
SWIGLU_LIMIT = 7.0
SWIGLU_ALPHA = 1.702
EPS = 1e-5

C_U = 0
C_V = D_POOL
C_B = D_POOL + D_CONV
C_C = D_POOL + 2 * D_CONV
C_Q = D_POOL + 3 * D_CONV

LANES = 128
HIST = 16

PROJ_TM = 1024
PROJ_TN = 1024
MIX_TM = 256
MIX_NS = 8
MERGE_TM = 256
ROUTE_TM = 1024
EXP_TM = 256
EXP_CHUNK = 2
EXP_IN_TN = 1024
EXP_OUT_TN = 2048
COMB_TM = 256
SC_CORES = 2
SC_SUBCORES = 16
SC_WORKERS = SC_CORES * SC_SUBCORES
SC_CHUNK = 32
VMEM_LIMIT = 56 * 1024 * 1024


def _sigmoid(x):
    return 1.0 / (1.0 + jnp.exp(-x))


def _rms(x, g):
    ms = jnp.mean(x * x, axis=-1, keepdims=True)
    return x * lax.rsqrt(ms + EPS) * g


def _pack_bf16_pairs(x):
    n = x.shape[1] // 2
    lo = lax.bitcast_convert_type(x[:, :n].astype(BF16).astype(F32), jnp.uint32)
    hi = lax.bitcast_convert_type(x[:, n:].astype(BF16).astype(F32), jnp.uint32)
    return lax.bitcast_convert_type((hi & jnp.uint32(0xFFFF0000)) | (lo >> 16), PACKED)


def _unpack_bf16_pairs(p):
    p = lax.bitcast_convert_type(p, jnp.uint32)
    lo = lax.bitcast_convert_type(p << 16, F32)
    hi = lax.bitcast_convert_type(p & jnp.uint32(0xFFFF0000), F32)
    return lo, hi


def _norm_matmul_kernel(x_ref, g_ref, w_ref, b_ref, o_ref, h_ref, *, act_from):
    j = pl.program_id(1)

    @pl.when(j == 0)
    def _():
        h_ref[...] = _rms(x_ref[...], g_ref[...]).astype(BF16)

    acc = jnp.dot(h_ref[...], w_ref[...], preferred_element_type=F32) + b_ref[...]

    @pl.when(j < act_from)
    def _():
        o_ref[...] = acc

    @pl.when(j >= act_from)
    def _():
        o_ref[...] = _sigmoid(acc)


def _norm_matmul(x, gain, w, bias, act_from_col, name):
    t, d = x.shape
    n = w.shape[1]
    tm = min(PROJ_TM, t)
    tn = PROJ_TN
    assert t % tm == 0 and n % tn == 0 and act_from_col % tn == 0
    return pl.pallas_call(
        functools.partial(_norm_matmul_kernel, act_from=act_from_col // tn),
        grid=(t // tm, n // tn),
        in_specs=[
            pl.BlockSpec((tm, d), lambda i, j: (i, 0)),
            pl.BlockSpec((1, d), lambda i, j: (0, 0)),
            pl.BlockSpec((d, tn), lambda i, j: (0, j)),
            pl.BlockSpec((1, tn), lambda i, j: (0, j)),
        ],
        out_specs=pl.BlockSpec((tm, tn), lambda i, j: (i, j)),
        out_shape=jax.ShapeDtypeStruct((t, n), F32),
        scratch_shapes=[pltpu.VMEM((tm, d), BF16)],
        compiler_params=pltpu.CompilerParams(
            dimension_semantics=("arbitrary", "arbitrary"), vmem_limit_bytes=VMEM_LIMIT),
        name=name,
    )(x, gain.reshape(1, d), w, bias.reshape(1, n))


def _pool_project(pooled, wpg_ref, scale_ref, g):
    sl = slice(g * POOL_GROUP_DIM, (g + 1) * POOL_GROUP_DIM)
    y = jnp.dot(pooled.astype(BF16), wpg_ref[g].astype(BF16), preferred_element_type=F32)
    return y * scale_ref[:, sl]


def _softmax_rows(s):
    m = jnp.max(s, axis=-1, keepdims=True)
    e = jnp.exp(s - m)
    return e / jnp.sum(e, axis=-1, keepdims=True)


def _mix_prompt_kernel(cur_ref, prev_ref, mk_ref, mv_ref, wpg_ref, scale_ref, wconv_ref,
                       br_ref, zst_ref, extu_ref, extz_ref):
    t = pl.program_id(1)
    tm = cur_ref.shape[0]
    has_prev = t > 0

    u = cur_ref[:, C_U:C_U + D_POOL]
    extu_ref[0:HIST, :] = jnp.where(has_prev, prev_ref[:, C_U:C_U + D_POOL], 0.0)
    extu_ref[HIST:HIST + tm, :] = u
    pos = t * tm + lax.broadcasted_iota(jnp.int32, (tm, 1), 0)
    for g, w in enumerate(POOL_WINDOWS):
        sl = slice(g * POOL_GROUP_DIM, (g + 1) * POOL_GROUP_DIM)
        s = extu_ref[HIST:HIST + tm, sl]
        for k in range(1, w):
            s = s + extu_ref[HIST - k:HIST - k + tm, sl]
        cnt = jnp.minimum(w, pos + 1).astype(F32)
        pooled = s / cnt - extu_ref[HIST:HIST + tm, sl]
        br_ref[:, sl] = _pool_project(pooled, wpg_ref, scale_ref, g).astype(BF16)

    z = cur_ref[:, C_C:C_C + D_CONV] * cur_ref[:, C_V:C_V + D_CONV]
    zprev = prev_ref[:, C_C:C_C + D_CONV] * prev_ref[:, C_V:C_V + D_CONV]
    extz_ref[0:HIST, :] = jnp.where(has_prev, zprev, 0.0)
    extz_ref[HIST:HIST + tm, :] = z
    y = extz_ref[HIST - 2:HIST - 2 + tm, :] * wconv_ref[0:1, :]
    y = y + extz_ref[HIST - 1:HIST - 1 + tm, :] * wconv_ref[1:2, :]
    y = y + extz_ref[HIST:HIST + tm, :] * wconv_ref[2:3, :]
    br_ref[:, D_POOL:D_POOL + D_CONV] = (cur_ref[:, C_B:C_B + D_CONV] * y).astype(BF16)
    zst_ref[0] = extz_ref[HIST + tm - 8:HIST + tm, :]

    for h in range(N_XHEADS):
        sl = slice(h * XHEAD_DIM, (h + 1) * XHEAD_DIM)
        qh = cur_ref[:, C_Q + h * XHEAD_DIM:C_Q + (h + 1) * XHEAD_DIM].astype(BF16)
        kh = mk_ref[0, :, sl].astype(BF16)
        vh = mv_ref[0, :, sl].astype(BF16)
        s = lax.dot_general(qh, kh, (((1,), (1,)), ((), ())), preferred_element_type=F32)
        p = _softmax_rows(s * (XHEAD_DIM ** -0.5))
        o = jnp.dot(p.astype(BF16), vh, preferred_element_type=F32)
        c0 = D_POOL + D_CONV + h * XHEAD_DIM
        br_ref[:, c0:c0 + XHEAD_DIM] = o.astype(BF16)


def _mix_prompt(proj, mk, mv, wpg, scale, wconv, batch, seq):
    tm = MIX_TM
    nt = seq // tm
    assert seq % tm == 0 and tm % HIST == 0
    rpb = tm // HIST
    return pl.pallas_call(
        _mix_prompt_kernel,
        grid=(batch, nt),
        in_specs=[
            pl.BlockSpec((tm, D_MIX), lambda b, t: (b * nt + t, 0)),
            pl.BlockSpec((HIST, D_MIX), lambda b, t: (jnp.maximum((b * nt + t) * rpb - 1, 0), 0)),
            pl.BlockSpec((1, N_MEM, D_XATTN), lambda b, t: (b, 0, 0)),
            pl.BlockSpec((1, N_MEM, D_XATTN), lambda b, t: (b, 0, 0)),
            pl.BlockSpec((len(POOL_WINDOWS), POOL_GROUP_DIM, POOL_GROUP_DIM), lambda b, t: (0, 0, 0)),
            pl.BlockSpec((1, D_POOL), lambda b, t: (0, 0)),
            pl.BlockSpec((CONV_WIDTH, D_CONV), lambda b, t: (0, 0)),
        ],
        out_specs=[
            pl.BlockSpec((tm, D_MODEL), lambda b, t: (b * nt + t, 0)),
            pl.BlockSpec((1, 8, D_CONV), lambda b, t: (b, 0, 0)),
        ],
        out_shape=[
            jax.ShapeDtypeStruct((batch * seq, D_MODEL), BF16),
            jax.ShapeDtypeStruct((batch, 8, D_CONV), F32),
        ],
        scratch_shapes=[pltpu.VMEM((HIST + tm, D_POOL), F32), pltpu.VMEM((HIST + tm, D_CONV), F32)],
        compiler_params=pltpu.CompilerParams(
            dimension_semantics=("arbitrary", "arbitrary"), vmem_limit_bytes=VMEM_LIMIT),
        name="mix_prompt",
    )(proj, proj, mk, mv, wpg, scale.reshape(1, D_POOL), wconv)


def _mix_sample_kernel(cur_ref, sp_ref, sc_ref, k_ref, v_ref, wpg_ref, scale_ref, wconv_ref,
                       br_ref, zst_ref, extu_ref, extz_ref):
    ns, ln = zst_ref.shape[0], zst_ref.shape[1]
    rows = ns * ln

    extu_ref[:, HIST - POOL_STATE_LEN:HIST, :] = sp_ref[...]
    extu_ref[:, HIST:HIST + ln, :] = cur_ref[:, C_U:C_U + D_POOL].reshape(ns, ln, D_POOL)
    for g, w in enumerate(POOL_WINDOWS):
        sl = slice(g * POOL_GROUP_DIM, (g + 1) * POOL_GROUP_DIM)
        s = extu_ref[:, HIST:HIST + ln, sl]
        for k in range(1, w):
            s = s + extu_ref[:, HIST - k:HIST - k + ln, sl]
        pooled = s / float(w) - extu_ref[:, HIST:HIST + ln, sl]
        pooled = pooled.reshape(rows, POOL_GROUP_DIM)
        br_ref[:, sl] = _pool_project(pooled, wpg_ref, scale_ref, g).astype(BF16)

    z = cur_ref[:, C_C:C_C + D_CONV] * cur_ref[:, C_V:C_V + D_CONV]
    extz_ref[:, HIST - 2:HIST, :] = sc_ref[...]
    extz_ref[:, HIST:HIST + ln, :] = z.reshape(ns, ln, D_CONV)
    y = extz_ref[:, HIST - 2:HIST - 2 + ln, :] * wconv_ref[0:1, :]
    y = y + extz_ref[:, HIST - 1:HIST - 1 + ln, :] * wconv_ref[1:2, :]
    y = y + extz_ref[:, HIST:HIST + ln, :] * wconv_ref[2:3, :]
    br_ref[:, D_POOL:D_POOL + D_CONV] = (
        cur_ref[:, C_B:C_B + D_CONV] * y.reshape(rows, D_CONV)).astype(BF16)
    zst_ref[...] = extz_ref[:, HIST:HIST + ln, :]

    q3 = cur_ref[:, C_Q:C_Q + D_XATTN].reshape(ns, ln, D_XATTN)
    q4 = jnp.concatenate([q3[:, :, h * XHEAD_DIM:(h + 1) * XHEAD_DIM] for h in range(N_XHEADS)], axis=1)
    s = jnp.einsum("nqd,nkd->nqk", q4.astype(BF16), k_ref[...].astype(BF16),
                   preferred_element_type=F32)
    row_head = lax.broadcasted_iota(jnp.int32, (N_XHEADS * ln, N_XHEADS * N_MEM), 0) // ln
    col_head = lax.broadcasted_iota(jnp.int32, (N_XHEADS * ln, N_XHEADS * N_MEM), 1) % N_XHEADS
    s = jnp.where((row_head == col_head)[None], s * (XHEAD_DIM ** -0.5), -jnp.inf)
    p = _softmax_rows(s)
    r = jnp.einsum("nqk,nkd->nqd", p.astype(BF16), v_ref[...].astype(BF16),
                   preferred_element_type=F32)
    for h in range(N_XHEADS):
        c0 = D_POOL + D_CONV + h * XHEAD_DIM
        br_ref[:, c0:c0 + XHEAD_DIM] = r[:, h * ln:(h + 1) * ln, :].reshape(rows, XHEAD_DIM).astype(BF16)


def _mix_sample(proj, state_pool, state_conv, mem_k, mem_v, wpg, scale, wconv, nseq, ln):
    ns = MIX_NS
    rows = ns * ln
    assert nseq % ns == 0 and ln == 8
    return pl.pallas_call(
        _mix_sample_kernel,
        grid=(nseq // ns,),
        in_specs=[
            pl.BlockSpec((rows, D_MIX), lambda s: (s, 0)),
            pl.BlockSpec((ns, POOL_STATE_LEN, D_POOL), lambda s: (s, 0, 0)),
            pl.BlockSpec((ns, CONV_WIDTH - 1, D_CONV), lambda s: (s, 0, 0)),
            pl.BlockSpec((ns, N_MEM * N_XHEADS, XHEAD_DIM), lambda s: (s, 0, 0)),
            pl.BlockSpec((ns, N_MEM * N_XHEADS, XHEAD_DIM), lambda s: (s, 0, 0)),
            pl.BlockSpec((len(POOL_WINDOWS), POOL_GROUP_DIM, POOL_GROUP_DIM), lambda s: (0, 0, 0)),
            pl.BlockSpec((1, D_POOL), lambda s: (0, 0)),
            pl.BlockSpec((CONV_WIDTH, D_CONV), lambda s: (0, 0)),
        ],
        out_specs=[
            pl.BlockSpec((rows, D_MODEL), lambda s: (s, 0)),
            pl.BlockSpec((ns, ln, D_CONV), lambda s: (s, 0, 0)),
        ],
        out_shape=[
            jax.ShapeDtypeStruct((nseq * ln, D_MODEL), BF16),
            jax.ShapeDtypeStruct((nseq, ln, D_CONV), F32),
        ],
        scratch_shapes=[pltpu.VMEM((ns, HIST + ln, D_POOL), F32),
                        pltpu.VMEM((ns, HIST + ln, D_CONV), F32)],
        compiler_params=pltpu.CompilerParams(
            dimension_semantics=("arbitrary",), vmem_limit_bytes=VMEM_LIMIT),
        name="mix_sample",
    )(proj, state_pool, state_conv, mem_k, mem_v, wpg, scale.reshape(1, D_POOL), wconv)


def _merge_route_kernel(x_ref, br_ref, g0_ref, g1_ref, g2_ref, wpo_ref, wco_ref, wao_ref, wo_ref,
                        nf_ref, wrh_ref, wrl_ref, brt_ref,
                        x1_ref, h2_ref, ri_ref, rw_ref, cnt_ref, carry_ref):
    i = pl.program_id(0)
    tm = x_ref.shape[0]

    @pl.when(i == 0)
    def _():
        carry_ref[...] = jnp.zeros_like(carry_ref)

    merged = g0_ref[...] * jnp.dot(br_ref[:, 0:D_POOL], wpo_ref[...], preferred_element_type=F32)
    merged = merged + g1_ref[...] * jnp.dot(br_ref[:, D_POOL:D_POOL + D_CONV], wco_ref[...],
                                            preferred_element_type=F32)
    merged = merged + g2_ref[...] * jnp.dot(br_ref[:, D_POOL + D_CONV:D_MODEL], wao_ref[...],
                                            preferred_element_type=F32)
    x1 = x_ref[...] + jnp.dot(merged.astype(BF16), wo_ref[...], preferred_element_type=F32)
    x1_ref[...] = x1
    h2 = _rms(x1, nf_ref[...])
    h2_ref[...] = _pack_bf16_pairs(h2)

    h2_hi = h2.astype(BF16)
    h2_lo = (h2 - h2_hi.astype(F32)).astype(BF16)
    logits = (jnp.dot(h2_hi, wrh_ref[...], preferred_element_type=F32)
              + jnp.dot(h2_hi, wrl_ref[...], preferred_element_type=F32)
              + jnp.dot(h2_lo, wrh_ref[...], preferred_element_type=F32)) + brt_ref[...]
    lane = lax.broadcasted_iota(jnp.int32, (tm, N_EXPERTS), 1).astype(F32)
    vals, idxs, hots = [], [], []
    work = logits
    for _ in range(TOP_K):
        m = jnp.max(work, axis=-1, keepdims=True)
        idx = jnp.min(jnp.where(work == m, lane, float(N_EXPERTS)), axis=-1, keepdims=True)
        hot = lane == idx
        work = jnp.where(hot, -jnp.inf, work)
        vals.append(m)
        idxs.append(idx)
        hots.append(hot)
    es = [jnp.exp(v - vals[0]) for v in vals]
    denom = es[0] + es[1] + es[2] + es[3]

    chosen = jnp.where(hots[0] | hots[1] | hots[2] | hots[3], 1.0, 0.0).astype(BF16)
    r_i = lax.broadcasted_iota(jnp.int32, (tm, tm), 0)
    c_i = lax.broadcasted_iota(jnp.int32, (tm, tm), 1)
    lower = jnp.where(c_i < r_i, 1.0, 0.0).astype(BF16)
    before = jnp.dot(lower, chosen, preferred_element_type=F32) + carry_ref[0:1, 0:N_EXPERTS]
    carry_ref[0:1, 0:N_EXPERTS] = (carry_ref[0:1, 0:N_EXPERTS]
                                   + jnp.sum(chosen.astype(F32), axis=0, keepdims=True))
    cnt_ref[...] = carry_ref[...]

    out_lane = lax.broadcasted_iota(jnp.int32, (tm, LANES), 1)
    ri = jnp.zeros((tm, LANES), jnp.int32)
    rw = jnp.zeros((tm, LANES), F32)
    for k in range(TOP_K):
        rank = jnp.sum(jnp.where(hots[k], before, 0.0), axis=-1, keepdims=True).astype(jnp.int32)
        ri = jnp.where(out_lane == k, idxs[k].astype(jnp.int32), ri)
        ri = jnp.where(out_lane == TOP_K + k, rank, ri)
        rw = jnp.where(out_lane == k, es[k] / denom, rw)
    ri_ref[...] = ri
    rw_ref[...] = rw


def _merge_route(x, branch, proj, wpo, wco, wao, wo, norm_ffn, w_router, b_router, name):
    t = x.shape[0]
    tm = MERGE_TM
    assert t % tm == 0
    gate_blk0 = D_MIX // D_MODEL
    assert D_MIX % D_MODEL == 0
    const = lambda i: (0, 0)
    wr_hi = w_router.astype(BF16)
    wr_lo = (w_router - wr_hi.astype(F32)).astype(BF16)
    return pl.pallas_call(
        _merge_route_kernel,
        grid=(t // tm,),
        in_specs=[
            pl.BlockSpec((tm, D_MODEL), lambda i: (i, 0)),
            pl.BlockSpec((tm, D_MODEL), lambda i: (i, 0)),
            pl.BlockSpec((tm, D_MODEL), lambda i: (i, gate_blk0)),
            pl.BlockSpec((tm, D_MODEL), lambda i: (i, gate_blk0 + 1)),
            pl.BlockSpec((tm, D_MODEL), lambda i: (i, gate_blk0 + 2)),
            pl.BlockSpec((D_POOL, D_MODEL), const, pipeline_mode=pl.Buffered(1)),
            pl.BlockSpec((D_CONV, D_MODEL), const, pipeline_mode=pl.Buffered(1)),
            pl.BlockSpec((D_XATTN, D_MODEL), const, pipeline_mode=pl.Buffered(1)),
            pl.BlockSpec((D_MODEL, D_MODEL), const, pipeline_mode=pl.Buffered(1)),
            pl.BlockSpec((1, D_MODEL), const),
            pl.BlockSpec((D_MODEL, N_EXPERTS), const),
            pl.BlockSpec((D_MODEL, N_EXPERTS), const),
            pl.BlockSpec((1, N_EXPERTS), const),
        ],
        out_specs=[
            pl.BlockSpec((tm, D_MODEL), lambda i: (i, 0)),
            pl.BlockSpec((tm, D_MODEL // 2), lambda i: (i, 0)),
            pl.BlockSpec((tm, LANES), lambda i: (i, 0)),
            pl.BlockSpec((tm, LANES), lambda i: (i, 0)),
            pl.BlockSpec((8, LANES), const),
        ],
        out_shape=[
            jax.ShapeDtypeStruct((t, D_MODEL), F32),
            jax.ShapeDtypeStruct((t, D_MODEL // 2), PACKED),
            jax.ShapeDtypeStruct((t, LANES), jnp.int32),
            jax.ShapeDtypeStruct((t, LANES), F32),
            jax.ShapeDtypeStruct((8, LANES), F32),
        ],
        scratch_shapes=[pltpu.VMEM((8, LANES), F32)],
        compiler_params=pltpu.CompilerParams(
            dimension_semantics=("arbitrary",), vmem_limit_bytes=VMEM_LIMIT),
        name=name,
    )(x, branch, proj, proj, proj, wpo, wco, wao, wo, norm_ffn.reshape(1, D_MODEL),
      wr_hi, wr_lo, b_router.reshape(1, N_EXPERTS))


def _fill_work_list(nblk_s, pad_s, nj, refs):
    te, tj, trow, tnb, tfirst, tne, tnj, ttot = refs

    def clear(i, c):
        for r in (te, tj, trow, tnb, tfirst):
            r[i] = 0
        tne[i] = -1
        tnj[i] = -1
        return c

    lax.fori_loop(0, te.shape[0], clear, 0)

    carry = (jnp.int32(0), jnp.int32(-1))
    for j in range(nj):
        def per_expert(e, carry):
            q, prev = carry
            n = nblk_s[e]
            nch = (n + EXP_CHUNK - 1) // EXP_CHUNK

            @pl.when((nch > 0) & (prev >= 0))
            def _():
                tne[prev] = e
                tnj[prev] = jnp.int32(j)

            def per_chunk(c, q):
                te[q] = e
                tj[q] = jnp.int32(j)
                trow[q] = pad_s[e] + c * (EXP_CHUNK * EXP_TM)
                tnb[q] = jnp.minimum(EXP_CHUNK, n - c * EXP_CHUNK)
                tfirst[q] = (c == 0).astype(jnp.int32)
                return q + 1

            return lax.fori_loop(0, nch, per_chunk, q), jnp.where(nch > 0, q, prev)

        carry = lax.fori_loop(0, N_EXPERTS, per_expert, carry)
    ttot[0] = carry[0]


def _route_tables_kernel(cntp_ref, cnts_ref, rip_ref, ris_ref, dest_ref, *refs, nj_in, nj_out):
    tabs_in, tabs_out = refs[0:8], refs[8:16]
    nblk_s, pad_s = refs[16:18]

    def per_expert(e, start_blk):
        n = (cntp_ref[e] + cnts_ref[e] + EXP_TM - 1) // EXP_TM
        nblk_s[e] = n
        pad_s[e] = start_blk * EXP_TM
        return start_blk + n

    lax.fori_loop(0, N_EXPERTS, per_expert, jnp.int32(0))
    _fill_work_list(nblk_s, pad_s, nj_in, tabs_in)
    _fill_work_list(nblk_s, pad_s, nj_out, tabs_out)

    tile = ROUTE_TM
    col0 = 0
    for ri_ref, is_sample in ((rip_ref, False), (ris_ref, True)):
        for r in range(ri_ref.shape[0] // tile):
            ri = ri_ref[r * tile:(r + 1) * tile, :]
            base = jnp.zeros_like(ri)
            for e in range(N_EXPERTS):
                first_row = pad_s[e] + cntp_ref[e] if is_sample else pad_s[e]
                base = jnp.where(ri == e, first_row, base)
            dest = base + pltpu.roll(ri, LANES - TOP_K, axis=1)
            dest_ref[:, col0:col0 + tile] = jnp.transpose(dest)[0:8, :]
            col0 += tile


def _route_tables(cnt_p, cnt_s, ri_p, ri_s, nj_in, nj_out, max_items):
    t_all = ri_p.shape[0] + ri_s.shape[0]
    assert ri_p.shape[0] % ROUTE_TM == 0 and ri_s.shape[0] % ROUTE_TM == 0
    smem = pl.BlockSpec(memory_space=pltpu.SMEM)

    def tables(nj):
        n = nj * max_items
        return [jax.ShapeDtypeStruct((n,), jnp.int32)] * 7 + [jax.ShapeDtypeStruct((1,), jnp.int32)]

    out = pl.pallas_call(
        functools.partial(_route_tables_kernel, nj_in=nj_in, nj_out=nj_out),
        grid=(1,),
        in_specs=[smem, smem,
                  pl.BlockSpec(ri_p.shape, lambda i: (0, 0)), pl.BlockSpec(ri_s.shape, lambda i: (0, 0))],
        out_specs=[pl.BlockSpec((8, t_all), lambda i: (0, 0))] + [smem] * 16,
        out_shape=[jax.ShapeDtypeStruct((8, t_all), jnp.int32)] + tables(nj_in) + tables(nj_out),
        scratch_shapes=[pltpu.SMEM((N_EXPERTS,), jnp.int32), pltpu.SMEM((N_EXPERTS,), jnp.int32)],
        compiler_params=pltpu.CompilerParams(
            dimension_semantics=("arbitrary",), vmem_limit_bytes=VMEM_LIMIT),
        name="route_tables",
    )(cnt_p, cnt_s, ri_p, ri_s)
    return out[0], tuple(out[1:9]), tuple(out[9:17])


def _sc_mesh():
    return plsc.VectorSubcoreMesh(core_axis_name="c", subcore_axis_name="s",
                                  num_cores=SC_CORES, num_subcores=SC_SUBCORES)


def _sc_worker_id():
    return lax.axis_index("s") * SC_CORES + lax.axis_index("c")


def _sc_dispatch(h2_a, h2_b, dest_by_slot, rows_out):
    ta, w = h2_a.shape
    t = ta + h2_b.shape[0]
    per_w = t // SC_WORKERS
    n_chunks = per_w // SC_CHUNK
    assert t == SC_WORKERS * n_chunks * SC_CHUNK and ta % SC_CHUNK == 0
    idx = dest_by_slot[:TOP_K].reshape(TOP_K, SC_WORKERS, n_chunks, SC_CHUNK).transpose(1, 0, 2, 3)

    def body(a_hbm, b_hbm, idx_hbm, xs_hbm, idx_v, rows_v, sems):
        base = _sc_worker_id() * per_w
        pltpu.sync_copy(idx_hbm.at[_sc_worker_id()], idx_v)
        pending = [[], []]
        for c in range(n_chunks):
            b = c % 2
            for d in pending[b]:
                d.wait()
            tok0 = base + c * SC_CHUNK

            @pl.when(tok0 < ta)
            def _():
                pltpu.sync_copy(a_hbm.at[pl.ds(tok0, SC_CHUNK)], rows_v.at[b])

            @pl.when(tok0 >= ta)
            def _():
                pltpu.sync_copy(b_hbm.at[pl.ds(tok0 - ta, SC_CHUNK)], rows_v.at[b])

            pending[b] = [pltpu.async_copy(rows_v.at[b], xs_hbm.at[idx_v.at[k, c]], sems.at[b])
                          for k in range(TOP_K)]
        for b in range(2):
            for d in pending[b]:
                d.wait()

    return pl.kernel(
        body,
        out_type=jax.ShapeDtypeStruct((rows_out, w), h2_a.dtype),
        mesh=_sc_mesh(),
        scratch_types=[pltpu.VMEM((TOP_K, n_chunks, SC_CHUNK), jnp.int32),
                       pltpu.VMEM((2, SC_CHUNK, w), h2_a.dtype),
                       pltpu.SemaphoreType.DMA((2,))],
        name="sc_dispatch",
    )(h2_a, h2_b, idx)


def _sc_gather_rows(table, idx):
    n = idx.shape[0]
    w = table.shape[1]
    per_w = n // SC_WORKERS
    n_chunks = per_w // SC_CHUNK
    assert n == SC_WORKERS * n_chunks * SC_CHUNK
    idx3 = idx.reshape(SC_WORKERS, n_chunks, SC_CHUNK)

    def body(table_hbm, idx_hbm, out_hbm, idx_v, rows_v, gsem, wsems):
        base = _sc_worker_id() * per_w
        pltpu.sync_copy(idx_hbm.at[_sc_worker_id()], idx_v)
        pending = [None, None]
        for c in range(n_chunks):
            b = c % 2
            if pending[b] is not None:
                pending[b].wait()
            pltpu.async_copy(table_hbm.at[idx_v.at[c]], rows_v.at[b], gsem).wait()
            pending[b] = pltpu.async_copy(
                rows_v.at[b], out_hbm.at[pl.ds(base + c * SC_CHUNK, SC_CHUNK)], wsems.at[b])
        for b in range(2):
            if pending[b] is not None:
                pending[b].wait()

    return pl.kernel(
        body,
        out_type=jax.ShapeDtypeStruct((n, w), table.dtype),
        mesh=_sc_mesh(),
        scratch_types=[pltpu.VMEM((n_chunks, SC_CHUNK), jnp.int32),
                       pltpu.VMEM((2, SC_CHUNK, w), table.dtype),
                       pltpu.SemaphoreType.DMA,
                       pltpu.SemaphoreType.DMA((2,))],
        name="sc_gather_rows",
    )(table, idx3)


def _grouped_pipeline(tabs, total_ref, weight_copies, cast_weights, in_copy, out_copy, compute):
    te, tj, _, tnb, tfirst, tne, tnj = tabs
    total = total_ref[0]

    def each_block(q, copy, op):
        slot = lax.rem(q, 2)
        for k in range(EXP_CHUNK):
            @pl.when(k < tnb[q])
            def _():
                op(copy(q, slot, k))

    def start(d):
        d.start()

    def wait(d):
        d.wait()

    for d in weight_copies(te[0], tj[0]):
        d.start()
    each_block(0, in_copy, start)

    def body(q, carry):
        each_block(q, in_copy, wait)

        @pl.when(q + 1 < total)
        def _():
            each_block(q + 1, in_copy, start)

        @pl.when(tfirst[q] == 1)
        def _():
            for d in weight_copies(te[q], tj[q]):
                d.wait()
            cast_weights()

            @pl.when(tne[q] >= 0)
            def _():
                for d in weight_copies(tne[q], tnj[q]):
                    d.start()

        @pl.when(q >= 2)
        def _():
            each_block(q - 2, out_copy, wait)

        for m in range(1, EXP_CHUNK + 1):
            @pl.when(tnb[q] == m)
            def _():
                compute(q, lax.rem(q, 2), m)

        each_block(q, out_copy, start)
        return carry

    lax.fori_loop(0, total, body, 0)

    @pl.when(total >= 2)
    def _():
        each_block(total - 2, out_copy, wait)

    each_block(total - 1, out_copy, wait)


def _expert_in_kernel(te, tj, trow, tnb, tfirst, tne, tnj, total_ref, xs_hbm, w_hbm, b_ref, act_hbm,
                      wstage, wb, x_buf, a_buf, w_sem, x_sem, a_sem):
    tm, tn = EXP_TM, a_buf.shape[2]
    nj = D_FF // tn

    def weight_copies(e, j):
        return [pltpu.make_async_copy(
            w_hbm.at[e, :, pl.ds(pl.multiple_of(half * D_FF + j * tn, tn), tn)], wstage.at[half], w_sem)
            for half in range(2)]

    def cast_weights():
        wb[...] = wstage[...].astype(BF16)

    def rows(q, k):
        return pl.ds(pl.multiple_of(trow[q] + k * tm, tm), tm)

    def x_copy(q, slot, k):
        return pltpu.make_async_copy(xs_hbm.at[rows(q, k)], x_buf.at[slot, pl.ds(k * tm, tm)], x_sem.at[slot])

    def a_copy(q, slot, k):
        return pltpu.make_async_copy(
            a_buf.at[slot, pl.ds(k * tm, tm)],
            act_hbm.at[rows(q, k), pl.ds(pl.multiple_of(tj[q] * tn, tn), tn)], a_sem.at[slot])

    def compute(q, slot, m):
        bias0 = te[q] * (2 * nj) + tj[q]
        x = jnp.concatenate(_unpack_bf16_pairs(x_buf[slot, 0:m * tm]), axis=1).astype(BF16)
        g = jnp.dot(x, wb[0], preferred_element_type=F32) + b_ref[bias0]
        up = jnp.dot(x, wb[1], preferred_element_type=F32) + b_ref[bias0 + nj]
        g = jnp.minimum(g, SWIGLU_LIMIT)
        up = jnp.clip(up, -SWIGLU_LIMIT, SWIGLU_LIMIT)
        a_buf[slot, 0:m * tm] = (g * _sigmoid(SWIGLU_ALPHA * g) * (up + 1.0)).astype(BF16)

    _grouped_pipeline((te, tj, trow, tnb, tfirst, tne, tnj), total_ref,
                      weight_copies, cast_weights, x_copy, a_copy, compute)


def _expert_in(tabs, xs, w_in, b_in):
    rows = xs.shape[0]
    tm, tn = EXP_TM, EXP_IN_TN
    nj = D_FF // tn
    bias = b_in.reshape(N_EXPERTS * 2 * nj, 1, tn)
    grid_spec = pltpu.PrefetchScalarGridSpec(
        num_scalar_prefetch=len(tabs),
        grid=(1,),
        in_specs=[
            pl.BlockSpec(memory_space=pl.ANY),
            pl.BlockSpec(memory_space=pl.ANY),
            pl.BlockSpec(bias.shape, lambda i, *_: (0, 0, 0), pipeline_mode=pl.Buffered(1)),
        ],
        out_specs=pl.BlockSpec(memory_space=pl.ANY),
        scratch_shapes=[pltpu.VMEM((2, D_MODEL, tn), F32), pltpu.VMEM((2, D_MODEL, tn), BF16),
                        pltpu.VMEM((2, EXP_CHUNK * tm, D_MODEL // 2), PACKED),
                        pltpu.VMEM((2, EXP_CHUNK * tm, tn), BF16),
                        pltpu.SemaphoreType.DMA, pltpu.SemaphoreType.DMA((2,)),
                        pltpu.SemaphoreType.DMA((2,))],
    )
    return pl.pallas_call(
        _expert_in_kernel,
        grid_spec=grid_spec,
        out_shape=jax.ShapeDtypeStruct((rows, D_FF), BF16),
        compiler_params=pltpu.CompilerParams(
            dimension_semantics=("arbitrary",), vmem_limit_bytes=VMEM_LIMIT),
        name="expert_in",
    )(*tabs, xs, w_in, bias)


def _expert_out_kernel(te, tj, trow, tnb, tfirst, tne, tnj, total_ref, act_hbm, w_hbm, b_ref, ys_hbm,
                       wstage, wb, a_buf, y_buf, w_sem, a_sem, y_sem):
    tm, tnp = EXP_TM, y_buf.shape[2]
    tn = 2 * tnp
    nj = D_MODEL // tn

    def weight_copies(e, j):
        return [pltpu.make_async_copy(
            w_hbm.at[e, :, pl.ds(pl.multiple_of(j * tn, tn), tn)], wstage, w_sem)]

    def cast_weights():
        wb[...] = wstage[...].astype(BF16)

    def rows(q, k):
        return pl.ds(pl.multiple_of(trow[q] + k * tm, tm), tm)

    def a_copy(q, slot, k):
        return pltpu.make_async_copy(act_hbm.at[rows(q, k)], a_buf.at[slot, pl.ds(k * tm, tm)], a_sem.at[slot])

    def y_copy(q, slot, k):
        return pltpu.make_async_copy(
            y_buf.at[slot, pl.ds(k * tm, tm)],
            ys_hbm.at[rows(q, k), pl.ds(pl.multiple_of(tj[q] * tnp, tnp), tnp)], y_sem.at[slot])

    def compute(q, slot, m):
        y = jnp.dot(a_buf[slot, 0:m * tm], wb[...], preferred_element_type=F32) + b_ref[te[q] * nj + tj[q]]
        y_buf[slot, 0:m * tm] = _pack_bf16_pairs(y)

    _grouped_pipeline((te, tj, trow, tnb, tfirst, tne, tnj), total_ref,
                      weight_copies, cast_weights, a_copy, y_copy, compute)


def _expert_out(tabs, act, w_out, b_out):
    rows = act.shape[0]
    tm, tn = EXP_TM, EXP_OUT_TN
    nj = D_MODEL // tn
    bias = b_out.reshape(N_EXPERTS * nj, 1, tn)
    grid_spec = pltpu.PrefetchScalarGridSpec(
        num_scalar_prefetch=len(tabs),
        grid=(1,),
        in_specs=[
            pl.BlockSpec(memory_space=pl.ANY),
            pl.BlockSpec(memory_space=pl.ANY),
            pl.BlockSpec(bias.shape, lambda i, *_: (0, 0, 0), pipeline_mode=pl.Buffered(1)),
        ],
        out_specs=pl.BlockSpec(memory_space=pl.ANY),
        scratch_shapes=[pltpu.VMEM((D_FF, tn), F32), pltpu.VMEM((D_FF, tn), BF16),
                        pltpu.VMEM((2, EXP_CHUNK * tm, D_FF), BF16),
                        pltpu.VMEM((2, EXP_CHUNK * tm, tn // 2), PACKED),
                        pltpu.SemaphoreType.DMA, pltpu.SemaphoreType.DMA((2,)),
                        pltpu.SemaphoreType.DMA((2,))],
    )
    return pl.pallas_call(
        _expert_out_kernel,
        grid_spec=grid_spec,
        out_shape=jax.ShapeDtypeStruct((rows, D_MODEL // 2), PACKED),
        compiler_params=pltpu.CompilerParams(
            dimension_semantics=("arbitrary",), vmem_limit_bytes=VMEM_LIMIT),
        name="expert_out",
    )(*tabs, act, w_out, bias)


def _combine_kernel(x1_ref, yg_ref, rw_ref, g_ref, o_ref):
    half = EXP_OUT_TN // 2
    pieces = []
    ssq = jnp.zeros((x1_ref.shape[0], 1), F32)
    for jt in range(D_MODEL // EXP_OUT_TN):
        acc_lo = x1_ref[:, jt * EXP_OUT_TN:jt * EXP_OUT_TN + half]
        acc_hi = x1_ref[:, jt * EXP_OUT_TN + half:(jt + 1) * EXP_OUT_TN]
        moe_lo = jnp.zeros_like(acc_lo)
        moe_hi = jnp.zeros_like(acc_hi)
        for k in range(TOP_K):
            lo, hi = _unpack_bf16_pairs(yg_ref[k, :, jt * half:(jt + 1) * half])
            moe_lo = moe_lo + lo * rw_ref[:, k:k + 1]
            moe_hi = moe_hi + hi * rw_ref[:, k:k + 1]
        for acc in (acc_lo + moe_lo, acc_hi + moe_hi):
            ssq = ssq + jnp.sum(acc * acc, axis=-1, keepdims=True)
            pieces.append(acc)
    inv = lax.rsqrt(ssq / D_MODEL + EPS)
    for n, acc in enumerate(pieces):
        o_ref[:, n * half:(n + 1) * half] = acc * inv * g_ref[:, n * half:(n + 1) * half]


def _combine(x1, yg, rw, gain, row_off, name):
    t = x1.shape[0]
    tm = COMB_TM
    assert t % tm == 0 and row_off % tm == 0
    off = row_off // tm
    return pl.pallas_call(
        _combine_kernel,
        grid=(t // tm,),
        in_specs=[
            pl.BlockSpec((tm, D_MODEL), lambda i: (i, 0)),
            pl.BlockSpec((TOP_K, tm, D_MODEL // 2), lambda i: (0, i + off, 0)),
            pl.BlockSpec((tm, LANES), lambda i: (i, 0)),
            pl.BlockSpec((1, D_MODEL), lambda i: (0, 0)),
        ],
        out_specs=pl.BlockSpec((tm, D_MODEL), lambda i: (i, 0)),
        out_shape=jax.ShapeDtypeStruct((t, D_MODEL), F32),
        compiler_params=pltpu.CompilerParams(
            dimension_semantics=("arbitrary",), vmem_limit_bytes=VMEM_LIMIT),
        name=name,
    )(x1, yg, rw, gain.reshape(1, D_MODEL))


def kernel(x_prompt, x_sample, state_pool, state_conv, cache_mem_k, cache_mem_v, mem_prompt,
           norm_mix, w_in, b_gate, w_pool_group, pool_scale, w_conv, mem_norm, w_mem_kv,
           w_pool_out, w_conv_out, w_attn_out, w_o, norm_ffn, w_router, b_router,
           w_exp_in, b_exp_in, w_exp_out, b_exp_out, final_norm):
    depth = norm_mix.shape[0]
    assert depth == 1
    l = 0
    bp, seq, _ = x_prompt.shape
    bs, ln, _ = x_sample.shape
    tp, ts = bp * seq, bs * ln

    kv = _norm_matmul(mem_prompt.reshape(bp * N_MEM, D_MODEL), mem_norm[l], w_mem_kv[l].astype(BF16),
                      jnp.zeros((2 * D_XATTN,), F32), 2 * D_XATTN, "mem_kv")
    mk = kv[:, :D_XATTN].reshape(bp, N_MEM, D_XATTN)
    mv = kv[:, D_XATTN:].reshape(bp, N_MEM, D_XATTN)

    bias_in = jnp.concatenate([jnp.zeros((D_MIX,), F32), b_gate[l]])
    xp = x_prompt.reshape(tp, D_MODEL)
    xs_ = x_sample.reshape(ts, D_MODEL)
    w_in_b = w_in[l].astype(BF16)
    proj_p = _norm_matmul(xp, norm_mix[l], w_in_b, bias_in, D_MIX, "proj_prompt")
    proj_s = _norm_matmul(xs_, norm_mix[l], w_in_b, bias_in, D_MIX, "proj_sample")

    br_p, zst_p = _mix_prompt(proj_p, mk, mv, w_pool_group[l], pool_scale[l], w_conv[l], bp, seq)
    br_s, zst_s = _mix_sample(proj_s, state_pool[l], state_conv[l],
                              cache_mem_k[l].reshape(bs, N_MEM * N_XHEADS, XHEAD_DIM),
                              cache_mem_v[l].reshape(bs, N_MEM * N_XHEADS, XHEAD_DIM),
                              w_pool_group[l], pool_scale[l], w_conv[l], bs, ln)

    wpo, wco, wao, wo = (w.astype(BF16) for w in (w_pool_out[l], w_conv_out[l], w_attn_out[l], w_o[l]))
    x1_p, h2_p, ri_p, rw_p, cnt_p = _merge_route(xp, br_p, proj_p, wpo, wco, wao, wo, norm_ffn[l],
                                                  w_router[l], b_router[l], "merge_route_prompt")
    x1_s, h2_s, ri_s, rw_s, cnt_s = _merge_route(xs_, br_s, proj_s, wpo, wco, wao, wo, norm_ffn[l],
                                                  w_router[l], b_router[l], "merge_route_sample")

    t_all = tp + ts
    n_assign = t_all * TOP_K
    nb_max = n_assign // EXP_TM + N_EXPERTS
    max_items = nb_max // EXP_CHUNK + N_EXPERTS
    dest, tabs_in, tabs_out = _route_tables(
        cnt_p[0, :N_EXPERTS].astype(jnp.int32), cnt_s[0, :N_EXPERTS].astype(jnp.int32), ri_p, ri_s,
        D_FF // EXP_IN_TN, D_MODEL // EXP_OUT_TN, max_items)
    xs_sorted = _sc_dispatch(h2_p, h2_s, dest, nb_max * EXP_TM)

    act = _expert_in(tabs_in, xs_sorted, w_exp_in[l], b_exp_in[l])
    ys = _expert_out(tabs_out, act, w_exp_out[l], b_exp_out[l])
    yg = _sc_gather_rows(ys, dest[:TOP_K].reshape(-1)).reshape(TOP_K, t_all, D_MODEL // 2)

    y_p = _combine(x1_p, yg, rw_p, final_norm, 0, "combine_prompt")
    y_s = _combine(x1_s, yg, rw_s, final_norm, tp, "combine_sample")

    new_pool_p = proj_p[:, :D_POOL].reshape(bp, seq, D_POOL)[:, seq - POOL_STATE_LEN:]
    new_conv_p = zst_p[:, 8 - (CONV_WIDTH - 1):]
    u_s = proj_s[:, :D_POOL].reshape(bs, ln, D_POOL)
    new_pool_s = jnp.concatenate([state_pool[l], u_s], axis=1)[:, -POOL_STATE_LEN:]
    new_conv_s = zst_s[:, ln - (CONV_WIDTH - 1):]

    return (y_p.reshape(bp, seq, D_MODEL), y_s.reshape(bs, ln, D_MODEL),
            new_pool_p[None], new_conv_p[None],
            mk.reshape(1, bp, N_MEM, N_XHEADS, XHEAD_DIM), mv.reshape(1, bp, N_MEM, N_XHEADS, XHEAD_DIM),
            new_pool_s[None], new_conv_s[None])
```

```python
import functools

import jax
import jax.numpy as jnp
from jax import lax
from jax.experimental import pallas as pl
from jax.experimental.pallas import tpu as pltpu
from jax.experimental.pallas import tpu_sc as plsc

F32 = jnp.float32
BF16 = jnp.bfloat16
PACKED = jnp.int32

D_MODEL = 2048
POOL_WINDOWS = (2, 4, 8, 16)
POOL_GROUP_DIM = 128
D_POOL = 512
POOL_STATE_LEN = 15
D_CONV = 1024
CONV_WIDTH = 3
N_MEM = 256
N_XHEADS = 4
XHEAD_DIM = 128
D_XATTN = 512
N_BRANCH = 3
D_MIX = D_POOL + 3 * D_CONV + D_XATTN
D_IN_TOTAL = D_MIX + N_BRANCH * D_MODEL
N_EXPERTS = 32
TOP_K = 4
D_FF = D_MODEL
SWIGLU_LIMIT = 7.0
SWIGLU_ALPHA = 1.702
EPS = 1e-5

C_U = 0
C_V = D_POOL
C_B = D_POOL + D_CONV
C_C = D_POOL + 2 * D_CONV
C_Q = D_POOL + 3 * D_CONV

LANES = 128
HIST = 16

PROJ_TM = 1024
PROJ_TN = 1024
MIX_TM = 256
MIX_NS = 8
MERGE_TM = 256
ROUTE_TM = 1024
EXP_TM = 128
EXP_CHUNK = 4
EXP_IN_TN = 1024
EXP_OUT_TN = 2048
COMB_TM = 256
SC_CORES = 2
SC_SUBCORES = 16
SC_WORKERS = SC_CORES * SC_SUBCORES
SC_CHUNK = 32
VMEM_LIMIT = 56 * 1024 * 1024


def _sigmoid(x):
    return 0.5 * (jnp.tanh(0.5 * x) + 1.0)


def _rms(x, g):
    ms = jnp.mean(x * x, axis=-1, keepdims=True)
    return x * lax.rsqrt(ms + EPS) * g


def _pack_bf16_pairs(x):
    n = x.shape[1] // 2
    lo = lax.bitcast_convert_type(x[:, :n].astype(BF16).astype(F32), jnp.uint32)
    hi = lax.bitcast_convert_type(x[:, n:].astype(BF16).astype(F32), jnp.uint32)
    return lax.bitcast_convert_type((hi & jnp.uint32(0xFFFF0000)) | (lo >> 16), PACKED)


def _unpack_bf16_pairs(p):
    p = lax.bitcast_convert_type(p, jnp.uint32)
    lo = lax.bitcast_convert_type(p << 16, F32)
    hi = lax.bitcast_convert_type(p & jnp.uint32(0xFFFF0000), F32)
    return lo, hi


def _norm_matmul_kernel(x_ref, g_ref, w_ref, b_ref, o_ref, h_ref, *, act_from):
    j = pl.program_id(1)

    @pl.when(j == 0)
    def _():
        h_ref[...] = _rms(x_ref[...], g_ref[...]).astype(BF16)

    acc = jnp.dot(h_ref[...], w_ref[...], preferred_element_type=F32) + b_ref[...]

    @pl.when(j < act_from)
    def _():
        o_ref[...] = acc

    @pl.when(j >= act_from)
    def _():
        o_ref[...] = _sigmoid(acc)


def _norm_matmul(x, gain, w, bias, act_from_col, name):
    t, d = x.shape
    n = w.shape[1]
    tm = min(PROJ_TM, t)
    tn = PROJ_TN
    assert t % tm == 0 and n % tn == 0 and act_from_col % tn == 0
    return pl.pallas_call(
        functools.partial(_norm_matmul_kernel, act_from=act_from_col // tn),
        grid=(t // tm, n // tn),
        in_specs=[
            pl.BlockSpec((tm, d), lambda i, j: (i, 0)),
            pl.BlockSpec((1, d), lambda i, j: (0, 0)),
            pl.BlockSpec((d, tn), lambda i, j: (0, j)),
            pl.BlockSpec((1, tn), lambda i, j: (0, j)),
        ],
        out_specs=pl.BlockSpec((tm, tn), lambda i, j: (i, j)),
        out_shape=jax.ShapeDtypeStruct((t, n), F32),
        scratch_shapes=[pltpu.VMEM((tm, d), BF16)],
        compiler_params=pltpu.CompilerParams(
            dimension_semantics=("arbitrary", "arbitrary"), vmem_limit_bytes=VMEM_LIMIT),
        name=name,
    )(x, gain.reshape(1, d), w, bias.reshape(1, n))


def _pool_project(pooled, wpg_ref, scale_ref, g):
    sl = slice(g * POOL_GROUP_DIM, (g + 1) * POOL_GROUP_DIM)
    y = jnp.dot(pooled.astype(BF16), wpg_ref[g].astype(BF16), preferred_element_type=F32)
    return y * scale_ref[:, sl]


def _softmax_rows(s):
    m = jnp.max(s, axis=-1, keepdims=True)
    e = jnp.exp(s - m)
    return e / jnp.sum(e, axis=-1, keepdims=True)


def _mix_prompt_kernel(cur_ref, prev_ref, mk_ref, mv_ref, wpg_ref, scale_ref, wconv_ref,
                       br_ref, zst_ref, extu_ref, extz_ref):
    t = pl.program_id(1)
    tm = cur_ref.shape[0]
    has_prev = t > 0

    u = cur_ref[:, C_U:C_U + D_POOL]
    extu_ref[0:HIST, :] = jnp.where(has_prev, prev_ref[:, C_U:C_U + D_POOL], 0.0)
    extu_ref[HIST:HIST + tm, :] = u
    pos = t * tm + lax.broadcasted_iota(jnp.int32, (tm, 1), 0)
    for g, w in enumerate(POOL_WINDOWS):
        sl = slice(g * POOL_GROUP_DIM, (g + 1) * POOL_GROUP_DIM)
        s = extu_ref[HIST:HIST + tm, sl]
        for k in range(1, w):
            s = s + extu_ref[HIST - k:HIST - k + tm, sl]
        cnt = jnp.minimum(w, pos + 1).astype(F32)
        pooled = s / cnt - extu_ref[HIST:HIST + tm, sl]
        br_ref[:, sl] = _pool_project(pooled, wpg_ref, scale_ref, g).astype(BF16)

    z = cur_ref[:, C_C:C_C + D_CONV] * cur_ref[:, C_V:C_V + D_CONV]
    zprev = prev_ref[:, C_C:C_C + D_CONV] * prev_ref[:, C_V:C_V + D_CONV]
    extz_ref[0:HIST, :] = jnp.where(has_prev, zprev, 0.0)
    extz_ref[HIST:HIST + tm, :] = z
    y = extz_ref[HIST - 2:HIST - 2 + tm, :] * wconv_ref[0:1, :]
    y = y + extz_ref[HIST - 1:HIST - 1 + tm, :] * wconv_ref[1:2, :]
    y = y + extz_ref[HIST:HIST + tm, :] * wconv_ref[2:3, :]
    br_ref[:, D_POOL:D_POOL + D_CONV] = (cur_ref[:, C_B:C_B + D_CONV] * y).astype(BF16)
    zst_ref[0] = extz_ref[HIST + tm - 8:HIST + tm, :]

    for h in range(N_XHEADS):
        sl = slice(h * XHEAD_DIM, (h + 1) * XHEAD_DIM)
        qh = cur_ref[:, C_Q + h * XHEAD_DIM:C_Q + (h + 1) * XHEAD_DIM].astype(BF16)
        kh = mk_ref[0, :, sl].astype(BF16)
        vh = mv_ref[0, :, sl].astype(BF16)
        s = lax.dot_general(qh, kh, (((1,), (1,)), ((), ())), preferred_element_type=F32)
        p = _softmax_rows(s * (XHEAD_DIM ** -0.5))
        o = jnp.dot(p.astype(BF16), vh, preferred_element_type=F32)
        c0 = D_POOL + D_CONV + h * XHEAD_DIM
        br_ref[:, c0:c0 + XHEAD_DIM] = o.astype(BF16)


def _mix_prompt(proj, mk, mv, wpg, scale, wconv, batch, seq):
    tm = MIX_TM
    nt = seq // tm
    assert seq % tm == 0 and tm % HIST == 0
    rpb = tm // HIST
    return pl.pallas_call(
        _mix_prompt_kernel,
        grid=(batch, nt),
        in_specs=[
            pl.BlockSpec((tm, D_MIX), lambda b, t: (b * nt + t, 0)),
            pl.BlockSpec((HIST, D_MIX), lambda b, t: (jnp.maximum((b * nt + t) * rpb - 1, 0), 0)),
            pl.BlockSpec((1, N_MEM, D_XATTN), lambda b, t: (b, 0, 0)),
            pl.BlockSpec((1, N_MEM, D_XATTN), lambda b, t: (b, 0, 0)),
            pl.BlockSpec((len(POOL_WINDOWS), POOL_GROUP_DIM, POOL_GROUP_DIM), lambda b, t: (0, 0, 0)),
            pl.BlockSpec((1, D_POOL), lambda b, t: (0, 0)),
            pl.BlockSpec((CONV_WIDTH, D_CONV), lambda b, t: (0, 0)),
        ],
        out_specs=[
            pl.BlockSpec((tm, D_MODEL), lambda b, t: (b * nt + t, 0)),
            pl.BlockSpec((1, 8, D_CONV), lambda b, t: (b, 0, 0)),
        ],
        out_shape=[
            jax.ShapeDtypeStruct((batch * seq, D_MODEL), BF16),
            jax.ShapeDtypeStruct((batch, 8, D_CONV), F32),
        ],
        scratch_shapes=[pltpu.VMEM((HIST + tm, D_POOL), F32), pltpu.VMEM((HIST + tm, D_CONV), F32)],
        compiler_params=pltpu.CompilerParams(
            dimension_semantics=("arbitrary", "arbitrary"), vmem_limit_bytes=VMEM_LIMIT),
        name="mix_prompt",
    )(proj, proj, mk, mv, wpg, scale.reshape(1, D_POOL), wconv)


def _mix_sample_kernel(cur_ref, sp_ref, sc_ref, k_ref, v_ref, wpg_ref, scale_ref, wconv_ref,
                       br_ref, zst_ref, extu_ref, extz_ref):
    ns, ln = zst_ref.shape[0], zst_ref.shape[1]
    rows = ns * ln

    extu_ref[:, HIST - POOL_STATE_LEN:HIST, :] = sp_ref[...]
    extu_ref[:, HIST:HIST + ln, :] = cur_ref[:, C_U:C_U + D_POOL].reshape(ns, ln, D_POOL)
    for g, w in enumerate(POOL_WINDOWS):
        sl = slice(g * POOL_GROUP_DIM, (g + 1) * POOL_GROUP_DIM)
        s = extu_ref[:, HIST:HIST + ln, sl]
        for k in range(1, w):
            s = s + extu_ref[:, HIST - k:HIST - k + ln, sl]
        pooled = s / float(w) - extu_ref[:, HIST:HIST + ln, sl]
        pooled = pooled.reshape(rows, POOL_GROUP_DIM)
        br_ref[:, sl] = _pool_project(pooled, wpg_ref, scale_ref, g).astype(BF16)

    z = cur_ref[:, C_C:C_C + D_CONV] * cur_ref[:, C_V:C_V + D_CONV]
    extz_ref[:, HIST - 2:HIST, :] = sc_ref[...]
    extz_ref[:, HIST:HIST + ln, :] = z.reshape(ns, ln, D_CONV)
    y = extz_ref[:, HIST - 2:HIST - 2 + ln, :] * wconv_ref[0:1, :]
    y = y + extz_ref[:, HIST - 1:HIST - 1 + ln, :] * wconv_ref[1:2, :]
    y = y + extz_ref[:, HIST:HIST + ln, :] * wconv_ref[2:3, :]
    br_ref[:, D_POOL:D_POOL + D_CONV] = (
        cur_ref[:, C_B:C_B + D_CONV] * y.reshape(rows, D_CONV)).astype(BF16)
    zst_ref[...] = extz_ref[:, HIST:HIST + ln, :]

    q3 = cur_ref[:, C_Q:C_Q + D_XATTN].reshape(ns, ln, D_XATTN)
    q4 = jnp.concatenate([q3[:, :, h * XHEAD_DIM:(h + 1) * XHEAD_DIM] for h in range(N_XHEADS)], axis=1)
    s = jnp.einsum("nqd,nkd->nqk", q4.astype(BF16), k_ref[...].astype(BF16),
                   preferred_element_type=F32)
    row_head = lax.broadcasted_iota(jnp.int32, (N_XHEADS * ln, N_XHEADS * N_MEM), 0) // ln
    col_head = lax.broadcasted_iota(jnp.int32, (N_XHEADS * ln, N_XHEADS * N_MEM), 1) % N_XHEADS
    s = jnp.where((row_head == col_head)[None], s * (XHEAD_DIM ** -0.5), -jnp.inf)
    p = _softmax_rows(s)
    r = jnp.einsum("nqk,nkd->nqd", p.astype(BF16), v_ref[...].astype(BF16),
                   preferred_element_type=F32)
    for h in range(N_XHEADS):
        c0 = D_POOL + D_CONV + h * XHEAD_DIM
        br_ref[:, c0:c0 + XHEAD_DIM] = r[:, h * ln:(h + 1) * ln, :].reshape(rows, XHEAD_DIM).astype(BF16)


def _mix_sample(proj, state_pool, state_conv, mem_k, mem_v, wpg, scale, wconv, nseq, ln):
    ns = MIX_NS
    rows = ns * ln
    assert nseq % ns == 0 and ln == 8
    return pl.pallas_call(
        _mix_sample_kernel,
        grid=(nseq // ns,),
        in_specs=[
            pl.BlockSpec((rows, D_MIX), lambda s: (s, 0)),
            pl.BlockSpec((ns, POOL_STATE_LEN, D_POOL), lambda s: (s, 0, 0)),
            pl.BlockSpec((ns, CONV_WIDTH - 1, D_CONV), lambda s: (s, 0, 0)),
            pl.BlockSpec((ns, N_MEM * N_XHEADS, XHEAD_DIM), lambda s: (s, 0, 0)),
            pl.BlockSpec((ns, N_MEM * N_XHEADS, XHEAD_DIM), lambda s: (s, 0, 0)),
            pl.BlockSpec((len(POOL_WINDOWS), POOL_GROUP_DIM, POOL_GROUP_DIM), lambda s: (0, 0, 0)),
            pl.BlockSpec((1, D_POOL), lambda s: (0, 0)),
            pl.BlockSpec((CONV_WIDTH, D_CONV), lambda s: (0, 0)),
        ],
        out_specs=[
            pl.BlockSpec((rows, D_MODEL), lambda s: (s, 0)),
            pl.BlockSpec((ns, ln, D_CONV), lambda s: (s, 0, 0)),
        ],
        out_shape=[
            jax.ShapeDtypeStruct((nseq * ln, D_MODEL), BF16),
            jax.ShapeDtypeStruct((nseq, ln, D_CONV), F32),
        ],
        scratch_shapes=[pltpu.VMEM((ns, HIST + ln, D_POOL), F32),
                        pltpu.VMEM((ns, HIST + ln, D_CONV), F32)],
        compiler_params=pltpu.CompilerParams(
            dimension_semantics=("arbitrary",), vmem_limit_bytes=VMEM_LIMIT),
        name="mix_sample",
    )(proj, state_pool, state_conv, mem_k, mem_v, wpg, scale.reshape(1, D_POOL), wconv)


def _merge_route_kernel(x_ref, br_ref, g0_ref, g1_ref, g2_ref, wpo_ref, wco_ref, wao_ref, wo_ref,
                        nf_ref, wrh_ref, wrl_ref, brt_ref,
                        x1_ref, h2_ref, ri_ref, rw_ref, cnt_ref, carry_ref):
    i = pl.program_id(0)
    tm = x_ref.shape[0]

    @pl.when(i == 0)
    def _():
        carry_ref[...] = jnp.zeros_like(carry_ref)

    merged = g0_ref[...] * jnp.dot(br_ref[:, 0:D_POOL], wpo_ref[...], preferred_element_type=F32)
    merged = merged + g1_ref[...] * jnp.dot(br_ref[:, D_POOL:D_POOL + D_CONV], wco_ref[...],
                                            preferred_element_type=F32)
    merged = merged + g2_ref[...] * jnp.dot(br_ref[:, D_POOL + D_CONV:D_MODEL], wao_ref[...],
                                            preferred_element_type=F32)
    x1 = x_ref[...] + jnp.dot(merged.astype(BF16), wo_ref[...], preferred_element_type=F32)
    x1_ref[...] = x1
    h2 = _rms(x1, nf_ref[...])
    h2_ref[...] = _pack_bf16_pairs(h2)

    h2_hi = h2.astype(BF16)
    h2_lo = (h2 - h2_hi.astype(F32)).astype(BF16)
    logits = (jnp.dot(h2_hi, wrh_ref[...], preferred_element_type=F32)
              + jnp.dot(h2_hi, wrl_ref[...], preferred_element_type=F32)
              + jnp.dot(h2_lo, wrh_ref[...], preferred_element_type=F32)) + brt_ref[...]
    lane = lax.broadcasted_iota(jnp.int32, (tm, N_EXPERTS), 1).astype(F32)
    vals, idxs, hots = [], [], []
    work = logits
    for _ in range(TOP_K):
        m = jnp.max(work, axis=-1, keepdims=True)
        idx = jnp.min(jnp.where(work == m, lane, float(N_EXPERTS)), axis=-1, keepdims=True)
        hot = lane == idx
        work = jnp.where(hot, -jnp.inf, work)
        vals.append(m)
        idxs.append(idx)
        hots.append(hot)
    es = [jnp.exp(v - vals[0]) for v in vals]
    denom = es[0] + es[1] + es[2] + es[3]

    chosen = jnp.where(hots[0] | hots[1] | hots[2] | hots[3], 1.0, 0.0).astype(BF16)
    r_i = lax.broadcasted_iota(jnp.int32, (tm, tm), 0)
    c_i = lax.broadcasted_iota(jnp.int32, (tm, tm), 1)
    lower = jnp.where(c_i < r_i, 1.0, 0.0).astype(BF16)
    before = jnp.dot(lower, chosen, preferred_element_type=F32) + carry_ref[0:1, 0:N_EXPERTS]
    carry_ref[0:1, 0:N_EXPERTS] = (carry_ref[0:1, 0:N_EXPERTS]
                                   + jnp.sum(chosen.astype(F32), axis=0, keepdims=True))
    cnt_ref[...] = carry_ref[...]

    out_lane = lax.broadcasted_iota(jnp.int32, (tm, LANES), 1)
    ri = jnp.zeros((tm, LANES), jnp.int32)
    rw = jnp.zeros((tm, LANES), F32)
    for k in range(TOP_K):
        rank = jnp.sum(jnp.where(hots[k], before, 0.0), axis=-1, keepdims=True).astype(jnp.int32)
        ri = jnp.where(out_lane == k, idxs[k].astype(jnp.int32), ri)
        ri = jnp.where(out_lane == TOP_K + k, rank, ri)
        rw = jnp.where(out_lane == k, es[k] / denom, rw)
    ri_ref[...] = ri
    rw_ref[...] = rw


def _merge_route(x, branch, proj, wpo, wco, wao, wo, norm_ffn, w_router, b_router, name):
    t = x.shape[0]
    tm = MERGE_TM
    assert t % tm == 0
    gate_blk0 = D_MIX // D_MODEL
    assert D_MIX % D_MODEL == 0
    const = lambda i: (0, 0)
    wr_hi = w_router.astype(BF16)
    wr_lo = (w_router - wr_hi.astype(F32)).astype(BF16)
    return pl.pallas_call(
        _merge_route_kernel,
        grid=(t // tm,),
        in_specs=[
            pl.BlockSpec((tm, D_MODEL), lambda i: (i, 0)),
            pl.BlockSpec((tm, D_MODEL), lambda i: (i, 0)),
            pl.BlockSpec((tm, D_MODEL), lambda i: (i, gate_blk0)),
            pl.BlockSpec((tm, D_MODEL), lambda i: (i, gate_blk0 + 1)),
            pl.BlockSpec((tm, D_MODEL), lambda i: (i, gate_blk0 + 2)),
            pl.BlockSpec((D_POOL, D_MODEL), const, pipeline_mode=pl.Buffered(1)),
            pl.BlockSpec((D_CONV, D_MODEL), const, pipeline_mode=pl.Buffered(1)),
            pl.BlockSpec((D_XATTN, D_MODEL), const, pipeline_mode=pl.Buffered(1)),
            pl.BlockSpec((D_MODEL, D_MODEL), const, pipeline_mode=pl.Buffered(1)),
            pl.BlockSpec((1, D_MODEL), const),
            pl.BlockSpec((D_MODEL, N_EXPERTS), const),
            pl.BlockSpec((D_MODEL, N_EXPERTS), const),
            pl.BlockSpec((1, N_EXPERTS), const),
        ],
        out_specs=[
            pl.BlockSpec((tm, D_MODEL), lambda i: (i, 0)),
            pl.BlockSpec((tm, D_MODEL // 2), lambda i: (i, 0)),
            pl.BlockSpec((tm, LANES), lambda i: (i, 0)),
            pl.BlockSpec((tm, LANES), lambda i: (i, 0)),
            pl.BlockSpec((8, LANES), const),
        ],
        out_shape=[
            jax.ShapeDtypeStruct((t, D_MODEL), F32),
            jax.ShapeDtypeStruct((t, D_MODEL // 2), PACKED),
            jax.ShapeDtypeStruct((t, LANES), jnp.int32),
            jax.ShapeDtypeStruct((t, LANES), F32),
            jax.ShapeDtypeStruct((8, LANES), F32),
        ],
        scratch_shapes=[pltpu.VMEM((8, LANES), F32)],
        compiler_params=pltpu.CompilerParams(
            dimension_semantics=("arbitrary",), vmem_limit_bytes=VMEM_LIMIT),
        name=name,
    )(x, branch, proj, proj, proj, wpo, wco, wao, wo, norm_ffn.reshape(1, D_MODEL),
      wr_hi, wr_lo, b_router.reshape(1, N_EXPERTS))


def _fill_work_list(nblk_s, pad_s, nj, refs):
    te, tj, trow, tnb, tfirst, tne, tnj, ttot = refs

    def clear(i, c):
        for r in (te, tj, trow, tnb, tfirst):
            r[i] = 0
        tne[i] = -1
        tnj[i] = -1
        return c

    lax.fori_loop(0, te.shape[0], clear, 0)

    carry = (jnp.int32(0), jnp.int32(-1))
    for j in range(nj):
        def per_expert(e, carry):
            q, prev = carry
            n = nblk_s[e]
            nch = (n + EXP_CHUNK - 1) // EXP_CHUNK

            @pl.when((nch > 0) & (prev >= 0))
            def _():
                tne[prev] = e
                tnj[prev] = jnp.int32(j)

            def per_chunk(c, q):
                te[q] = e
                tj[q] = jnp.int32(j)
                trow[q] = pad_s[e] + c * (EXP_CHUNK * EXP_TM)
                tnb[q] = jnp.minimum(EXP_CHUNK, n - c * EXP_CHUNK)
                tfirst[q] = (c == 0).astype(jnp.int32)
                return q + 1

            return lax.fori_loop(0, nch, per_chunk, q), jnp.where(nch > 0, q, prev)

        carry = lax.fori_loop(0, N_EXPERTS, per_expert, carry)
    ttot[0] = carry[0]


def _route_tables_kernel(cntp_ref, cnts_ref, rip_ref, ris_ref, dest_ref, *refs, nj_in, nj_out):
    tabs_in, tabs_out = refs[0:8], refs[8:16]
    nblk_s, pad_s = refs[16:18]

    def per_expert(e, start_blk):
        n = (cntp_ref[e] + cnts_ref[e] + EXP_TM - 1) // EXP_TM
        nblk_s[e] = n
        pad_s[e] = start_blk * EXP_TM
        return start_blk + n

    lax.fori_loop(0, N_EXPERTS, per_expert, jnp.int32(0))
    _fill_work_list(nblk_s, pad_s, nj_in, tabs_in)
    _fill_work_list(nblk_s, pad_s, nj_out, tabs_out)

    tile = ROUTE_TM
    col0 = 0
    for ri_ref, is_sample in ((rip_ref, False), (ris_ref, True)):
        for r in range(ri_ref.shape[0] // tile):
            ri = ri_ref[r * tile:(r + 1) * tile, :]
            base = jnp.zeros_like(ri)
            for e in range(N_EXPERTS):
                first_row = pad_s[e] + cntp_ref[e] if is_sample else pad_s[e]
                base = jnp.where(ri == e, first_row, base)
            dest = base + pltpu.roll(ri, LANES - TOP_K, axis=1)
            dest_ref[:, col0:col0 + tile] = jnp.transpose(dest)[0:8, :]
            col0 += tile


def _route_tables(cnt_p, cnt_s, ri_p, ri_s, nj_in, nj_out, max_items):
    t_all = ri_p.shape[0] + ri_s.shape[0]
    assert ri_p.shape[0] % ROUTE_TM == 0 and ri_s.shape[0] % ROUTE_TM == 0
    smem = pl.BlockSpec(memory_space=pltpu.SMEM)

    def tables(nj):
        n = nj * max_items
        return [jax.ShapeDtypeStruct((n,), jnp.int32)] * 7 + [jax.ShapeDtypeStruct((1,), jnp.int32)]

    out = pl.pallas_call(
        functools.partial(_route_tables_kernel, nj_in=nj_in, nj_out=nj_out),
        grid=(1,),
        in_specs=[smem, smem,
                  pl.BlockSpec(ri_p.shape, lambda i: (0, 0)), pl.BlockSpec(ri_s.shape, lambda i: (0, 0))],
        out_specs=[pl.BlockSpec((8, t_all), lambda i: (0, 0))] + [smem] * 16,
        out_shape=[jax.ShapeDtypeStruct((8, t_all), jnp.int32)] + tables(nj_in) + tables(nj_out),
        scratch_shapes=[pltpu.SMEM((N_EXPERTS,), jnp.int32), pltpu.SMEM((N_EXPERTS,), jnp.int32)],
        compiler_params=pltpu.CompilerParams(
            dimension_semantics=("arbitrary",), vmem_limit_bytes=VMEM_LIMIT),
        name="route_tables",
    )(cnt_p, cnt_s, ri_p, ri_s)
    return out[0], tuple(out[1:9]), tuple(out[9:17])


def _sc_mesh():
    return plsc.VectorSubcoreMesh(core_axis_name="c", subcore_axis_name="s",
                                  num_cores=SC_CORES, num_subcores=SC_SUBCORES)


def _sc_worker_id():
    return lax.axis_index("s") * SC_CORES + lax.axis_index("c")


def _sc_dispatch(h2_a, h2_b, dest_by_slot, rows_out):
    ta, w = h2_a.shape
    t = ta + h2_b.shape[0]
    per_w = t // SC_WORKERS
    n_chunks = per_w // SC_CHUNK
    assert t == SC_WORKERS * n_chunks * SC_CHUNK and ta % SC_CHUNK == 0
    idx = dest_by_slot[:TOP_K].reshape(TOP_K, SC_WORKERS, n_chunks, SC_CHUNK).transpose(1, 0, 2, 3)

    def body(a_hbm, b_hbm, idx_hbm, xs_hbm, idx_v, rows_v, sems):
        base = _sc_worker_id() * per_w
        pltpu.sync_copy(idx_hbm.at[_sc_worker_id()], idx_v)
        pending = [[], []]
        for c in range(n_chunks):
            b = c % 2
            for d in pending[b]:
                d.wait()
            tok0 = base + c * SC_CHUNK

            @pl.when(tok0 < ta)
            def _():
                pltpu.sync_copy(a_hbm.at[pl.ds(tok0, SC_CHUNK)], rows_v.at[b])

            @pl.when(tok0 >= ta)
            def _():
                pltpu.sync_copy(b_hbm.at[pl.ds(tok0 - ta, SC_CHUNK)], rows_v.at[b])

            pending[b] = [pltpu.async_copy(rows_v.at[b], xs_hbm.at[idx_v.at[k, c]], sems.at[b])
                          for k in range(TOP_K)]
        for b in range(2):
            for d in pending[b]:
                d.wait()

    return pl.kernel(
        body,
        out_type=jax.ShapeDtypeStruct((rows_out, w), h2_a.dtype),
        mesh=_sc_mesh(),
        scratch_types=[pltpu.VMEM((TOP_K, n_chunks, SC_CHUNK), jnp.int32),
                       pltpu.VMEM((2, SC_CHUNK, w), h2_a.dtype),
                       pltpu.SemaphoreType.DMA((2,))],
        name="sc_dispatch",
    )(h2_a, h2_b, idx)


def _sc_gather_rows(table, idx):
    n = idx.shape[0]
    w = table.shape[1]
    per_w = n // SC_WORKERS
    n_chunks = per_w // SC_CHUNK
    assert n == SC_WORKERS * n_chunks * SC_CHUNK
    idx3 = idx.reshape(SC_WORKERS, n_chunks, SC_CHUNK)

    def body(table_hbm, idx_hbm, out_hbm, idx_v, rows_v, gsem, wsems):
        base = _sc_worker_id() * per_w
        pltpu.sync_copy(idx_hbm.at[_sc_worker_id()], idx_v)
        pending = [None, None]
        for c in range(n_chunks):
            b = c % 2
            if pending[b] is not None:
                pending[b].wait()
            pltpu.async_copy(table_hbm.at[idx_v.at[c]], rows_v.at[b], gsem).wait()
            pending[b] = pltpu.async_copy(
                rows_v.at[b], out_hbm.at[pl.ds(base + c * SC_CHUNK, SC_CHUNK)], wsems.at[b])
        for b in range(2):
            if pending[b] is not None:
                pending[b].wait()

    return pl.kernel(
        body,
        out_type=jax.ShapeDtypeStruct((n, w), table.dtype),
        mesh=_sc_mesh(),
        scratch_types=[pltpu.VMEM((n_chunks, SC_CHUNK), jnp.int32),
                       pltpu.VMEM((2, SC_CHUNK, w), table.dtype),
                       pltpu.SemaphoreType.DMA,
                       pltpu.SemaphoreType.DMA((2,))],
        name="sc_gather_rows",
    )(table, idx3)


def _grouped_pipeline(tabs, total_ref, weight_copies, cast_weights, in_copy, out_copy, compute):
    te, tj, _, tnb, tfirst, tne, tnj = tabs
    total = total_ref[0]

    def each_block(q, copy, op):
        slot = lax.rem(q, 2)
        for k in range(EXP_CHUNK):
            @pl.when(k < tnb[q])
            def _():
                op(copy(q, slot, k))

    def start(d):
        d.start()

    def wait(d):
        d.wait()

    for d in weight_copies(te[0], tj[0]):
        d.start()
    each_block(0, in_copy, start)

    def body(q, carry):
        each_block(q, in_copy, wait)

        @pl.when(q + 1 < total)
        def _():
            each_block(q + 1, in_copy, start)

        @pl.when(tfirst[q] == 1)
        def _():
            for d in weight_copies(te[q], tj[q]):
                d.wait()
            cast_weights()

            @pl.when(tne[q] >= 0)
            def _():
                for d in weight_copies(tne[q], tnj[q]):
                    d.start()

        @pl.when(q >= 2)
        def _():
            each_block(q - 2, out_copy, wait)

        for m in range(1, EXP_CHUNK + 1):
            @pl.when(tnb[q] == m)
            def _():
                compute(q, lax.rem(q, 2), m)

        each_block(q, out_copy, start)
        return carry

    lax.fori_loop(0, total, body, 0)

    @pl.when(total >= 2)
    def _():
        each_block(total - 2, out_copy, wait)

    each_block(total - 1, out_copy, wait)


def _expert_in_kernel(te, tj, trow, tnb, tfirst, tne, tnj, total_ref, xs_hbm, w_hbm, b_ref, act_hbm,
                      wstage, wb, x_buf, a_buf, w_sem, x_sem, a_sem):
    tm, tn = EXP_TM, a_buf.shape[2]
    nj = D_FF // tn

    def weight_copies(e, j):
        return [pltpu.make_async_copy(
            w_hbm.at[e, :, pl.ds(pl.multiple_of(half * D_FF + j * tn, tn), tn)], wstage.at[half], w_sem)
            for half in range(2)]

    def cast_weights():
        wb[...] = wstage[...].astype(BF16)

    def rows(q, k):
        return pl.ds(pl.multiple_of(trow[q] + k * tm, tm), tm)

    def x_copy(q, slot, k):
        return pltpu.make_async_copy(xs_hbm.at[rows(q, k)], x_buf.at[slot, pl.ds(k * tm, tm)], x_sem.at[slot])

    def a_copy(q, slot, k):
        return pltpu.make_async_copy(
            a_buf.at[slot, pl.ds(k * tm, tm)],
            act_hbm.at[rows(q, k), pl.ds(pl.multiple_of(tj[q] * tn, tn), tn)], a_sem.at[slot])

    def compute(q, slot, m):
        bias0 = te[q] * (2 * nj) + tj[q]
        x = jnp.concatenate(_unpack_bf16_pairs(x_buf[slot, 0:m * tm]), axis=1).astype(BF16)
        g = jnp.dot(x, wb[0], preferred_element_type=F32) + b_ref[bias0]
        up = jnp.dot(x, wb[1], preferred_element_type=F32) + b_ref[bias0 + nj]
        g = jnp.minimum(g, SWIGLU_LIMIT)
        up = jnp.clip(up, -SWIGLU_LIMIT, SWIGLU_LIMIT)
        a_buf[slot, 0:m * tm] = (g * _sigmoid(SWIGLU_ALPHA * g) * (up + 1.0)).astype(BF16)

    _grouped_pipeline((te, tj, trow, tnb, tfirst, tne, tnj), total_ref,
                      weight_copies, cast_weights, x_copy, a_copy, compute)


def _expert_in(tabs, xs, w_in, b_in):
    rows = xs.shape[0]
    tm, tn = EXP_TM, EXP_IN_TN
    nj = D_FF // tn
    bias = b_in.reshape(N_EXPERTS * 2 * nj, 1, tn)
    grid_spec = pltpu.PrefetchScalarGridSpec(
        num_scalar_prefetch=len(tabs),
        grid=(1,),
        in_specs=[
            pl.BlockSpec(memory_space=pl.ANY),
            pl.BlockSpec(memory_space=pl.ANY),
            pl.BlockSpec(bias.shape, lambda i, *_: (0, 0, 0), pipeline_mode=pl.Buffered(1)),
        ],
        out_specs=pl.BlockSpec(memory_space=pl.ANY),
        scratch_shapes=[pltpu.VMEM((2, D_MODEL, tn), F32), pltpu.VMEM((2, D_MODEL, tn), BF16),
                        pltpu.VMEM((2, EXP_CHUNK * tm, D_MODEL // 2), PACKED),
                        pltpu.VMEM((2, EXP_CHUNK * tm, tn), BF16),
                        pltpu.SemaphoreType.DMA, pltpu.SemaphoreType.DMA((2,)),
                        pltpu.SemaphoreType.DMA((2,))],
    )
    return pl.pallas_call(
        _expert_in_kernel,
        grid_spec=grid_spec,
        out_shape=jax.ShapeDtypeStruct((rows, D_FF), BF16),
        compiler_params=pltpu.CompilerParams(
            dimension_semantics=("arbitrary",), vmem_limit_bytes=VMEM_LIMIT),
        name="expert_in",
    )(*tabs, xs, w_in, bias)


def _expert_out_kernel(te, tj, trow, tnb, tfirst, tne, tnj, total_ref, act_hbm, w_hbm, b_ref, ys_hbm,
                       wstage, wb, a_buf, y_buf, w_sem, a_sem, y_sem):
    tm, tnp = EXP_TM, y_buf.shape[2]
    tn = 2 * tnp
    nj = D_MODEL // tn

    def weight_copies(e, j):
        return [pltpu.make_async_copy(
            w_hbm.at[e, :, pl.ds(pl.multiple_of(j * tn, tn), tn)], wstage, w_sem)]

    def cast_weights():
        wb[...] = wstage[...].astype(BF16)

    def rows(q, k):
        return pl.ds(pl.multiple_of(trow[q] + k * tm, tm), tm)

    def a_copy(q, slot, k):
        return pltpu.make_async_copy(act_hbm.at[rows(q, k)], a_buf.at[slot, pl.ds(k * tm, tm)], a_sem.at[slot])

    def y_copy(q, slot, k):
        return pltpu.make_async_copy(
            y_buf.at[slot, pl.ds(k * tm, tm)],
            ys_hbm.at[rows(q, k), pl.ds(pl.multiple_of(tj[q] * tnp, tnp), tnp)], y_sem.at[slot])

    def compute(q, slot, m):
        y = jnp.dot(a_buf[slot, 0:m * tm], wb[...], preferred_element_type=F32) + b_ref[te[q] * nj + tj[q]]
        y_buf[slot, 0:m * tm] = _pack_bf16_pairs(y)

    _grouped_pipeline((te, tj, trow, tnb, tfirst, tne, tnj), total_ref,
                      weight_copies, cast_weights, a_copy, y_copy, compute)


def _expert_out(tabs, act, w_out, b_out):
    rows = act.shape[0]
    tm, tn = EXP_TM, EXP_OUT_TN
    nj = D_MODEL // tn
    bias = b_out.reshape(N_EXPERTS * nj, 1, tn)
    grid_spec = pltpu.PrefetchScalarGridSpec(
        num_scalar_prefetch=len(tabs),
        grid=(1,),
        in_specs=[
            pl.BlockSpec(memory_space=pl.ANY),
            pl.BlockSpec(memory_space=pl.ANY),
            pl.BlockSpec(bias.shape, lambda i, *_: (0, 0, 0), pipeline_mode=pl.Buffered(1)),
        ],
        out_specs=pl.BlockSpec(memory_space=pl.ANY),
        scratch_shapes=[pltpu.VMEM((D_FF, tn), F32), pltpu.VMEM((D_FF, tn), BF16),
                        pltpu.VMEM((2, EXP_CHUNK * tm, D_FF), BF16),
                        pltpu.VMEM((2, EXP_CHUNK * tm, tn // 2), PACKED),
                        pltpu.SemaphoreType.DMA, pltpu.SemaphoreType.DMA((2,)),
                        pltpu.SemaphoreType.DMA((2,))],
    )
    return pl.pallas_call(
        _expert_out_kernel,
        grid_spec=grid_spec,
        out_shape=jax.ShapeDtypeStruct((rows, D_MODEL // 2), PACKED),
        compiler_params=pltpu.CompilerParams(
            dimension_semantics=("arbitrary",), vmem_limit_bytes=VMEM_LIMIT),
        name="expert_out",
    )(*tabs, act, w_out, bias)


def _combine_kernel(x1_ref, yg_ref, rw_ref, g_ref, o_ref):
    half = EXP_OUT_TN // 2
    pieces = []
    ssq = jnp.zeros((x1_ref.shape[0], 1), F32)
    for jt in range(D_MODEL // EXP_OUT_TN):
        acc_lo = x1_ref[:, jt * EXP_OUT_TN:jt * EXP_OUT_TN + half]
        acc_hi = x1_ref[:, jt * EXP_OUT_TN + half:(jt + 1) * EXP_OUT_TN]
        moe_lo = jnp.zeros_like(acc_lo)
        moe_hi = jnp.zeros_like(acc_hi)
        for k in range(TOP_K):
            lo, hi = _unpack_bf16_pairs(yg_ref[k, :, jt * half:(jt + 1) * half])
            moe_lo = moe_lo + lo * rw_ref[:, k:k + 1]
            moe_hi = moe_hi + hi * rw_ref[:, k:k + 1]
        for acc in (acc_lo + moe_lo, acc_hi + moe_hi):
            ssq = ssq + jnp.sum(acc * acc, axis=-1, keepdims=True)
            pieces.append(acc)
    inv = lax.rsqrt(ssq / D_MODEL + EPS)
    for n, acc in enumerate(pieces):
        o_ref[:, n * half:(n + 1) * half] = acc * inv * g_ref[:, n * half:(n + 1) * half]


def _combine(x1, yg, rw, gain, row_off, name):
    t = x1.shape[0]
    tm = COMB_TM
    assert t % tm == 0 and row_off % tm == 0
    off = row_off // tm
    return pl.pallas_call(
        _combine_kernel,
        grid=(t // tm,),
        in_specs=[
            pl.BlockSpec((tm, D_MODEL), lambda i: (i, 0)),
            pl.BlockSpec((TOP_K, tm, D_MODEL // 2), lambda i: (0, i + off, 0)),
            pl.BlockSpec((tm, LANES), lambda i: (i, 0)),
            pl.BlockSpec((1, D_MODEL), lambda i: (0, 0)),
        ],
        out_specs=pl.BlockSpec((tm, D_MODEL), lambda i: (i, 0)),
        out_shape=jax.ShapeDtypeStruct((t, D_MODEL), F32),
        compiler_params=pltpu.CompilerParams(
            dimension_semantics=("arbitrary",), vmem_limit_bytes=VMEM_LIMIT),
        name=name,
    )(x1, yg, rw, gain.reshape(1, D_MODEL))


def kernel(x_prompt, x_sample, state_pool, state_conv, cache_mem_k, cache_mem_v, mem_prompt,
           norm_mix, w_in, b_gate, w_pool_group, pool_scale, w_conv, mem_norm, w_mem_kv,
           w_pool_out, w_conv_out, w_attn_out, w_o, norm_ffn, w_router, b_router,
           w_exp_in, b_exp_in, w_exp_out, b_exp_out, final_norm):
    depth = norm_mix.shape[0]
    assert depth == 1
    l = 0
    bp, seq, _ = x_prompt.shape
    bs, ln, _ = x_sample.shape
    tp, ts = bp * seq, bs * ln

    kv = _norm_matmul(mem_prompt.reshape(bp * N_MEM, D_MODEL), mem_norm[l], w_mem_kv[l].astype(BF16),
                      jnp.zeros((2 * D_XATTN,), F32), 2 * D_XATTN, "mem_kv")
    mk = kv[:, :D_XATTN].reshape(bp, N_MEM, D_XATTN)
    mv = kv[:, D_XATTN:].reshape(bp, N_MEM, D_XATTN)

    bias_in = jnp.concatenate([jnp.zeros((D_MIX,), F32), b_gate[l]])
    xp = x_prompt.reshape(tp, D_MODEL)
    xs_ = x_sample.reshape(ts, D_MODEL)
    w_in_b = w_in[l].astype(BF16)
    proj_p = _norm_matmul(xp, norm_mix[l], w_in_b, bias_in, D_MIX, "proj_prompt")
    proj_s = _norm_matmul(xs_, norm_mix[l], w_in_b, bias_in, D_MIX, "proj_sample")

    br_p, zst_p = _mix_prompt(proj_p, mk, mv, w_pool_group[l], pool_scale[l], w_conv[l], bp, seq)
    br_s, zst_s = _mix_sample(proj_s, state_pool[l], state_conv[l],
                              cache_mem_k[l].reshape(bs, N_MEM * N_XHEADS, XHEAD_DIM),
                              cache_mem_v[l].reshape(bs, N_MEM * N_XHEADS, XHEAD_DIM),
                              w_pool_group[l], pool_scale[l], w_conv[l], bs, ln)

    wpo, wco, wao, wo = (w.astype(BF16) for w in (w_pool_out[l], w_conv_out[l], w_attn_out[l], w_o[l]))
    x1_p, h2_p, ri_p, rw_p, cnt_p = _merge_route(xp, br_p, proj_p, wpo, wco, wao, wo, norm_ffn[l],
                                                  w_router[l], b_router[l], "merge_route_prompt")
    x1_s, h2_s, ri_s, rw_s, cnt_s = _merge_route(xs_, br_s, proj_s, wpo, wco, wao, wo, norm_ffn[l],
                                                  w_router[l], b_router[l], "merge_route_sample")

    t_all = tp + ts
    n_assign = t_all * TOP_K
    nb_max = n_assign // EXP_TM + N_EXPERTS
    max_items = nb_max // EXP_CHUNK + N_EXPERTS
    dest, tabs_in, tabs_out = _route_tables(
        cnt_p[0, :N_EXPERTS].astype(jnp.int32), cnt_s[0, :N_EXPERTS].astype(jnp.int32), ri_p, ri_s,
        D_FF // EXP_IN_TN, D_MODEL // EXP_OUT_TN, max_items)
    xs_sorted = _sc_dispatch(h2_p, h2_s, dest, nb_max * EXP_TM)

    act = _expert_in(tabs_in, xs_sorted, w_exp_in[l], b_exp_in[l])
    ys = _expert_out(tabs_out, act, w_exp_out[l], b_exp_out[l])
    yg = _sc_gather_rows(ys, dest[:TOP_K].reshape(-1)).reshape(TOP_K, t_all, D_MODEL // 2)

    y_p = _combine(x1_p, yg, rw_p, final_norm, 0, "combine_prompt")
    y_s = _combine(x1_s, yg, rw_s, final_norm, tp, "combine_sample")

    new_pool_p = proj_p[:, :D_POOL].reshape(bp, seq, D_POOL)[:, seq - POOL_STATE_LEN:]
    new_conv_p = zst_p[:, 8 - (CONV_WIDTH - 1):]
    u_s = proj_s[:, :D_POOL].reshape(bs, ln, D_POOL)
    new_pool_s = jnp.concatenate([state_pool[l], u_s], axis=1)[:, -POOL_STATE_LEN:]
    new_conv_s = zst_s[:, ln - (CONV_WIDTH - 1):]

    return (y_p.reshape(bp, seq, D_MODEL), y_s.reshape(bs, ln, D_MODEL),
            new_pool_p[None], new_conv_p[None],
            mk.reshape(1, bp, N_MEM, N_XHEADS, XHEAD_DIM), mv.reshape(1, bp, N_MEM, N_XHEADS, XHEAD_DIM),
            new_pool_s[None], new_conv_s[None])
```

```python
import functools

import jax
import jax.numpy as jnp
from jax import lax
from jax.experimental import pallas as pl
from jax.experimental.pallas import tpu as pltpu
from jax.experimental.pallas import tpu_sc as plsc

F32 = jnp.float32
BF16 = jnp.bfloat16
PACKED = jnp.int32

D_MODEL = 2048
POOL_WINDOWS = (2, 4, 8, 16)
POOL_GROUP_DIM = 128
D_POOL = 512
POOL_STATE_LEN = 15
D_CONV = 1024
CONV_WIDTH = 3
N_MEM = 256
N_XHEADS = 4
XHEAD_DIM = 128
D_XATTN = 512
N_BRANCH = 3
D_MIX = D_POOL + 3 * D_CONV + D_XATTN
D_IN_TOTAL = D_MIX + N_BRANCH * D_MODEL
N_EXPERTS = 32
TOP_K = 4
D_FF = D_MODEL
SWIGLU_LIMIT = 7.0
SWIGLU_ALPHA = 1.702
EPS = 1e-5

C_U = 0
C_V = D_POOL
C_B = D_POOL + D_CONV
C_C = D_POOL + 2 * D_CONV
C_Q = D_POOL + 3 * D_CONV

LANES = 128
HIST = 16

PROJ_TM = 1024
PROJ_TN = 1024
MIX_TM = 256
MIX_NS = 8
MERGE_TM = 256
ROUTE_TM = 1024
N_TABS = 10
EXP_TM = 128
EXP_CHUNK = 4
EXP_IN_TN = 1024
EXP_OUT_TN = 2048
COMB_TM = 256
SC_CORES = 2
SC_SUBCORES = 16
SC_WORKERS = SC_CORES * SC_SUBCORES
SC_CHUNK = 32
VMEM_LIMIT = 56 * 1024 * 1024


def _sigmoid(x):
    return 0.5 * (jnp.tanh(0.5 * x) + 1.0)


def _rms(x, g):
    ms = jnp.mean(x * x, axis=-1, keepdims=True)
    return x * lax.rsqrt(ms + EPS) * g


def _pack_bf16_pairs(x):
    n = x.shape[1] // 2
    lo = lax.bitcast_convert_type(x[:, :n].astype(BF16).astype(F32), jnp.uint32)
    hi = lax.bitcast_convert_type(x[:, n:].astype(BF16).astype(F32), jnp.uint32)
    return lax.bitcast_convert_type((hi & jnp.uint32(0xFFFF0000)) | (lo >> 16), PACKED)


def _unpack_bf16_pairs(p):
    p = lax.bitcast_convert_type(p, jnp.uint32)
    lo = lax.bitcast_convert_type(p << 16, F32)
    hi = lax.bitcast_convert_type(p & jnp.uint32(0xFFFF0000), F32)
    return lo, hi


def _norm_matmul_kernel(x_ref, g_ref, w_ref, b_ref, o_ref, h_ref, *, act_from):
    j = pl.program_id(1)

    @pl.when(j == 0)
    def _():
        h_ref[...] = _rms(x_ref[...], g_ref[...]).astype(BF16)

    acc = jnp.dot(h_ref[...], w_ref[...], preferred_element_type=F32) + b_ref[...]

    @pl.when(j < act_from)
    def _():
        o_ref[...] = acc

    @pl.when(j >= act_from)
    def _():
        o_ref[...] = _sigmoid(acc)


def _norm_matmul(x, gain, w, bias, act_from_col, name):
    t, d = x.shape
    n = w.shape[1]
    tm = min(PROJ_TM, t)
    tn = PROJ_TN
    assert t % tm == 0 and n % tn == 0 and act_from_col % tn == 0
    return pl.pallas_call(
        functools.partial(_norm_matmul_kernel, act_from=act_from_col // tn),
        grid=(t // tm, n // tn),
        in_specs=[
            pl.BlockSpec((tm, d), lambda i, j: (i, 0)),
            pl.BlockSpec((1, d), lambda i, j: (0, 0)),
            pl.BlockSpec((d, tn), lambda i, j: (0, j)),
            pl.BlockSpec((1, tn), lambda i, j: (0, j)),
        ],
        out_specs=pl.BlockSpec((tm, tn), lambda i, j: (i, j)),
        out_shape=jax.ShapeDtypeStruct((t, n), F32),
        scratch_shapes=[pltpu.VMEM((tm, d), BF16)],
        compiler_params=pltpu.CompilerParams(
            dimension_semantics=("arbitrary", "arbitrary"), vmem_limit_bytes=VMEM_LIMIT),
        name=name,
    )(x, gain.reshape(1, d), w, bias.reshape(1, n))


def _pool_project(pooled, wpg_ref, scale_ref, g):
    sl = slice(g * POOL_GROUP_DIM, (g + 1) * POOL_GROUP_DIM)
    y = jnp.dot(pooled.astype(BF16), wpg_ref[g].astype(BF16), preferred_element_type=F32)
    return y * scale_ref[:, sl]


def _softmax_rows(s):
    m = jnp.max(s, axis=-1, keepdims=True)
    e = jnp.exp(s - m)
    return e / jnp.sum(e, axis=-1, keepdims=True)


def _mix_prompt_kernel(cur_ref, prev_ref, mk_ref, mv_ref, wpg_ref, scale_ref, wconv_ref,
                       br_ref, zst_ref, extu_ref, extz_ref):
    t = pl.program_id(1)
    tm = cur_ref.shape[0]
    has_prev = t > 0

    u = cur_ref[:, C_U:C_U + D_POOL]
    extu_ref[0:HIST, :] = jnp.where(has_prev, prev_ref[:, C_U:C_U + D_POOL], 0.0)
    extu_ref[HIST:HIST + tm, :] = u
    pos = t * tm + lax.broadcasted_iota(jnp.int32, (tm, 1), 0)
    for g, w in enumerate(POOL_WINDOWS):
        sl = slice(g * POOL_GROUP_DIM, (g + 1) * POOL_GROUP_DIM)
        s = extu_ref[HIST:HIST + tm, sl]
        for k in range(1, w):
            s = s + extu_ref[HIST - k:HIST - k + tm, sl]
        cnt = jnp.minimum(w, pos + 1).astype(F32)
        pooled = s / cnt - extu_ref[HIST:HIST + tm, sl]
        br_ref[:, sl] = _pool_project(pooled, wpg_ref, scale_ref, g).astype(BF16)

    z = cur_ref[:, C_C:C_C + D_CONV] * cur_ref[:, C_V:C_V + D_CONV]
    zprev = prev_ref[:, C_C:C_C + D_CONV] * prev_ref[:, C_V:C_V + D_CONV]
    extz_ref[0:HIST, :] = jnp.where(has_prev, zprev, 0.0)
    extz_ref[HIST:HIST + tm, :] = z
    y = extz_ref[HIST - 2:HIST - 2 + tm, :] * wconv_ref[0:1, :]
    y = y + extz_ref[HIST - 1:HIST - 1 + tm, :] * wconv_ref[1:2, :]
    y = y + extz_ref[HIST:HIST + tm, :] * wconv_ref[2:3, :]
    br_ref[:, D_POOL:D_POOL + D_CONV] = (cur_ref[:, C_B:C_B + D_CONV] * y).astype(BF16)
    zst_ref[0] = extz_ref[HIST + tm - 8:HIST + tm, :]

    for h in range(N_XHEADS):
        sl = slice(h * XHEAD_DIM, (h + 1) * XHEAD_DIM)
        qh = cur_ref[:, C_Q + h * XHEAD_DIM:C_Q + (h + 1) * XHEAD_DIM].astype(BF16)
        kh = mk_ref[0, :, sl].astype(BF16)
        vh = mv_ref[0, :, sl].astype(BF16)
        s = lax.dot_general(qh, kh, (((1,), (1,)), ((), ())), preferred_element_type=F32)
        p = _softmax_rows(s * (XHEAD_DIM ** -0.5))
        o = jnp.dot(p.astype(BF16), vh, preferred_element_type=F32)
        c0 = D_POOL + D_CONV + h * XHEAD_DIM
        br_ref[:, c0:c0 + XHEAD_DIM] = o.astype(BF16)


def _mix_prompt(proj, mk, mv, wpg, scale, wconv, batch, seq):
    tm = MIX_TM
    nt = seq // tm
    assert seq % tm == 0 and tm % HIST == 0
    rpb = tm // HIST
    return pl.pallas_call(
        _mix_prompt_kernel,
        grid=(batch, nt),
        in_specs=[
            pl.BlockSpec((tm, D_MIX), lambda b, t: (b * nt + t, 0)),
            pl.BlockSpec((HIST, D_MIX), lambda b, t: (jnp.maximum((b * nt + t) * rpb - 1, 0), 0)),
            pl.BlockSpec((1, N_MEM, D_XATTN), lambda b, t: (b, 0, 0)),
            pl.BlockSpec((1, N_MEM, D_XATTN), lambda b, t: (b, 0, 0)),
            pl.BlockSpec((len(POOL_WINDOWS), POOL_GROUP_DIM, POOL_GROUP_DIM), lambda b, t: (0, 0, 0)),
            pl.BlockSpec((1, D_POOL), lambda b, t: (0, 0)),
            pl.BlockSpec((CONV_WIDTH, D_CONV), lambda b, t: (0, 0)),
        ],
        out_specs=[
            pl.BlockSpec((tm, D_MODEL), lambda b, t: (b * nt + t, 0)),
            pl.BlockSpec((1, 8, D_CONV), lambda b, t: (b, 0, 0)),
        ],
        out_shape=[
            jax.ShapeDtypeStruct((batch * seq, D_MODEL), BF16),
            jax.ShapeDtypeStruct((batch, 8, D_CONV), F32),
        ],
        scratch_shapes=[pltpu.VMEM((HIST + tm, D_POOL), F32), pltpu.VMEM((HIST + tm, D_CONV), F32)],
        compiler_params=pltpu.CompilerParams(
            dimension_semantics=("arbitrary", "arbitrary"), vmem_limit_bytes=VMEM_LIMIT),
        name="mix_prompt",
    )(proj, proj, mk, mv, wpg, scale.reshape(1, D_POOL), wconv)


def _mix_sample_kernel(cur_ref, sp_ref, sc_ref, k_ref, v_ref, wpg_ref, scale_ref, wconv_ref,
                       br_ref, zst_ref, extu_ref, extz_ref):
    ns, ln = zst_ref.shape[0], zst_ref.shape[1]
    rows = ns * ln

    extu_ref[:, HIST - POOL_STATE_LEN:HIST, :] = sp_ref[...]
    extu_ref[:, HIST:HIST + ln, :] = cur_ref[:, C_U:C_U + D_POOL].reshape(ns, ln, D_POOL)
    for g, w in enumerate(POOL_WINDOWS):
        sl = slice(g * POOL_GROUP_DIM, (g + 1) * POOL_GROUP_DIM)
        s = extu_ref[:, HIST:HIST + ln, sl]
        for k in range(1, w):
            s = s + extu_ref[:, HIST - k:HIST - k + ln, sl]
        pooled = s / float(w) - extu_ref[:, HIST:HIST + ln, sl]
        pooled = pooled.reshape(rows, POOL_GROUP_DIM)
        br_ref[:, sl] = _pool_project(pooled, wpg_ref, scale_ref, g).astype(BF16)

    z = cur_ref[:, C_C:C_C + D_CONV] * cur_ref[:, C_V:C_V + D_CONV]
    extz_ref[:, HIST - 2:HIST, :] = sc_ref[...]
    extz_ref[:, HIST:HIST + ln, :] = z.reshape(ns, ln, D_CONV)
    y = extz_ref[:, HIST - 2:HIST - 2 + ln, :] * wconv_ref[0:1, :]
    y = y + extz_ref[:, HIST - 1:HIST - 1 + ln, :] * wconv_ref[1:2, :]
    y = y + extz_ref[:, HIST:HIST + ln, :] * wconv_ref[2:3, :]
    br_ref[:, D_POOL:D_POOL + D_CONV] = (
        cur_ref[:, C_B:C_B + D_CONV] * y.reshape(rows, D_CONV)).astype(BF16)
    zst_ref[...] = extz_ref[:, HIST:HIST + ln, :]

    q3 = cur_ref[:, C_Q:C_Q + D_XATTN].reshape(ns, ln, D_XATTN)
    q4 = jnp.concatenate([q3[:, :, h * XHEAD_DIM:(h + 1) * XHEAD_DIM] for h in range(N_XHEADS)], axis=1)
    s = jnp.einsum("nqd,nkd->nqk", q4.astype(BF16), k_ref[...].astype(BF16),
                   preferred_element_type=F32)
    row_head = lax.broadcasted_iota(jnp.int32, (N_XHEADS * ln, N_XHEADS * N_MEM), 0) // ln
    col_head = lax.broadcasted_iota(jnp.int32, (N_XHEADS * ln, N_XHEADS * N_MEM), 1) % N_XHEADS
    s = jnp.where((row_head == col_head)[None], s * (XHEAD_DIM ** -0.5), -jnp.inf)
    p = _softmax_rows(s)
    r = jnp.einsum("nqk,nkd->nqd", p.astype(BF16), v_ref[...].astype(BF16),
                   preferred_element_type=F32)
    for h in range(N_XHEADS):
        c0 = D_POOL + D_CONV + h * XHEAD_DIM
        br_ref[:, c0:c0 + XHEAD_DIM] = r[:, h * ln:(h + 1) * ln, :].reshape(rows, XHEAD_DIM).astype(BF16)


def _mix_sample(proj, state_pool, state_conv, mem_k, mem_v, wpg, scale, wconv, nseq, ln):
    ns = MIX_NS
    rows = ns * ln
    assert nseq % ns == 0 and ln == 8
    return pl.pallas_call(
        _mix_sample_kernel,
        grid=(nseq // ns,),
        in_specs=[
            pl.BlockSpec((rows, D_MIX), lambda s: (s, 0)),
            pl.BlockSpec((ns, POOL_STATE_LEN, D_POOL), lambda s: (s, 0, 0)),
            pl.BlockSpec((ns, CONV_WIDTH - 1, D_CONV), lambda s: (s, 0, 0)),
            pl.BlockSpec((ns, N_MEM * N_XHEADS, XHEAD_DIM), lambda s: (s, 0, 0)),
            pl.BlockSpec((ns, N_MEM * N_XHEADS, XHEAD_DIM), lambda s: (s, 0, 0)),
            pl.BlockSpec((len(POOL_WINDOWS), POOL_GROUP_DIM, POOL_GROUP_DIM), lambda s: (0, 0, 0)),
            pl.BlockSpec((1, D_POOL), lambda s: (0, 0)),
            pl.BlockSpec((CONV_WIDTH, D_CONV), lambda s: (0, 0)),
        ],
        out_specs=[
            pl.BlockSpec((rows, D_MODEL), lambda s: (s, 0)),
            pl.BlockSpec((ns, ln, D_CONV), lambda s: (s, 0, 0)),
        ],
        out_shape=[
            jax.ShapeDtypeStruct((nseq * ln, D_MODEL), BF16),
            jax.ShapeDtypeStruct((nseq, ln, D_CONV), F32),
        ],
        scratch_shapes=[pltpu.VMEM((ns, HIST + ln, D_POOL), F32),
                        pltpu.VMEM((ns, HIST + ln, D_CONV), F32)],
        compiler_params=pltpu.CompilerParams(
            dimension_semantics=("arbitrary",), vmem_limit_bytes=VMEM_LIMIT),
        name="mix_sample",
    )(proj, state_pool, state_conv, mem_k, mem_v, wpg, scale.reshape(1, D_POOL), wconv)


def _merge_route_kernel(x_ref, br_ref, g0_ref, g1_ref, g2_ref, wpo_ref, wco_ref, wao_ref, wo_ref,
                        nf_ref, wrh_ref, wrl_ref, brt_ref,
                        x1_ref, h2_ref, ri_ref, rw_ref, cnt_ref, carry_ref):
    i = pl.program_id(0)
    tm = x_ref.shape[0]

    @pl.when(i == 0)
    def _():
        carry_ref[...] = jnp.zeros_like(carry_ref)

    merged = g0_ref[...] * jnp.dot(br_ref[:, 0:D_POOL], wpo_ref[...], preferred_element_type=F32)
    merged = merged + g1_ref[...] * jnp.dot(br_ref[:, D_POOL:D_POOL + D_CONV], wco_ref[...],
                                            preferred_element_type=F32)
    merged = merged + g2_ref[...] * jnp.dot(br_ref[:, D_POOL + D_CONV:D_MODEL], wao_ref[...],
                                            preferred_element_type=F32)
    x1 = x_ref[...] + jnp.dot(merged.astype(BF16), wo_ref[...], preferred_element_type=F32)
    x1_ref[...] = x1
    h2 = _rms(x1, nf_ref[...])
    h2_ref[...] = _pack_bf16_pairs(h2)

    h2_hi = h2.astype(BF16)
    h2_lo = (h2 - h2_hi.astype(F32)).astype(BF16)
    logits = (jnp.dot(h2_hi, wrh_ref[...], preferred_element_type=F32)
              + jnp.dot(h2_hi, wrl_ref[...], preferred_element_type=F32)
              + jnp.dot(h2_lo, wrh_ref[...], preferred_element_type=F32)) + brt_ref[...]
    lane = lax.broadcasted_iota(jnp.int32, (tm, N_EXPERTS), 1).astype(F32)
    vals, idxs, hots = [], [], []
    work = logits
    for _ in range(TOP_K):
        m = jnp.max(work, axis=-1, keepdims=True)
        idx = jnp.min(jnp.where(work == m, lane, float(N_EXPERTS)), axis=-1, keepdims=True)
        hot = lane == idx
        work = jnp.where(hot, -jnp.inf, work)
        vals.append(m)
        idxs.append(idx)
        hots.append(hot)
    es = [jnp.exp(v - vals[0]) for v in vals]
    denom = es[0] + es[1] + es[2] + es[3]

    chosen = jnp.where(hots[0] | hots[1] | hots[2] | hots[3], 1.0, 0.0).astype(BF16)
    r_i = lax.broadcasted_iota(jnp.int32, (tm, tm), 0)
    c_i = lax.broadcasted_iota(jnp.int32, (tm, tm), 1)
    lower = jnp.where(c_i < r_i, 1.0, 0.0).astype(BF16)
    before = jnp.dot(lower, chosen, preferred_element_type=F32) + carry_ref[0:1, 0:N_EXPERTS]
    carry_ref[0:1, 0:N_EXPERTS] = (carry_ref[0:1, 0:N_EXPERTS]
                                   + jnp.sum(chosen.astype(F32), axis=0, keepdims=True))
    cnt_ref[...] = carry_ref[...]

    out_lane = lax.broadcasted_iota(jnp.int32, (tm, LANES), 1)
    ri = jnp.zeros((tm, LANES), jnp.int32)
    rw = jnp.zeros((tm, LANES), F32)
    for k in range(TOP_K):
        rank = jnp.sum(jnp.where(hots[k], before, 0.0), axis=-1, keepdims=True).astype(jnp.int32)
        ri = jnp.where(out_lane == k, idxs[k].astype(jnp.int32), ri)
        ri = jnp.where(out_lane == TOP_K + k, rank, ri)
        rw = jnp.where(out_lane == k, es[k] / denom, rw)
    ri_ref[...] = ri
    rw_ref[...] = rw


def _merge_route(x, branch, proj, wpo, wco, wao, wo, norm_ffn, w_router, b_router, name):
    t = x.shape[0]
    tm = MERGE_TM
    assert t % tm == 0
    gate_blk0 = D_MIX // D_MODEL
    assert D_MIX % D_MODEL == 0
    const = lambda i: (0, 0)
    wr_hi = w_router.astype(BF16)
    wr_lo = (w_router - wr_hi.astype(F32)).astype(BF16)
    return pl.pallas_call(
        _merge_route_kernel,
        grid=(t // tm,),
        in_specs=[
            pl.BlockSpec((tm, D_MODEL), lambda i: (i, 0)),
            pl.BlockSpec((tm, D_MODEL), lambda i: (i, 0)),
            pl.BlockSpec((tm, D_MODEL), lambda i: (i, gate_blk0)),
            pl.BlockSpec((tm, D_MODEL), lambda i: (i, gate_blk0 + 1)),
            pl.BlockSpec((tm, D_MODEL), lambda i: (i, gate_blk0 + 2)),
            pl.BlockSpec((D_POOL, D_MODEL), const, pipeline_mode=pl.Buffered(1)),
            pl.BlockSpec((D_CONV, D_MODEL), const, pipeline_mode=pl.Buffered(1)),
            pl.BlockSpec((D_XATTN, D_MODEL), const, pipeline_mode=pl.Buffered(1)),
            pl.BlockSpec((D_MODEL, D_MODEL), const, pipeline_mode=pl.Buffered(1)),
            pl.BlockSpec((1, D_MODEL), const),
            pl.BlockSpec((D_MODEL, N_EXPERTS), const),
            pl.BlockSpec((D_MODEL, N_EXPERTS), const),
            pl.BlockSpec((1, N_EXPERTS), const),
        ],
        out_specs=[
            pl.BlockSpec((tm, D_MODEL), lambda i: (i, 0)),
            pl.BlockSpec((tm, D_MODEL // 2), lambda i: (i, 0)),
            pl.BlockSpec((tm, LANES), lambda i: (i, 0)),
            pl.BlockSpec((tm, LANES), lambda i: (i, 0)),
            pl.BlockSpec((8, LANES), const),
        ],
        out_shape=[
            jax.ShapeDtypeStruct((t, D_MODEL), F32),
            jax.ShapeDtypeStruct((t, D_MODEL // 2), PACKED),
            jax.ShapeDtypeStruct((t, LANES), jnp.int32),
            jax.ShapeDtypeStruct((t, LANES), F32),
            jax.ShapeDtypeStruct((8, LANES), F32),
        ],
        scratch_shapes=[pltpu.VMEM((8, LANES), F32)],
        compiler_params=pltpu.CompilerParams(
            dimension_semantics=("arbitrary",), vmem_limit_bytes=VMEM_LIMIT),
        name=name,
    )(x, branch, proj, proj, proj, wpo, wco, wao, wo, norm_ffn.reshape(1, D_MODEL),
      wr_hi, wr_lo, b_router.reshape(1, N_EXPERTS))


def _fill_work_list(nblk_s, pad_s, nj, refs):
    te, tj, trow, tnb, tfirst, tne, tnj, tcast, tpar, ttot = refs

    def clear(i, c):
        for r in (te, tj, trow, tnb, tfirst, tcast, tpar):
            r[i] = 0
        tne[i] = -1
        tnj[i] = -1
        return c

    lax.fori_loop(0, te.shape[0], clear, 0)

    carry = (jnp.int32(0), jnp.int32(-1), jnp.int32(0))
    for j in range(nj):
        def per_expert(e, carry):
            q, prev, groups = carry
            n = nblk_s[e]
            nch = (n + EXP_CHUNK - 1) // EXP_CHUNK

            @pl.when((nch > 0) & (prev >= 0))
            def _():
                tne[prev] = e
                tnj[prev] = jnp.int32(j)
                tcast[q - 1] = 1

            def per_chunk(c, q):
                te[q] = e
                tj[q] = jnp.int32(j)
                trow[q] = pad_s[e] + c * (EXP_CHUNK * EXP_TM)
                tnb[q] = jnp.minimum(EXP_CHUNK, n - c * EXP_CHUNK)
                tfirst[q] = (c == 0).astype(jnp.int32)
                tpar[q] = lax.rem(groups, 2)
                return q + 1

            return (lax.fori_loop(0, nch, per_chunk, q), jnp.where(nch > 0, q, prev),
                    groups + (nch > 0).astype(jnp.int32))

        carry = lax.fori_loop(0, N_EXPERTS, per_expert, carry)
    ttot[0] = carry[0]


def _route_tables_kernel(cntp_ref, cnts_ref, rip_ref, ris_ref, dest_ref, *refs, nj_in, nj_out):
    tabs_in, tabs_out = refs[0:N_TABS], refs[N_TABS:2 * N_TABS]
    nblk_s, pad_s = refs[2 * N_TABS:]

    def per_expert(e, start_blk):
        n = (cntp_ref[e] + cnts_ref[e] + EXP_TM - 1) // EXP_TM
        nblk_s[e] = n
        pad_s[e] = start_blk * EXP_TM
        return start_blk + n

    lax.fori_loop(0, N_EXPERTS, per_expert, jnp.int32(0))
    _fill_work_list(nblk_s, pad_s, nj_in, tabs_in)
    _fill_work_list(nblk_s, pad_s, nj_out, tabs_out)

    tile = ROUTE_TM
    col0 = 0
    for ri_ref, is_sample in ((rip_ref, False), (ris_ref, True)):
        for r in range(ri_ref.shape[0] // tile):
            ri = ri_ref[r * tile:(r + 1) * tile, :]
            base = jnp.zeros_like(ri)
            for e in range(N_EXPERTS):
                first_row = pad_s[e] + cntp_ref[e] if is_sample else pad_s[e]
                base = jnp.where(ri == e, first_row, base)
            dest = base + pltpu.roll(ri, LANES - TOP_K, axis=1)
            dest_ref[:, col0:col0 + tile] = jnp.transpose(dest)[0:8, :]
            col0 += tile


def _route_tables(cnt_p, cnt_s, ri_p, ri_s, nj_in, nj_out, max_items):
    t_all = ri_p.shape[0] + ri_s.shape[0]
    assert ri_p.shape[0] % ROUTE_TM == 0 and ri_s.shape[0] % ROUTE_TM == 0
    smem = pl.BlockSpec(memory_space=pltpu.SMEM)

    def tables(nj):
        n = nj * max_items
        return [jax.ShapeDtypeStruct((n,), jnp.int32)] * (N_TABS - 1) + [jax.ShapeDtypeStruct((1,), jnp.int32)]

    out = pl.pallas_call(
        functools.partial(_route_tables_kernel, nj_in=nj_in, nj_out=nj_out),
        grid=(1,),
        in_specs=[smem, smem,
                  pl.BlockSpec(ri_p.shape, lambda i: (0, 0)), pl.BlockSpec(ri_s.shape, lambda i: (0, 0))],
        out_specs=[pl.BlockSpec((8, t_all), lambda i: (0, 0))] + [smem] * (2 * N_TABS),
        out_shape=[jax.ShapeDtypeStruct((8, t_all), jnp.int32)] + tables(nj_in) + tables(nj_out),
        scratch_shapes=[pltpu.SMEM((N_EXPERTS,), jnp.int32), pltpu.SMEM((N_EXPERTS,), jnp.int32)],
        compiler_params=pltpu.CompilerParams(
            dimension_semantics=("arbitrary",), vmem_limit_bytes=VMEM_LIMIT),
        name="route_tables",
    )(cnt_p, cnt_s, ri_p, ri_s)
    return out[0], tuple(out[1:1 + N_TABS]), tuple(out[1 + N_TABS:1 + 2 * N_TABS])


def _sc_mesh():
    return plsc.VectorSubcoreMesh(core_axis_name="c", subcore_axis_name="s",
                                  num_cores=SC_CORES, num_subcores=SC_SUBCORES)


def _sc_worker_id():
    return lax.axis_index("s") * SC_CORES + lax.axis_index("c")


def _sc_dispatch(h2_a, h2_b, dest_by_slot, rows_out):
    ta, w = h2_a.shape
    t = ta + h2_b.shape[0]
    per_w = t // SC_WORKERS
    n_chunks = per_w // SC_CHUNK
    assert t == SC_WORKERS * n_chunks * SC_CHUNK and ta % SC_CHUNK == 0
    idx = dest_by_slot[:TOP_K].reshape(TOP_K, SC_WORKERS, n_chunks, SC_CHUNK).transpose(1, 0, 2, 3)

    def body(a_hbm, b_hbm, idx_hbm, xs_hbm, idx_v, rows_v, sems):
        base = _sc_worker_id() * per_w
        pltpu.sync_copy(idx_hbm.at[_sc_worker_id()], idx_v)
        pending = [[], []]
        for c in range(n_chunks):
            b = c % 2
            for d in pending[b]:
                d.wait()
            tok0 = base + c * SC_CHUNK

            @pl.when(tok0 < ta)
            def _():
                pltpu.sync_copy(a_hbm.at[pl.ds(tok0, SC_CHUNK)], rows_v.at[b])

            @pl.when(tok0 >= ta)
            def _():
                pltpu.sync_copy(b_hbm.at[pl.ds(tok0 - ta, SC_CHUNK)], rows_v.at[b])

            pending[b] = [pltpu.async_copy(rows_v.at[b], xs_hbm.at[idx_v.at[k, c]], sems.at[b])
                          for k in range(TOP_K)]
        for b in range(2):
            for d in pending[b]:
                d.wait()

    return pl.kernel(
        body,
        out_type=jax.ShapeDtypeStruct((rows_out, w), h2_a.dtype),
        mesh=_sc_mesh(),
        scratch_types=[pltpu.VMEM((TOP_K, n_chunks, SC_CHUNK), jnp.int32),
                       pltpu.VMEM((2, SC_CHUNK, w), h2_a.dtype),
                       pltpu.SemaphoreType.DMA((2,))],
        name="sc_dispatch",
    )(h2_a, h2_b, idx)


def _sc_gather_rows(table, idx):
    n = idx.shape[0]
    w = table.shape[1]
    per_w = n // SC_WORKERS
    n_chunks = per_w // SC_CHUNK
    assert n == SC_WORKERS * n_chunks * SC_CHUNK
    idx3 = idx.reshape(SC_WORKERS, n_chunks, SC_CHUNK)

    def body(table_hbm, idx_hbm, out_hbm, idx_v, rows_v, gsem, wsems):
        base = _sc_worker_id() * per_w
        pltpu.sync_copy(idx_hbm.at[_sc_worker_id()], idx_v)
        pending = [None, None]
        for c in range(n_chunks):
            b = c % 2
            if pending[b] is not None:
                pending[b].wait()
            pltpu.async_copy(table_hbm.at[idx_v.at[c]], rows_v.at[b], gsem).wait()
            pending[b] = pltpu.async_copy(
                rows_v.at[b], out_hbm.at[pl.ds(base + c * SC_CHUNK, SC_CHUNK)], wsems.at[b])
        for b in range(2):
            if pending[b] is not None:
                pending[b].wait()

    return pl.kernel(
        body,
        out_type=jax.ShapeDtypeStruct((n, w), table.dtype),
        mesh=_sc_mesh(),
        scratch_types=[pltpu.VMEM((n_chunks, SC_CHUNK), jnp.int32),
                       pltpu.VMEM((2, SC_CHUNK, w), table.dtype),
                       pltpu.SemaphoreType.DMA,
                       pltpu.SemaphoreType.DMA((2,))],
        name="sc_gather_rows",
    )(table, idx3)


def _grouped_pipeline(tabs, total_ref, weight_copies, cast_weights, in_copy, out_copy, compute):
    te, tj, _, tnb, tfirst, tne, tnj, tcast, tpar = tabs
    total = total_ref[0]

    def each_block(q, copy, op):
        slot = lax.rem(q, 2)
        for k in range(EXP_CHUNK):
            @pl.when(k < tnb[q])
            def _():
                op(copy(q, slot, k))

    def start(d):
        d.start()

    def wait(d):
        d.wait()

    for d in weight_copies(te[0], tj[0]):
        d.start()
    each_block(0, in_copy, start)
    for d in weight_copies(te[0], tj[0]):
        d.wait()
    cast_weights(tpar[0])

    def body(q, carry):
        each_block(q, in_copy, wait)

        @pl.when(q + 1 < total)
        def _():
            each_block(q + 1, in_copy, start)

        @pl.when((tfirst[q] == 1) & (tne[q] >= 0))
        def _():
            for d in weight_copies(tne[q], tnj[q]):
                d.start()

        @pl.when(q >= 2)
        def _():
            each_block(q - 2, out_copy, wait)

        @pl.when(tcast[q] == 1)
        def _():
            for d in weight_copies(te[q], tj[q]):
                d.wait()

        for m in range(1, EXP_CHUNK + 1):
            for cast_next in (0, 1):
                @pl.when((tnb[q] == m) & (tcast[q] == cast_next))
                def _():
                    compute(q, lax.rem(q, 2), m, tpar[q], bool(cast_next))

        each_block(q, out_copy, start)
        return carry

    lax.fori_loop(0, total, body, 0)

    @pl.when(total >= 2)
    def _():
        each_block(total - 2, out_copy, wait)

    each_block(total - 1, out_copy, wait)


def _expert_in_kernel(*refs):
    tabs, total_ref = refs[:N_TABS - 1], refs[N_TABS - 1]
    xs_hbm, w_hbm, b_ref, act_hbm, wstage, wb, x_buf, a_buf, w_sem, x_sem, a_sem = refs[N_TABS:]
    te, tj, trow = tabs[:3]
    tm, tn = EXP_TM, a_buf.shape[2]
    nj = D_FF // tn

    def weight_copies(e, j):
        return [pltpu.make_async_copy(
            w_hbm.at[e, :, pl.ds(pl.multiple_of(half * D_FF + j * tn, tn), tn)], wstage.at[half], w_sem)
            for half in range(2)]

    def cast_weights(par):
        for half in range(2):
            wb[2 * par + half] = wstage[half].astype(BF16)

    def rows(q, k):
        return pl.ds(pl.multiple_of(trow[q] + k * tm, tm), tm)

    def x_copy(q, slot, k):
        return pltpu.make_async_copy(xs_hbm.at[rows(q, k)], x_buf.at[slot, pl.ds(k * tm, tm)], x_sem.at[slot])

    def a_copy(q, slot, k):
        return pltpu.make_async_copy(
            a_buf.at[slot, pl.ds(k * tm, tm)],
            act_hbm.at[rows(q, k), pl.ds(pl.multiple_of(tj[q] * tn, tn), tn)], a_sem.at[slot])

    def compute(q, slot, m, par, cast_next):
        bias0 = te[q] * (2 * nj) + tj[q]
        x = jnp.concatenate(_unpack_bf16_pairs(x_buf[slot, 0:m * tm]), axis=1).astype(BF16)
        g = jnp.dot(x, wb[2 * par], preferred_element_type=F32) + b_ref[bias0]
        up = jnp.dot(x, wb[2 * par + 1], preferred_element_type=F32) + b_ref[bias0 + nj]
        if cast_next:
            cast_weights(1 - par)
        g = jnp.minimum(g, SWIGLU_LIMIT)
        up = jnp.clip(up, -SWIGLU_LIMIT, SWIGLU_LIMIT)
        a_buf[slot, 0:m * tm] = (g * _sigmoid(SWIGLU_ALPHA * g) * (up + 1.0)).astype(BF16)

    _grouped_pipeline(tabs, total_ref, weight_copies, cast_weights, x_copy, a_copy, compute)


def _expert_in(tabs, xs, w_in, b_in):
    rows = xs.shape[0]
    tm, tn = EXP_TM, EXP_IN_TN
    nj = D_FF // tn
    bias = b_in.reshape(N_EXPERTS * 2 * nj, 1, tn)
    grid_spec = pltpu.PrefetchScalarGridSpec(
        num_scalar_prefetch=len(tabs),
        grid=(1,),
        in_specs=[
            pl.BlockSpec(memory_space=pl.ANY),
            pl.BlockSpec(memory_space=pl.ANY),
            pl.BlockSpec(bias.shape, lambda i, *_: (0, 0, 0), pipeline_mode=pl.Buffered(1)),
        ],
        out_specs=pl.BlockSpec(memory_space=pl.ANY),
        scratch_shapes=[pltpu.VMEM((2, D_MODEL, tn), F32), pltpu.VMEM((4, D_MODEL, tn), BF16),
                        pltpu.VMEM((2, EXP_CHUNK * tm, D_MODEL // 2), PACKED),
                        pltpu.VMEM((2, EXP_CHUNK * tm, tn), BF16),
                        pltpu.SemaphoreType.DMA, pltpu.SemaphoreType.DMA((2,)),
                        pltpu.SemaphoreType.DMA((2,))],
    )
    return pl.pallas_call(
        _expert_in_kernel,
        grid_spec=grid_spec,
        out_shape=jax.ShapeDtypeStruct((rows, D_FF), BF16),
        compiler_params=pltpu.CompilerParams(
            dimension_semantics=("arbitrary",), vmem_limit_bytes=VMEM_LIMIT),
        name="expert_in",
    )(*tabs, xs, w_in, bias)


def _expert_out_kernel(*refs):
    tabs, total_ref = refs[:N_TABS - 1], refs[N_TABS - 1]
    act_hbm, w_hbm, b_ref, ys_hbm, wstage, wb, a_buf, y_buf, w_sem, a_sem, y_sem = refs[N_TABS:]
    te, tj, trow = tabs[:3]
    tm, tnp = EXP_TM, y_buf.shape[2]
    tn = 2 * tnp
    nj = D_MODEL // tn

    def weight_copies(e, j):
        return [pltpu.make_async_copy(
            w_hbm.at[e, :, pl.ds(pl.multiple_of(j * tn, tn), tn)], wstage, w_sem)]

    def cast_weights(par):
        wb[par] = wstage[...].astype(BF16)

    def rows(q, k):
        return pl.ds(pl.multiple_of(trow[q] + k * tm, tm), tm)

    def a_copy(q, slot, k):
        return pltpu.make_async_copy(act_hbm.at[rows(q, k)], a_buf.at[slot, pl.ds(k * tm, tm)], a_sem.at[slot])

    def y_copy(q, slot, k):
        return pltpu.make_async_copy(
            y_buf.at[slot, pl.ds(k * tm, tm)],
            ys_hbm.at[rows(q, k), pl.ds(pl.multiple_of(tj[q] * tnp, tnp), tnp)], y_sem.at[slot])

    def compute(q, slot, m, par, cast_next):
        y = jnp.dot(a_buf[slot, 0:m * tm], wb[par], preferred_element_type=F32) + b_ref[te[q] * nj + tj[q]]
        if cast_next:
            cast_weights(1 - par)
        y_buf[slot, 0:m * tm] = _pack_bf16_pairs(y)

    _grouped_pipeline(tabs, total_ref, weight_copies, cast_weights, a_copy, y_copy, compute)


def _expert_out(tabs, act, w_out, b_out):
    rows = act.shape[0]
    tm, tn = EXP_TM, EXP_OUT_TN
    nj = D_MODEL // tn
    bias = b_out.reshape(N_EXPERTS * nj, 1, tn)
    grid_spec = pltpu.PrefetchScalarGridSpec(
        num_scalar_prefetch=len(tabs),
        grid=(1,),
        in_specs=[
            pl.BlockSpec(memory_space=pl.ANY),
            pl.BlockSpec(memory_space=pl.ANY),
            pl.BlockSpec(bias.shape, lambda i, *_: (0, 0, 0), pipeline_mode=pl.Buffered(1)),
        ],
        out_specs=pl.BlockSpec(memory_space=pl.ANY),
        scratch_shapes=[pltpu.VMEM((D_FF, tn), F32), pltpu.VMEM((2, D_FF, tn), BF16),
                        pltpu.VMEM((2, EXP_CHUNK * tm, D_FF), BF16),
                        pltpu.VMEM((2, EXP_CHUNK * tm, tn // 2), PACKED),
                        pltpu.SemaphoreType.DMA, pltpu.SemaphoreType.DMA((2,)),
                        pltpu.SemaphoreType.DMA((2,))],
    )
    return pl.pallas_call(
        _expert_out_kernel,
        grid_spec=grid_spec,
        out_shape=jax.ShapeDtypeStruct((rows, D_MODEL // 2), PACKED),
        compiler_params=pltpu.CompilerParams(
            dimension_semantics=("arbitrary",), vmem_limit_bytes=VMEM_LIMIT),
        name="expert_out",
    )(*tabs, act, w_out, bias)


def _combine_kernel(x1_ref, yg_ref, rw_ref, g_ref, o_ref):
    half = EXP_OUT_TN // 2
    pieces = []
    ssq = jnp.zeros((x1_ref.shape[0], 1), F32)
    for jt in range(D_MODEL // EXP_OUT_TN):
        acc_lo = x1_ref[:, jt * EXP_OUT_TN:jt * EXP_OUT_TN + half]
        acc_hi = x1_ref[:, jt * EXP_OUT_TN + half:(jt + 1) * EXP_OUT_TN]
        moe_lo = jnp.zeros_like(acc_lo)
        moe_hi = jnp.zeros_like(acc_hi)
        for k in range(TOP_K):
            lo, hi = _unpack_bf16_pairs(yg_ref[k, :, jt * half:(jt + 1) * half])
            moe_lo = moe_lo + lo * rw_ref[:, k:k + 1]
            moe_hi = moe_hi + hi * rw_ref[:, k:k + 1]
        for acc in (acc_lo + moe_lo, acc_hi + moe_hi):
            ssq = ssq + jnp.sum(acc * acc, axis=-1, keepdims=True)
            pieces.append(acc)
    inv = lax.rsqrt(ssq / D_MODEL + EPS)
    for n, acc in enumerate(pieces):
        o_ref[:, n * half:(n + 1) * half] = acc * inv * g_ref[:, n * half:(n + 1) * half]


def _combine(x1, yg, rw, gain, row_off, name):
    t = x1.shape[0]
    tm = COMB_TM
    assert t % tm == 0 and row_off % tm == 0
    off = row_off // tm
    return pl.pallas_call(
        _combine_kernel,
        grid=(t // tm,),
        in_specs=[
            pl.BlockSpec((tm, D_MODEL), lambda i: (i, 0)),
            pl.BlockSpec((TOP_K, tm, D_MODEL // 2), lambda i: (0, i + off, 0)),
            pl.BlockSpec((tm, LANES), lambda i: (i, 0)),
            pl.BlockSpec((1, D_MODEL), lambda i: (0, 0)),
        ],
        out_specs=pl.BlockSpec((tm, D_MODEL), lambda i: (i, 0)),
        out_shape=jax.ShapeDtypeStruct((t, D_MODEL), F32),
        compiler_params=pltpu.CompilerParams(
            dimension_semantics=("arbitrary",), vmem_limit_bytes=VMEM_LIMIT),
        name=name,
    )(x1, yg, rw, gain.reshape(1, D_MODEL))


def kernel(x_prompt, x_sample, state_pool, state_conv, cache_mem_k, cache_mem_v, mem_prompt,
           norm_mix, w_in, b_gate, w_pool_group, pool_scale, w_conv, mem_norm, w_mem_kv,
           w_pool_out, w_conv_out, w_attn_out, w_o, norm_ffn, w_router, b_router,
           w_exp_in, b_exp_in, w_exp_out, b_exp_out, final_norm):
    depth = norm_mix.shape[0]
    assert depth == 1
    l = 0
    bp, seq, _ = x_prompt.shape
    bs, ln, _ = x_sample.shape
    tp, ts = bp * seq, bs * ln

    kv = _norm_matmul(mem_prompt.reshape(bp * N_MEM, D_MODEL), mem_norm[l], w_mem_kv[l].astype(BF16),
                      jnp.zeros((2 * D_XATTN,), F32), 2 * D_XATTN, "mem_kv")
    mk = kv[:, :D_XATTN].reshape(bp, N_MEM, D_XATTN)
    mv = kv[:, D_XATTN:].reshape(bp, N_MEM, D_XATTN)

    bias_in = jnp.concatenate([jnp.zeros((D_MIX,), F32), b_gate[l]])
    xp = x_prompt.reshape(tp, D_MODEL)
    xs_ = x_sample.reshape(ts, D_MODEL)
    w_in_b = w_in[l].astype(BF16)
    proj_p = _norm_matmul(xp, norm_mix[l], w_in_b, bias_in, D_MIX, "proj_prompt")
    proj_s = _norm_matmul(xs_, norm_mix[l], w_in_b, bias_in, D_MIX, "proj_sample")

    br_p, zst_p = _mix_prompt(proj_p, mk, mv, w_pool_group[l], pool_scale[l], w_conv[l], bp, seq)
    br_s, zst_s = _mix_sample(proj_s, state_pool[l], state_conv[l],
                              cache_mem_k[l].reshape(bs, N_MEM * N_XHEADS, XHEAD_DIM),
                              cache_mem_v[l].reshape(bs, N_MEM * N_XHEADS, XHEAD_DIM),
                              w_pool_group[l], pool_scale[l], w_conv[l], bs, ln)

    wpo, wco, wao, wo = (w.astype(BF16) for w in (w_pool_out[l], w_conv_out[l], w_attn_out[l], w_o[l]))
    x1_p, h2_p, ri_p, rw_p, cnt_p = _merge_route(xp, br_p, proj_p, wpo, wco, wao, wo, norm_ffn[l],
                                                  w_router[l], b_router[l], "merge_route_prompt")
    x1_s, h2_s, ri_s, rw_s, cnt_s = _merge_route(xs_, br_s, proj_s, wpo, wco, wao, wo, norm_ffn[l],
                                                  w_router[l], b_router[l], "merge_route_sample")

    t_all = tp + ts
    n_assign = t_all * TOP_K
    nb_max = n_assign // EXP_TM + N_EXPERTS
    max_items = nb_max // EXP_CHUNK + N_EXPERTS
    dest, tabs_in, tabs_out = _route_tables(
        cnt_p[0, :N_EXPERTS].astype(jnp.int32), cnt_s[0, :N_EXPERTS].astype(jnp.int32), ri_p, ri_s,
        D_FF // EXP_IN_TN, D_MODEL // EXP_OUT_TN, max_items)
    xs_sorted = _sc_dispatch(h2_p, h2_s, dest, nb_max * EXP_TM)

    act = _expert_in(tabs_in, xs_sorted, w_exp_in[l], b_exp_in[l])
    ys = _expert_out(tabs_out, act, w_exp_out[l], b_exp_out[l])
    yg = _sc_gather_rows(ys, dest[:TOP_K].reshape(-1)).reshape(TOP_K, t_all, D_MODEL // 2)

    y_p = _combine(x1_p, yg, rw_p, final_norm, 0, "combine_prompt")
    y_s = _combine(x1_s, yg, rw_s, final_norm, tp, "combine_sample")

    new_pool_p = proj_p[:, :D_POOL].reshape(bp, seq, D_POOL)[:, seq - POOL_STATE_LEN:]
    new_conv_p = zst_p[:, 8 - (CONV_WIDTH - 1):]
    u_s = proj_s[:, :D_POOL].reshape(bs, ln, D_POOL)
    new_pool_s = jnp.concatenate([state_pool[l], u_s], axis=1)[:, -POOL_STATE_LEN:]
    new_conv_s = zst_s[:, ln - (CONV_WIDTH - 1):]

    return (y_p.reshape(bp, seq, D_MODEL), y_s.reshape(bs, ln, D_MODEL),
            new_pool_p[None], new_conv_p[None],
            mk.reshape(1, bp, N_MEM, N_XHEADS, XHEAD_DIM), mv.reshape(1, bp, N_MEM, N_XHEADS, XHEAD_DIM),
            new_pool_s[None], new_conv_s[None])
```

```python
import functools

import jax
import jax.numpy as jnp
from jax import lax
from jax.experimental import pallas as pl
from jax.experimental.pallas import tpu as pltpu
from jax.experimental.pallas import tpu_sc as plsc

F32 = jnp.float32
BF16 = jnp.bfloat16
PACKED = jnp.int32

D_MODEL = 2048
POOL_WINDOWS = (2, 4, 8, 16)
POOL_GROUP_DIM = 128
D_POOL = 512
POOL_STATE_LEN = 15
D_CONV = 1024
CONV_WIDTH = 3
N_MEM = 256
N_XHEADS = 4
XHEAD_DIM = 128
D_XATTN = 512
N_BRANCH = 3
D_MIX = D_POOL + 3 * D_CONV + D_XATTN
D_IN_TOTAL = D_MIX + N_BRANCH * D_MODEL
N_EXPERTS = 32
TOP_K = 4
D_FF = D_MODEL
SWIGLU_LIMIT = 7.0
SWIGLU_ALPHA = 1.702
EPS = 1e-5

C_U = 0
C_V = D_POOL
C_B = D_POOL + D_CONV
C_C = D_POOL + 2 * D_CONV
C_Q = D_POOL + 3 * D_CONV

LANES = 128
HIST = 16

PROJ_TM = 1024
PROJ_TN = 1024
MIX_TM = 256
MIX_NS = 8
MERGE_TM = 256
ROUTE_TM = 1024
N_TABS = 9
EXP_TM = 128
EXP_CHUNK = 4
EXP_IN_TN = 1024
EXP_OUT_TN = 2048
COMB_TM = 256
SC_CORES = 2
SC_SUBCORES = 16
SC_WORKERS = SC_CORES * SC_SUBCORES
SC_CHUNK = 32
VMEM_LIMIT = 56 * 1024 * 1024


def _sigmoid(x):
    return 0.5 * (jnp.tanh(0.5 * x) + 1.0)


def _rms(x, g):
    ms = jnp.mean(x * x, axis=-1, keepdims=True)
    return x * lax.rsqrt(ms + EPS) * g


def _pack_bf16_pairs(x):
    n = x.shape[1] // 2
    lo = lax.bitcast_convert_type(x[:, :n].astype(BF16).astype(F32), jnp.uint32)
    hi = lax.bitcast_convert_type(x[:, n:].astype(BF16).astype(F32), jnp.uint32)
    return lax.bitcast_convert_type((hi & jnp.uint32(0xFFFF0000)) | (lo >> 16), PACKED)


def _unpack_bf16_pairs(p):
    p = lax.bitcast_convert_type(p, jnp.uint32)
    lo = lax.bitcast_convert_type(p << 16, F32)
    hi = lax.bitcast_convert_type(p & jnp.uint32(0xFFFF0000), F32)
    return lo, hi


def _norm_matmul_kernel(x_ref, g_ref, w_ref, b_ref, o_ref, h_ref, *, act_from):
    j = pl.program_id(1)

    @pl.when(j == 0)
    def _():
        h_ref[...] = _rms(x_ref[...], g_ref[...]).astype(BF16)

    acc = jnp.dot(h_ref[...], w_ref[...].astype(BF16), preferred_element_type=F32) + b_ref[...]

    @pl.when(j < act_from)
    def _():
        o_ref[...] = acc

    @pl.when(j >= act_from)
    def _():
        o_ref[...] = _sigmoid(acc)


def _norm_matmul(x, gain, w, bias, act_from_col, name):
    t, d = x.shape
    n = w.shape[1]
    tm = min(PROJ_TM, t)
    tn = PROJ_TN
    assert t % tm == 0 and n % tn == 0 and act_from_col % tn == 0
    return pl.pallas_call(
        functools.partial(_norm_matmul_kernel, act_from=act_from_col // tn),
        grid=(t // tm, n // tn),
        in_specs=[
            pl.BlockSpec((tm, d), lambda i, j: (i, 0)),
            pl.BlockSpec((1, d), lambda i, j: (0, 0)),
            pl.BlockSpec((d, tn), lambda i, j: (0, j)),
            pl.BlockSpec((1, tn), lambda i, j: (0, j)),
        ],
        out_specs=pl.BlockSpec((tm, tn), lambda i, j: (i, j)),
        out_shape=jax.ShapeDtypeStruct((t, n), F32),
        scratch_shapes=[pltpu.VMEM((tm, d), BF16)],
        compiler_params=pltpu.CompilerParams(
            dimension_semantics=("arbitrary", "arbitrary"), vmem_limit_bytes=VMEM_LIMIT),
        name=name,
    )(x, gain.reshape(1, d), w, bias.reshape(1, n))


def _pool_project(pooled, wpg_ref, scale_ref, g):
    sl = slice(g * POOL_GROUP_DIM, (g + 1) * POOL_GROUP_DIM)
    y = jnp.dot(pooled.astype(BF16), wpg_ref[g].astype(BF16), preferred_element_type=F32)
    return y * scale_ref[:, sl]


def _softmax_rows(s):
    m = jnp.max(s, axis=-1, keepdims=True)
    e = jnp.exp(s - m)
    return e / jnp.sum(e, axis=-1, keepdims=True)


def _mix_prompt_kernel(cur_ref, prev_ref, mk_ref, mv_ref, wpg_ref, scale_ref, wconv_ref,
                       br_ref, zst_ref, extu_ref, extz_ref):
    t = pl.program_id(1)
    tm = cur_ref.shape[0]
    has_prev = t > 0

    u = cur_ref[:, C_U:C_U + D_POOL]
    extu_ref[0:HIST, :] = jnp.where(has_prev, prev_ref[:, C_U:C_U + D_POOL], 0.0)
    extu_ref[HIST:HIST + tm, :] = u
    pos = t * tm + lax.broadcasted_iota(jnp.int32, (tm, 1), 0)
    for g, w in enumerate(POOL_WINDOWS):
        sl = slice(g * POOL_GROUP_DIM, (g + 1) * POOL_GROUP_DIM)
        s = extu_ref[HIST:HIST + tm, sl]
        for k in range(1, w):
            s = s + extu_ref[HIST - k:HIST - k + tm, sl]
        cnt = jnp.minimum(w, pos + 1).astype(F32)
        pooled = s / cnt - extu_ref[HIST:HIST + tm, sl]
        br_ref[:, sl] = _pool_project(pooled, wpg_ref, scale_ref, g).astype(BF16)

    z = cur_ref[:, C_C:C_C + D_CONV] * cur_ref[:, C_V:C_V + D_CONV]
    zprev = prev_ref[:, C_C:C_C + D_CONV] * prev_ref[:, C_V:C_V + D_CONV]
    extz_ref[0:HIST, :] = jnp.where(has_prev, zprev, 0.0)
    extz_ref[HIST:HIST + tm, :] = z
    y = extz_ref[HIST - 2:HIST - 2 + tm, :] * wconv_ref[0:1, :]
    y = y + extz_ref[HIST - 1:HIST - 1 + tm, :] * wconv_ref[1:2, :]
    y = y + extz_ref[HIST:HIST + tm, :] * wconv_ref[2:3, :]
    br_ref[:, D_POOL:D_POOL + D_CONV] = (cur_ref[:, C_B:C_B + D_CONV] * y).astype(BF16)
    zst_ref[0] = extz_ref[HIST + tm - 8:HIST + tm, :]

    for h in range(N_XHEADS):
        sl = slice(h * XHEAD_DIM, (h + 1) * XHEAD_DIM)
        qh = cur_ref[:, C_Q + h * XHEAD_DIM:C_Q + (h + 1) * XHEAD_DIM].astype(BF16)
        kh = mk_ref[0, :, sl].astype(BF16)
        vh = mv_ref[0, :, sl].astype(BF16)
        s = lax.dot_general(qh, kh, (((1,), (1,)), ((), ())), preferred_element_type=F32)
        p = _softmax_rows(s * (XHEAD_DIM ** -0.5))
        o = jnp.dot(p.astype(BF16), vh, preferred_element_type=F32)
        c0 = D_POOL + D_CONV + h * XHEAD_DIM
        br_ref[:, c0:c0 + XHEAD_DIM] = o.astype(BF16)


def _mix_prompt(proj, mk, mv, wpg, scale, wconv, batch, seq):
    tm = MIX_TM
    nt = seq // tm
    assert seq % tm == 0 and tm % HIST == 0
    rpb = tm // HIST
    return pl.pallas_call(
        _mix_prompt_kernel,
        grid=(batch, nt),
        in_specs=[
            pl.BlockSpec((tm, D_MIX), lambda b, t: (b * nt + t, 0)),
            pl.BlockSpec((HIST, D_MIX), lambda b, t: (jnp.maximum((b * nt + t) * rpb - 1, 0), 0)),
            pl.BlockSpec((1, N_MEM, D_XATTN), lambda b, t: (b, 0, 0)),
            pl.BlockSpec((1, N_MEM, D_XATTN), lambda b, t: (b, 0, 0)),
            pl.BlockSpec((len(POOL_WINDOWS), POOL_GROUP_DIM, POOL_GROUP_DIM), lambda b, t: (0, 0, 0)),
            pl.BlockSpec((1, D_POOL), lambda b, t: (0, 0)),
            pl.BlockSpec((CONV_WIDTH, D_CONV), lambda b, t: (0, 0)),
        ],
        out_specs=[
            pl.BlockSpec((tm, D_MODEL), lambda b, t: (b * nt + t, 0)),
            pl.BlockSpec((1, 8, D_CONV), lambda b, t: (b, 0, 0)),
        ],
        out_shape=[
            jax.ShapeDtypeStruct((batch * seq, D_MODEL), BF16),
            jax.ShapeDtypeStruct((batch, 8, D_CONV), F32),
        ],
        scratch_shapes=[pltpu.VMEM((HIST + tm, D_POOL), F32), pltpu.VMEM((HIST + tm, D_CONV), F32)],
        compiler_params=pltpu.CompilerParams(
            dimension_semantics=("arbitrary", "arbitrary"), vmem_limit_bytes=VMEM_LIMIT),
        name="mix_prompt",
    )(proj, proj, mk, mv, wpg, scale.reshape(1, D_POOL), wconv)


def _mix_sample_kernel(cur_ref, sp_ref, sc_ref, k_ref, v_ref, wpg_ref, scale_ref, wconv_ref,
                       br_ref, zst_ref, extu_ref, extz_ref):
    ns, ln = zst_ref.shape[0], zst_ref.shape[1]
    rows = ns * ln

    extu_ref[:, HIST - POOL_STATE_LEN:HIST, :] = sp_ref[...]
    extu_ref[:, HIST:HIST + ln, :] = cur_ref[:, C_U:C_U + D_POOL].reshape(ns, ln, D_POOL)
    for g, w in enumerate(POOL_WINDOWS):
        sl = slice(g * POOL_GROUP_DIM, (g + 1) * POOL_GROUP_DIM)
        s = extu_ref[:, HIST:HIST + ln, sl]
        for k in range(1, w):
            s = s + extu_ref[:, HIST - k:HIST - k + ln, sl]
        pooled = s / float(w) - extu_ref[:, HIST:HIST + ln, sl]
        pooled = pooled.reshape(rows, POOL_GROUP_DIM)
        br_ref[:, sl] = _pool_project(pooled, wpg_ref, scale_ref, g).astype(BF16)

    z = cur_ref[:, C_C:C_C + D_CONV] * cur_ref[:, C_V:C_V + D_CONV]
    extz_ref[:, HIST - 2:HIST, :] = sc_ref[...]
    extz_ref[:, HIST:HIST + ln, :] = z.reshape(ns, ln, D_CONV)
    y = extz_ref[:, HIST - 2:HIST - 2 + ln, :] * wconv_ref[0:1, :]
    y = y + extz_ref[:, HIST - 1:HIST - 1 + ln, :] * wconv_ref[1:2, :]
    y = y + extz_ref[:, HIST:HIST + ln, :] * wconv_ref[2:3, :]
    br_ref[:, D_POOL:D_POOL + D_CONV] = (
        cur_ref[:, C_B:C_B + D_CONV] * y.reshape(rows, D_CONV)).astype(BF16)
    zst_ref[...] = extz_ref[:, HIST:HIST + ln, :]

    q3 = cur_ref[:, C_Q:C_Q + D_XATTN].reshape(ns, ln, D_XATTN)
    q4 = jnp.concatenate([q3[:, :, h * XHEAD_DIM:(h + 1) * XHEAD_DIM] for h in range(N_XHEADS)], axis=1)
    s = jnp.einsum("nqd,nkd->nqk", q4.astype(BF16), k_ref[...].astype(BF16),
                   preferred_element_type=F32)
    row_head = lax.broadcasted_iota(jnp.int32, (N_XHEADS * ln, N_XHEADS * N_MEM), 0) // ln
    col_head = lax.broadcasted_iota(jnp.int32, (N_XHEADS * ln, N_XHEADS * N_MEM), 1) % N_XHEADS
    s = jnp.where((row_head == col_head)[None], s * (XHEAD_DIM ** -0.5), -jnp.inf)
    p = _softmax_rows(s)
    r = jnp.einsum("nqk,nkd->nqd", p.astype(BF16), v_ref[...].astype(BF16),
                   preferred_element_type=F32)
    for h in range(N_XHEADS):
        c0 = D_POOL + D_CONV + h * XHEAD_DIM
        br_ref[:, c0:c0 + XHEAD_DIM] = r[:, h * ln:(h + 1) * ln, :].reshape(rows, XHEAD_DIM).astype(BF16)


def _mix_sample(proj, state_pool, state_conv, mem_k, mem_v, wpg, scale, wconv, nseq, ln):
    ns = MIX_NS
    rows = ns * ln
    assert nseq % ns == 0 and ln == 8
    return pl.pallas_call(
        _mix_sample_kernel,
        grid=(nseq // ns,),
        in_specs=[
            pl.BlockSpec((rows, D_MIX), lambda s: (s, 0)),
            pl.BlockSpec((ns, POOL_STATE_LEN, D_POOL), lambda s: (s, 0, 0)),
            pl.BlockSpec((ns, CONV_WIDTH - 1, D_CONV), lambda s: (s, 0, 0)),
            pl.BlockSpec((ns, N_MEM * N_XHEADS, XHEAD_DIM), lambda s: (s, 0, 0)),
            pl.BlockSpec((ns, N_MEM * N_XHEADS, XHEAD_DIM), lambda s: (s, 0, 0)),
            pl.BlockSpec((len(POOL_WINDOWS), POOL_GROUP_DIM, POOL_GROUP_DIM), lambda s: (0, 0, 0)),
            pl.BlockSpec((1, D_POOL), lambda s: (0, 0)),
            pl.BlockSpec((CONV_WIDTH, D_CONV), lambda s: (0, 0)),
        ],
        out_specs=[
            pl.BlockSpec((rows, D_MODEL), lambda s: (s, 0)),
            pl.BlockSpec((ns, ln, D_CONV), lambda s: (s, 0, 0)),
        ],
        out_shape=[
            jax.ShapeDtypeStruct((nseq * ln, D_MODEL), BF16),
            jax.ShapeDtypeStruct((nseq, ln, D_CONV), F32),
        ],
        scratch_shapes=[pltpu.VMEM((ns, HIST + ln, D_POOL), F32),
                        pltpu.VMEM((ns, HIST + ln, D_CONV), F32)],
        compiler_params=pltpu.CompilerParams(
            dimension_semantics=("arbitrary",), vmem_limit_bytes=VMEM_LIMIT),
        name="mix_sample",
    )(proj, state_pool, state_conv, mem_k, mem_v, wpg, scale.reshape(1, D_POOL), wconv)


def _merge_route_kernel(x_ref, br_ref, g0_ref, g1_ref, g2_ref, wpo_ref, wco_ref, wao_ref, wo_ref,
                        nf_ref, wrh_ref, wrl_ref, brt_ref,
                        x1_ref, h2_ref, ri_ref, rw_ref, cnt_ref, carry_ref):
    i = pl.program_id(0)
    tm = x_ref.shape[0]

    @pl.when(i == 0)
    def _():
        carry_ref[...] = jnp.zeros_like(carry_ref)

    merged = g0_ref[...] * jnp.dot(br_ref[:, 0:D_POOL], wpo_ref[...], preferred_element_type=F32)
    merged = merged + g1_ref[...] * jnp.dot(br_ref[:, D_POOL:D_POOL + D_CONV], wco_ref[...],
                                            preferred_element_type=F32)
    merged = merged + g2_ref[...] * jnp.dot(br_ref[:, D_POOL + D_CONV:D_MODEL], wao_ref[...],
                                            preferred_element_type=F32)
    x1 = x_ref[...] + jnp.dot(merged.astype(BF16), wo_ref[...], preferred_element_type=F32)
    x1_ref[...] = x1
    h2 = _rms(x1, nf_ref[...])
    h2_ref[...] = _pack_bf16_pairs(h2)

    h2_hi = h2.astype(BF16)
    h2_lo = (h2 - h2_hi.astype(F32)).astype(BF16)
    logits = (jnp.dot(h2_hi, wrh_ref[...], preferred_element_type=F32)
              + jnp.dot(h2_hi, wrl_ref[...], preferred_element_type=F32)
              + jnp.dot(h2_lo, wrh_ref[...], preferred_element_type=F32)) + brt_ref[...]
    lane = lax.broadcasted_iota(jnp.int32, (tm, N_EXPERTS), 1).astype(F32)
    vals, idxs, hots = [], [], []
    work = logits
    for _ in range(TOP_K):
        m = jnp.max(work, axis=-1, keepdims=True)
        idx = jnp.min(jnp.where(work == m, lane, float(N_EXPERTS)), axis=-1, keepdims=True)
        hot = lane == idx
        work = jnp.where(hot, -jnp.inf, work)
        vals.append(m)
        idxs.append(idx)
        hots.append(hot)
    es = [jnp.exp(v - vals[0]) for v in vals]
    denom = es[0] + es[1] + es[2] + es[3]

    chosen = jnp.where(hots[0] | hots[1] | hots[2] | hots[3], 1.0, 0.0).astype(BF16)
    r_i = lax.broadcasted_iota(jnp.int32, (tm, tm), 0)
    c_i = lax.broadcasted_iota(jnp.int32, (tm, tm), 1)
    lower = jnp.where(c_i < r_i, 1.0, 0.0).astype(BF16)
    before = jnp.dot(lower, chosen, preferred_element_type=F32) + carry_ref[0:1, 0:N_EXPERTS]
    carry_ref[0:1, 0:N_EXPERTS] = (carry_ref[0:1, 0:N_EXPERTS]
                                   + jnp.sum(chosen.astype(F32), axis=0, keepdims=True))
    cnt_ref[...] = carry_ref[...]

    out_lane = lax.broadcasted_iota(jnp.int32, (tm, LANES), 1)
    ri = jnp.zeros((tm, LANES), jnp.int32)
    rw = jnp.zeros((tm, LANES), F32)
    for k in range(TOP_K):
        rank = jnp.sum(jnp.where(hots[k], before, 0.0), axis=-1, keepdims=True).astype(jnp.int32)
        ri = jnp.where(out_lane == k, idxs[k].astype(jnp.int32), ri)
        ri = jnp.where(out_lane == TOP_K + k, rank, ri)
        rw = jnp.where(out_lane == k, es[k] / denom, rw)
    ri_ref[...] = ri
    rw_ref[...] = rw


def _merge_route(x, branch, proj, wpo, wco, wao, wo, norm_ffn, w_router, b_router, name):
    t = x.shape[0]
    tm = MERGE_TM
    assert t % tm == 0
    gate_blk0 = D_MIX // D_MODEL
    assert D_MIX % D_MODEL == 0
    const = lambda i: (0, 0)
    wr_hi = w_router.astype(BF16)
    wr_lo = (w_router - wr_hi.astype(F32)).astype(BF16)
    return pl.pallas_call(
        _merge_route_kernel,
        grid=(t // tm,),
        in_specs=[
            pl.BlockSpec((tm, D_MODEL), lambda i: (i, 0)),
            pl.BlockSpec((tm, D_MODEL), lambda i: (i, 0)),
            pl.BlockSpec((tm, D_MODEL), lambda i: (i, gate_blk0)),
            pl.BlockSpec((tm, D_MODEL), lambda i: (i, gate_blk0 + 1)),
            pl.BlockSpec((tm, D_MODEL), lambda i: (i, gate_blk0 + 2)),
            pl.BlockSpec((D_POOL, D_MODEL), const, pipeline_mode=pl.Buffered(1)),
            pl.BlockSpec((D_CONV, D_MODEL), const, pipeline_mode=pl.Buffered(1)),
            pl.BlockSpec((D_XATTN, D_MODEL), const, pipeline_mode=pl.Buffered(1)),
            pl.BlockSpec((D_MODEL, D_MODEL), const, pipeline_mode=pl.Buffered(1)),
            pl.BlockSpec((1, D_MODEL), const),
            pl.BlockSpec((D_MODEL, N_EXPERTS), const),
            pl.BlockSpec((D_MODEL, N_EXPERTS), const),
            pl.BlockSpec((1, N_EXPERTS), const),
        ],
        out_specs=[
            pl.BlockSpec((tm, D_MODEL), lambda i: (i, 0)),
            pl.BlockSpec((tm, D_MODEL // 2), lambda i: (i, 0)),
            pl.BlockSpec((tm, LANES), lambda i: (i, 0)),
            pl.BlockSpec((tm, LANES), lambda i: (i, 0)),
            pl.BlockSpec((8, LANES), const),
        ],
        out_shape=[
            jax.ShapeDtypeStruct((t, D_MODEL), F32),
            jax.ShapeDtypeStruct((t, D_MODEL // 2), PACKED),
            jax.ShapeDtypeStruct((t, LANES), jnp.int32),
            jax.ShapeDtypeStruct((t, LANES), F32),
            jax.ShapeDtypeStruct((8, LANES), F32),
        ],
        scratch_shapes=[pltpu.VMEM((8, LANES), F32)],
        compiler_params=pltpu.CompilerParams(
            dimension_semantics=("arbitrary",), vmem_limit_bytes=VMEM_LIMIT),
        name=name,
    )(x, branch, proj, proj, proj, wpo, wco, wao, wo, norm_ffn.reshape(1, D_MODEL),
      wr_hi, wr_lo, b_router.reshape(1, N_EXPERTS))


def _fill_work_list(nblk_s, pad_s, nj, refs):
    te, tj, trow, tnb, tfirst, tne, tnj, tpar, ttot = refs

    def clear(i, c):
        for r in (te, tj, trow, tnb, tfirst, tpar):
            r[i] = 0
        tne[i] = -1
        tnj[i] = -1
        return c

    lax.fori_loop(0, te.shape[0], clear, 0)

    carry = (jnp.int32(0), jnp.int32(-1), jnp.int32(0))
    for j in range(nj):
        def per_expert(e, carry):
            q, prev, groups = carry
            n = nblk_s[e]
            nch = (n + EXP_CHUNK - 1) // EXP_CHUNK

            @pl.when((nch > 0) & (prev >= 0))
            def _():
                tne[prev] = e
                tnj[prev] = jnp.int32(j)

            def per_chunk(c, q):
                te[q] = e
                tj[q] = jnp.int32(j)
                trow[q] = pad_s[e] + c * (EXP_CHUNK * EXP_TM)
                tnb[q] = jnp.minimum(EXP_CHUNK, n - c * EXP_CHUNK)
                tfirst[q] = (c == 0).astype(jnp.int32)
                tpar[q] = lax.rem(groups, 2)
                return q + 1

            return (lax.fori_loop(0, nch, per_chunk, q), jnp.where(nch > 0, q, prev),
                    groups + (nch > 0).astype(jnp.int32))

        carry = lax.fori_loop(0, N_EXPERTS, per_expert, carry)
    ttot[0] = carry[0]


def _route_tables_kernel(cntp_ref, cnts_ref, rip_ref, ris_ref, dest_ref, *refs, nj_in, nj_out):
    tabs_in, tabs_out = refs[0:N_TABS], refs[N_TABS:2 * N_TABS]
    nblk_s, pad_s = refs[2 * N_TABS:]

    def per_expert(e, start_blk):
        n = (cntp_ref[e] + cnts_ref[e] + EXP_TM - 1) // EXP_TM
        nblk_s[e] = n
        pad_s[e] = start_blk * EXP_TM
        return start_blk + n

    lax.fori_loop(0, N_EXPERTS, per_expert, jnp.int32(0))
    _fill_work_list(nblk_s, pad_s, nj_in, tabs_in)
    _fill_work_list(nblk_s, pad_s, nj_out, tabs_out)

    tile = ROUTE_TM
    col0 = 0
    for ri_ref, is_sample in ((rip_ref, False), (ris_ref, True)):
        for r in range(ri_ref.shape[0] // tile):
            ri = ri_ref[r * tile:(r + 1) * tile, :]
            base = jnp.zeros_like(ri)
            for e in range(N_EXPERTS):
                first_row = pad_s[e] + cntp_ref[e] if is_sample else pad_s[e]
                base = jnp.where(ri == e, first_row, base)
            dest = base + pltpu.roll(ri, LANES - TOP_K, axis=1)
            dest_ref[:, col0:col0 + tile] = jnp.transpose(dest)[0:8, :]
            col0 += tile


def _route_tables(cnt_p, cnt_s, ri_p, ri_s, nj_in, nj_out, max_items):
    t_all = ri_p.shape[0] + ri_s.shape[0]
    assert ri_p.shape[0] % ROUTE_TM == 0 and ri_s.shape[0] % ROUTE_TM == 0
    smem = pl.BlockSpec(memory_space=pltpu.SMEM)

    def tables(nj):
        n = nj * max_items
        return [jax.ShapeDtypeStruct((n,), jnp.int32)] * (N_TABS - 1) + [jax.ShapeDtypeStruct((1,), jnp.int32)]

    out = pl.pallas_call(
        functools.partial(_route_tables_kernel, nj_in=nj_in, nj_out=nj_out),
        grid=(1,),
        in_specs=[smem, smem,
                  pl.BlockSpec(ri_p.shape, lambda i: (0, 0)), pl.BlockSpec(ri_s.shape, lambda i: (0, 0))],
        out_specs=[pl.BlockSpec((8, t_all), lambda i: (0, 0))] + [smem] * (2 * N_TABS),
        out_shape=[jax.ShapeDtypeStruct((8, t_all), jnp.int32)] + tables(nj_in) + tables(nj_out),
        scratch_shapes=[pltpu.SMEM((N_EXPERTS,), jnp.int32), pltpu.SMEM((N_EXPERTS,), jnp.int32)],
        compiler_params=pltpu.CompilerParams(
            dimension_semantics=("arbitrary",), vmem_limit_bytes=VMEM_LIMIT),
        name="route_tables",
    )(cnt_p, cnt_s, ri_p, ri_s)
    return out[0], tuple(out[1:1 + N_TABS]), tuple(out[1 + N_TABS:1 + 2 * N_TABS])


def _sc_mesh():
    return plsc.VectorSubcoreMesh(core_axis_name="c", subcore_axis_name="s",
                                  num_cores=SC_CORES, num_subcores=SC_SUBCORES)


def _sc_worker_id():
    return lax.axis_index("s") * SC_CORES + lax.axis_index("c")


def _sc_dispatch(h2_a, h2_b, dest_by_slot, rows_out):
    ta, w = h2_a.shape
    t = ta + h2_b.shape[0]
    per_w = t // SC_WORKERS
    n_chunks = per_w // SC_CHUNK
    assert t == SC_WORKERS * n_chunks * SC_CHUNK and ta % SC_CHUNK == 0
    idx = dest_by_slot[:TOP_K].reshape(TOP_K, SC_WORKERS, n_chunks, SC_CHUNK).transpose(1, 0, 2, 3)

    def body(a_hbm, b_hbm, idx_hbm, xs_hbm, idx_v, rows_v, sems):
        base = _sc_worker_id() * per_w
        pltpu.sync_copy(idx_hbm.at[_sc_worker_id()], idx_v)
        pending = [[], []]
        for c in range(n_chunks):
            b = c % 2
            for d in pending[b]:
                d.wait()
            tok0 = base + c * SC_CHUNK

            @pl.when(tok0 < ta)
            def _():
                pltpu.sync_copy(a_hbm.at[pl.ds(tok0, SC_CHUNK)], rows_v.at[b])

            @pl.when(tok0 >= ta)
            def _():
                pltpu.sync_copy(b_hbm.at[pl.ds(tok0 - ta, SC_CHUNK)], rows_v.at[b])

            pending[b] = [pltpu.async_copy(rows_v.at[b], xs_hbm.at[idx_v.at[k, c]], sems.at[b])
                          for k in range(TOP_K)]
        for b in range(2):
            for d in pending[b]:
                d.wait()

    return pl.kernel(
        body,
        out_type=jax.ShapeDtypeStruct((rows_out, w), h2_a.dtype),
        mesh=_sc_mesh(),
        scratch_types=[pltpu.VMEM((TOP_K, n_chunks, SC_CHUNK), jnp.int32),
                       pltpu.VMEM((2, SC_CHUNK, w), h2_a.dtype),
                       pltpu.SemaphoreType.DMA((2,))],
        name="sc_dispatch",
    )(h2_a, h2_b, idx)


def _sc_gather_rows(table, idx):
    n = idx.shape[0]
    w = table.shape[1]
    per_w = n // SC_WORKERS
    n_chunks = per_w // SC_CHUNK
    assert n == SC_WORKERS * n_chunks * SC_CHUNK
    idx3 = idx.reshape(SC_WORKERS, n_chunks, SC_CHUNK)

    def body(table_hbm, idx_hbm, out_hbm, idx_v, rows_v, gsem, wsems):
        base = _sc_worker_id() * per_w
        pltpu.sync_copy(idx_hbm.at[_sc_worker_id()], idx_v)
        pending = [None, None]
        for c in range(n_chunks):
            b = c % 2
            if pending[b] is not None:
                pending[b].wait()
            pltpu.async_copy(table_hbm.at[idx_v.at[c]], rows_v.at[b], gsem).wait()
            pending[b] = pltpu.async_copy(
                rows_v.at[b], out_hbm.at[pl.ds(base + c * SC_CHUNK, SC_CHUNK)], wsems.at[b])
        for b in range(2):
            if pending[b] is not None:
                pending[b].wait()

    return pl.kernel(
        body,
        out_type=jax.ShapeDtypeStruct((n, w), table.dtype),
        mesh=_sc_mesh(),
        scratch_types=[pltpu.VMEM((n_chunks, SC_CHUNK), jnp.int32),
                       pltpu.VMEM((2, SC_CHUNK, w), table.dtype),
                       pltpu.SemaphoreType.DMA,
                       pltpu.SemaphoreType.DMA((2,))],
        name="sc_gather_rows",
    )(table, idx3)


def _grouped_pipeline(tabs, total_ref, weight_copies, in_copy, out_copy, compute):
    te, tj, _, tnb, tfirst, tne, tnj, tpar = tabs
    total = total_ref[0]

    def each_block(q, copy, op):
        slot = lax.rem(q, 2)
        for k in range(EXP_CHUNK):
            @pl.when(k < tnb[q])
            def _():
                op(copy(q, slot, k))

    def start(d):
        d.start()

    def wait(d):
        d.wait()

    for d in weight_copies(te[0], tj[0], tpar[0]):
        d.start()
    each_block(0, in_copy, start)

    def body(q, carry):
        each_block(q, in_copy, wait)

        @pl.when(q + 1 < total)
        def _():
            each_block(q + 1, in_copy, start)

        @pl.when(tfirst[q] == 1)
        def _():
            for d in weight_copies(te[q], tj[q], tpar[q]):
                d.wait()

            @pl.when(tne[q] >= 0)
            def _():
                for d in weight_copies(tne[q], tnj[q], 1 - tpar[q]):
                    d.start()

        @pl.when(q >= 2)
        def _():
            each_block(q - 2, out_copy, wait)

        for m in range(1, EXP_CHUNK + 1):
            @pl.when(tnb[q] == m)
            def _():
                compute(q, lax.rem(q, 2), m, tpar[q])

        each_block(q, out_copy, start)
        return carry

    lax.fori_loop(0, total, body, 0)

    @pl.when(total >= 2)
    def _():
        each_block(total - 2, out_copy, wait)

    each_block(total - 1, out_copy, wait)


def _expert_in_kernel(*refs):
    tabs, total_ref = refs[:N_TABS - 1], refs[N_TABS - 1]
    xs_hbm, w_hbm, b_ref, act_hbm, wbuf, x_buf, a_buf, w_sem, x_sem, a_sem = refs[N_TABS:]
    te, tj, trow = tabs[:3]
    tm, tn = EXP_TM, a_buf.shape[2]
    nj = D_FF // tn

    def weight_copies(e, j, par):
        return [pltpu.make_async_copy(
            w_hbm.at[e, :, pl.ds(pl.multiple_of(half * D_FF + j * tn, tn), tn)],
            wbuf.at[2 * par + half], w_sem)
            for half in range(2)]

    def rows(q, k):
        return pl.ds(pl.multiple_of(trow[q] + k * tm, tm), tm)

    def x_copy(q, slot, k):
        return pltpu.make_async_copy(xs_hbm.at[rows(q, k)], x_buf.at[slot, pl.ds(k * tm, tm)], x_sem.at[slot])

    def a_copy(q, slot, k):
        return pltpu.make_async_copy(
            a_buf.at[slot, pl.ds(k * tm, tm)],
            act_hbm.at[rows(q, k), pl.ds(pl.multiple_of(tj[q] * tn, tn), tn)], a_sem.at[slot])

    def compute(q, slot, m, par):
        bias0 = te[q] * (2 * nj) + tj[q]
        x = jnp.concatenate(_unpack_bf16_pairs(x_buf[slot, 0:m * tm]), axis=1).astype(BF16)
        g = jnp.dot(x, wbuf[2 * par].astype(BF16), preferred_element_type=F32) + b_ref[bias0]
        up = jnp.dot(x, wbuf[2 * par + 1].astype(BF16), preferred_element_type=F32) + b_ref[bias0 + nj]
        g = jnp.minimum(g, SWIGLU_LIMIT)
        up = jnp.clip(up, -SWIGLU_LIMIT, SWIGLU_LIMIT)
        a_buf[slot, 0:m * tm] = (g * _sigmoid(SWIGLU_ALPHA * g) * (up + 1.0)).astype(BF16)

    _grouped_pipeline(tabs, total_ref, weight_copies, x_copy, a_copy, compute)


def _expert_in(tabs, xs, w_in, b_in):
    rows = xs.shape[0]
    tm, tn = EXP_TM, EXP_IN_TN
    nj = D_FF // tn
    bias = b_in.reshape(N_EXPERTS * 2 * nj, 1, tn)
    grid_spec = pltpu.PrefetchScalarGridSpec(
        num_scalar_prefetch=len(tabs),
        grid=(1,),
        in_specs=[
            pl.BlockSpec(memory_space=pl.ANY),
            pl.BlockSpec(memory_space=pl.ANY),
            pl.BlockSpec(bias.shape, lambda i, *_: (0, 0, 0), pipeline_mode=pl.Buffered(1)),
        ],
        out_specs=pl.BlockSpec(memory_space=pl.ANY),
        scratch_shapes=[pltpu.VMEM((4, D_MODEL, tn), F32),
                        pltpu.VMEM((2, EXP_CHUNK * tm, D_MODEL // 2), PACKED),
                        pltpu.VMEM((2, EXP_CHUNK * tm, tn), BF16),
                        pltpu.SemaphoreType.DMA, pltpu.SemaphoreType.DMA((2,)),
                        pltpu.SemaphoreType.DMA((2,))],
    )
    return pl.pallas_call(
        _expert_in_kernel,
        grid_spec=grid_spec,
        out_shape=jax.ShapeDtypeStruct((rows, D_FF), BF16),
        compiler_params=pltpu.CompilerParams(
            dimension_semantics=("arbitrary",), vmem_limit_bytes=VMEM_LIMIT),
        name="expert_in",
    )(*tabs, xs, w_in, bias)


def _expert_out_kernel(*refs):
    tabs, total_ref = refs[:N_TABS - 1], refs[N_TABS - 1]
    act_hbm, w_hbm, b_ref, ys_hbm, wbuf, a_buf, y_buf, w_sem, a_sem, y_sem = refs[N_TABS:]
    te, tj, trow = tabs[:3]
    tm, tnp = EXP_TM, y_buf.shape[2]
    tn = 2 * tnp
    nj = D_MODEL // tn

    def weight_copies(e, j, par):
        return [pltpu.make_async_copy(
            w_hbm.at[e, :, pl.ds(pl.multiple_of(j * tn, tn), tn)], wbuf.at[par], w_sem)]

    def rows(q, k):
        return pl.ds(pl.multiple_of(trow[q] + k * tm, tm), tm)

    def a_copy(q, slot, k):
        return pltpu.make_async_copy(act_hbm.at[rows(q, k)], a_buf.at[slot, pl.ds(k * tm, tm)], a_sem.at[slot])

    def y_copy(q, slot, k):
        return pltpu.make_async_copy(
            y_buf.at[slot, pl.ds(k * tm, tm)],
            ys_hbm.at[rows(q, k), pl.ds(pl.multiple_of(tj[q] * tnp, tnp), tnp)], y_sem.at[slot])

    def compute(q, slot, m, par):
        y = jnp.dot(a_buf[slot, 0:m * tm], wbuf[par].astype(BF16),
                    preferred_element_type=F32) + b_ref[te[q] * nj + tj[q]]
        y_buf[slot, 0:m * tm] = _pack_bf16_pairs(y)

    _grouped_pipeline(tabs, total_ref, weight_copies, a_copy, y_copy, compute)


def _expert_out(tabs, act, w_out, b_out):
    rows = act.shape[0]
    tm, tn = EXP_TM, EXP_OUT_TN
    nj = D_MODEL // tn
    bias = b_out.reshape(N_EXPERTS * nj, 1, tn)
    grid_spec = pltpu.PrefetchScalarGridSpec(
        num_scalar_prefetch=len(tabs),
        grid=(1,),
        in_specs=[
            pl.BlockSpec(memory_space=pl.ANY),
            pl.BlockSpec(memory_space=pl.ANY),
            pl.BlockSpec(bias.shape, lambda i, *_: (0, 0, 0), pipeline_mode=pl.Buffered(1)),
        ],
        out_specs=pl.BlockSpec(memory_space=pl.ANY),
        scratch_shapes=[pltpu.VMEM((2, D_FF, tn), F32),
                        pltpu.VMEM((2, EXP_CHUNK * tm, D_FF), BF16),
                        pltpu.VMEM((2, EXP_CHUNK * tm, tn // 2), PACKED),
                        pltpu.SemaphoreType.DMA, pltpu.SemaphoreType.DMA((2,)),
                        pltpu.SemaphoreType.DMA((2,))],
    )
    return pl.pallas_call(
        _expert_out_kernel,
        grid_spec=grid_spec,
        out_shape=jax.ShapeDtypeStruct((rows, D_MODEL // 2), PACKED),
        compiler_params=pltpu.CompilerParams(
            dimension_semantics=("arbitrary",), vmem_limit_bytes=VMEM_LIMIT),
        name="expert_out",
    )(*tabs, act, w_out, bias)


def _combine_kernel(x1_ref, yg_ref, rw_ref, g_ref, o_ref):
    half = EXP_OUT_TN // 2
    pieces = []
    ssq = jnp.zeros((x1_ref.shape[0], 1), F32)
    for jt in range(D_MODEL // EXP_OUT_TN):
        acc_lo = x1_ref[:, jt * EXP_OUT_TN:jt * EXP_OUT_TN + half]
        acc_hi = x1_ref[:, jt * EXP_OUT_TN + half:(jt + 1) * EXP_OUT_TN]
        moe_lo = jnp.zeros_like(acc_lo)
        moe_hi = jnp.zeros_like(acc_hi)
        for k in range(TOP_K):
            lo, hi = _unpack_bf16_pairs(yg_ref[k, :, jt * half:(jt + 1) * half])
            moe_lo = moe_lo + lo * rw_ref[:, k:k + 1]
            moe_hi = moe_hi + hi * rw_ref[:, k:k + 1]
        for acc in (acc_lo + moe_lo, acc_hi + moe_hi):
            ssq = ssq + jnp.sum(acc * acc, axis=-1, keepdims=True)
            pieces.append(acc)
    inv = lax.rsqrt(ssq / D_MODEL + EPS)
    for n, acc in enumerate(pieces):
        o_ref[:, n * half:(n + 1) * half] = acc * inv * g_ref[:, n * half:(n + 1) * half]


def _combine(x1, yg, rw, gain, row_off, name):
    t = x1.shape[0]
    tm = COMB_TM
    assert t % tm == 0 and row_off % tm == 0
    off = row_off // tm
    return pl.pallas_call(
        _combine_kernel,
        grid=(t // tm,),
        in_specs=[
            pl.BlockSpec((tm, D_MODEL), lambda i: (i, 0)),
            pl.BlockSpec((TOP_K, tm, D_MODEL // 2), lambda i: (0, i + off, 0)),
            pl.BlockSpec((tm, LANES), lambda i: (i, 0)),
            pl.BlockSpec((1, D_MODEL), lambda i: (0, 0)),
        ],
        out_specs=pl.BlockSpec((tm, D_MODEL), lambda i: (i, 0)),
        out_shape=jax.ShapeDtypeStruct((t, D_MODEL), F32),
        compiler_params=pltpu.CompilerParams(
            dimension_semantics=("arbitrary",), vmem_limit_bytes=VMEM_LIMIT),
        name=name,
    )(x1, yg, rw, gain.reshape(1, D_MODEL))


def kernel(x_prompt, x_sample, state_pool, state_conv, cache_mem_k, cache_mem_v, mem_prompt,
           norm_mix, w_in, b_gate, w_pool_group, pool_scale, w_conv, mem_norm, w_mem_kv,
           w_pool_out, w_conv_out, w_attn_out, w_o, norm_ffn, w_router, b_router,
           w_exp_in, b_exp_in, w_exp_out, b_exp_out, final_norm):
    depth = norm_mix.shape[0]
    assert depth == 1
    l = 0
    bp, seq, _ = x_prompt.shape
    bs, ln, _ = x_sample.shape
    tp, ts = bp * seq, bs * ln

    kv = _norm_matmul(mem_prompt.reshape(bp * N_MEM, D_MODEL), mem_norm[l], w_mem_kv[l],
                      jnp.zeros((2 * D_XATTN,), F32), 2 * D_XATTN, "mem_kv")
    mk = kv[:, :D_XATTN].reshape(bp, N_MEM, D_XATTN)
    mv = kv[:, D_XATTN:].reshape(bp, N_MEM, D_XATTN)

    bias_in = jnp.concatenate([jnp.zeros((D_MIX,), F32), b_gate[l]])
    xp = x_prompt.reshape(tp, D_MODEL)
    xs_ = x_sample.reshape(ts, D_MODEL)
    proj_p = _norm_matmul(xp, norm_mix[l], w_in[l], bias_in, D_MIX, "proj_prompt")
    proj_s = _norm_matmul(xs_, norm_mix[l], w_in[l], bias_in, D_MIX, "proj_sample")

    br_p, zst_p = _mix_prompt(proj_p, mk, mv, w_pool_group[l], pool_scale[l], w_conv[l], bp, seq)
    br_s, zst_s = _mix_sample(proj_s, state_pool[l], state_conv[l],
                              cache_mem_k[l].reshape(bs, N_MEM * N_XHEADS, XHEAD_DIM),
                              cache_mem_v[l].reshape(bs, N_MEM * N_XHEADS, XHEAD_DIM),
                              w_pool_group[l], pool_scale[l], w_conv[l], bs, ln)

    wpo, wco, wao, wo = (w.astype(BF16) for w in (w_pool_out[l], w_conv_out[l], w_attn_out[l], w_o[l]))
    x1_p, h2_p, ri_p, rw_p, cnt_p = _merge_route(xp, br_p, proj_p, wpo, wco, wao, wo, norm_ffn[l],
                                                  w_router[l], b_router[l], "merge_route_prompt")
    x1_s, h2_s, ri_s, rw_s, cnt_s = _merge_route(xs_, br_s, proj_s, wpo, wco, wao, wo, norm_ffn[l],
                                                  w_router[l], b_router[l], "merge_route_sample")

    t_all = tp + ts
    n_assign = t_all * TOP_K
    nb_max = n_assign // EXP_TM + N_EXPERTS
    max_items = nb_max // EXP_CHUNK + N_EXPERTS
    dest, tabs_in, tabs_out = _route_tables(
        cnt_p[0, :N_EXPERTS].astype(jnp.int32), cnt_s[0, :N_EXPERTS].astype(jnp.int32), ri_p, ri_s,
        D_FF // EXP_IN_TN, D_MODEL // EXP_OUT_TN, max_items)
    xs_sorted = _sc_dispatch(h2_p, h2_s, dest, nb_max * EXP_TM)

    act = _expert_in(tabs_in, xs_sorted, w_exp_in[l], b_exp_in[l])
    ys = _expert_out(tabs_out, act, w_exp_out[l], b_exp_out[l])
    yg = _sc_gather_rows(ys, dest[:TOP_K].reshape(-1)).reshape(TOP_K, t_all, D_MODEL // 2)

    y_p = _combine(x1_p, yg, rw_p, final_norm, 0, "combine_prompt")
    y_s = _combine(x1_s, yg, rw_s, final_norm, tp, "combine_sample")

    new_pool_p = proj_p[:, :D_POOL].reshape(bp, seq, D_POOL)[:, seq - POOL_STATE_LEN:]
    new_conv_p = zst_p[:, 8 - (CONV_WIDTH - 1):]
    u_s = proj_s[:, :D_POOL].reshape(bs, ln, D_POOL)
    new_pool_s = jnp.concatenate([state_pool[l], u_s], axis=1)[:, -POOL_STATE_LEN:]
    new_conv_s = zst_s[:, ln - (CONV_WIDTH - 1):]

    return (y_p.reshape(bp, seq, D_MODEL), y_s.reshape(bs, ln, D_MODEL),
            new_pool_p[None], new_conv_p[None],
            mk.reshape(1, bp, N_MEM, N_XHEADS, XHEAD_DIM), mv.reshape(1, bp, N_MEM, N_XHEADS, XHEAD_DIM),
            new_pool_s[None], new_conv_s[None])
```

```python
import functools

import jax
import jax.numpy as jnp
from jax import lax
from jax.experimental import pallas as pl
from jax.experimental.pallas import tpu as pltpu
from jax.experimental.pallas import tpu_sc as plsc

F32 = jnp.float32
BF16 = jnp.bfloat16
PACKED = jnp.int32

D_MODEL = 2048
POOL_WINDOWS = (2, 4, 8, 16)
POOL_GROUP_DIM = 128
D_POOL = 512
POOL_STATE_LEN = 15
D_CONV = 1024
CONV_WIDTH = 3
N_MEM = 256
N_XHEADS = 4
XHEAD_DIM = 128
D_XATTN = 512
N_BRANCH = 3
D_MIX = D_POOL + 3 * D_CONV + D_XATTN
D_IN_TOTAL = D_MIX + N_BRANCH * D_MODEL
N_EXPERTS = 32
TOP_K = 4
D_FF = D_MODEL
SWIGLU_LIMIT = 7.0
SWIGLU_ALPHA = 1.702
EPS = 1e-5

C_U = 0
C_V = D_POOL
C_B = D_POOL + D_CONV
C_C = D_POOL + 2 * D_CONV
C_Q = D_POOL + 3 * D_CONV

LANES = 128
HIST = 16

PROJ_TM = 1024
PROJ_TN = 1024
MIX_TM = 256
MIX_NS = 8
MERGE_TM = 256
ROUTE_TM = 1024
N_TABS = 9
EXP_TM = 128
EXP_CHUNK = 4
EXP_IN_TN = 1024
EXP_OUT_TN = 2048
COMB_TM = 256
COMB_PARTS = 2
SC_CORES = 2
SC_SUBCORES = 16
SC_WORKERS = SC_CORES * SC_SUBCORES
SC_CHUNK = 32
VMEM_LIMIT = 56 * 1024 * 1024


def _sigmoid(x):
    return 0.5 * (jnp.tanh(0.5 * x) + 1.0)


def _rms(x, g):
    ms = jnp.mean(x * x, axis=-1, keepdims=True)
    return x * lax.rsqrt(ms + EPS) * g


def _pack_bf16_pairs(x):
    n = x.shape[1] // 2
    lo = lax.bitcast_convert_type(x[:, :n].astype(BF16).astype(F32), jnp.uint32)
    hi = lax.bitcast_convert_type(x[:, n:].astype(BF16).astype(F32), jnp.uint32)
    return lax.bitcast_convert_type((hi & jnp.uint32(0xFFFF0000)) | (lo >> 16), PACKED)


def _unpack_bf16_pairs(p):
    p = lax.bitcast_convert_type(p, jnp.uint32)
    lo = lax.bitcast_convert_type(p << 16, F32)
    hi = lax.bitcast_convert_type(p & jnp.uint32(0xFFFF0000), F32)
    return lo, hi


def _norm_matmul_kernel(x_ref, g_ref, w_ref, b_ref, o_ref, h_ref):
    @pl.when(pl.program_id(1) == 0)
    def _():
        h_ref[...] = _rms(x_ref[...], g_ref[...]).astype(BF16)

    o_ref[...] = jnp.dot(h_ref[...], w_ref[...].astype(BF16), preferred_element_type=F32) + b_ref[...]


def _norm_matmul(x, gain, w, bias, name):
    t, d = x.shape
    n = w.shape[1]
    tm = min(PROJ_TM, t)
    tn = PROJ_TN
    assert t % tm == 0 and n % tn == 0
    return pl.pallas_call(
        _norm_matmul_kernel,
        grid=(t // tm, n // tn),
        in_specs=[
            pl.BlockSpec((tm, d), lambda i, j: (i, 0)),
            pl.BlockSpec((1, d), lambda i, j: (0, 0)),
            pl.BlockSpec((d, tn), lambda i, j: (0, j)),
            pl.BlockSpec((1, tn), lambda i, j: (0, j)),
        ],
        out_specs=pl.BlockSpec((tm, tn), lambda i, j: (i, j)),
        out_shape=jax.ShapeDtypeStruct((t, n), F32),
        scratch_shapes=[pltpu.VMEM((tm, d), BF16)],
        compiler_params=pltpu.CompilerParams(
            dimension_semantics=("arbitrary", "arbitrary"), vmem_limit_bytes=VMEM_LIMIT),
        name=name,
    )(x, gain.reshape(1, d), w, bias.reshape(1, n))


def _pool_project(pooled, wpg_ref, scale_ref, g):
    sl = slice(g * POOL_GROUP_DIM, (g + 1) * POOL_GROUP_DIM)
    y = jnp.dot(pooled.astype(BF16), wpg_ref[g].astype(BF16), preferred_element_type=F32)
    return y * scale_ref[:, sl]


def _softmax_rows(s):
    m = jnp.max(s, axis=-1, keepdims=True)
    e = jnp.exp(s - m)
    return e / jnp.sum(e, axis=-1, keepdims=True)


def _mix_prompt_kernel(cur_ref, prev_ref, mk_ref, mv_ref, wpg_ref, scale_ref, wconv_ref,
                       br_ref, zst_ref, extu_ref, extz_ref):
    t = pl.program_id(1)
    tm = cur_ref.shape[0]
    has_prev = t > 0

    u = cur_ref[:, C_U:C_U + D_POOL]
    extu_ref[0:HIST, :] = jnp.where(has_prev, prev_ref[:, C_U:C_U + D_POOL], 0.0)
    extu_ref[HIST:HIST + tm, :] = u
    pos = t * tm + lax.broadcasted_iota(jnp.int32, (tm, 1), 0)
    for g, w in enumerate(POOL_WINDOWS):
        sl = slice(g * POOL_GROUP_DIM, (g + 1) * POOL_GROUP_DIM)
        s = extu_ref[HIST:HIST + tm, sl]
        for k in range(1, w):
            s = s + extu_ref[HIST - k:HIST - k + tm, sl]
        cnt = jnp.minimum(w, pos + 1).astype(F32)
        pooled = s / cnt - extu_ref[HIST:HIST + tm, sl]
        br_ref[:, sl] = _pool_project(pooled, wpg_ref, scale_ref, g).astype(BF16)

    z = cur_ref[:, C_C:C_C + D_CONV] * cur_ref[:, C_V:C_V + D_CONV]
    zprev = prev_ref[:, C_C:C_C + D_CONV] * prev_ref[:, C_V:C_V + D_CONV]
    extz_ref[0:HIST, :] = jnp.where(has_prev, zprev, 0.0)
    extz_ref[HIST:HIST + tm, :] = z
    y = extz_ref[HIST - 2:HIST - 2 + tm, :] * wconv_ref[0:1, :]
    y = y + extz_ref[HIST - 1:HIST - 1 + tm, :] * wconv_ref[1:2, :]
    y = y + extz_ref[HIST:HIST + tm, :] * wconv_ref[2:3, :]
    br_ref[:, D_POOL:D_POOL + D_CONV] = (cur_ref[:, C_B:C_B + D_CONV] * y).astype(BF16)
    zst_ref[0] = extz_ref[HIST + tm - 8:HIST + tm, :]

    for h in range(N_XHEADS):
        sl = slice(h * XHEAD_DIM, (h + 1) * XHEAD_DIM)
        qh = cur_ref[:, C_Q + h * XHEAD_DIM:C_Q + (h + 1) * XHEAD_DIM].astype(BF16)
        kh = mk_ref[0, :, sl].astype(BF16)
        vh = mv_ref[0, :, sl].astype(BF16)
        s = lax.dot_general(qh, kh, (((1,), (1,)), ((), ())), preferred_element_type=F32)
        p = _softmax_rows(s * (XHEAD_DIM ** -0.5))
        o = jnp.dot(p.astype(BF16), vh, preferred_element_type=F32)
        c0 = D_POOL + D_CONV + h * XHEAD_DIM
        br_ref[:, c0:c0 + XHEAD_DIM] = o.astype(BF16)


def _mix_prompt(proj, mk, mv, wpg, scale, wconv, batch, seq):
    tm = MIX_TM
    nt = seq // tm
    assert seq % tm == 0 and tm % HIST == 0
    rpb = tm // HIST
    return pl.pallas_call(
        _mix_prompt_kernel,
        grid=(batch, nt),
        in_specs=[
            pl.BlockSpec((tm, D_MIX), lambda b, t: (b * nt + t, 0)),
            pl.BlockSpec((HIST, D_MIX), lambda b, t: (jnp.maximum((b * nt + t) * rpb - 1, 0), 0)),
            pl.BlockSpec((1, N_MEM, D_XATTN), lambda b, t: (b, 0, 0)),
            pl.BlockSpec((1, N_MEM, D_XATTN), lambda b, t: (b, 0, 0)),
            pl.BlockSpec((len(POOL_WINDOWS), POOL_GROUP_DIM, POOL_GROUP_DIM), lambda b, t: (0, 0, 0)),
            pl.BlockSpec((1, D_POOL), lambda b, t: (0, 0)),
            pl.BlockSpec((CONV_WIDTH, D_CONV), lambda b, t: (0, 0)),
        ],
        out_specs=[
            pl.BlockSpec((tm, D_MODEL), lambda b, t: (b * nt + t, 0)),
            pl.BlockSpec((1, 8, D_CONV), lambda b, t: (b, 0, 0)),
        ],
        out_shape=[
            jax.ShapeDtypeStruct((batch * seq, D_MODEL), BF16),
            jax.ShapeDtypeStruct((batch, 8, D_CONV), F32),
        ],
        scratch_shapes=[pltpu.VMEM((HIST + tm, D_POOL), F32), pltpu.VMEM((HIST + tm, D_CONV), F32)],
        compiler_params=pltpu.CompilerParams(
            dimension_semantics=("arbitrary", "arbitrary"), vmem_limit_bytes=VMEM_LIMIT),
        name="mix_prompt",
    )(proj, proj, mk, mv, wpg, scale.reshape(1, D_POOL), wconv)


def _mix_sample_kernel(cur_ref, sp_ref, sc_ref, k_ref, v_ref, wpg_ref, scale_ref, wconv_ref,
                       br_ref, zst_ref, extu_ref, extz_ref):
    ns, ln = zst_ref.shape[0], zst_ref.shape[1]
    rows = ns * ln

    extu_ref[:, HIST - POOL_STATE_LEN:HIST, :] = sp_ref[...]
    extu_ref[:, HIST:HIST + ln, :] = cur_ref[:, C_U:C_U + D_POOL].reshape(ns, ln, D_POOL)
    for g, w in enumerate(POOL_WINDOWS):
        sl = slice(g * POOL_GROUP_DIM, (g + 1) * POOL_GROUP_DIM)
        s = extu_ref[:, HIST:HIST + ln, sl]
        for k in range(1, w):
            s = s + extu_ref[:, HIST - k:HIST - k + ln, sl]
        pooled = s / float(w) - extu_ref[:, HIST:HIST + ln, sl]
        pooled = pooled.reshape(rows, POOL_GROUP_DIM)
        br_ref[:, sl] = _pool_project(pooled, wpg_ref, scale_ref, g).astype(BF16)

    z = cur_ref[:, C_C:C_C + D_CONV] * cur_ref[:, C_V:C_V + D_CONV]
    extz_ref[:, HIST - 2:HIST, :] = sc_ref[...]
    extz_ref[:, HIST:HIST + ln, :] = z.reshape(ns, ln, D_CONV)
    y = extz_ref[:, HIST - 2:HIST - 2 + ln, :] * wconv_ref[0:1, :]
    y = y + extz_ref[:, HIST - 1:HIST - 1 + ln, :] * wconv_ref[1:2, :]
    y = y + extz_ref[:, HIST:HIST + ln, :] * wconv_ref[2:3, :]
    br_ref[:, D_POOL:D_POOL + D_CONV] = (
        cur_ref[:, C_B:C_B + D_CONV] * y.reshape(rows, D_CONV)).astype(BF16)
    zst_ref[...] = extz_ref[:, HIST:HIST + ln, :]

    q3 = cur_ref[:, C_Q:C_Q + D_XATTN].reshape(ns, ln, D_XATTN)
    q4 = jnp.concatenate([q3[:, :, h * XHEAD_DIM:(h + 1) * XHEAD_DIM] for h in range(N_XHEADS)], axis=1)
    s = jnp.einsum("nqd,nkd->nqk", q4.astype(BF16), k_ref[...].astype(BF16),
                   preferred_element_type=F32)
    row_head = lax.broadcasted_iota(jnp.int32, (N_XHEADS * ln, N_XHEADS * N_MEM), 0) // ln
    col_head = lax.broadcasted_iota(jnp.int32, (N_XHEADS * ln, N_XHEADS * N_MEM), 1) % N_XHEADS
    s = jnp.where((row_head == col_head)[None], s * (XHEAD_DIM ** -0.5), -jnp.inf)
    p = _softmax_rows(s)
    r = jnp.einsum("nqk,nkd->nqd", p.astype(BF16), v_ref[...].astype(BF16),
                   preferred_element_type=F32)
    for h in range(N_XHEADS):
        c0 = D_POOL + D_CONV + h * XHEAD_DIM
        br_ref[:, c0:c0 + XHEAD_DIM] = r[:, h * ln:(h + 1) * ln, :].reshape(rows, XHEAD_DIM).astype(BF16)


def _mix_sample(proj, state_pool, state_conv, mem_k, mem_v, wpg, scale, wconv, nseq, ln):
    ns = MIX_NS
    rows = ns * ln
    assert nseq % ns == 0 and ln == 8
    return pl.pallas_call(
        _mix_sample_kernel,
        grid=(nseq // ns,),
        in_specs=[
            pl.BlockSpec((rows, D_MIX), lambda s: (s, 0)),
            pl.BlockSpec((ns, POOL_STATE_LEN, D_POOL), lambda s: (s, 0, 0)),
            pl.BlockSpec((ns, CONV_WIDTH - 1, D_CONV), lambda s: (s, 0, 0)),
            pl.BlockSpec((ns, N_MEM * N_XHEADS, XHEAD_DIM), lambda s: (s, 0, 0)),
            pl.BlockSpec((ns, N_MEM * N_XHEADS, XHEAD_DIM), lambda s: (s, 0, 0)),
            pl.BlockSpec((len(POOL_WINDOWS), POOL_GROUP_DIM, POOL_GROUP_DIM), lambda s: (0, 0, 0)),
            pl.BlockSpec((1, D_POOL), lambda s: (0, 0)),
            pl.BlockSpec((CONV_WIDTH, D_CONV), lambda s: (0, 0)),
        ],
        out_specs=[
            pl.BlockSpec((rows, D_MODEL), lambda s: (s, 0)),
            pl.BlockSpec((ns, ln, D_CONV), lambda s: (s, 0, 0)),
        ],
        out_shape=[
            jax.ShapeDtypeStruct((nseq * ln, D_MODEL), BF16),
            jax.ShapeDtypeStruct((nseq, ln, D_CONV), F32),
        ],
        scratch_shapes=[pltpu.VMEM((ns, HIST + ln, D_POOL), F32),
                        pltpu.VMEM((ns, HIST + ln, D_CONV), F32)],
        compiler_params=pltpu.CompilerParams(
            dimension_semantics=("arbitrary",), vmem_limit_bytes=VMEM_LIMIT),
        name="mix_sample",
    )(proj, state_pool, state_conv, mem_k, mem_v, wpg, scale.reshape(1, D_POOL), wconv)


def _merge_route_kernel(x_ref, br_ref, g0_ref, g1_ref, g2_ref, wpo_ref, wco_ref, wao_ref, wo_ref,
                        nf_ref, wrhl_ref, wrh_ref, brt_ref,
                        x1_ref, h2_ref, ri_ref, rw_ref, cnt_ref, carry_ref):
    i = pl.program_id(0)
    tm = x_ref.shape[0]

    @pl.when(i == 0)
    def _():
        carry_ref[...] = jnp.zeros_like(carry_ref)

    merged = _sigmoid(g0_ref[...]) * jnp.dot(br_ref[:, 0:D_POOL], wpo_ref[...], preferred_element_type=F32)
    merged = merged + _sigmoid(g1_ref[...]) * jnp.dot(
        br_ref[:, D_POOL:D_POOL + D_CONV], wco_ref[...], preferred_element_type=F32)
    merged = merged + _sigmoid(g2_ref[...]) * jnp.dot(
        br_ref[:, D_POOL + D_CONV:D_MODEL], wao_ref[...], preferred_element_type=F32)
    x1 = x_ref[...] + jnp.dot(merged.astype(BF16), wo_ref[...], preferred_element_type=F32)
    x1_ref[...] = x1
    h2 = _rms(x1, nf_ref[...])
    h2_ref[...] = _pack_bf16_pairs(h2)

    h2_hi = h2.astype(BF16)
    h2_lo = (h2 - h2_hi.astype(F32)).astype(BF16)
    hi_part = jnp.dot(h2_hi, wrhl_ref[...], preferred_element_type=F32)
    logits = (hi_part[:, 0:N_EXPERTS] + hi_part[:, N_EXPERTS:2 * N_EXPERTS]
              + jnp.dot(h2_lo, wrh_ref[...], preferred_element_type=F32)) + brt_ref[...]
    lane = lax.broadcasted_iota(jnp.int32, (tm, N_EXPERTS), 1).astype(F32)
    vals, idxs, hots = [], [], []
    work = logits
    for _ in range(TOP_K):
        m = jnp.max(work, axis=-1, keepdims=True)
        idx = jnp.min(jnp.where(work == m, lane, float(N_EXPERTS)), axis=-1, keepdims=True)
        hot = lane == idx
        work = jnp.where(hot, -jnp.inf, work)
        vals.append(m)
        idxs.append(idx)
        hots.append(hot)
    es = [jnp.exp(v - vals[0]) for v in vals]
    denom = es[0] + es[1] + es[2] + es[3]

    chosen = jnp.where(hots[0] | hots[1] | hots[2] | hots[3], 1.0, 0.0).astype(BF16)
    r_i = lax.broadcasted_iota(jnp.int32, (tm, tm), 0)
    c_i = lax.broadcasted_iota(jnp.int32, (tm, tm), 1)
    lower = jnp.where(c_i < r_i, 1.0, 0.0).astype(BF16)
    before = jnp.dot(lower, chosen, preferred_element_type=F32) + carry_ref[0:1, 0:N_EXPERTS]
    carry_ref[0:1, 0:N_EXPERTS] = (carry_ref[0:1, 0:N_EXPERTS]
                                   + jnp.sum(chosen.astype(F32), axis=0, keepdims=True))
    cnt_ref[...] = carry_ref[...]

    out_lane = lax.broadcasted_iota(jnp.int32, (tm, LANES), 1)
    ri = jnp.zeros((tm, LANES), jnp.int32)
    rw = jnp.zeros((tm, LANES), F32)
    for k in range(TOP_K):
        rank = jnp.sum(jnp.where(hots[k], before, 0.0), axis=-1, keepdims=True).astype(jnp.int32)
        ri = jnp.where(out_lane == k, idxs[k].astype(jnp.int32), ri)
        ri = jnp.where(out_lane == TOP_K + k, rank, ri)
        rw = jnp.where(out_lane == k, es[k] / denom, rw)
    ri_ref[...] = ri
    rw_ref[...] = rw


def _merge_route(x, branch, proj, wpo, wco, wao, wo, norm_ffn, w_router, b_router, name):
    t = x.shape[0]
    tm = MERGE_TM
    assert t % tm == 0
    gate_blk0 = D_MIX // D_MODEL
    assert D_MIX % D_MODEL == 0
    const = lambda i: (0, 0)
    wr_hi = w_router.astype(BF16)
    wr_lo = (w_router - wr_hi.astype(F32)).astype(BF16)
    return pl.pallas_call(
        _merge_route_kernel,
        grid=(t // tm,),
        in_specs=[
            pl.BlockSpec((tm, D_MODEL), lambda i: (i, 0)),
            pl.BlockSpec((tm, D_MODEL), lambda i: (i, 0)),
            pl.BlockSpec((tm, D_MODEL), lambda i: (i, gate_blk0)),
            pl.BlockSpec((tm, D_MODEL), lambda i: (i, gate_blk0 + 1)),
            pl.BlockSpec((tm, D_MODEL), lambda i: (i, gate_blk0 + 2)),
            pl.BlockSpec((D_POOL, D_MODEL), const, pipeline_mode=pl.Buffered(1)),
            pl.BlockSpec((D_CONV, D_MODEL), const, pipeline_mode=pl.Buffered(1)),
            pl.BlockSpec((D_XATTN, D_MODEL), const, pipeline_mode=pl.Buffered(1)),
            pl.BlockSpec((D_MODEL, D_MODEL), const, pipeline_mode=pl.Buffered(1)),
            pl.BlockSpec((1, D_MODEL), const),
            pl.BlockSpec((D_MODEL, 2 * N_EXPERTS), const),
            pl.BlockSpec((D_MODEL, N_EXPERTS), const),
            pl.BlockSpec((1, N_EXPERTS), const),
        ],
        out_specs=[
            pl.BlockSpec((tm, D_MODEL), lambda i: (i, 0)),
            pl.BlockSpec((tm, D_MODEL // 2), lambda i: (i, 0)),
            pl.BlockSpec((tm, LANES), lambda i: (i, 0)),
            pl.BlockSpec((tm, LANES), lambda i: (i, 0)),
            pl.BlockSpec((8, LANES), const),
        ],
        out_shape=[
            jax.ShapeDtypeStruct((t, D_MODEL), F32),
            jax.ShapeDtypeStruct((t, D_MODEL // 2), PACKED),
            jax.ShapeDtypeStruct((t, LANES), jnp.int32),
            jax.ShapeDtypeStruct((t, LANES), F32),
            jax.ShapeDtypeStruct((8, LANES), F32),
        ],
        scratch_shapes=[pltpu.VMEM((8, LANES), F32)],
        compiler_params=pltpu.CompilerParams(
            dimension_semantics=("arbitrary",), vmem_limit_bytes=VMEM_LIMIT),
        name=name,
    )(x, branch, proj, proj, proj, wpo, wco, wao, wo, norm_ffn.reshape(1, D_MODEL),
      jnp.concatenate([wr_hi, wr_lo], axis=1), wr_hi, b_router.reshape(1, N_EXPERTS))


def _fill_work_list(nblk_s, pad_s, nj, refs):
    te, tj, trow, tnb, tfirst, tne, tnj, tpar, ttot = refs

    def clear(i, c):
        for r in (te, tj, trow, tnb, tfirst, tpar):
            r[i] = 0
        tne[i] = -1
        tnj[i] = -1
        return c

    lax.fori_loop(0, te.shape[0], clear, 0)

    carry = (jnp.int32(0), jnp.int32(-1), jnp.int32(0))
    for j in range(nj):
        def per_expert(e, carry):
            q, prev, groups = carry
            n = nblk_s[e]
            nch = (n + EXP_CHUNK - 1) // EXP_CHUNK

            @pl.when((nch > 0) & (prev >= 0))
            def _():
                tne[prev] = e
                tnj[prev] = jnp.int32(j)

            def per_chunk(c, q):
                te[q] = e
                tj[q] = jnp.int32(j)
                trow[q] = pad_s[e] + c * (EXP_CHUNK * EXP_TM)
                tnb[q] = jnp.minimum(EXP_CHUNK, n - c * EXP_CHUNK)
                tfirst[q] = (c == 0).astype(jnp.int32)
                tpar[q] = lax.rem(groups, 2)
                return q + 1

            return (lax.fori_loop(0, nch, per_chunk, q), jnp.where(nch > 0, q, prev),
                    groups + (nch > 0).astype(jnp.int32))

        carry = lax.fori_loop(0, N_EXPERTS, per_expert, carry)
    ttot[0] = carry[0]


def _route_tables_kernel(cntp_ref, cnts_ref, rip_ref, ris_ref, dest_ref, *refs, nj_in, nj_out):
    tabs_in, tabs_out = refs[0:N_TABS], refs[N_TABS:2 * N_TABS]
    nblk_s, pad_s = refs[2 * N_TABS:]

    def per_expert(e, start_blk):
        n = (cntp_ref[e] + cnts_ref[e] + EXP_TM - 1) // EXP_TM
        nblk_s[e] = n
        pad_s[e] = start_blk * EXP_TM
        return start_blk + n

    lax.fori_loop(0, N_EXPERTS, per_expert, jnp.int32(0))
    _fill_work_list(nblk_s, pad_s, nj_in, tabs_in)
    _fill_work_list(nblk_s, pad_s, nj_out, tabs_out)

    tile = ROUTE_TM
    col0 = 0
    for ri_ref, is_sample in ((rip_ref, False), (ris_ref, True)):
        for r in range(ri_ref.shape[0] // tile):
            ri = ri_ref[r * tile:(r + 1) * tile, :]
            base = jnp.zeros_like(ri)
            for e in range(N_EXPERTS):
                first_row = pad_s[e] + cntp_ref[e] if is_sample else pad_s[e]
                base = jnp.where(ri == e, first_row, base)
            dest = base + pltpu.roll(ri, LANES - TOP_K, axis=1)
            dest_ref[:, col0:col0 + tile] = jnp.transpose(dest)[0:8, :]
            col0 += tile


def _route_tables(cnt_p, cnt_s, ri_p, ri_s, nj_in, nj_out, max_items):
    t_all = ri_p.shape[0] + ri_s.shape[0]
    assert ri_p.shape[0] % ROUTE_TM == 0 and ri_s.shape[0] % ROUTE_TM == 0
    smem = pl.BlockSpec(memory_space=pltpu.SMEM)

    def tables(nj):
        n = nj * max_items
        return [jax.ShapeDtypeStruct((n,), jnp.int32)] * (N_TABS - 1) + [jax.ShapeDtypeStruct((1,), jnp.int32)]

    out = pl.pallas_call(
        functools.partial(_route_tables_kernel, nj_in=nj_in, nj_out=nj_out),
        grid=(1,),
        in_specs=[smem, smem,
                  pl.BlockSpec(ri_p.shape, lambda i: (0, 0)), pl.BlockSpec(ri_s.shape, lambda i: (0, 0))],
        out_specs=[pl.BlockSpec((8, t_all), lambda i: (0, 0))] + [smem] * (2 * N_TABS),
        out_shape=[jax.ShapeDtypeStruct((8, t_all), jnp.int32)] + tables(nj_in) + tables(nj_out),
        scratch_shapes=[pltpu.SMEM((N_EXPERTS,), jnp.int32), pltpu.SMEM((N_EXPERTS,), jnp.int32)],
        compiler_params=pltpu.CompilerParams(
            dimension_semantics=("arbitrary",), vmem_limit_bytes=VMEM_LIMIT),
        name="route_tables",
    )(cnt_p, cnt_s, ri_p, ri_s)
    return out[0], tuple(out[1:1 + N_TABS]), tuple(out[1 + N_TABS:1 + 2 * N_TABS])


def _sc_mesh():
    return plsc.VectorSubcoreMesh(core_axis_name="c", subcore_axis_name="s",
                                  num_cores=SC_CORES, num_subcores=SC_SUBCORES)


def _sc_worker_id():
    return lax.axis_index("s") * SC_CORES + lax.axis_index("c")


def _sc_dispatch(h2_a, h2_b, dest_by_slot, rows_out):
    ta, w = h2_a.shape
    t = ta + h2_b.shape[0]
    per_w = t // SC_WORKERS
    n_chunks = per_w // SC_CHUNK
    assert t == SC_WORKERS * n_chunks * SC_CHUNK and ta % SC_CHUNK == 0
    idx = dest_by_slot[:TOP_K].reshape(TOP_K, SC_WORKERS, n_chunks, SC_CHUNK).transpose(1, 0, 2, 3)

    def body(a_hbm, b_hbm, idx_hbm, xs_hbm, idx_v, rows_v, sems):
        base = _sc_worker_id() * per_w
        pltpu.sync_copy(idx_hbm.at[_sc_worker_id()], idx_v)
        pending = [[], []]
        for c in range(n_chunks):
            b = c % 2
            for d in pending[b]:
                d.wait()
            tok0 = base + c * SC_CHUNK

            @pl.when(tok0 < ta)
            def _():
                pltpu.sync_copy(a_hbm.at[pl.ds(tok0, SC_CHUNK)], rows_v.at[b])

            @pl.when(tok0 >= ta)
            def _():
                pltpu.sync_copy(b_hbm.at[pl.ds(tok0 - ta, SC_CHUNK)], rows_v.at[b])

            pending[b] = [pltpu.async_copy(rows_v.at[b], xs_hbm.at[idx_v.at[k, c]], sems.at[b])
                          for k in range(TOP_K)]
        for b in range(2):
            for d in pending[b]:
                d.wait()

    return pl.kernel(
        body,
        out_type=jax.ShapeDtypeStruct((rows_out, w), h2_a.dtype),
        mesh=_sc_mesh(),
        scratch_types=[pltpu.VMEM((TOP_K, n_chunks, SC_CHUNK), jnp.int32),
                       pltpu.VMEM((2, SC_CHUNK, w), h2_a.dtype),
                       pltpu.SemaphoreType.DMA((2,))],
        name="sc_dispatch",
    )(h2_a, h2_b, idx)


def _sc_gather_rows(table, idx):
    n = idx.shape[0]
    w = table.shape[1]
    per_w = n // SC_WORKERS
    n_chunks = per_w // SC_CHUNK
    assert n == SC_WORKERS * n_chunks * SC_CHUNK
    idx3 = idx.reshape(SC_WORKERS, n_chunks, SC_CHUNK)

    def body(table_hbm, idx_hbm, out_hbm, idx_v, rows_v, gsem, wsems):
        base = _sc_worker_id() * per_w
        pltpu.sync_copy(idx_hbm.at[_sc_worker_id()], idx_v)
        pending = [None, None]
        for c in range(n_chunks):
            b = c % 2
            if pending[b] is not None:
                pending[b].wait()
            pltpu.async_copy(table_hbm.at[idx_v.at[c]], rows_v.at[b], gsem).wait()
            pending[b] = pltpu.async_copy(
                rows_v.at[b], out_hbm.at[pl.ds(base + c * SC_CHUNK, SC_CHUNK)], wsems.at[b])
        for b in range(2):
            if pending[b] is not None:
                pending[b].wait()

    return pl.kernel(
        body,
        out_type=jax.ShapeDtypeStruct((n, w), table.dtype),
        mesh=_sc_mesh(),
        scratch_types=[pltpu.VMEM((n_chunks, SC_CHUNK), jnp.int32),
                       pltpu.VMEM((2, SC_CHUNK, w), table.dtype),
                       pltpu.SemaphoreType.DMA,
                       pltpu.SemaphoreType.DMA((2,))],
        name="sc_gather_rows",
    )(table, idx3)


def _grouped_pipeline(tabs, total_ref, weight_copies, in_copy, out_copy, compute):
    te, tj, _, tnb, tfirst, tne, tnj, tpar = tabs
    total = total_ref[0]

    def each_block(q, copy, op):
        slot = lax.rem(q, 2)
        for k in range(EXP_CHUNK):
            @pl.when(k < tnb[q])
            def _():
                op(copy(q, slot, k))

    def start(d):
        d.start()

    def wait(d):
        d.wait()

    for d in weight_copies(te[0], tj[0], tpar[0]):
        d.start()
    each_block(0, in_copy, start)

    def body(q, carry):
        each_block(q, in_copy, wait)

        @pl.when(q + 1 < total)
        def _():
            each_block(q + 1, in_copy, start)

        @pl.when(tfirst[q] == 1)
        def _():
            for d in weight_copies(te[q], tj[q], tpar[q]):
                d.wait()

            @pl.when(tne[q] >= 0)
            def _():
                for d in weight_copies(tne[q], tnj[q], 1 - tpar[q]):
                    d.start()

        @pl.when(q >= 2)
        def _():
            each_block(q - 2, out_copy, wait)

        for m in range(1, EXP_CHUNK + 1):
            @pl.when(tnb[q] == m)
            def _():
                compute(q, lax.rem(q, 2), m, tpar[q])

        each_block(q, out_copy, start)
        return carry

    lax.fori_loop(0, total, body, 0)

    @pl.when(total >= 2)
    def _():
        each_block(total - 2, out_copy, wait)

    each_block(total - 1, out_copy, wait)


def _expert_in_kernel(*refs):
    tabs, total_ref = refs[:N_TABS - 1], refs[N_TABS - 1]
    xs_hbm, w_hbm, b_ref, act_hbm, wbuf, x_buf, a_buf, w_sem, x_sem, a_sem = refs[N_TABS:]
    te, tj, trow = tabs[:3]
    tm, tn = EXP_TM, a_buf.shape[2]
    nj = D_FF // tn

    def weight_copies(e, j, par):
        return [pltpu.make_async_copy(
            w_hbm.at[e, :, pl.ds(pl.multiple_of(half * D_FF + j * tn, tn), tn)],
            wbuf.at[2 * par + half], w_sem)
            for half in range(2)]

    def rows(q, k):
        return pl.ds(pl.multiple_of(trow[q] + k * tm, tm), tm)

    def x_copy(q, slot, k):
        return pltpu.make_async_copy(xs_hbm.at[rows(q, k)], x_buf.at[slot, pl.ds(k * tm, tm)], x_sem.at[slot])

    def a_copy(q, slot, k):
        return pltpu.make_async_copy(
            a_buf.at[slot, pl.ds(k * tm, tm)],
            act_hbm.at[rows(q, k), pl.ds(pl.multiple_of(tj[q] * tn, tn), tn)], a_sem.at[slot])

    def compute(q, slot, m, par):
        bias0 = te[q] * (2 * nj) + tj[q]
        x = jnp.concatenate(_unpack_bf16_pairs(x_buf[slot, 0:m * tm]), axis=1).astype(BF16)
        g = jnp.dot(x, wbuf[2 * par].astype(BF16), preferred_element_type=F32) + b_ref[bias0]
        up = jnp.dot(x, wbuf[2 * par + 1].astype(BF16), preferred_element_type=F32) + b_ref[bias0 + nj]
        g = jnp.minimum(g, SWIGLU_LIMIT)
        up = jnp.clip(up, -SWIGLU_LIMIT, SWIGLU_LIMIT)
        a_buf[slot, 0:m * tm] = (g * _sigmoid(SWIGLU_ALPHA * g) * (up + 1.0)).astype(BF16)

    _grouped_pipeline(tabs, total_ref, weight_copies, x_copy, a_copy, compute)


def _expert_in(tabs, xs, w_in, b_in):
    rows = xs.shape[0]
    tm, tn = EXP_TM, EXP_IN_TN
    nj = D_FF // tn
    bias = b_in.reshape(N_EXPERTS * 2 * nj, 1, tn)
    grid_spec = pltpu.PrefetchScalarGridSpec(
        num_scalar_prefetch=len(tabs),
        grid=(1,),
        in_specs=[
            pl.BlockSpec(memory_space=pl.ANY),
            pl.BlockSpec(memory_space=pl.ANY),
            pl.BlockSpec(bias.shape, lambda i, *_: (0, 0, 0), pipeline_mode=pl.Buffered(1)),
        ],
        out_specs=pl.BlockSpec(memory_space=pl.ANY),
        scratch_shapes=[pltpu.VMEM((4, D_MODEL, tn), F32),
                        pltpu.VMEM((2, EXP_CHUNK * tm, D_MODEL // 2), PACKED),
                        pltpu.VMEM((2, EXP_CHUNK * tm, tn), BF16),
                        pltpu.SemaphoreType.DMA, pltpu.SemaphoreType.DMA((2,)),
                        pltpu.SemaphoreType.DMA((2,))],
    )
    return pl.pallas_call(
        _expert_in_kernel,
        grid_spec=grid_spec,
        out_shape=jax.ShapeDtypeStruct((rows, D_FF), BF16),
        compiler_params=pltpu.CompilerParams(
            dimension_semantics=("arbitrary",), vmem_limit_bytes=VMEM_LIMIT),
        name="expert_in",
    )(*tabs, xs, w_in, bias)


def _expert_out_kernel(*refs):
    tabs, total_ref = refs[:N_TABS - 1], refs[N_TABS - 1]
    act_hbm, w_hbm, b_ref, ys_hbm, wbuf, a_buf, y_buf, w_sem, a_sem, y_sem = refs[N_TABS:]
    te, tj, trow = tabs[:3]
    tm, tnp = EXP_TM, y_buf.shape[2]
    tn = 2 * tnp
    nj = D_MODEL // tn

    def weight_copies(e, j, par):
        return [pltpu.make_async_copy(
            w_hbm.at[e, :, pl.ds(pl.multiple_of(j * tn, tn), tn)], wbuf.at[par], w_sem)]

    def rows(q, k):
        return pl.ds(pl.multiple_of(trow[q] + k * tm, tm), tm)

    def a_copy(q, slot, k):
        return pltpu.make_async_copy(act_hbm.at[rows(q, k)], a_buf.at[slot, pl.ds(k * tm, tm)], a_sem.at[slot])

    def y_copy(q, slot, k):
        return pltpu.make_async_copy(
            y_buf.at[slot, pl.ds(k * tm, tm)],
            ys_hbm.at[rows(q, k), pl.ds(pl.multiple_of(tj[q] * tnp, tnp), tnp)], y_sem.at[slot])

    def compute(q, slot, m, par):
        y = jnp.dot(a_buf[slot, 0:m * tm], wbuf[par].astype(BF16),
                    preferred_element_type=F32) + b_ref[te[q] * nj + tj[q]]
        y_buf[slot, 0:m * tm] = _pack_bf16_pairs(y)

    _grouped_pipeline(tabs, total_ref, weight_copies, a_copy, y_copy, compute)


def _expert_out(tabs, act, w_out, b_out):
    rows = act.shape[0]
    tm, tn = EXP_TM, EXP_OUT_TN
    nj = D_MODEL // tn
    bias = b_out.reshape(N_EXPERTS * nj, 1, tn)
    grid_spec = pltpu.PrefetchScalarGridSpec(
        num_scalar_prefetch=len(tabs),
        grid=(1,),
        in_specs=[
            pl.BlockSpec(memory_space=pl.ANY),
            pl.BlockSpec(memory_space=pl.ANY),
            pl.BlockSpec(bias.shape, lambda i, *_: (0, 0, 0), pipeline_mode=pl.Buffered(1)),
        ],
        out_specs=pl.BlockSpec(memory_space=pl.ANY),
        scratch_shapes=[pltpu.VMEM((2, D_FF, tn), F32),
                        pltpu.VMEM((2, EXP_CHUNK * tm, D_FF), BF16),
                        pltpu.VMEM((2, EXP_CHUNK * tm, tn // 2), PACKED),
                        pltpu.SemaphoreType.DMA, pltpu.SemaphoreType.DMA((2,)),
                        pltpu.SemaphoreType.DMA((2,))],
    )
    return pl.pallas_call(
        _expert_out_kernel,
        grid_spec=grid_spec,
        out_shape=jax.ShapeDtypeStruct((rows, D_MODEL // 2), PACKED),
        compiler_params=pltpu.CompilerParams(
            dimension_semantics=("arbitrary",), vmem_limit_bytes=VMEM_LIMIT),
        name="expert_out",
    )(*tabs, act, w_out, bias)


def _combine_kernel(x1_ref, yg_ref, rw_ref, g_ref, o_ref):
    half = EXP_OUT_TN // 2
    pieces = []
    ssq = jnp.zeros((x1_ref.shape[0], 1), F32)
    for jt in range(D_MODEL // EXP_OUT_TN):
        acc_lo = x1_ref[:, jt * EXP_OUT_TN:jt * EXP_OUT_TN + half]
        acc_hi = x1_ref[:, jt * EXP_OUT_TN + half:(jt + 1) * EXP_OUT_TN]
        moe_lo = jnp.zeros_like(acc_lo)
        moe_hi = jnp.zeros_like(acc_hi)
        for k in range(TOP_K):
            lo, hi = _unpack_bf16_pairs(yg_ref[k, :, jt * half:(jt + 1) * half])
            moe_lo = moe_lo + lo * rw_ref[:, k:k + 1]
            moe_hi = moe_hi + hi * rw_ref[:, k:k + 1]
        for acc in (acc_lo + moe_lo, acc_hi + moe_hi):
            ssq = ssq + jnp.sum(acc * acc, axis=-1, keepdims=True)
            pieces.append(acc)
    inv = lax.rsqrt(ssq / D_MODEL + EPS)
    for n, acc in enumerate(pieces):
        o_ref[:, n * half:(n + 1) * half] = acc * inv * g_ref[:, n * half:(n + 1) * half]


def _combine_part_kernel(x1_ref, yg_ref, rw_ref, g_ref, prev_hbm, o_ref):
    del prev_hbm
    _combine_kernel(x1_ref, yg_ref, rw_ref, g_ref, o_ref)


def _combine(x1, yg, rw, gain, row0, prev, name):
    t = x1.shape[0]
    n = yg.shape[1]
    tm = COMB_TM
    assert n % tm == 0 and row0 % tm == 0
    off = row0 // tm
    in_specs = [
        pl.BlockSpec((tm, D_MODEL), lambda i: (i + off, 0)),
        pl.BlockSpec((TOP_K, tm, D_MODEL // 2), lambda i: (0, i, 0)),
        pl.BlockSpec((tm, LANES), lambda i: (i + off, 0)),
        pl.BlockSpec((1, D_MODEL), lambda i: (0, 0)),
    ]
    args = [x1, yg, rw, gain.reshape(1, D_MODEL)]
    if prev is not None:
        in_specs.append(pl.BlockSpec(memory_space=pl.ANY))
        args.append(prev)
    return pl.pallas_call(
        _combine_kernel if prev is None else _combine_part_kernel,
        grid=(n // tm,),
        in_specs=in_specs,
        out_specs=pl.BlockSpec((tm, D_MODEL), lambda i: (i + off, 0)),
        out_shape=jax.ShapeDtypeStruct((t, D_MODEL), F32),
        input_output_aliases={} if prev is None else {len(args) - 1: 0},
        compiler_params=pltpu.CompilerParams(
            dimension_semantics=("arbitrary",), vmem_limit_bytes=VMEM_LIMIT),
        name=name,
    )(*args)


def kernel(x_prompt, x_sample, state_pool, state_conv, cache_mem_k, cache_mem_v, mem_prompt,
           norm_mix, w_in, b_gate, w_pool_group, pool_scale, w_conv, mem_norm, w_mem_kv,
           w_pool_out, w_conv_out, w_attn_out, w_o, norm_ffn, w_router, b_router,
           w_exp_in, b_exp_in, w_exp_out, b_exp_out, final_norm):
    depth = norm_mix.shape[0]
    assert depth == 1
    l = 0
    bp, seq, _ = x_prompt.shape
    bs, ln, _ = x_sample.shape
    tp, ts = bp * seq, bs * ln

    kv = _norm_matmul(mem_prompt.reshape(bp * N_MEM, D_MODEL), mem_norm[l], w_mem_kv[l],
                      jnp.zeros((2 * D_XATTN,), F32), "mem_kv")
    mk = kv[:, :D_XATTN].reshape(bp, N_MEM, D_XATTN)
    mv = kv[:, D_XATTN:].reshape(bp, N_MEM, D_XATTN)

    bias_in = jnp.concatenate([jnp.zeros((D_MIX,), F32), b_gate[l]])
    xp = x_prompt.reshape(tp, D_MODEL)
    xs_ = x_sample.reshape(ts, D_MODEL)
    proj_p = _norm_matmul(xp, norm_mix[l], w_in[l], bias_in, "proj_prompt")
    proj_s = _norm_matmul(xs_, norm_mix[l], w_in[l], bias_in, "proj_sample")

    br_p, zst_p = _mix_prompt(proj_p, mk, mv, w_pool_group[l], pool_scale[l], w_conv[l], bp, seq)
    br_s, zst_s = _mix_sample(proj_s, state_pool[l], state_conv[l],
                              cache_mem_k[l].reshape(bs, N_MEM * N_XHEADS, XHEAD_DIM),
                              cache_mem_v[l].reshape(bs, N_MEM * N_XHEADS, XHEAD_DIM),
                              w_pool_group[l], pool_scale[l], w_conv[l], bs, ln)

    wpo, wco, wao, wo = (w.astype(BF16) for w in (w_pool_out[l], w_conv_out[l], w_attn_out[l], w_o[l]))
    x1_p, h2_p, ri_p, rw_p, cnt_p = _merge_route(xp, br_p, proj_p, wpo, wco, wao, wo, norm_ffn[l],
                                                  w_router[l], b_router[l], "merge_route_prompt")
    x1_s, h2_s, ri_s, rw_s, cnt_s = _merge_route(xs_, br_s, proj_s, wpo, wco, wao, wo, norm_ffn[l],
                                                  w_router[l], b_router[l], "merge_route_sample")

    t_all = tp + ts
    n_assign = t_all * TOP_K
    nb_max = n_assign // EXP_TM + N_EXPERTS
    max_items = nb_max // EXP_CHUNK + N_EXPERTS
    dest, tabs_in, tabs_out = _route_tables(
        cnt_p[0, :N_EXPERTS].astype(jnp.int32), cnt_s[0, :N_EXPERTS].astype(jnp.int32), ri_p, ri_s,
        D_FF // EXP_IN_TN, D_MODEL // EXP_OUT_TN, max_items)
    xs_sorted = _sc_dispatch(h2_p, h2_s, dest, nb_max * EXP_TM)

    act = _expert_in(tabs_in, xs_sorted, w_exp_in[l], b_exp_in[l])
    ys = _expert_out(tabs_out, act, w_exp_out[l], b_exp_out[l])
    def combine_part(x1, rw, tok0, n, row0, prev, name):
        idx = dest[:TOP_K, tok0:tok0 + n].reshape(-1)
        yg = _sc_gather_rows(ys, idx).reshape(TOP_K, n, D_MODEL // 2)
        return _combine(x1, yg, rw, final_norm, row0, prev, name)

    y_p = None
    part = tp // COMB_PARTS
    for c in range(COMB_PARTS):
        y_p = combine_part(x1_p, rw_p, c * part, part, c * part, y_p, f"combine_prompt_{c}")
    y_s = combine_part(x1_s, rw_s, tp, ts, 0, None, "combine_sample")

    new_pool_p = proj_p[:, :D_POOL].reshape(bp, seq, D_POOL)[:, seq - POOL_STATE_LEN:]
    new_conv_p = zst_p[:, 8 - (CONV_WIDTH - 1):]
    u_s = proj_s[:, :D_POOL].reshape(bs, ln, D_POOL)
    new_pool_s = jnp.concatenate([state_pool[l], u_s], axis=1)[:, -POOL_STATE_LEN:]
    new_conv_s = zst_s[:, ln - (CONV_WIDTH - 1):]

    return (y_p.reshape(bp, seq, D_MODEL), y_s.reshape(bs, ln, D_MODEL),
            new_pool_p[None], new_conv_p[None],
            mk.reshape(1, bp, N_MEM, N_XHEADS, XHEAD_DIM), mv.reshape(1, bp, N_MEM, N_XHEADS, XHEAD_DIM),
            new_pool_s[None], new_conv_s[None])
```

```python
import functools

import jax
import jax.numpy as jnp
from jax import lax
from jax.experimental import pallas as pl
from jax.experimental.pallas import tpu as pltpu
from jax.experimental.pallas import tpu_sc as plsc

F32 = jnp.float32
BF16 = jnp.bfloat16
PACKED = jnp.int32

D_MODEL = 2048
POOL_WINDOWS = (2, 4, 8, 16)
POOL_GROUP_DIM = 128
D_POOL = 512
POOL_STATE_LEN = 15
D_CONV = 1024
CONV_WIDTH = 3
N_MEM = 256
N_XHEADS = 4
XHEAD_DIM = 128
D_XATTN = 512
N_BRANCH = 3
D_MIX = D_POOL + 3 * D_CONV + D_XATTN
D_IN_TOTAL = D_MIX + N_BRANCH * D_MODEL
N_EXPERTS = 32
TOP_K = 4
D_FF = D_MODEL
SWIGLU_LIMIT = 7.0
SWIGLU_ALPHA = 1.702
EPS = 1e-5

C_U = 0
C_V = D_POOL
C_B = D_POOL + D_CONV
C_C = D_POOL + 2 * D_CONV
C_Q = D_POOL + 3 * D_CONV

LANES = 128
HIST = 16

PROJ_TM = 1024
PROJ_TN = 1024
MIX_TM = 256
MIX_NS = 8
MERGE_TM = 256
ROUTE_TM = 1024
N_TABS = 9
EXP_TM = 128
EXP_CHUNK = 4
EXP_IN_TN = 1024
EXP_OUT_TN = 2048
COMB_TM = 256
COMB_PARTS = 4
SC_CORES = 2
SC_SUBCORES = 16
SC_WORKERS = SC_CORES * SC_SUBCORES
SC_CHUNK = 32
VMEM_LIMIT = 56 * 1024 * 1024


def _sigmoid(x):
    return 0.5 * (jnp.tanh(0.5 * x) + 1.0)


def _rms(x, g):
    ms = jnp.mean(x * x, axis=-1, keepdims=True)
    return x * lax.rsqrt(ms + EPS) * g


def _pack_bf16_pairs(x):
    n = x.shape[1] // 2
    lo = lax.bitcast_convert_type(x[:, :n].astype(BF16).astype(F32), jnp.uint32)
    hi = lax.bitcast_convert_type(x[:, n:].astype(BF16).astype(F32), jnp.uint32)
    return lax.bitcast_convert_type((hi & jnp.uint32(0xFFFF0000)) | (lo >> 16), PACKED)


def _unpack_bf16_pairs(p):
    p = lax.bitcast_convert_type(p, jnp.uint32)
    lo = lax.bitcast_convert_type(p << 16, F32)
    hi = lax.bitcast_convert_type(p & jnp.uint32(0xFFFF0000), F32)
    return lo, hi


def _norm_matmul_kernel(x_ref, g_ref, w_ref, b_ref, o_ref, h_ref):
    @pl.when(pl.program_id(1) == 0)
    def _():
        h_ref[...] = _rms(x_ref[...], g_ref[...]).astype(BF16)

    o_ref[...] = jnp.dot(h_ref[...], w_ref[...].astype(BF16), preferred_element_type=F32) + b_ref[...]


def _norm_matmul(x, gain, w, bias, name):
    t, d = x.shape
    n = w.shape[1]
    tm = min(PROJ_TM, t)
    tn = PROJ_TN
    assert t % tm == 0 and n % tn == 0
    return pl.pallas_call(
        _norm_matmul_kernel,
        grid=(t // tm, n // tn),
        in_specs=[
            pl.BlockSpec((tm, d), lambda i, j: (i, 0)),
            pl.BlockSpec((1, d), lambda i, j: (0, 0)),
            pl.BlockSpec((d, tn), lambda i, j: (0, j)),
            pl.BlockSpec((1, tn), lambda i, j: (0, j)),
        ],
        out_specs=pl.BlockSpec((tm, tn), lambda i, j: (i, j)),
        out_shape=jax.ShapeDtypeStruct((t, n), F32),
        scratch_shapes=[pltpu.VMEM((tm, d), BF16)],
        compiler_params=pltpu.CompilerParams(
            dimension_semantics=("arbitrary", "arbitrary"), vmem_limit_bytes=VMEM_LIMIT),
        name=name,
    )(x, gain.reshape(1, d), w, bias.reshape(1, n))


def _norm_matmul_split_kernel(x_ref, g_ref, w_ref, b_ref, o_ref, og_ref, h_ref, *, n_main):
    j = pl.program_id(1)

    @pl.when(j == 0)
    def _():
        h_ref[...] = _rms(x_ref[...], g_ref[...]).astype(BF16)

    acc = jnp.dot(h_ref[...], w_ref[...].astype(BF16), preferred_element_type=F32) + b_ref[...]

    @pl.when(j < n_main)
    def _():
        o_ref[...] = acc

    @pl.when(j >= n_main)
    def _():
        og_ref[...] = acc.astype(BF16)


def _norm_matmul_split(x, gain, w, bias, split_col, name):
    t, d = x.shape
    n = w.shape[1]
    tm = min(PROJ_TM, t)
    tn = PROJ_TN
    assert t % tm == 0 and n % tn == 0 and split_col % tn == 0 and 0 < split_col < n
    n_main = split_col // tn
    return pl.pallas_call(
        functools.partial(_norm_matmul_split_kernel, n_main=n_main),
        grid=(t // tm, n // tn),
        in_specs=[
            pl.BlockSpec((tm, d), lambda i, j: (i, 0)),
            pl.BlockSpec((1, d), lambda i, j: (0, 0)),
            pl.BlockSpec((d, tn), lambda i, j: (0, j)),
            pl.BlockSpec((1, tn), lambda i, j: (0, j)),
        ],
        out_specs=[
            pl.BlockSpec((tm, tn), lambda i, j: (i, jnp.minimum(j, n_main - 1))),
            pl.BlockSpec((tm, tn), lambda i, j: (i, jnp.maximum(j - n_main, 0))),
        ],
        out_shape=[jax.ShapeDtypeStruct((t, split_col), F32),
                   jax.ShapeDtypeStruct((t, n - split_col), BF16)],
        scratch_shapes=[pltpu.VMEM((tm, d), BF16)],
        compiler_params=pltpu.CompilerParams(
            dimension_semantics=("arbitrary", "arbitrary"), vmem_limit_bytes=VMEM_LIMIT),
        name=name,
    )(x, gain.reshape(1, d), w, bias.reshape(1, n))


def _pool_project(pooled, wpg_ref, scale_ref, g):
    sl = slice(g * POOL_GROUP_DIM, (g + 1) * POOL_GROUP_DIM)
    y = jnp.dot(pooled.astype(BF16), wpg_ref[g].astype(BF16), preferred_element_type=F32)
    return y * scale_ref[:, sl]


def _softmax_rows(s):
    m = jnp.max(s, axis=-1, keepdims=True)
    e = jnp.exp(s - m)
    return e / jnp.sum(e, axis=-1, keepdims=True)


def _mix_prompt_kernel(cur_ref, prev_ref, mk_ref, mv_ref, wpg_ref, scale_ref, wconv_ref,
                       br_ref, zst_ref, extu_ref, extz_ref):
    t = pl.program_id(1)
    tm = cur_ref.shape[0]
    has_prev = t > 0

    u = cur_ref[:, C_U:C_U + D_POOL]
    extu_ref[0:HIST, :] = jnp.where(has_prev, prev_ref[:, C_U:C_U + D_POOL], 0.0)
    extu_ref[HIST:HIST + tm, :] = u
    pos = t * tm + lax.broadcasted_iota(jnp.int32, (tm, 1), 0)
    for g, w in enumerate(POOL_WINDOWS):
        sl = slice(g * POOL_GROUP_DIM, (g + 1) * POOL_GROUP_DIM)
        s = extu_ref[HIST:HIST + tm, sl]
        for k in range(1, w):
            s = s + extu_ref[HIST - k:HIST - k + tm, sl]
        cnt = jnp.minimum(w, pos + 1).astype(F32)
        pooled = s / cnt - extu_ref[HIST:HIST + tm, sl]
        br_ref[:, sl] = _pool_project(pooled, wpg_ref, scale_ref, g).astype(BF16)

    z = cur_ref[:, C_C:C_C + D_CONV] * cur_ref[:, C_V:C_V + D_CONV]
    zprev = prev_ref[:, C_C:C_C + D_CONV] * prev_ref[:, C_V:C_V + D_CONV]
    extz_ref[0:HIST, :] = jnp.where(has_prev, zprev, 0.0)
    extz_ref[HIST:HIST + tm, :] = z
    y = extz_ref[HIST - 2:HIST - 2 + tm, :] * wconv_ref[0:1, :]
    y = y + extz_ref[HIST - 1:HIST - 1 + tm, :] * wconv_ref[1:2, :]
    y = y + extz_ref[HIST:HIST + tm, :] * wconv_ref[2:3, :]
    br_ref[:, D_POOL:D_POOL + D_CONV] = (cur_ref[:, C_B:C_B + D_CONV] * y).astype(BF16)
    zst_ref[0] = extz_ref[HIST + tm - 8:HIST + tm, :]

    for h in range(N_XHEADS):
        sl = slice(h * XHEAD_DIM, (h + 1) * XHEAD_DIM)
        qh = cur_ref[:, C_Q + h * XHEAD_DIM:C_Q + (h + 1) * XHEAD_DIM].astype(BF16)
        kh = mk_ref[0, :, sl].astype(BF16)
        vh = mv_ref[0, :, sl].astype(BF16)
        s = lax.dot_general(qh, kh, (((1,), (1,)), ((), ())), preferred_element_type=F32)
        p = _softmax_rows(s * (XHEAD_DIM ** -0.5))
        o = jnp.dot(p.astype(BF16), vh, preferred_element_type=F32)
        c0 = D_POOL + D_CONV + h * XHEAD_DIM
        br_ref[:, c0:c0 + XHEAD_DIM] = o.astype(BF16)


def _mix_prompt(proj, mk, mv, wpg, scale, wconv, batch, seq):
    tm = MIX_TM
    nt = seq // tm
    assert seq % tm == 0 and tm % HIST == 0
    rpb = tm // HIST
    return pl.pallas_call(
        _mix_prompt_kernel,
        grid=(batch, nt),
        in_specs=[
            pl.BlockSpec((tm, D_MIX), lambda b, t: (b * nt + t, 0)),
            pl.BlockSpec((HIST, D_MIX), lambda b, t: (jnp.maximum((b * nt + t) * rpb - 1, 0), 0)),
            pl.BlockSpec((1, N_MEM, D_XATTN), lambda b, t: (b, 0, 0)),
            pl.BlockSpec((1, N_MEM, D_XATTN), lambda b, t: (b, 0, 0)),
            pl.BlockSpec((len(POOL_WINDOWS), POOL_GROUP_DIM, POOL_GROUP_DIM), lambda b, t: (0, 0, 0)),
            pl.BlockSpec((1, D_POOL), lambda b, t: (0, 0)),
            pl.BlockSpec((CONV_WIDTH, D_CONV), lambda b, t: (0, 0)),
        ],
        out_specs=[
            pl.BlockSpec((tm, D_MODEL), lambda b, t: (b * nt + t, 0)),
            pl.BlockSpec((1, 8, D_CONV), lambda b, t: (b, 0, 0)),
        ],
        out_shape=[
            jax.ShapeDtypeStruct((batch * seq, D_MODEL), BF16),
            jax.ShapeDtypeStruct((batch, 8, D_CONV), F32),
        ],
        scratch_shapes=[pltpu.VMEM((HIST + tm, D_POOL), F32), pltpu.VMEM((HIST + tm, D_CONV), F32)],
        compiler_params=pltpu.CompilerParams(
            dimension_semantics=("arbitrary", "arbitrary"), vmem_limit_bytes=VMEM_LIMIT),
        name="mix_prompt",
    )(proj, proj, mk, mv, wpg, scale.reshape(1, D_POOL), wconv)


def _mix_sample_kernel(cur_ref, sp_ref, sc_ref, k_ref, v_ref, wpg_ref, scale_ref, wconv_ref,
                       br_ref, zst_ref, extu_ref, extz_ref):
    ns, ln = zst_ref.shape[0], zst_ref.shape[1]
    rows = ns * ln

    extu_ref[:, HIST - POOL_STATE_LEN:HIST, :] = sp_ref[...]
    extu_ref[:, HIST:HIST + ln, :] = cur_ref[:, C_U:C_U + D_POOL].reshape(ns, ln, D_POOL)
    for g, w in enumerate(POOL_WINDOWS):
        sl = slice(g * POOL_GROUP_DIM, (g + 1) * POOL_GROUP_DIM)
        s = extu_ref[:, HIST:HIST + ln, sl]
        for k in range(1, w):
            s = s + extu_ref[:, HIST - k:HIST - k + ln, sl]
        pooled = s / float(w) - extu_ref[:, HIST:HIST + ln, sl]
        pooled = pooled.reshape(rows, POOL_GROUP_DIM)
        br_ref[:, sl] = _pool_project(pooled, wpg_ref, scale_ref, g).astype(BF16)

    z = cur_ref[:, C_C:C_C + D_CONV] * cur_ref[:, C_V:C_V + D_CONV]
    extz_ref[:, HIST - 2:HIST, :] = sc_ref[...]
    extz_ref[:, HIST:HIST + ln, :] = z.reshape(ns, ln, D_CONV)
    y = extz_ref[:, HIST - 2:HIST - 2 + ln, :] * wconv_ref[0:1, :]
    y = y + extz_ref[:, HIST - 1:HIST - 1 + ln, :] * wconv_ref[1:2, :]
    y = y + extz_ref[:, HIST:HIST + ln, :] * wconv_ref[2:3, :]
    br_ref[:, D_POOL:D_POOL + D_CONV] = (
        cur_ref[:, C_B:C_B + D_CONV] * y.reshape(rows, D_CONV)).astype(BF16)
    zst_ref[...] = extz_ref[:, HIST:HIST + ln, :]

    q3 = cur_ref[:, C_Q:C_Q + D_XATTN].reshape(ns, ln, D_XATTN)
    q4 = jnp.concatenate([q3[:, :, h * XHEAD_DIM:(h + 1) * XHEAD_DIM] for h in range(N_XHEADS)], axis=1)
    s = jnp.einsum("nqd,nkd->nqk", q4.astype(BF16), k_ref[...].astype(BF16),
                   preferred_element_type=F32)
    row_head = lax.broadcasted_iota(jnp.int32, (N_XHEADS * ln, N_XHEADS * N_MEM), 0) // ln
    col_head = lax.broadcasted_iota(jnp.int32, (N_XHEADS * ln, N_XHEADS * N_MEM), 1) % N_XHEADS
    s = jnp.where((row_head == col_head)[None], s * (XHEAD_DIM ** -0.5), -jnp.inf)
    p = _softmax_rows(s)
    r = jnp.einsum("nqk,nkd->nqd", p.astype(BF16), v_ref[...].astype(BF16),
                   preferred_element_type=F32)
    for h in range(N_XHEADS):
        c0 = D_POOL + D_CONV + h * XHEAD_DIM
        br_ref[:, c0:c0 + XHEAD_DIM] = r[:, h * ln:(h + 1) * ln, :].reshape(rows, XHEAD_DIM).astype(BF16)


def _mix_sample(proj, state_pool, state_conv, mem_k, mem_v, wpg, scale, wconv, nseq, ln):
    ns = MIX_NS
    rows = ns * ln
    assert nseq % ns == 0 and ln == 8
    return pl.pallas_call(
        _mix_sample_kernel,
        grid=(nseq // ns,),
        in_specs=[
            pl.BlockSpec((rows, D_MIX), lambda s: (s, 0)),
            pl.BlockSpec((ns, POOL_STATE_LEN, D_POOL), lambda s: (s, 0, 0)),
            pl.BlockSpec((ns, CONV_WIDTH - 1, D_CONV), lambda s: (s, 0, 0)),
            pl.BlockSpec((ns, N_MEM * N_XHEADS, XHEAD_DIM), lambda s: (s, 0, 0)),
            pl.BlockSpec((ns, N_MEM * N_XHEADS, XHEAD_DIM), lambda s: (s, 0, 0)),
            pl.BlockSpec((len(POOL_WINDOWS), POOL_GROUP_DIM, POOL_GROUP_DIM), lambda s: (0, 0, 0)),
            pl.BlockSpec((1, D_POOL), lambda s: (0, 0)),
            pl.BlockSpec((CONV_WIDTH, D_CONV), lambda s: (0, 0)),
        ],
        out_specs=[
            pl.BlockSpec((rows, D_MODEL), lambda s: (s, 0)),
            pl.BlockSpec((ns, ln, D_CONV), lambda s: (s, 0, 0)),
        ],
        out_shape=[
            jax.ShapeDtypeStruct((nseq * ln, D_MODEL), BF16),
            jax.ShapeDtypeStruct((nseq, ln, D_CONV), F32),
        ],
        scratch_shapes=[pltpu.VMEM((ns, HIST + ln, D_POOL), F32),
                        pltpu.VMEM((ns, HIST + ln, D_CONV), F32)],
        compiler_params=pltpu.CompilerParams(
            dimension_semantics=("arbitrary",), vmem_limit_bytes=VMEM_LIMIT),
        name="mix_sample",
    )(proj, state_pool, state_conv, mem_k, mem_v, wpg, scale.reshape(1, D_POOL), wconv)


def _merge_route_kernel(x_ref, br_ref, g0_ref, g1_ref, g2_ref, wpo_ref, wco_ref, wao_ref, wo_ref,
                        nf_ref, wrhl_ref, wrh_ref, brt_ref,
                        x1_ref, h2_ref, ri_ref, rw_ref, cnt_ref, carry_ref):
    i = pl.program_id(0)
    tm = x_ref.shape[0]

    @pl.when(i == 0)
    def _():
        carry_ref[...] = jnp.zeros_like(carry_ref)

    merged = _sigmoid(g0_ref[...].astype(F32)) * jnp.dot(
        br_ref[:, 0:D_POOL], wpo_ref[...], preferred_element_type=F32)
    merged = merged + _sigmoid(g1_ref[...].astype(F32)) * jnp.dot(
        br_ref[:, D_POOL:D_POOL + D_CONV], wco_ref[...], preferred_element_type=F32)
    merged = merged + _sigmoid(g2_ref[...].astype(F32)) * jnp.dot(
        br_ref[:, D_POOL + D_CONV:D_MODEL], wao_ref[...], preferred_element_type=F32)
    x1 = x_ref[...] + jnp.dot(merged.astype(BF16), wo_ref[...], preferred_element_type=F32)
    x1_ref[...] = x1
    h2 = _rms(x1, nf_ref[...])
    h2_ref[...] = _pack_bf16_pairs(h2)

    h2_hi = h2.astype(BF16)
    h2_lo = (h2 - h2_hi.astype(F32)).astype(BF16)
    hi_part = jnp.dot(h2_hi, wrhl_ref[...], preferred_element_type=F32)
    logits = (hi_part[:, 0:N_EXPERTS] + hi_part[:, N_EXPERTS:2 * N_EXPERTS]
              + jnp.dot(h2_lo, wrh_ref[...], preferred_element_type=F32)) + brt_ref[...]
    lane = lax.broadcasted_iota(jnp.int32, (tm, N_EXPERTS), 1).astype(F32)
    vals, idxs, hots = [], [], []
    work = logits
    for _ in range(TOP_K):
        m = jnp.max(work, axis=-1, keepdims=True)
        idx = jnp.min(jnp.where(work == m, lane, float(N_EXPERTS)), axis=-1, keepdims=True)
        hot = lane == idx
        work = jnp.where(hot, -jnp.inf, work)
        vals.append(m)
        idxs.append(idx)
        hots.append(hot)
    es = [jnp.exp(v - vals[0]) for v in vals]
    denom = es[0] + es[1] + es[2] + es[3]

    chosen = jnp.where(hots[0] | hots[1] | hots[2] | hots[3], 1.0, 0.0).astype(BF16)
    r_i = lax.broadcasted_iota(jnp.int32, (tm, tm), 0)
    c_i = lax.broadcasted_iota(jnp.int32, (tm, tm), 1)
    lower = jnp.where(c_i < r_i, 1.0, 0.0).astype(BF16)
    before = jnp.dot(lower, chosen, preferred_element_type=F32) + carry_ref[0:1, 0:N_EXPERTS]
    carry_ref[0:1, 0:N_EXPERTS] = (carry_ref[0:1, 0:N_EXPERTS]
                                   + jnp.sum(chosen.astype(F32), axis=0, keepdims=True))
    cnt_ref[...] = carry_ref[...]

    out_lane = lax.broadcasted_iota(jnp.int32, (tm, LANES), 1)
    ri = jnp.zeros((tm, LANES), jnp.int32)
    rw = jnp.zeros((tm, LANES), F32)
    for k in range(TOP_K):
        rank = jnp.sum(jnp.where(hots[k], before, 0.0), axis=-1, keepdims=True).astype(jnp.int32)
        ri = jnp.where(out_lane == k, idxs[k].astype(jnp.int32), ri)
        ri = jnp.where(out_lane == TOP_K + k, rank, ri)
        rw = jnp.where(out_lane == k, es[k] / denom, rw)
    ri_ref[...] = ri
    rw_ref[...] = rw


def _merge_route(x, branch, gates, wpo, wco, wao, wo, norm_ffn, w_router, b_router, name):
    t = x.shape[0]
    tm = MERGE_TM
    assert t % tm == 0 and gates.shape == (t, N_BRANCH * D_MODEL)
    const = lambda i: (0, 0)
    wr_hi = w_router.astype(BF16)
    wr_lo = (w_router - wr_hi.astype(F32)).astype(BF16)
    return pl.pallas_call(
        _merge_route_kernel,
        grid=(t // tm,),
        in_specs=[
            pl.BlockSpec((tm, D_MODEL), lambda i: (i, 0)),
            pl.BlockSpec((tm, D_MODEL), lambda i: (i, 0)),
            pl.BlockSpec((tm, D_MODEL), lambda i: (i, 0)),
            pl.BlockSpec((tm, D_MODEL), lambda i: (i, 1)),
            pl.BlockSpec((tm, D_MODEL), lambda i: (i, 2)),
            pl.BlockSpec((D_POOL, D_MODEL), const, pipeline_mode=pl.Buffered(1)),
            pl.BlockSpec((D_CONV, D_MODEL), const, pipeline_mode=pl.Buffered(1)),
            pl.BlockSpec((D_XATTN, D_MODEL), const, pipeline_mode=pl.Buffered(1)),
            pl.BlockSpec((D_MODEL, D_MODEL), const, pipeline_mode=pl.Buffered(1)),
            pl.BlockSpec((1, D_MODEL), const),
            pl.BlockSpec((D_MODEL, 2 * N_EXPERTS), const),
            pl.BlockSpec((D_MODEL, N_EXPERTS), const),
            pl.BlockSpec((1, N_EXPERTS), const),
        ],
        out_specs=[
            pl.BlockSpec((tm, D_MODEL), lambda i: (i, 0)),
            pl.BlockSpec((tm, D_MODEL // 2), lambda i: (i, 0)),
            pl.BlockSpec((tm, LANES), lambda i: (i, 0)),
            pl.BlockSpec((tm, LANES), lambda i: (i, 0)),
            pl.BlockSpec((8, LANES), const),
        ],
        out_shape=[
            jax.ShapeDtypeStruct((t, D_MODEL), F32),
            jax.ShapeDtypeStruct((t, D_MODEL // 2), PACKED),
            jax.ShapeDtypeStruct((t, LANES), jnp.int32),
            jax.ShapeDtypeStruct((t, LANES), F32),
            jax.ShapeDtypeStruct((8, LANES), F32),
        ],
        scratch_shapes=[pltpu.VMEM((8, LANES), F32)],
        compiler_params=pltpu.CompilerParams(
            dimension_semantics=("arbitrary",), vmem_limit_bytes=VMEM_LIMIT),
        name=name,
    )(x, branch, gates, gates, gates, wpo, wco, wao, wo, norm_ffn.reshape(1, D_MODEL),
      jnp.concatenate([wr_hi, wr_lo], axis=1), wr_hi, b_router.reshape(1, N_EXPERTS))


def _fill_work_list(nblk_s, pad_s, nj, refs):
    te, tj, trow, tnb, tfirst, tne, tnj, tpar, ttot = refs

    def clear(i, c):
        for r in (te, tj, trow, tnb, tfirst, tpar):
            r[i] = 0
        tne[i] = -1
        tnj[i] = -1
        return c

    lax.fori_loop(0, te.shape[0], clear, 0)

    carry = (jnp.int32(0), jnp.int32(-1), jnp.int32(0))
    for j in range(nj):
        def per_expert(e, carry):
            q, prev, groups = carry
            n = nblk_s[e]
            nch = (n + EXP_CHUNK - 1) // EXP_CHUNK

            @pl.when((nch > 0) & (prev >= 0))
            def _():
                tne[prev] = e
                tnj[prev] = jnp.int32(j)

            def per_chunk(c, q):
                te[q] = e
                tj[q] = jnp.int32(j)
                trow[q] = pad_s[e] + c * (EXP_CHUNK * EXP_TM)
                tnb[q] = jnp.minimum(EXP_CHUNK, n - c * EXP_CHUNK)
                tfirst[q] = (c == 0).astype(jnp.int32)
                tpar[q] = lax.rem(groups, 2)
                return q + 1

            return (lax.fori_loop(0, nch, per_chunk, q), jnp.where(nch > 0, q, prev),
                    groups + (nch > 0).astype(jnp.int32))

        carry = lax.fori_loop(0, N_EXPERTS, per_expert, carry)
    ttot[0] = carry[0]


def _route_tables_kernel(cntp_ref, cnts_ref, rip_ref, ris_ref, dest_ref, *refs, nj_in, nj_out):
    tabs_in, tabs_out = refs[0:N_TABS], refs[N_TABS:2 * N_TABS]
    nblk_s, pad_s = refs[2 * N_TABS:]

    def per_expert(e, start_blk):
        n = (cntp_ref[e] + cnts_ref[e] + EXP_TM - 1) // EXP_TM
        nblk_s[e] = n
        pad_s[e] = start_blk * EXP_TM
        return start_blk + n

    lax.fori_loop(0, N_EXPERTS, per_expert, jnp.int32(0))
    _fill_work_list(nblk_s, pad_s, nj_in, tabs_in)
    _fill_work_list(nblk_s, pad_s, nj_out, tabs_out)

    tile = ROUTE_TM
    col0 = 0
    for ri_ref, is_sample in ((rip_ref, False), (ris_ref, True)):
        for r in range(ri_ref.shape[0] // tile):
            ri = ri_ref[r * tile:(r + 1) * tile, :]
            base = jnp.zeros_like(ri)
            for e in range(N_EXPERTS):
                first_row = pad_s[e] + cntp_ref[e] if is_sample else pad_s[e]
                base = jnp.where(ri == e, first_row, base)
            dest = base + pltpu.roll(ri, LANES - TOP_K, axis=1)
            dest_ref[:, col0:col0 + tile] = jnp.transpose(dest)[0:8, :]
            col0 += tile


def _route_tables(cnt_p, cnt_s, ri_p, ri_s, nj_in, nj_out, max_items):
    t_all = ri_p.shape[0] + ri_s.shape[0]
    assert ri_p.shape[0] % ROUTE_TM == 0 and ri_s.shape[0] % ROUTE_TM == 0
    smem = pl.BlockSpec(memory_space=pltpu.SMEM)

    def tables(nj):
        n = nj * max_items
        return [jax.ShapeDtypeStruct((n,), jnp.int32)] * (N_TABS - 1) + [jax.ShapeDtypeStruct((1,), jnp.int32)]

    out = pl.pallas_call(
        functools.partial(_route_tables_kernel, nj_in=nj_in, nj_out=nj_out),
        grid=(1,),
        in_specs=[smem, smem,
                  pl.BlockSpec(ri_p.shape, lambda i: (0, 0)), pl.BlockSpec(ri_s.shape, lambda i: (0, 0))],
        out_specs=[pl.BlockSpec((8, t_all), lambda i: (0, 0))] + [smem] * (2 * N_TABS),
        out_shape=[jax.ShapeDtypeStruct((8, t_all), jnp.int32)] + tables(nj_in) + tables(nj_out),
        scratch_shapes=[pltpu.SMEM((N_EXPERTS,), jnp.int32), pltpu.SMEM((N_EXPERTS,), jnp.int32)],
        compiler_params=pltpu.CompilerParams(
            dimension_semantics=("arbitrary",), vmem_limit_bytes=VMEM_LIMIT),
        name="route_tables",
    )(cnt_p, cnt_s, ri_p, ri_s)
    return out[0], tuple(out[1:1 + N_TABS]), tuple(out[1 + N_TABS:1 + 2 * N_TABS])


def _sc_mesh():
    return plsc.VectorSubcoreMesh(core_axis_name="c", subcore_axis_name="s",
                                  num_cores=SC_CORES, num_subcores=SC_SUBCORES)


def _sc_worker_id():
    return lax.axis_index("s") * SC_CORES + lax.axis_index("c")


def _sc_dispatch(h2_a, h2_b, dest_by_slot, rows_out):
    ta, w = h2_a.shape
    t = ta + h2_b.shape[0]
    per_w = t // SC_WORKERS
    n_chunks = per_w // SC_CHUNK
    assert t == SC_WORKERS * n_chunks * SC_CHUNK and ta % SC_CHUNK == 0
    idx = dest_by_slot[:TOP_K].reshape(TOP_K, SC_WORKERS, n_chunks, SC_CHUNK).transpose(1, 0, 2, 3)

    def body(a_hbm, b_hbm, idx_hbm, xs_hbm, idx_v, rows_v, sems):
        base = _sc_worker_id() * per_w
        pltpu.sync_copy(idx_hbm.at[_sc_worker_id()], idx_v)
        pending = [[], []]
        for c in range(n_chunks):
            b = c % 2
            for d in pending[b]:
                d.wait()
            tok0 = base + c * SC_CHUNK

            @pl.when(tok0 < ta)
            def _():
                pltpu.sync_copy(a_hbm.at[pl.ds(tok0, SC_CHUNK)], rows_v.at[b])

            @pl.when(tok0 >= ta)
            def _():
                pltpu.sync_copy(b_hbm.at[pl.ds(tok0 - ta, SC_CHUNK)], rows_v.at[b])

            pending[b] = [pltpu.async_copy(rows_v.at[b], xs_hbm.at[idx_v.at[k, c]], sems.at[b])
                          for k in range(TOP_K)]
        for b in range(2):
            for d in pending[b]:
                d.wait()

    return pl.kernel(
        body,
        out_type=jax.ShapeDtypeStruct((rows_out, w), h2_a.dtype),
        mesh=_sc_mesh(),
        scratch_types=[pltpu.VMEM((TOP_K, n_chunks, SC_CHUNK), jnp.int32),
                       pltpu.VMEM((2, SC_CHUNK, w), h2_a.dtype),
                       pltpu.SemaphoreType.DMA((2,))],
        name="sc_dispatch",
    )(h2_a, h2_b, idx)


def _sc_gather_rows(table, idx):
    n = idx.shape[0]
    w = table.shape[1]
    per_w = n // SC_WORKERS
    n_chunks = per_w // SC_CHUNK
    assert n == SC_WORKERS * n_chunks * SC_CHUNK
    idx3 = idx.reshape(SC_WORKERS, n_chunks, SC_CHUNK)

    def body(table_hbm, idx_hbm, out_hbm, idx_v, rows_v, gsem, wsems):
        base = _sc_worker_id() * per_w
        pltpu.sync_copy(idx_hbm.at[_sc_worker_id()], idx_v)
        pending = [None, None]
        for c in range(n_chunks):
            b = c % 2
            if pending[b] is not None:
                pending[b].wait()
            pltpu.async_copy(table_hbm.at[idx_v.at[c]], rows_v.at[b], gsem).wait()
            pending[b] = pltpu.async_copy(
                rows_v.at[b], out_hbm.at[pl.ds(base + c * SC_CHUNK, SC_CHUNK)], wsems.at[b])
        for b in range(2):
            if pending[b] is not None:
                pending[b].wait()

    return pl.kernel(
        body,
        out_type=jax.ShapeDtypeStruct((n, w), table.dtype),
        mesh=_sc_mesh(),
        scratch_types=[pltpu.VMEM((n_chunks, SC_CHUNK), jnp.int32),
                       pltpu.VMEM((2, SC_CHUNK, w), table.dtype),
                       pltpu.SemaphoreType.DMA,
                       pltpu.SemaphoreType.DMA((2,))],
        name="sc_gather_rows",
    )(table, idx3)


def _grouped_pipeline(tabs, total_ref, weight_copies, in_copy, out_copy, compute):
    te, tj, _, tnb, tfirst, tne, tnj, tpar = tabs
    total = total_ref[0]

    def each_block(q, copy, op):
        slot = lax.rem(q, 2)
        for m in range(1, EXP_CHUNK + 1):
            @pl.when(tnb[q] == m)
            def _():
                op(copy(q, slot, m))

    def start(d):
        d.start()

    def wait(d):
        d.wait()

    for d in weight_copies(te[0], tj[0], tpar[0]):
        d.start()
    each_block(0, in_copy, start)

    def body(q, carry):
        each_block(q, in_copy, wait)

        @pl.when(q + 1 < total)
        def _():
            each_block(q + 1, in_copy, start)

        @pl.when(tfirst[q] == 1)
        def _():
            for d in weight_copies(te[q], tj[q], tpar[q]):
                d.wait()

            @pl.when(tne[q] >= 0)
            def _():
                for d in weight_copies(tne[q], tnj[q], 1 - tpar[q]):
                    d.start()

        @pl.when(q >= 2)
        def _():
            each_block(q - 2, out_copy, wait)

        for m in range(1, EXP_CHUNK + 1):
            @pl.when(tnb[q] == m)
            def _():
                compute(q, lax.rem(q, 2), m, tpar[q])

        each_block(q, out_copy, start)
        return carry

    lax.fori_loop(0, total, body, 0)

    @pl.when(total >= 2)
    def _():
        each_block(total - 2, out_copy, wait)

    each_block(total - 1, out_copy, wait)


def _expert_in_kernel(*refs):
    tabs, total_ref = refs[:N_TABS - 1], refs[N_TABS - 1]
    xs_hbm, w_hbm, b_ref, act_hbm, wbuf, x_buf, a_buf, w_sem, x_sem, a_sem = refs[N_TABS:]
    te, tj, trow = tabs[:3]
    tm, tn = EXP_TM, a_buf.shape[2]
    nj = D_FF // tn

    def weight_copies(e, j, par):
        return [pltpu.make_async_copy(
            w_hbm.at[e, :, pl.ds(pl.multiple_of(half * D_FF + j * tn, tn), tn)],
            wbuf.at[2 * par + half], w_sem)
            for half in range(2)]

    def rows(q, m):
        return pl.ds(pl.multiple_of(trow[q], tm), m * tm)

    def x_copy(q, slot, m):
        return pltpu.make_async_copy(xs_hbm.at[rows(q, m)], x_buf.at[slot, pl.ds(0, m * tm)], x_sem.at[slot])

    def a_copy(q, slot, m):
        return pltpu.make_async_copy(
            a_buf.at[slot, pl.ds(0, m * tm)],
            act_hbm.at[rows(q, m), pl.ds(pl.multiple_of(tj[q] * tn, tn), tn)], a_sem.at[slot])

    def compute(q, slot, m, par):
        bias0 = te[q] * (2 * nj) + tj[q]
        x = jnp.concatenate(_unpack_bf16_pairs(x_buf[slot, 0:m * tm]), axis=1).astype(BF16)
        g = jnp.dot(x, wbuf[2 * par].astype(BF16), preferred_element_type=F32) + b_ref[bias0]
        up = jnp.dot(x, wbuf[2 * par + 1].astype(BF16), preferred_element_type=F32) + b_ref[bias0 + nj]
        g = jnp.minimum(g, SWIGLU_LIMIT)
        up = jnp.clip(up, -SWIGLU_LIMIT, SWIGLU_LIMIT)
        a_buf[slot, 0:m * tm] = (g * _sigmoid(SWIGLU_ALPHA * g) * (up + 1.0)).astype(BF16)

    _grouped_pipeline(tabs, total_ref, weight_copies, x_copy, a_copy, compute)


def _expert_in(tabs, xs, w_in, b_in):
    rows = xs.shape[0]
    tm, tn = EXP_TM, EXP_IN_TN
    nj = D_FF // tn
    bias = b_in.reshape(N_EXPERTS * 2 * nj, 1, tn)
    grid_spec = pltpu.PrefetchScalarGridSpec(
        num_scalar_prefetch=len(tabs),
        grid=(1,),
        in_specs=[
            pl.BlockSpec(memory_space=pl.ANY),
            pl.BlockSpec(memory_space=pl.ANY),
            pl.BlockSpec(bias.shape, lambda i, *_: (0, 0, 0), pipeline_mode=pl.Buffered(1)),
        ],
        out_specs=pl.BlockSpec(memory_space=pl.ANY),
        scratch_shapes=[pltpu.VMEM((4, D_MODEL, tn), F32),
                        pltpu.VMEM((2, EXP_CHUNK * tm, D_MODEL // 2), PACKED),
                        pltpu.VMEM((2, EXP_CHUNK * tm, tn), BF16),
                        pltpu.SemaphoreType.DMA, pltpu.SemaphoreType.DMA((2,)),
                        pltpu.SemaphoreType.DMA((2,))],
    )
    return pl.pallas_call(
        _expert_in_kernel,
        grid_spec=grid_spec,
        out_shape=jax.ShapeDtypeStruct((rows, D_FF), BF16),
        compiler_params=pltpu.CompilerParams(
            dimension_semantics=("arbitrary",), vmem_limit_bytes=VMEM_LIMIT),
        name="expert_in",
    )(*tabs, xs, w_in, bias)


def _expert_out_kernel(*refs):
    tabs, total_ref = refs[:N_TABS - 1], refs[N_TABS - 1]
    act_hbm, w_hbm, b_ref, ys_hbm, wbuf, a_buf, y_buf, w_sem, a_sem, y_sem = refs[N_TABS:]
    te, tj, trow = tabs[:3]
    tm, tnp = EXP_TM, y_buf.shape[2]
    tn = 2 * tnp
    nj = D_MODEL // tn

    def weight_copies(e, j, par):
        return [pltpu.make_async_copy(
            w_hbm.at[e, :, pl.ds(pl.multiple_of(j * tn, tn), tn)], wbuf.at[par], w_sem)]

    def rows(q, m):
        return pl.ds(pl.multiple_of(trow[q], tm), m * tm)

    def a_copy(q, slot, m):
        return pltpu.make_async_copy(act_hbm.at[rows(q, m)], a_buf.at[slot, pl.ds(0, m * tm)], a_sem.at[slot])

    def y_copy(q, slot, m):
        return pltpu.make_async_copy(
            y_buf.at[slot, pl.ds(0, m * tm)],
            ys_hbm.at[rows(q, m), pl.ds(pl.multiple_of(tj[q] * tnp, tnp), tnp)], y_sem.at[slot])

    def compute(q, slot, m, par):
        y = jnp.dot(a_buf[slot, 0:m * tm], wbuf[par].astype(BF16),
                    preferred_element_type=F32) + b_ref[te[q] * nj + tj[q]]
        y_buf[slot, 0:m * tm] = _pack_bf16_pairs(y)

    _grouped_pipeline(tabs, total_ref, weight_copies, a_copy, y_copy, compute)


def _expert_out(tabs, act, w_out, b_out):
    rows = act.shape[0]
    tm, tn = EXP_TM, EXP_OUT_TN
    nj = D_MODEL // tn
    bias = b_out.reshape(N_EXPERTS * nj, 1, tn)
    grid_spec = pltpu.PrefetchScalarGridSpec(
        num_scalar_prefetch=len(tabs),
        grid=(1,),
        in_specs=[
            pl.BlockSpec(memory_space=pl.ANY),
            pl.BlockSpec(memory_space=pl.ANY),
            pl.BlockSpec(bias.shape, lambda i, *_: (0, 0, 0), pipeline_mode=pl.Buffered(1)),
        ],
        out_specs=pl.BlockSpec(memory_space=pl.ANY),
        scratch_shapes=[pltpu.VMEM((2, D_FF, tn), F32),
                        pltpu.VMEM((2, EXP_CHUNK * tm, D_FF), BF16),
                        pltpu.VMEM((2, EXP_CHUNK * tm, tn // 2), PACKED),
                        pltpu.SemaphoreType.DMA, pltpu.SemaphoreType.DMA((2,)),
                        pltpu.SemaphoreType.DMA((2,))],
    )
    return pl.pallas_call(
        _expert_out_kernel,
        grid_spec=grid_spec,
        out_shape=jax.ShapeDtypeStruct((rows, D_MODEL // 2), PACKED),
        compiler_params=pltpu.CompilerParams(
            dimension_semantics=("arbitrary",), vmem_limit_bytes=VMEM_LIMIT),
        name="expert_out",
    )(*tabs, act, w_out, bias)


def _combine_kernel(x1_ref, yg_ref, rw_ref, g_ref, o_ref):
    half = EXP_OUT_TN // 2
    pieces = []
    ssq = jnp.zeros((x1_ref.shape[0], 1), F32)
    for jt in range(D_MODEL // EXP_OUT_TN):
        acc_lo = x1_ref[:, jt * EXP_OUT_TN:jt * EXP_OUT_TN + half]
        acc_hi = x1_ref[:, jt * EXP_OUT_TN + half:(jt + 1) * EXP_OUT_TN]
        moe_lo = jnp.zeros_like(acc_lo)
        moe_hi = jnp.zeros_like(acc_hi)
        for k in range(TOP_K):
            lo, hi = _unpack_bf16_pairs(yg_ref[k, :, jt * half:(jt + 1) * half])
            moe_lo = moe_lo + lo * rw_ref[:, k:k + 1]
            moe_hi = moe_hi + hi * rw_ref[:, k:k + 1]
        for acc in (acc_lo + moe_lo, acc_hi + moe_hi):
            ssq = ssq + jnp.sum(acc * acc, axis=-1, keepdims=True)
            pieces.append(acc)
    inv = lax.rsqrt(ssq / D_MODEL + EPS)
    for n, acc in enumerate(pieces):
        o_ref[:, n * half:(n + 1) * half] = acc * inv * g_ref[:, n * half:(n + 1) * half]


def _combine_part_kernel(x1_ref, yg_ref, rw_ref, g_ref, prev_hbm, o_ref):
    del prev_hbm
    _combine_kernel(x1_ref, yg_ref, rw_ref, g_ref, o_ref)


def _combine(x1, yg, rw, gain, row0, prev, name):
    t = x1.shape[0]
    n = yg.shape[1]
    tm = COMB_TM
    assert n % tm == 0 and row0 % tm == 0
    off = row0 // tm
    in_specs = [
        pl.BlockSpec((tm, D_MODEL), lambda i: (i + off, 0)),
        pl.BlockSpec((TOP_K, tm, D_MODEL // 2), lambda i: (0, i, 0)),
        pl.BlockSpec((tm, LANES), lambda i: (i + off, 0)),
        pl.BlockSpec((1, D_MODEL), lambda i: (0, 0)),
    ]
    args = [x1, yg, rw, gain.reshape(1, D_MODEL)]
    if prev is not None:
        in_specs.append(pl.BlockSpec(memory_space=pl.ANY))
        args.append(prev)
    return pl.pallas_call(
        _combine_kernel if prev is None else _combine_part_kernel,
        grid=(n // tm,),
        in_specs=in_specs,
        out_specs=pl.BlockSpec((tm, D_MODEL), lambda i: (i + off, 0)),
        out_shape=jax.ShapeDtypeStruct((t, D_MODEL), F32),
        input_output_aliases={} if prev is None else {len(args) - 1: 0},
        compiler_params=pltpu.CompilerParams(
            dimension_semantics=("arbitrary",), vmem_limit_bytes=VMEM_LIMIT),
        name=name,
    )(*args)


def kernel(x_prompt, x_sample, state_pool, state_conv, cache_mem_k, cache_mem_v, mem_prompt,
           norm_mix, w_in, b_gate, w_pool_group, pool_scale, w_conv, mem_norm, w_mem_kv,
           w_pool_out, w_conv_out, w_attn_out, w_o, norm_ffn, w_router, b_router,
           w_exp_in, b_exp_in, w_exp_out, b_exp_out, final_norm):
    depth = norm_mix.shape[0]
    assert depth == 1
    l = 0
    bp, seq, _ = x_prompt.shape
    bs, ln, _ = x_sample.shape
    tp, ts = bp * seq, bs * ln

    kv = _norm_matmul(mem_prompt.reshape(bp * N_MEM, D_MODEL), mem_norm[l], w_mem_kv[l],
                      jnp.zeros((2 * D_XATTN,), F32), "mem_kv")
    mk = kv[:, :D_XATTN].reshape(bp, N_MEM, D_XATTN)
    mv = kv[:, D_XATTN:].reshape(bp, N_MEM, D_XATTN)

    bias_in = jnp.concatenate([jnp.zeros((D_MIX,), F32), b_gate[l]])
    xp = x_prompt.reshape(tp, D_MODEL)
    xs_ = x_sample.reshape(ts, D_MODEL)
    proj_p, gates_p = _norm_matmul_split(xp, norm_mix[l], w_in[l], bias_in, D_MIX, "proj_prompt")
    proj_s, gates_s = _norm_matmul_split(xs_, norm_mix[l], w_in[l], bias_in, D_MIX, "proj_sample")

    br_p, zst_p = _mix_prompt(proj_p, mk, mv, w_pool_group[l], pool_scale[l], w_conv[l], bp, seq)
    br_s, zst_s = _mix_sample(proj_s, state_pool[l], state_conv[l],
                              cache_mem_k[l].reshape(bs, N_MEM * N_XHEADS, XHEAD_DIM),
                              cache_mem_v[l].reshape(bs, N_MEM * N_XHEADS, XHEAD_DIM),
                              w_pool_group[l], pool_scale[l], w_conv[l], bs, ln)

    wpo, wco, wao, wo = (w.astype(BF16) for w in (w_pool_out[l], w_conv_out[l], w_attn_out[l], w_o[l]))
    x1_p, h2_p, ri_p, rw_p, cnt_p = _merge_route(xp, br_p, gates_p, wpo, wco, wao, wo, norm_ffn[l],
                                                  w_router[l], b_router[l], "merge_route_prompt")
    x1_s, h2_s, ri_s, rw_s, cnt_s = _merge_route(xs_, br_s, gates_s, wpo, wco, wao, wo, norm_ffn[l],
                                                  w_router[l], b_router[l], "merge_route_sample")

    t_all = tp + ts
    n_assign = t_all * TOP_K
    nb_max = n_assign // EXP_TM + N_EXPERTS
    max_items = nb_max // EXP_CHUNK + N_EXPERTS
    dest, tabs_in, tabs_out = _route_tables(
        cnt_p[0, :N_EXPERTS].astype(jnp.int32), cnt_s[0, :N_EXPERTS].astype(jnp.int32), ri_p, ri_s,
        D_FF // EXP_IN_TN, D_MODEL // EXP_OUT_TN, max_items)
    xs_sorted = _sc_dispatch(h2_p, h2_s, dest, nb_max * EXP_TM)

    act = _expert_in(tabs_in, xs_sorted, w_exp_in[l], b_exp_in[l])
    ys = _expert_out(tabs_out, act, w_exp_out[l], b_exp_out[l])
    def combine_part(x1, rw, tok0, n, row0, prev, name):
        idx = dest[:TOP_K, tok0:tok0 + n].reshape(-1)
        yg = _sc_gather_rows(ys, idx).reshape(TOP_K, n, D_MODEL // 2)
        return _combine(x1, yg, rw, final_norm, row0, prev, name)

    y_p = None
    part = tp // COMB_PARTS
    for c in range(COMB_PARTS):
        y_p = combine_part(x1_p, rw_p, c * part, part, c * part, y_p, f"combine_prompt_{c}")
    y_s = combine_part(x1_s, rw_s, tp, ts, 0, None, "combine_sample")

    new_pool_p = proj_p[:, :D_POOL].reshape(bp, seq, D_POOL)[:, seq - POOL_STATE_LEN:]
    new_conv_p = zst_p[:, 8 - (CONV_WIDTH - 1):]
    u_s = proj_s[:, :D_POOL].reshape(bs, ln, D_POOL)
    new_pool_s = jnp.concatenate([state_pool[l], u_s], axis=1)[:, -POOL_STATE_LEN:]
    new_conv_s = zst_s[:, ln - (CONV_WIDTH - 1):]

    return (y_p.reshape(bp, seq, D_MODEL), y_s.reshape(bs, ln, D_MODEL),
            new_pool_p[None], new_conv_p[None],
            mk.reshape(1, bp, N_MEM, N_XHEADS, XHEAD_DIM), mv.reshape(1, bp, N_MEM, N_XHEADS, XHEAD_DIM),
            new_pool_s[None], new_conv_s[None])
```

```python
import functools

import jax
import jax.numpy as jnp
from jax import lax
from jax.experimental import pallas as pl
from jax.experimental.pallas import tpu as pltpu
from jax.experimental.pallas import tpu_sc as plsc

F32 = jnp.float32
BF16 = jnp.bfloat16
PACKED = jnp.int32

D_MODEL = 2048
POOL_WINDOWS = (2, 4, 8, 16)
POOL_GROUP_DIM = 128
D_POOL = 512
POOL_STATE_LEN = 15
D_CONV = 1024
CONV_WIDTH = 3
N_MEM = 256
N_XHEADS = 4
XHEAD_DIM = 128
D_XATTN = 512
N_BRANCH = 3
D_MIX = D_POOL + 3 * D_CONV + D_XATTN
D_IN_TOTAL = D_MIX + N_BRANCH * D_MODEL
N_EXPERTS = 32
TOP_K = 4
D_FF = D_MODEL
SWIGLU_LIMIT = 7.0
SWIGLU_ALPHA = 1.702
EPS = 1e-5

C_U = 0
C_V = D_POOL
C_B = D_POOL + D_CONV
C_C = D_POOL + 2 * D_CONV
C_Q = D_POOL + 3 * D_CONV

LANES = 128
HIST = 16

PROJ_TM = 1024
PROJ_TN = 1024
MIX_TM = 512
MIX_NS = 8
MERGE_TM = 256
ROUTE_TM = 1024
N_TABS = 9
EXP_TM = 128
EXP_CHUNK = 4
EXP_IN_TN = 1024
EXP_OUT_TN = 2048
COMB_TM = 256
COMB_PARTS = 4
SC_CORES = 2
SC_SUBCORES = 16
SC_WORKERS = SC_CORES * SC_SUBCORES
SC_CHUNK = 32
VMEM_LIMIT = 56 * 1024 * 1024


def _sigmoid(x):
    return 0.5 * (jnp.tanh(0.5 * x) + 1.0)


def _rms(x, g):
    ms = jnp.mean(x * x, axis=-1, keepdims=True)
    return x * lax.rsqrt(ms + EPS) * g


def _pack_bf16_pairs(x):
    n = x.shape[1] // 2
    lo = lax.bitcast_convert_type(x[:, :n].astype(BF16).astype(F32), jnp.uint32)
    hi = lax.bitcast_convert_type(x[:, n:].astype(BF16).astype(F32), jnp.uint32)
    return lax.bitcast_convert_type((hi & jnp.uint32(0xFFFF0000)) | (lo >> 16), PACKED)


def _unpack_bf16_pairs(p):
    p = lax.bitcast_convert_type(p, jnp.uint32)
    lo = lax.bitcast_convert_type(p << 16, F32)
    hi = lax.bitcast_convert_type(p & jnp.uint32(0xFFFF0000), F32)
    return lo, hi


def _norm_matmul_kernel(x_ref, g_ref, w_ref, b_ref, o_ref, h_ref):
    @pl.when(pl.program_id(1) == 0)
    def _():
        h_ref[...] = _rms(x_ref[...], g_ref[...]).astype(BF16)

    o_ref[...] = jnp.dot(h_ref[...], w_ref[...].astype(BF16), preferred_element_type=F32) + b_ref[...]


def _norm_matmul(x, gain, w, bias, name):
    t, d = x.shape
    n = w.shape[1]
    tm = min(PROJ_TM, t)
    tn = PROJ_TN
    assert t % tm == 0 and n % tn == 0
    return pl.pallas_call(
        _norm_matmul_kernel,
        grid=(t // tm, n // tn),
        in_specs=[
            pl.BlockSpec((tm, d), lambda i, j: (i, 0)),
            pl.BlockSpec((1, d), lambda i, j: (0, 0)),
            pl.BlockSpec((d, tn), lambda i, j: (0, j)),
            pl.BlockSpec((1, tn), lambda i, j: (0, j)),
        ],
        out_specs=pl.BlockSpec((tm, tn), lambda i, j: (i, j)),
        out_shape=jax.ShapeDtypeStruct((t, n), F32),
        scratch_shapes=[pltpu.VMEM((tm, d), BF16)],
        compiler_params=pltpu.CompilerParams(
            dimension_semantics=("arbitrary", "arbitrary"), vmem_limit_bytes=VMEM_LIMIT),
        name=name,
    )(x, gain.reshape(1, d), w, bias.reshape(1, n))


def _norm_matmul_split_kernel(x_ref, g_ref, w_ref, b_ref, o_ref, og_ref, h_ref, *, n_main):
    j = pl.program_id(1)

    @pl.when(j == 0)
    def _():
        h_ref[...] = _rms(x_ref[...], g_ref[...]).astype(BF16)

    acc = jnp.dot(h_ref[...], w_ref[...].astype(BF16), preferred_element_type=F32) + b_ref[...]

    @pl.when(j < n_main)
    def _():
        o_ref[...] = acc

    @pl.when(j >= n_main)
    def _():
        og_ref[...] = acc.astype(BF16)


def _norm_matmul_split(x, gain, w, bias, split_col, name):
    t, d = x.shape
    n = w.shape[1]
    tm = min(PROJ_TM, t)
    tn = PROJ_TN
    assert t % tm == 0 and n % tn == 0 and split_col % tn == 0 and 0 < split_col < n
    n_main = split_col // tn
    return pl.pallas_call(
        functools.partial(_norm_matmul_split_kernel, n_main=n_main),
        grid=(t // tm, n // tn),
        in_specs=[
            pl.BlockSpec((tm, d), lambda i, j: (i, 0)),
            pl.BlockSpec((1, d), lambda i, j: (0, 0)),
            pl.BlockSpec((d, tn), lambda i, j: (0, j)),
            pl.BlockSpec((1, tn), lambda i, j: (0, j)),
        ],
        out_specs=[
            pl.BlockSpec((tm, tn), lambda i, j: (i, jnp.minimum(j, n_main - 1))),
            pl.BlockSpec((tm, tn), lambda i, j: (i, jnp.maximum(j - n_main, 0))),
        ],
        out_shape=[jax.ShapeDtypeStruct((t, split_col), F32),
                   jax.ShapeDtypeStruct((t, n - split_col), BF16)],
        scratch_shapes=[pltpu.VMEM((tm, d), BF16)],
        compiler_params=pltpu.CompilerParams(
            dimension_semantics=("arbitrary", "arbitrary"), vmem_limit_bytes=VMEM_LIMIT),
        name=name,
    )(x, gain.reshape(1, d), w, bias.reshape(1, n))


def _pool_project(pooled, wpg_ref, scale_ref, g):
    sl = slice(g * POOL_GROUP_DIM, (g + 1) * POOL_GROUP_DIM)
    y = jnp.dot(pooled.astype(BF16), wpg_ref[g].astype(BF16), preferred_element_type=F32)
    return y * scale_ref[:, sl]


def _softmax_rows(s):
    m = jnp.max(s, axis=-1, keepdims=True)
    e = jnp.exp(s - m)
    return e / jnp.sum(e, axis=-1, keepdims=True)


def _mix_prompt_kernel(cur_ref, prev_ref, mk_ref, mv_ref, wpg_ref, scale_ref, wconv_ref,
                       br_ref, zst_ref, extu_ref, extz_ref):
    t = pl.program_id(1)
    tm = cur_ref.shape[0]
    has_prev = t > 0

    u = cur_ref[:, C_U:C_U + D_POOL]
    extu_ref[0:HIST, :] = jnp.where(has_prev, prev_ref[:, C_U:C_U + D_POOL], 0.0)
    extu_ref[HIST:HIST + tm, :] = u
    pos = t * tm + lax.broadcasted_iota(jnp.int32, (tm, 1), 0)
    for g, w in enumerate(POOL_WINDOWS):
        sl = slice(g * POOL_GROUP_DIM, (g + 1) * POOL_GROUP_DIM)
        s = extu_ref[HIST:HIST + tm, sl]
        for k in range(1, w):
            s = s + extu_ref[HIST - k:HIST - k + tm, sl]
        cnt = jnp.minimum(w, pos + 1).astype(F32)
        pooled = s / cnt - extu_ref[HIST:HIST + tm, sl]
        br_ref[:, sl] = _pool_project(pooled, wpg_ref, scale_ref, g).astype(BF16)

    z = cur_ref[:, C_C:C_C + D_CONV] * cur_ref[:, C_V:C_V + D_CONV]
    zprev = prev_ref[:, C_C:C_C + D_CONV] * prev_ref[:, C_V:C_V + D_CONV]
    extz_ref[0:HIST, :] = jnp.where(has_prev, zprev, 0.0)
    extz_ref[HIST:HIST + tm, :] = z
    y = extz_ref[HIST - 2:HIST - 2 + tm, :] * wconv_ref[0:1, :]
    y = y + extz_ref[HIST - 1:HIST - 1 + tm, :] * wconv_ref[1:2, :]
    y = y + extz_ref[HIST:HIST + tm, :] * wconv_ref[2:3, :]
    br_ref[:, D_POOL:D_POOL + D_CONV] = (cur_ref[:, C_B:C_B + D_CONV] * y).astype(BF16)
    zst_ref[0] = extz_ref[HIST + tm - 8:HIST + tm, :]

    for h in range(N_XHEADS):
        sl = slice(h * XHEAD_DIM, (h + 1) * XHEAD_DIM)
        qh = cur_ref[:, C_Q + h * XHEAD_DIM:C_Q + (h + 1) * XHEAD_DIM].astype(BF16)
        kh = mk_ref[0, :, sl].astype(BF16)
        vh = mv_ref[0, :, sl].astype(BF16)
        s = lax.dot_general(qh, kh, (((1,), (1,)), ((), ())), preferred_element_type=F32)
        p = _softmax_rows(s * (XHEAD_DIM ** -0.5))
        o = jnp.dot(p.astype(BF16), vh, preferred_element_type=F32)
        c0 = D_POOL + D_CONV + h * XHEAD_DIM
        br_ref[:, c0:c0 + XHEAD_DIM] = o.astype(BF16)


def _mix_prompt(proj, mk, mv, wpg, scale, wconv, batch, seq):
    tm = MIX_TM
    nt = seq // tm
    assert seq % tm == 0 and tm % HIST == 0
    rpb = tm // HIST
    return pl.pallas_call(
        _mix_prompt_kernel,
        grid=(batch, nt),
        in_specs=[
            pl.BlockSpec((tm, D_MIX), lambda b, t: (b * nt + t, 0)),
            pl.BlockSpec((HIST, D_MIX), lambda b, t: (jnp.maximum((b * nt + t) * rpb - 1, 0), 0)),
            pl.BlockSpec((1, N_MEM, D_XATTN), lambda b, t: (b, 0, 0)),
            pl.BlockSpec((1, N_MEM, D_XATTN), lambda b, t: (b, 0, 0)),
            pl.BlockSpec((len(POOL_WINDOWS), POOL_GROUP_DIM, POOL_GROUP_DIM), lambda b, t: (0, 0, 0)),
            pl.BlockSpec((1, D_POOL), lambda b, t: (0, 0)),
            pl.BlockSpec((CONV_WIDTH, D_CONV), lambda b, t: (0, 0)),
        ],
        out_specs=[
            pl.BlockSpec((tm, D_MODEL), lambda b, t: (b * nt + t, 0)),
            pl.BlockSpec((1, 8, D_CONV), lambda b, t: (b, 0, 0)),
        ],
        out_shape=[
            jax.ShapeDtypeStruct((batch * seq, D_MODEL), BF16),
            jax.ShapeDtypeStruct((batch, 8, D_CONV), F32),
        ],
        scratch_shapes=[pltpu.VMEM((HIST + tm, D_POOL), F32), pltpu.VMEM((HIST + tm, D_CONV), F32)],
        compiler_params=pltpu.CompilerParams(
            dimension_semantics=("arbitrary", "arbitrary"), vmem_limit_bytes=VMEM_LIMIT),
        name="mix_prompt",
    )(proj, proj, mk, mv, wpg, scale.reshape(1, D_POOL), wconv)


def _mix_sample_kernel(cur_ref, sp_ref, sc_ref, k_ref, v_ref, wpg_ref, scale_ref, wconv_ref,
                       br_ref, zst_ref, extu_ref, extz_ref):
    ns, ln = zst_ref.shape[0], zst_ref.shape[1]
    rows = ns * ln

    extu_ref[:, HIST - POOL_STATE_LEN:HIST, :] = sp_ref[...]
    extu_ref[:, HIST:HIST + ln, :] = cur_ref[:, C_U:C_U + D_POOL].reshape(ns, ln, D_POOL)
    for g, w in enumerate(POOL_WINDOWS):
        sl = slice(g * POOL_GROUP_DIM, (g + 1) * POOL_GROUP_DIM)
        s = extu_ref[:, HIST:HIST + ln, sl]
        for k in range(1, w):
            s = s + extu_ref[:, HIST - k:HIST - k + ln, sl]
        pooled = s / float(w) - extu_ref[:, HIST:HIST + ln, sl]
        pooled = pooled.reshape(rows, POOL_GROUP_DIM)
        br_ref[:, sl] = _pool_project(pooled, wpg_ref, scale_ref, g).astype(BF16)

    z = cur_ref[:, C_C:C_C + D_CONV] * cur_ref[:, C_V:C_V + D_CONV]
    extz_ref[:, HIST - 2:HIST, :] = sc_ref[...]
    extz_ref[:, HIST:HIST + ln, :] = z.reshape(ns, ln, D_CONV)
    y = extz_ref[:, HIST - 2:HIST - 2 + ln, :] * wconv_ref[0:1, :]
    y = y + extz_ref[:, HIST - 1:HIST - 1 + ln, :] * wconv_ref[1:2, :]
    y = y + extz_ref[:, HIST:HIST + ln, :] * wconv_ref[2:3, :]
    br_ref[:, D_POOL:D_POOL + D_CONV] = (
        cur_ref[:, C_B:C_B + D_CONV] * y.reshape(rows, D_CONV)).astype(BF16)
    zst_ref[...] = extz_ref[:, HIST:HIST + ln, :]

    q3 = cur_ref[:, C_Q:C_Q + D_XATTN].reshape(ns, ln, D_XATTN)
    q4 = jnp.concatenate([q3[:, :, h * XHEAD_DIM:(h + 1) * XHEAD_DIM] for h in range(N_XHEADS)], axis=1)
    s = jnp.einsum("nqd,nkd->nqk", q4.astype(BF16), k_ref[...].astype(BF16),
                   preferred_element_type=F32)
    row_head = lax.broadcasted_iota(jnp.int32, (N_XHEADS * ln, N_XHEADS * N_MEM), 0) // ln
    col_head = lax.broadcasted_iota(jnp.int32, (N_XHEADS * ln, N_XHEADS * N_MEM), 1) % N_XHEADS
    s = jnp.where((row_head == col_head)[None], s * (XHEAD_DIM ** -0.5), -jnp.inf)
    p = _softmax_rows(s)
    r = jnp.einsum("nqk,nkd->nqd", p.astype(BF16), v_ref[...].astype(BF16),
                   preferred_element_type=F32)
    for h in range(N_XHEADS):
        c0 = D_POOL + D_CONV + h * XHEAD_DIM
        br_ref[:, c0:c0 + XHEAD_DIM] = r[:, h * ln:(h + 1) * ln, :].reshape(rows, XHEAD_DIM).astype(BF16)


def _mix_sample(proj, state_pool, state_conv, mem_k, mem_v, wpg, scale, wconv, nseq, ln):
    ns = MIX_NS
    rows = ns * ln
    assert nseq % ns == 0 and ln == 8
    return pl.pallas_call(
        _mix_sample_kernel,
        grid=(nseq // ns,),
        in_specs=[
            pl.BlockSpec((rows, D_MIX), lambda s: (s, 0)),
            pl.BlockSpec((ns, POOL_STATE_LEN, D_POOL), lambda s: (s, 0, 0)),
            pl.BlockSpec((ns, CONV_WIDTH - 1, D_CONV), lambda s: (s, 0, 0)),
            pl.BlockSpec((ns, N_MEM * N_XHEADS, XHEAD_DIM), lambda s: (s, 0, 0)),
            pl.BlockSpec((ns, N_MEM * N_XHEADS, XHEAD_DIM), lambda s: (s, 0, 0)),
            pl.BlockSpec((len(POOL_WINDOWS), POOL_GROUP_DIM, POOL_GROUP_DIM), lambda s: (0, 0, 0)),
            pl.BlockSpec((1, D_POOL), lambda s: (0, 0)),
            pl.BlockSpec((CONV_WIDTH, D_CONV), lambda s: (0, 0)),
        ],
        out_specs=[
            pl.BlockSpec((rows, D_MODEL), lambda s: (s, 0)),
            pl.BlockSpec((ns, ln, D_CONV), lambda s: (s, 0, 0)),
        ],
        out_shape=[
            jax.ShapeDtypeStruct((nseq * ln, D_MODEL), BF16),
            jax.ShapeDtypeStruct((nseq, ln, D_CONV), F32),
        ],
        scratch_shapes=[pltpu.VMEM((ns, HIST + ln, D_POOL), F32),
                        pltpu.VMEM((ns, HIST + ln, D_CONV), F32)],
        compiler_params=pltpu.CompilerParams(
            dimension_semantics=("arbitrary",), vmem_limit_bytes=VMEM_LIMIT),
        name="mix_sample",
    )(proj, state_pool, state_conv, mem_k, mem_v, wpg, scale.reshape(1, D_POOL), wconv)


def _merge_route_kernel(x_ref, br_ref, g0_ref, g1_ref, g2_ref, wpo_ref, wco_ref, wao_ref, wo_ref,
                        nf_ref, wrhl_ref, wrh_ref, brt_ref,
                        x1_ref, h2_ref, ri_ref, rw_ref, cnt_ref, carry_ref):
    i = pl.program_id(0)
    tm = x_ref.shape[0]

    @pl.when(i == 0)
    def _():
        carry_ref[...] = jnp.zeros_like(carry_ref)

    merged = _sigmoid(g0_ref[...].astype(F32)) * jnp.dot(
        br_ref[:, 0:D_POOL], wpo_ref[...], preferred_element_type=F32)
    merged = merged + _sigmoid(g1_ref[...].astype(F32)) * jnp.dot(
        br_ref[:, D_POOL:D_POOL + D_CONV], wco_ref[...], preferred_element_type=F32)
    merged = merged + _sigmoid(g2_ref[...].astype(F32)) * jnp.dot(
        br_ref[:, D_POOL + D_CONV:D_MODEL], wao_ref[...], preferred_element_type=F32)
    x1 = x_ref[...] + jnp.dot(merged.astype(BF16), wo_ref[...], preferred_element_type=F32)
    x1_ref[...] = x1
    h2 = _rms(x1, nf_ref[...])
    h2_ref[...] = _pack_bf16_pairs(h2)

    h2_hi = h2.astype(BF16)
    h2_lo = (h2 - h2_hi.astype(F32)).astype(BF16)
    hi_part = jnp.dot(h2_hi, wrhl_ref[...], preferred_element_type=F32)
    logits = (hi_part[:, 0:N_EXPERTS] + hi_part[:, N_EXPERTS:2 * N_EXPERTS]
              + jnp.dot(h2_lo, wrh_ref[...], preferred_element_type=F32)) + brt_ref[...]
    lane = lax.broadcasted_iota(jnp.int32, (tm, N_EXPERTS), 1).astype(F32)
    vals, idxs, hots = [], [], []
    work = logits
    for _ in range(TOP_K):
        m = jnp.max(work, axis=-1, keepdims=True)
        idx = jnp.min(jnp.where(work == m, lane, float(N_EXPERTS)), axis=-1, keepdims=True)
        hot = lane == idx
        work = jnp.where(hot, -jnp.inf, work)
        vals.append(m)
        idxs.append(idx)
        hots.append(hot)
    es = [jnp.exp(v - vals[0]) for v in vals]
    denom = es[0] + es[1] + es[2] + es[3]

    chosen = jnp.where(hots[0] | hots[1] | hots[2] | hots[3], 1.0, 0.0).astype(BF16)
    r_i = lax.broadcasted_iota(jnp.int32, (tm, tm), 0)
    c_i = lax.broadcasted_iota(jnp.int32, (tm, tm), 1)
    lower = jnp.where(c_i < r_i, 1.0, 0.0).astype(BF16)
    before = jnp.dot(lower, chosen, preferred_element_type=F32) + carry_ref[0:1, 0:N_EXPERTS]
    carry_ref[0:1, 0:N_EXPERTS] = (carry_ref[0:1, 0:N_EXPERTS]
                                   + jnp.sum(chosen.astype(F32), axis=0, keepdims=True))
    cnt_ref[...] = carry_ref[...]

    out_lane = lax.broadcasted_iota(jnp.int32, (tm, LANES), 1)
    ri = jnp.zeros((tm, LANES), jnp.int32)
    rw = jnp.zeros((tm, LANES), F32)
    for k in range(TOP_K):
        rank = jnp.sum(jnp.where(hots[k], before, 0.0), axis=-1, keepdims=True).astype(jnp.int32)
        ri = jnp.where(out_lane == k, idxs[k].astype(jnp.int32), ri)
        ri = jnp.where(out_lane == TOP_K + k, rank, ri)
        rw = jnp.where(out_lane == k, es[k] / denom, rw)
    ri_ref[...] = ri
    rw_ref[...] = rw


def _merge_route(x, branch, gates, wpo, wco, wao, wo, norm_ffn, w_router, b_router, name):
    t = x.shape[0]
    tm = MERGE_TM
    assert t % tm == 0 and gates.shape == (t, N_BRANCH * D_MODEL)
    const = lambda i: (0, 0)
    wr_hi = w_router.astype(BF16)
    wr_lo = (w_router - wr_hi.astype(F32)).astype(BF16)
    return pl.pallas_call(
        _merge_route_kernel,
        grid=(t // tm,),
        in_specs=[
            pl.BlockSpec((tm, D_MODEL), lambda i: (i, 0)),
            pl.BlockSpec((tm, D_MODEL), lambda i: (i, 0)),
            pl.BlockSpec((tm, D_MODEL), lambda i: (i, 0)),
            pl.BlockSpec((tm, D_MODEL), lambda i: (i, 1)),
            pl.BlockSpec((tm, D_MODEL), lambda i: (i, 2)),
            pl.BlockSpec((D_POOL, D_MODEL), const, pipeline_mode=pl.Buffered(1)),
            pl.BlockSpec((D_CONV, D_MODEL), const, pipeline_mode=pl.Buffered(1)),
            pl.BlockSpec((D_XATTN, D_MODEL), const, pipeline_mode=pl.Buffered(1)),
            pl.BlockSpec((D_MODEL, D_MODEL), const, pipeline_mode=pl.Buffered(1)),
            pl.BlockSpec((1, D_MODEL), const),
            pl.BlockSpec((D_MODEL, 2 * N_EXPERTS), const),
            pl.BlockSpec((D_MODEL, N_EXPERTS), const),
            pl.BlockSpec((1, N_EXPERTS), const),
        ],
        out_specs=[
            pl.BlockSpec((tm, D_MODEL), lambda i: (i, 0)),
            pl.BlockSpec((tm, D_MODEL // 2), lambda i: (i, 0)),
            pl.BlockSpec((tm, LANES), lambda i: (i, 0)),
            pl.BlockSpec((tm, LANES), lambda i: (i, 0)),
            pl.BlockSpec((8, LANES), const),
        ],
        out_shape=[
            jax.ShapeDtypeStruct((t, D_MODEL), F32),
            jax.ShapeDtypeStruct((t, D_MODEL // 2), PACKED),
            jax.ShapeDtypeStruct((t, LANES), jnp.int32),
            jax.ShapeDtypeStruct((t, LANES), F32),
            jax.ShapeDtypeStruct((8, LANES), F32),
        ],
        scratch_shapes=[pltpu.VMEM((8, LANES), F32)],
        compiler_params=pltpu.CompilerParams(
            dimension_semantics=("arbitrary",), vmem_limit_bytes=VMEM_LIMIT),
        name=name,
    )(x, branch, gates, gates, gates, wpo, wco, wao, wo, norm_ffn.reshape(1, D_MODEL),
      jnp.concatenate([wr_hi, wr_lo], axis=1), wr_hi, b_router.reshape(1, N_EXPERTS))


def _fill_work_list(nblk_s, pad_s, nj, refs):
    te, tj, trow, tnb, tfirst, tne, tnj, tpar, ttot = refs

    def clear(i, c):
        for r in (te, tj, trow, tnb, tfirst, tpar):
            r[i] = 0
        tne[i] = -1
        tnj[i] = -1
        return c

    lax.fori_loop(0, te.shape[0], clear, 0)

    carry = (jnp.int32(0), jnp.int32(-1), jnp.int32(0))
    for j in range(nj):
        def per_expert(e, carry):
            q, prev, groups = carry
            n = nblk_s[e]
            nch = (n + EXP_CHUNK - 1) // EXP_CHUNK

            @pl.when((nch > 0) & (prev >= 0))
            def _():
                tne[prev] = e
                tnj[prev] = jnp.int32(j)

            def per_chunk(c, q):
                te[q] = e
                tj[q] = jnp.int32(j)
                trow[q] = pad_s[e] + c * (EXP_CHUNK * EXP_TM)
                tnb[q] = jnp.minimum(EXP_CHUNK, n - c * EXP_CHUNK)
                tfirst[q] = (c == 0).astype(jnp.int32)
                tpar[q] = lax.rem(groups, 2)
                return q + 1

            return (lax.fori_loop(0, nch, per_chunk, q), jnp.where(nch > 0, q, prev),
                    groups + (nch > 0).astype(jnp.int32))

        carry = lax.fori_loop(0, N_EXPERTS, per_expert, carry)
    ttot[0] = carry[0]


def _route_tables_kernel(cntp_ref, cnts_ref, rip_ref, ris_ref, dest_ref, *refs, nj_in, nj_out):
    tabs_in, tabs_out = refs[0:N_TABS], refs[N_TABS:2 * N_TABS]
    nblk_s, pad_s = refs[2 * N_TABS:]

    def per_expert(e, start_blk):
        n = (cntp_ref[e] + cnts_ref[e] + EXP_TM - 1) // EXP_TM
        nblk_s[e] = n
        pad_s[e] = start_blk * EXP_TM
        return start_blk + n

    lax.fori_loop(0, N_EXPERTS, per_expert, jnp.int32(0))
    _fill_work_list(nblk_s, pad_s, nj_in, tabs_in)
    _fill_work_list(nblk_s, pad_s, nj_out, tabs_out)

    tile = ROUTE_TM
    col0 = 0
    for ri_ref, is_sample in ((rip_ref, False), (ris_ref, True)):
        for r in range(ri_ref.shape[0] // tile):
            ri = ri_ref[r * tile:(r + 1) * tile, :]
            base = jnp.zeros_like(ri)
            for e in range(N_EXPERTS):
                first_row = pad_s[e] + cntp_ref[e] if is_sample else pad_s[e]
                base = jnp.where(ri == e, first_row, base)
            dest = base + pltpu.roll(ri, LANES - TOP_K, axis=1)
            dest_ref[:, col0:col0 + tile] = jnp.transpose(dest)[0:8, :]
            col0 += tile


def _route_tables(cnt_p, cnt_s, ri_p, ri_s, nj_in, nj_out, max_items):
    t_all = ri_p.shape[0] + ri_s.shape[0]
    assert ri_p.shape[0] % ROUTE_TM == 0 and ri_s.shape[0] % ROUTE_TM == 0
    smem = pl.BlockSpec(memory_space=pltpu.SMEM)

    def tables(nj):
        n = nj * max_items
        return [jax.ShapeDtypeStruct((n,), jnp.int32)] * (N_TABS - 1) + [jax.ShapeDtypeStruct((1,), jnp.int32)]

    out = pl.pallas_call(
        functools.partial(_route_tables_kernel, nj_in=nj_in, nj_out=nj_out),
        grid=(1,),
        in_specs=[smem, smem,
                  pl.BlockSpec(ri_p.shape, lambda i: (0, 0)), pl.BlockSpec(ri_s.shape, lambda i: (0, 0))],
        out_specs=[pl.BlockSpec((8, t_all), lambda i: (0, 0))] + [smem] * (2 * N_TABS),
        out_shape=[jax.ShapeDtypeStruct((8, t_all), jnp.int32)] + tables(nj_in) + tables(nj_out),
        scratch_shapes=[pltpu.SMEM((N_EXPERTS,), jnp.int32), pltpu.SMEM((N_EXPERTS,), jnp.int32)],
        compiler_params=pltpu.CompilerParams(
            dimension_semantics=("arbitrary",), vmem_limit_bytes=VMEM_LIMIT),
        name="route_tables",
    )(cnt_p, cnt_s, ri_p, ri_s)
    return out[0], tuple(out[1:1 + N_TABS]), tuple(out[1 + N_TABS:1 + 2 * N_TABS])


def _sc_mesh():
    return plsc.VectorSubcoreMesh(core_axis_name="c", subcore_axis_name="s",
                                  num_cores=SC_CORES, num_subcores=SC_SUBCORES)


def _sc_worker_id():
    return lax.axis_index("s") * SC_CORES + lax.axis_index("c")


def _sc_dispatch(h2_a, h2_b, dest_by_slot, rows_out):
    ta, w = h2_a.shape
    t = ta + h2_b.shape[0]
    per_w = t // SC_WORKERS
    n_chunks = per_w // SC_CHUNK
    assert t == SC_WORKERS * n_chunks * SC_CHUNK and ta % SC_CHUNK == 0
    idx = dest_by_slot[:TOP_K].reshape(TOP_K, SC_WORKERS, n_chunks, SC_CHUNK).transpose(1, 0, 2, 3)

    def body(a_hbm, b_hbm, idx_hbm, xs_hbm, idx_v, rows_v, sems):
        base = _sc_worker_id() * per_w
        pltpu.sync_copy(idx_hbm.at[_sc_worker_id()], idx_v)
        pending = [[], []]
        for c in range(n_chunks):
            b = c % 2
            for d in pending[b]:
                d.wait()
            tok0 = base + c * SC_CHUNK

            @pl.when(tok0 < ta)
            def _():
                pltpu.sync_copy(a_hbm.at[pl.ds(tok0, SC_CHUNK)], rows_v.at[b])

            @pl.when(tok0 >= ta)
            def _():
                pltpu.sync_copy(b_hbm.at[pl.ds(tok0 - ta, SC_CHUNK)], rows_v.at[b])

            pending[b] = [pltpu.async_copy(rows_v.at[b], xs_hbm.at[idx_v.at[k, c]], sems.at[b])
                          for k in range(TOP_K)]
        for b in range(2):
            for d in pending[b]:
                d.wait()

    return pl.kernel(
        body,
        out_type=jax.ShapeDtypeStruct((rows_out, w), h2_a.dtype),
        mesh=_sc_mesh(),
        scratch_types=[pltpu.VMEM((TOP_K, n_chunks, SC_CHUNK), jnp.int32),
                       pltpu.VMEM((2, SC_CHUNK, w), h2_a.dtype),
                       pltpu.SemaphoreType.DMA((2,))],
        name="sc_dispatch",
    )(h2_a, h2_b, idx)


def _sc_gather_rows(table, idx):
    n = idx.shape[0]
    w = table.shape[1]
    per_w = n // SC_WORKERS
    n_chunks = per_w // SC_CHUNK
    assert n == SC_WORKERS * n_chunks * SC_CHUNK
    idx3 = idx.reshape(SC_WORKERS, n_chunks, SC_CHUNK)

    def body(table_hbm, idx_hbm, out_hbm, idx_v, rows_v, gsem, wsems):
        base = _sc_worker_id() * per_w
        pltpu.sync_copy(idx_hbm.at[_sc_worker_id()], idx_v)
        pending = [None, None]
        for c in range(n_chunks):
            b = c % 2
            if pending[b] is not None:
                pending[b].wait()
            pltpu.async_copy(table_hbm.at[idx_v.at[c]], rows_v.at[b], gsem).wait()
            pending[b] = pltpu.async_copy(
                rows_v.at[b], out_hbm.at[pl.ds(base + c * SC_CHUNK, SC_CHUNK)], wsems.at[b])
        for b in range(2):
            if pending[b] is not None:
                pending[b].wait()

    return pl.kernel(
        body,
        out_type=jax.ShapeDtypeStruct((n, w), table.dtype),
        mesh=_sc_mesh(),
        scratch_types=[pltpu.VMEM((n_chunks, SC_CHUNK), jnp.int32),
                       pltpu.VMEM((2, SC_CHUNK, w), table.dtype),
                       pltpu.SemaphoreType.DMA,
                       pltpu.SemaphoreType.DMA((2,))],
        name="sc_gather_rows",
    )(table, idx3)


def _grouped_pipeline(tabs, total_ref, weight_copies, in_copy, out_copy, compute):
    te, tj, _, tnb, tfirst, tne, tnj, tpar = tabs
    total = total_ref[0]

    def each_block(q, copy, op):
        slot = lax.rem(q, 2)
        for m in range(1, EXP_CHUNK + 1):
            @pl.when(tnb[q] == m)
            def _():
                op(copy(q, slot, m))

    def start(d):
        d.start()

    def wait(d):
        d.wait()

    for d in weight_copies(te[0], tj[0], tpar[0]):
        d.start()
    each_block(0, in_copy, start)

    def body(q, carry):
        each_block(q, in_copy, wait)

        @pl.when(q + 1 < total)
        def _():
            each_block(q + 1, in_copy, start)

        @pl.when(tfirst[q] == 1)
        def _():
            for d in weight_copies(te[q], tj[q], tpar[q]):
                d.wait()

            @pl.when(tne[q] >= 0)
            def _():
                for d in weight_copies(tne[q], tnj[q], 1 - tpar[q]):
                    d.start()

        @pl.when(q >= 2)
        def _():
            each_block(q - 2, out_copy, wait)

        for m in range(1, EXP_CHUNK + 1):
            @pl.when(tnb[q] == m)
            def _():
                compute(q, lax.rem(q, 2), m, tpar[q])

        each_block(q, out_copy, start)
        return carry

    lax.fori_loop(0, total, body, 0)

    @pl.when(total >= 2)
    def _():
        each_block(total - 2, out_copy, wait)

    each_block(total - 1, out_copy, wait)


def _expert_in_kernel(*refs):
    tabs, total_ref = refs[:N_TABS - 1], refs[N_TABS - 1]
    xs_hbm, w_hbm, b_ref, act_hbm, wbuf, x_buf, a_buf, w_sem, x_sem, a_sem = refs[N_TABS:]
    te, tj, trow = tabs[:3]
    tm, tn = EXP_TM, a_buf.shape[2]
    nj = D_FF // tn

    def weight_copies(e, j, par):
        return [pltpu.make_async_copy(
            w_hbm.at[e, :, pl.ds(pl.multiple_of(half * D_FF + j * tn, tn), tn)],
            wbuf.at[2 * par + half], w_sem)
            for half in range(2)]

    def rows(q, m):
        return pl.ds(pl.multiple_of(trow[q], tm), m * tm)

    def x_copy(q, slot, m):
        return pltpu.make_async_copy(xs_hbm.at[rows(q, m)], x_buf.at[slot, pl.ds(0, m * tm)], x_sem.at[slot])

    def a_copy(q, slot, m):
        return pltpu.make_async_copy(
            a_buf.at[slot, pl.ds(0, m * tm)],
            act_hbm.at[rows(q, m), pl.ds(pl.multiple_of(tj[q] * tn, tn), tn)], a_sem.at[slot])

    def compute(q, slot, m, par):
        bias0 = te[q] * (2 * nj) + tj[q]
        x = jnp.concatenate(_unpack_bf16_pairs(x_buf[slot, 0:m * tm]), axis=1).astype(BF16)
        g = jnp.dot(x, wbuf[2 * par].astype(BF16), preferred_element_type=F32) + b_ref[bias0]
        up = jnp.dot(x, wbuf[2 * par + 1].astype(BF16), preferred_element_type=F32) + b_ref[bias0 + nj]
        g = jnp.minimum(g, SWIGLU_LIMIT)
        up = jnp.clip(up, -SWIGLU_LIMIT, SWIGLU_LIMIT)
        a_buf[slot, 0:m * tm] = (g * _sigmoid(SWIGLU_ALPHA * g) * (up + 1.0)).astype(BF16)

    _grouped_pipeline(tabs, total_ref, weight_copies, x_copy, a_copy, compute)


def _expert_in(tabs, xs, w_in, b_in):
    rows = xs.shape[0]
    tm, tn = EXP_TM, EXP_IN_TN
    nj = D_FF // tn
    bias = b_in.reshape(N_EXPERTS * 2 * nj, 1, tn)
    grid_spec = pltpu.PrefetchScalarGridSpec(
        num_scalar_prefetch=len(tabs),
        grid=(1,),
        in_specs=[
            pl.BlockSpec(memory_space=pl.ANY),
            pl.BlockSpec(memory_space=pl.ANY),
            pl.BlockSpec(bias.shape, lambda i, *_: (0, 0, 0), pipeline_mode=pl.Buffered(1)),
        ],
        out_specs=pl.BlockSpec(memory_space=pl.ANY),
        scratch_shapes=[pltpu.VMEM((4, D_MODEL, tn), F32),
                        pltpu.VMEM((2, EXP_CHUNK * tm, D_MODEL // 2), PACKED),
                        pltpu.VMEM((2, EXP_CHUNK * tm, tn), BF16),
                        pltpu.SemaphoreType.DMA, pltpu.SemaphoreType.DMA((2,)),
                        pltpu.SemaphoreType.DMA((2,))],
    )
    return pl.pallas_call(
        _expert_in_kernel,
        grid_spec=grid_spec,
        out_shape=jax.ShapeDtypeStruct((rows, D_FF), BF16),
        compiler_params=pltpu.CompilerParams(
            dimension_semantics=("arbitrary",), vmem_limit_bytes=VMEM_LIMIT),
        name="expert_in",
    )(*tabs, xs, w_in, bias)


def _expert_out_kernel(*refs):
    tabs, total_ref = refs[:N_TABS - 1], refs[N_TABS - 1]
    act_hbm, w_hbm, b_ref, ys_hbm, wbuf, a_buf, y_buf, w_sem, a_sem, y_sem = refs[N_TABS:]
    te, tj, trow = tabs[:3]
    tm, tnp = EXP_TM, y_buf.shape[2]
    tn = 2 * tnp
    nj = D_MODEL // tn

    def weight_copies(e, j, par):
        return [pltpu.make_async_copy(
            w_hbm.at[e, :, pl.ds(pl.multiple_of(j * tn, tn), tn)], wbuf.at[par], w_sem)]

    def rows(q, m):
        return pl.ds(pl.multiple_of(trow[q], tm), m * tm)

    def a_copy(q, slot, m):
        return pltpu.make_async_copy(act_hbm.at[rows(q, m)], a_buf.at[slot, pl.ds(0, m * tm)], a_sem.at[slot])

    def y_copy(q, slot, m):
        return pltpu.make_async_copy(
            y_buf.at[slot, pl.ds(0, m * tm)],
            ys_hbm.at[rows(q, m), pl.ds(pl.multiple_of(tj[q] * tnp, tnp), tnp)], y_sem.at[slot])

    def compute(q, slot, m, par):
        y = jnp.dot(a_buf[slot, 0:m * tm], wbuf[par].astype(BF16),
                    preferred_element_type=F32) + b_ref[te[q] * nj + tj[q]]
        y_buf[slot, 0:m * tm] = _pack_bf16_pairs(y)

    _grouped_pipeline(tabs, total_ref, weight_copies, a_copy, y_copy, compute)


def _expert_out(tabs, act, w_out, b_out):
    rows = act.shape[0]
    tm, tn = EXP_TM, EXP_OUT_TN
    nj = D_MODEL // tn
    bias = b_out.reshape(N_EXPERTS * nj, 1, tn)
    grid_spec = pltpu.PrefetchScalarGridSpec(
        num_scalar_prefetch=len(tabs),
        grid=(1,),
        in_specs=[
            pl.BlockSpec(memory_space=pl.ANY),
            pl.BlockSpec(memory_space=pl.ANY),
            pl.BlockSpec(bias.shape, lambda i, *_: (0, 0, 0), pipeline_mode=pl.Buffered(1)),
        ],
        out_specs=pl.BlockSpec(memory_space=pl.ANY),
        scratch_shapes=[pltpu.VMEM((2, D_FF, tn), F32),
                        pltpu.VMEM((2, EXP_CHUNK * tm, D_FF), BF16),
                        pltpu.VMEM((2, EXP_CHUNK * tm, tn // 2), PACKED),
                        pltpu.SemaphoreType.DMA, pltpu.SemaphoreType.DMA((2,)),
                        pltpu.SemaphoreType.DMA((2,))],
    )
    return pl.pallas_call(
        _expert_out_kernel,
        grid_spec=grid_spec,
        out_shape=jax.ShapeDtypeStruct((rows, D_MODEL // 2), PACKED),
        compiler_params=pltpu.CompilerParams(
            dimension_semantics=("arbitrary",), vmem_limit_bytes=VMEM_LIMIT),
        name="expert_out",
    )(*tabs, act, w_out, bias)


def _combine_kernel(x1_ref, yg_ref, rw_ref, g_ref, o_ref):
    half = EXP_OUT_TN // 2
    pieces = []
    ssq = jnp.zeros((x1_ref.shape[0], 1), F32)
    for jt in range(D_MODEL // EXP_OUT_TN):
        acc_lo = x1_ref[:, jt * EXP_OUT_TN:jt * EXP_OUT_TN + half]
        acc_hi = x1_ref[:, jt * EXP_OUT_TN + half:(jt + 1) * EXP_OUT_TN]
        moe_lo = jnp.zeros_like(acc_lo)
        moe_hi = jnp.zeros_like(acc_hi)
        for k in range(TOP_K):
            lo, hi = _unpack_bf16_pairs(yg_ref[k, :, jt * half:(jt + 1) * half])
            moe_lo = moe_lo + lo * rw_ref[:, k:k + 1]
            moe_hi = moe_hi + hi * rw_ref[:, k:k + 1]
        for acc in (acc_lo + moe_lo, acc_hi + moe_hi):
            ssq = ssq + jnp.sum(acc * acc, axis=-1, keepdims=True)
            pieces.append(acc)
    inv = lax.rsqrt(ssq / D_MODEL + EPS)
    for n, acc in enumerate(pieces):
        o_ref[:, n * half:(n + 1) * half] = acc * inv * g_ref[:, n * half:(n + 1) * half]


def _combine_part_kernel(x1_ref, yg_ref, rw_ref, g_ref, prev_hbm, o_ref):
    del prev_hbm
    _combine_kernel(x1_ref, yg_ref, rw_ref, g_ref, o_ref)


def _combine(x1, yg, rw, gain, row0, prev, name):
    t = x1.shape[0]
    n = yg.shape[1]
    tm = COMB_TM
    assert n % tm == 0 and row0 % tm == 0
    off = row0 // tm
    in_specs = [
        pl.BlockSpec((tm, D_MODEL), lambda i: (i + off, 0)),
        pl.BlockSpec((TOP_K, tm, D_MODEL // 2), lambda i: (0, i, 0)),
        pl.BlockSpec((tm, LANES), lambda i: (i + off, 0)),
        pl.BlockSpec((1, D_MODEL), lambda i: (0, 0)),
    ]
    args = [x1, yg, rw, gain.reshape(1, D_MODEL)]
    if prev is not None:
        in_specs.append(pl.BlockSpec(memory_space=pl.ANY))
        args.append(prev)
    return pl.pallas_call(
        _combine_kernel if prev is None else _combine_part_kernel,
        grid=(n // tm,),
        in_specs=in_specs,
        out_specs=pl.BlockSpec((tm, D_MODEL), lambda i: (i + off, 0)),
        out_shape=jax.ShapeDtypeStruct((t, D_MODEL), F32),
        input_output_aliases={} if prev is None else {len(args) - 1: 0},
        compiler_params=pltpu.CompilerParams(
            dimension_semantics=("arbitrary",), vmem_limit_bytes=VMEM_LIMIT),
        name=name,
    )(*args)


def kernel(x_prompt, x_sample, state_pool, state_conv, cache_mem_k, cache_mem_v, mem_prompt,
           norm_mix, w_in, b_gate, w_pool_group, pool_scale, w_conv, mem_norm, w_mem_kv,
           w_pool_out, w_conv_out, w_attn_out, w_o, norm_ffn, w_router, b_router,
           w_exp_in, b_exp_in, w_exp_out, b_exp_out, final_norm):
    depth = norm_mix.shape[0]
    assert depth == 1
    l = 0
    bp, seq, _ = x_prompt.shape
    bs, ln, _ = x_sample.shape
    tp, ts = bp * seq, bs * ln

    kv = _norm_matmul(mem_prompt.reshape(bp * N_MEM, D_MODEL), mem_norm[l], w_mem_kv[l],
                      jnp.zeros((2 * D_XATTN,), F32), "mem_kv")
    mk = kv[:, :D_XATTN].reshape(bp, N_MEM, D_XATTN)
    mv = kv[:, D_XATTN:].reshape(bp, N_MEM, D_XATTN)

    bias_in = jnp.concatenate([jnp.zeros((D_MIX,), F32), b_gate[l]])
    xp = x_prompt.reshape(tp, D_MODEL)
    xs_ = x_sample.reshape(ts, D_MODEL)
    proj_p, gates_p = _norm_matmul_split(xp, norm_mix[l], w_in[l], bias_in, D_MIX, "proj_prompt")
    proj_s, gates_s = _norm_matmul_split(xs_, norm_mix[l], w_in[l], bias_in, D_MIX, "proj_sample")

    br_p, zst_p = _mix_prompt(proj_p, mk, mv, w_pool_group[l], pool_scale[l], w_conv[l], bp, seq)
    br_s, zst_s = _mix_sample(proj_s, state_pool[l], state_conv[l],
                              cache_mem_k[l].reshape(bs, N_MEM * N_XHEADS, XHEAD_DIM),
                              cache_mem_v[l].reshape(bs, N_MEM * N_XHEADS, XHEAD_DIM),
                              w_pool_group[l], pool_scale[l], w_conv[l], bs, ln)

    wpo, wco, wao, wo = (w.astype(BF16) for w in (w_pool_out[l], w_conv_out[l], w_attn_out[l], w_o[l]))
    x1_p, h2_p, ri_p, rw_p, cnt_p = _merge_route(xp, br_p, gates_p, wpo, wco, wao, wo, norm_ffn[l],
                                                  w_router[l], b_router[l], "merge_route_prompt")
    x1_s, h2_s, ri_s, rw_s, cnt_s = _merge_route(xs_, br_s, gates_s, wpo, wco, wao, wo, norm_ffn[l],
                                                  w_router[l], b_router[l], "merge_route_sample")

    t_all = tp + ts
    n_assign = t_all * TOP_K
    nb_max = n_assign // EXP_TM + N_EXPERTS
    max_items = nb_max // EXP_CHUNK + N_EXPERTS
    dest, tabs_in, tabs_out = _route_tables(
        cnt_p[0, :N_EXPERTS].astype(jnp.int32), cnt_s[0, :N_EXPERTS].astype(jnp.int32), ri_p, ri_s,
        D_FF // EXP_IN_TN, D_MODEL // EXP_OUT_TN, max_items)
    xs_sorted = _sc_dispatch(h2_p, h2_s, dest, nb_max * EXP_TM)

    act = _expert_in(tabs_in, xs_sorted, w_exp_in[l], b_exp_in[l])
    ys = _expert_out(tabs_out, act, w_exp_out[l], b_exp_out[l])
    def combine_part(x1, rw, tok0, n, row0, prev, name):
        idx = dest[:TOP_K, tok0:tok0 + n].reshape(-1)
        yg = _sc_gather_rows(ys, idx).reshape(TOP_K, n, D_MODEL // 2)
        return _combine(x1, yg, rw, final_norm, row0, prev, name)

    y_p = None
    part = tp // COMB_PARTS
    for c in range(COMB_PARTS):
        y_p = combine_part(x1_p, rw_p, c * part, part, c * part, y_p, f"combine_prompt_{c}")
    y_s = combine_part(x1_s, rw_s, tp, ts, 0, None, "combine_sample")

    new_pool_p = proj_p.reshape(bp, seq, D_MIX)[:, seq - POOL_STATE_LEN:, :D_POOL]
    new_conv_p = zst_p[:, 8 - (CONV_WIDTH - 1):]
    u_s = proj_s[:, :D_POOL].reshape(bs, ln, D_POOL)
    new_pool_s = jnp.concatenate([state_pool[l], u_s], axis=1)[:, -POOL_STATE_LEN:]
    new_conv_s = zst_s[:, ln - (CONV_WIDTH - 1):]

    return (y_p.reshape(bp, seq, D_MODEL), y_s.reshape(bs, ln, D_MODEL),
            new_pool_p[None], new_conv_p[None],
            mk.reshape(1, bp, N_MEM, N_XHEADS, XHEAD_DIM), mv.reshape(1, bp, N_MEM, N_XHEADS, XHEAD_DIM),
            new_pool_s[None], new_conv_s[None])
```

```python
import functools

import jax
import jax.numpy as jnp
from jax import lax
from jax.experimental import pallas as pl
from jax.experimental.pallas import tpu as pltpu
from jax.experimental.pallas import tpu_sc as plsc

F32 = jnp.float32
BF16 = jnp.bfloat16
PACKED = jnp.int32

D_MODEL = 2048
POOL_WINDOWS = (2, 4, 8, 16)
POOL_GROUP_DIM = 128
D_POOL = 512
POOL_STATE_LEN = 15
D_CONV = 1024
CONV_WIDTH = 3
N_MEM = 256
N_XHEADS = 4
XHEAD_DIM = 128
D_XATTN = 512
N_BRANCH = 3
D_MIX = D_POOL + 3 * D_CONV + D_XATTN
D_IN_TOTAL = D_MIX + N_BRANCH * D_MODEL
N_EXPERTS = 32
TOP_K = 4
D_FF = D_MODEL
SWIGLU_LIMIT = 7.0
SWIGLU_ALPHA = 1.702
EPS = 1e-5

C_U = 0
C_V = D_POOL
C_B = D_POOL + D_CONV
C_C = D_POOL + 2 * D_CONV
C_Q = D_POOL + 3 * D_CONV

LANES = 128
HIST = 16

PROJ_TM = 1024
PROJ_TN = 1024
MIX_TM = 512
MIX_NS = 8
MERGE_TM = 256
ROUTE_TM = 1024
N_TABS = 9
EXP_TM = 128
EXP_CHUNK = 4
EXP_IN_TN = 1024
EXP_OUT_TN = 2048
COMB_TM = 256
COMB_PARTS = (2, 3, 3)
SC_CORES = 2
SC_SUBCORES = 16
SC_WORKERS = SC_CORES * SC_SUBCORES
SC_CHUNK = 32
VMEM_LIMIT = 56 * 1024 * 1024


def _sigmoid(x):
    return 0.5 * (jnp.tanh(0.5 * x) + 1.0)


def _rms(x, g):
    ms = jnp.mean(x * x, axis=-1, keepdims=True)
    return x * lax.rsqrt(ms + EPS) * g


def _pack_bf16_pairs(x):
    n = x.shape[1] // 2
    lo = lax.bitcast_convert_type(x[:, :n].astype(BF16).astype(F32), jnp.uint32)
    hi = lax.bitcast_convert_type(x[:, n:].astype(BF16).astype(F32), jnp.uint32)
    return lax.bitcast_convert_type((hi & jnp.uint32(0xFFFF0000)) | (lo >> 16), PACKED)


def _unpack_bf16_pairs(p):
    p = lax.bitcast_convert_type(p, jnp.uint32)
    lo = lax.bitcast_convert_type(p << 16, F32)
    hi = lax.bitcast_convert_type(p & jnp.uint32(0xFFFF0000), F32)
    return lo, hi


def _norm_matmul_kernel(x_ref, g_ref, w_ref, b_ref, o_ref, h_ref):
    @pl.when(pl.program_id(1) == 0)
    def _():
        h_ref[...] = _rms(x_ref[...], g_ref[...]).astype(BF16)

    o_ref[...] = jnp.dot(h_ref[...], w_ref[...].astype(BF16), preferred_element_type=F32) + b_ref[...]


def _norm_matmul(x, gain, w, bias, name):
    t, d = x.shape
    n = w.shape[1]
    tm = min(PROJ_TM, t)
    tn = PROJ_TN
    assert t % tm == 0 and n % tn == 0
    return pl.pallas_call(
        _norm_matmul_kernel,
        grid=(t // tm, n // tn),
        in_specs=[
            pl.BlockSpec((tm, d), lambda i, j: (i, 0)),
            pl.BlockSpec((1, d), lambda i, j: (0, 0)),
            pl.BlockSpec((d, tn), lambda i, j: (0, j)),
            pl.BlockSpec((1, tn), lambda i, j: (0, j)),
        ],
        out_specs=pl.BlockSpec((tm, tn), lambda i, j: (i, j)),
        out_shape=jax.ShapeDtypeStruct((t, n), F32),
        scratch_shapes=[pltpu.VMEM((tm, d), BF16)],
        compiler_params=pltpu.CompilerParams(
            dimension_semantics=("arbitrary", "arbitrary"), vmem_limit_bytes=VMEM_LIMIT),
        name=name,
    )(x, gain.reshape(1, d), w, bias.reshape(1, n))


def _norm_matmul_split_kernel(x_ref, g_ref, w_ref, b_ref, o_ref, og_ref, h_ref, *, n_main):
    j = pl.program_id(1)

    @pl.when(j == 0)
    def _():
        h_ref[...] = _rms(x_ref[...], g_ref[...]).astype(BF16)

    acc = jnp.dot(h_ref[...], w_ref[...].astype(BF16), preferred_element_type=F32) + b_ref[...]

    @pl.when(j < n_main)
    def _():
        o_ref[...] = acc

    @pl.when(j >= n_main)
    def _():
        og_ref[...] = acc.astype(BF16)


def _norm_matmul_split(x, gain, w, bias, split_col, name):
    t, d = x.shape
    n = w.shape[1]
    tm = min(PROJ_TM, t)
    tn = PROJ_TN
    assert t % tm == 0 and n % tn == 0 and split_col % tn == 0 and 0 < split_col < n
    n_main = split_col // tn
    return pl.pallas_call(
        functools.partial(_norm_matmul_split_kernel, n_main=n_main),
        grid=(t // tm, n // tn),
        in_specs=[
            pl.BlockSpec((tm, d), lambda i, j: (i, 0)),
            pl.BlockSpec((1, d), lambda i, j: (0, 0)),
            pl.BlockSpec((d, tn), lambda i, j: (0, j)),
            pl.BlockSpec((1, tn), lambda i, j: (0, j)),
        ],
        out_specs=[
            pl.BlockSpec((tm, tn), lambda i, j: (i, jnp.minimum(j, n_main - 1))),
            pl.BlockSpec((tm, tn), lambda i, j: (i, jnp.maximum(j - n_main, 0))),
        ],
        out_shape=[jax.ShapeDtypeStruct((t, split_col), F32),
                   jax.ShapeDtypeStruct((t, n - split_col), BF16)],
        scratch_shapes=[pltpu.VMEM((tm, d), BF16)],
        compiler_params=pltpu.CompilerParams(
            dimension_semantics=("arbitrary", "arbitrary"), vmem_limit_bytes=VMEM_LIMIT),
        name=name,
    )(x, gain.reshape(1, d), w, bias.reshape(1, n))


def _pool_project(pooled, wpg_ref, scale_ref, g):
    sl = slice(g * POOL_GROUP_DIM, (g + 1) * POOL_GROUP_DIM)
    y = jnp.dot(pooled.astype(BF16), wpg_ref[g].astype(BF16), preferred_element_type=F32)
    return y * scale_ref[:, sl]


def _softmax_rows(s):
    m = jnp.max(s, axis=-1, keepdims=True)
    e = jnp.exp(s - m)
    return e / jnp.sum(e, axis=-1, keepdims=True)


def _mix_prompt_kernel(cur_ref, prev_ref, mk_ref, mv_ref, wpg_ref, scale_ref, wconv_ref,
                       br_ref, zst_ref, extu_ref, extz_ref):
    t = pl.program_id(1)
    tm = cur_ref.shape[0]
    has_prev = t > 0

    u = cur_ref[:, C_U:C_U + D_POOL]
    extu_ref[0:HIST, :] = jnp.where(has_prev, prev_ref[:, C_U:C_U + D_POOL], 0.0)
    extu_ref[HIST:HIST + tm, :] = u
    pos = t * tm + lax.broadcasted_iota(jnp.int32, (tm, 1), 0)
    for g, w in enumerate(POOL_WINDOWS):
        sl = slice(g * POOL_GROUP_DIM, (g + 1) * POOL_GROUP_DIM)
        s = extu_ref[HIST:HIST + tm, sl]
        for k in range(1, w):
            s = s + extu_ref[HIST - k:HIST - k + tm, sl]
        cnt = jnp.minimum(w, pos + 1).astype(F32)
        pooled = s / cnt - extu_ref[HIST:HIST + tm, sl]
        br_ref[:, sl] = _pool_project(pooled, wpg_ref, scale_ref, g).astype(BF16)

    z = cur_ref[:, C_C:C_C + D_CONV] * cur_ref[:, C_V:C_V + D_CONV]
    zprev = prev_ref[:, C_C:C_C + D_CONV] * prev_ref[:, C_V:C_V + D_CONV]
    extz_ref[0:HIST, :] = jnp.where(has_prev, zprev, 0.0)
    extz_ref[HIST:HIST + tm, :] = z
    y = extz_ref[HIST - 2:HIST - 2 + tm, :] * wconv_ref[0:1, :]
    y = y + extz_ref[HIST - 1:HIST - 1 + tm, :] * wconv_ref[1:2, :]
    y = y + extz_ref[HIST:HIST + tm, :] * wconv_ref[2:3, :]
    br_ref[:, D_POOL:D_POOL + D_CONV] = (cur_ref[:, C_B:C_B + D_CONV] * y).astype(BF16)
    zst_ref[0] = extz_ref[HIST + tm - 8:HIST + tm, :]

    for h in range(N_XHEADS):
        sl = slice(h * XHEAD_DIM, (h + 1) * XHEAD_DIM)
        qh = cur_ref[:, C_Q + h * XHEAD_DIM:C_Q + (h + 1) * XHEAD_DIM].astype(BF16)
        kh = mk_ref[0, :, sl].astype(BF16)
        vh = mv_ref[0, :, sl].astype(BF16)
        s = lax.dot_general(qh, kh, (((1,), (1,)), ((), ())), preferred_element_type=F32)
        p = _softmax_rows(s * (XHEAD_DIM ** -0.5))
        o = jnp.dot(p.astype(BF16), vh, preferred_element_type=F32)
        c0 = D_POOL + D_CONV + h * XHEAD_DIM
        br_ref[:, c0:c0 + XHEAD_DIM] = o.astype(BF16)


def _mix_prompt(proj, mk, mv, wpg, scale, wconv, batch, seq):
    tm = MIX_TM
    nt = seq // tm
    assert seq % tm == 0 and tm % HIST == 0
    rpb = tm // HIST
    return pl.pallas_call(
        _mix_prompt_kernel,
        grid=(batch, nt),
        in_specs=[
            pl.BlockSpec((tm, D_MIX), lambda b, t: (b * nt + t, 0)),
            pl.BlockSpec((HIST, D_MIX), lambda b, t: (jnp.maximum((b * nt + t) * rpb - 1, 0), 0)),
            pl.BlockSpec((1, N_MEM, D_XATTN), lambda b, t: (b, 0, 0)),
            pl.BlockSpec((1, N_MEM, D_XATTN), lambda b, t: (b, 0, 0)),
            pl.BlockSpec((len(POOL_WINDOWS), POOL_GROUP_DIM, POOL_GROUP_DIM), lambda b, t: (0, 0, 0)),
            pl.BlockSpec((1, D_POOL), lambda b, t: (0, 0)),
            pl.BlockSpec((CONV_WIDTH, D_CONV), lambda b, t: (0, 0)),
        ],
        out_specs=[
            pl.BlockSpec((tm, D_MODEL), lambda b, t: (b * nt + t, 0)),
            pl.BlockSpec((1, 8, D_CONV), lambda b, t: (b, 0, 0)),
        ],
        out_shape=[
            jax.ShapeDtypeStruct((batch * seq, D_MODEL), BF16),
            jax.ShapeDtypeStruct((batch, 8, D_CONV), F32),
        ],
        scratch_shapes=[pltpu.VMEM((HIST + tm, D_POOL), F32), pltpu.VMEM((HIST + tm, D_CONV), F32)],
        compiler_params=pltpu.CompilerParams(
            dimension_semantics=("arbitrary", "arbitrary"), vmem_limit_bytes=VMEM_LIMIT),
        name="mix_prompt",
    )(proj, proj, mk, mv, wpg, scale.reshape(1, D_POOL), wconv)


def _mix_sample_kernel(cur_ref, sp_ref, sc_ref, k_ref, v_ref, wpg_ref, scale_ref, wconv_ref,
                       br_ref, zst_ref, extu_ref, extz_ref):
    ns, ln = zst_ref.shape[0], zst_ref.shape[1]
    rows = ns * ln

    extu_ref[:, HIST - POOL_STATE_LEN:HIST, :] = sp_ref[...]
    extu_ref[:, HIST:HIST + ln, :] = cur_ref[:, C_U:C_U + D_POOL].reshape(ns, ln, D_POOL)
    for g, w in enumerate(POOL_WINDOWS):
        sl = slice(g * POOL_GROUP_DIM, (g + 1) * POOL_GROUP_DIM)
        s = extu_ref[:, HIST:HIST + ln, sl]
        for k in range(1, w):
            s = s + extu_ref[:, HIST - k:HIST - k + ln, sl]
        pooled = s / float(w) - extu_ref[:, HIST:HIST + ln, sl]
        pooled = pooled.reshape(rows, POOL_GROUP_DIM)
        br_ref[:, sl] = _pool_project(pooled, wpg_ref, scale_ref, g).astype(BF16)

    z = cur_ref[:, C_C:C_C + D_CONV] * cur_ref[:, C_V:C_V + D_CONV]
    extz_ref[:, HIST - 2:HIST, :] = sc_ref[...]
    extz_ref[:, HIST:HIST + ln, :] = z.reshape(ns, ln, D_CONV)
    y = extz_ref[:, HIST - 2:HIST - 2 + ln, :] * wconv_ref[0:1, :]
    y = y + extz_ref[:, HIST - 1:HIST - 1 + ln, :] * wconv_ref[1:2, :]
    y = y + extz_ref[:, HIST:HIST + ln, :] * wconv_ref[2:3, :]
    br_ref[:, D_POOL:D_POOL + D_CONV] = (
        cur_ref[:, C_B:C_B + D_CONV] * y.reshape(rows, D_CONV)).astype(BF16)
    zst_ref[...] = extz_ref[:, HIST:HIST + ln, :]

    q3 = cur_ref[:, C_Q:C_Q + D_XATTN].reshape(ns, ln, D_XATTN)
    q4 = jnp.concatenate([q3[:, :, h * XHEAD_DIM:(h + 1) * XHEAD_DIM] for h in range(N_XHEADS)], axis=1)
    s = jnp.einsum("nqd,nkd->nqk", q4.astype(BF16), k_ref[...].astype(BF16),
                   preferred_element_type=F32)
    row_head = lax.broadcasted_iota(jnp.int32, (N_XHEADS * ln, N_XHEADS * N_MEM), 0) // ln
    col_head = lax.broadcasted_iota(jnp.int32, (N_XHEADS * ln, N_XHEADS * N_MEM), 1) % N_XHEADS
    s = jnp.where((row_head == col_head)[None], s * (XHEAD_DIM ** -0.5), -jnp.inf)
    p = _softmax_rows(s)
    r = jnp.einsum("nqk,nkd->nqd", p.astype(BF16), v_ref[...].astype(BF16),
                   preferred_element_type=F32)
    for h in range(N_XHEADS):
        c0 = D_POOL + D_CONV + h * XHEAD_DIM
        br_ref[:, c0:c0 + XHEAD_DIM] = r[:, h * ln:(h + 1) * ln, :].reshape(rows, XHEAD_DIM).astype(BF16)


def _mix_sample(proj, state_pool, state_conv, mem_k, mem_v, wpg, scale, wconv, nseq, ln):
    ns = MIX_NS
    rows = ns * ln
    assert nseq % ns == 0 and ln == 8
    return pl.pallas_call(
        _mix_sample_kernel,
        grid=(nseq // ns,),
        in_specs=[
            pl.BlockSpec((rows, D_MIX), lambda s: (s, 0)),
            pl.BlockSpec((ns, POOL_STATE_LEN, D_POOL), lambda s: (s, 0, 0)),
            pl.BlockSpec((ns, CONV_WIDTH - 1, D_CONV), lambda s: (s, 0, 0)),
            pl.BlockSpec((ns, N_MEM * N_XHEADS, XHEAD_DIM), lambda s: (s, 0, 0)),
            pl.BlockSpec((ns, N_MEM * N_XHEADS, XHEAD_DIM), lambda s: (s, 0, 0)),
            pl.BlockSpec((len(POOL_WINDOWS), POOL_GROUP_DIM, POOL_GROUP_DIM), lambda s: (0, 0, 0)),
            pl.BlockSpec((1, D_POOL), lambda s: (0, 0)),
            pl.BlockSpec((CONV_WIDTH, D_CONV), lambda s: (0, 0)),
        ],
        out_specs=[
            pl.BlockSpec((rows, D_MODEL), lambda s: (s, 0)),
            pl.BlockSpec((ns, ln, D_CONV), lambda s: (s, 0, 0)),
        ],
        out_shape=[
            jax.ShapeDtypeStruct((nseq * ln, D_MODEL), BF16),
            jax.ShapeDtypeStruct((nseq, ln, D_CONV), F32),
        ],
        scratch_shapes=[pltpu.VMEM((ns, HIST + ln, D_POOL), F32),
                        pltpu.VMEM((ns, HIST + ln, D_CONV), F32)],
        compiler_params=pltpu.CompilerParams(
            dimension_semantics=("arbitrary",), vmem_limit_bytes=VMEM_LIMIT),
        name="mix_sample",
    )(proj, state_pool, state_conv, mem_k, mem_v, wpg, scale.reshape(1, D_POOL), wconv)


def _merge_route_kernel(x_ref, br_ref, g0_ref, g1_ref, g2_ref, wpo_ref, wco_ref, wao_ref, wo_ref,
                        nf_ref, wrhl_ref, wrh_ref, brt_ref,
                        x1_ref, h2_ref, ri_ref, rw_ref, cnt_ref, carry_ref):
    i = pl.program_id(0)
    tm = x_ref.shape[0]

    @pl.when(i == 0)
    def _():
        carry_ref[...] = jnp.zeros_like(carry_ref)

    merged = _sigmoid(g0_ref[...].astype(F32)) * jnp.dot(
        br_ref[:, 0:D_POOL], wpo_ref[...], preferred_element_type=F32)
    merged = merged + _sigmoid(g1_ref[...].astype(F32)) * jnp.dot(
        br_ref[:, D_POOL:D_POOL + D_CONV], wco_ref[...], preferred_element_type=F32)
    merged = merged + _sigmoid(g2_ref[...].astype(F32)) * jnp.dot(
        br_ref[:, D_POOL + D_CONV:D_MODEL], wao_ref[...], preferred_element_type=F32)
    x1 = x_ref[...] + jnp.dot(merged.astype(BF16), wo_ref[...], preferred_element_type=F32)
    x1_ref[...] = x1
    h2 = _rms(x1, nf_ref[...])
    h2_ref[...] = _pack_bf16_pairs(h2)

    h2_hi = h2.astype(BF16)
    h2_lo = (h2 - h2_hi.astype(F32)).astype(BF16)
    hi_part = jnp.dot(h2_hi, wrhl_ref[...], preferred_element_type=F32)
    logits = (hi_part[:, 0:N_EXPERTS] + hi_part[:, N_EXPERTS:2 * N_EXPERTS]
              + jnp.dot(h2_lo, wrh_ref[...], preferred_element_type=F32)) + brt_ref[...]
    lane = lax.broadcasted_iota(jnp.int32, (tm, N_EXPERTS), 1).astype(F32)
    vals, idxs, hots = [], [], []
    work = logits
    for _ in range(TOP_K):
        m = jnp.max(work, axis=-1, keepdims=True)
        idx = jnp.min(jnp.where(work == m, lane, float(N_EXPERTS)), axis=-1, keepdims=True)
        hot = lane == idx
        work = jnp.where(hot, -jnp.inf, work)
        vals.append(m)
        idxs.append(idx)
        hots.append(hot)
    es = [jnp.exp(v - vals[0]) for v in vals]
    denom = es[0] + es[1] + es[2] + es[3]

    chosen = jnp.where(hots[0] | hots[1] | hots[2] | hots[3], 1.0, 0.0).astype(BF16)
    r_i = lax.broadcasted_iota(jnp.int32, (tm, tm), 0)
    c_i = lax.broadcasted_iota(jnp.int32, (tm, tm), 1)
    lower = jnp.where(c_i < r_i, 1.0, 0.0).astype(BF16)
    before = jnp.dot(lower, chosen, preferred_element_type=F32) + carry_ref[0:1, 0:N_EXPERTS]
    carry_ref[0:1, 0:N_EXPERTS] = (carry_ref[0:1, 0:N_EXPERTS]
                                   + jnp.sum(chosen.astype(F32), axis=0, keepdims=True))
    cnt_ref[...] = carry_ref[...]

    out_lane = lax.broadcasted_iota(jnp.int32, (tm, LANES), 1)
    ri = jnp.zeros((tm, LANES), jnp.int32)
    rw = jnp.zeros((tm, LANES), F32)
    for k in range(TOP_K):
        rank = jnp.sum(jnp.where(hots[k], before, 0.0), axis=-1, keepdims=True).astype(jnp.int32)
        ri = jnp.where(out_lane == k, idxs[k].astype(jnp.int32), ri)
        ri = jnp.where(out_lane == TOP_K + k, rank, ri)
        rw = jnp.where(out_lane == k, es[k] / denom, rw)
    ri_ref[...] = ri
    rw_ref[...] = rw


def _merge_route(x, branch, gates, wpo, wco, wao, wo, norm_ffn, w_router, b_router, name):
    t = x.shape[0]
    tm = MERGE_TM
    assert t % tm == 0 and gates.shape == (t, N_BRANCH * D_MODEL)
    const = lambda i: (0, 0)
    wr_hi = w_router.astype(BF16)
    wr_lo = (w_router - wr_hi.astype(F32)).astype(BF16)
    return pl.pallas_call(
        _merge_route_kernel,
        grid=(t // tm,),
        in_specs=[
            pl.BlockSpec((tm, D_MODEL), lambda i: (i, 0)),
            pl.BlockSpec((tm, D_MODEL), lambda i: (i, 0)),
            pl.BlockSpec((tm, D_MODEL), lambda i: (i, 0)),
            pl.BlockSpec((tm, D_MODEL), lambda i: (i, 1)),
            pl.BlockSpec((tm, D_MODEL), lambda i: (i, 2)),
            pl.BlockSpec((D_POOL, D_MODEL), const, pipeline_mode=pl.Buffered(1)),
            pl.BlockSpec((D_CONV, D_MODEL), const, pipeline_mode=pl.Buffered(1)),
            pl.BlockSpec((D_XATTN, D_MODEL), const, pipeline_mode=pl.Buffered(1)),
            pl.BlockSpec((D_MODEL, D_MODEL), const, pipeline_mode=pl.Buffered(1)),
            pl.BlockSpec((1, D_MODEL), const),
            pl.BlockSpec((D_MODEL, 2 * N_EXPERTS), const),
            pl.BlockSpec((D_MODEL, N_EXPERTS), const),
            pl.BlockSpec((1, N_EXPERTS), const),
        ],
        out_specs=[
            pl.BlockSpec((tm, D_MODEL), lambda i: (i, 0)),
            pl.BlockSpec((tm, D_MODEL // 2), lambda i: (i, 0)),
            pl.BlockSpec((tm, LANES), lambda i: (i, 0)),
            pl.BlockSpec((tm, LANES), lambda i: (i, 0)),
            pl.BlockSpec((8, LANES), const),
        ],
        out_shape=[
            jax.ShapeDtypeStruct((t, D_MODEL), F32),
            jax.ShapeDtypeStruct((t, D_MODEL // 2), PACKED),
            jax.ShapeDtypeStruct((t, LANES), jnp.int32),
            jax.ShapeDtypeStruct((t, LANES), F32),
            jax.ShapeDtypeStruct((8, LANES), F32),
        ],
        scratch_shapes=[pltpu.VMEM((8, LANES), F32)],
        compiler_params=pltpu.CompilerParams(
            dimension_semantics=("arbitrary",), vmem_limit_bytes=VMEM_LIMIT),
        name=name,
    )(x, branch, gates, gates, gates, wpo, wco, wao, wo, norm_ffn.reshape(1, D_MODEL),
      jnp.concatenate([wr_hi, wr_lo], axis=1), wr_hi, b_router.reshape(1, N_EXPERTS))


def _fill_work_list(nblk_s, pad_s, nj, refs):
    te, tj, trow, tnb, tfirst, tne, tnj, tpar, ttot = refs

    def clear(i, c):
        for r in (te, tj, trow, tnb, tfirst, tpar):
            r[i] = 0
        tne[i] = -1
        tnj[i] = -1
        return c

    lax.fori_loop(0, te.shape[0], clear, 0)

    carry = (jnp.int32(0), jnp.int32(-1), jnp.int32(0))
    for j in range(nj):
        def per_expert(e, carry):
            q, prev, groups = carry
            n = nblk_s[e]
            nch = (n + EXP_CHUNK - 1) // EXP_CHUNK

            @pl.when((nch > 0) & (prev >= 0))
            def _():
                tne[prev] = e
                tnj[prev] = jnp.int32(j)

            def per_chunk(c, q):
                te[q] = e
                tj[q] = jnp.int32(j)
                trow[q] = pad_s[e] + c * (EXP_CHUNK * EXP_TM)
                tnb[q] = jnp.minimum(EXP_CHUNK, n - c * EXP_CHUNK)
                tfirst[q] = (c == 0).astype(jnp.int32)
                tpar[q] = lax.rem(groups, 2)
                return q + 1

            return (lax.fori_loop(0, nch, per_chunk, q), jnp.where(nch > 0, q, prev),
                    groups + (nch > 0).astype(jnp.int32))

        carry = lax.fori_loop(0, N_EXPERTS, per_expert, carry)
    ttot[0] = carry[0]


def _route_tables_kernel(cntp_ref, cnts_ref, rip_ref, ris_ref, dest_ref, *refs, nj_in, nj_out):
    tabs_in, tabs_out = refs[0:N_TABS], refs[N_TABS:2 * N_TABS]
    nblk_s, pad_s = refs[2 * N_TABS:]

    def per_expert(e, start_blk):
        n = (cntp_ref[e] + cnts_ref[e] + EXP_TM - 1) // EXP_TM
        nblk_s[e] = n
        pad_s[e] = start_blk * EXP_TM
        return start_blk + n

    lax.fori_loop(0, N_EXPERTS, per_expert, jnp.int32(0))
    _fill_work_list(nblk_s, pad_s, nj_in, tabs_in)
    _fill_work_list(nblk_s, pad_s, nj_out, tabs_out)

    tile = ROUTE_TM
    col0 = 0
    for ri_ref, is_sample in ((rip_ref, False), (ris_ref, True)):
        for r in range(ri_ref.shape[0] // tile):
            ri = ri_ref[r * tile:(r + 1) * tile, :]
            base = jnp.zeros_like(ri)
            for e in range(N_EXPERTS):
                first_row = pad_s[e] + cntp_ref[e] if is_sample else pad_s[e]
                base = jnp.where(ri == e, first_row, base)
            dest = base + pltpu.roll(ri, LANES - TOP_K, axis=1)
            dest_ref[:, col0:col0 + tile] = jnp.transpose(dest)[0:8, :]
            col0 += tile


def _route_tables(cnt_p, cnt_s, ri_p, ri_s, nj_in, nj_out, max_items):
    t_all = ri_p.shape[0] + ri_s.shape[0]
    assert ri_p.shape[0] % ROUTE_TM == 0 and ri_s.shape[0] % ROUTE_TM == 0
    smem = pl.BlockSpec(memory_space=pltpu.SMEM)

    def tables(nj):
        n = nj * max_items
        return [jax.ShapeDtypeStruct((n,), jnp.int32)] * (N_TABS - 1) + [jax.ShapeDtypeStruct((1,), jnp.int32)]

    out = pl.pallas_call(
        functools.partial(_route_tables_kernel, nj_in=nj_in, nj_out=nj_out),
        grid=(1,),
        in_specs=[smem, smem,
                  pl.BlockSpec(ri_p.shape, lambda i: (0, 0)), pl.BlockSpec(ri_s.shape, lambda i: (0, 0))],
        out_specs=[pl.BlockSpec((8, t_all), lambda i: (0, 0))] + [smem] * (2 * N_TABS),
        out_shape=[jax.ShapeDtypeStruct((8, t_all), jnp.int32)] + tables(nj_in) + tables(nj_out),
        scratch_shapes=[pltpu.SMEM((N_EXPERTS,), jnp.int32), pltpu.SMEM((N_EXPERTS,), jnp.int32)],
        compiler_params=pltpu.CompilerParams(
            dimension_semantics=("arbitrary",), vmem_limit_bytes=VMEM_LIMIT),
        name="route_tables",
    )(cnt_p, cnt_s, ri_p, ri_s)
    return out[0], tuple(out[1:1 + N_TABS]), tuple(out[1 + N_TABS:1 + 2 * N_TABS])


def _sc_mesh():
    return plsc.VectorSubcoreMesh(core_axis_name="c", subcore_axis_name="s",
                                  num_cores=SC_CORES, num_subcores=SC_SUBCORES)


def _sc_worker_id():
    return lax.axis_index("s") * SC_CORES + lax.axis_index("c")


def _sc_dispatch(h2_a, h2_b, dest_by_slot, rows_out):
    ta, w = h2_a.shape
    t = ta + h2_b.shape[0]
    per_w = t // SC_WORKERS
    n_chunks = per_w // SC_CHUNK
    assert t == SC_WORKERS * n_chunks * SC_CHUNK and ta % SC_CHUNK == 0
    idx = dest_by_slot[:TOP_K].reshape(TOP_K, SC_WORKERS, n_chunks, SC_CHUNK).transpose(1, 0, 2, 3)

    def body(a_hbm, b_hbm, idx_hbm, xs_hbm, idx_v, rows_v, sems):
        base = _sc_worker_id() * per_w
        pltpu.sync_copy(idx_hbm.at[_sc_worker_id()], idx_v)
        pending = [[], []]
        for c in range(n_chunks):
            b = c % 2
            for d in pending[b]:
                d.wait()
            tok0 = base + c * SC_CHUNK

            @pl.when(tok0 < ta)
            def _():
                pltpu.sync_copy(a_hbm.at[pl.ds(tok0, SC_CHUNK)], rows_v.at[b])

            @pl.when(tok0 >= ta)
            def _():
                pltpu.sync_copy(b_hbm.at[pl.ds(tok0 - ta, SC_CHUNK)], rows_v.at[b])

            pending[b] = [pltpu.async_copy(rows_v.at[b], xs_hbm.at[idx_v.at[k, c]], sems.at[b])
                          for k in range(TOP_K)]
        for b in range(2):
            for d in pending[b]:
                d.wait()

    return pl.kernel(
        body,
        out_type=jax.ShapeDtypeStruct((rows_out, w), h2_a.dtype),
        mesh=_sc_mesh(),
        scratch_types=[pltpu.VMEM((TOP_K, n_chunks, SC_CHUNK), jnp.int32),
                       pltpu.VMEM((2, SC_CHUNK, w), h2_a.dtype),
                       pltpu.SemaphoreType.DMA((2,))],
        name="sc_dispatch",
    )(h2_a, h2_b, idx)


def _sc_gather_rows(table, idx):
    n = idx.shape[0]
    w = table.shape[1]
    per_w = n // SC_WORKERS
    n_chunks = per_w // SC_CHUNK
    assert n == SC_WORKERS * n_chunks * SC_CHUNK
    idx3 = idx.reshape(SC_WORKERS, n_chunks, SC_CHUNK)

    def body(table_hbm, idx_hbm, out_hbm, idx_v, rows_v, gsem, wsems):
        base = _sc_worker_id() * per_w
        pltpu.sync_copy(idx_hbm.at[_sc_worker_id()], idx_v)
        pending = [None, None]
        for c in range(n_chunks):
            b = c % 2
            if pending[b] is not None:
                pending[b].wait()
            pltpu.async_copy(table_hbm.at[idx_v.at[c]], rows_v.at[b], gsem).wait()
            pending[b] = pltpu.async_copy(
                rows_v.at[b], out_hbm.at[pl.ds(base + c * SC_CHUNK, SC_CHUNK)], wsems.at[b])
        for b in range(2):
            if pending[b] is not None:
                pending[b].wait()

    return pl.kernel(
        body,
        out_type=jax.ShapeDtypeStruct((n, w), table.dtype),
        mesh=_sc_mesh(),
        scratch_types=[pltpu.VMEM((n_chunks, SC_CHUNK), jnp.int32),
                       pltpu.VMEM((2, SC_CHUNK, w), table.dtype),
                       pltpu.SemaphoreType.DMA,
                       pltpu.SemaphoreType.DMA((2,))],
        name="sc_gather_rows",
    )(table, idx3)


def _grouped_pipeline(tabs, total_ref, weight_copies, in_copy, out_copy, compute):
    te, tj, _, tnb, tfirst, tne, tnj, tpar = tabs
    total = total_ref[0]

    def wait_out(q):
        for m in range(1, EXP_CHUNK + 1):
            @pl.when(tnb[q] == m)
            def _():
                out_copy(q, lax.rem(q, 2), m).wait()

    for d in weight_copies(te[0], tj[0], tpar[0]):
        d.start()
    in_copy(0, 0).start()

    def body(q, carry):
        slot = lax.rem(q, 2)

        @pl.when(q + 1 < total)
        def _():
            in_copy(q + 1, 1 - slot).start()

        in_copy(q, slot).wait()

        @pl.when(tfirst[q] == 1)
        def _():
            for d in weight_copies(te[q], tj[q], tpar[q]):
                d.wait()

            @pl.when(tne[q] >= 0)
            def _():
                for d in weight_copies(tne[q], tnj[q], 1 - tpar[q]):
                    d.start()

        @pl.when(q >= 2)
        def _():
            wait_out(q - 2)

        for m in range(1, EXP_CHUNK + 1):
            @pl.when(tnb[q] == m)
            def _():
                compute(q, slot, m, tpar[q])
                out_copy(q, slot, m).start()

        return carry

    lax.fori_loop(0, total, body, 0)

    @pl.when(total >= 2)
    def _():
        wait_out(total - 2)

    wait_out(total - 1)


def _expert_in_kernel(*refs):
    tabs, total_ref = refs[:N_TABS - 1], refs[N_TABS - 1]
    xs_hbm, w_hbm, b_ref, act_hbm, wbuf, x_buf, a_buf, w_sem, x_sem, a_sem = refs[N_TABS:]
    te, tj, trow = tabs[:3]
    tm, tn = EXP_TM, a_buf.shape[2]
    nj = D_FF // tn

    def weight_copies(e, j, par):
        return [pltpu.make_async_copy(
            w_hbm.at[e, :, pl.ds(pl.multiple_of(half * D_FF + j * tn, tn), tn)],
            wbuf.at[2 * par + half], w_sem)
            for half in range(2)]

    def rows(q, m):
        return pl.ds(pl.multiple_of(trow[q], tm), m * tm)

    def x_copy(q, slot):
        return pltpu.make_async_copy(xs_hbm.at[rows(q, EXP_CHUNK)], x_buf.at[slot], x_sem.at[slot])

    def a_copy(q, slot, m):
        return pltpu.make_async_copy(
            a_buf.at[slot, pl.ds(0, m * tm)],
            act_hbm.at[rows(q, m), pl.ds(pl.multiple_of(tj[q] * tn, tn), tn)], a_sem.at[slot])

    def compute(q, slot, m, par):
        bias0 = te[q] * (2 * nj) + tj[q]
        x = jnp.concatenate(_unpack_bf16_pairs(x_buf[slot, 0:m * tm]), axis=1).astype(BF16)
        g = jnp.dot(x, wbuf[2 * par].astype(BF16), preferred_element_type=F32) + b_ref[bias0]
        up = jnp.dot(x, wbuf[2 * par + 1].astype(BF16), preferred_element_type=F32) + b_ref[bias0 + nj]
        g = jnp.minimum(g, SWIGLU_LIMIT)
        up = jnp.clip(up, -SWIGLU_LIMIT, SWIGLU_LIMIT)
        a_buf[slot, 0:m * tm] = (g * _sigmoid(SWIGLU_ALPHA * g) * (up + 1.0)).astype(BF16)

    _grouped_pipeline(tabs, total_ref, weight_copies, x_copy, a_copy, compute)


def _expert_in(tabs, xs, w_in, b_in):
    rows = xs.shape[0]
    tm, tn = EXP_TM, EXP_IN_TN
    nj = D_FF // tn
    bias = b_in.reshape(N_EXPERTS * 2 * nj, 1, tn)
    grid_spec = pltpu.PrefetchScalarGridSpec(
        num_scalar_prefetch=len(tabs),
        grid=(1,),
        in_specs=[
            pl.BlockSpec(memory_space=pl.ANY),
            pl.BlockSpec(memory_space=pl.ANY),
            pl.BlockSpec(bias.shape, lambda i, *_: (0, 0, 0), pipeline_mode=pl.Buffered(1)),
        ],
        out_specs=pl.BlockSpec(memory_space=pl.ANY),
        scratch_shapes=[pltpu.VMEM((4, D_MODEL, tn), F32),
                        pltpu.VMEM((2, EXP_CHUNK * tm, D_MODEL // 2), PACKED),
                        pltpu.VMEM((2, EXP_CHUNK * tm, tn), BF16),
                        pltpu.SemaphoreType.DMA, pltpu.SemaphoreType.DMA((2,)),
                        pltpu.SemaphoreType.DMA((2,))],
    )
    return pl.pallas_call(
        _expert_in_kernel,
        grid_spec=grid_spec,
        out_shape=jax.ShapeDtypeStruct((rows, D_FF), BF16),
        compiler_params=pltpu.CompilerParams(
            dimension_semantics=("arbitrary",), vmem_limit_bytes=VMEM_LIMIT),
        name="expert_in",
    )(*tabs, xs, w_in, bias)


def _expert_out_kernel(*refs):
    tabs, total_ref = refs[:N_TABS - 1], refs[N_TABS - 1]
    act_hbm, w_hbm, b_ref, ys_hbm, wbuf, a_buf, y_buf, w_sem, a_sem, y_sem = refs[N_TABS:]
    te, tj, trow = tabs[:3]
    tm, tnp = EXP_TM, y_buf.shape[2]
    tn = 2 * tnp
    nj = D_MODEL // tn

    def weight_copies(e, j, par):
        return [pltpu.make_async_copy(
            w_hbm.at[e, :, pl.ds(pl.multiple_of(j * tn, tn), tn)], wbuf.at[par], w_sem)]

    def rows(q, m):
        return pl.ds(pl.multiple_of(trow[q], tm), m * tm)

    def a_copy(q, slot):
        return pltpu.make_async_copy(act_hbm.at[rows(q, EXP_CHUNK)], a_buf.at[slot], a_sem.at[slot])

    def y_copy(q, slot, m):
        return pltpu.make_async_copy(
            y_buf.at[slot, pl.ds(0, m * tm)],
            ys_hbm.at[rows(q, m), pl.ds(pl.multiple_of(tj[q] * tnp, tnp), tnp)], y_sem.at[slot])

    def compute(q, slot, m, par):
        y = jnp.dot(a_buf[slot, 0:m * tm], wbuf[par].astype(BF16),
                    preferred_element_type=F32) + b_ref[te[q] * nj + tj[q]]
        y_buf[slot, 0:m * tm] = _pack_bf16_pairs(y)

    _grouped_pipeline(tabs, total_ref, weight_copies, a_copy, y_copy, compute)


def _expert_out(tabs, act, w_out, b_out):
    rows = act.shape[0]
    tm, tn = EXP_TM, EXP_OUT_TN
    nj = D_MODEL // tn
    bias = b_out.reshape(N_EXPERTS * nj, 1, tn)
    grid_spec = pltpu.PrefetchScalarGridSpec(
        num_scalar_prefetch=len(tabs),
        grid=(1,),
        in_specs=[
            pl.BlockSpec(memory_space=pl.ANY),
            pl.BlockSpec(memory_space=pl.ANY),
            pl.BlockSpec(bias.shape, lambda i, *_: (0, 0, 0), pipeline_mode=pl.Buffered(1)),
        ],
        out_specs=pl.BlockSpec(memory_space=pl.ANY),
        scratch_shapes=[pltpu.VMEM((2, D_FF, tn), F32),
                        pltpu.VMEM((2, EXP_CHUNK * tm, D_FF), BF16),
                        pltpu.VMEM((2, EXP_CHUNK * tm, tn // 2), PACKED),
                        pltpu.SemaphoreType.DMA, pltpu.SemaphoreType.DMA((2,)),
                        pltpu.SemaphoreType.DMA((2,))],
    )
    return pl.pallas_call(
        _expert_out_kernel,
        grid_spec=grid_spec,
        out_shape=jax.ShapeDtypeStruct((rows, D_MODEL // 2), PACKED),
        compiler_params=pltpu.CompilerParams(
            dimension_semantics=("arbitrary",), vmem_limit_bytes=VMEM_LIMIT),
        name="expert_out",
    )(*tabs, act, w_out, bias)


def _combine_kernel(x1_ref, yg_ref, rw_ref, g_ref, o_ref):
    half = EXP_OUT_TN // 2
    pieces = []
    ssq = jnp.zeros((x1_ref.shape[0], 1), F32)
    for jt in range(D_MODEL // EXP_OUT_TN):
        acc_lo = x1_ref[:, jt * EXP_OUT_TN:jt * EXP_OUT_TN + half]
        acc_hi = x1_ref[:, jt * EXP_OUT_TN + half:(jt + 1) * EXP_OUT_TN]
        moe_lo = jnp.zeros_like(acc_lo)
        moe_hi = jnp.zeros_like(acc_hi)
        for k in range(TOP_K):
            lo, hi = _unpack_bf16_pairs(yg_ref[k, :, jt * half:(jt + 1) * half])
            moe_lo = moe_lo + lo * rw_ref[:, k:k + 1]
            moe_hi = moe_hi + hi * rw_ref[:, k:k + 1]
        for acc in (acc_lo + moe_lo, acc_hi + moe_hi):
            ssq = ssq + jnp.sum(acc * acc, axis=-1, keepdims=True)
            pieces.append(acc)
    inv = lax.rsqrt(ssq / D_MODEL + EPS)
    for n, acc in enumerate(pieces):
        o_ref[:, n * half:(n + 1) * half] = acc * inv * g_ref[:, n * half:(n + 1) * half]


def _combine_part_kernel(x1_ref, yg_ref, rw_ref, g_ref, prev_hbm, o_ref):
    del prev_hbm
    _combine_kernel(x1_ref, yg_ref, rw_ref, g_ref, o_ref)


def _combine(x1, yg, rw, gain, row0, prev, name):
    t = x1.shape[0]
    n = yg.shape[1]
    tm = COMB_TM
    assert n % tm == 0 and row0 % tm == 0
    off = row0 // tm
    in_specs = [
        pl.BlockSpec((tm, D_MODEL), lambda i: (i + off, 0)),
        pl.BlockSpec((TOP_K, tm, D_MODEL // 2), lambda i: (0, i, 0)),
        pl.BlockSpec((tm, LANES), lambda i: (i + off, 0)),
        pl.BlockSpec((1, D_MODEL), lambda i: (0, 0)),
    ]
    args = [x1, yg, rw, gain.reshape(1, D_MODEL)]
    if prev is not None:
        in_specs.append(pl.BlockSpec(memory_space=pl.ANY))
        args.append(prev)
    return pl.pallas_call(
        _combine_kernel if prev is None else _combine_part_kernel,
        grid=(n // tm,),
        in_specs=in_specs,
        out_specs=pl.BlockSpec((tm, D_MODEL), lambda i: (i + off, 0)),
        out_shape=jax.ShapeDtypeStruct((t, D_MODEL), F32),
        input_output_aliases={} if prev is None else {len(args) - 1: 0},
        compiler_params=pltpu.CompilerParams(
            dimension_semantics=("arbitrary",), vmem_limit_bytes=VMEM_LIMIT),
        name=name,
    )(*args)


def kernel(x_prompt, x_sample, state_pool, state_conv, cache_mem_k, cache_mem_v, mem_prompt,
           norm_mix, w_in, b_gate, w_pool_group, pool_scale, w_conv, mem_norm, w_mem_kv,
           w_pool_out, w_conv_out, w_attn_out, w_o, norm_ffn, w_router, b_router,
           w_exp_in, b_exp_in, w_exp_out, b_exp_out, final_norm):
    depth = norm_mix.shape[0]
    assert depth == 1
    l = 0
    bp, seq, _ = x_prompt.shape
    bs, ln, _ = x_sample.shape
    tp, ts = bp * seq, bs * ln

    kv = _norm_matmul(mem_prompt.reshape(bp * N_MEM, D_MODEL), mem_norm[l], w_mem_kv[l],
                      jnp.zeros((2 * D_XATTN,), F32), "mem_kv")
    mk = kv[:, :D_XATTN].reshape(bp, N_MEM, D_XATTN)
    mv = kv[:, D_XATTN:].reshape(bp, N_MEM, D_XATTN)

    bias_in = jnp.concatenate([jnp.zeros((D_MIX,), F32), b_gate[l]])
    xp = x_prompt.reshape(tp, D_MODEL)
    xs_ = x_sample.reshape(ts, D_MODEL)
    proj_p, gates_p = _norm_matmul_split(xp, norm_mix[l], w_in[l], bias_in, D_MIX, "proj_prompt")
    proj_s, gates_s = _norm_matmul_split(xs_, norm_mix[l], w_in[l], bias_in, D_MIX, "proj_sample")

    br_p, zst_p = _mix_prompt(proj_p, mk, mv, w_pool_group[l], pool_scale[l], w_conv[l], bp, seq)
    br_s, zst_s = _mix_sample(proj_s, state_pool[l], state_conv[l],
                              cache_mem_k[l].reshape(bs, N_MEM * N_XHEADS, XHEAD_DIM),
                              cache_mem_v[l].reshape(bs, N_MEM * N_XHEADS, XHEAD_DIM),
                              w_pool_group[l], pool_scale[l], w_conv[l], bs, ln)

    wpo, wco, wao, wo = (w.astype(BF16) for w in (w_pool_out[l], w_conv_out[l], w_attn_out[l], w_o[l]))
    x1_p, h2_p, ri_p, rw_p, cnt_p = _merge_route(xp, br_p, gates_p, wpo, wco, wao, wo, norm_ffn[l],
                                                  w_router[l], b_router[l], "merge_route_prompt")
    x1_s, h2_s, ri_s, rw_s, cnt_s = _merge_route(xs_, br_s, gates_s, wpo, wco, wao, wo, norm_ffn[l],
                                                  w_router[l], b_router[l], "merge_route_sample")

    t_all = tp + ts
    n_assign = t_all * TOP_K
    nb_max = n_assign // EXP_TM + N_EXPERTS
    max_items = nb_max // EXP_CHUNK + N_EXPERTS
    dest, tabs_in, tabs_out = _route_tables(
        cnt_p[0, :N_EXPERTS].astype(jnp.int32), cnt_s[0, :N_EXPERTS].astype(jnp.int32), ri_p, ri_s,
        D_FF // EXP_IN_TN, D_MODEL // EXP_OUT_TN, max_items)
    xs_sorted = _sc_dispatch(h2_p, h2_s, dest, (nb_max + EXP_CHUNK) * EXP_TM)

    act = _expert_in(tabs_in, xs_sorted, w_exp_in[l], b_exp_in[l])
    ys = _expert_out(tabs_out, act, w_exp_out[l], b_exp_out[l])
    def combine_part(x1, rw, tok0, n, row0, prev, name):
        idx = dest[:TOP_K, tok0:tok0 + n].reshape(-1)
        yg = _sc_gather_rows(ys, idx).reshape(TOP_K, n, D_MODEL // 2)
        return _combine(x1, yg, rw, final_norm, row0, prev, name)

    y_s = combine_part(x1_s, rw_s, tp, ts, 0, None, "combine_sample")
    y_p = None
    tok0 = 0
    for c, units in enumerate(COMB_PARTS):
        n = tp * units // sum(COMB_PARTS)
        y_p = combine_part(x1_p, rw_p, tok0, n, tok0, y_p, f"combine_prompt_{c}")
        tok0 += n
    assert tok0 == tp

    new_pool_p = proj_p.reshape(bp, seq, D_MIX)[:, seq - POOL_STATE_LEN:, :D_POOL]
    new_conv_p = zst_p[:, 8 - (CONV_WIDTH - 1):]
    u_s = proj_s[:, :D_POOL].reshape(bs, ln, D_POOL)
    new_pool_s = jnp.concatenate([state_pool[l], u_s], axis=1)[:, -POOL_STATE_LEN:]
    new_conv_s = zst_s[:, ln - (CONV_WIDTH - 1):]

    return (y_p.reshape(bp, seq, D_MODEL), y_s.reshape(bs, ln, D_MODEL),
            new_pool_p[None], new_conv_p[None],
            mk.reshape(1, bp, N_MEM, N_XHEADS, XHEAD_DIM), mv.reshape(1, bp, N_MEM, N_XHEADS, XHEAD_DIM),
            new_pool_s[None], new_conv_s[None])
```

```python
import functools

import jax
import jax.numpy as jnp
from jax import lax
from jax.experimental import pallas as pl
from jax.experimental.pallas import tpu as pltpu
from jax.experimental.pallas import tpu_sc as plsc

F32 = jnp.float32
BF16 = jnp.bfloat16
PACKED = jnp.int32

D_MODEL = 2048
POOL_WINDOWS = (2, 4, 8, 16)
POOL_GROUP_DIM = 128
D_POOL = 512
POOL_STATE_LEN = 15
D_CONV = 1024
CONV_WIDTH = 3
N_MEM = 256
N_XHEADS = 4
XHEAD_DIM = 128
D_XATTN = 512
N_BRANCH = 3
D_MIX = D_POOL + 3 * D_CONV + D_XATTN
D_IN_TOTAL = D_MIX + N_BRANCH * D_MODEL
N_EXPERTS = 32
TOP_K = 4
D_FF = D_MODEL
SWIGLU_LIMIT = 7.0
SWIGLU_ALPHA = 1.702
EPS = 1e-5

C_U = 0
C_V = D_POOL
C_B = D_POOL + D_CONV
C_C = D_POOL + 2 * D_CONV
C_Q = D_POOL + 3 * D_CONV

LANES = 128
HIST = 16

PROJ_TM = 1024
PROJ_TN = 1024
MIX_TM = 512
MIX_NS = 8
MERGE_TM = 256
ROUTE_TM = 1024
N_TABS = 9
EXP_TM = 128
EXP_CHUNK = 4
EXP_IN_TN = 1024
EXP_OUT_TN = 2048
COMB_TM = 512
COMB_PARTS = (2, 3, 3)
SC_CORES = 2
SC_SUBCORES = 16
SC_WORKERS = SC_CORES * SC_SUBCORES
SC_CHUNK = 32
VMEM_LIMIT = 56 * 1024 * 1024


def _sigmoid(x):
    return 0.5 * (jnp.tanh(0.5 * x) + 1.0)


def _rms(x, g):
    ms = jnp.mean(x * x, axis=-1, keepdims=True)
    return x * lax.rsqrt(ms + EPS) * g


def _pack_bf16_pairs(x):
    n = x.shape[1] // 2
    lo = lax.bitcast_convert_type(x[:, :n].astype(BF16).astype(F32), jnp.uint32)
    hi = lax.bitcast_convert_type(x[:, n:].astype(BF16).astype(F32), jnp.uint32)
    return lax.bitcast_convert_type((hi & jnp.uint32(0xFFFF0000)) | (lo >> 16), PACKED)


def _unpack_bf16_pairs(p):
    p = lax.bitcast_convert_type(p, jnp.uint32)
    lo = lax.bitcast_convert_type(p << 16, F32)
    hi = lax.bitcast_convert_type(p & jnp.uint32(0xFFFF0000), F32)
    return lo, hi


def _norm_matmul_kernel(x_ref, g_ref, w_ref, b_ref, o_ref, h_ref):
    @pl.when(pl.program_id(1) == 0)
    def _():
        h_ref[...] = _rms(x_ref[...], g_ref[...]).astype(BF16)

    o_ref[...] = jnp.dot(h_ref[...], w_ref[...].astype(BF16), preferred_element_type=F32) + b_ref[...]


def _norm_matmul(x, gain, w, bias, name):
    t, d = x.shape
    n = w.shape[1]
    tm = min(PROJ_TM, t)
    tn = PROJ_TN
    assert t % tm == 0 and n % tn == 0
    return pl.pallas_call(
        _norm_matmul_kernel,
        grid=(t // tm, n // tn),
        in_specs=[
            pl.BlockSpec((tm, d), lambda i, j: (i, 0)),
            pl.BlockSpec((1, d), lambda i, j: (0, 0)),
            pl.BlockSpec((d, tn), lambda i, j: (0, j)),
            pl.BlockSpec((1, tn), lambda i, j: (0, j)),
        ],
        out_specs=pl.BlockSpec((tm, tn), lambda i, j: (i, j)),
        out_shape=jax.ShapeDtypeStruct((t, n), F32),
        scratch_shapes=[pltpu.VMEM((tm, d), BF16)],
        compiler_params=pltpu.CompilerParams(
            dimension_semantics=("arbitrary", "arbitrary"), vmem_limit_bytes=VMEM_LIMIT),
        name=name,
    )(x, gain.reshape(1, d), w, bias.reshape(1, n))


def _norm_matmul_split_kernel(x_ref, g_ref, w_ref, b_ref, o_ref, og_ref, h_ref, *, n_main):
    j = pl.program_id(1)

    @pl.when(j == 0)
    def _():
        h_ref[...] = _rms(x_ref[...], g_ref[...]).astype(BF16)

    acc = jnp.dot(h_ref[...], w_ref[...].astype(BF16), preferred_element_type=F32) + b_ref[...]

    @pl.when(j < n_main)
    def _():
        o_ref[...] = acc

    @pl.when(j >= n_main)
    def _():
        og_ref[...] = acc.astype(BF16)


def _norm_matmul_split(x, gain, w, bias, split_col, name):
    t, d = x.shape
    n = w.shape[1]
    tm = min(PROJ_TM, t)
    tn = PROJ_TN
    assert t % tm == 0 and n % tn == 0 and split_col % tn == 0 and 0 < split_col < n
    n_main = split_col // tn
    return pl.pallas_call(
        functools.partial(_norm_matmul_split_kernel, n_main=n_main),
        grid=(t // tm, n // tn),
        in_specs=[
            pl.BlockSpec((tm, d), lambda i, j: (i, 0)),
            pl.BlockSpec((1, d), lambda i, j: (0, 0)),
            pl.BlockSpec((d, tn), lambda i, j: (0, j)),
            pl.BlockSpec((1, tn), lambda i, j: (0, j)),
        ],
        out_specs=[
            pl.BlockSpec((tm, tn), lambda i, j: (i, jnp.minimum(j, n_main - 1))),
            pl.BlockSpec((tm, tn), lambda i, j: (i, jnp.maximum(j - n_main, 0))),
        ],
        out_shape=[jax.ShapeDtypeStruct((t, split_col), F32),
                   jax.ShapeDtypeStruct((t, n - split_col), BF16)],
        scratch_shapes=[pltpu.VMEM((tm, d), BF16)],
        compiler_params=pltpu.CompilerParams(
            dimension_semantics=("arbitrary", "arbitrary"), vmem_limit_bytes=VMEM_LIMIT),
        name=name,
    )(x, gain.reshape(1, d), w, bias.reshape(1, n))


def _pool_project(pooled, wpg_ref, scale_ref, g):
    sl = slice(g * POOL_GROUP_DIM, (g + 1) * POOL_GROUP_DIM)
    y = jnp.dot(pooled.astype(BF16), wpg_ref[g].astype(BF16), preferred_element_type=F32)
    return y * scale_ref[:, sl]


def _softmax_rows(s):
    m = jnp.max(s, axis=-1, keepdims=True)
    e = jnp.exp(s - m)
    return e / jnp.sum(e, axis=-1, keepdims=True)


def _mix_prompt_kernel(cur_ref, prev_ref, mk_ref, mv_ref, wpg_ref, scale_ref, wconv_ref,
                       br_ref, zst_ref, extu_ref, extz_ref):
    t = pl.program_id(1)
    tm = cur_ref.shape[0]
    has_prev = t > 0

    u = cur_ref[:, C_U:C_U + D_POOL]
    extu_ref[0:HIST, :] = jnp.where(has_prev, prev_ref[:, C_U:C_U + D_POOL], 0.0)
    extu_ref[HIST:HIST + tm, :] = u
    pos = t * tm + lax.broadcasted_iota(jnp.int32, (tm, 1), 0)
    for g, w in enumerate(POOL_WINDOWS):
        sl = slice(g * POOL_GROUP_DIM, (g + 1) * POOL_GROUP_DIM)
        s = extu_ref[HIST:HIST + tm, sl]
        for k in range(1, w):
            s = s + extu_ref[HIST - k:HIST - k + tm, sl]
        cnt = jnp.minimum(w, pos + 1).astype(F32)
        pooled = s / cnt - extu_ref[HIST:HIST + tm, sl]
        br_ref[:, sl] = _pool_project(pooled, wpg_ref, scale_ref, g).astype(BF16)

    z = cur_ref[:, C_C:C_C + D_CONV] * cur_ref[:, C_V:C_V + D_CONV]
    zprev = prev_ref[:, C_C:C_C + D_CONV] * prev_ref[:, C_V:C_V + D_CONV]
    extz_ref[0:HIST, :] = jnp.where(has_prev, zprev, 0.0)
    extz_ref[HIST:HIST + tm, :] = z
    y = extz_ref[HIST - 2:HIST - 2 + tm, :] * wconv_ref[0:1, :]
    y = y + extz_ref[HIST - 1:HIST - 1 + tm, :] * wconv_ref[1:2, :]
    y = y + extz_ref[HIST:HIST + tm, :] * wconv_ref[2:3, :]
    br_ref[:, D_POOL:D_POOL + D_CONV] = (cur_ref[:, C_B:C_B + D_CONV] * y).astype(BF16)
    zst_ref[0] = extz_ref[HIST + tm - 8:HIST + tm, :]

    for h in range(N_XHEADS):
        sl = slice(h * XHEAD_DIM, (h + 1) * XHEAD_DIM)
        qh = cur_ref[:, C_Q + h * XHEAD_DIM:C_Q + (h + 1) * XHEAD_DIM].astype(BF16)
        kh = mk_ref[0, :, sl].astype(BF16)
        vh = mv_ref[0, :, sl].astype(BF16)
        s = lax.dot_general(qh, kh, (((1,), (1,)), ((), ())), preferred_element_type=F32)
        p = _softmax_rows(s * (XHEAD_DIM ** -0.5))
        o = jnp.dot(p.astype(BF16), vh, preferred_element_type=F32)
        c0 = D_POOL + D_CONV + h * XHEAD_DIM
        br_ref[:, c0:c0 + XHEAD_DIM] = o.astype(BF16)


def _mix_prompt(proj, mk, mv, wpg, scale, wconv, batch, seq):
    tm = MIX_TM
    nt = seq // tm
    assert seq % tm == 0 and tm % HIST == 0
    rpb = tm // HIST
    return pl.pallas_call(
        _mix_prompt_kernel,
        grid=(batch, nt),
        in_specs=[
            pl.BlockSpec((tm, D_MIX), lambda b, t: (b * nt + t, 0)),
            pl.BlockSpec((HIST, D_MIX), lambda b, t: (jnp.maximum((b * nt + t) * rpb - 1, 0), 0)),
            pl.BlockSpec((1, N_MEM, D_XATTN), lambda b, t: (b, 0, 0)),
            pl.BlockSpec((1, N_MEM, D_XATTN), lambda b, t: (b, 0, 0)),
            pl.BlockSpec((len(POOL_WINDOWS), POOL_GROUP_DIM, POOL_GROUP_DIM), lambda b, t: (0, 0, 0)),
            pl.BlockSpec((1, D_POOL), lambda b, t: (0, 0)),
            pl.BlockSpec((CONV_WIDTH, D_CONV), lambda b, t: (0, 0)),
        ],
        out_specs=[
            pl.BlockSpec((tm, D_MODEL), lambda b, t: (b * nt + t, 0)),
            pl.BlockSpec((1, 8, D_CONV), lambda b, t: (b, 0, 0)),
        ],
        out_shape=[
            jax.ShapeDtypeStruct((batch * seq, D_MODEL), BF16),
            jax.ShapeDtypeStruct((batch, 8, D_CONV), F32),
        ],
        scratch_shapes=[pltpu.VMEM((HIST + tm, D_POOL), F32), pltpu.VMEM((HIST + tm, D_CONV), F32)],
        compiler_params=pltpu.CompilerParams(
            dimension_semantics=("arbitrary", "arbitrary"), vmem_limit_bytes=VMEM_LIMIT),
        name="mix_prompt",
    )(proj, proj, mk, mv, wpg, scale.reshape(1, D_POOL), wconv)


def _mix_sample_kernel(cur_ref, sp_ref, sc_ref, k_ref, v_ref, wpg_ref, scale_ref, wconv_ref,
                       br_ref, zst_ref, extu_ref, extz_ref):
    ns, ln = zst_ref.shape[0], zst_ref.shape[1]
    rows = ns * ln

    extu_ref[:, HIST - POOL_STATE_LEN:HIST, :] = sp_ref[...]
    extu_ref[:, HIST:HIST + ln, :] = cur_ref[:, C_U:C_U + D_POOL].reshape(ns, ln, D_POOL)
    for g, w in enumerate(POOL_WINDOWS):
        sl = slice(g * POOL_GROUP_DIM, (g + 1) * POOL_GROUP_DIM)
        s = extu_ref[:, HIST:HIST + ln, sl]
        for k in range(1, w):
            s = s + extu_ref[:, HIST - k:HIST - k + ln, sl]
        pooled = s / float(w) - extu_ref[:, HIST:HIST + ln, sl]
        pooled = pooled.reshape(rows, POOL_GROUP_DIM)
        br_ref[:, sl] = _pool_project(pooled, wpg_ref, scale_ref, g).astype(BF16)

    z = cur_ref[:, C_C:C_C + D_CONV] * cur_ref[:, C_V:C_V + D_CONV]
    extz_ref[:, HIST - 2:HIST, :] = sc_ref[...]
    extz_ref[:, HIST:HIST + ln, :] = z.reshape(ns, ln, D_CONV)
    y = extz_ref[:, HIST - 2:HIST - 2 + ln, :] * wconv_ref[0:1, :]
    y = y + extz_ref[:, HIST - 1:HIST - 1 + ln, :] * wconv_ref[1:2, :]
    y = y + extz_ref[:, HIST:HIST + ln, :] * wconv_ref[2:3, :]
    br_ref[:, D_POOL:D_POOL + D_CONV] = (
        cur_ref[:, C_B:C_B + D_CONV] * y.reshape(rows, D_CONV)).astype(BF16)
    zst_ref[...] = extz_ref[:, HIST:HIST + ln, :]

    q3 = cur_ref[:, C_Q:C_Q + D_XATTN].reshape(ns, ln, D_XATTN)
    q4 = jnp.concatenate([q3[:, :, h * XHEAD_DIM:(h + 1) * XHEAD_DIM] for h in range(N_XHEADS)], axis=1)
    s = jnp.einsum("nqd,nkd->nqk", q4.astype(BF16), k_ref[...].astype(BF16),
                   preferred_element_type=F32)
    row_head = lax.broadcasted_iota(jnp.int32, (N_XHEADS * ln, N_XHEADS * N_MEM), 0) // ln
    col_head = lax.broadcasted_iota(jnp.int32, (N_XHEADS * ln, N_XHEADS * N_MEM), 1) % N_XHEADS
    s = jnp.where((row_head == col_head)[None], s * (XHEAD_DIM ** -0.5), -jnp.inf)
    p = _softmax_rows(s)
    r = jnp.einsum("nqk,nkd->nqd", p.astype(BF16), v_ref[...].astype(BF16),
                   preferred_element_type=F32)
    for h in range(N_XHEADS):
        c0 = D_POOL + D_CONV + h * XHEAD_DIM
        br_ref[:, c0:c0 + XHEAD_DIM] = r[:, h * ln:(h + 1) * ln, :].reshape(rows, XHEAD_DIM).astype(BF16)


def _mix_sample(proj, state_pool, state_conv, mem_k, mem_v, wpg, scale, wconv, nseq, ln):
    ns = MIX_NS
    rows = ns * ln
    assert nseq % ns == 0 and ln == 8
    return pl.pallas_call(
        _mix_sample_kernel,
        grid=(nseq // ns,),
        in_specs=[
            pl.BlockSpec((rows, D_MIX), lambda s: (s, 0)),
            pl.BlockSpec((ns, POOL_STATE_LEN, D_POOL), lambda s: (s, 0, 0)),
            pl.BlockSpec((ns, CONV_WIDTH - 1, D_CONV), lambda s: (s, 0, 0)),
            pl.BlockSpec((ns, N_MEM * N_XHEADS, XHEAD_DIM), lambda s: (s, 0, 0)),
            pl.BlockSpec((ns, N_MEM * N_XHEADS, XHEAD_DIM), lambda s: (s, 0, 0)),
            pl.BlockSpec((len(POOL_WINDOWS), POOL_GROUP_DIM, POOL_GROUP_DIM), lambda s: (0, 0, 0)),
            pl.BlockSpec((1, D_POOL), lambda s: (0, 0)),
            pl.BlockSpec((CONV_WIDTH, D_CONV), lambda s: (0, 0)),
        ],
        out_specs=[
            pl.BlockSpec((rows, D_MODEL), lambda s: (s, 0)),
            pl.BlockSpec((ns, ln, D_CONV), lambda s: (s, 0, 0)),
        ],
        out_shape=[
            jax.ShapeDtypeStruct((nseq * ln, D_MODEL), BF16),
            jax.ShapeDtypeStruct((nseq, ln, D_CONV), F32),
        ],
        scratch_shapes=[pltpu.VMEM((ns, HIST + ln, D_POOL), F32),
                        pltpu.VMEM((ns, HIST + ln, D_CONV), F32)],
        compiler_params=pltpu.CompilerParams(
            dimension_semantics=("arbitrary",), vmem_limit_bytes=VMEM_LIMIT),
        name="mix_sample",
    )(proj, state_pool, state_conv, mem_k, mem_v, wpg, scale.reshape(1, D_POOL), wconv)


def _merge_route_kernel(x_ref, br_ref, g0_ref, g1_ref, g2_ref, wpo_ref, wco_ref, wao_ref, wo_ref,
                        nf_ref, wrhl_ref, wrh_ref, brt_ref,
                        x1_ref, h2_ref, ri_ref, rw_ref, cnt_ref, carry_ref):
    i = pl.program_id(0)
    tm = x_ref.shape[0]

    @pl.when(i == 0)
    def _():
        carry_ref[...] = jnp.zeros_like(carry_ref)

    merged = _sigmoid(g0_ref[...].astype(F32)) * jnp.dot(
        br_ref[:, 0:D_POOL], wpo_ref[...], preferred_element_type=F32)
    merged = merged + _sigmoid(g1_ref[...].astype(F32)) * jnp.dot(
        br_ref[:, D_POOL:D_POOL + D_CONV], wco_ref[...], preferred_element_type=F32)
    merged = merged + _sigmoid(g2_ref[...].astype(F32)) * jnp.dot(
        br_ref[:, D_POOL + D_CONV:D_MODEL], wao_ref[...], preferred_element_type=F32)
    x1 = x_ref[...] + jnp.dot(merged.astype(BF16), wo_ref[...], preferred_element_type=F32)
    x1_ref[...] = x1
    h2 = _rms(x1, nf_ref[...])
    h2_ref[...] = _pack_bf16_pairs(h2)

    h2_hi = h2.astype(BF16)
    h2_lo = (h2 - h2_hi.astype(F32)).astype(BF16)
    hi_part = jnp.dot(h2_hi, wrhl_ref[...], preferred_element_type=F32)
    logits = (hi_part[:, 0:N_EXPERTS] + hi_part[:, N_EXPERTS:2 * N_EXPERTS]
              + jnp.dot(h2_lo, wrh_ref[...], preferred_element_type=F32)) + brt_ref[...]
    lane = lax.broadcasted_iota(jnp.int32, (tm, N_EXPERTS), 1).astype(F32)
    vals, idxs, hots = [], [], []
    work = logits
    for _ in range(TOP_K):
        m = jnp.max(work, axis=-1, keepdims=True)
        idx = jnp.min(jnp.where(work == m, lane, float(N_EXPERTS)), axis=-1, keepdims=True)
        hot = lane == idx
        work = jnp.where(hot, -jnp.inf, work)
        vals.append(m)
        idxs.append(idx)
        hots.append(hot)
    es = [jnp.exp(v - vals[0]) for v in vals]
    denom = es[0] + es[1] + es[2] + es[3]

    chosen = jnp.where(hots[0] | hots[1] | hots[2] | hots[3], 1.0, 0.0).astype(BF16)
    r_i = lax.broadcasted_iota(jnp.int32, (tm, tm), 0)
    c_i = lax.broadcasted_iota(jnp.int32, (tm, tm), 1)
    lower = jnp.where(c_i < r_i, 1.0, 0.0).astype(BF16)
    before = jnp.dot(lower, chosen, preferred_element_type=F32) + carry_ref[0:1, 0:N_EXPERTS]
    carry_ref[0:1, 0:N_EXPERTS] = (carry_ref[0:1, 0:N_EXPERTS]
                                   + jnp.sum(chosen.astype(F32), axis=0, keepdims=True))
    cnt_ref[...] = carry_ref[...]

    out_lane = lax.broadcasted_iota(jnp.int32, (tm, LANES), 1)
    ri = jnp.zeros((tm, LANES), jnp.int32)
    rw = jnp.zeros((tm, LANES), F32)
    for k in range(TOP_K):
        rank = jnp.sum(jnp.where(hots[k], before, 0.0), axis=-1, keepdims=True).astype(jnp.int32)
        ri = jnp.where(out_lane == k, idxs[k].astype(jnp.int32), ri)
        ri = jnp.where(out_lane == TOP_K + k, rank, ri)
        rw = jnp.where(out_lane == k, es[k] / denom, rw)
    ri_ref[...] = ri
    rw_ref[...] = rw


def _merge_route(x, branch, gates, wpo, wco, wao, wo, norm_ffn, w_router, b_router, name):
    t = x.shape[0]
    tm = MERGE_TM
    assert t % tm == 0 and gates.shape == (t, N_BRANCH * D_MODEL)
    const = lambda i: (0, 0)
    wr_hi = w_router.astype(BF16)
    wr_lo = (w_router - wr_hi.astype(F32)).astype(BF16)
    return pl.pallas_call(
        _merge_route_kernel,
        grid=(t // tm,),
        in_specs=[
            pl.BlockSpec((tm, D_MODEL), lambda i: (i, 0)),
            pl.BlockSpec((tm, D_MODEL), lambda i: (i, 0)),
            pl.BlockSpec((tm, D_MODEL), lambda i: (i, 0)),
            pl.BlockSpec((tm, D_MODEL), lambda i: (i, 1)),
            pl.BlockSpec((tm, D_MODEL), lambda i: (i, 2)),
            pl.BlockSpec((D_POOL, D_MODEL), const, pipeline_mode=pl.Buffered(1)),
            pl.BlockSpec((D_CONV, D_MODEL), const, pipeline_mode=pl.Buffered(1)),
            pl.BlockSpec((D_XATTN, D_MODEL), const, pipeline_mode=pl.Buffered(1)),
            pl.BlockSpec((D_MODEL, D_MODEL), const, pipeline_mode=pl.Buffered(1)),
            pl.BlockSpec((1, D_MODEL), const),
            pl.BlockSpec((D_MODEL, 2 * N_EXPERTS), const),
            pl.BlockSpec((D_MODEL, N_EXPERTS), const),
            pl.BlockSpec((1, N_EXPERTS), const),
        ],
        out_specs=[
            pl.BlockSpec((tm, D_MODEL), lambda i: (i, 0)),
            pl.BlockSpec((tm, D_MODEL // 2), lambda i: (i, 0)),
            pl.BlockSpec((tm, LANES), lambda i: (i, 0)),
            pl.BlockSpec((tm, LANES), lambda i: (i, 0)),
            pl.BlockSpec((8, LANES), const),
        ],
        out_shape=[
            jax.ShapeDtypeStruct((t, D_MODEL), F32),
            jax.ShapeDtypeStruct((t, D_MODEL // 2), PACKED),
            jax.ShapeDtypeStruct((t, LANES), jnp.int32),
            jax.ShapeDtypeStruct((t, LANES), F32),
            jax.ShapeDtypeStruct((8, LANES), F32),
        ],
        scratch_shapes=[pltpu.VMEM((8, LANES), F32)],
        compiler_params=pltpu.CompilerParams(
            dimension_semantics=("arbitrary",), vmem_limit_bytes=VMEM_LIMIT),
        name=name,
    )(x, branch, gates, gates, gates, wpo, wco, wao, wo, norm_ffn.reshape(1, D_MODEL),
      jnp.concatenate([wr_hi, wr_lo], axis=1), wr_hi, b_router.reshape(1, N_EXPERTS))


def _fill_work_list(nblk_s, pad_s, nj, refs):
    te, tj, trow, tnb, tfirst, tne, tnj, tpar, ttot = refs

    def clear(i, c):
        for r in (te, tj, trow, tnb, tfirst, tpar):
            r[i] = 0
        tne[i] = -1
        tnj[i] = -1
        return c

    lax.fori_loop(0, te.shape[0], clear, 0)

    carry = (jnp.int32(0), jnp.int32(-1), jnp.int32(0))
    for j in range(nj):
        def per_expert(e, carry):
            q, prev, groups = carry
            n = nblk_s[e]
            nch = (n + EXP_CHUNK - 1) // EXP_CHUNK

            @pl.when((nch > 0) & (prev >= 0))
            def _():
                tne[prev] = e
                tnj[prev] = jnp.int32(j)

            def per_chunk(c, q):
                te[q] = e
                tj[q] = jnp.int32(j)
                trow[q] = pad_s[e] + c * (EXP_CHUNK * EXP_TM)
                tnb[q] = jnp.minimum(EXP_CHUNK, n - c * EXP_CHUNK)
                tfirst[q] = (c == 0).astype(jnp.int32)
                tpar[q] = lax.rem(groups, 2)
                return q + 1

            return (lax.fori_loop(0, nch, per_chunk, q), jnp.where(nch > 0, q, prev),
                    groups + (nch > 0).astype(jnp.int32))

        carry = lax.fori_loop(0, N_EXPERTS, per_expert, carry)
    ttot[0] = carry[0]


def _route_tables_kernel(cntp_ref, cnts_ref, rip_ref, ris_ref, dest_ref, *refs, nj_in, nj_out):
    tabs_in, tabs_out = refs[0:N_TABS], refs[N_TABS:2 * N_TABS]
    nblk_s, pad_s = refs[2 * N_TABS:]

    def per_expert(e, start_blk):
        n = (cntp_ref[e] + cnts_ref[e] + EXP_TM - 1) // EXP_TM
        nblk_s[e] = n
        pad_s[e] = start_blk * EXP_TM
        return start_blk + n

    lax.fori_loop(0, N_EXPERTS, per_expert, jnp.int32(0))
    _fill_work_list(nblk_s, pad_s, nj_in, tabs_in)
    _fill_work_list(nblk_s, pad_s, nj_out, tabs_out)

    tile = ROUTE_TM
    col0 = 0
    for ri_ref, is_sample in ((rip_ref, False), (ris_ref, True)):
        for r in range(ri_ref.shape[0] // tile):
            ri = ri_ref[r * tile:(r + 1) * tile, :]
            base = jnp.zeros_like(ri)
            for e in range(N_EXPERTS):
                first_row = pad_s[e] + cntp_ref[e] if is_sample else pad_s[e]
                base = jnp.where(ri == e, first_row, base)
            dest = base + pltpu.roll(ri, LANES - TOP_K, axis=1)
            dest_ref[:, col0:col0 + tile] = jnp.transpose(dest)[0:8, :]
            col0 += tile


def _route_tables(cnt_p, cnt_s, ri_p, ri_s, nj_in, nj_out, max_items):
    t_all = ri_p.shape[0] + ri_s.shape[0]
    assert ri_p.shape[0] % ROUTE_TM == 0 and ri_s.shape[0] % ROUTE_TM == 0
    smem = pl.BlockSpec(memory_space=pltpu.SMEM)

    def tables(nj):
        n = nj * max_items
        return [jax.ShapeDtypeStruct((n,), jnp.int32)] * (N_TABS - 1) + [jax.ShapeDtypeStruct((1,), jnp.int32)]

    out = pl.pallas_call(
        functools.partial(_route_tables_kernel, nj_in=nj_in, nj_out=nj_out),
        grid=(1,),
        in_specs=[smem, smem,
                  pl.BlockSpec(ri_p.shape, lambda i: (0, 0)), pl.BlockSpec(ri_s.shape, lambda i: (0, 0))],
        out_specs=[pl.BlockSpec((8, t_all), lambda i: (0, 0))] + [smem] * (2 * N_TABS),
        out_shape=[jax.ShapeDtypeStruct((8, t_all), jnp.int32)] + tables(nj_in) + tables(nj_out),
        scratch_shapes=[pltpu.SMEM((N_EXPERTS,), jnp.int32), pltpu.SMEM((N_EXPERTS,), jnp.int32)],
        compiler_params=pltpu.CompilerParams(
            dimension_semantics=("arbitrary",), vmem_limit_bytes=VMEM_LIMIT),
        name="route_tables",
    )(cnt_p, cnt_s, ri_p, ri_s)
    return out[0], tuple(out[1:1 + N_TABS]), tuple(out[1 + N_TABS:1 + 2 * N_TABS])


def _sc_mesh():
    return plsc.VectorSubcoreMesh(core_axis_name="c", subcore_axis_name="s",
                                  num_cores=SC_CORES, num_subcores=SC_SUBCORES)


def _sc_worker_id():
    return lax.axis_index("s") * SC_CORES + lax.axis_index("c")


def _sc_dispatch(h2_a, h2_b, dest_by_slot, rows_out):
    ta, w = h2_a.shape
    t = ta + h2_b.shape[0]
    per_w = t // SC_WORKERS
    n_chunks = per_w // SC_CHUNK
    assert t == SC_WORKERS * n_chunks * SC_CHUNK and ta % SC_CHUNK == 0
    idx = dest_by_slot[:TOP_K].reshape(TOP_K, SC_WORKERS, n_chunks, SC_CHUNK).transpose(1, 0, 2, 3)

    def body(a_hbm, b_hbm, idx_hbm, xs_hbm, idx_v, rows_v, sems):
        base = _sc_worker_id() * per_w
        pltpu.sync_copy(idx_hbm.at[_sc_worker_id()], idx_v)
        pending = [[], []]
        for c in range(n_chunks):
            b = c % 2
            for d in pending[b]:
                d.wait()
            tok0 = base + c * SC_CHUNK

            @pl.when(tok0 < ta)
            def _():
                pltpu.sync_copy(a_hbm.at[pl.ds(tok0, SC_CHUNK)], rows_v.at[b])

            @pl.when(tok0 >= ta)
            def _():
                pltpu.sync_copy(b_hbm.at[pl.ds(tok0 - ta, SC_CHUNK)], rows_v.at[b])

            pending[b] = [pltpu.async_copy(rows_v.at[b], xs_hbm.at[idx_v.at[k, c]], sems.at[b])
                          for k in range(TOP_K)]
        for b in range(2):
            for d in pending[b]:
                d.wait()

    return pl.kernel(
        body,
        out_type=jax.ShapeDtypeStruct((rows_out, w), h2_a.dtype),
        mesh=_sc_mesh(),
        scratch_types=[pltpu.VMEM((TOP_K, n_chunks, SC_CHUNK), jnp.int32),
                       pltpu.VMEM((2, SC_CHUNK, w), h2_a.dtype),
                       pltpu.SemaphoreType.DMA((2,))],
        name="sc_dispatch",
    )(h2_a, h2_b, idx)


def _sc_gather_rows(table, idx):
    n = idx.shape[0]
    w = table.shape[1]
    per_w = n // SC_WORKERS
    n_chunks = per_w // SC_CHUNK
    assert n == SC_WORKERS * n_chunks * SC_CHUNK
    idx3 = idx.reshape(SC_WORKERS, n_chunks, SC_CHUNK)

    def body(table_hbm, idx_hbm, out_hbm, idx_v, rows_v, gsem, wsems):
        base = _sc_worker_id() * per_w
        pltpu.sync_copy(idx_hbm.at[_sc_worker_id()], idx_v)
        pending = [None, None]
        for c in range(n_chunks):
            b = c % 2
            if pending[b] is not None:
                pending[b].wait()
            pltpu.async_copy(table_hbm.at[idx_v.at[c]], rows_v.at[b], gsem).wait()
            pending[b] = pltpu.async_copy(
                rows_v.at[b], out_hbm.at[pl.ds(base + c * SC_CHUNK, SC_CHUNK)], wsems.at[b])
        for b in range(2):
            if pending[b] is not None:
                pending[b].wait()

    return pl.kernel(
        body,
        out_type=jax.ShapeDtypeStruct((n, w), table.dtype),
        mesh=_sc_mesh(),
        scratch_types=[pltpu.VMEM((n_chunks, SC_CHUNK), jnp.int32),
                       pltpu.VMEM((2, SC_CHUNK, w), table.dtype),
                       pltpu.SemaphoreType.DMA,
                       pltpu.SemaphoreType.DMA((2,))],
        name="sc_gather_rows",
    )(table, idx3)


def _grouped_pipeline(tabs, total_ref, weight_copies, in_copy, out_copy, compute):
    te, tj, _, tnb, tfirst, tne, tnj, tpar = tabs
    total = total_ref[0]

    def wait_out(q):
        for m in range(1, EXP_CHUNK + 1):
            @pl.when(tnb[q] == m)
            def _():
                out_copy(q, lax.rem(q, 2), m).wait()

    for d in weight_copies(te[0], tj[0], tpar[0]):
        d.start()
    in_copy(0, 0).start()

    def body(q, carry):
        slot = lax.rem(q, 2)

        @pl.when(q + 1 < total)
        def _():
            in_copy(q + 1, 1 - slot).start()

        in_copy(q, slot).wait()

        @pl.when(tfirst[q] == 1)
        def _():
            for d in weight_copies(te[q], tj[q], tpar[q]):
                d.wait()

            @pl.when(tne[q] >= 0)
            def _():
                for d in weight_copies(tne[q], tnj[q], 1 - tpar[q]):
                    d.start()

        @pl.when(q >= 2)
        def _():
            wait_out(q - 2)

        for m in range(1, EXP_CHUNK + 1):
            @pl.when(tnb[q] == m)
            def _():
                compute(q, slot, m, tpar[q])
                out_copy(q, slot, m).start()

        return carry

    lax.fori_loop(0, total, body, 0)

    @pl.when(total >= 2)
    def _():
        wait_out(total - 2)

    wait_out(total - 1)


def _expert_in_kernel(*refs):
    tabs, total_ref = refs[:N_TABS - 1], refs[N_TABS - 1]
    xs_hbm, w_hbm, b_ref, act_hbm, wbuf, x_buf, a_buf, w_sem, x_sem, a_sem = refs[N_TABS:]
    te, tj, trow = tabs[:3]
    tm, tn = EXP_TM, a_buf.shape[2]
    nj = D_FF // tn

    def weight_copies(e, j, par):
        return [pltpu.make_async_copy(
            w_hbm.at[e, :, pl.ds(pl.multiple_of(half * D_FF + j * tn, tn), tn)],
            wbuf.at[2 * par + half], w_sem)
            for half in range(2)]

    def rows(q, m):
        return pl.ds(pl.multiple_of(trow[q], tm), m * tm)

    def x_copy(q, slot):
        return pltpu.make_async_copy(xs_hbm.at[rows(q, EXP_CHUNK)], x_buf.at[slot], x_sem.at[slot])

    def a_copy(q, slot, m):
        return pltpu.make_async_copy(
            a_buf.at[slot, pl.ds(0, m * tm)],
            act_hbm.at[rows(q, m), pl.ds(pl.multiple_of(tj[q] * tn, tn), tn)], a_sem.at[slot])

    def compute(q, slot, m, par):
        bias0 = te[q] * (2 * nj) + tj[q]
        x = jnp.concatenate(_unpack_bf16_pairs(x_buf[slot, 0:m * tm]), axis=1).astype(BF16)
        g = jnp.dot(x, wbuf[2 * par].astype(BF16), preferred_element_type=F32) + b_ref[bias0]
        up = jnp.dot(x, wbuf[2 * par + 1].astype(BF16), preferred_element_type=F32) + b_ref[bias0 + nj]
        g = jnp.minimum(g, SWIGLU_LIMIT)
        up = jnp.clip(up, -SWIGLU_LIMIT, SWIGLU_LIMIT)
        a_buf[slot, 0:m * tm] = (g * _sigmoid(SWIGLU_ALPHA * g) * (up + 1.0)).astype(BF16)

    _grouped_pipeline(tabs, total_ref, weight_copies, x_copy, a_copy, compute)


def _expert_in(tabs, xs, w_in, b_in):
    rows = xs.shape[0]
    tm, tn = EXP_TM, EXP_IN_TN
    nj = D_FF // tn
    bias = b_in.reshape(N_EXPERTS * 2 * nj, 1, tn)
    grid_spec = pltpu.PrefetchScalarGridSpec(
        num_scalar_prefetch=len(tabs),
        grid=(1,),
        in_specs=[
            pl.BlockSpec(memory_space=pl.ANY),
            pl.BlockSpec(memory_space=pl.ANY),
            pl.BlockSpec(bias.shape, lambda i, *_: (0, 0, 0), pipeline_mode=pl.Buffered(1)),
        ],
        out_specs=pl.BlockSpec(memory_space=pl.ANY),
        scratch_shapes=[pltpu.VMEM((4, D_MODEL, tn), F32),
                        pltpu.VMEM((2, EXP_CHUNK * tm, D_MODEL // 2), PACKED),
                        pltpu.VMEM((2, EXP_CHUNK * tm, tn), BF16),
                        pltpu.SemaphoreType.DMA, pltpu.SemaphoreType.DMA((2,)),
                        pltpu.SemaphoreType.DMA((2,))],
    )
    return pl.pallas_call(
        _expert_in_kernel,
        grid_spec=grid_spec,
        out_shape=jax.ShapeDtypeStruct((rows, D_FF), BF16),
        compiler_params=pltpu.CompilerParams(
            dimension_semantics=("arbitrary",), vmem_limit_bytes=VMEM_LIMIT),
        name="expert_in",
    )(*tabs, xs, w_in, bias)


def _expert_out_kernel(*refs):
    tabs, total_ref = refs[:N_TABS - 1], refs[N_TABS - 1]
    act_hbm, w_hbm, b_ref, ys_hbm, wbuf, a_buf, y_buf, w_sem, a_sem, y_sem = refs[N_TABS:]
    te, tj, trow = tabs[:3]
    tm, tnp = EXP_TM, y_buf.shape[2]
    tn = 2 * tnp
    nj = D_MODEL // tn

    def weight_copies(e, j, par):
        return [pltpu.make_async_copy(
            w_hbm.at[e, :, pl.ds(pl.multiple_of(j * tn, tn), tn)], wbuf.at[par], w_sem)]

    def rows(q, m):
        return pl.ds(pl.multiple_of(trow[q], tm), m * tm)

    def a_copy(q, slot):
        return pltpu.make_async_copy(act_hbm.at[rows(q, EXP_CHUNK)], a_buf.at[slot], a_sem.at[slot])

    def y_copy(q, slot, m):
        return pltpu.make_async_copy(
            y_buf.at[slot, pl.ds(0, m * tm)],
            ys_hbm.at[rows(q, m), pl.ds(pl.multiple_of(tj[q] * tnp, tnp), tnp)], y_sem.at[slot])

    def compute(q, slot, m, par):
        y = jnp.dot(a_buf[slot, 0:m * tm], wbuf[par].astype(BF16),
                    preferred_element_type=F32) + b_ref[te[q] * nj + tj[q]]
        y_buf[slot, 0:m * tm] = _pack_bf16_pairs(y)

    _grouped_pipeline(tabs, total_ref, weight_copies, a_copy, y_copy, compute)


def _expert_out(tabs, act, w_out, b_out):
    rows = act.shape[0]
    tm, tn = EXP_TM, EXP_OUT_TN
    nj = D_MODEL // tn
    bias = b_out.reshape(N_EXPERTS * nj, 1, tn)
    grid_spec = pltpu.PrefetchScalarGridSpec(
        num_scalar_prefetch=len(tabs),
        grid=(1,),
        in_specs=[
            pl.BlockSpec(memory_space=pl.ANY),
            pl.BlockSpec(memory_space=pl.ANY),
            pl.BlockSpec(bias.shape, lambda i, *_: (0, 0, 0), pipeline_mode=pl.Buffered(1)),
        ],
        out_specs=pl.BlockSpec(memory_space=pl.ANY),
        scratch_shapes=[pltpu.VMEM((2, D_FF, tn), F32),
                        pltpu.VMEM((2, EXP_CHUNK * tm, D_FF), BF16),
                        pltpu.VMEM((2, EXP_CHUNK * tm, tn // 2), PACKED),
                        pltpu.SemaphoreType.DMA, pltpu.SemaphoreType.DMA((2,)),
                        pltpu.SemaphoreType.DMA((2,))],
    )
    return pl.pallas_call(
        _expert_out_kernel,
        grid_spec=grid_spec,
        out_shape=jax.ShapeDtypeStruct((rows, D_MODEL // 2), PACKED),
        compiler_params=pltpu.CompilerParams(
            dimension_semantics=("arbitrary",), vmem_limit_bytes=VMEM_LIMIT),
        name="expert_out",
    )(*tabs, act, w_out, bias)


def _combine_kernel(x1_ref, yg_ref, rw_ref, g_ref, o_ref):
    half = EXP_OUT_TN // 2
    pieces = []
    ssq = jnp.zeros((x1_ref.shape[0], 1), F32)
    for jt in range(D_MODEL // EXP_OUT_TN):
        acc_lo = x1_ref[:, jt * EXP_OUT_TN:jt * EXP_OUT_TN + half]
        acc_hi = x1_ref[:, jt * EXP_OUT_TN + half:(jt + 1) * EXP_OUT_TN]
        moe_lo = jnp.zeros_like(acc_lo)
        moe_hi = jnp.zeros_like(acc_hi)
        for k in range(TOP_K):
            lo, hi = _unpack_bf16_pairs(yg_ref[k, :, jt * half:(jt + 1) * half])
            moe_lo = moe_lo + lo * rw_ref[:, k:k + 1]
            moe_hi = moe_hi + hi * rw_ref[:, k:k + 1]
        for acc in (acc_lo + moe_lo, acc_hi + moe_hi):
            ssq = ssq + jnp.sum(acc * acc, axis=-1, keepdims=True)
            pieces.append(acc)
    inv = lax.rsqrt(ssq / D_MODEL + EPS)
    for n, acc in enumerate(pieces):
        o_ref[:, n * half:(n + 1) * half] = acc * inv * g_ref[:, n * half:(n + 1) * half]


def _combine_part_kernel(x1_ref, yg_ref, rw_ref, g_ref, prev_hbm, o_ref):
    del prev_hbm
    _combine_kernel(x1_ref, yg_ref, rw_ref, g_ref, o_ref)


def _combine(x1, yg, rw, gain, row0, prev, name):
    t = x1.shape[0]
    n = yg.shape[1]
    tm = COMB_TM
    assert n % tm == 0 and row0 % tm == 0
    off = row0 // tm
    in_specs = [
        pl.BlockSpec((tm, D_MODEL), lambda i: (i + off, 0)),
        pl.BlockSpec((TOP_K, tm, D_MODEL // 2), lambda i: (0, i, 0)),
        pl.BlockSpec((tm, LANES), lambda i: (i + off, 0)),
        pl.BlockSpec((1, D_MODEL), lambda i: (0, 0)),
    ]
    args = [x1, yg, rw, gain.reshape(1, D_MODEL)]
    if prev is not None:
        in_specs.append(pl.BlockSpec(memory_space=pl.ANY))
        args.append(prev)
    return pl.pallas_call(
        _combine_kernel if prev is None else _combine_part_kernel,
        grid=(n // tm,),
        in_specs=in_specs,
        out_specs=pl.BlockSpec((tm, D_MODEL), lambda i: (i + off, 0)),
        out_shape=jax.ShapeDtypeStruct((t, D_MODEL), F32),
        input_output_aliases={} if prev is None else {len(args) - 1: 0},
        compiler_params=pltpu.CompilerParams(
            dimension_semantics=("arbitrary",), vmem_limit_bytes=VMEM_LIMIT),
        name=name,
    )(*args)


def kernel(x_prompt, x_sample, state_pool, state_conv, cache_mem_k, cache_mem_v, mem_prompt,
           norm_mix, w_in, b_gate, w_pool_group, pool_scale, w_conv, mem_norm, w_mem_kv,
           w_pool_out, w_conv_out, w_attn_out, w_o, norm_ffn, w_router, b_router,
           w_exp_in, b_exp_in, w_exp_out, b_exp_out, final_norm):
    depth = norm_mix.shape[0]
    assert depth == 1
    l = 0
    bp, seq, _ = x_prompt.shape
    bs, ln, _ = x_sample.shape
    tp, ts = bp * seq, bs * ln

    kv = _norm_matmul(mem_prompt.reshape(bp * N_MEM, D_MODEL), mem_norm[l], w_mem_kv[l],
                      jnp.zeros((2 * D_XATTN,), F32), "mem_kv")
    mk = kv[:, :D_XATTN].reshape(bp, N_MEM, D_XATTN)
    mv = kv[:, D_XATTN:].reshape(bp, N_MEM, D_XATTN)

    bias_in = jnp.concatenate([jnp.zeros((D_MIX,), F32), b_gate[l]])
    xp = x_prompt.reshape(tp, D_MODEL)
    xs_ = x_sample.reshape(ts, D_MODEL)
    proj_p, gates_p = _norm_matmul_split(xp, norm_mix[l], w_in[l], bias_in, D_MIX, "proj_prompt")
    proj_s, gates_s = _norm_matmul_split(xs_, norm_mix[l], w_in[l], bias_in, D_MIX, "proj_sample")

    br_p, zst_p = _mix_prompt(proj_p, mk, mv, w_pool_group[l], pool_scale[l], w_conv[l], bp, seq)
    br_s, zst_s = _mix_sample(proj_s, state_pool[l], state_conv[l],
                              cache_mem_k[l].reshape(bs, N_MEM * N_XHEADS, XHEAD_DIM),
                              cache_mem_v[l].reshape(bs, N_MEM * N_XHEADS, XHEAD_DIM),
                              w_pool_group[l], pool_scale[l], w_conv[l], bs, ln)

    wpo, wco, wao, wo = (w.astype(BF16) for w in (w_pool_out[l], w_conv_out[l], w_attn_out[l], w_o[l]))
    x1_p, h2_p, ri_p, rw_p, cnt_p = _merge_route(xp, br_p, gates_p, wpo, wco, wao, wo, norm_ffn[l],
                                                  w_router[l], b_router[l], "merge_route_prompt")
    x1_s, h2_s, ri_s, rw_s, cnt_s = _merge_route(xs_, br_s, gates_s, wpo, wco, wao, wo, norm_ffn[l],
                                                  w_router[l], b_router[l], "merge_route_sample")

    t_all = tp + ts
    n_assign = t_all * TOP_K
    nb_max = n_assign // EXP_TM + N_EXPERTS
    max_items = nb_max // EXP_CHUNK + N_EXPERTS
    dest, tabs_in, tabs_out = _route_tables(
        cnt_p[0, :N_EXPERTS].astype(jnp.int32), cnt_s[0, :N_EXPERTS].astype(jnp.int32), ri_p, ri_s,
        D_FF // EXP_IN_TN, D_MODEL // EXP_OUT_TN, max_items)
    xs_sorted = _sc_dispatch(h2_p, h2_s, dest, (nb_max + EXP_CHUNK) * EXP_TM)

    act = _expert_in(tabs_in, xs_sorted, w_exp_in[l], b_exp_in[l])
    ys = _expert_out(tabs_out, act, w_exp_out[l], b_exp_out[l])
    def combine_part(x1, rw, tok0, n, row0, prev, name):
        idx = dest[:TOP_K, tok0:tok0 + n].reshape(-1)
        yg = _sc_gather_rows(ys, idx).reshape(TOP_K, n, D_MODEL // 2)
        return _combine(x1, yg, rw, final_norm, row0, prev, name)

    y_s = combine_part(x1_s, rw_s, tp, ts, 0, None, "combine_sample")
    y_p = None
    tok0 = 0
    for c, units in enumerate(COMB_PARTS):
        n = tp * units // sum(COMB_PARTS)
        y_p = combine_part(x1_p, rw_p, tok0, n, tok0, y_p, f"combine_prompt_{c}")
        tok0 += n
    assert tok0 == tp

    new_pool_p = proj_p.reshape(bp, seq, D_MIX)[:, seq - POOL_STATE_LEN:, :D_POOL]
    new_conv_p = zst_p[:, 8 - (CONV_WIDTH - 1):]
    u_s = proj_s[:, :D_POOL].reshape(bs, ln, D_POOL)
    new_pool_s = jnp.concatenate([state_pool[l], u_s], axis=1)[:, -POOL_STATE_LEN:]
    new_conv_s = zst_s[:, ln - (CONV_WIDTH - 1):]

    return (y_p.reshape(bp, seq, D_MODEL), y_s.reshape(bs, ln, D_MODEL),
            new_pool_p[None], new_conv_p[None],
            mk.reshape(1, bp, N_MEM, N_XHEADS, XHEAD_DIM), mv.reshape(1, bp, N_MEM, N_XHEADS, XHEAD_DIM),
            new_pool_s[None], new_conv_s[None])
```

```python
import functools

import jax
import jax.numpy as jnp
from jax import lax
from jax.experimental import pallas as pl
from jax.experimental.pallas import tpu as pltpu
from jax.experimental.pallas import tpu_sc as plsc

F32 = jnp.float32
BF16 = jnp.bfloat16
PACKED = jnp.int32

D_MODEL = 2048
POOL_WINDOWS = (2, 4, 8, 16)
POOL_GROUP_DIM = 128
D_POOL = 512
POOL_STATE_LEN = 15
D_CONV = 1024
CONV_WIDTH = 3
N_MEM = 256
N_XHEADS = 4
XHEAD_DIM = 128
D_XATTN = 512
N_BRANCH = 3
D_MIX = D_POOL + 3 * D_CONV + D_XATTN
D_IN_TOTAL = D_MIX + N_BRANCH * D_MODEL
N_EXPERTS = 32
TOP_K = 4
D_FF = D_MODEL
SWIGLU_LIMIT = 7.0
SWIGLU_ALPHA = 1.702
EPS = 1e-5

C_U = 0
C_V = D_POOL
C_B = D_POOL + D_CONV
C_C = D_POOL + 2 * D_CONV
C_Q = D_POOL + 3 * D_CONV

LANES = 128
HIST = 16

PROJ_TM = 1024
PROJ_TN = 1024
MIX_TM = 512
MIX_NS = 8
MERGE_TM = 256
ROUTE_TM = 1024
N_TABS = 9
EXP_TM = 128
EXP_CHUNK = 6
EXP_IN_TN = 1024
EXP_OUT_TN = 2048
COMB_TM = 512
COMB_PARTS = (2, 3, 3)
SC_CORES = 2
SC_SUBCORES = 16
SC_WORKERS = SC_CORES * SC_SUBCORES
SC_CHUNK = 32
VMEM_LIMIT = 56 * 1024 * 1024


def _sigmoid(x):
    return 0.5 * (jnp.tanh(0.5 * x) + 1.0)


def _rms(x, g):
    ms = jnp.mean(x * x, axis=-1, keepdims=True)
    return x * lax.rsqrt(ms + EPS) * g


def _pack_bf16_pairs(x):
    n = x.shape[1] // 2
    lo = lax.bitcast_convert_type(x[:, :n].astype(BF16).astype(F32), jnp.uint32)
    hi = lax.bitcast_convert_type(x[:, n:].astype(BF16).astype(F32), jnp.uint32)
    return lax.bitcast_convert_type((hi & jnp.uint32(0xFFFF0000)) | (lo >> 16), PACKED)


def _unpack_bf16_pairs(p):
    p = lax.bitcast_convert_type(p, jnp.uint32)
    lo = lax.bitcast_convert_type(p << 16, F32)
    hi = lax.bitcast_convert_type(p & jnp.uint32(0xFFFF0000), F32)
    return lo, hi


def _norm_matmul_kernel(x_ref, g_ref, w_ref, b_ref, o_ref, h_ref):
    @pl.when(pl.program_id(1) == 0)
    def _():
        h_ref[...] = _rms(x_ref[...], g_ref[...]).astype(BF16)

    o_ref[...] = jnp.dot(h_ref[...], w_ref[...].astype(BF16), preferred_element_type=F32) + b_ref[...]


def _norm_matmul(x, gain, w, bias, name):
    t, d = x.shape
    n = w.shape[1]
    tm = min(PROJ_TM, t)
    tn = PROJ_TN
    assert t % tm == 0 and n % tn == 0
    return pl.pallas_call(
        _norm_matmul_kernel,
        grid=(t // tm, n // tn),
        in_specs=[
            pl.BlockSpec((tm, d), lambda i, j: (i, 0)),
            pl.BlockSpec((1, d), lambda i, j: (0, 0)),
            pl.BlockSpec((d, tn), lambda i, j: (0, j)),
            pl.BlockSpec((1, tn), lambda i, j: (0, j)),
        ],
        out_specs=pl.BlockSpec((tm, tn), lambda i, j: (i, j)),
        out_shape=jax.ShapeDtypeStruct((t, n), F32),
        scratch_shapes=[pltpu.VMEM((tm, d), BF16)],
        compiler_params=pltpu.CompilerParams(
            dimension_semantics=("arbitrary", "arbitrary"), vmem_limit_bytes=VMEM_LIMIT),
        name=name,
    )(x, gain.reshape(1, d), w, bias.reshape(1, n))


def _norm_matmul_split_kernel(x_ref, g_ref, w_ref, b_ref, o_ref, og_ref, h_ref, *, n_main):
    j = pl.program_id(1)

    @pl.when(j == 0)
    def _():
        h_ref[...] = _rms(x_ref[...], g_ref[...]).astype(BF16)

    acc = jnp.dot(h_ref[...], w_ref[...].astype(BF16), preferred_element_type=F32) + b_ref[...]

    @pl.when(j < n_main)
    def _():
        o_ref[...] = acc

    @pl.when(j >= n_main)
    def _():
        og_ref[...] = acc.astype(BF16)


def _norm_matmul_split(x, gain, w, bias, split_col, name):
    t, d = x.shape
    n = w.shape[1]
    tm = min(PROJ_TM, t)
    tn = PROJ_TN
    assert t % tm == 0 and n % tn == 0 and split_col % tn == 0 and 0 < split_col < n
    n_main = split_col // tn
    return pl.pallas_call(
        functools.partial(_norm_matmul_split_kernel, n_main=n_main),
        grid=(t // tm, n // tn),
        in_specs=[
            pl.BlockSpec((tm, d), lambda i, j: (i, 0)),
            pl.BlockSpec((1, d), lambda i, j: (0, 0)),
            pl.BlockSpec((d, tn), lambda i, j: (0, j)),
            pl.BlockSpec((1, tn), lambda i, j: (0, j)),
        ],
        out_specs=[
            pl.BlockSpec((tm, tn), lambda i, j: (i, jnp.minimum(j, n_main - 1))),
            pl.BlockSpec((tm, tn), lambda i, j: (i, jnp.maximum(j - n_main, 0))),
        ],
        out_shape=[jax.ShapeDtypeStruct((t, split_col), F32),
                   jax.ShapeDtypeStruct((t, n - split_col), BF16)],
        scratch_shapes=[pltpu.VMEM((tm, d), BF16)],
        compiler_params=pltpu.CompilerParams(
            dimension_semantics=("arbitrary", "arbitrary"), vmem_limit_bytes=VMEM_LIMIT),
        name=name,
    )(x, gain.reshape(1, d), w, bias.reshape(1, n))


def _pool_project(pooled, wpg_ref, scale_ref, g):
    sl = slice(g * POOL_GROUP_DIM, (g + 1) * POOL_GROUP_DIM)
    y = jnp.dot(pooled.astype(BF16), wpg_ref[g].astype(BF16), preferred_element_type=F32)
    return y * scale_ref[:, sl]


def _softmax_rows(s):
    m = jnp.max(s, axis=-1, keepdims=True)
    e = jnp.exp(s - m)
    return e / jnp.sum(e, axis=-1, keepdims=True)


def _mix_prompt_kernel(cur_ref, prev_ref, mk_ref, mv_ref, wpg_ref, scale_ref, wconv_ref,
                       br_ref, zst_ref, extu_ref, extz_ref):
    t = pl.program_id(1)
    tm = cur_ref.shape[0]
    has_prev = t > 0

    u = cur_ref[:, C_U:C_U + D_POOL]
    extu_ref[0:HIST, :] = jnp.where(has_prev, prev_ref[:, C_U:C_U + D_POOL], 0.0)
    extu_ref[HIST:HIST + tm, :] = u
    pos = t * tm + lax.broadcasted_iota(jnp.int32, (tm, 1), 0)
    for g, w in enumerate(POOL_WINDOWS):
        sl = slice(g * POOL_GROUP_DIM, (g + 1) * POOL_GROUP_DIM)
        s = extu_ref[HIST:HIST + tm, sl]
        for k in range(1, w):
            s = s + extu_ref[HIST - k:HIST - k + tm, sl]
        cnt = jnp.minimum(w, pos + 1).astype(F32)
        pooled = s / cnt - extu_ref[HIST:HIST + tm, sl]
        br_ref[:, sl] = _pool_project(pooled, wpg_ref, scale_ref, g).astype(BF16)

    z = cur_ref[:, C_C:C_C + D_CONV] * cur_ref[:, C_V:C_V + D_CONV]
    zprev = prev_ref[:, C_C:C_C + D_CONV] * prev_ref[:, C_V:C_V + D_CONV]
    extz_ref[0:HIST, :] = jnp.where(has_prev, zprev, 0.0)
    extz_ref[HIST:HIST + tm, :] = z
    y = extz_ref[HIST - 2:HIST - 2 + tm, :] * wconv_ref[0:1, :]
    y = y + extz_ref[HIST - 1:HIST - 1 + tm, :] * wconv_ref[1:2, :]
    y = y + extz_ref[HIST:HIST + tm, :] * wconv_ref[2:3, :]
    br_ref[:, D_POOL:D_POOL + D_CONV] = (cur_ref[:, C_B:C_B + D_CONV] * y).astype(BF16)
    zst_ref[0] = extz_ref[HIST + tm - 8:HIST + tm, :]

    for h in range(N_XHEADS):
        sl = slice(h * XHEAD_DIM, (h + 1) * XHEAD_DIM)
        qh = cur_ref[:, C_Q + h * XHEAD_DIM:C_Q + (h + 1) * XHEAD_DIM].astype(BF16)
        kh = mk_ref[0, :, sl].astype(BF16)
        vh = mv_ref[0, :, sl].astype(BF16)
        s = lax.dot_general(qh, kh, (((1,), (1,)), ((), ())), preferred_element_type=F32)
        p = _softmax_rows(s * (XHEAD_DIM ** -0.5))
        o = jnp.dot(p.astype(BF16), vh, preferred_element_type=F32)
        c0 = D_POOL + D_CONV + h * XHEAD_DIM
        br_ref[:, c0:c0 + XHEAD_DIM] = o.astype(BF16)


def _mix_prompt(proj, mk, mv, wpg, scale, wconv, batch, seq):
    tm = MIX_TM
    nt = seq // tm
    assert seq % tm == 0 and tm % HIST == 0
    rpb = tm // HIST
    return pl.pallas_call(
        _mix_prompt_kernel,
        grid=(batch, nt),
        in_specs=[
            pl.BlockSpec((tm, D_MIX), lambda b, t: (b * nt + t, 0)),
            pl.BlockSpec((HIST, D_MIX), lambda b, t: (jnp.maximum((b * nt + t) * rpb - 1, 0), 0)),
            pl.BlockSpec((1, N_MEM, D_XATTN), lambda b, t: (b, 0, 0)),
            pl.BlockSpec((1, N_MEM, D_XATTN), lambda b, t: (b, 0, 0)),
            pl.BlockSpec((len(POOL_WINDOWS), POOL_GROUP_DIM, POOL_GROUP_DIM), lambda b, t: (0, 0, 0)),
            pl.BlockSpec((1, D_POOL), lambda b, t: (0, 0)),
            pl.BlockSpec((CONV_WIDTH, D_CONV), lambda b, t: (0, 0)),
        ],
        out_specs=[
            pl.BlockSpec((tm, D_MODEL), lambda b, t: (b * nt + t, 0)),
            pl.BlockSpec((1, 8, D_CONV), lambda b, t: (b, 0, 0)),
        ],
        out_shape=[
            jax.ShapeDtypeStruct((batch * seq, D_MODEL), BF16),
            jax.ShapeDtypeStruct((batch, 8, D_CONV), F32),
        ],
        scratch_shapes=[pltpu.VMEM((HIST + tm, D_POOL), F32), pltpu.VMEM((HIST + tm, D_CONV), F32)],
        compiler_params=pltpu.CompilerParams(
            dimension_semantics=("arbitrary", "arbitrary"), vmem_limit_bytes=VMEM_LIMIT),
        name="mix_prompt",
    )(proj, proj, mk, mv, wpg, scale.reshape(1, D_POOL), wconv)


def _mix_sample_kernel(cur_ref, sp_ref, sc_ref, k_ref, v_ref, wpg_ref, scale_ref, wconv_ref,
                       br_ref, zst_ref, extu_ref, extz_ref):
    ns, ln = zst_ref.shape[0], zst_ref.shape[1]
    rows = ns * ln

    extu_ref[:, HIST - POOL_STATE_LEN:HIST, :] = sp_ref[...]
    extu_ref[:, HIST:HIST + ln, :] = cur_ref[:, C_U:C_U + D_POOL].reshape(ns, ln, D_POOL)
    for g, w in enumerate(POOL_WINDOWS):
        sl = slice(g * POOL_GROUP_DIM, (g + 1) * POOL_GROUP_DIM)
        s = extu_ref[:, HIST:HIST + ln, sl]
        for k in range(1, w):
            s = s + extu_ref[:, HIST - k:HIST - k + ln, sl]
        pooled = s / float(w) - extu_ref[:, HIST:HIST + ln, sl]
        pooled = pooled.reshape(rows, POOL_GROUP_DIM)
        br_ref[:, sl] = _pool_project(pooled, wpg_ref, scale_ref, g).astype(BF16)

    z = cur_ref[:, C_C:C_C + D_CONV] * cur_ref[:, C_V:C_V + D_CONV]
    extz_ref[:, HIST - 2:HIST, :] = sc_ref[...]
    extz_ref[:, HIST:HIST + ln, :] = z.reshape(ns, ln, D_CONV)
    y = extz_ref[:, HIST - 2:HIST - 2 + ln, :] * wconv_ref[0:1, :]
    y = y + extz_ref[:, HIST - 1:HIST - 1 + ln, :] * wconv_ref[1:2, :]
    y = y + extz_ref[:, HIST:HIST + ln, :] * wconv_ref[2:3, :]
    br_ref[:, D_POOL:D_POOL + D_CONV] = (
        cur_ref[:, C_B:C_B + D_CONV] * y.reshape(rows, D_CONV)).astype(BF16)
    zst_ref[...] = extz_ref[:, HIST:HIST + ln, :]

    q3 = cur_ref[:, C_Q:C_Q + D_XATTN].reshape(ns, ln, D_XATTN)
    q4 = jnp.concatenate([q3[:, :, h * XHEAD_DIM:(h + 1) * XHEAD_DIM] for h in range(N_XHEADS)], axis=1)
    s = jnp.einsum("nqd,nkd->nqk", q4.astype(BF16), k_ref[...].astype(BF16),
                   preferred_element_type=F32)
    row_head = lax.broadcasted_iota(jnp.int32, (N_XHEADS * ln, N_XHEADS * N_MEM), 0) // ln
    col_head = lax.broadcasted_iota(jnp.int32, (N_XHEADS * ln, N_XHEADS * N_MEM), 1) % N_XHEADS
    s = jnp.where((row_head == col_head)[None], s * (XHEAD_DIM ** -0.5), -jnp.inf)
    p = _softmax_rows(s)
    r = jnp.einsum("nqk,nkd->nqd", p.astype(BF16), v_ref[...].astype(BF16),
                   preferred_element_type=F32)
    for h in range(N_XHEADS):
        c0 = D_POOL + D_CONV + h * XHEAD_DIM
        br_ref[:, c0:c0 + XHEAD_DIM] = r[:, h * ln:(h + 1) * ln, :].reshape(rows, XHEAD_DIM).astype(BF16)


def _mix_sample(proj, state_pool, state_conv, mem_k, mem_v, wpg, scale, wconv, nseq, ln):
    ns = MIX_NS
    rows = ns * ln
    assert nseq % ns == 0 and ln == 8
    return pl.pallas_call(
        _mix_sample_kernel,
        grid=(nseq // ns,),
        in_specs=[
            pl.BlockSpec((rows, D_MIX), lambda s: (s, 0)),
            pl.BlockSpec((ns, POOL_STATE_LEN, D_POOL), lambda s: (s, 0, 0)),
            pl.BlockSpec((ns, CONV_WIDTH - 1, D_CONV), lambda s: (s, 0, 0)),
            pl.BlockSpec((ns, N_MEM * N_XHEADS, XHEAD_DIM), lambda s: (s, 0, 0)),
            pl.BlockSpec((ns, N_MEM * N_XHEADS, XHEAD_DIM), lambda s: (s, 0, 0)),
            pl.BlockSpec((len(POOL_WINDOWS), POOL_GROUP_DIM, POOL_GROUP_DIM), lambda s: (0, 0, 0)),
            pl.BlockSpec((1, D_POOL), lambda s: (0, 0)),
            pl.BlockSpec((CONV_WIDTH, D_CONV), lambda s: (0, 0)),
        ],
        out_specs=[
            pl.BlockSpec((rows, D_MODEL), lambda s: (s, 0)),
            pl.BlockSpec((ns, ln, D_CONV), lambda s: (s, 0, 0)),
        ],
        out_shape=[
            jax.ShapeDtypeStruct((nseq * ln, D_MODEL), BF16),
            jax.ShapeDtypeStruct((nseq, ln, D_CONV), F32),
        ],
        scratch_shapes=[pltpu.VMEM((ns, HIST + ln, D_POOL), F32),
                        pltpu.VMEM((ns, HIST + ln, D_CONV), F32)],
        compiler_params=pltpu.CompilerParams(
            dimension_semantics=("arbitrary",), vmem_limit_bytes=VMEM_LIMIT),
        name="mix_sample",
    )(proj, state_pool, state_conv, mem_k, mem_v, wpg, scale.reshape(1, D_POOL), wconv)


def _merge_route_kernel(x_ref, br_ref, g0_ref, g1_ref, g2_ref, wpo_ref, wco_ref, wao_ref, wo_ref,
                        nf_ref, wrhl_ref, wrh_ref, brt_ref,
                        x1_ref, h2_ref, ri_ref, rw_ref, cnt_ref, carry_ref):
    i = pl.program_id(0)
    tm = x_ref.shape[0]

    @pl.when(i == 0)
    def _():
        carry_ref[...] = jnp.zeros_like(carry_ref)

    merged = _sigmoid(g0_ref[...].astype(F32)) * jnp.dot(
        br_ref[:, 0:D_POOL], wpo_ref[...], preferred_element_type=F32)
    merged = merged + _sigmoid(g1_ref[...].astype(F32)) * jnp.dot(
        br_ref[:, D_POOL:D_POOL + D_CONV], wco_ref[...], preferred_element_type=F32)
    merged = merged + _sigmoid(g2_ref[...].astype(F32)) * jnp.dot(
        br_ref[:, D_POOL + D_CONV:D_MODEL], wao_ref[...], preferred_element_type=F32)
    x1 = x_ref[...] + jnp.dot(merged.astype(BF16), wo_ref[...], preferred_element_type=F32)
    x1_ref[...] = x1
    h2 = _rms(x1, nf_ref[...])
    h2_ref[...] = _pack_bf16_pairs(h2)

    h2_hi = h2.astype(BF16)
    h2_lo = (h2 - h2_hi.astype(F32)).astype(BF16)
    hi_part = jnp.dot(h2_hi, wrhl_ref[...], preferred_element_type=F32)
    logits = (hi_part[:, 0:N_EXPERTS] + hi_part[:, N_EXPERTS:2 * N_EXPERTS]
              + jnp.dot(h2_lo, wrh_ref[...], preferred_element_type=F32)) + brt_ref[...]
    lane = lax.broadcasted_iota(jnp.int32, (tm, N_EXPERTS), 1).astype(F32)
    vals, idxs, hots = [], [], []
    work = logits
    for _ in range(TOP_K):
        m = jnp.max(work, axis=-1, keepdims=True)
        idx = jnp.min(jnp.where(work == m, lane, float(N_EXPERTS)), axis=-1, keepdims=True)
        hot = lane == idx
        work = jnp.where(hot, -jnp.inf, work)
        vals.append(m)
        idxs.append(idx)
        hots.append(hot)
    es = [jnp.exp(v - vals[0]) for v in vals]
    denom = es[0] + es[1] + es[2] + es[3]

    chosen = jnp.where(hots[0] | hots[1] | hots[2] | hots[3], 1.0, 0.0).astype(BF16)
    r_i = lax.broadcasted_iota(jnp.int32, (tm, tm), 0)
    c_i = lax.broadcasted_iota(jnp.int32, (tm, tm), 1)
    lower = jnp.where(c_i < r_i, 1.0, 0.0).astype(BF16)
    before = jnp.dot(lower, chosen, preferred_element_type=F32) + carry_ref[0:1, 0:N_EXPERTS]
    carry_ref[0:1, 0:N_EXPERTS] = (carry_ref[0:1, 0:N_EXPERTS]
                                   + jnp.sum(chosen.astype(F32), axis=0, keepdims=True))
    cnt_ref[...] = carry_ref[...]

    out_lane = lax.broadcasted_iota(jnp.int32, (tm, LANES), 1)
    ri = jnp.zeros((tm, LANES), jnp.int32)
    rw = jnp.zeros((tm, LANES), F32)
    for k in range(TOP_K):
        rank = jnp.sum(jnp.where(hots[k], before, 0.0), axis=-1, keepdims=True).astype(jnp.int32)
        ri = jnp.where(out_lane == k, idxs[k].astype(jnp.int32), ri)
        ri = jnp.where(out_lane == TOP_K + k, rank, ri)
        rw = jnp.where(out_lane == k, es[k] / denom, rw)
    ri_ref[...] = ri
    rw_ref[...] = rw


def _merge_route(x, branch, gates, wpo, wco, wao, wo, norm_ffn, w_router, b_router, name):
    t = x.shape[0]
    tm = MERGE_TM
    assert t % tm == 0 and gates.shape == (t, N_BRANCH * D_MODEL)
    const = lambda i: (0, 0)
    wr_hi = w_router.astype(BF16)
    wr_lo = (w_router - wr_hi.astype(F32)).astype(BF16)
    return pl.pallas_call(
        _merge_route_kernel,
        grid=(t // tm,),
        in_specs=[
            pl.BlockSpec((tm, D_MODEL), lambda i: (i, 0)),
            pl.BlockSpec((tm, D_MODEL), lambda i: (i, 0)),
            pl.BlockSpec((tm, D_MODEL), lambda i: (i, 0)),
            pl.BlockSpec((tm, D_MODEL), lambda i: (i, 1)),
            pl.BlockSpec((tm, D_MODEL), lambda i: (i, 2)),
            pl.BlockSpec((D_POOL, D_MODEL), const, pipeline_mode=pl.Buffered(1)),
            pl.BlockSpec((D_CONV, D_MODEL), const, pipeline_mode=pl.Buffered(1)),
            pl.BlockSpec((D_XATTN, D_MODEL), const, pipeline_mode=pl.Buffered(1)),
            pl.BlockSpec((D_MODEL, D_MODEL), const, pipeline_mode=pl.Buffered(1)),
            pl.BlockSpec((1, D_MODEL), const),
            pl.BlockSpec((D_MODEL, 2 * N_EXPERTS), const),
            pl.BlockSpec((D_MODEL, N_EXPERTS), const),
            pl.BlockSpec((1, N_EXPERTS), const),
        ],
        out_specs=[
            pl.BlockSpec((tm, D_MODEL), lambda i: (i, 0)),
            pl.BlockSpec((tm, D_MODEL // 2), lambda i: (i, 0)),
            pl.BlockSpec((tm, LANES), lambda i: (i, 0)),
            pl.BlockSpec((tm, LANES), lambda i: (i, 0)),
            pl.BlockSpec((8, LANES), const),
        ],
        out_shape=[
            jax.ShapeDtypeStruct((t, D_MODEL), F32),
            jax.ShapeDtypeStruct((t, D_MODEL // 2), PACKED),
            jax.ShapeDtypeStruct((t, LANES), jnp.int32),
            jax.ShapeDtypeStruct((t, LANES), F32),
            jax.ShapeDtypeStruct((8, LANES), F32),
        ],
        scratch_shapes=[pltpu.VMEM((8, LANES), F32)],
        compiler_params=pltpu.CompilerParams(
            dimension_semantics=("arbitrary",), vmem_limit_bytes=VMEM_LIMIT),
        name=name,
    )(x, branch, gates, gates, gates, wpo, wco, wao, wo, norm_ffn.reshape(1, D_MODEL),
      jnp.concatenate([wr_hi, wr_lo], axis=1), wr_hi, b_router.reshape(1, N_EXPERTS))


def _fill_work_list(nblk_s, pad_s, nj, refs):
    te, tj, trow, tnb, tfirst, tne, tnj, tpar, ttot = refs

    def clear(i, c):
        for r in (te, tj, trow, tnb, tfirst, tpar):
            r[i] = 0
        tne[i] = -1
        tnj[i] = -1
        return c

    lax.fori_loop(0, te.shape[0], clear, 0)

    carry = (jnp.int32(0), jnp.int32(-1), jnp.int32(0))
    for j in range(nj):
        def per_expert(e, carry):
            q, prev, groups = carry
            n = nblk_s[e]
            nch = (n + EXP_CHUNK - 1) // EXP_CHUNK

            @pl.when((nch > 0) & (prev >= 0))
            def _():
                tne[prev] = e
                tnj[prev] = jnp.int32(j)

            def per_chunk(c, q):
                te[q] = e
                tj[q] = jnp.int32(j)
                trow[q] = pad_s[e] + c * (EXP_CHUNK * EXP_TM)
                tnb[q] = jnp.minimum(EXP_CHUNK, n - c * EXP_CHUNK)
                tfirst[q] = (c == 0).astype(jnp.int32)
                tpar[q] = lax.rem(groups, 2)
                return q + 1

            return (lax.fori_loop(0, nch, per_chunk, q), jnp.where(nch > 0, q, prev),
                    groups + (nch > 0).astype(jnp.int32))

        carry = lax.fori_loop(0, N_EXPERTS, per_expert, carry)
    ttot[0] = carry[0]


def _route_tables_kernel(cntp_ref, cnts_ref, rip_ref, ris_ref, dest_ref, *refs, nj_in, nj_out):
    tabs_in, tabs_out = refs[0:N_TABS], refs[N_TABS:2 * N_TABS]
    nblk_s, pad_s = refs[2 * N_TABS:]

    def per_expert(e, start_blk):
        n = (cntp_ref[e] + cnts_ref[e] + EXP_TM - 1) // EXP_TM
        nblk_s[e] = n
        pad_s[e] = start_blk * EXP_TM
        return start_blk + n

    lax.fori_loop(0, N_EXPERTS, per_expert, jnp.int32(0))
    _fill_work_list(nblk_s, pad_s, nj_in, tabs_in)
    _fill_work_list(nblk_s, pad_s, nj_out, tabs_out)

    tile = ROUTE_TM
    col0 = 0
    for ri_ref, is_sample in ((rip_ref, False), (ris_ref, True)):
        for r in range(ri_ref.shape[0] // tile):
            ri = ri_ref[r * tile:(r + 1) * tile, :]
            base = jnp.zeros_like(ri)
            for e in range(N_EXPERTS):
                first_row = pad_s[e] + cntp_ref[e] if is_sample else pad_s[e]
                base = jnp.where(ri == e, first_row, base)
            dest = base + pltpu.roll(ri, LANES - TOP_K, axis=1)
            dest_ref[:, col0:col0 + tile] = jnp.transpose(dest)[0:8, :]
            col0 += tile


def _route_tables(cnt_p, cnt_s, ri_p, ri_s, nj_in, nj_out, max_items):
    t_all = ri_p.shape[0] + ri_s.shape[0]
    assert ri_p.shape[0] % ROUTE_TM == 0 and ri_s.shape[0] % ROUTE_TM == 0
    smem = pl.BlockSpec(memory_space=pltpu.SMEM)

    def tables(nj):
        n = nj * max_items
        return [jax.ShapeDtypeStruct((n,), jnp.int32)] * (N_TABS - 1) + [jax.ShapeDtypeStruct((1,), jnp.int32)]

    out = pl.pallas_call(
        functools.partial(_route_tables_kernel, nj_in=nj_in, nj_out=nj_out),
        grid=(1,),
        in_specs=[smem, smem,
                  pl.BlockSpec(ri_p.shape, lambda i: (0, 0)), pl.BlockSpec(ri_s.shape, lambda i: (0, 0))],
        out_specs=[pl.BlockSpec((8, t_all), lambda i: (0, 0))] + [smem] * (2 * N_TABS),
        out_shape=[jax.ShapeDtypeStruct((8, t_all), jnp.int32)] + tables(nj_in) + tables(nj_out),
        scratch_shapes=[pltpu.SMEM((N_EXPERTS,), jnp.int32), pltpu.SMEM((N_EXPERTS,), jnp.int32)],
        compiler_params=pltpu.CompilerParams(
            dimension_semantics=("arbitrary",), vmem_limit_bytes=VMEM_LIMIT),
        name="route_tables",
    )(cnt_p, cnt_s, ri_p, ri_s)
    return out[0], tuple(out[1:1 + N_TABS]), tuple(out[1 + N_TABS:1 + 2 * N_TABS])


def _sc_mesh():
    return plsc.VectorSubcoreMesh(core_axis_name="c", subcore_axis_name="s",
                                  num_cores=SC_CORES, num_subcores=SC_SUBCORES)


def _sc_worker_id():
    return lax.axis_index("s") * SC_CORES + lax.axis_index("c")


def _sc_dispatch(h2_a, h2_b, dest_by_slot, rows_out):
    ta, w = h2_a.shape
    t = ta + h2_b.shape[0]
    per_w = t // SC_WORKERS
    n_chunks = per_w // SC_CHUNK
    assert t == SC_WORKERS * n_chunks * SC_CHUNK and ta % SC_CHUNK == 0
    idx = dest_by_slot[:TOP_K].reshape(TOP_K, SC_WORKERS, n_chunks, SC_CHUNK).transpose(1, 0, 2, 3)

    def body(a_hbm, b_hbm, idx_hbm, xs_hbm, idx_v, rows_v, sems):
        base = _sc_worker_id() * per_w
        pltpu.sync_copy(idx_hbm.at[_sc_worker_id()], idx_v)
        pending = [[], []]
        for c in range(n_chunks):
            b = c % 2
            for d in pending[b]:
                d.wait()
            tok0 = base + c * SC_CHUNK

            @pl.when(tok0 < ta)
            def _():
                pltpu.sync_copy(a_hbm.at[pl.ds(tok0, SC_CHUNK)], rows_v.at[b])

            @pl.when(tok0 >= ta)
            def _():
                pltpu.sync_copy(b_hbm.at[pl.ds(tok0 - ta, SC_CHUNK)], rows_v.at[b])

            pending[b] = [pltpu.async_copy(rows_v.at[b], xs_hbm.at[idx_v.at[k, c]], sems.at[b])
                          for k in range(TOP_K)]
        for b in range(2):
            for d in pending[b]:
                d.wait()

    return pl.kernel(
        body,
        out_type=jax.ShapeDtypeStruct((rows_out, w), h2_a.dtype),
        mesh=_sc_mesh(),
        scratch_types=[pltpu.VMEM((TOP_K, n_chunks, SC_CHUNK), jnp.int32),
                       pltpu.VMEM((2, SC_CHUNK, w), h2_a.dtype),
                       pltpu.SemaphoreType.DMA((2,))],
        name="sc_dispatch",
    )(h2_a, h2_b, idx)


def _sc_gather_rows(table, idx):
    n = idx.shape[0]
    w = table.shape[1]
    per_w = n // SC_WORKERS
    n_chunks = per_w // SC_CHUNK
    assert n == SC_WORKERS * n_chunks * SC_CHUNK
    idx3 = idx.reshape(SC_WORKERS, n_chunks, SC_CHUNK)

    def body(table_hbm, idx_hbm, out_hbm, idx_v, rows_v, gsem, wsems):
        base = _sc_worker_id() * per_w
        pltpu.sync_copy(idx_hbm.at[_sc_worker_id()], idx_v)
        pending = [None, None]
        for c in range(n_chunks):
            b = c % 2
            if pending[b] is not None:
                pending[b].wait()
            pltpu.async_copy(table_hbm.at[idx_v.at[c]], rows_v.at[b], gsem).wait()
            pending[b] = pltpu.async_copy(
                rows_v.at[b], out_hbm.at[pl.ds(base + c * SC_CHUNK, SC_CHUNK)], wsems.at[b])
        for b in range(2):
            if pending[b] is not None:
                pending[b].wait()

    return pl.kernel(
        body,
        out_type=jax.ShapeDtypeStruct((n, w), table.dtype),
        mesh=_sc_mesh(),
        scratch_types=[pltpu.VMEM((n_chunks, SC_CHUNK), jnp.int32),
                       pltpu.VMEM((2, SC_CHUNK, w), table.dtype),
                       pltpu.SemaphoreType.DMA,
                       pltpu.SemaphoreType.DMA((2,))],
        name="sc_gather_rows",
    )(table, idx3)


def _grouped_pipeline(tabs, total_ref, weight_copies, in_copy, out_copy, compute):
    te, tj, _, tnb, tfirst, tne, tnj, tpar = tabs
    total = total_ref[0]

    def wait_out(q):
        for m in range(1, EXP_CHUNK + 1):
            @pl.when(tnb[q] == m)
            def _():
                out_copy(q, lax.rem(q, 2), m).wait()

    for d in weight_copies(te[0], tj[0], tpar[0]):
        d.start()
    in_copy(0, 0).start()

    def body(q, carry):
        slot = lax.rem(q, 2)

        @pl.when(q + 1 < total)
        def _():
            in_copy(q + 1, 1 - slot).start()

        in_copy(q, slot).wait()

        @pl.when(tfirst[q] == 1)
        def _():
            for d in weight_copies(te[q], tj[q], tpar[q]):
                d.wait()

            @pl.when(tne[q] >= 0)
            def _():
                for d in weight_copies(tne[q], tnj[q], 1 - tpar[q]):
                    d.start()

        @pl.when(q >= 2)
        def _():
            wait_out(q - 2)

        for m in range(1, EXP_CHUNK + 1):
            @pl.when(tnb[q] == m)
            def _():
                compute(q, slot, m, tpar[q])
                out_copy(q, slot, m).start()

        return carry

    lax.fori_loop(0, total, body, 0)

    @pl.when(total >= 2)
    def _():
        wait_out(total - 2)

    wait_out(total - 1)


def _expert_in_kernel(*refs):
    tabs, total_ref = refs[:N_TABS - 1], refs[N_TABS - 1]
    xs_hbm, w_hbm, b_ref, act_hbm, wbuf, x_buf, a_buf, w_sem, x_sem, a_sem = refs[N_TABS:]
    te, tj, trow = tabs[:3]
    tm, tn = EXP_TM, a_buf.shape[2]
    nj = D_FF // tn

    def weight_copies(e, j, par):
        return [pltpu.make_async_copy(
            w_hbm.at[e, :, pl.ds(pl.multiple_of(half * D_FF + j * tn, tn), tn)],
            wbuf.at[2 * par + half], w_sem)
            for half in range(2)]

    def rows(q, m):
        return pl.ds(pl.multiple_of(trow[q], tm), m * tm)

    def x_copy(q, slot):
        return pltpu.make_async_copy(xs_hbm.at[rows(q, EXP_CHUNK)], x_buf.at[slot], x_sem.at[slot])

    def a_copy(q, slot, m):
        return pltpu.make_async_copy(
            a_buf.at[slot, pl.ds(0, m * tm)],
            act_hbm.at[rows(q, m), pl.ds(pl.multiple_of(tj[q] * tn, tn), tn)], a_sem.at[slot])

    def compute(q, slot, m, par):
        bias0 = te[q] * (2 * nj) + tj[q]
        x = jnp.concatenate(_unpack_bf16_pairs(x_buf[slot, 0:m * tm]), axis=1).astype(BF16)
        g = jnp.dot(x, wbuf[2 * par].astype(BF16), preferred_element_type=F32) + b_ref[bias0]
        up = jnp.dot(x, wbuf[2 * par + 1].astype(BF16), preferred_element_type=F32) + b_ref[bias0 + nj]
        g = jnp.minimum(g, SWIGLU_LIMIT)
        up = jnp.clip(up, -SWIGLU_LIMIT, SWIGLU_LIMIT)
        a_buf[slot, 0:m * tm] = (g * _sigmoid(SWIGLU_ALPHA * g) * (up + 1.0)).astype(BF16)

    _grouped_pipeline(tabs, total_ref, weight_copies, x_copy, a_copy, compute)


def _expert_in(tabs, xs, w_in, b_in):
    rows = xs.shape[0]
    tm, tn = EXP_TM, EXP_IN_TN
    nj = D_FF // tn
    bias = b_in.reshape(N_EXPERTS * 2 * nj, 1, tn)
    grid_spec = pltpu.PrefetchScalarGridSpec(
        num_scalar_prefetch=len(tabs),
        grid=(1,),
        in_specs=[
            pl.BlockSpec(memory_space=pl.ANY),
            pl.BlockSpec(memory_space=pl.ANY),
            pl.BlockSpec(bias.shape, lambda i, *_: (0, 0, 0), pipeline_mode=pl.Buffered(1)),
        ],
        out_specs=pl.BlockSpec(memory_space=pl.ANY),
        scratch_shapes=[pltpu.VMEM((4, D_MODEL, tn), F32),
                        pltpu.VMEM((2, EXP_CHUNK * tm, D_MODEL // 2), PACKED),
                        pltpu.VMEM((2, EXP_CHUNK * tm, tn), BF16),
                        pltpu.SemaphoreType.DMA, pltpu.SemaphoreType.DMA((2,)),
                        pltpu.SemaphoreType.DMA((2,))],
    )
    return pl.pallas_call(
        _expert_in_kernel,
        grid_spec=grid_spec,
        out_shape=jax.ShapeDtypeStruct((rows, D_FF), BF16),
        compiler_params=pltpu.CompilerParams(
            dimension_semantics=("arbitrary",), vmem_limit_bytes=VMEM_LIMIT),
        name="expert_in",
    )(*tabs, xs, w_in, bias)


def _expert_out_kernel(*refs):
    tabs, total_ref = refs[:N_TABS - 1], refs[N_TABS - 1]
    act_hbm, w_hbm, b_ref, ys_hbm, wbuf, a_buf, y_buf, w_sem, a_sem, y_sem = refs[N_TABS:]
    te, tj, trow = tabs[:3]
    tm, tnp = EXP_TM, y_buf.shape[2]
    tn = 2 * tnp
    nj = D_MODEL // tn

    def weight_copies(e, j, par):
        return [pltpu.make_async_copy(
            w_hbm.at[e, :, pl.ds(pl.multiple_of(j * tn, tn), tn)], wbuf.at[par], w_sem)]

    def rows(q, m):
        return pl.ds(pl.multiple_of(trow[q], tm), m * tm)

    def a_copy(q, slot):
        return pltpu.make_async_copy(act_hbm.at[rows(q, EXP_CHUNK)], a_buf.at[slot], a_sem.at[slot])

    def y_copy(q, slot, m):
        return pltpu.make_async_copy(
            y_buf.at[slot, pl.ds(0, m * tm)],
            ys_hbm.at[rows(q, m), pl.ds(pl.multiple_of(tj[q] * tnp, tnp), tnp)], y_sem.at[slot])

    def compute(q, slot, m, par):
        y = jnp.dot(a_buf[slot, 0:m * tm], wbuf[par].astype(BF16),
                    preferred_element_type=F32) + b_ref[te[q] * nj + tj[q]]
        y_buf[slot, 0:m * tm] = _pack_bf16_pairs(y)

    _grouped_pipeline(tabs, total_ref, weight_copies, a_copy, y_copy, compute)


def _expert_out(tabs, act, w_out, b_out):
    rows = act.shape[0]
    tm, tn = EXP_TM, EXP_OUT_TN
    nj = D_MODEL // tn
    bias = b_out.reshape(N_EXPERTS * nj, 1, tn)
    grid_spec = pltpu.PrefetchScalarGridSpec(
        num_scalar_prefetch=len(tabs),
        grid=(1,),
        in_specs=[
            pl.BlockSpec(memory_space=pl.ANY),
            pl.BlockSpec(memory_space=pl.ANY),
            pl.BlockSpec(bias.shape, lambda i, *_: (0, 0, 0), pipeline_mode=pl.Buffered(1)),
        ],
        out_specs=pl.BlockSpec(memory_space=pl.ANY),
        scratch_shapes=[pltpu.VMEM((2, D_FF, tn), F32),
                        pltpu.VMEM((2, EXP_CHUNK * tm, D_FF), BF16),
                        pltpu.VMEM((2, EXP_CHUNK * tm, tn // 2), PACKED),
                        pltpu.SemaphoreType.DMA, pltpu.SemaphoreType.DMA((2,)),
                        pltpu.SemaphoreType.DMA((2,))],
    )
    return pl.pallas_call(
        _expert_out_kernel,
        grid_spec=grid_spec,
        out_shape=jax.ShapeDtypeStruct((rows, D_MODEL // 2), PACKED),
        compiler_params=pltpu.CompilerParams(
            dimension_semantics=("arbitrary",), vmem_limit_bytes=VMEM_LIMIT),
        name="expert_out",
    )(*tabs, act, w_out, bias)


def _combine_kernel(x1_ref, yg_ref, rw_ref, g_ref, o_ref):
    half = EXP_OUT_TN // 2
    pieces = []
    ssq = jnp.zeros((x1_ref.shape[0], 1), F32)
    for jt in range(D_MODEL // EXP_OUT_TN):
        acc_lo = x1_ref[:, jt * EXP_OUT_TN:jt * EXP_OUT_TN + half]
        acc_hi = x1_ref[:, jt * EXP_OUT_TN + half:(jt + 1) * EXP_OUT_TN]
        moe_lo = jnp.zeros_like(acc_lo)
        moe_hi = jnp.zeros_like(acc_hi)
        for k in range(TOP_K):
            lo, hi = _unpack_bf16_pairs(yg_ref[k, :, jt * half:(jt + 1) * half])
            moe_lo = moe_lo + lo * rw_ref[:, k:k + 1]
            moe_hi = moe_hi + hi * rw_ref[:, k:k + 1]
        for acc in (acc_lo + moe_lo, acc_hi + moe_hi):
            ssq = ssq + jnp.sum(acc * acc, axis=-1, keepdims=True)
            pieces.append(acc)
    inv = lax.rsqrt(ssq / D_MODEL + EPS)
    for n, acc in enumerate(pieces):
        o_ref[:, n * half:(n + 1) * half] = acc * inv * g_ref[:, n * half:(n + 1) * half]


def _combine_part_kernel(x1_ref, yg_ref, rw_ref, g_ref, prev_hbm, o_ref):
    del prev_hbm
    _combine_kernel(x1_ref, yg_ref, rw_ref, g_ref, o_ref)


def _combine(x1, yg, rw, gain, row0, prev, name):
    t = x1.shape[0]
    n = yg.shape[1]
    tm = COMB_TM
    assert n % tm == 0 and row0 % tm == 0
    off = row0 // tm
    in_specs = [
        pl.BlockSpec((tm, D_MODEL), lambda i: (i + off, 0)),
        pl.BlockSpec((TOP_K, tm, D_MODEL // 2), lambda i: (0, i, 0)),
        pl.BlockSpec((tm, LANES), lambda i: (i + off, 0)),
        pl.BlockSpec((1, D_MODEL), lambda i: (0, 0)),
    ]
    args = [x1, yg, rw, gain.reshape(1, D_MODEL)]
    if prev is not None:
        in_specs.append(pl.BlockSpec(memory_space=pl.ANY))
        args.append(prev)
    return pl.pallas_call(
        _combine_kernel if prev is None else _combine_part_kernel,
        grid=(n // tm,),
        in_specs=in_specs,
        out_specs=pl.BlockSpec((tm, D_MODEL), lambda i: (i + off, 0)),
        out_shape=jax.ShapeDtypeStruct((t, D_MODEL), F32),
        input_output_aliases={} if prev is None else {len(args) - 1: 0},
        compiler_params=pltpu.CompilerParams(
            dimension_semantics=("arbitrary",), vmem_limit_bytes=VMEM_LIMIT),
        name=name,
    )(*args)


def kernel(x_prompt, x_sample, state_pool, state_conv, cache_mem_k, cache_mem_v, mem_prompt,
           norm_mix, w_in, b_gate, w_pool_group, pool_scale, w_conv, mem_norm, w_mem_kv,
           w_pool_out, w_conv_out, w_attn_out, w_o, norm_ffn, w_router, b_router,
           w_exp_in, b_exp_in, w_exp_out, b_exp_out, final_norm):
    depth = norm_mix.shape[0]
    assert depth == 1
    l = 0
    bp, seq, _ = x_prompt.shape
    bs, ln, _ = x_sample.shape
    tp, ts = bp * seq, bs * ln

    kv = _norm_matmul(mem_prompt.reshape(bp * N_MEM, D_MODEL), mem_norm[l], w_mem_kv[l],
                      jnp.zeros((2 * D_XATTN,), F32), "mem_kv")
    mk = kv[:, :D_XATTN].reshape(bp, N_MEM, D_XATTN)
    mv = kv[:, D_XATTN:].reshape(bp, N_MEM, D_XATTN)

    bias_in = jnp.concatenate([jnp.zeros((D_MIX,), F32), b_gate[l]])
    xp = x_prompt.reshape(tp, D_MODEL)
    xs_ = x_sample.reshape(ts, D_MODEL)
    proj_p, gates_p = _norm_matmul_split(xp, norm_mix[l], w_in[l], bias_in, D_MIX, "proj_prompt")
    proj_s, gates_s = _norm_matmul_split(xs_, norm_mix[l], w_in[l], bias_in, D_MIX, "proj_sample")

    br_p, zst_p = _mix_prompt(proj_p, mk, mv, w_pool_group[l], pool_scale[l], w_conv[l], bp, seq)
    br_s, zst_s = _mix_sample(proj_s, state_pool[l], state_conv[l],
                              cache_mem_k[l].reshape(bs, N_MEM * N_XHEADS, XHEAD_DIM),
                              cache_mem_v[l].reshape(bs, N_MEM * N_XHEADS, XHEAD_DIM),
                              w_pool_group[l], pool_scale[l], w_conv[l], bs, ln)

    wpo, wco, wao, wo = (w.astype(BF16) for w in (w_pool_out[l], w_conv_out[l], w_attn_out[l], w_o[l]))
    x1_p, h2_p, ri_p, rw_p, cnt_p = _merge_route(xp, br_p, gates_p, wpo, wco, wao, wo, norm_ffn[l],
                                                  w_router[l], b_router[l], "merge_route_prompt")
    x1_s, h2_s, ri_s, rw_s, cnt_s = _merge_route(xs_, br_s, gates_s, wpo, wco, wao, wo, norm_ffn[l],
                                                  w_router[l], b_router[l], "merge_route_sample")

    t_all = tp + ts
    n_assign = t_all * TOP_K
    nb_max = n_assign // EXP_TM + N_EXPERTS
    max_items = nb_max // EXP_CHUNK + N_EXPERTS
    dest, tabs_in, tabs_out = _route_tables(
        cnt_p[0, :N_EXPERTS].astype(jnp.int32), cnt_s[0, :N_EXPERTS].astype(jnp.int32), ri_p, ri_s,
        D_FF // EXP_IN_TN, D_MODEL // EXP_OUT_TN, max_items)
    xs_sorted = _sc_dispatch(h2_p, h2_s, dest, (nb_max + EXP_CHUNK) * EXP_TM)

    act = _expert_in(tabs_in, xs_sorted, w_exp_in[l], b_exp_in[l])
    ys = _expert_out(tabs_out, act, w_exp_out[l], b_exp_out[l])
    def combine_part(x1, rw, tok0, n, row0, prev, name):
        idx = dest[:TOP_K, tok0:tok0 + n].reshape(-1)
        yg = _sc_gather_rows(ys, idx).reshape(TOP_K, n, D_MODEL // 2)
        return _combine(x1, yg, rw, final_norm, row0, prev, name)

    y_s = combine_part(x1_s, rw_s, tp, ts, 0, None, "combine_sample")
    y_p = None
    tok0 = 0
    for c, units in enumerate(COMB_PARTS):
        n = tp * units // sum(COMB_PARTS)
        y_p = combine_part(x1_p, rw_p, tok0, n, tok0, y_p, f"combine_prompt_{c}")
        tok0 += n
    assert tok0 == tp

    new_pool_p = proj_p.reshape(bp, seq, D_MIX)[:, seq - POOL_STATE_LEN:, :D_POOL]
    new_conv_p = zst_p[:, 8 - (CONV_WIDTH - 1):]
    u_s = proj_s[:, :D_POOL].reshape(bs, ln, D_POOL)
    new_pool_s = jnp.concatenate([state_pool[l], u_s], axis=1)[:, -POOL_STATE_LEN:]
    new_conv_s = zst_s[:, ln - (CONV_WIDTH - 1):]

    return (y_p.reshape(bp, seq, D_MODEL), y_s.reshape(bs, ln, D_MODEL),
            new_pool_p[None], new_conv_p[None],
            mk.reshape(1, bp, N_MEM, N_XHEADS, XHEAD_DIM), mv.reshape(1, bp, N_MEM, N_XHEADS, XHEAD_DIM),
            new_pool_s[None], new_conv_s[None])
```

```python
import functools

import jax
import jax.numpy as jnp
from jax import lax
from jax.experimental import pallas as pl
from jax.experimental.pallas import tpu as pltpu
from jax.experimental.pallas import tpu_sc as plsc

F32 = jnp.float32
BF16 = jnp.bfloat16
PACKED = jnp.int32

D_MODEL = 2048
POOL_WINDOWS = (2, 4, 8, 16)
POOL_GROUP_DIM = 128
D_POOL = 512
POOL_STATE_LEN = 15
D_CONV = 1024
CONV_WIDTH = 3
N_MEM = 256
N_XHEADS = 4
XHEAD_DIM = 128
D_XATTN = 512
N_BRANCH = 3
D_MIX = D_POOL + 3 * D_CONV + D_XATTN
D_IN_TOTAL = D_MIX + N_BRANCH * D_MODEL
N_EXPERTS = 32
TOP_K = 4
D_FF = D_MODEL
SWIGLU_LIMIT = 7.0
SWIGLU_ALPHA = 1.702
EPS = 1e-5

C_U = 0
C_V = D_POOL
C_B = D_POOL + D_CONV
C_C = D_POOL + 2 * D_CONV
C_Q = D_POOL + 3 * D_CONV

LANES = 128
HIST = 16

PROJ_TM = 1024
PROJ_TN = 1024
MIX_TM = 512
MIX_NS = 8
MERGE_TM = 256
ROUTE_TM = 1024
N_TABS = 9
EXP_TM = 128
EXP_CHUNK = 8
EXP_IN_TN = 1024
EXP_OUT_TN = 2048
COMB_TM = 512
COMB_PARTS = (2, 3, 3)
SC_CORES = 2
SC_SUBCORES = 16
SC_WORKERS = SC_CORES * SC_SUBCORES
SC_CHUNK = 32
VMEM_LIMIT = 56 * 1024 * 1024


def _sigmoid(x):
    return 0.5 * (jnp.tanh(0.5 * x) + 1.0)


def _rms(x, g):
    ms = jnp.mean(x * x, axis=-1, keepdims=True)
    return x * lax.rsqrt(ms + EPS) * g


def _pack_bf16_pairs(x):
    n = x.shape[1] // 2
    lo = lax.bitcast_convert_type(x[:, :n].astype(BF16).astype(F32), jnp.uint32)
    hi = lax.bitcast_convert_type(x[:, n:].astype(BF16).astype(F32), jnp.uint32)
    return lax.bitcast_convert_type((hi & jnp.uint32(0xFFFF0000)) | (lo >> 16), PACKED)


def _unpack_bf16_pairs(p):
    p = lax.bitcast_convert_type(p, jnp.uint32)
    lo = lax.bitcast_convert_type(p << 16, F32)
    hi = lax.bitcast_convert_type(p & jnp.uint32(0xFFFF0000), F32)
    return lo, hi


def _norm_matmul_kernel(x_ref, g_ref, w_ref, b_ref, o_ref, h_ref):
    @pl.when(pl.program_id(1) == 0)
    def _():
        h_ref[...] = _rms(x_ref[...], g_ref[...]).astype(BF16)

    o_ref[...] = jnp.dot(h_ref[...], w_ref[...].astype(BF16), preferred_element_type=F32) + b_ref[...]


def _norm_matmul(x, gain, w, bias, name):
    t, d = x.shape
    n = w.shape[1]
    tm = min(PROJ_TM, t)
    tn = PROJ_TN
    assert t % tm == 0 and n % tn == 0
    return pl.pallas_call(
        _norm_matmul_kernel,
        grid=(t // tm, n // tn),
        in_specs=[
            pl.BlockSpec((tm, d), lambda i, j: (i, 0)),
            pl.BlockSpec((1, d), lambda i, j: (0, 0)),
            pl.BlockSpec((d, tn), lambda i, j: (0, j)),
            pl.BlockSpec((1, tn), lambda i, j: (0, j)),
        ],
        out_specs=pl.BlockSpec((tm, tn), lambda i, j: (i, j)),
        out_shape=jax.ShapeDtypeStruct((t, n), F32),
        scratch_shapes=[pltpu.VMEM((tm, d), BF16)],
        compiler_params=pltpu.CompilerParams(
            dimension_semantics=("arbitrary", "arbitrary"), vmem_limit_bytes=VMEM_LIMIT),
        name=name,
    )(x, gain.reshape(1, d), w, bias.reshape(1, n))


def _norm_matmul_split_kernel(x_ref, g_ref, w_ref, b_ref, o_ref, og_ref, h_ref, *, n_main):
    j = pl.program_id(1)

    @pl.when(j == 0)
    def _():
        h_ref[...] = _rms(x_ref[...], g_ref[...]).astype(BF16)

    acc = jnp.dot(h_ref[...], w_ref[...].astype(BF16), preferred_element_type=F32) + b_ref[...]

    @pl.when(j < n_main)
    def _():
        o_ref[...] = acc

    @pl.when(j >= n_main)
    def _():
        og_ref[...] = acc.astype(BF16)


def _norm_matmul_split(x, gain, w, bias, split_col, name):
    t, d = x.shape
    n = w.shape[1]
    tm = min(PROJ_TM, t)
    tn = PROJ_TN
    assert t % tm == 0 and n % tn == 0 and split_col % tn == 0 and 0 < split_col < n
    n_main = split_col // tn
    return pl.pallas_call(
        functools.partial(_norm_matmul_split_kernel, n_main=n_main),
        grid=(t // tm, n // tn),
        in_specs=[
            pl.BlockSpec((tm, d), lambda i, j: (i, 0)),
            pl.BlockSpec((1, d), lambda i, j: (0, 0)),
            pl.BlockSpec((d, tn), lambda i, j: (0, j)),
            pl.BlockSpec((1, tn), lambda i, j: (0, j)),
        ],
        out_specs=[
            pl.BlockSpec((tm, tn), lambda i, j: (i, jnp.minimum(j, n_main - 1))),
            pl.BlockSpec((tm, tn), lambda i, j: (i, jnp.maximum(j - n_main, 0))),
        ],
        out_shape=[jax.ShapeDtypeStruct((t, split_col), F32),
                   jax.ShapeDtypeStruct((t, n - split_col), BF16)],
        scratch_shapes=[pltpu.VMEM((tm, d), BF16)],
        compiler_params=pltpu.CompilerParams(
            dimension_semantics=("arbitrary", "arbitrary"), vmem_limit_bytes=VMEM_LIMIT),
        name=name,
    )(x, gain.reshape(1, d), w, bias.reshape(1, n))


def _pool_project(pooled, wpg_ref, scale_ref, g):
    sl = slice(g * POOL_GROUP_DIM, (g + 1) * POOL_GROUP_DIM)
    y = jnp.dot(pooled.astype(BF16), wpg_ref[g].astype(BF16), preferred_element_type=F32)
    return y * scale_ref[:, sl]


def _softmax_rows(s):
    m = jnp.max(s, axis=-1, keepdims=True)
    e = jnp.exp(s - m)
    return e / jnp.sum(e, axis=-1, keepdims=True)


def _mix_prompt_kernel(cur_ref, prev_ref, mk_ref, mv_ref, wpg_ref, scale_ref, wconv_ref,
                       br_ref, zst_ref, extu_ref, extz_ref):
    t = pl.program_id(1)
    tm = cur_ref.shape[0]
    has_prev = t > 0

    u = cur_ref[:, C_U:C_U + D_POOL]
    extu_ref[0:HIST, :] = jnp.where(has_prev, prev_ref[:, C_U:C_U + D_POOL], 0.0)
    extu_ref[HIST:HIST + tm, :] = u
    pos = t * tm + lax.broadcasted_iota(jnp.int32, (tm, 1), 0)
    for g, w in enumerate(POOL_WINDOWS):
        sl = slice(g * POOL_GROUP_DIM, (g + 1) * POOL_GROUP_DIM)
        s = extu_ref[HIST:HIST + tm, sl]
        for k in range(1, w):
            s = s + extu_ref[HIST - k:HIST - k + tm, sl]
        cnt = jnp.minimum(w, pos + 1).astype(F32)
        pooled = s / cnt - extu_ref[HIST:HIST + tm, sl]
        br_ref[:, sl] = _pool_project(pooled, wpg_ref, scale_ref, g).astype(BF16)

    z = cur_ref[:, C_C:C_C + D_CONV] * cur_ref[:, C_V:C_V + D_CONV]
    zprev = prev_ref[:, C_C:C_C + D_CONV] * prev_ref[:, C_V:C_V + D_CONV]
    extz_ref[0:HIST, :] = jnp.where(has_prev, zprev, 0.0)
    extz_ref[HIST:HIST + tm, :] = z
    y = extz_ref[HIST - 2:HIST - 2 + tm, :] * wconv_ref[0:1, :]
    y = y + extz_ref[HIST - 1:HIST - 1 + tm, :] * wconv_ref[1:2, :]
    y = y + extz_ref[HIST:HIST + tm, :] * wconv_ref[2:3, :]
    br_ref[:, D_POOL:D_POOL + D_CONV] = (cur_ref[:, C_B:C_B + D_CONV] * y).astype(BF16)
    zst_ref[0] = extz_ref[HIST + tm - 8:HIST + tm, :]

    for h in range(N_XHEADS):
        sl = slice(h * XHEAD_DIM, (h + 1) * XHEAD_DIM)
        qh = cur_ref[:, C_Q + h * XHEAD_DIM:C_Q + (h + 1) * XHEAD_DIM].astype(BF16)
        kh = mk_ref[0, :, sl].astype(BF16)
        vh = mv_ref[0, :, sl].astype(BF16)
        s = lax.dot_general(qh, kh, (((1,), (1,)), ((), ())), preferred_element_type=F32)
        p = _softmax_rows(s * (XHEAD_DIM ** -0.5))
        o = jnp.dot(p.astype(BF16), vh, preferred_element_type=F32)
        c0 = D_POOL + D_CONV + h * XHEAD_DIM
        br_ref[:, c0:c0 + XHEAD_DIM] = o.astype(BF16)


def _mix_prompt(proj, mk, mv, wpg, scale, wconv, batch, seq):
    tm = MIX_TM
    nt = seq // tm
    assert seq % tm == 0 and tm % HIST == 0
    rpb = tm // HIST
    return pl.pallas_call(
        _mix_prompt_kernel,
        grid=(batch, nt),
        in_specs=[
            pl.BlockSpec((tm, D_MIX), lambda b, t: (b * nt + t, 0)),
            pl.BlockSpec((HIST, D_MIX), lambda b, t: (jnp.maximum((b * nt + t) * rpb - 1, 0), 0)),
            pl.BlockSpec((1, N_MEM, D_XATTN), lambda b, t: (b, 0, 0)),
            pl.BlockSpec((1, N_MEM, D_XATTN), lambda b, t: (b, 0, 0)),
            pl.BlockSpec((len(POOL_WINDOWS), POOL_GROUP_DIM, POOL_GROUP_DIM), lambda b, t: (0, 0, 0)),
            pl.BlockSpec((1, D_POOL), lambda b, t: (0, 0)),
            pl.BlockSpec((CONV_WIDTH, D_CONV), lambda b, t: (0, 0)),
        ],
        out_specs=[
            pl.BlockSpec((tm, D_MODEL), lambda b, t: (b * nt + t, 0)),
            pl.BlockSpec((1, 8, D_CONV), lambda b, t: (b, 0, 0)),
        ],
        out_shape=[
            jax.ShapeDtypeStruct((batch * seq, D_MODEL), BF16),
            jax.ShapeDtypeStruct((batch, 8, D_CONV), F32),
        ],
        scratch_shapes=[pltpu.VMEM((HIST + tm, D_POOL), F32), pltpu.VMEM((HIST + tm, D_CONV), F32)],
        compiler_params=pltpu.CompilerParams(
            dimension_semantics=("arbitrary", "arbitrary"), vmem_limit_bytes=VMEM_LIMIT),
        name="mix_prompt",
    )(proj, proj, mk, mv, wpg, scale.reshape(1, D_POOL), wconv)


def _mix_sample_kernel(cur_ref, sp_ref, sc_ref, k_ref, v_ref, wpg_ref, scale_ref, wconv_ref,
                       br_ref, zst_ref, extu_ref, extz_ref):
    ns, ln = zst_ref.shape[0], zst_ref.shape[1]
    rows = ns * ln

    extu_ref[:, HIST - POOL_STATE_LEN:HIST, :] = sp_ref[...]
    extu_ref[:, HIST:HIST + ln, :] = cur_ref[:, C_U:C_U + D_POOL].reshape(ns, ln, D_POOL)
    for g, w in enumerate(POOL_WINDOWS):
        sl = slice(g * POOL_GROUP_DIM, (g + 1) * POOL_GROUP_DIM)
        s = extu_ref[:, HIST:HIST + ln, sl]
        for k in range(1, w):
            s = s + extu_ref[:, HIST - k:HIST - k + ln, sl]
        pooled = s / float(w) - extu_ref[:, HIST:HIST + ln, sl]
        pooled = pooled.reshape(rows, POOL_GROUP_DIM)
        br_ref[:, sl] = _pool_project(pooled, wpg_ref, scale_ref, g).astype(BF16)

    z = cur_ref[:, C_C:C_C + D_CONV] * cur_ref[:, C_V:C_V + D_CONV]
    extz_ref[:, HIST - 2:HIST, :] = sc_ref[...]
    extz_ref[:, HIST:HIST + ln, :] = z.reshape(ns, ln, D_CONV)
    y = extz_ref[:, HIST - 2:HIST - 2 + ln, :] * wconv_ref[0:1, :]
    y = y + extz_ref[:, HIST - 1:HIST - 1 + ln, :] * wconv_ref[1:2, :]
    y = y + extz_ref[:, HIST:HIST + ln, :] * wconv_ref[2:3, :]
    br_ref[:, D_POOL:D_POOL + D_CONV] = (
        cur_ref[:, C_B:C_B + D_CONV] * y.reshape(rows, D_CONV)).astype(BF16)
    zst_ref[...] = extz_ref[:, HIST:HIST + ln, :]

    q3 = cur_ref[:, C_Q:C_Q + D_XATTN].reshape(ns, ln, D_XATTN)
    q4 = jnp.concatenate([q3[:, :, h * XHEAD_DIM:(h + 1) * XHEAD_DIM] for h in range(N_XHEADS)], axis=1)
    s = jnp.einsum("nqd,nkd->nqk", q4.astype(BF16), k_ref[...].astype(BF16),
                   preferred_element_type=F32)
    row_head = lax.broadcasted_iota(jnp.int32, (N_XHEADS * ln, N_XHEADS * N_MEM), 0) // ln
    col_head = lax.broadcasted_iota(jnp.int32, (N_XHEADS * ln, N_XHEADS * N_MEM), 1) % N_XHEADS
    s = jnp.where((row_head == col_head)[None], s * (XHEAD_DIM ** -0.5), -jnp.inf)
    p = _softmax_rows(s)
    r = jnp.einsum("nqk,nkd->nqd", p.astype(BF16), v_ref[...].astype(BF16),
                   preferred_element_type=F32)
    for h in range(N_XHEADS):
        c0 = D_POOL + D_CONV + h * XHEAD_DIM
        br_ref[:, c0:c0 + XHEAD_DIM] = r[:, h * ln:(h + 1) * ln, :].reshape(rows, XHEAD_DIM).astype(BF16)


def _mix_sample(proj, state_pool, state_conv, mem_k, mem_v, wpg, scale, wconv, nseq, ln):
    ns = MIX_NS
    rows = ns * ln
    assert nseq % ns == 0 and ln == 8
    return pl.pallas_call(
        _mix_sample_kernel,
        grid=(nseq // ns,),
        in_specs=[
            pl.BlockSpec((rows, D_MIX), lambda s: (s, 0)),
            pl.BlockSpec((ns, POOL_STATE_LEN, D_POOL), lambda s: (s, 0, 0)),
            pl.BlockSpec((ns, CONV_WIDTH - 1, D_CONV), lambda s: (s, 0, 0)),
            pl.BlockSpec((ns, N_MEM * N_XHEADS, XHEAD_DIM), lambda s: (s, 0, 0)),
            pl.BlockSpec((ns, N_MEM * N_XHEADS, XHEAD_DIM), lambda s: (s, 0, 0)),
            pl.BlockSpec((len(POOL_WINDOWS), POOL_GROUP_DIM, POOL_GROUP_DIM), lambda s: (0, 0, 0)),
            pl.BlockSpec((1, D_POOL), lambda s: (0, 0)),
            pl.BlockSpec((CONV_WIDTH, D_CONV), lambda s: (0, 0)),
        ],
        out_specs=[
            pl.BlockSpec((rows, D_MODEL), lambda s: (s, 0)),
            pl.BlockSpec((ns, ln, D_CONV), lambda s: (s, 0, 0)),
        ],
        out_shape=[
            jax.ShapeDtypeStruct((nseq * ln, D_MODEL), BF16),
            jax.ShapeDtypeStruct((nseq, ln, D_CONV), F32),
        ],
        scratch_shapes=[pltpu.VMEM((ns, HIST + ln, D_POOL), F32),
                        pltpu.VMEM((ns, HIST + ln, D_CONV), F32)],
        compiler_params=pltpu.CompilerParams(
            dimension_semantics=("arbitrary",), vmem_limit_bytes=VMEM_LIMIT),
        name="mix_sample",
    )(proj, state_pool, state_conv, mem_k, mem_v, wpg, scale.reshape(1, D_POOL), wconv)


def _merge_route_kernel(x_ref, br_ref, g0_ref, g1_ref, g2_ref, wpo_ref, wco_ref, wao_ref, wo_ref,
                        nf_ref, wrhl_ref, wrh_ref, brt_ref,
                        x1_ref, h2_ref, ri_ref, rw_ref, cnt_ref, carry_ref):
    i = pl.program_id(0)
    tm = x_ref.shape[0]

    @pl.when(i == 0)
    def _():
        carry_ref[...] = jnp.zeros_like(carry_ref)

    merged = _sigmoid(g0_ref[...].astype(F32)) * jnp.dot(
        br_ref[:, 0:D_POOL], wpo_ref[...], preferred_element_type=F32)
    merged = merged + _sigmoid(g1_ref[...].astype(F32)) * jnp.dot(
        br_ref[:, D_POOL:D_POOL + D_CONV], wco_ref[...], preferred_element_type=F32)
    merged = merged + _sigmoid(g2_ref[...].astype(F32)) * jnp.dot(
        br_ref[:, D_POOL + D_CONV:D_MODEL], wao_ref[...], preferred_element_type=F32)
    x1 = x_ref[...] + jnp.dot(merged.astype(BF16), wo_ref[...], preferred_element_type=F32)
    x1_ref[...] = x1
    h2 = _rms(x1, nf_ref[...])
    h2_ref[...] = _pack_bf16_pairs(h2)

    h2_hi = h2.astype(BF16)
    h2_lo = (h2 - h2_hi.astype(F32)).astype(BF16)
    hi_part = jnp.dot(h2_hi, wrhl_ref[...], preferred_element_type=F32)
    logits = (hi_part[:, 0:N_EXPERTS] + hi_part[:, N_EXPERTS:2 * N_EXPERTS]
              + jnp.dot(h2_lo, wrh_ref[...], preferred_element_type=F32)) + brt_ref[...]
    lane = lax.broadcasted_iota(jnp.int32, (tm, N_EXPERTS), 1).astype(F32)
    vals, idxs, hots = [], [], []
    work = logits
    for _ in range(TOP_K):
        m = jnp.max(work, axis=-1, keepdims=True)
        idx = jnp.min(jnp.where(work == m, lane, float(N_EXPERTS)), axis=-1, keepdims=True)
        hot = lane == idx
        work = jnp.where(hot, -jnp.inf, work)
        vals.append(m)
        idxs.append(idx)
        hots.append(hot)
    es = [jnp.exp(v - vals[0]) for v in vals]
    denom = es[0] + es[1] + es[2] + es[3]

    chosen = jnp.where(hots[0] | hots[1] | hots[2] | hots[3], 1.0, 0.0).astype(BF16)
    r_i = lax.broadcasted_iota(jnp.int32, (tm, tm), 0)
    c_i = lax.broadcasted_iota(jnp.int32, (tm, tm), 1)
    lower = jnp.where(c_i < r_i, 1.0, 0.0).astype(BF16)
    before = jnp.dot(lower, chosen, preferred_element_type=F32) + carry_ref[0:1, 0:N_EXPERTS]
    carry_ref[0:1, 0:N_EXPERTS] = (carry_ref[0:1, 0:N_EXPERTS]
                                   + jnp.sum(chosen.astype(F32), axis=0, keepdims=True))
    cnt_ref[...] = carry_ref[...]

    out_lane = lax.broadcasted_iota(jnp.int32, (tm, LANES), 1)
    ri = jnp.zeros((tm, LANES), jnp.int32)
    rw = jnp.zeros((tm, LANES), F32)
    for k in range(TOP_K):
        rank = jnp.sum(jnp.where(hots[k], before, 0.0), axis=-1, keepdims=True).astype(jnp.int32)
        ri = jnp.where(out_lane == k, idxs[k].astype(jnp.int32), ri)
        ri = jnp.where(out_lane == TOP_K + k, rank, ri)
        rw = jnp.where(out_lane == k, es[k] / denom, rw)
    ri_ref[...] = ri
    rw_ref[...] = rw


def _merge_route(x, branch, gates, wpo, wco, wao, wo, norm_ffn, w_router, b_router, name):
    t = x.shape[0]
    tm = MERGE_TM
    assert t % tm == 0 and gates.shape == (t, N_BRANCH * D_MODEL)
    const = lambda i: (0, 0)
    wr_hi = w_router.astype(BF16)
    wr_lo = (w_router - wr_hi.astype(F32)).astype(BF16)
    return pl.pallas_call(
        _merge_route_kernel,
        grid=(t // tm,),
        in_specs=[
            pl.BlockSpec((tm, D_MODEL), lambda i: (i, 0)),
            pl.BlockSpec((tm, D_MODEL), lambda i: (i, 0)),
            pl.BlockSpec((tm, D_MODEL), lambda i: (i, 0)),
            pl.BlockSpec((tm, D_MODEL), lambda i: (i, 1)),
            pl.BlockSpec((tm, D_MODEL), lambda i: (i, 2)),
            pl.BlockSpec((D_POOL, D_MODEL), const, pipeline_mode=pl.Buffered(1)),
            pl.BlockSpec((D_CONV, D_MODEL), const, pipeline_mode=pl.Buffered(1)),
            pl.BlockSpec((D_XATTN, D_MODEL), const, pipeline_mode=pl.Buffered(1)),
            pl.BlockSpec((D_MODEL, D_MODEL), const, pipeline_mode=pl.Buffered(1)),
            pl.BlockSpec((1, D_MODEL), const),
            pl.BlockSpec((D_MODEL, 2 * N_EXPERTS), const),
            pl.BlockSpec((D_MODEL, N_EXPERTS), const),
            pl.BlockSpec((1, N_EXPERTS), const),
        ],
        out_specs=[
            pl.BlockSpec((tm, D_MODEL), lambda i: (i, 0)),
            pl.BlockSpec((tm, D_MODEL // 2), lambda i: (i, 0)),
            pl.BlockSpec((tm, LANES), lambda i: (i, 0)),
            pl.BlockSpec((tm, LANES), lambda i: (i, 0)),
            pl.BlockSpec((8, LANES), const),
        ],
        out_shape=[
            jax.ShapeDtypeStruct((t, D_MODEL), F32),
            jax.ShapeDtypeStruct((t, D_MODEL // 2), PACKED),
            jax.ShapeDtypeStruct((t, LANES), jnp.int32),
            jax.ShapeDtypeStruct((t, LANES), F32),
            jax.ShapeDtypeStruct((8, LANES), F32),
        ],
        scratch_shapes=[pltpu.VMEM((8, LANES), F32)],
        compiler_params=pltpu.CompilerParams(
            dimension_semantics=("arbitrary",), vmem_limit_bytes=VMEM_LIMIT),
        name=name,
    )(x, branch, gates, gates, gates, wpo, wco, wao, wo, norm_ffn.reshape(1, D_MODEL),
      jnp.concatenate([wr_hi, wr_lo], axis=1), wr_hi, b_router.reshape(1, N_EXPERTS))


def _fill_work_list(nblk_s, pad_s, nj, refs):
    te, tj, trow, tnb, tfirst, tne, tnj, tpar, ttot = refs

    def clear(i, c):
        for r in (te, tj, trow, tnb, tfirst, tpar):
            r[i] = 0
        tne[i] = -1
        tnj[i] = -1
        return c

    lax.fori_loop(0, te.shape[0], clear, 0)

    carry = (jnp.int32(0), jnp.int32(-1), jnp.int32(0))
    for j in range(nj):
        def per_expert(e, carry):
            q, prev, groups = carry
            n = nblk_s[e]
            nch = (n + EXP_CHUNK - 1) // EXP_CHUNK

            @pl.when((nch > 0) & (prev >= 0))
            def _():
                tne[prev] = e
                tnj[prev] = jnp.int32(j)

            def per_chunk(c, q):
                te[q] = e
                tj[q] = jnp.int32(j)
                trow[q] = pad_s[e] + c * (EXP_CHUNK * EXP_TM)
                tnb[q] = jnp.minimum(EXP_CHUNK, n - c * EXP_CHUNK)
                tfirst[q] = (c == 0).astype(jnp.int32)
                tpar[q] = lax.rem(groups, 2)
                return q + 1

            return (lax.fori_loop(0, nch, per_chunk, q), jnp.where(nch > 0, q, prev),
                    groups + (nch > 0).astype(jnp.int32))

        carry = lax.fori_loop(0, N_EXPERTS, per_expert, carry)
    ttot[0] = carry[0]


def _route_tables_kernel(cntp_ref, cnts_ref, rip_ref, ris_ref, dest_ref, *refs, nj_in, nj_out):
    tabs_in, tabs_out = refs[0:N_TABS], refs[N_TABS:2 * N_TABS]
    nblk_s, pad_s = refs[2 * N_TABS:]

    def per_expert(e, start_blk):
        n = (cntp_ref[e] + cnts_ref[e] + EXP_TM - 1) // EXP_TM
        nblk_s[e] = n
        pad_s[e] = start_blk * EXP_TM
        return start_blk + n

    lax.fori_loop(0, N_EXPERTS, per_expert, jnp.int32(0))
    _fill_work_list(nblk_s, pad_s, nj_in, tabs_in)
    _fill_work_list(nblk_s, pad_s, nj_out, tabs_out)

    tile = ROUTE_TM
    col0 = 0
    for ri_ref, is_sample in ((rip_ref, False), (ris_ref, True)):
        for r in range(ri_ref.shape[0] // tile):
            ri = ri_ref[r * tile:(r + 1) * tile, :]
            base = jnp.zeros_like(ri)
            for e in range(N_EXPERTS):
                first_row = pad_s[e] + cntp_ref[e] if is_sample else pad_s[e]
                base = jnp.where(ri == e, first_row, base)
            dest = base + pltpu.roll(ri, LANES - TOP_K, axis=1)
            dest_ref[:, col0:col0 + tile] = jnp.transpose(dest)[0:8, :]
            col0 += tile


def _route_tables(cnt_p, cnt_s, ri_p, ri_s, nj_in, nj_out, max_items):
    t_all = ri_p.shape[0] + ri_s.shape[0]
    assert ri_p.shape[0] % ROUTE_TM == 0 and ri_s.shape[0] % ROUTE_TM == 0
    smem = pl.BlockSpec(memory_space=pltpu.SMEM)

    def tables(nj):
        n = nj * max_items
        return [jax.ShapeDtypeStruct((n,), jnp.int32)] * (N_TABS - 1) + [jax.ShapeDtypeStruct((1,), jnp.int32)]

    out = pl.pallas_call(
        functools.partial(_route_tables_kernel, nj_in=nj_in, nj_out=nj_out),
        grid=(1,),
        in_specs=[smem, smem,
                  pl.BlockSpec(ri_p.shape, lambda i: (0, 0)), pl.BlockSpec(ri_s.shape, lambda i: (0, 0))],
        out_specs=[pl.BlockSpec((8, t_all), lambda i: (0, 0))] + [smem] * (2 * N_TABS),
        out_shape=[jax.ShapeDtypeStruct((8, t_all), jnp.int32)] + tables(nj_in) + tables(nj_out),
        scratch_shapes=[pltpu.SMEM((N_EXPERTS,), jnp.int32), pltpu.SMEM((N_EXPERTS,), jnp.int32)],
        compiler_params=pltpu.CompilerParams(
            dimension_semantics=("arbitrary",), vmem_limit_bytes=VMEM_LIMIT),
        name="route_tables",
    )(cnt_p, cnt_s, ri_p, ri_s)
    return out[0], tuple(out[1:1 + N_TABS]), tuple(out[1 + N_TABS:1 + 2 * N_TABS])


def _sc_mesh():
    return plsc.VectorSubcoreMesh(core_axis_name="c", subcore_axis_name="s",
                                  num_cores=SC_CORES, num_subcores=SC_SUBCORES)


def _sc_worker_id():
    return lax.axis_index("s") * SC_CORES + lax.axis_index("c")


def _sc_dispatch(h2_a, h2_b, dest_by_slot, rows_out):
    ta, w = h2_a.shape
    t = ta + h2_b.shape[0]
    per_w = t // SC_WORKERS
    n_chunks = per_w // SC_CHUNK
    assert t == SC_WORKERS * n_chunks * SC_CHUNK and ta % SC_CHUNK == 0
    idx = dest_by_slot[:TOP_K].reshape(TOP_K, SC_WORKERS, n_chunks, SC_CHUNK).transpose(1, 0, 2, 3)

    def body(a_hbm, b_hbm, idx_hbm, xs_hbm, idx_v, rows_v, sems):
        base = _sc_worker_id() * per_w
        pltpu.sync_copy(idx_hbm.at[_sc_worker_id()], idx_v)
        pending = [[], []]
        for c in range(n_chunks):
            b = c % 2
            for d in pending[b]:
                d.wait()
            tok0 = base + c * SC_CHUNK

            @pl.when(tok0 < ta)
            def _():
                pltpu.sync_copy(a_hbm.at[pl.ds(tok0, SC_CHUNK)], rows_v.at[b])

            @pl.when(tok0 >= ta)
            def _():
                pltpu.sync_copy(b_hbm.at[pl.ds(tok0 - ta, SC_CHUNK)], rows_v.at[b])

            pending[b] = [pltpu.async_copy(rows_v.at[b], xs_hbm.at[idx_v.at[k, c]], sems.at[b])
                          for k in range(TOP_K)]
        for b in range(2):
            for d in pending[b]:
                d.wait()

    return pl.kernel(
        body,
        out_type=jax.ShapeDtypeStruct((rows_out, w), h2_a.dtype),
        mesh=_sc_mesh(),
        scratch_types=[pltpu.VMEM((TOP_K, n_chunks, SC_CHUNK), jnp.int32),
                       pltpu.VMEM((2, SC_CHUNK, w), h2_a.dtype),
                       pltpu.SemaphoreType.DMA((2,))],
        name="sc_dispatch",
    )(h2_a, h2_b, idx)


def _sc_gather_rows(table, idx):
    n = idx.shape[0]
    w = table.shape[1]
    per_w = n // SC_WORKERS
    n_chunks = per_w // SC_CHUNK
    assert n == SC_WORKERS * n_chunks * SC_CHUNK
    idx3 = idx.reshape(SC_WORKERS, n_chunks, SC_CHUNK)

    def body(table_hbm, idx_hbm, out_hbm, idx_v, rows_v, gsem, wsems):
        base = _sc_worker_id() * per_w
        pltpu.sync_copy(idx_hbm.at[_sc_worker_id()], idx_v)
        pending = [None, None]
        for c in range(n_chunks):
            b = c % 2
            if pending[b] is not None:
                pending[b].wait()
            pltpu.async_copy(table_hbm.at[idx_v.at[c]], rows_v.at[b], gsem).wait()
            pending[b] = pltpu.async_copy(
                rows_v.at[b], out_hbm.at[pl.ds(base + c * SC_CHUNK, SC_CHUNK)], wsems.at[b])
        for b in range(2):
            if pending[b] is not None:
                pending[b].wait()

    return pl.kernel(
        body,
        out_type=jax.ShapeDtypeStruct((n, w), table.dtype),
        mesh=_sc_mesh(),
        scratch_types=[pltpu.VMEM((n_chunks, SC_CHUNK), jnp.int32),
                       pltpu.VMEM((2, SC_CHUNK, w), table.dtype),
                       pltpu.SemaphoreType.DMA,
                       pltpu.SemaphoreType.DMA((2,))],
        name="sc_gather_rows",
    )(table, idx3)


def _grouped_pipeline(tabs, total_ref, weight_copies, in_copy, out_copy, compute):
    te, tj, _, tnb, tfirst, tne, tnj, tpar = tabs
    total = total_ref[0]

    def wait_out(q):
        for m in range(1, EXP_CHUNK + 1):
            @pl.when(tnb[q] == m)
            def _():
                out_copy(q, lax.rem(q, 2), m).wait()

    for d in weight_copies(te[0], tj[0], tpar[0]):
        d.start()
    in_copy(0, 0).start()

    def body(q, carry):
        slot = lax.rem(q, 2)

        @pl.when(q + 1 < total)
        def _():
            in_copy(q + 1, 1 - slot).start()

        in_copy(q, slot).wait()

        @pl.when(tfirst[q] == 1)
        def _():
            for d in weight_copies(te[q], tj[q], tpar[q]):
                d.wait()

            @pl.when(tne[q] >= 0)
            def _():
                for d in weight_copies(tne[q], tnj[q], 1 - tpar[q]):
                    d.start()

        @pl.when(q >= 2)
        def _():
            wait_out(q - 2)

        for m in range(1, EXP_CHUNK + 1):
            @pl.when(tnb[q] == m)
            def _():
                compute(q, slot, m, tpar[q])
                out_copy(q, slot, m).start()

        return carry

    lax.fori_loop(0, total, body, 0)

    @pl.when(total >= 2)
    def _():
        wait_out(total - 2)

    wait_out(total - 1)


def _expert_in_kernel(*refs):
    tabs, total_ref = refs[:N_TABS - 1], refs[N_TABS - 1]
    xs_hbm, w_hbm, b_ref, act_hbm, wbuf, x_buf, a_buf, w_sem, x_sem, a_sem = refs[N_TABS:]
    te, tj, trow = tabs[:3]
    tm, tn = EXP_TM, a_buf.shape[2]
    nj = D_FF // tn

    def weight_copies(e, j, par):
        return [pltpu.make_async_copy(
            w_hbm.at[e, :, pl.ds(pl.multiple_of(half * D_FF + j * tn, tn), tn)],
            wbuf.at[2 * par + half], w_sem)
            for half in range(2)]

    def rows(q, m):
        return pl.ds(pl.multiple_of(trow[q], tm), m * tm)

    def x_copy(q, slot):
        return pltpu.make_async_copy(xs_hbm.at[rows(q, EXP_CHUNK)], x_buf.at[slot], x_sem.at[slot])

    def a_copy(q, slot, m):
        return pltpu.make_async_copy(
            a_buf.at[slot, pl.ds(0, m * tm)],
            act_hbm.at[rows(q, m), pl.ds(pl.multiple_of(tj[q] * tn, tn), tn)], a_sem.at[slot])

    def compute(q, slot, m, par):
        bias0 = te[q] * (2 * nj) + tj[q]
        x = jnp.concatenate(_unpack_bf16_pairs(x_buf[slot, 0:m * tm]), axis=1).astype(BF16)
        g = jnp.dot(x, wbuf[2 * par].astype(BF16), preferred_element_type=F32) + b_ref[bias0]
        up = jnp.dot(x, wbuf[2 * par + 1].astype(BF16), preferred_element_type=F32) + b_ref[bias0 + nj]
        g = jnp.minimum(g, SWIGLU_LIMIT)
        up = jnp.clip(up, -SWIGLU_LIMIT, SWIGLU_LIMIT)
        a_buf[slot, 0:m * tm] = (g * _sigmoid(SWIGLU_ALPHA * g) * (up + 1.0)).astype(BF16)

    _grouped_pipeline(tabs, total_ref, weight_copies, x_copy, a_copy, compute)


def _expert_in(tabs, xs, w_in, b_in):
    rows = xs.shape[0]
    tm, tn = EXP_TM, EXP_IN_TN
    nj = D_FF // tn
    bias = b_in.reshape(N_EXPERTS * 2 * nj, 1, tn)
    grid_spec = pltpu.PrefetchScalarGridSpec(
        num_scalar_prefetch=len(tabs),
        grid=(1,),
        in_specs=[
            pl.BlockSpec(memory_space=pl.ANY),
            pl.BlockSpec(memory_space=pl.ANY),
            pl.BlockSpec(bias.shape, lambda i, *_: (0, 0, 0), pipeline_mode=pl.Buffered(1)),
        ],
        out_specs=pl.BlockSpec(memory_space=pl.ANY),
        scratch_shapes=[pltpu.VMEM((4, D_MODEL, tn), F32),
                        pltpu.VMEM((2, EXP_CHUNK * tm, D_MODEL // 2), PACKED),
                        pltpu.VMEM((2, EXP_CHUNK * tm, tn), BF16),
                        pltpu.SemaphoreType.DMA, pltpu.SemaphoreType.DMA((2,)),
                        pltpu.SemaphoreType.DMA((2,))],
    )
    return pl.pallas_call(
        _expert_in_kernel,
        grid_spec=grid_spec,
        out_shape=jax.ShapeDtypeStruct((rows, D_FF), BF16),
        compiler_params=pltpu.CompilerParams(
            dimension_semantics=("arbitrary",), vmem_limit_bytes=VMEM_LIMIT),
        name="expert_in",
    )(*tabs, xs, w_in, bias)


def _expert_out_kernel(*refs):
    tabs, total_ref = refs[:N_TABS - 1], refs[N_TABS - 1]
    act_hbm, w_hbm, b_ref, ys_hbm, wbuf, a_buf, y_buf, w_sem, a_sem, y_sem = refs[N_TABS:]
    te, tj, trow = tabs[:3]
    tm, tnp = EXP_TM, y_buf.shape[2]
    tn = 2 * tnp
    nj = D_MODEL // tn

    def weight_copies(e, j, par):
        return [pltpu.make_async_copy(
            w_hbm.at[e, :, pl.ds(pl.multiple_of(j * tn, tn), tn)], wbuf.at[par], w_sem)]

    def rows(q, m):
        return pl.ds(pl.multiple_of(trow[q], tm), m * tm)

    def a_copy(q, slot):
        return pltpu.make_async_copy(act_hbm.at[rows(q, EXP_CHUNK)], a_buf.at[slot], a_sem.at[slot])

    def y_copy(q, slot, m):
        return pltpu.make_async_copy(
            y_buf.at[slot, pl.ds(0, m * tm)],
            ys_hbm.at[rows(q, m), pl.ds(pl.multiple_of(tj[q] * tnp, tnp), tnp)], y_sem.at[slot])

    def compute(q, slot, m, par):
        y = jnp.dot(a_buf[slot, 0:m * tm], wbuf[par].astype(BF16),
                    preferred_element_type=F32) + b_ref[te[q] * nj + tj[q]]
        y_buf[slot, 0:m * tm] = _pack_bf16_pairs(y)

    _grouped_pipeline(tabs, total_ref, weight_copies, a_copy, y_copy, compute)


def _expert_out(tabs, act, w_out, b_out):
    rows = act.shape[0]
    tm, tn = EXP_TM, EXP_OUT_TN
    nj = D_MODEL // tn
    bias = b_out.reshape(N_EXPERTS * nj, 1, tn)
    grid_spec = pltpu.PrefetchScalarGridSpec(
        num_scalar_prefetch=len(tabs),
        grid=(1,),
        in_specs=[
            pl.BlockSpec(memory_space=pl.ANY),
            pl.BlockSpec(memory_space=pl.ANY),
            pl.BlockSpec(bias.shape, lambda i, *_: (0, 0, 0), pipeline_mode=pl.Buffered(1)),
        ],
        out_specs=pl.BlockSpec(memory_space=pl.ANY),
        scratch_shapes=[pltpu.VMEM((2, D_FF, tn), F32),
                        pltpu.VMEM((2, EXP_CHUNK * tm, D_FF), BF16),
                        pltpu.VMEM((2, EXP_CHUNK * tm, tn // 2), PACKED),
                        pltpu.SemaphoreType.DMA, pltpu.SemaphoreType.DMA((2,)),
                        pltpu.SemaphoreType.DMA((2,))],
    )
    return pl.pallas_call(
        _expert_out_kernel,
        grid_spec=grid_spec,
        out_shape=jax.ShapeDtypeStruct((rows, D_MODEL // 2), PACKED),
        compiler_params=pltpu.CompilerParams(
            dimension_semantics=("arbitrary",), vmem_limit_bytes=VMEM_LIMIT),
        name="expert_out",
    )(*tabs, act, w_out, bias)


def _combine_kernel(x1_ref, yg_ref, rw_ref, g_ref, o_ref):
    half = EXP_OUT_TN // 2
    pieces = []
    ssq = jnp.zeros((x1_ref.shape[0], 1), F32)
    for jt in range(D_MODEL // EXP_OUT_TN):
        acc_lo = x1_ref[:, jt * EXP_OUT_TN:jt * EXP_OUT_TN + half]
        acc_hi = x1_ref[:, jt * EXP_OUT_TN + half:(jt + 1) * EXP_OUT_TN]
        moe_lo = jnp.zeros_like(acc_lo)
        moe_hi = jnp.zeros_like(acc_hi)
        for k in range(TOP_K):
            lo, hi = _unpack_bf16_pairs(yg_ref[k, :, jt * half:(jt + 1) * half])
            moe_lo = moe_lo + lo * rw_ref[:, k:k + 1]
            moe_hi = moe_hi + hi * rw_ref[:, k:k + 1]
        for acc in (acc_lo + moe_lo, acc_hi + moe_hi):
            ssq = ssq + jnp.sum(acc * acc, axis=-1, keepdims=True)
            pieces.append(acc)
    inv = lax.rsqrt(ssq / D_MODEL + EPS)
    for n, acc in enumerate(pieces):
        o_ref[:, n * half:(n + 1) * half] = acc * inv * g_ref[:, n * half:(n + 1) * half]


def _combine_part_kernel(x1_ref, yg_ref, rw_ref, g_ref, prev_hbm, o_ref):
    del prev_hbm
    _combine_kernel(x1_ref, yg_ref, rw_ref, g_ref, o_ref)


def _combine(x1, yg, rw, gain, row0, prev, name):
    t = x1.shape[0]
    n = yg.shape[1]
    tm = COMB_TM
    assert n % tm == 0 and row0 % tm == 0
    off = row0 // tm
    in_specs = [
        pl.BlockSpec((tm, D_MODEL), lambda i: (i + off, 0)),
        pl.BlockSpec((TOP_K, tm, D_MODEL // 2), lambda i: (0, i, 0)),
        pl.BlockSpec((tm, LANES), lambda i: (i + off, 0)),
        pl.BlockSpec((1, D_MODEL), lambda i: (0, 0)),
    ]
    args = [x1, yg, rw, gain.reshape(1, D_MODEL)]
    if prev is not None:
        in_specs.append(pl.BlockSpec(memory_space=pl.ANY))
        args.append(prev)
    return pl.pallas_call(
        _combine_kernel if prev is None else _combine_part_kernel,
        grid=(n // tm,),
        in_specs=in_specs,
        out_specs=pl.BlockSpec((tm, D_MODEL), lambda i: (i + off, 0)),
        out_shape=jax.ShapeDtypeStruct((t, D_MODEL), F32),
        input_output_aliases={} if prev is None else {len(args) - 1: 0},
        compiler_params=pltpu.CompilerParams(
            dimension_semantics=("arbitrary",), vmem_limit_bytes=VMEM_LIMIT),
        name=name,
    )(*args)


def kernel(x_prompt, x_sample, state_pool, state_conv, cache_mem_k, cache_mem_v, mem_prompt,
           norm_mix, w_in, b_gate, w_pool_group, pool_scale, w_conv, mem_norm, w_mem_kv,
           w_pool_out, w_conv_out, w_attn_out, w_o, norm_ffn, w_router, b_router,
           w_exp_in, b_exp_in, w_exp_out, b_exp_out, final_norm):
    depth = norm_mix.shape[0]
    assert depth == 1
    l = 0
    bp, seq, _ = x_prompt.shape
    bs, ln, _ = x_sample.shape
    tp, ts = bp * seq, bs * ln

    kv = _norm_matmul(mem_prompt.reshape(bp * N_MEM, D_MODEL), mem_norm[l], w_mem_kv[l],
                      jnp.zeros((2 * D_XATTN,), F32), "mem_kv")
    mk = kv[:, :D_XATTN].reshape(bp, N_MEM, D_XATTN)
    mv = kv[:, D_XATTN:].reshape(bp, N_MEM, D_XATTN)

    bias_in = jnp.concatenate([jnp.zeros((D_MIX,), F32), b_gate[l]])
    xp = x_prompt.reshape(tp, D_MODEL)
    xs_ = x_sample.reshape(ts, D_MODEL)
    proj_p, gates_p = _norm_matmul_split(xp, norm_mix[l], w_in[l], bias_in, D_MIX, "proj_prompt")
    proj_s, gates_s = _norm_matmul_split(xs_, norm_mix[l], w_in[l], bias_in, D_MIX, "proj_sample")

    br_p, zst_p = _mix_prompt(proj_p, mk, mv, w_pool_group[l], pool_scale[l], w_conv[l], bp, seq)
    br_s, zst_s = _mix_sample(proj_s, state_pool[l], state_conv[l],
                              cache_mem_k[l].reshape(bs, N_MEM * N_XHEADS, XHEAD_DIM),
                              cache_mem_v[l].reshape(bs, N_MEM * N_XHEADS, XHEAD_DIM),
                              w_pool_group[l], pool_scale[l], w_conv[l], bs, ln)

    wpo, wco, wao, wo = (w.astype(BF16) for w in (w_pool_out[l], w_conv_out[l], w_attn_out[l], w_o[l]))
    x1_p, h2_p, ri_p, rw_p, cnt_p = _merge_route(xp, br_p, gates_p, wpo, wco, wao, wo, norm_ffn[l],
                                                  w_router[l], b_router[l], "merge_route_prompt")
    x1_s, h2_s, ri_s, rw_s, cnt_s = _merge_route(xs_, br_s, gates_s, wpo, wco, wao, wo, norm_ffn[l],
                                                  w_router[l], b_router[l], "merge_route_sample")

    t_all = tp + ts
    n_assign = t_all * TOP_K
    nb_max = n_assign // EXP_TM + N_EXPERTS
    max_items = nb_max // EXP_CHUNK + N_EXPERTS
    dest, tabs_in, tabs_out = _route_tables(
        cnt_p[0, :N_EXPERTS].astype(jnp.int32), cnt_s[0, :N_EXPERTS].astype(jnp.int32), ri_p, ri_s,
        D_FF // EXP_IN_TN, D_MODEL // EXP_OUT_TN, max_items)
    xs_sorted = _sc_dispatch(h2_p, h2_s, dest, (nb_max + EXP_CHUNK) * EXP_TM)

    act = _expert_in(tabs_in, xs_sorted, w_exp_in[l], b_exp_in[l])
    ys = _expert_out(tabs_out, act, w_exp_out[l], b_exp_out[l])
    def combine_part(x1, rw, tok0, n, row0, prev, name):
        idx = dest[:TOP_K, tok0:tok0 + n].reshape(-1)
        yg = _sc_gather_rows(ys, idx).reshape(TOP_K, n, D_MODEL // 2)
        return _combine(x1, yg, rw, final_norm, row0, prev, name)

    y_s = combine_part(x1_s, rw_s, tp, ts, 0, None, "combine_sample")
    y_p = None
    tok0 = 0
    for c, units in enumerate(COMB_PARTS):
        n = tp * units // sum(COMB_PARTS)
        y_p = combine_part(x1_p, rw_p, tok0, n, tok0, y_p, f"combine_prompt_{c}")
        tok0 += n
    assert tok0 == tp

    new_pool_p = proj_p.reshape(bp, seq, D_MIX)[:, seq - POOL_STATE_LEN:, :D_POOL]
    new_conv_p = zst_p[:, 8 - (CONV_WIDTH - 1):]
    u_s = proj_s[:, :D_POOL].reshape(bs, ln, D_POOL)
    new_pool_s = jnp.concatenate([state_pool[l], u_s], axis=1)[:, -POOL_STATE_LEN:]
    new_conv_s = zst_s[:, ln - (CONV_WIDTH - 1):]

    return (y_p.reshape(bp, seq, D_MODEL), y_s.reshape(bs, ln, D_MODEL),
            new_pool_p[None], new_conv_p[None],
            mk.reshape(1, bp, N_MEM, N_XHEADS, XHEAD_DIM), mv.reshape(1, bp, N_MEM, N_XHEADS, XHEAD_DIM),
            new_pool_s[None], new_conv_s[None])
```

```python
import functools

import jax
import jax.numpy as jnp
from jax import lax
from jax.experimental import pallas as pl
from jax.experimental.pallas import tpu as pltpu
from jax.experimental.pallas import tpu_sc as plsc

F32 = jnp.float32
BF16 = jnp.bfloat16
PACKED = jnp.int32

D_MODEL = 2048
POOL_WINDOWS = (2, 4, 8, 16)
POOL_GROUP_DIM = 128
D_POOL = 512
POOL_STATE_LEN = 15
D_CONV = 1024
CONV_WIDTH = 3
N_MEM = 256
N_XHEADS = 4
XHEAD_DIM = 128
D_XATTN = 512
N_BRANCH = 3
D_MIX = D_POOL + 3 * D_CONV + D_XATTN
D_IN_TOTAL = D_MIX + N_BRANCH * D_MODEL
N_EXPERTS = 32
TOP_K = 4
D_FF = D_MODEL
SWIGLU_LIMIT = 7.0
SWIGLU_ALPHA = 1.702
EPS = 1e-5

C_U = 0
C_V = D_POOL
C_B = D_POOL + D_CONV
C_C = D_POOL + 2 * D_CONV
C_Q = D_POOL + 3 * D_CONV

LANES = 128
HIST = 16

PROJ_TM = 1024
PROJ_TN = 1024
MIX_TM = 512
MIX_NS = 8
MERGE_TM = 256
ROUTE_TM = 1024
N_TABS = 9
EXP_TM = 128
EXP_CHUNK = 6
EXP_IN_TN = 1024
EXP_OUT_TN = 2048
COMB_TM = 512
COMB_PARTS = (2, 3, 3)
SC_CORES = 2
SC_SUBCORES = 16
SC_WORKERS = SC_CORES * SC_SUBCORES
SC_CHUNK = 32
VMEM_LIMIT = 56 * 1024 * 1024


def _sigmoid(x):
    return 0.5 * (jnp.tanh(0.5 * x) + 1.0)


def _rms(x, g):
    ms = jnp.mean(x * x, axis=-1, keepdims=True)
    return x * lax.rsqrt(ms + EPS) * g


def _pack_bf16_pairs(x):
    n = x.shape[1] // 2
    lo = lax.bitcast_convert_type(x[:, :n].astype(BF16).astype(F32), jnp.uint32)
    hi = lax.bitcast_convert_type(x[:, n:].astype(BF16).astype(F32), jnp.uint32)
    return lax.bitcast_convert_type((hi & jnp.uint32(0xFFFF0000)) | (lo >> 16), PACKED)


def _unpack_bf16_pairs(p):
    p = lax.bitcast_convert_type(p, jnp.uint32)
    lo = lax.bitcast_convert_type(p << 16, F32)
    hi = lax.bitcast_convert_type(p & jnp.uint32(0xFFFF0000), F32)
    return lo, hi


def _norm_matmul_kernel(x_ref, g_ref, w_ref, b_ref, o_ref, h_ref):
    @pl.when(pl.program_id(1) == 0)
    def _():
        h_ref[...] = _rms(x_ref[...], g_ref[...]).astype(BF16)

    o_ref[...] = jnp.dot(h_ref[...], w_ref[...].astype(BF16), preferred_element_type=F32) + b_ref[...]


def _norm_matmul(x, gain, w, bias, name):
    t, d = x.shape
    n = w.shape[1]
    tm = min(PROJ_TM, t)
    tn = PROJ_TN
    assert t % tm == 0 and n % tn == 0
    return pl.pallas_call(
        _norm_matmul_kernel,
        grid=(t // tm, n // tn),
        in_specs=[
            pl.BlockSpec((tm, d), lambda i, j: (i, 0)),
            pl.BlockSpec((1, d), lambda i, j: (0, 0)),
            pl.BlockSpec((d, tn), lambda i, j: (0, j)),
            pl.BlockSpec((1, tn), lambda i, j: (0, j)),
        ],
        out_specs=pl.BlockSpec((tm, tn), lambda i, j: (i, j)),
        out_shape=jax.ShapeDtypeStruct((t, n), F32),
        scratch_shapes=[pltpu.VMEM((tm, d), BF16)],
        compiler_params=pltpu.CompilerParams(
            dimension_semantics=("arbitrary", "arbitrary"), vmem_limit_bytes=VMEM_LIMIT),
        name=name,
    )(x, gain.reshape(1, d), w, bias.reshape(1, n))


def _norm_matmul_split_kernel(x_ref, g_ref, w_ref, b_ref, o_ref, og_ref, h_ref, *, n_main):
    j = pl.program_id(1)

    @pl.when(j == 0)
    def _():
        h_ref[...] = _rms(x_ref[...], g_ref[...]).astype(BF16)

    acc = jnp.dot(h_ref[...], w_ref[...].astype(BF16), preferred_element_type=F32) + b_ref[...]

    @pl.when(j < n_main)
    def _():
        o_ref[...] = acc

    @pl.when(j >= n_main)
    def _():
        og_ref[...] = acc.astype(BF16)


def _norm_matmul_split(x, gain, w, bias, split_col, name):
    t, d = x.shape
    n = w.shape[1]
    tm = min(PROJ_TM, t)
    tn = PROJ_TN
    assert t % tm == 0 and n % tn == 0 and split_col % tn == 0 and 0 < split_col < n
    n_main = split_col // tn
    return pl.pallas_call(
        functools.partial(_norm_matmul_split_kernel, n_main=n_main),
        grid=(t // tm, n // tn),
        in_specs=[
            pl.BlockSpec((tm, d), lambda i, j: (i, 0)),
            pl.BlockSpec((1, d), lambda i, j: (0, 0)),
            pl.BlockSpec((d, tn), lambda i, j: (0, j)),
            pl.BlockSpec((1, tn), lambda i, j: (0, j)),
        ],
        out_specs=[
            pl.BlockSpec((tm, tn), lambda i, j: (i, jnp.minimum(j, n_main - 1))),
            pl.BlockSpec((tm, tn), lambda i, j: (i, jnp.maximum(j - n_main, 0))),
        ],
        out_shape=[jax.ShapeDtypeStruct((t, split_col), F32),
                   jax.ShapeDtypeStruct((t, n - split_col), BF16)],
        scratch_shapes=[pltpu.VMEM((tm, d), BF16)],
        compiler_params=pltpu.CompilerParams(
            dimension_semantics=("arbitrary", "arbitrary"), vmem_limit_bytes=VMEM_LIMIT),
        name=name,
    )(x, gain.reshape(1, d), w, bias.reshape(1, n))


def _pool_project(pooled, wpg_ref, scale_ref, g):
    sl = slice(g * POOL_GROUP_DIM, (g + 1) * POOL_GROUP_DIM)
    y = jnp.dot(pooled.astype(BF16), wpg_ref[g].astype(BF16), preferred_element_type=F32)
    return y * scale_ref[:, sl]


def _softmax_rows(s):
    m = jnp.max(s, axis=-1, keepdims=True)
    e = jnp.exp(s - m)
    return e / jnp.sum(e, axis=-1, keepdims=True)


def _mix_prompt_kernel(cur_ref, prev_ref, mk_ref, mv_ref, wpg_ref, scale_ref, wconv_ref,
                       br_ref, zst_ref, extu_ref, extz_ref):
    t = pl.program_id(1)
    tm = cur_ref.shape[0]
    has_prev = t > 0

    u = cur_ref[:, C_U:C_U + D_POOL]
    extu_ref[0:HIST, :] = jnp.where(has_prev, prev_ref[:, C_U:C_U + D_POOL], 0.0)
    extu_ref[HIST:HIST + tm, :] = u
    pos = t * tm + lax.broadcasted_iota(jnp.int32, (tm, 1), 0)
    for g, w in enumerate(POOL_WINDOWS):
        sl = slice(g * POOL_GROUP_DIM, (g + 1) * POOL_GROUP_DIM)
        s = extu_ref[HIST:HIST + tm, sl]
        for k in range(1, w):
            s = s + extu_ref[HIST - k:HIST - k + tm, sl]
        cnt = jnp.minimum(w, pos + 1).astype(F32)
        pooled = s / cnt - extu_ref[HIST:HIST + tm, sl]
        br_ref[:, sl] = _pool_project(pooled, wpg_ref, scale_ref, g).astype(BF16)

    z = cur_ref[:, C_C:C_C + D_CONV] * cur_ref[:, C_V:C_V + D_CONV]
    zprev = prev_ref[:, C_C:C_C + D_CONV] * prev_ref[:, C_V:C_V + D_CONV]
    extz_ref[0:HIST, :] = jnp.where(has_prev, zprev, 0.0)
    extz_ref[HIST:HIST + tm, :] = z
    y = extz_ref[HIST - 2:HIST - 2 + tm, :] * wconv_ref[0:1, :]
    y = y + extz_ref[HIST - 1:HIST - 1 + tm, :] * wconv_ref[1:2, :]
    y = y + extz_ref[HIST:HIST + tm, :] * wconv_ref[2:3, :]
    br_ref[:, D_POOL:D_POOL + D_CONV] = (cur_ref[:, C_B:C_B + D_CONV] * y).astype(BF16)
    zst_ref[0] = extz_ref[HIST + tm - 8:HIST + tm, :]

    for h in range(N_XHEADS):
        sl = slice(h * XHEAD_DIM, (h + 1) * XHEAD_DIM)
        qh = cur_ref[:, C_Q + h * XHEAD_DIM:C_Q + (h + 1) * XHEAD_DIM].astype(BF16)
        kh = mk_ref[0, :, sl].astype(BF16)
        vh = mv_ref[0, :, sl].astype(BF16)
        s = lax.dot_general(qh, kh, (((1,), (1,)), ((), ())), preferred_element_type=F32)
        p = _softmax_rows(s * (XHEAD_DIM ** -0.5))
        o = jnp.dot(p.astype(BF16), vh, preferred_element_type=F32)
        c0 = D_POOL + D_CONV + h * XHEAD_DIM
        br_ref[:, c0:c0 + XHEAD_DIM] = o.astype(BF16)


def _mix_prompt(proj, mk, mv, wpg, scale, wconv, batch, seq):
    tm = MIX_TM
    nt = seq // tm
    assert seq % tm == 0 and tm % HIST == 0
    rpb = tm // HIST
    return pl.pallas_call(
        _mix_prompt_kernel,
        grid=(batch, nt),
        in_specs=[
            pl.BlockSpec((tm, D_MIX), lambda b, t: (b * nt + t, 0)),
            pl.BlockSpec((HIST, D_MIX), lambda b, t: (jnp.maximum((b * nt + t) * rpb - 1, 0), 0)),
            pl.BlockSpec((1, N_MEM, D_XATTN), lambda b, t: (b, 0, 0)),
            pl.BlockSpec((1, N_MEM, D_XATTN), lambda b, t: (b, 0, 0)),
            pl.BlockSpec((len(POOL_WINDOWS), POOL_GROUP_DIM, POOL_GROUP_DIM), lambda b, t: (0, 0, 0)),
            pl.BlockSpec((1, D_POOL), lambda b, t: (0, 0)),
            pl.BlockSpec((CONV_WIDTH, D_CONV), lambda b, t: (0, 0)),
        ],
        out_specs=[
            pl.BlockSpec((tm, D_MODEL), lambda b, t: (b * nt + t, 0)),
            pl.BlockSpec((1, 8, D_CONV), lambda b, t: (b, 0, 0)),
        ],
        out_shape=[
            jax.ShapeDtypeStruct((batch * seq, D_MODEL), BF16),
            jax.ShapeDtypeStruct((batch, 8, D_CONV), F32),
        ],
        scratch_shapes=[pltpu.VMEM((HIST + tm, D_POOL), F32), pltpu.VMEM((HIST + tm, D_CONV), F32)],
        compiler_params=pltpu.CompilerParams(
            dimension_semantics=("arbitrary", "arbitrary"), vmem_limit_bytes=VMEM_LIMIT),
        name="mix_prompt",
    )(proj, proj, mk, mv, wpg, scale.reshape(1, D_POOL), wconv)


def _mix_sample_kernel(cur_ref, sp_ref, sc_ref, k_ref, v_ref, wpg_ref, scale_ref, wconv_ref,
                       br_ref, zst_ref, extu_ref, extz_ref):
    ns, ln = zst_ref.shape[0], zst_ref.shape[1]
    rows = ns * ln

    extu_ref[:, HIST - POOL_STATE_LEN:HIST, :] = sp_ref[...]
    extu_ref[:, HIST:HIST + ln, :] = cur_ref[:, C_U:C_U + D_POOL].reshape(ns, ln, D_POOL)
    for g, w in enumerate(POOL_WINDOWS):
        sl = slice(g * POOL_GROUP_DIM, (g + 1) * POOL_GROUP_DIM)
        s = extu_ref[:, HIST:HIST + ln, sl]
        for k in range(1, w):
            s = s + extu_ref[:, HIST - k:HIST - k + ln, sl]
        pooled = s / float(w) - extu_ref[:, HIST:HIST + ln, sl]
        pooled = pooled.reshape(rows, POOL_GROUP_DIM)
        br_ref[:, sl] = _pool_project(pooled, wpg_ref, scale_ref, g).astype(BF16)

    z = cur_ref[:, C_C:C_C + D_CONV] * cur_ref[:, C_V:C_V + D_CONV]
    extz_ref[:, HIST - 2:HIST, :] = sc_ref[...]
    extz_ref[:, HIST:HIST + ln, :] = z.reshape(ns, ln, D_CONV)
    y = extz_ref[:, HIST - 2:HIST - 2 + ln, :] * wconv_ref[0:1, :]
    y = y + extz_ref[:, HIST - 1:HIST - 1 + ln, :] * wconv_ref[1:2, :]
    y = y + extz_ref[:, HIST:HIST + ln, :] * wconv_ref[2:3, :]
    br_ref[:, D_POOL:D_POOL + D_CONV] = (
        cur_ref[:, C_B:C_B + D_CONV] * y.reshape(rows, D_CONV)).astype(BF16)
    zst_ref[...] = extz_ref[:, HIST:HIST + ln, :]

    q3 = cur_ref[:, C_Q:C_Q + D_XATTN].reshape(ns, ln, D_XATTN)
    q4 = jnp.concatenate([q3[:, :, h * XHEAD_DIM:(h + 1) * XHEAD_DIM] for h in range(N_XHEADS)], axis=1)
    s = jnp.einsum("nqd,nkd->nqk", q4.astype(BF16), k_ref[...].astype(BF16),
                   preferred_element_type=F32)
    row_head = lax.broadcasted_iota(jnp.int32, (N_XHEADS * ln, N_XHEADS * N_MEM), 0) // ln
    col_head = lax.broadcasted_iota(jnp.int32, (N_XHEADS * ln, N_XHEADS * N_MEM), 1) % N_XHEADS
    s = jnp.where((row_head == col_head)[None], s * (XHEAD_DIM ** -0.5), -jnp.inf)
    p = _softmax_rows(s)
    r = jnp.einsum("nqk,nkd->nqd", p.astype(BF16), v_ref[...].astype(BF16),
                   preferred_element_type=F32)
    for h in range(N_XHEADS):
        c0 = D_POOL + D_CONV + h * XHEAD_DIM
        br_ref[:, c0:c0 + XHEAD_DIM] = r[:, h * ln:(h + 1) * ln, :].reshape(rows, XHEAD_DIM).astype(BF16)


def _mix_sample(proj, state_pool, state_conv, mem_k, mem_v, wpg, scale, wconv, nseq, ln):
    ns = MIX_NS
    rows = ns * ln
    assert nseq % ns == 0 and ln == 8
    return pl.pallas_call(
        _mix_sample_kernel,
        grid=(nseq // ns,),
        in_specs=[
            pl.BlockSpec((rows, D_MIX), lambda s: (s, 0)),
            pl.BlockSpec((ns, POOL_STATE_LEN, D_POOL), lambda s: (s, 0, 0)),
            pl.BlockSpec((ns, CONV_WIDTH - 1, D_CONV), lambda s: (s, 0, 0)),
            pl.BlockSpec((ns, N_MEM * N_XHEADS, XHEAD_DIM), lambda s: (s, 0, 0)),
            pl.BlockSpec((ns, N_MEM * N_XHEADS, XHEAD_DIM), lambda s: (s, 0, 0)),
            pl.BlockSpec((len(POOL_WINDOWS), POOL_GROUP_DIM, POOL_GROUP_DIM), lambda s: (0, 0, 0)),
            pl.BlockSpec((1, D_POOL), lambda s: (0, 0)),
            pl.BlockSpec((CONV_WIDTH, D_CONV), lambda s: (0, 0)),
        ],
        out_specs=[
            pl.BlockSpec((rows, D_MODEL), lambda s: (s, 0)),
            pl.BlockSpec((ns, ln, D_CONV), lambda s: (s, 0, 0)),
        ],
        out_shape=[
            jax.ShapeDtypeStruct((nseq * ln, D_MODEL), BF16),
            jax.ShapeDtypeStruct((nseq, ln, D_CONV), F32),
        ],
        scratch_shapes=[pltpu.VMEM((ns, HIST + ln, D_POOL), F32),
                        pltpu.VMEM((ns, HIST + ln, D_CONV), F32)],
        compiler_params=pltpu.CompilerParams(
            dimension_semantics=("arbitrary",), vmem_limit_bytes=VMEM_LIMIT),
        name="mix_sample",
    )(proj, state_pool, state_conv, mem_k, mem_v, wpg, scale.reshape(1, D_POOL), wconv)


def _merge_route_kernel(x_ref, br_ref, g0_ref, g1_ref, g2_ref, wpo_ref, wco_ref, wao_ref, wo_ref,
                        nf_ref, wr_ref, brt_ref,
                        x1_ref, h2_ref, ri_ref, rw_ref, cnt_ref, carry_ref):
    i = pl.program_id(0)
    tm = x_ref.shape[0]

    @pl.when(i == 0)
    def _():
        carry_ref[...] = jnp.zeros_like(carry_ref)

    merged = _sigmoid(g0_ref[...].astype(F32)) * jnp.dot(
        br_ref[:, 0:D_POOL], wpo_ref[...], preferred_element_type=F32)
    merged = merged + _sigmoid(g1_ref[...].astype(F32)) * jnp.dot(
        br_ref[:, D_POOL:D_POOL + D_CONV], wco_ref[...], preferred_element_type=F32)
    merged = merged + _sigmoid(g2_ref[...].astype(F32)) * jnp.dot(
        br_ref[:, D_POOL + D_CONV:D_MODEL], wao_ref[...], preferred_element_type=F32)
    x1 = x_ref[...] + jnp.dot(merged.astype(BF16), wo_ref[...], preferred_element_type=F32)
    x1_ref[...] = x1
    h2 = _rms(x1, nf_ref[...])
    h2_ref[...] = _pack_bf16_pairs(h2)

    h2_hi = h2.astype(BF16)
    h2_lo = (h2 - h2_hi.astype(F32)).astype(BF16)
    parts = jnp.transpose(jnp.dot(jnp.concatenate([h2_hi, h2_lo], axis=1), wr_ref[...],
                                  preferred_element_type=F32))
    logits = parts[0:N_EXPERTS] + parts[N_EXPERTS:2 * N_EXPERTS] + brt_ref[...]
    expert = lax.broadcasted_iota(jnp.int32, (N_EXPERTS, tm), 0).astype(F32)
    vals, idxs, hots = [], [], []
    work = logits
    for _ in range(TOP_K):
        m = jnp.max(work, axis=0, keepdims=True)
        idx = jnp.min(jnp.where(work == m, expert, float(N_EXPERTS)), axis=0, keepdims=True)
        hot = expert == idx
        work = jnp.where(hot, -jnp.inf, work)
        vals.append(m)
        idxs.append(idx)
        hots.append(hot)
    es = [jnp.exp(v - vals[0]) for v in vals]
    denom = es[0] + es[1] + es[2] + es[3]

    chosen = jnp.where(hots[0] | hots[1] | hots[2] | hots[3], 1.0, 0.0).astype(BF16)
    r_i = lax.broadcasted_iota(jnp.int32, (tm, tm), 0)
    c_i = lax.broadcasted_iota(jnp.int32, (tm, tm), 1)
    earlier = jnp.where(r_i < c_i, 1.0, 0.0).astype(BF16)
    before = jnp.dot(chosen, earlier, preferred_element_type=F32) + carry_ref[:, 0:1]
    carry_ref[...] = carry_ref[...] + jnp.sum(chosen.astype(F32), axis=1, keepdims=True)
    cnt_ref[...] = carry_ref[...]

    row = lax.broadcasted_iota(jnp.int32, (2 * TOP_K, tm), 0)
    ri = jnp.zeros((2 * TOP_K, tm), F32)
    rw = jnp.zeros((2 * TOP_K, tm), F32)
    for k in range(TOP_K):
        rank = jnp.sum(jnp.where(hots[k], before, 0.0), axis=0, keepdims=True)
        ri = jnp.where(row == k, idxs[k], ri)
        ri = jnp.where(row == TOP_K + k, rank, ri)
        rw = jnp.where(row == k, es[k] / denom, rw)
    pad = jnp.zeros((LANES - 2 * TOP_K, tm), F32)
    ri_ref[...] = jnp.transpose(jnp.concatenate([ri, pad], axis=0)).astype(jnp.int32)
    rw_ref[...] = jnp.transpose(jnp.concatenate([rw, pad], axis=0))


def _merge_route(x, branch, gates, wpo, wco, wao, wo, norm_ffn, w_router, b_router, name):
    t = x.shape[0]
    tm = MERGE_TM
    assert t % tm == 0 and gates.shape == (t, N_BRANCH * D_MODEL)
    const = lambda i: (0, 0)
    wr_hi = w_router.astype(BF16)
    wr_lo = (w_router - wr_hi.astype(F32)).astype(BF16)
    zeros = jnp.zeros_like(wr_hi)
    wr = jnp.concatenate([jnp.concatenate([wr_hi, wr_lo], axis=1), jnp.concatenate([wr_hi, zeros], axis=1)], axis=0)
    wr = jnp.pad(wr, ((0, 0), (0, LANES - 2 * N_EXPERTS)))
    return pl.pallas_call(
        _merge_route_kernel,
        grid=(t // tm,),
        in_specs=[
            pl.BlockSpec((tm, D_MODEL), lambda i: (i, 0)),
            pl.BlockSpec((tm, D_MODEL), lambda i: (i, 0)),
            pl.BlockSpec((tm, D_MODEL), lambda i: (i, 0)),
            pl.BlockSpec((tm, D_MODEL), lambda i: (i, 1)),
            pl.BlockSpec((tm, D_MODEL), lambda i: (i, 2)),
            pl.BlockSpec((D_POOL, D_MODEL), const, pipeline_mode=pl.Buffered(1)),
            pl.BlockSpec((D_CONV, D_MODEL), const, pipeline_mode=pl.Buffered(1)),
            pl.BlockSpec((D_XATTN, D_MODEL), const, pipeline_mode=pl.Buffered(1)),
            pl.BlockSpec((D_MODEL, D_MODEL), const, pipeline_mode=pl.Buffered(1)),
            pl.BlockSpec((1, D_MODEL), const),
            pl.BlockSpec((2 * D_MODEL, LANES), const),
            pl.BlockSpec((N_EXPERTS, 1), const),
        ],
        out_specs=[
            pl.BlockSpec((tm, D_MODEL), lambda i: (i, 0)),
            pl.BlockSpec((tm, D_MODEL // 2), lambda i: (i, 0)),
            pl.BlockSpec((tm, LANES), lambda i: (i, 0)),
            pl.BlockSpec((tm, LANES), lambda i: (i, 0)),
            pl.BlockSpec((N_EXPERTS, LANES), const),
        ],
        out_shape=[
            jax.ShapeDtypeStruct((t, D_MODEL), F32),
            jax.ShapeDtypeStruct((t, D_MODEL // 2), PACKED),
            jax.ShapeDtypeStruct((t, LANES), jnp.int32),
            jax.ShapeDtypeStruct((t, LANES), F32),
            jax.ShapeDtypeStruct((N_EXPERTS, LANES), F32),
        ],
        scratch_shapes=[pltpu.VMEM((N_EXPERTS, LANES), F32)],
        compiler_params=pltpu.CompilerParams(
            dimension_semantics=("arbitrary",), vmem_limit_bytes=VMEM_LIMIT),
        name=name,
    )(x, branch, gates, gates, gates, wpo, wco, wao, wo, norm_ffn.reshape(1, D_MODEL),
      wr, b_router.reshape(N_EXPERTS, 1))


def _fill_work_list(nblk_s, pad_s, nj, refs):
    te, tj, trow, tnb, tfirst, tne, tnj, tpar, ttot = refs

    def clear(i, c):
        for r in (te, tj, trow, tnb, tfirst, tpar):
            r[i] = 0
        tne[i] = -1
        tnj[i] = -1
        return c

    lax.fori_loop(0, te.shape[0], clear, 0)

    carry = (jnp.int32(0), jnp.int32(-1), jnp.int32(0))
    for j in range(nj):
        def per_expert(e, carry):
            q, prev, groups = carry
            n = nblk_s[e]
            nch = (n + EXP_CHUNK - 1) // EXP_CHUNK

            @pl.when((nch > 0) & (prev >= 0))
            def _():
                tne[prev] = e
                tnj[prev] = jnp.int32(j)

            def per_chunk(c, q):
                te[q] = e
                tj[q] = jnp.int32(j)
                trow[q] = pad_s[e] + c * (EXP_CHUNK * EXP_TM)
                tnb[q] = jnp.minimum(EXP_CHUNK, n - c * EXP_CHUNK)
                tfirst[q] = (c == 0).astype(jnp.int32)
                tpar[q] = lax.rem(groups, 2)
                return q + 1

            return (lax.fori_loop(0, nch, per_chunk, q), jnp.where(nch > 0, q, prev),
                    groups + (nch > 0).astype(jnp.int32))

        carry = lax.fori_loop(0, N_EXPERTS, per_expert, carry)
    ttot[0] = carry[0]


def _route_tables_kernel(cntp_ref, cnts_ref, rip_ref, ris_ref, dest_ref, *refs, nj_in, nj_out):
    tabs_in, tabs_out = refs[0:N_TABS], refs[N_TABS:2 * N_TABS]
    nblk_s, pad_s = refs[2 * N_TABS:]

    def per_expert(e, start_blk):
        n = (cntp_ref[e] + cnts_ref[e] + EXP_TM - 1) // EXP_TM
        nblk_s[e] = n
        pad_s[e] = start_blk * EXP_TM
        return start_blk + n

    lax.fori_loop(0, N_EXPERTS, per_expert, jnp.int32(0))
    _fill_work_list(nblk_s, pad_s, nj_in, tabs_in)
    _fill_work_list(nblk_s, pad_s, nj_out, tabs_out)

    tile = ROUTE_TM
    col0 = 0
    for ri_ref, is_sample in ((rip_ref, False), (ris_ref, True)):
        for r in range(ri_ref.shape[0] // tile):
            rt = jnp.transpose(ri_ref[r * tile:(r + 1) * tile, :])[0:2 * TOP_K, :]
            base = jnp.zeros_like(rt)
            for e in range(N_EXPERTS):
                first_row = pad_s[e] + cntp_ref[e] if is_sample else pad_s[e]
                base = jnp.where(rt == e, first_row, base)
            dest_ref[:, col0:col0 + tile] = base + pltpu.roll(rt, TOP_K, axis=0)
            col0 += tile


def _route_tables(cnt_p, cnt_s, ri_p, ri_s, nj_in, nj_out, max_items):
    t_all = ri_p.shape[0] + ri_s.shape[0]
    assert ri_p.shape[0] % ROUTE_TM == 0 and ri_s.shape[0] % ROUTE_TM == 0
    smem = pl.BlockSpec(memory_space=pltpu.SMEM)

    def tables(nj):
        n = nj * max_items
        return [jax.ShapeDtypeStruct((n,), jnp.int32)] * (N_TABS - 1) + [jax.ShapeDtypeStruct((1,), jnp.int32)]

    out = pl.pallas_call(
        functools.partial(_route_tables_kernel, nj_in=nj_in, nj_out=nj_out),
        grid=(1,),
        in_specs=[smem, smem,
                  pl.BlockSpec(ri_p.shape, lambda i: (0, 0)), pl.BlockSpec(ri_s.shape, lambda i: (0, 0))],
        out_specs=[pl.BlockSpec((8, t_all), lambda i: (0, 0))] + [smem] * (2 * N_TABS),
        out_shape=[jax.ShapeDtypeStruct((8, t_all), jnp.int32)] + tables(nj_in) + tables(nj_out),
        scratch_shapes=[pltpu.SMEM((N_EXPERTS,), jnp.int32), pltpu.SMEM((N_EXPERTS,), jnp.int32)],
        compiler_params=pltpu.CompilerParams(
            dimension_semantics=("arbitrary",), vmem_limit_bytes=VMEM_LIMIT),
        name="route_tables",
    )(cnt_p, cnt_s, ri_p, ri_s)
    return out[0], tuple(out[1:1 + N_TABS]), tuple(out[1 + N_TABS:1 + 2 * N_TABS])


def _sc_mesh():
    return plsc.VectorSubcoreMesh(core_axis_name="c", subcore_axis_name="s",
                                  num_cores=SC_CORES, num_subcores=SC_SUBCORES)


def _sc_worker_id():
    return lax.axis_index("s") * SC_CORES + lax.axis_index("c")


def _sc_dispatch(h2_a, h2_b, dest_by_slot, rows_out):
    ta, w = h2_a.shape
    t = ta + h2_b.shape[0]
    per_w = t // SC_WORKERS
    n_chunks = per_w // SC_CHUNK
    assert t == SC_WORKERS * n_chunks * SC_CHUNK and ta % SC_CHUNK == 0
    idx = dest_by_slot[:TOP_K].reshape(TOP_K, SC_WORKERS, n_chunks, SC_CHUNK).transpose(1, 0, 2, 3)

    def body(a_hbm, b_hbm, idx_hbm, xs_hbm, idx_v, rows_v, sems):
        base = _sc_worker_id() * per_w
        pltpu.sync_copy(idx_hbm.at[_sc_worker_id()], idx_v)
        pending = [[], []]
        for c in range(n_chunks):
            b = c % 2
            for d in pending[b]:
                d.wait()
            tok0 = base + c * SC_CHUNK

            @pl.when(tok0 < ta)
            def _():
                pltpu.sync_copy(a_hbm.at[pl.ds(tok0, SC_CHUNK)], rows_v.at[b])

            @pl.when(tok0 >= ta)
            def _():
                pltpu.sync_copy(b_hbm.at[pl.ds(tok0 - ta, SC_CHUNK)], rows_v.at[b])

            pending[b] = [pltpu.async_copy(rows_v.at[b], xs_hbm.at[idx_v.at[k, c]], sems.at[b])
                          for k in range(TOP_K)]
        for b in range(2):
            for d in pending[b]:
                d.wait()

    return pl.kernel(
        body,
        out_type=jax.ShapeDtypeStruct((rows_out, w), h2_a.dtype),
        mesh=_sc_mesh(),
        scratch_types=[pltpu.VMEM((TOP_K, n_chunks, SC_CHUNK), jnp.int32),
                       pltpu.VMEM((2, SC_CHUNK, w), h2_a.dtype),
                       pltpu.SemaphoreType.DMA((2,))],
        name="sc_dispatch",
    )(h2_a, h2_b, idx)


def _sc_gather_rows(table, idx):
    n = idx.shape[0]
    w = table.shape[1]
    per_w = n // SC_WORKERS
    n_chunks = per_w // SC_CHUNK
    assert n == SC_WORKERS * n_chunks * SC_CHUNK
    idx3 = idx.reshape(SC_WORKERS, n_chunks, SC_CHUNK)

    def body(table_hbm, idx_hbm, out_hbm, idx_v, rows_v, gsem, wsems):
        base = _sc_worker_id() * per_w
        pltpu.sync_copy(idx_hbm.at[_sc_worker_id()], idx_v)
        pending = [None, None]
        for c in range(n_chunks):
            b = c % 2
            if pending[b] is not None:
                pending[b].wait()
            pltpu.async_copy(table_hbm.at[idx_v.at[c]], rows_v.at[b], gsem).wait()
            pending[b] = pltpu.async_copy(
                rows_v.at[b], out_hbm.at[pl.ds(base + c * SC_CHUNK, SC_CHUNK)], wsems.at[b])
        for b in range(2):
            if pending[b] is not None:
                pending[b].wait()

    return pl.kernel(
        body,
        out_type=jax.ShapeDtypeStruct((n, w), table.dtype),
        mesh=_sc_mesh(),
        scratch_types=[pltpu.VMEM((n_chunks, SC_CHUNK), jnp.int32),
                       pltpu.VMEM((2, SC_CHUNK, w), table.dtype),
                       pltpu.SemaphoreType.DMA,
                       pltpu.SemaphoreType.DMA((2,))],
        name="sc_gather_rows",
    )(table, idx3)


def _grouped_pipeline(tabs, total_ref, weight_copies, in_copy, out_copy, compute):
    te, tj, _, tnb, tfirst, tne, tnj, tpar = tabs
    total = total_ref[0]

    def wait_out(q):
        for m in range(1, EXP_CHUNK + 1):
            @pl.when(tnb[q] == m)
            def _():
                out_copy(q, lax.rem(q, 2), m).wait()

    for d in weight_copies(te[0], tj[0], tpar[0]):
        d.start()
    in_copy(0, 0).start()

    def body(q, carry):
        slot = lax.rem(q, 2)

        @pl.when(q + 1 < total)
        def _():
            in_copy(q + 1, 1 - slot).start()

        in_copy(q, slot).wait()

        @pl.when(tfirst[q] == 1)
        def _():
            for d in weight_copies(te[q], tj[q], tpar[q]):
                d.wait()

            @pl.when(tne[q] >= 0)
            def _():
                for d in weight_copies(tne[q], tnj[q], 1 - tpar[q]):
                    d.start()

        @pl.when(q >= 2)
        def _():
            wait_out(q - 2)

        for m in range(1, EXP_CHUNK + 1):
            @pl.when(tnb[q] == m)
            def _():
                compute(q, slot, m, tpar[q])
                out_copy(q, slot, m).start()

        return carry

    lax.fori_loop(0, total, body, 0)

    @pl.when(total >= 2)
    def _():
        wait_out(total - 2)

    wait_out(total - 1)


def _expert_in_kernel(*refs):
    tabs, total_ref = refs[:N_TABS - 1], refs[N_TABS - 1]
    xs_hbm, w_hbm, b_ref, act_hbm, wbuf, x_buf, a_buf, w_sem, x_sem, a_sem = refs[N_TABS:]
    te, tj, trow = tabs[:3]
    tm, tn = EXP_TM, a_buf.shape[2]
    nj = D_FF // tn

    def weight_copies(e, j, par):
        return [pltpu.make_async_copy(
            w_hbm.at[e, :, pl.ds(pl.multiple_of(half * D_FF + j * tn, tn), tn)],
            wbuf.at[2 * par + half], w_sem)
            for half in range(2)]

    def rows(q, m):
        return pl.ds(pl.multiple_of(trow[q], tm), m * tm)

    def x_copy(q, slot):
        return pltpu.make_async_copy(xs_hbm.at[rows(q, EXP_CHUNK)], x_buf.at[slot], x_sem.at[slot])

    def a_copy(q, slot, m):
        return pltpu.make_async_copy(
            a_buf.at[slot, pl.ds(0, m * tm)],
            act_hbm.at[rows(q, m), pl.ds(pl.multiple_of(tj[q] * tn, tn), tn)], a_sem.at[slot])

    def compute(q, slot, m, par):
        bias0 = te[q] * (2 * nj) + tj[q]
        x = jnp.concatenate(_unpack_bf16_pairs(x_buf[slot, 0:m * tm]), axis=1).astype(BF16)
        g = jnp.dot(x, wbuf[2 * par].astype(BF16), preferred_element_type=F32) + b_ref[bias0]
        up = jnp.dot(x, wbuf[2 * par + 1].astype(BF16), preferred_element_type=F32) + b_ref[bias0 + nj]
        g = jnp.minimum(g, SWIGLU_LIMIT)
        up = jnp.clip(up, -SWIGLU_LIMIT, SWIGLU_LIMIT)
        a_buf[slot, 0:m * tm] = (g * _sigmoid(SWIGLU_ALPHA * g) * (up + 1.0)).astype(BF16)

    _grouped_pipeline(tabs, total_ref, weight_copies, x_copy, a_copy, compute)


def _expert_in(tabs, xs, w_in, b_in):
    rows = xs.shape[0]
    tm, tn = EXP_TM, EXP_IN_TN
    nj = D_FF // tn
    bias = b_in.reshape(N_EXPERTS * 2 * nj, 1, tn)
    grid_spec = pltpu.PrefetchScalarGridSpec(
        num_scalar_prefetch=len(tabs),
        grid=(1,),
        in_specs=[
            pl.BlockSpec(memory_space=pl.ANY),
            pl.BlockSpec(memory_space=pl.ANY),
            pl.BlockSpec(bias.shape, lambda i, *_: (0, 0, 0), pipeline_mode=pl.Buffered(1)),
        ],
        out_specs=pl.BlockSpec(memory_space=pl.ANY),
        scratch_shapes=[pltpu.VMEM((4, D_MODEL, tn), F32),
                        pltpu.VMEM((2, EXP_CHUNK * tm, D_MODEL // 2), PACKED),
                        pltpu.VMEM((2, EXP_CHUNK * tm, tn), BF16),
                        pltpu.SemaphoreType.DMA, pltpu.SemaphoreType.DMA((2,)),
                        pltpu.SemaphoreType.DMA((2,))],
    )
    return pl.pallas_call(
        _expert_in_kernel,
        grid_spec=grid_spec,
        out_shape=jax.ShapeDtypeStruct((rows, D_FF), BF16),
        compiler_params=pltpu.CompilerParams(
            dimension_semantics=("arbitrary",), vmem_limit_bytes=VMEM_LIMIT),
        name="expert_in",
    )(*tabs, xs, w_in, bias)


def _expert_out_kernel(*refs):
    tabs, total_ref = refs[:N_TABS - 1], refs[N_TABS - 1]
    act_hbm, w_hbm, b_ref, ys_hbm, wbuf, a_buf, y_buf, w_sem, a_sem, y_sem = refs[N_TABS:]
    te, tj, trow = tabs[:3]
    tm, tnp = EXP_TM, y_buf.shape[2]
    tn = 2 * tnp
    nj = D_MODEL // tn

    def weight_copies(e, j, par):
        return [pltpu.make_async_copy(
            w_hbm.at[e, :, pl.ds(pl.multiple_of(j * tn, tn), tn)], wbuf.at[par], w_sem)]

    def rows(q, m):
        return pl.ds(pl.multiple_of(trow[q], tm), m * tm)

    def a_copy(q, slot):
        return pltpu.make_async_copy(act_hbm.at[rows(q, EXP_CHUNK)], a_buf.at[slot], a_sem.at[slot])

    def y_copy(q, slot, m):
        return pltpu.make_async_copy(
            y_buf.at[slot, pl.ds(0, m * tm)],
            ys_hbm.at[rows(q, m), pl.ds(pl.multiple_of(tj[q] * tnp, tnp), tnp)], y_sem.at[slot])

    def compute(q, slot, m, par):
        y = jnp.dot(a_buf[slot, 0:m * tm], wbuf[par].astype(BF16),
                    preferred_element_type=F32) + b_ref[te[q] * nj + tj[q]]
        y_buf[slot, 0:m * tm] = _pack_bf16_pairs(y)

    _grouped_pipeline(tabs, total_ref, weight_copies, a_copy, y_copy, compute)


def _expert_out(tabs, act, w_out, b_out):
    rows = act.shape[0]
    tm, tn = EXP_TM, EXP_OUT_TN
    nj = D_MODEL // tn
    bias = b_out.reshape(N_EXPERTS * nj, 1, tn)
    grid_spec = pltpu.PrefetchScalarGridSpec(
        num_scalar_prefetch=len(tabs),
        grid=(1,),
        in_specs=[
            pl.BlockSpec(memory_space=pl.ANY),
            pl.BlockSpec(memory_space=pl.ANY),
            pl.BlockSpec(bias.shape, lambda i, *_: (0, 0, 0), pipeline_mode=pl.Buffered(1)),
        ],
        out_specs=pl.BlockSpec(memory_space=pl.ANY),
        scratch_shapes=[pltpu.VMEM((2, D_FF, tn), F32),
                        pltpu.VMEM((2, EXP_CHUNK * tm, D_FF), BF16),
                        pltpu.VMEM((2, EXP_CHUNK * tm, tn // 2), PACKED),
                        pltpu.SemaphoreType.DMA, pltpu.SemaphoreType.DMA((2,)),
                        pltpu.SemaphoreType.DMA((2,))],
    )
    return pl.pallas_call(
        _expert_out_kernel,
        grid_spec=grid_spec,
        out_shape=jax.ShapeDtypeStruct((rows, D_MODEL // 2), PACKED),
        compiler_params=pltpu.CompilerParams(
            dimension_semantics=("arbitrary",), vmem_limit_bytes=VMEM_LIMIT),
        name="expert_out",
    )(*tabs, act, w_out, bias)


def _combine_kernel(x1_ref, yg_ref, rw_ref, g_ref, o_ref):
    half = EXP_OUT_TN // 2
    pieces = []
    ssq = jnp.zeros((x1_ref.shape[0], 1), F32)
    for jt in range(D_MODEL // EXP_OUT_TN):
        acc_lo = x1_ref[:, jt * EXP_OUT_TN:jt * EXP_OUT_TN + half]
        acc_hi = x1_ref[:, jt * EXP_OUT_TN + half:(jt + 1) * EXP_OUT_TN]
        moe_lo = jnp.zeros_like(acc_lo)
        moe_hi = jnp.zeros_like(acc_hi)
        for k in range(TOP_K):
            lo, hi = _unpack_bf16_pairs(yg_ref[k, :, jt * half:(jt + 1) * half])
            moe_lo = moe_lo + lo * rw_ref[:, k:k + 1]
            moe_hi = moe_hi + hi * rw_ref[:, k:k + 1]
        for acc in (acc_lo + moe_lo, acc_hi + moe_hi):
            ssq = ssq + jnp.sum(acc * acc, axis=-1, keepdims=True)
            pieces.append(acc)
    inv = lax.rsqrt(ssq / D_MODEL + EPS)
    for n, acc in enumerate(pieces):
        o_ref[:, n * half:(n + 1) * half] = acc * inv * g_ref[:, n * half:(n + 1) * half]


def _combine_part_kernel(x1_ref, yg_ref, rw_ref, g_ref, prev_hbm, o_ref):
    del prev_hbm
    _combine_kernel(x1_ref, yg_ref, rw_ref, g_ref, o_ref)


def _combine(x1, yg, rw, gain, row0, prev, name):
    t = x1.shape[0]
    n = yg.shape[1]
    tm = COMB_TM
    assert n % tm == 0 and row0 % tm == 0
    off = row0 // tm
    in_specs = [
        pl.BlockSpec((tm, D_MODEL), lambda i: (i + off, 0)),
        pl.BlockSpec((TOP_K, tm, D_MODEL // 2), lambda i: (0, i, 0)),
        pl.BlockSpec((tm, LANES), lambda i: (i + off, 0)),
        pl.BlockSpec((1, D_MODEL), lambda i: (0, 0)),
    ]
    args = [x1, yg, rw, gain.reshape(1, D_MODEL)]
    if prev is not None:
        in_specs.append(pl.BlockSpec(memory_space=pl.ANY))
        args.append(prev)
    return pl.pallas_call(
        _combine_kernel if prev is None else _combine_part_kernel,
        grid=(n // tm,),
        in_specs=in_specs,
        out_specs=pl.BlockSpec((tm, D_MODEL), lambda i: (i + off, 0)),
        out_shape=jax.ShapeDtypeStruct((t, D_MODEL), F32),
        input_output_aliases={} if prev is None else {len(args) - 1: 0},
        compiler_params=pltpu.CompilerParams(
            dimension_semantics=("arbitrary",), vmem_limit_bytes=VMEM_LIMIT),
        name=name,
    )(*args)


def kernel(x_prompt, x_sample, state_pool, state_conv, cache_mem_k, cache_mem_v, mem_prompt,
           norm_mix, w_in, b_gate, w_pool_group, pool_scale, w_conv, mem_norm, w_mem_kv,
           w_pool_out, w_conv_out, w_attn_out, w_o, norm_ffn, w_router, b_router,
           w_exp_in, b_exp_in, w_exp_out, b_exp_out, final_norm):
    depth = norm_mix.shape[0]
    assert depth == 1
    l = 0
    bp, seq, _ = x_prompt.shape
    bs, ln, _ = x_sample.shape
    tp, ts = bp * seq, bs * ln
    assert x_prompt.shape[2] == x_sample.shape[2] == D_MODEL and w_in.shape[1:] == (D_MODEL, D_IN_TOTAL)
    assert w_exp_in.shape[1:] == (N_EXPERTS, D_MODEL, 2 * D_FF) and w_exp_out.shape[1:] == (N_EXPERTS, D_FF, D_MODEL)
    assert cache_mem_k.shape[1:] == (bs, N_MEM, N_XHEADS, XHEAD_DIM) and mem_prompt.shape == (bp, N_MEM, D_MODEL)

    kv = _norm_matmul(mem_prompt.reshape(bp * N_MEM, D_MODEL), mem_norm[l], w_mem_kv[l],
                      jnp.zeros((2 * D_XATTN,), F32), "mem_kv")
    mk = kv[:, :D_XATTN].reshape(bp, N_MEM, D_XATTN)
    mv = kv[:, D_XATTN:].reshape(bp, N_MEM, D_XATTN)

    bias_in = jnp.concatenate([jnp.zeros((D_MIX,), F32), b_gate[l]])
    xp = x_prompt.reshape(tp, D_MODEL)
    xs_ = x_sample.reshape(ts, D_MODEL)
    proj_p, gates_p = _norm_matmul_split(xp, norm_mix[l], w_in[l], bias_in, D_MIX, "proj_prompt")
    proj_s, gates_s = _norm_matmul_split(xs_, norm_mix[l], w_in[l], bias_in, D_MIX, "proj_sample")

    br_p, zst_p = _mix_prompt(proj_p, mk, mv, w_pool_group[l], pool_scale[l], w_conv[l], bp, seq)
    br_s, zst_s = _mix_sample(proj_s, state_pool[l], state_conv[l],
                              cache_mem_k[l].reshape(bs, N_MEM * N_XHEADS, XHEAD_DIM),
                              cache_mem_v[l].reshape(bs, N_MEM * N_XHEADS, XHEAD_DIM),
                              w_pool_group[l], pool_scale[l], w_conv[l], bs, ln)

    wpo, wco, wao, wo = (w.astype(BF16) for w in (w_pool_out[l], w_conv_out[l], w_attn_out[l], w_o[l]))
    x1_p, h2_p, ri_p, rw_p, cnt_p = _merge_route(xp, br_p, gates_p, wpo, wco, wao, wo, norm_ffn[l],
                                                  w_router[l], b_router[l], "merge_route_prompt")
    x1_s, h2_s, ri_s, rw_s, cnt_s = _merge_route(xs_, br_s, gates_s, wpo, wco, wao, wo, norm_ffn[l],
                                                  w_router[l], b_router[l], "merge_route_sample")

    t_all = tp + ts
    n_assign = t_all * TOP_K
    nb_max = n_assign // EXP_TM + N_EXPERTS
    max_items = nb_max // EXP_CHUNK + N_EXPERTS
    dest, tabs_in, tabs_out = _route_tables(
        cnt_p[:, 0].astype(jnp.int32), cnt_s[:, 0].astype(jnp.int32), ri_p, ri_s,
        D_FF // EXP_IN_TN, D_MODEL // EXP_OUT_TN, max_items)
    xs_sorted = _sc_dispatch(h2_p, h2_s, dest, (nb_max + EXP_CHUNK) * EXP_TM)

    act = _expert_in(tabs_in, xs_sorted, w_exp_in[l], b_exp_in[l])
    ys = _expert_out(tabs_out, act, w_exp_out[l], b_exp_out[l])
    def combine_part(x1, rw, tok0, n, row0, prev, name):
        idx = dest[:TOP_K, tok0:tok0 + n].reshape(-1)
        yg = _sc_gather_rows(ys, idx).reshape(TOP_K, n, D_MODEL // 2)
        return _combine(x1, yg, rw, final_norm, row0, prev, name)

    y_s = combine_part(x1_s, rw_s, tp, ts, 0, None, "combine_sample")
    y_p = None
    tok0 = 0
    for c, units in enumerate(COMB_PARTS):
        n = tp * units // sum(COMB_PARTS)
        y_p = combine_part(x1_p, rw_p, tok0, n, tok0, y_p, f"combine_prompt_{c}")
        tok0 += n
    assert tok0 == tp

    new_pool_p = proj_p.reshape(bp, seq, D_MIX)[:, seq - POOL_STATE_LEN:, :D_POOL]
    new_conv_p = zst_p[:, 8 - (CONV_WIDTH - 1):]
    u_s = proj_s[:, :D_POOL].reshape(bs, ln, D_POOL)
    new_pool_s = jnp.concatenate([state_pool[l], u_s], axis=1)[:, -POOL_STATE_LEN:]
    new_conv_s = zst_s[:, ln - (CONV_WIDTH - 1):]

    return (y_p.reshape(bp, seq, D_MODEL), y_s.reshape(bs, ln, D_MODEL),
            new_pool_p[None], new_conv_p[None],
            mk.reshape(1, bp, N_MEM, N_XHEADS, XHEAD_DIM), mv.reshape(1, bp, N_MEM, N_XHEADS, XHEAD_DIM),
            new_pool_s[None], new_conv_s[None])
```

```python
import functools

import jax
import jax.numpy as jnp
from jax import lax
from jax.experimental import pallas as pl
from jax.experimental.pallas import tpu as pltpu
from jax.experimental.pallas import tpu_sc as plsc

F32 = jnp.float32
BF16 = jnp.bfloat16
PACKED = jnp.int32

D_MODEL = 2048
POOL_WINDOWS = (2, 4, 8, 16)
POOL_GROUP_DIM = 128
D_POOL = 512
POOL_STATE_LEN = 15
D_CONV = 1024
CONV_WIDTH = 3
N_MEM = 256
N_XHEADS = 4
XHEAD_DIM = 128
D_XATTN = 512
N_BRANCH = 3
D_MIX = D_POOL + 3 * D_CONV + D_XATTN
D_IN_TOTAL = D_MIX + N_BRANCH * D_MODEL
N_EXPERTS = 32
TOP_K = 4
D_FF = D_MODEL
SWIGLU_LIMIT = 7.0
SWIGLU_ALPHA = 1.702
EPS = 1e-5

C_U = 0
C_V = D_POOL
C_B = D_POOL + D_CONV
C_C = D_POOL + 2 * D_CONV
C_Q = D_POOL + 3 * D_CONV

LANES = 128
HIST = 16

PROJ_TM = 1024
PROJ_TN = 1024
MIX_TM = 512
KV_TM = 256
MIX_NS = 8
MERGE_TM = 256
ROUTE_TM = 1024
N_TABS = 9
EXP_TM = 128
EXP_CHUNK = 5
EXP_IN_TN = 1024
EXP_OUT_TN = 2048
COMB_TM = 512
COMB_PARTS = (2, 3, 3)
SC_CORES = 2
SC_SUBCORES = 16
SC_WORKERS = SC_CORES * SC_SUBCORES
SC_CHUNK = 32
VMEM_LIMIT = 56 * 1024 * 1024


def _sigmoid(x):
    return 0.5 * (jnp.tanh(0.5 * x) + 1.0)


def _rms(x, g):
    ms = jnp.mean(x * x, axis=-1, keepdims=True)
    return x * lax.rsqrt(ms + EPS) * g


def _pack_bf16_pairs(x):
    n = x.shape[1] // 2
    lo = lax.bitcast_convert_type(x[:, :n].astype(BF16).astype(F32), jnp.uint32)
    hi = lax.bitcast_convert_type(x[:, n:].astype(BF16).astype(F32), jnp.uint32)
    return lax.bitcast_convert_type((hi & jnp.uint32(0xFFFF0000)) | (lo >> 16), PACKED)


def _unpack_bf16_pairs(p):
    p = lax.bitcast_convert_type(p, jnp.uint32)
    lo = lax.bitcast_convert_type(p << 16, F32)
    hi = lax.bitcast_convert_type(p & jnp.uint32(0xFFFF0000), F32)
    return lo, hi


def _norm_matmul_kernel(x_ref, g_ref, w_ref, b_ref, o_ref, h_ref):
    @pl.when(pl.program_id(1) == 0)
    def _():
        h_ref[...] = _rms(x_ref[...], g_ref[...]).astype(BF16)

    o_ref[...] = jnp.dot(h_ref[...], w_ref[...].astype(BF16), preferred_element_type=F32) + b_ref[...]


def _norm_matmul(x, gain, w, bias, name):
    t, d = x.shape
    n = w.shape[1]
    tm = min(KV_TM, t)
    tn = PROJ_TN
    assert t % tm == 0 and n % tn == 0
    return pl.pallas_call(
        _norm_matmul_kernel,
        grid=(t // tm, n // tn),
        in_specs=[
            pl.BlockSpec((tm, d), lambda i, j: (i, 0)),
            pl.BlockSpec((1, d), lambda i, j: (0, 0)),
            pl.BlockSpec((d, tn), lambda i, j: (0, j)),
            pl.BlockSpec((1, tn), lambda i, j: (0, j)),
        ],
        out_specs=pl.BlockSpec((tm, tn), lambda i, j: (i, j)),
        out_shape=jax.ShapeDtypeStruct((t, n), F32),
        scratch_shapes=[pltpu.VMEM((tm, d), BF16)],
        compiler_params=pltpu.CompilerParams(
            dimension_semantics=("arbitrary", "arbitrary"), vmem_limit_bytes=VMEM_LIMIT),
        name=name,
    )(x, gain.reshape(1, d), w, bias.reshape(1, n))


def _norm_matmul_split_kernel(x_ref, g_ref, w_ref, b_ref, o_ref, og_ref, h_ref, *, n_main):
    j = pl.program_id(1)

    @pl.when(j == 0)
    def _():
        h_ref[...] = _rms(x_ref[...], g_ref[...]).astype(BF16)

    acc = jnp.dot(h_ref[...], w_ref[...].astype(BF16), preferred_element_type=F32) + b_ref[...]

    @pl.when(j < n_main)
    def _():
        o_ref[...] = acc

    @pl.when(j >= n_main)
    def _():
        og_ref[...] = acc.astype(BF16)


def _norm_matmul_split(x, gain, w, bias, split_col, name):
    t, d = x.shape
    n = w.shape[1]
    tm = min(PROJ_TM, t)
    tn = PROJ_TN
    assert t % tm == 0 and n % tn == 0 and split_col % tn == 0 and 0 < split_col < n
    n_main = split_col // tn
    return pl.pallas_call(
        functools.partial(_norm_matmul_split_kernel, n_main=n_main),
        grid=(t // tm, n // tn),
        in_specs=[
            pl.BlockSpec((tm, d), lambda i, j: (i, 0)),
            pl.BlockSpec((1, d), lambda i, j: (0, 0)),
            pl.BlockSpec((d, tn), lambda i, j: (0, j)),
            pl.BlockSpec((1, tn), lambda i, j: (0, j)),
        ],
        out_specs=[
            pl.BlockSpec((tm, tn), lambda i, j: (i, jnp.minimum(j, n_main - 1))),
            pl.BlockSpec((tm, tn), lambda i, j: (i, jnp.maximum(j - n_main, 0))),
        ],
        out_shape=[jax.ShapeDtypeStruct((t, split_col), F32),
                   jax.ShapeDtypeStruct((t, n - split_col), BF16)],
        scratch_shapes=[pltpu.VMEM((tm, d), BF16)],
        compiler_params=pltpu.CompilerParams(
            dimension_semantics=("arbitrary", "arbitrary"), vmem_limit_bytes=VMEM_LIMIT),
        name=name,
    )(x, gain.reshape(1, d), w, bias.reshape(1, n))


def _pool_project(pooled, wpg_ref, scale_ref, g):
    sl = slice(g * POOL_GROUP_DIM, (g + 1) * POOL_GROUP_DIM)
    y = jnp.dot(pooled.astype(BF16), wpg_ref[g].astype(BF16), preferred_element_type=F32)
    return y * scale_ref[:, sl]


def _softmax_rows(s):
    m = jnp.max(s, axis=-1, keepdims=True)
    e = jnp.exp(s - m)
    return e / jnp.sum(e, axis=-1, keepdims=True)


def _mix_prompt_kernel(cur_ref, prev_ref, mk_ref, mv_ref, wpg_ref, scale_ref, wconv_ref,
                       br_ref, zst_ref, extu_ref, extz_ref):
    t = pl.program_id(1)
    tm = cur_ref.shape[0]
    has_prev = t > 0

    u = cur_ref[:, C_U:C_U + D_POOL]
    extu_ref[0:HIST, :] = jnp.where(has_prev, prev_ref[:, C_U:C_U + D_POOL], 0.0)
    extu_ref[HIST:HIST + tm, :] = u
    pos = t * tm + lax.broadcasted_iota(jnp.int32, (tm, 1), 0)
    for g, w in enumerate(POOL_WINDOWS):
        sl = slice(g * POOL_GROUP_DIM, (g + 1) * POOL_GROUP_DIM)
        s = extu_ref[HIST:HIST + tm, sl]
        for k in range(1, w):
            s = s + extu_ref[HIST - k:HIST - k + tm, sl]
        cnt = jnp.minimum(w, pos + 1).astype(F32)
        pooled = s / cnt - extu_ref[HIST:HIST + tm, sl]
        br_ref[:, sl] = _pool_project(pooled, wpg_ref, scale_ref, g).astype(BF16)

    z = cur_ref[:, C_C:C_C + D_CONV] * cur_ref[:, C_V:C_V + D_CONV]
    zprev = prev_ref[:, C_C:C_C + D_CONV] * prev_ref[:, C_V:C_V + D_CONV]
    extz_ref[0:HIST, :] = jnp.where(has_prev, zprev, 0.0)
    extz_ref[HIST:HIST + tm, :] = z
    y = extz_ref[HIST - 2:HIST - 2 + tm, :] * wconv_ref[0:1, :]
    y = y + extz_ref[HIST - 1:HIST - 1 + tm, :] * wconv_ref[1:2, :]
    y = y + extz_ref[HIST:HIST + tm, :] * wconv_ref[2:3, :]
    br_ref[:, D_POOL:D_POOL + D_CONV] = (cur_ref[:, C_B:C_B + D_CONV] * y).astype(BF16)
    zst_ref[0] = extz_ref[HIST + tm - 8:HIST + tm, :]

    for h in range(N_XHEADS):
        sl = slice(h * XHEAD_DIM, (h + 1) * XHEAD_DIM)
        qh = cur_ref[:, C_Q + h * XHEAD_DIM:C_Q + (h + 1) * XHEAD_DIM].astype(BF16)
        kh = mk_ref[0, :, sl].astype(BF16)
        vh = mv_ref[0, :, sl].astype(BF16)
        s = lax.dot_general(qh, kh, (((1,), (1,)), ((), ())), preferred_element_type=F32)
        p = _softmax_rows(s * (XHEAD_DIM ** -0.5))
        o = jnp.dot(p.astype(BF16), vh, preferred_element_type=F32)
        c0 = D_POOL + D_CONV + h * XHEAD_DIM
        br_ref[:, c0:c0 + XHEAD_DIM] = o.astype(BF16)


def _mix_prompt(proj, mk, mv, wpg, scale, wconv, batch, seq):
    tm = MIX_TM
    nt = seq // tm
    assert seq % tm == 0 and tm % HIST == 0
    rpb = tm // HIST
    return pl.pallas_call(
        _mix_prompt_kernel,
        grid=(batch, nt),
        in_specs=[
            pl.BlockSpec((tm, D_MIX), lambda b, t: (b * nt + t, 0)),
            pl.BlockSpec((HIST, D_MIX), lambda b, t: (jnp.maximum((b * nt + t) * rpb - 1, 0), 0)),
            pl.BlockSpec((1, N_MEM, D_XATTN), lambda b, t: (b, 0, 0)),
            pl.BlockSpec((1, N_MEM, D_XATTN), lambda b, t: (b, 0, 0)),
            pl.BlockSpec((len(POOL_WINDOWS), POOL_GROUP_DIM, POOL_GROUP_DIM), lambda b, t: (0, 0, 0)),
            pl.BlockSpec((1, D_POOL), lambda b, t: (0, 0)),
            pl.BlockSpec((CONV_WIDTH, D_CONV), lambda b, t: (0, 0)),
        ],
        out_specs=[
            pl.BlockSpec((tm, D_MODEL), lambda b, t: (b * nt + t, 0)),
            pl.BlockSpec((1, 8, D_CONV), lambda b, t: (b, 0, 0)),
        ],
        out_shape=[
            jax.ShapeDtypeStruct((batch * seq, D_MODEL), BF16),
            jax.ShapeDtypeStruct((batch, 8, D_CONV), F32),
        ],
        scratch_shapes=[pltpu.VMEM((HIST + tm, D_POOL), F32), pltpu.VMEM((HIST + tm, D_CONV), F32)],
        compiler_params=pltpu.CompilerParams(
            dimension_semantics=("arbitrary", "arbitrary"), vmem_limit_bytes=VMEM_LIMIT),
        name="mix_prompt",
    )(proj, proj, mk, mv, wpg, scale.reshape(1, D_POOL), wconv)


def _mix_sample_kernel(cur_ref, sp_ref, sc_ref, k_ref, v_ref, wpg_ref, scale_ref, wconv_ref,
                       br_ref, zst_ref, extu_ref, extz_ref):
    ns, ln = zst_ref.shape[0], zst_ref.shape[1]
    rows = ns * ln

    extu_ref[:, HIST - POOL_STATE_LEN:HIST, :] = sp_ref[...]
    extu_ref[:, HIST:HIST + ln, :] = cur_ref[:, C_U:C_U + D_POOL].reshape(ns, ln, D_POOL)
    for g, w in enumerate(POOL_WINDOWS):
        sl = slice(g * POOL_GROUP_DIM, (g + 1) * POOL_GROUP_DIM)
        s = extu_ref[:, HIST:HIST + ln, sl]
        for k in range(1, w):
            s = s + extu_ref[:, HIST - k:HIST - k + ln, sl]
        pooled = s / float(w) - extu_ref[:, HIST:HIST + ln, sl]
        pooled = pooled.reshape(rows, POOL_GROUP_DIM)
        br_ref[:, sl] = _pool_project(pooled, wpg_ref, scale_ref, g).astype(BF16)

    z = cur_ref[:, C_C:C_C + D_CONV] * cur_ref[:, C_V:C_V + D_CONV]
    extz_ref[:, HIST - 2:HIST, :] = sc_ref[...]
    extz_ref[:, HIST:HIST + ln, :] = z.reshape(ns, ln, D_CONV)
    y = extz_ref[:, HIST - 2:HIST - 2 + ln, :] * wconv_ref[0:1, :]
    y = y + extz_ref[:, HIST - 1:HIST - 1 + ln, :] * wconv_ref[1:2, :]
    y = y + extz_ref[:, HIST:HIST + ln, :] * wconv_ref[2:3, :]
    br_ref[:, D_POOL:D_POOL + D_CONV] = (
        cur_ref[:, C_B:C_B + D_CONV] * y.reshape(rows, D_CONV)).astype(BF16)
    zst_ref[...] = extz_ref[:, HIST:HIST + ln, :]

    q3 = cur_ref[:, C_Q:C_Q + D_XATTN].reshape(ns, ln, D_XATTN)
    q4 = jnp.concatenate([q3[:, :, h * XHEAD_DIM:(h + 1) * XHEAD_DIM] for h in range(N_XHEADS)], axis=1)
    s = jnp.einsum("nqd,nkd->nqk", q4.astype(BF16), k_ref[...].astype(BF16),
                   preferred_element_type=F32)
    row_head = lax.broadcasted_iota(jnp.int32, (N_XHEADS * ln, N_XHEADS * N_MEM), 0) // ln
    col_head = lax.broadcasted_iota(jnp.int32, (N_XHEADS * ln, N_XHEADS * N_MEM), 1) % N_XHEADS
    s = jnp.where((row_head == col_head)[None], s * (XHEAD_DIM ** -0.5), -jnp.inf)
    p = _softmax_rows(s)
    r = jnp.einsum("nqk,nkd->nqd", p.astype(BF16), v_ref[...].astype(BF16),
                   preferred_element_type=F32)
    for h in range(N_XHEADS):
        c0 = D_POOL + D_CONV + h * XHEAD_DIM
        br_ref[:, c0:c0 + XHEAD_DIM] = r[:, h * ln:(h + 1) * ln, :].reshape(rows, XHEAD_DIM).astype(BF16)


def _mix_sample(proj, state_pool, state_conv, mem_k, mem_v, wpg, scale, wconv, nseq, ln):
    ns = MIX_NS
    rows = ns * ln
    assert nseq % ns == 0 and ln == 8
    return pl.pallas_call(
        _mix_sample_kernel,
        grid=(nseq // ns,),
        in_specs=[
            pl.BlockSpec((rows, D_MIX), lambda s: (s, 0)),
            pl.BlockSpec((ns, POOL_STATE_LEN, D_POOL), lambda s: (s, 0, 0)),
            pl.BlockSpec((ns, CONV_WIDTH - 1, D_CONV), lambda s: (s, 0, 0)),
            pl.BlockSpec((ns, N_MEM * N_XHEADS, XHEAD_DIM), lambda s: (s, 0, 0)),
            pl.BlockSpec((ns, N_MEM * N_XHEADS, XHEAD_DIM), lambda s: (s, 0, 0)),
            pl.BlockSpec((len(POOL_WINDOWS), POOL_GROUP_DIM, POOL_GROUP_DIM), lambda s: (0, 0, 0)),
            pl.BlockSpec((1, D_POOL), lambda s: (0, 0)),
            pl.BlockSpec((CONV_WIDTH, D_CONV), lambda s: (0, 0)),
        ],
        out_specs=[
            pl.BlockSpec((rows, D_MODEL), lambda s: (s, 0)),
            pl.BlockSpec((ns, ln, D_CONV), lambda s: (s, 0, 0)),
        ],
        out_shape=[
            jax.ShapeDtypeStruct((nseq * ln, D_MODEL), BF16),
            jax.ShapeDtypeStruct((nseq, ln, D_CONV), F32),
        ],
        scratch_shapes=[pltpu.VMEM((ns, HIST + ln, D_POOL), F32),
                        pltpu.VMEM((ns, HIST + ln, D_CONV), F32)],
        compiler_params=pltpu.CompilerParams(
            dimension_semantics=("arbitrary",), vmem_limit_bytes=VMEM_LIMIT),
        name="mix_sample",
    )(proj, state_pool, state_conv, mem_k, mem_v, wpg, scale.reshape(1, D_POOL), wconv)


def _merge_route_kernel(x_ref, br_ref, g0_ref, g1_ref, g2_ref, wpo_ref, wco_ref, wao_ref, wo_ref,
                        nf_ref, wr_ref, brt_ref,
                        x1_ref, h2_ref, ri_ref, rw_ref, cnt_ref, carry_ref):
    i = pl.program_id(0)
    tm = x_ref.shape[0]

    @pl.when(i == 0)
    def _():
        carry_ref[...] = jnp.zeros_like(carry_ref)

    merged = _sigmoid(g0_ref[...].astype(F32)) * jnp.dot(
        br_ref[:, 0:D_POOL], wpo_ref[...], preferred_element_type=F32)
    merged = merged + _sigmoid(g1_ref[...].astype(F32)) * jnp.dot(
        br_ref[:, D_POOL:D_POOL + D_CONV], wco_ref[...], preferred_element_type=F32)
    merged = merged + _sigmoid(g2_ref[...].astype(F32)) * jnp.dot(
        br_ref[:, D_POOL + D_CONV:D_MODEL], wao_ref[...], preferred_element_type=F32)
    x1 = x_ref[...] + jnp.dot(merged.astype(BF16), wo_ref[...], preferred_element_type=F32)
    x1_ref[...] = x1
    h2 = _rms(x1, nf_ref[...])
    h2_ref[...] = _pack_bf16_pairs(h2)

    h2_hi = h2.astype(BF16)
    h2_lo = (h2 - h2_hi.astype(F32)).astype(BF16)
    parts = jnp.transpose(jnp.dot(jnp.concatenate([h2_hi, h2_lo], axis=1), wr_ref[...],
                                  preferred_element_type=F32))
    logits = parts[0:N_EXPERTS] + parts[N_EXPERTS:2 * N_EXPERTS] + brt_ref[...]
    expert = lax.broadcasted_iota(jnp.int32, (N_EXPERTS, tm), 0).astype(F32)
    vals, idxs, hots = [], [], []
    work = logits
    for _ in range(TOP_K):
        m = jnp.max(work, axis=0, keepdims=True)
        idx = jnp.min(jnp.where(work == m, expert, float(N_EXPERTS)), axis=0, keepdims=True)
        hot = expert == idx
        work = jnp.where(hot, -jnp.inf, work)
        vals.append(m)
        idxs.append(idx)
        hots.append(hot)
    es = [jnp.exp(v - vals[0]) for v in vals]
    denom = es[0] + es[1] + es[2] + es[3]

    chosen = jnp.where(hots[0] | hots[1] | hots[2] | hots[3], 1.0, 0.0).astype(BF16)
    r_i = lax.broadcasted_iota(jnp.int32, (tm, tm), 0)
    c_i = lax.broadcasted_iota(jnp.int32, (tm, tm), 1)
    earlier = jnp.where(r_i < c_i, 1.0, 0.0).astype(BF16)
    before = jnp.dot(chosen, earlier, preferred_element_type=F32) + carry_ref[:, 0:1]
    carry_ref[...] = carry_ref[...] + jnp.sum(chosen.astype(F32), axis=1, keepdims=True)
    cnt_ref[...] = carry_ref[...]

    row = lax.broadcasted_iota(jnp.int32, (2 * TOP_K, tm), 0)
    ri = jnp.zeros((2 * TOP_K, tm), F32)
    rw = jnp.zeros((2 * TOP_K, tm), F32)
    for k in range(TOP_K):
        rank = jnp.sum(jnp.where(hots[k], before, 0.0), axis=0, keepdims=True)
        ri = jnp.where(row == k, idxs[k], ri)
        ri = jnp.where(row == TOP_K + k, rank, ri)
        rw = jnp.where(row == k, es[k] / denom, rw)
    pad = jnp.zeros((LANES - 2 * TOP_K, tm), F32)
    ri_ref[...] = jnp.transpose(jnp.concatenate([ri, pad], axis=0)).astype(jnp.int32)
    rw_ref[...] = jnp.transpose(jnp.concatenate([rw, pad], axis=0))


def _merge_route(x, branch, gates, wpo, wco, wao, wo, norm_ffn, w_router, b_router, name):
    t = x.shape[0]
    tm = MERGE_TM
    assert t % tm == 0 and gates.shape == (t, N_BRANCH * D_MODEL)
    const = lambda i: (0, 0)
    wr_hi = w_router.astype(BF16)
    wr_lo = (w_router - wr_hi.astype(F32)).astype(BF16)
    zeros = jnp.zeros_like(wr_hi)
    wr = jnp.concatenate([jnp.concatenate([wr_hi, wr_lo], axis=1), jnp.concatenate([wr_hi, zeros], axis=1)], axis=0)
    wr = jnp.pad(wr, ((0, 0), (0, LANES - 2 * N_EXPERTS)))
    return pl.pallas_call(
        _merge_route_kernel,
        grid=(t // tm,),
        in_specs=[
            pl.BlockSpec((tm, D_MODEL), lambda i: (i, 0)),
            pl.BlockSpec((tm, D_MODEL), lambda i: (i, 0)),
            pl.BlockSpec((tm, D_MODEL), lambda i: (i, 0)),
            pl.BlockSpec((tm, D_MODEL), lambda i: (i, 1)),
            pl.BlockSpec((tm, D_MODEL), lambda i: (i, 2)),
            pl.BlockSpec((D_POOL, D_MODEL), const, pipeline_mode=pl.Buffered(1)),
            pl.BlockSpec((D_CONV, D_MODEL), const, pipeline_mode=pl.Buffered(1)),
            pl.BlockSpec((D_XATTN, D_MODEL), const, pipeline_mode=pl.Buffered(1)),
            pl.BlockSpec((D_MODEL, D_MODEL), const, pipeline_mode=pl.Buffered(1)),
            pl.BlockSpec((1, D_MODEL), const),
            pl.BlockSpec((2 * D_MODEL, LANES), const),
            pl.BlockSpec((N_EXPERTS, 1), const),
        ],
        out_specs=[
            pl.BlockSpec((tm, D_MODEL), lambda i: (i, 0)),
            pl.BlockSpec((tm, D_MODEL // 2), lambda i: (i, 0)),
            pl.BlockSpec((tm, LANES), lambda i: (i, 0)),
            pl.BlockSpec((tm, LANES), lambda i: (i, 0)),
            pl.BlockSpec((N_EXPERTS, LANES), const),
        ],
        out_shape=[
            jax.ShapeDtypeStruct((t, D_MODEL), F32),
            jax.ShapeDtypeStruct((t, D_MODEL // 2), PACKED),
            jax.ShapeDtypeStruct((t, LANES), jnp.int32),
            jax.ShapeDtypeStruct((t, LANES), F32),
            jax.ShapeDtypeStruct((N_EXPERTS, LANES), F32),
        ],
        scratch_shapes=[pltpu.VMEM((N_EXPERTS, LANES), F32)],
        compiler_params=pltpu.CompilerParams(
            dimension_semantics=("arbitrary",), vmem_limit_bytes=VMEM_LIMIT),
        name=name,
    )(x, branch, gates, gates, gates, wpo, wco, wao, wo, norm_ffn.reshape(1, D_MODEL),
      wr, b_router.reshape(N_EXPERTS, 1))


def _fill_work_list(nblk_s, pad_s, nj, refs):
    te, tj, trow, tnb, tfirst, tne, tnj, tpar, ttot = refs

    def clear(i, c):
        for r in (te, tj, trow, tnb, tfirst, tpar):
            r[i] = 0
        tne[i] = -1
        tnj[i] = -1
        return c

    lax.fori_loop(0, te.shape[0], clear, 0)

    carry = (jnp.int32(0), jnp.int32(-1), jnp.int32(0))
    for j in range(nj):
        def per_expert(e, carry):
            q, prev, groups = carry
            n = nblk_s[e]
            nch = (n + EXP_CHUNK - 1) // EXP_CHUNK

            @pl.when((nch > 0) & (prev >= 0))
            def _():
                tne[prev] = e
                tnj[prev] = jnp.int32(j)

            def per_chunk(c, q):
                te[q] = e
                tj[q] = jnp.int32(j)
                trow[q] = pad_s[e] + c * (EXP_CHUNK * EXP_TM)
                tnb[q] = jnp.minimum(EXP_CHUNK, n - c * EXP_CHUNK)
                tfirst[q] = (c == 0).astype(jnp.int32)
                tpar[q] = lax.rem(groups, 2)
                return q + 1

            return (lax.fori_loop(0, nch, per_chunk, q), jnp.where(nch > 0, q, prev),
                    groups + (nch > 0).astype(jnp.int32))

        carry = lax.fori_loop(0, N_EXPERTS, per_expert, carry)
    ttot[0] = carry[0]


def _route_tables_kernel(cntp_ref, cnts_ref, rip_ref, ris_ref, dest_ref, *refs, nj_in, nj_out):
    tabs_in, tabs_out = refs[0:N_TABS], refs[N_TABS:2 * N_TABS]
    nblk_s, pad_s = refs[2 * N_TABS:]

    def per_expert(e, start_blk):
        n = (cntp_ref[e] + cnts_ref[e] + EXP_TM - 1) // EXP_TM
        nblk_s[e] = n
        pad_s[e] = start_blk * EXP_TM
        return start_blk + n

    lax.fori_loop(0, N_EXPERTS, per_expert, jnp.int32(0))
    _fill_work_list(nblk_s, pad_s, nj_in, tabs_in)
    _fill_work_list(nblk_s, pad_s, nj_out, tabs_out)

    tile = ROUTE_TM
    col0 = 0
    for ri_ref, is_sample in ((rip_ref, False), (ris_ref, True)):
        for r in range(ri_ref.shape[0] // tile):
            rt = jnp.transpose(ri_ref[r * tile:(r + 1) * tile, :])[0:2 * TOP_K, :]
            base = jnp.zeros_like(rt)
            for e in range(N_EXPERTS):
                first_row = pad_s[e] + cntp_ref[e] if is_sample else pad_s[e]
                base = jnp.where(rt == e, first_row, base)
            dest_ref[:, col0:col0 + tile] = base + pltpu.roll(rt, TOP_K, axis=0)
            col0 += tile


def _route_tables(cnt_p, cnt_s, ri_p, ri_s, nj_in, nj_out, max_items):
    t_all = ri_p.shape[0] + ri_s.shape[0]
    assert ri_p.shape[0] % ROUTE_TM == 0 and ri_s.shape[0] % ROUTE_TM == 0
    smem = pl.BlockSpec(memory_space=pltpu.SMEM)

    def tables(nj):
        n = nj * max_items
        return [jax.ShapeDtypeStruct((n,), jnp.int32)] * (N_TABS - 1) + [jax.ShapeDtypeStruct((1,), jnp.int32)]

    out = pl.pallas_call(
        functools.partial(_route_tables_kernel, nj_in=nj_in, nj_out=nj_out),
        grid=(1,),
        in_specs=[smem, smem,
                  pl.BlockSpec(ri_p.shape, lambda i: (0, 0)), pl.BlockSpec(ri_s.shape, lambda i: (0, 0))],
        out_specs=[pl.BlockSpec((8, t_all), lambda i: (0, 0))] + [smem] * (2 * N_TABS),
        out_shape=[jax.ShapeDtypeStruct((8, t_all), jnp.int32)] + tables(nj_in) + tables(nj_out),
        scratch_shapes=[pltpu.SMEM((N_EXPERTS,), jnp.int32), pltpu.SMEM((N_EXPERTS,), jnp.int32)],
        compiler_params=pltpu.CompilerParams(
            dimension_semantics=("arbitrary",), vmem_limit_bytes=VMEM_LIMIT),
        name="route_tables",
    )(cnt_p, cnt_s, ri_p, ri_s)
    return out[0], tuple(out[1:1 + N_TABS]), tuple(out[1 + N_TABS:1 + 2 * N_TABS])


def _sc_mesh():
    return plsc.VectorSubcoreMesh(core_axis_name="c", subcore_axis_name="s",
                                  num_cores=SC_CORES, num_subcores=SC_SUBCORES)


def _sc_worker_id():
    return lax.axis_index("s") * SC_CORES + lax.axis_index("c")


def _sc_dispatch(h2_a, h2_b, dest_by_slot, rows_out):
    ta, w = h2_a.shape
    t = ta + h2_b.shape[0]
    per_w = t // SC_WORKERS
    n_chunks = per_w // SC_CHUNK
    assert t == SC_WORKERS * n_chunks * SC_CHUNK and ta % SC_CHUNK == 0
    idx = dest_by_slot[:TOP_K].reshape(TOP_K, SC_WORKERS, n_chunks, SC_CHUNK).transpose(1, 0, 2, 3)

    def body(a_hbm, b_hbm, idx_hbm, xs_hbm, idx_v, rows_v, sems):
        base = _sc_worker_id() * per_w
        pltpu.sync_copy(idx_hbm.at[_sc_worker_id()], idx_v)
        pending = [[], []]
        for c in range(n_chunks):
            b = c % 2
            for d in pending[b]:
                d.wait()
            tok0 = base + c * SC_CHUNK

            @pl.when(tok0 < ta)
            def _():
                pltpu.sync_copy(a_hbm.at[pl.ds(tok0, SC_CHUNK)], rows_v.at[b])

            @pl.when(tok0 >= ta)
            def _():
                pltpu.sync_copy(b_hbm.at[pl.ds(tok0 - ta, SC_CHUNK)], rows_v.at[b])

            pending[b] = [pltpu.async_copy(rows_v.at[b], xs_hbm.at[idx_v.at[k, c]], sems.at[b])
                          for k in range(TOP_K)]
        for b in range(2):
            for d in pending[b]:
                d.wait()

    return pl.kernel(
        body,
        out_type=jax.ShapeDtypeStruct((rows_out, w), h2_a.dtype),
        mesh=_sc_mesh(),
        scratch_types=[pltpu.VMEM((TOP_K, n_chunks, SC_CHUNK), jnp.int32),
                       pltpu.VMEM((2, SC_CHUNK, w), h2_a.dtype),
                       pltpu.SemaphoreType.DMA((2,))],
        name="sc_dispatch",
    )(h2_a, h2_b, idx)


def _sc_gather_rows(table, idx):
    n = idx.shape[0]
    w = table.shape[1]
    per_w = n // SC_WORKERS
    n_chunks = per_w // SC_CHUNK
    assert n == SC_WORKERS * n_chunks * SC_CHUNK
    idx3 = idx.reshape(SC_WORKERS, n_chunks, SC_CHUNK)

    def body(table_hbm, idx_hbm, out_hbm, idx_v, rows_v, gsem, wsems):
        base = _sc_worker_id() * per_w
        pltpu.sync_copy(idx_hbm.at[_sc_worker_id()], idx_v)
        pending = [None, None]
        for c in range(n_chunks):
            b = c % 2
            if pending[b] is not None:
                pending[b].wait()
            pltpu.async_copy(table_hbm.at[idx_v.at[c]], rows_v.at[b], gsem).wait()
            pending[b] = pltpu.async_copy(
                rows_v.at[b], out_hbm.at[pl.ds(base + c * SC_CHUNK, SC_CHUNK)], wsems.at[b])
        for b in range(2):
            if pending[b] is not None:
                pending[b].wait()

    return pl.kernel(
        body,
        out_type=jax.ShapeDtypeStruct((n, w), table.dtype),
        mesh=_sc_mesh(),
        scratch_types=[pltpu.VMEM((n_chunks, SC_CHUNK), jnp.int32),
                       pltpu.VMEM((2, SC_CHUNK, w), table.dtype),
                       pltpu.SemaphoreType.DMA,
                       pltpu.SemaphoreType.DMA((2,))],
        name="sc_gather_rows",
    )(table, idx3)


def _grouped_pipeline(tabs, total_ref, weight_copies, in_copy, out_copy, compute):
    te, tj, _, tnb, tfirst, tne, tnj, tpar = tabs
    total = total_ref[0]

    def wait_out(q):
        for m in range(1, EXP_CHUNK + 1):
            @pl.when(tnb[q] == m)
            def _():
                out_copy(q, lax.rem(q, 2), m).wait()

    for d in weight_copies(te[0], tj[0], tpar[0]):
        d.start()
    in_copy(0, 0).start()

    def body(q, carry):
        slot = lax.rem(q, 2)

        @pl.when(q + 1 < total)
        def _():
            in_copy(q + 1, 1 - slot).start()

        in_copy(q, slot).wait()

        @pl.when(tfirst[q] == 1)
        def _():
            for d in weight_copies(te[q], tj[q], tpar[q]):
                d.wait()

            @pl.when(tne[q] >= 0)
            def _():
                for d in weight_copies(tne[q], tnj[q], 1 - tpar[q]):
                    d.start()

        @pl.when(q >= 2)
        def _():
            wait_out(q - 2)

        for m in range(1, EXP_CHUNK + 1):
            @pl.when(tnb[q] == m)
            def _():
                compute(q, slot, m, tpar[q])
                out_copy(q, slot, m).start()

        return carry

    lax.fori_loop(0, total, body, 0)

    @pl.when(total >= 2)
    def _():
        wait_out(total - 2)

    wait_out(total - 1)


def _expert_in_kernel(*refs):
    tabs, total_ref = refs[:N_TABS - 1], refs[N_TABS - 1]
    xs_hbm, w_hbm, b_ref, act_hbm, wbuf, x_buf, a_buf, w_sem, x_sem, a_sem = refs[N_TABS:]
    te, tj, trow = tabs[:3]
    tm, tn = EXP_TM, a_buf.shape[2]
    nj = D_FF // tn

    def weight_copies(e, j, par):
        return [pltpu.make_async_copy(
            w_hbm.at[e, :, pl.ds(pl.multiple_of(half * D_FF + j * tn, tn), tn)],
            wbuf.at[2 * par + half], w_sem)
            for half in range(2)]

    def rows(q, m):
        return pl.ds(pl.multiple_of(trow[q], tm), m * tm)

    def x_copy(q, slot):
        return pltpu.make_async_copy(xs_hbm.at[rows(q, EXP_CHUNK)], x_buf.at[slot], x_sem.at[slot])

    def a_copy(q, slot, m):
        return pltpu.make_async_copy(
            a_buf.at[slot, pl.ds(0, m * tm)],
            act_hbm.at[rows(q, m), pl.ds(pl.multiple_of(tj[q] * tn, tn), tn)], a_sem.at[slot])

    def compute(q, slot, m, par):
        bias0 = te[q] * (2 * nj) + tj[q]
        x = jnp.concatenate(_unpack_bf16_pairs(x_buf[slot, 0:m * tm]), axis=1).astype(BF16)
        g = jnp.dot(x, wbuf[2 * par].astype(BF16), preferred_element_type=F32) + b_ref[bias0]
        up = jnp.dot(x, wbuf[2 * par + 1].astype(BF16), preferred_element_type=F32) + b_ref[bias0 + nj]
        g = jnp.minimum(g, SWIGLU_LIMIT)
        up = jnp.clip(up, -SWIGLU_LIMIT, SWIGLU_LIMIT)
        a_buf[slot, 0:m * tm] = (g * _sigmoid(SWIGLU_ALPHA * g) * (up + 1.0)).astype(BF16)

    _grouped_pipeline(tabs, total_ref, weight_copies, x_copy, a_copy, compute)


def _expert_in(tabs, xs, w_in, b_in):
    rows = xs.shape[0]
    tm, tn = EXP_TM, EXP_IN_TN
    nj = D_FF // tn
    bias = b_in.reshape(N_EXPERTS * 2 * nj, 1, tn)
    grid_spec = pltpu.PrefetchScalarGridSpec(
        num_scalar_prefetch=len(tabs),
        grid=(1,),
        in_specs=[
            pl.BlockSpec(memory_space=pl.ANY),
            pl.BlockSpec(memory_space=pl.ANY),
            pl.BlockSpec(bias.shape, lambda i, *_: (0, 0, 0), pipeline_mode=pl.Buffered(1)),
        ],
        out_specs=pl.BlockSpec(memory_space=pl.ANY),
        scratch_shapes=[pltpu.VMEM((4, D_MODEL, tn), F32),
                        pltpu.VMEM((2, EXP_CHUNK * tm, D_MODEL // 2), PACKED),
                        pltpu.VMEM((2, EXP_CHUNK * tm, tn), BF16),
                        pltpu.SemaphoreType.DMA, pltpu.SemaphoreType.DMA((2,)),
                        pltpu.SemaphoreType.DMA((2,))],
    )
    return pl.pallas_call(
        _expert_in_kernel,
        grid_spec=grid_spec,
        out_shape=jax.ShapeDtypeStruct((rows, D_FF), BF16),
        compiler_params=pltpu.CompilerParams(
            dimension_semantics=("arbitrary",), vmem_limit_bytes=VMEM_LIMIT),
        name="expert_in",
    )(*tabs, xs, w_in, bias)


def _expert_out_kernel(*refs):
    tabs, total_ref = refs[:N_TABS - 1], refs[N_TABS - 1]
    act_hbm, w_hbm, b_ref, ys_hbm, wbuf, a_buf, y_buf, w_sem, a_sem, y_sem = refs[N_TABS:]
    te, tj, trow = tabs[:3]
    tm, tnp = EXP_TM, y_buf.shape[2]
    tn = 2 * tnp
    nj = D_MODEL // tn

    def weight_copies(e, j, par):
        return [pltpu.make_async_copy(
            w_hbm.at[e, :, pl.ds(pl.multiple_of(j * tn, tn), tn)], wbuf.at[par], w_sem)]

    def rows(q, m):
        return pl.ds(pl.multiple_of(trow[q], tm), m * tm)

    def a_copy(q, slot):
        return pltpu.make_async_copy(act_hbm.at[rows(q, EXP_CHUNK)], a_buf.at[slot], a_sem.at[slot])

    def y_copy(q, slot, m):
        return pltpu.make_async_copy(
            y_buf.at[slot, pl.ds(0, m * tm)],
            ys_hbm.at[rows(q, m), pl.ds(pl.multiple_of(tj[q] * tnp, tnp), tnp)], y_sem.at[slot])

    def compute(q, slot, m, par):
        y = jnp.dot(a_buf[slot, 0:m * tm], wbuf[par].astype(BF16),
                    preferred_element_type=F32) + b_ref[te[q] * nj + tj[q]]
        y_buf[slot, 0:m * tm] = _pack_bf16_pairs(y)

    _grouped_pipeline(tabs, total_ref, weight_copies, a_copy, y_copy, compute)


def _expert_out(tabs, act, w_out, b_out):
    rows = act.shape[0]
    tm, tn = EXP_TM, EXP_OUT_TN
    nj = D_MODEL // tn
    bias = b_out.reshape(N_EXPERTS * nj, 1, tn)
    grid_spec = pltpu.PrefetchScalarGridSpec(
        num_scalar_prefetch=len(tabs),
        grid=(1,),
        in_specs=[
            pl.BlockSpec(memory_space=pl.ANY),
            pl.BlockSpec(memory_space=pl.ANY),
            pl.BlockSpec(bias.shape, lambda i, *_: (0, 0, 0), pipeline_mode=pl.Buffered(1)),
        ],
        out_specs=pl.BlockSpec(memory_space=pl.ANY),
        scratch_shapes=[pltpu.VMEM((2, D_FF, tn), F32),
                        pltpu.VMEM((2, EXP_CHUNK * tm, D_FF), BF16),
                        pltpu.VMEM((2, EXP_CHUNK * tm, tn // 2), PACKED),
                        pltpu.SemaphoreType.DMA, pltpu.SemaphoreType.DMA((2,)),
                        pltpu.SemaphoreType.DMA((2,))],
    )
    return pl.pallas_call(
        _expert_out_kernel,
        grid_spec=grid_spec,
        out_shape=jax.ShapeDtypeStruct((rows, D_MODEL // 2), PACKED),
        compiler_params=pltpu.CompilerParams(
            dimension_semantics=("arbitrary",), vmem_limit_bytes=VMEM_LIMIT),
        name="expert_out",
    )(*tabs, act, w_out, bias)


def _combine_kernel(x1_ref, yg_ref, rw_ref, g_ref, o_ref):
    half = EXP_OUT_TN // 2
    pieces = []
    ssq = jnp.zeros((x1_ref.shape[0], 1), F32)
    for jt in range(D_MODEL // EXP_OUT_TN):
        acc_lo = x1_ref[:, jt * EXP_OUT_TN:jt * EXP_OUT_TN + half]
        acc_hi = x1_ref[:, jt * EXP_OUT_TN + half:(jt + 1) * EXP_OUT_TN]
        moe_lo = jnp.zeros_like(acc_lo)
        moe_hi = jnp.zeros_like(acc_hi)
        for k in range(TOP_K):
            lo, hi = _unpack_bf16_pairs(yg_ref[k, :, jt * half:(jt + 1) * half])
            moe_lo = moe_lo + lo * rw_ref[:, k:k + 1]
            moe_hi = moe_hi + hi * rw_ref[:, k:k + 1]
        for acc in (acc_lo + moe_lo, acc_hi + moe_hi):
            ssq = ssq + jnp.sum(acc * acc, axis=-1, keepdims=True)
            pieces.append(acc)
    inv = lax.rsqrt(ssq / D_MODEL + EPS)
    for n, acc in enumerate(pieces):
        o_ref[:, n * half:(n + 1) * half] = acc * inv * g_ref[:, n * half:(n + 1) * half]


def _combine_part_kernel(x1_ref, yg_ref, rw_ref, g_ref, prev_hbm, o_ref):
    del prev_hbm
    _combine_kernel(x1_ref, yg_ref, rw_ref, g_ref, o_ref)


def _combine(x1, yg, rw, gain, row0, prev, name):
    t = x1.shape[0]
    n = yg.shape[1]
    tm = COMB_TM
    assert n % tm == 0 and row0 % tm == 0
    off = row0 // tm
    in_specs = [
        pl.BlockSpec((tm, D_MODEL), lambda i: (i + off, 0)),
        pl.BlockSpec((TOP_K, tm, D_MODEL // 2), lambda i: (0, i, 0)),
        pl.BlockSpec((tm, LANES), lambda i: (i + off, 0)),
        pl.BlockSpec((1, D_MODEL), lambda i: (0, 0)),
    ]
    args = [x1, yg, rw, gain.reshape(1, D_MODEL)]
    if prev is not None:
        in_specs.append(pl.BlockSpec(memory_space=pl.ANY))
        args.append(prev)
    return pl.pallas_call(
        _combine_kernel if prev is None else _combine_part_kernel,
        grid=(n // tm,),
        in_specs=in_specs,
        out_specs=pl.BlockSpec((tm, D_MODEL), lambda i: (i + off, 0)),
        out_shape=jax.ShapeDtypeStruct((t, D_MODEL), F32),
        input_output_aliases={} if prev is None else {len(args) - 1: 0},
        compiler_params=pltpu.CompilerParams(
            dimension_semantics=("arbitrary",), vmem_limit_bytes=VMEM_LIMIT),
        name=name,
    )(*args)


def kernel(x_prompt, x_sample, state_pool, state_conv, cache_mem_k, cache_mem_v, mem_prompt,
           norm_mix, w_in, b_gate, w_pool_group, pool_scale, w_conv, mem_norm, w_mem_kv,
           w_pool_out, w_conv_out, w_attn_out, w_o, norm_ffn, w_router, b_router,
           w_exp_in, b_exp_in, w_exp_out, b_exp_out, final_norm):
    depth = norm_mix.shape[0]
    assert depth == 1
    l = 0
    bp, seq, _ = x_prompt.shape
    bs, ln, _ = x_sample.shape
    tp, ts = bp * seq, bs * ln
    assert x_prompt.shape[2] == x_sample.shape[2] == D_MODEL and w_in.shape[1:] == (D_MODEL, D_IN_TOTAL)
    assert w_exp_in.shape[1:] == (N_EXPERTS, D_MODEL, 2 * D_FF) and w_exp_out.shape[1:] == (N_EXPERTS, D_FF, D_MODEL)
    assert cache_mem_k.shape[1:] == (bs, N_MEM, N_XHEADS, XHEAD_DIM) and mem_prompt.shape == (bp, N_MEM, D_MODEL)

    kv = _norm_matmul(mem_prompt.reshape(bp * N_MEM, D_MODEL), mem_norm[l], w_mem_kv[l],
                      jnp.zeros((2 * D_XATTN,), F32), "mem_kv")
    mk = kv[:, :D_XATTN].reshape(bp, N_MEM, D_XATTN)
    mv = kv[:, D_XATTN:].reshape(bp, N_MEM, D_XATTN)

    bias_in = jnp.concatenate([jnp.zeros((D_MIX,), F32), b_gate[l]])
    xp = x_prompt.reshape(tp, D_MODEL)
    xs_ = x_sample.reshape(ts, D_MODEL)
    proj_p, gates_p = _norm_matmul_split(xp, norm_mix[l], w_in[l], bias_in, D_MIX, "proj_prompt")
    proj_s, gates_s = _norm_matmul_split(xs_, norm_mix[l], w_in[l], bias_in, D_MIX, "proj_sample")

    br_p, zst_p = _mix_prompt(proj_p, mk, mv, w_pool_group[l], pool_scale[l], w_conv[l], bp, seq)
    br_s, zst_s = _mix_sample(proj_s, state_pool[l], state_conv[l],
                              cache_mem_k[l].reshape(bs, N_MEM * N_XHEADS, XHEAD_DIM),
                              cache_mem_v[l].reshape(bs, N_MEM * N_XHEADS, XHEAD_DIM),
                              w_pool_group[l], pool_scale[l], w_conv[l], bs, ln)

    wpo, wco, wao, wo = (w.astype(BF16) for w in (w_pool_out[l], w_conv_out[l], w_attn_out[l], w_o[l]))
    x1_p, h2_p, ri_p, rw_p, cnt_p = _merge_route(xp, br_p, gates_p, wpo, wco, wao, wo, norm_ffn[l],
                                                  w_router[l], b_router[l], "merge_route_prompt")
    x1_s, h2_s, ri_s, rw_s, cnt_s = _merge_route(xs_, br_s, gates_s, wpo, wco, wao, wo, norm_ffn[l],
                                                  w_router[l], b_router[l], "merge_route_sample")

    t_all = tp + ts
    n_assign = t_all * TOP_K
    nb_max = n_assign // EXP_TM + N_EXPERTS
    max_items = nb_max // EXP_CHUNK + N_EXPERTS
    dest, tabs_in, tabs_out = _route_tables(
        cnt_p[:, 0].astype(jnp.int32), cnt_s[:, 0].astype(jnp.int32), ri_p, ri_s,
        D_FF // EXP_IN_TN, D_MODEL // EXP_OUT_TN, max_items)
    xs_sorted = _sc_dispatch(h2_p, h2_s, dest, (nb_max + EXP_CHUNK) * EXP_TM)

    act = _expert_in(tabs_in, xs_sorted, w_exp_in[l], b_exp_in[l])
    ys = _expert_out(tabs_out, act, w_exp_out[l], b_exp_out[l])
    def combine_part(x1, rw, tok0, n, row0, prev, name):
        idx = dest[:TOP_K, tok0:tok0 + n].reshape(-1)
        yg = _sc_gather_rows(ys, idx).reshape(TOP_K, n, D_MODEL // 2)
        return _combine(x1, yg, rw, final_norm, row0, prev, name)

    y_s = combine_part(x1_s, rw_s, tp, ts, 0, None, "combine_sample")
    y_p = None
    tok0 = 0
    for c, units in enumerate(COMB_PARTS):
        n = tp * units // sum(COMB_PARTS)
        y_p = combine_part(x1_p, rw_p, tok0, n, tok0, y_p, f"combine_prompt_{c}")
        tok0 += n
    assert tok0 == tp

    new_pool_p = proj_p.reshape(bp, seq, D_MIX)[:, seq - POOL_STATE_LEN:, :D_POOL]
    new_conv_p = zst_p[:, 8 - (CONV_WIDTH - 1):]
    u_s = proj_s[:, :D_POOL].reshape(bs, ln, D_POOL)
    new_pool_s = jnp.concatenate([state_pool[l], u_s], axis=1)[:, -POOL_STATE_LEN:]
    new_conv_s = zst_s[:, ln - (CONV_WIDTH - 1):]

    return (y_p.reshape(bp, seq, D_MODEL), y_s.reshape(bs, ln, D_MODEL),
            new_pool_p[None], new_conv_p[None],
            mk.reshape(1, bp, N_MEM, N_XHEADS, XHEAD_DIM), mv.reshape(1, bp, N_MEM, N_XHEADS, XHEAD_DIM),
            new_pool_s[None], new_conv_s[None])
```
